```python
import jax, jax.numpy as jnp
from jax import lax
import numpy as np

D_MODEL = 2048
BATCH = 8
SEQ = 8192
DEPTH = 1

CHUNK = 64
D_MIX = D_MODEL
D_A = D_MIX // 2
D_B = D_MIX - D_A
GROUP_WIDTH = 128
N_GROUPS_A = D_A // GROUP_WIDTH
N_GROUPS_B = D_B // GROUP_WIDTH
CONV_A_WIDTH = 31
CONV_B_WIDTH = 3
CONV_FFN_WIDTH = 3
D_FF = 5632
PLE_DIM = 256
D_IN_PROJ = 2 * D_A + 3 * D_B
EPS = 1e-6

kernel_name = "hybrid_conformer_shortconv_block"


def rmsnorm(x, g):
    xf = x.astype(jnp.float32)
    y = xf * lax.rsqrt(jnp.mean(xf * xf, axis=-1, keepdims=True) + EPS)
    return (y * g.astype(jnp.float32)).astype(x.dtype)


def layernorm(x, g, b):
    xf = x.astype(jnp.float32)
    mu = jnp.mean(xf, axis=-1, keepdims=True)
    var = jnp.mean(jnp.square(xf - mu), axis=-1, keepdims=True)
    y = (xf - mu) * lax.rsqrt(var + EPS)
    return (y * g.astype(jnp.float32) + b.astype(jnp.float32)).astype(x.dtype)


def causal_dwconv(x, w):
    k, c = w.shape
    return lax.conv_general_dilated(
        x, w[:, None, :].astype(x.dtype),
        window_strides=(1,), padding=[(k - 1, 0)],
        dimension_numbers=("NWC", "WIO", "NWC"),
        feature_group_count=c)


def _fwd_setup_inputs(seed: int = 0) -> dict:
    key = jax.random.key(seed)
    ks = jax.random.split(key, 20)
    f32 = jnp.float32
    nrm = lambda k, shape, scale: jax.random.normal(k, shape, f32) * scale
    return {
        "x": nrm(ks[0], (BATCH, SEQ, D_MODEL), 1.0),
        "p": nrm(ks[1], (DEPTH, BATCH, SEQ, PLE_DIM), 1.0),
        "norm_mix_g": 1.0 + nrm(ks[2], (DEPTH, D_MODEL), 0.02),
        "w_in": nrm(ks[3], (DEPTH, D_MODEL, D_IN_PROJ), D_MODEL ** -0.5),
        "conv_a_w": nrm(ks[4], (DEPTH, CONV_A_WIDTH, D_A), CONV_A_WIDTH ** -0.5),
        "conv_a_b": nrm(ks[5], (DEPTH, D_A), 0.02),
        "ln_a_g": 1.0 + nrm(ks[6], (DEPTH, D_A), 0.02),
        "ln_a_b": nrm(ks[7], (DEPTH, D_A), 0.02),
        "conv_b_w": nrm(ks[8], (DEPTH, CONV_B_WIDTH, D_B), CONV_B_WIDTH ** -0.5),
        "w_out": nrm(ks[9], (DEPTH, D_MIX, D_MODEL), D_MIX ** -0.5),
        "norm_ffn_g": 1.0 + nrm(ks[10], (DEPTH, D_MODEL), 0.02),
        "w_up": nrm(ks[11], (DEPTH, D_MODEL, 2 * D_FF), D_MODEL ** -0.5),
        "conv_ffn_w": nrm(ks[12], (DEPTH, CONV_FFN_WIDTH, 2 * D_FF), CONV_FFN_WIDTH ** -0.5),
        "w_down": nrm(ks[13], (DEPTH, D_FF, D_MODEL), D_FF ** -0.5),
        "w_ple_gate": nrm(ks[14], (DEPTH, D_MODEL, D_MODEL), D_MODEL ** -0.5),
        "b_ple_gate": nrm(ks[15], (DEPTH, D_MODEL), 0.02),
        "w_ple_proj": nrm(ks[16], (DEPTH, PLE_DIM, D_MODEL), PLE_DIM ** -0.5),
        "norm_final_g": 1.0 + nrm(ks[17], (D_MODEL,), 0.02),
    }


def _fwd_reference(x, p, norm_mix_g, w_in, conv_a_w, conv_a_b, ln_a_g, ln_a_b, conv_b_w,
              w_out, norm_ffn_g, w_up, conv_ffn_w, w_down, w_ple_gate, b_ple_gate,
              w_ple_proj, norm_final_g):
    h = x
    split_pts = [D_A, 2 * D_A, 2 * D_A + D_B, 2 * D_A + 2 * D_B]
    for i in range(DEPTH):
        hn = rmsnorm(h, norm_mix_g[i])
        z = jnp.einsum("bsd,de->bse", hn, w_in[i])
        a_val, a_gate, b_gate, c_gate, b_h = jnp.split(z, split_pts, axis=-1)
        a = a_val * jax.nn.sigmoid(a_gate)
        a = causal_dwconv(a, conv_a_w[i]) + conv_a_b[i]
        a = jax.nn.silu(layernorm(a, ln_a_g[i], ln_a_b[i]))
        bx = b_gate * causal_dwconv(c_gate * b_h, conv_b_w[i])
        mix = jnp.einsum("bse,ed->bsd", jnp.concatenate([a, bx], axis=-1), w_out[i])
        h = h + mix
        hn = rmsnorm(h, norm_ffn_g[i])
        u = causal_dwconv(jnp.einsum("bsd,df->bsf", hn, w_up[i]), conv_ffn_w[i])
        g, up = jnp.split(u, 2, axis=-1)
        h = h + jnp.einsum("bsf,fd->bsd", jax.nn.silu(g) * up, w_down[i])
        gate = jax.nn.sigmoid(jnp.einsum("bsd,de->bse", h, w_ple_gate[i]) + b_ple_gate[i])
        h = h + jnp.einsum("bsk,kd->bsd", p[i], w_ple_proj[i]) * gate
    return rmsnorm(h, norm_final_g)


import jax as _jax
import jax.numpy as _jnp

TWIN_FORMAT = 'train_step'
FWD_PARAMS = ['x', 'p', 'norm_mix_g', 'w_in', 'conv_a_w', 'conv_a_b', 'ln_a_g', 'ln_a_b', 'conv_b_w', 'w_out', 'norm_ffn_g', 'w_up', 'conv_ffn_w', 'w_down', 'w_ple_gate', 'b_ple_gate', 'w_ple_proj', 'norm_final_g']
TWIN_WEIGHTS = ['norm_mix_g', 'w_in', 'conv_a_w', 'conv_a_b', 'ln_a_g', 'ln_a_b', 'conv_b_w', 'w_out', 'norm_ffn_g', 'w_up', 'conv_ffn_w', 'w_down', 'w_ple_gate', 'b_ple_gate', 'w_ple_proj', 'norm_final_g']
TWIN_DIFF_INPUT = 'x'
TWIN_INPUTS = ['x', 'p', 'norm_mix_g', 'w_in', 'conv_a_w', 'conv_a_b', 'ln_a_g', 'ln_a_b', 'conv_b_w', 'w_out', 'norm_ffn_g', 'w_up', 'conv_ffn_w', 'w_down', 'w_ple_gate', 'b_ple_gate', 'w_ple_proj', 'norm_final_g', 'loss_target', 'm_norm_mix_g', 'm_w_in', 'm_conv_a_w', 'm_conv_a_b', 'm_ln_a_g', 'm_ln_a_b', 'm_conv_b_w', 'm_w_out', 'm_norm_ffn_g', 'm_w_up', 'm_conv_ffn_w', 'm_w_down', 'm_w_ple_gate', 'm_b_ple_gate', 'm_w_ple_proj', 'm_norm_final_g', 'v_norm_mix_g', 'v_w_in', 'v_conv_a_w', 'v_conv_a_b', 'v_ln_a_g', 'v_ln_a_b', 'v_conv_b_w', 'v_w_out', 'v_norm_ffn_g', 'v_w_up', 'v_conv_ffn_w', 'v_w_down', 'v_w_ple_gate', 'v_b_ple_gate', 'v_w_ple_proj', 'v_norm_final_g']
TWIN_OUTPUTS = ['loss', 'grad_x', 'grad_norm_mix_g', 'grad_w_in', 'grad_conv_a_w', 'grad_conv_a_b', 'grad_ln_a_g', 'grad_ln_a_b', 'grad_conv_b_w', 'grad_w_out', 'grad_norm_ffn_g', 'grad_w_up', 'grad_conv_ffn_w', 'grad_w_down', 'grad_w_ple_gate', 'grad_b_ple_gate', 'grad_w_ple_proj', 'grad_norm_final_g', 'delta_norm_mix_g', 'delta_w_in', 'delta_conv_a_w', 'delta_conv_a_b', 'delta_ln_a_g', 'delta_ln_a_b', 'delta_conv_b_w', 'delta_w_out', 'delta_norm_ffn_g', 'delta_w_up', 'delta_conv_ffn_w', 'delta_w_down', 'delta_w_ple_gate', 'delta_b_ple_gate', 'delta_w_ple_proj', 'delta_norm_final_g', 'new_m_norm_mix_g', 'new_m_w_in', 'new_m_conv_a_w', 'new_m_conv_a_b', 'new_m_ln_a_g', 'new_m_ln_a_b', 'new_m_conv_b_w', 'new_m_w_out', 'new_m_norm_ffn_g', 'new_m_w_up', 'new_m_conv_ffn_w', 'new_m_w_down', 'new_m_w_ple_gate', 'new_m_b_ple_gate', 'new_m_w_ple_proj', 'new_m_norm_final_g', 'new_v_norm_mix_g', 'new_v_w_in', 'new_v_conv_a_w', 'new_v_conv_a_b', 'new_v_ln_a_g', 'new_v_ln_a_b', 'new_v_conv_b_w', 'new_v_w_out', 'new_v_norm_ffn_g', 'new_v_w_up', 'new_v_conv_ffn_w', 'new_v_w_down', 'new_v_w_ple_gate', 'new_v_b_ple_gate', 'new_v_w_ple_proj', 'new_v_norm_final_g']
TWIN_LEAF_KINDS = {'loss': 'loss', 'grad_x': 'grad_x', 'grad_norm_mix_g': 'grad_w', 'grad_w_in': 'grad_w', 'grad_conv_a_w': 'grad_w', 'grad_conv_a_b': 'grad_w', 'grad_ln_a_g': 'grad_w', 'grad_ln_a_b': 'grad_w', 'grad_conv_b_w': 'grad_w', 'grad_w_out': 'grad_w', 'grad_norm_ffn_g': 'grad_w', 'grad_w_up': 'grad_w', 'grad_conv_ffn_w': 'grad_w', 'grad_w_down': 'grad_w', 'grad_w_ple_gate': 'grad_w', 'grad_b_ple_gate': 'grad_w', 'grad_w_ple_proj': 'grad_w', 'grad_norm_final_g': 'grad_w', 'delta_norm_mix_g': 'delta_w', 'delta_w_in': 'delta_w', 'delta_conv_a_w': 'delta_w', 'delta_conv_a_b': 'delta_w', 'delta_ln_a_g': 'delta_w', 'delta_ln_a_b': 'delta_w', 'delta_conv_b_w': 'delta_w', 'delta_w_out': 'delta_w', 'delta_norm_ffn_g': 'delta_w', 'delta_w_up': 'delta_w', 'delta_conv_ffn_w': 'delta_w', 'delta_w_down': 'delta_w', 'delta_w_ple_gate': 'delta_w', 'delta_b_ple_gate': 'delta_w', 'delta_w_ple_proj': 'delta_w', 'delta_norm_final_g': 'delta_w', 'new_m_norm_mix_g': 'new_m', 'new_m_w_in': 'new_m', 'new_m_conv_a_w': 'new_m', 'new_m_conv_a_b': 'new_m', 'new_m_ln_a_g': 'new_m', 'new_m_ln_a_b': 'new_m', 'new_m_conv_b_w': 'new_m', 'new_m_w_out': 'new_m', 'new_m_norm_ffn_g': 'new_m', 'new_m_w_up': 'new_m', 'new_m_conv_ffn_w': 'new_m', 'new_m_w_down': 'new_m', 'new_m_w_ple_gate': 'new_m', 'new_m_b_ple_gate': 'new_m', 'new_m_w_ple_proj': 'new_m', 'new_m_norm_final_g': 'new_m', 'new_v_norm_mix_g': 'new_v', 'new_v_w_in': 'new_v', 'new_v_conv_a_w': 'new_v', 'new_v_conv_a_b': 'new_v', 'new_v_ln_a_g': 'new_v', 'new_v_ln_a_b': 'new_v', 'new_v_conv_b_w': 'new_v', 'new_v_w_out': 'new_v', 'new_v_norm_ffn_g': 'new_v', 'new_v_w_up': 'new_v', 'new_v_conv_ffn_w': 'new_v', 'new_v_w_down': 'new_v', 'new_v_w_ple_gate': 'new_v', 'new_v_b_ple_gate': 'new_v', 'new_v_w_ple_proj': 'new_v', 'new_v_norm_final_g': 'new_v'}


def _forward(args):
    return _fwd_reference(*[args[k] for k in FWD_PARAMS])


def _output_shape():
    def fwd():
        inp = _fwd_setup_inputs(0)
        return _fwd_reference(*[inp[k] for k in FWD_PARAMS])
    out = _jax.eval_shape(fwd)
    return out.shape, out.dtype

N_MICROBATCH = 1
ADAM_LR = 0.001
ADAM_B1 = 0.9
ADAM_B2 = 0.999
ADAM_EPS = 1e-08
ADAM_WD = 0.01
ADAM_STEP = 10
PER_EXAMPLE_BATCH_AXIS = {'x': 0, 'p': 1, 'loss_target': 0}
SHARED_INPUTS = []
_WEIGHT_DTYPES = {'norm_mix_g': _jnp.float32, 'w_in': _jnp.float32, 'conv_a_w': _jnp.float32, 'conv_a_b': _jnp.float32, 'ln_a_g': _jnp.float32, 'ln_a_b': _jnp.float32, 'conv_b_w': _jnp.float32, 'w_out': _jnp.float32, 'norm_ffn_g': _jnp.float32, 'w_up': _jnp.float32, 'conv_ffn_w': _jnp.float32, 'w_down': _jnp.float32, 'w_ple_gate': _jnp.float32, 'b_ple_gate': _jnp.float32, 'w_ple_proj': _jnp.float32, 'norm_final_g': _jnp.float32}
MOMENT_SCALE = {'norm_mix_g': 1.314212e-01, 'w_in': 8.272787e-02, 'conv_a_w': 6.248222e-02, 'conv_a_b': 1.335565e-01, 'ln_a_g': 7.842886e-02, 'ln_a_b': 7.427399e-02, 'conv_b_w': 1.008219e-01, 'w_out': 8.298613e-02, 'norm_ffn_g': 7.287578e-02, 'w_up': 3.151484e-02, 'conv_ffn_w': 3.139213e-02, 'w_down': 5.136329e-02, 'w_ple_gate': 2.267105e-02, 'b_ple_gate': 3.402242e-02, 'w_ple_proj': 4.638471e-02, 'norm_final_g': 3.199724e+01}


def _to_microbatches(a, axis):
    t = _jnp.moveaxis(a, axis, 0)
    t = t.reshape((N_MICROBATCH, t.shape[0] // N_MICROBATCH) + t.shape[1:])
    return _jnp.moveaxis(t, 1, axis + 1)


def setup_inputs(seed: int = 0) -> dict:
    inp = _fwd_setup_inputs(seed)
    key = _jax.random.fold_in(_jax.random.key(seed), 7919)
    shape, _ = _output_shape()
    out = dict(inp)
    out["loss_target"] = _jax.random.normal(_jax.random.fold_in(key, 0), shape, _jnp.float32)
    for i, name in enumerate(TWIN_WEIGHTS):
        w = inp[name].astype(_jnp.float32)
        if MOMENT_SCALE is None:
            s = _jnp.sqrt(_jnp.mean(_jnp.square(w)) + 1e-30)
        else:
            s = MOMENT_SCALE[name]
        km, kv = _jax.random.split(_jax.random.fold_in(key, i + 1))
        out[name] = w
        out["m_" + name] = s * _jax.random.normal(km, w.shape, _jnp.float32)
        out["v_" + name] = (s * s) * _jax.random.uniform(kv, w.shape, _jnp.float32, 0.5, 1.5)
    if N_MICROBATCH > 1:
        for name, axis in PER_EXAMPLE_BATCH_AXIS.items():
            out[name] = _to_microbatches(out[name], axis)
    return {'x': out['x'], 'p': out['p'], 'norm_mix_g': out['norm_mix_g'], 'w_in': out['w_in'], 'conv_a_w': out['conv_a_w'], 'conv_a_b': out['conv_a_b'], 'ln_a_g': out['ln_a_g'], 'ln_a_b': out['ln_a_b'], 'conv_b_w': out['conv_b_w'], 'w_out': out['w_out'], 'norm_ffn_g': out['norm_ffn_g'], 'w_up': out['w_up'], 'conv_ffn_w': out['conv_ffn_w'], 'w_down': out['w_down'], 'w_ple_gate': out['w_ple_gate'], 'b_ple_gate': out['b_ple_gate'], 'w_ple_proj': out['w_ple_proj'], 'norm_final_g': out['norm_final_g'], 'loss_target': out['loss_target'], 'm_norm_mix_g': out['m_norm_mix_g'], 'm_w_in': out['m_w_in'], 'm_conv_a_w': out['m_conv_a_w'], 'm_conv_a_b': out['m_conv_a_b'], 'm_ln_a_g': out['m_ln_a_g'], 'm_ln_a_b': out['m_ln_a_b'], 'm_conv_b_w': out['m_conv_b_w'], 'm_w_out': out['m_w_out'], 'm_norm_ffn_g': out['m_norm_ffn_g'], 'm_w_up': out['m_w_up'], 'm_conv_ffn_w': out['m_conv_ffn_w'], 'm_w_down': out['m_w_down'], 'm_w_ple_gate': out['m_w_ple_gate'], 'm_b_ple_gate': out['m_b_ple_gate'], 'm_w_ple_proj': out['m_w_ple_proj'], 'm_norm_final_g': out['m_norm_final_g'], 'v_norm_mix_g': out['v_norm_mix_g'], 'v_w_in': out['v_w_in'], 'v_conv_a_w': out['v_conv_a_w'], 'v_conv_a_b': out['v_conv_a_b'], 'v_ln_a_g': out['v_ln_a_g'], 'v_ln_a_b': out['v_ln_a_b'], 'v_conv_b_w': out['v_conv_b_w'], 'v_w_out': out['v_w_out'], 'v_norm_ffn_g': out['v_norm_ffn_g'], 'v_w_up': out['v_w_up'], 'v_conv_ffn_w': out['v_conv_ffn_w'], 'v_w_down': out['v_w_down'], 'v_w_ple_gate': out['v_w_ple_gate'], 'v_b_ple_gate': out['v_b_ple_gate'], 'v_w_ple_proj': out['v_w_ple_proj'], 'v_norm_final_g': out['v_norm_final_g']}


def _loss(weights, diff, rest, loss_target):
    with _jax.named_scope("forward"):
        args = {**rest, TWIN_DIFF_INPUT: diff, **{k: w.astype(_WEIGHT_DTYPES[k]) for k, w in weights.items()}}
        y = _forward(args)
    with _jax.named_scope("loss_head"):
        err = _jnp.square(y.astype(_jnp.float32) - loss_target)
        return 0.5 * _jnp.sum(_jnp.mean(err, axis=-1)) if err.ndim else 0.5 * err


def _adamw(w, g, m, v):
    m = ADAM_B1 * m + (1.0 - ADAM_B1) * g
    v = ADAM_B2 * v + (1.0 - ADAM_B2) * _jnp.square(g)
    m_hat = m / (1.0 - ADAM_B1 ** ADAM_STEP)
    v_hat = v / (1.0 - ADAM_B2 ** ADAM_STEP)
    delta = -ADAM_LR * (m_hat / (_jnp.sqrt(v_hat) + ADAM_EPS) + ADAM_WD * w)
    return delta, m, v


def reference(x, p, norm_mix_g, w_in, conv_a_w, conv_a_b, ln_a_g, ln_a_b, conv_b_w, w_out, norm_ffn_g, w_up, conv_ffn_w, w_down, w_ple_gate, b_ple_gate, w_ple_proj, norm_final_g, loss_target, m_norm_mix_g, m_w_in, m_conv_a_w, m_conv_a_b, m_ln_a_g, m_ln_a_b, m_conv_b_w, m_w_out, m_norm_ffn_g, m_w_up, m_conv_ffn_w, m_w_down, m_w_ple_gate, m_b_ple_gate, m_w_ple_proj, m_norm_final_g, v_norm_mix_g, v_w_in, v_conv_a_w, v_conv_a_b, v_ln_a_g, v_ln_a_b, v_conv_b_w, v_w_out, v_norm_ffn_g, v_w_up, v_conv_ffn_w, v_w_down, v_w_ple_gate, v_b_ple_gate, v_w_ple_proj, v_norm_final_g):
    given = dict(x=x, p=p, norm_mix_g=norm_mix_g, w_in=w_in, conv_a_w=conv_a_w, conv_a_b=conv_a_b, ln_a_g=ln_a_g, ln_a_b=ln_a_b, conv_b_w=conv_b_w, w_out=w_out, norm_ffn_g=norm_ffn_g, w_up=w_up, conv_ffn_w=conv_ffn_w, w_down=w_down, w_ple_gate=w_ple_gate, b_ple_gate=b_ple_gate, w_ple_proj=w_ple_proj, norm_final_g=norm_final_g, loss_target=loss_target, m_norm_mix_g=m_norm_mix_g, m_w_in=m_w_in, m_conv_a_w=m_conv_a_w, m_conv_a_b=m_conv_a_b, m_ln_a_g=m_ln_a_g, m_ln_a_b=m_ln_a_b, m_conv_b_w=m_conv_b_w, m_w_out=m_w_out, m_norm_ffn_g=m_norm_ffn_g, m_w_up=m_w_up, m_conv_ffn_w=m_conv_ffn_w, m_w_down=m_w_down, m_w_ple_gate=m_w_ple_gate, m_b_ple_gate=m_b_ple_gate, m_w_ple_proj=m_w_ple_proj, m_norm_final_g=m_norm_final_g, v_norm_mix_g=v_norm_mix_g, v_w_in=v_w_in, v_conv_a_w=v_conv_a_w, v_conv_a_b=v_conv_a_b, v_ln_a_g=v_ln_a_g, v_ln_a_b=v_ln_a_b, v_conv_b_w=v_conv_b_w, v_w_out=v_w_out, v_norm_ffn_g=v_norm_ffn_g, v_w_up=v_w_up, v_conv_ffn_w=v_conv_ffn_w, v_w_down=v_w_down, v_w_ple_gate=v_w_ple_gate, v_b_ple_gate=v_b_ple_gate, v_w_ple_proj=v_w_ple_proj, v_norm_final_g=v_norm_final_g)
    weights = {n: given[n] for n in TWIN_WEIGHTS}
    shared = {n: given[n] for n in SHARED_INPUTS}
    per_example = {n: given[n] for n in ['x', 'p']}
    grad_fn = _jax.value_and_grad(_loss, argnums=(0, 1))

    def one_microbatch(ex, loss_target):
        ex = dict(ex)
        diff = ex.pop(TWIN_DIFF_INPUT)
        return grad_fn(weights, diff, {**shared, **ex}, loss_target)

    if N_MICROBATCH == 1:
        loss, (grad_w, grad_x) = one_microbatch(per_example, given["loss_target"])
    else:
        def body(carry, xs):
            loss_sum, grad_sum = carry
            l_k, (gw_k, gx_k) = one_microbatch(xs[0], xs[1])
            with _jax.named_scope("update"):
                return (loss_sum + l_k, _jax.tree.map(_jnp.add, grad_sum, gw_k)), gx_k

        init = (_jnp.zeros((), _jnp.float32), _jax.tree.map(_jnp.zeros_like, weights))
        (loss, grad_w), grad_x = _jax.lax.scan(body, init, (per_example, given["loss_target"]))
    with _jax.named_scope("update"):
        delta_w, new_m, new_v = {}, {}, {}
        for n in TWIN_WEIGHTS:
            delta_w[n], new_m[n], new_v[n] = _adamw(weights[n], grad_w[n], given["m_" + n], given["v_" + n])
    return (loss, grad_x, *[grad_w[n] for n in TWIN_WEIGHTS], *[delta_w[n] for n in TWIN_WEIGHTS],
            *[new_m[n] for n in TWIN_WEIGHTS], *[new_v[n] for n in TWIN_WEIGHTS])
```

```python
import functools

import jax
import jax.numpy as jnp
from jax import lax
from jax.experimental import pallas as pl
from jax.experimental.pallas import tpu as pltpu

F32 = jnp.float32
BF16 = jnp.bfloat16
MESH = pl.DeviceIdType.MESH
ANY = pl.BlockSpec(memory_space=pl.ANY)

EPS = 1e-6
ADAM_LR = 0.001
ADAM_B1 = 0.9
ADAM_B2 = 0.999
ADAM_EPS = 1e-08
ADAM_WD = 0.01
ADAM_STEP = 10

N_CHIPS = 4
N_DEV = 8
LANES = 128
SUBLANES = 8
PACK_ALIGN = LANES * SUBLANES
VMEM_CAP = 60 * 1024 * 1024
VMEM_SLACK = 6 * 1024 * 1024


def _pick(n, cands):
    for c in cands:
        if n % c == 0:
            return c
    raise ValueError(f"no tile of {cands} divides {n}")


def _nbytes(shape, dtype):
    n = 1
    for s in shape:
        if s is not None:
            n *= s
    return n * jnp.dtype(dtype).itemsize


def _params(sem, blocks, scratch=(), temps=()):
    est = (2 * sum(_nbytes(s, d) for s, d in blocks) + sum(_nbytes(s, d) for s, d in scratch)
           + sum(_nbytes(s, d) for s, d in temps))
    return pltpu.CompilerParams(dimension_semantics=sem,
                                vmem_limit_bytes=min(est + VMEM_SLACK, VMEM_CAP))


def _sigmoid(x):
    return 1.0 / (1.0 + jnp.exp(-x))


def _rsum(x):
    return jnp.sum(x, axis=0, keepdims=True)


def _rms_stats(x):
    return lax.rsqrt(jnp.mean(x * x, axis=-1, keepdims=True) + EPS)


def _rms_bwd(h, g, dout):
    r = _rms_stats(h)
    n = h * r
    dn = dout * g
    dh = r * (dn - n * jnp.mean(dn * n, axis=-1, keepdims=True))
    return dh, _rsum(dout * n)


def _chip_major(nb):
    return lambda i, j: (j // nb, 0, j % nb)


def _rows_mm(name, S, TM, N, TN, *, row_ins, vec_ins=(), colvec_ins=(), weights, tile_ins=(),
             tile_outs, row_outs=(), acc_outs=(), prologue, epilogue, prologue_temps=0):
    nI, nJ = S // TM, N // TN
    n_row, n_vec, n_cv, n_w, n_tile = len(row_ins), len(vec_ins), len(colvec_ins), len(weights), len(tile_ins)
    n_to, n_ro, n_ao = len(tile_outs), len(row_outs), len(acc_outs)

    in_specs, blocks, scratch, ks = [], [], [], []
    for a in row_ins:
        in_specs.append(pl.BlockSpec((TM, a.shape[1]), lambda i, j: (i, 0)))
        blocks.append(((TM, a.shape[1]), a.dtype))
    for a in vec_ins:
        in_specs.append(pl.BlockSpec(a.shape, lambda i, j: (0, 0)))
        blocks.append((a.shape, a.dtype))
    for a in colvec_ins:
        in_specs.append(pl.BlockSpec((1, TN), lambda i, j: (0, j)))
        blocks.append(((1, TN), a.dtype))
    for w, mode in weights:
        if mode == "nn2":
            k = w.shape[0]
            in_specs.append(pl.BlockSpec((k, TN), lambda i, j: (0, j)))
        elif mode == "nn3":
            k = w.shape[1]
            in_specs.append(pl.BlockSpec((None, k, TN), _chip_major(w.shape[2] // TN)))
        else:
            k = w.shape[1]
            in_specs.append(pl.BlockSpec((TN, k), lambda i, j: (j, 0)))
        ks.append(k)
        blocks.append(((k, TN), BF16))
        scratch.append(((TM, k), BF16))
    for a in tile_ins:
        in_specs.append(pl.BlockSpec((TM, TN), lambda i, j: (i, j)))
        blocks.append(((TM, TN), a.dtype))

    out_shape, out_specs = [], []
    for dt in tile_outs:
        out_shape.append(jax.ShapeDtypeStruct((S, N), dt))
        out_specs.append(pl.BlockSpec((TM, TN), lambda i, j: (i, j)))
        blocks.append(((TM, TN), dt))
    for width, dt in row_outs:
        out_shape.append(jax.ShapeDtypeStruct((S, width), dt))
        out_specs.append(pl.BlockSpec((TM, width), lambda i, j: (i, 0)))
        blocks.append(((TM, width), dt))
    for rows, width in acc_outs:
        out_shape.append(jax.ShapeDtypeStruct((rows, width), F32))
        out_specs.append(pl.BlockSpec((rows, width), lambda i, j: (0, 0)))
        blocks.append(((rows, width), F32))

    modes = [m for _, m in weights]

    def body(*refs):
        pos = 0
        def take(n):
            nonlocal pos
            out = refs[pos:pos + n]
            pos += n
            return out
        row_r, vec_r, cv_r, w_r, tile_r = take(n_row), take(n_vec), take(n_cv), take(n_w), take(n_tile)
        to_r, ro_r, ao_r, a_sc = take(n_to), take(n_ro), take(n_ao), take(n_w)
        i, j = pl.program_id(0), pl.program_id(1)

        @pl.when(j == 0)
        def _():
            if n_ao:
                @pl.when(i == 0)
                def _():
                    for r in ao_r:
                        r[...] = jnp.zeros_like(r)
            a_vals = prologue(row_r, vec_r, ro_r, ao_r)
            for sc, a in zip(a_sc, a_vals):
                sc[...] = a

        accs = []
        for w_ref, sc, mode in zip(w_r, a_sc, modes):
            if mode == "nt2":
                accs.append(lax.dot_general(sc[...], w_ref[...], (((1,), (1,)), ((), ())),
                                            preferred_element_type=F32))
            else:
                accs.append(jnp.dot(sc[...], w_ref[...], preferred_element_type=F32))
        outs = epilogue(accs, tile_r, cv_r)
        for r, o in zip(to_r, outs):
            r[...] = o.astype(r.dtype)

    return pl.pallas_call(
        body, name=name, grid=(nI, nJ), in_specs=in_specs, out_specs=out_specs, out_shape=out_shape,
        scratch_shapes=[pltpu.VMEM(s, d) for s, d in scratch],
        compiler_params=_params(("arbitrary", "arbitrary"), blocks, scratch,
                                temps=[((TM, max(ks)), F32)] * prologue_temps + [((TM, TN), F32)] * 3),
    )(*row_ins, *vec_ins, *colvec_ins, *[w for w, _ in weights], *tile_ins)


def _kloop_mm(name, S, TM, a, w3, TK, *, row_ins, vec_ins, row_outs, acc_outs, epilogue):
    _, N, Ks = w3.shape
    nb = Ks // TK
    nK = N_CHIPS * nb
    n_row, n_vec, n_ro, n_ao = len(row_ins), len(vec_ins), len(row_outs), len(acc_outs)

    in_specs = [pl.BlockSpec((TM, TK), lambda i, k: (i, k)),
                pl.BlockSpec((None, N, TK), lambda i, k: (k // nb, 0, k % nb))]
    blocks = [((TM, TK), BF16), ((N, TK), BF16)]
    for r in row_ins:
        in_specs.append(pl.BlockSpec((TM, r.shape[1]), lambda i, k: (i, 0)))
        blocks.append(((TM, r.shape[1]), r.dtype))
    for v in vec_ins:
        in_specs.append(pl.BlockSpec(v.shape, lambda i, k: (0, 0)))
        blocks.append((v.shape, v.dtype))
    out_shape, out_specs = [], []
    for width, dt in row_outs:
        out_shape.append(jax.ShapeDtypeStruct((S, width), dt))
        out_specs.append(pl.BlockSpec((TM, width), lambda i, k: (i, 0)))
        blocks.append(((TM, width), dt))
    for rows, width in acc_outs:
        out_shape.append(jax.ShapeDtypeStruct((rows, width), F32))
        out_specs.append(pl.BlockSpec((rows, width), lambda i, k: (0, 0)))
        blocks.append(((rows, width), F32))
    scratch = [((TM, N), F32)]

    def body(*refs):
        a_ref, w_ref = refs[0], refs[1]
        row_r = refs[2:2 + n_row]
        vec_r = refs[2 + n_row:2 + n_row + n_vec]
        pos = 2 + n_row + n_vec
        ro_r = refs[pos:pos + n_ro]
        ao_r = refs[pos + n_ro:pos + n_ro + n_ao]
        acc_sc = refs[pos + n_ro + n_ao]
        i, k = pl.program_id(0), pl.program_id(1)
        d = lax.dot_general(a_ref[...], w_ref[...], (((1,), (1,)), ((), ())), preferred_element_type=F32)

        @pl.when(k == 0)
        def _():
            acc_sc[...] = d

        @pl.when(k > 0)
        def _():
            acc_sc[...] += d

        @pl.when(k == nK - 1)
        def _():
            @pl.when(i == 0)
            def _():
                for r in ao_r:
                    r[...] = jnp.zeros_like(r)
            epilogue(acc_sc[...], row_r, vec_r, ro_r, ao_r)

    return pl.pallas_call(
        body, name=name, grid=(S // TM, nK), in_specs=in_specs, out_specs=out_specs, out_shape=out_shape,
        scratch_shapes=[pltpu.VMEM(s, d) for s, d in scratch],
        compiler_params=_params(("arbitrary", "arbitrary"), blocks, scratch, temps=[((TM, N), F32)] * 8),
    )(a, w3, *row_ins, *vec_ins)


def _tn_mm(name, a, b, TMw, TNw, TK, cols_per_chip=None):
    S, M = a.shape
    N = b.shape[1]
    nK = S // TK
    if cols_per_chip is None:
        out_shape = jax.ShapeDtypeStruct((M, N), F32)
        out_spec = pl.BlockSpec((TMw, TNw), lambda i, j, k: (i, j))
    else:
        nb = cols_per_chip // TNw
        out_shape = jax.ShapeDtypeStruct((N_CHIPS, M, cols_per_chip), F32)
        out_spec = pl.BlockSpec((None, TMw, TNw), lambda i, j, k: (j // nb, i, j % nb))

    def body(a_ref, b_ref, o_ref):
        k = pl.program_id(2)
        d = lax.dot_general(a_ref[...], b_ref[...], (((0,), (0,)), ((), ())), preferred_element_type=F32)

        @pl.when(k == 0)
        def _():
            o_ref[...] = d

        @pl.when(k > 0)
        def _():
            o_ref[...] += d

    blocks = [((TK, TMw), BF16), ((TK, TNw), BF16), ((TMw, TNw), F32)]
    return pl.pallas_call(
        body, name=name, grid=(M // TMw, N // TNw, nK),
        in_specs=[pl.BlockSpec((TK, TMw), lambda i, j, k: (k, i)),
                  pl.BlockSpec((TK, TNw), lambda i, j, k: (k, j))],
        out_specs=out_spec, out_shape=out_shape,
        compiler_params=_params(("arbitrary", "arbitrary", "arbitrary"), blocks),
    )(a, b)


def _prev_rows(TM, H, col):
    return lambda i: (jnp.maximum(i * (TM // H) - 1, 0), col)


def _next_rows(S, TM, H, col):
    return lambda i: (jnp.minimum((i + 1) * (TM // H), S // H - 1), col)


def _taps_causal(ext_ref, w_ref, K, H, TM, cs):
    acc = None
    for k in range(K):
        term = ext_ref[pl.ds(H - (K - 1) + k, TM), cs] * w_ref[pl.ds(k, 1), cs]
        acc = term if acc is None else acc + term
    return acc


def _taps_anticausal(ext_ref, w_ref, K, TM, cs):
    acc = None
    for k in range(K):
        term = ext_ref[pl.ds(K - 1 - k, TM), cs] * w_ref[pl.ds(k, 1), cs]
        acc = term if acc is None else acc + term
    return acc


def _tap_grads(ext_ref, g, K, H, TM, cs):
    return [_rsum(ext_ref[pl.ds(H - (K - 1) + k, TM), cs] * g) for k in range(K)]


def _mixer_fwd(z, conv_a_w, conv_a_b, ln_g, ln_b, conv_b_w, S, TM, A):
    H = 32
    KA, KB = conv_a_w.shape[0], conv_b_w.shape[0]
    n_chunks = A // LANES
    RB = _pick(TM, (64, 32))

    def body(zc_ref, zh_ref, wa_ref, ba_ref, g_ref, b_ref, wb_ref, a1_ref, cat_ref, ext_a, ext_b):
        i = pl.program_id(0)
        live = (i > 0).astype(F32)
        zc = zc_ref[...].astype(F32)
        zh = zh_ref[...].astype(F32) * live
        ext_a[pl.ds(0, H), :] = zh[:, 0:A] * _sigmoid(zh[:, A:2 * A])
        ext_a[pl.ds(H, TM), :] = zc[:, 0:A] * _sigmoid(zc[:, A:2 * A])
        ext_b[pl.ds(0, H), :] = zh[:, 3 * A:4 * A] * zh[:, 4 * A:5 * A]
        ext_b[pl.ds(H, TM), :] = zc[:, 3 * A:4 * A] * zc[:, 4 * A:5 * A]

        def chunk(c, carry):
            cs = pl.ds(pl.multiple_of(c * LANES, LANES), LANES)
            for r0 in range(0, TM, RB):
                acc = None
                for k in range(KA):
                    term = ext_a[pl.ds(H - (KA - 1) + k + r0, RB), cs] * wa_ref[pl.ds(k, 1), cs]
                    acc = term if acc is None else acc + term
                a1_ref[pl.ds(r0, RB), cs] = acc + ba_ref[:, cs]
            return carry
        lax.fori_loop(0, n_chunks, chunk, 0)

        a1 = a1_ref[...]
        mu = jnp.mean(a1, axis=-1, keepdims=True)
        d = a1 - mu
        var = jnp.mean(d * d, axis=-1, keepdims=True)
        a2 = d * lax.rsqrt(var + EPS) * g_ref[...] + b_ref[...]
        cat_ref[:, 0:A] = (a2 * _sigmoid(a2)).astype(BF16)
        cbc = _taps_causal(ext_b, wb_ref, KB, H, TM, slice(None))
        cat_ref[:, A:2 * A] = (zc[:, 2 * A:3 * A] * cbc).astype(BF16)

    blocks = [((TM, 5 * A), BF16), ((H, 5 * A), BF16), ((KA, A), F32), ((KB, A), F32),
              ((TM, A), F32), ((TM, 2 * A), BF16)]
    scratch = [((H + TM, A), F32), ((H + TM, A), F32)]
    vec = lambda r: pl.BlockSpec((r, A), lambda i: (0, 0))
    return pl.pallas_call(
        body, name="mixer_fwd", grid=(S // TM,),
        in_specs=[pl.BlockSpec((TM, 5 * A), lambda i: (i, 0)),
                  pl.BlockSpec((H, 5 * A), _prev_rows(TM, H, 0)),
                  vec(KA), vec(1), vec(1), vec(1), vec(KB)],
        out_specs=[pl.BlockSpec((TM, A), lambda i: (i, 0)), pl.BlockSpec((TM, 2 * A), lambda i: (i, 0))],
        out_shape=[jax.ShapeDtypeStruct((S, A), F32), jax.ShapeDtypeStruct((S, 2 * A), BF16)],
        scratch_shapes=[pltpu.VMEM(s, d) for s, d in scratch],
        compiler_params=_params(("arbitrary",), blocks, scratch,
                                temps=[((TM, 5 * A), F32)] * 2 + [((TM, A), F32)] * 10),
    )(z, z, conv_a_w, conv_a_b, ln_g, ln_b, conv_b_w)


def _ffn_act(u0, conv_w, S, TM, F, TC):
    H = 16
    K = conv_w.shape[0]
    nF = F // TC

    def body(gc_ref, gh_ref, uc_ref, uh_ref, wg_ref, wu_ref, o_ref, ext_g, ext_u):
        live = (pl.program_id(0) > 0).astype(F32)
        ext_g[pl.ds(0, H), :] = gh_ref[...].astype(F32) * live
        ext_g[pl.ds(H, TM), :] = gc_ref[...].astype(F32)
        ext_u[pl.ds(0, H), :] = uh_ref[...].astype(F32) * live
        ext_u[pl.ds(H, TM), :] = uc_ref[...].astype(F32)
        g = _taps_causal(ext_g, wg_ref, K, H, TM, slice(None))
        up = _taps_causal(ext_u, wu_ref, K, H, TM, slice(None))
        o_ref[...] = (g * _sigmoid(g) * up).astype(BF16)

    blocks = [((TM, TC), BF16)] * 2 + [((H, TC), BF16)] * 2 + [((K, TC), F32)] * 2 + [((TM, TC), BF16)]
    scratch = [((H + TM, TC), F32)] * 2
    return pl.pallas_call(
        body, name="ffn_act", grid=(S // TM, nF),
        in_specs=[pl.BlockSpec((TM, TC), lambda i, j: (i, j)),
                  pl.BlockSpec((H, TC), lambda i, j: (jnp.maximum(i * (TM // H) - 1, 0), j)),
                  pl.BlockSpec((TM, TC), lambda i, j: (i, j + nF)),
                  pl.BlockSpec((H, TC), lambda i, j: (jnp.maximum(i * (TM // H) - 1, 0), j + nF)),
                  pl.BlockSpec((K, TC), lambda i, j: (0, j)),
                  pl.BlockSpec((K, TC), lambda i, j: (0, j + nF))],
        out_specs=pl.BlockSpec((TM, TC), lambda i, j: (i, j)),
        out_shape=jax.ShapeDtypeStruct((S, F), BF16),
        scratch_shapes=[pltpu.VMEM(s, d) for s, d in scratch],
        compiler_params=_params(("arbitrary", "arbitrary"), blocks, scratch, temps=[((TM, TC), F32)] * 8),
    )(u0, u0, u0, u0, conv_w, conv_w)


def _loss_head(h3, target, g_final, S, TM, D):
    def body(h_ref, t_ref, g_ref, loss_ref, dg_ref, dh_ref):
        @pl.when(pl.program_id(0) == 0)
        def _():
            loss_ref[...] = jnp.zeros_like(loss_ref)
            dg_ref[...] = jnp.zeros_like(dg_ref)
        h = h_ref[...]
        g = g_ref[...]
        r = _rms_stats(h)
        n = h * r
        diff = n * g - t_ref[...]
        loss_ref[...] += 0.5 * jnp.sum(jnp.mean(diff * diff, axis=-1, keepdims=True), axis=0, keepdims=True)
        dy = diff * (1.0 / D)
        dn = dy * g
        dh_ref[...] = r * (dn - n * jnp.mean(dn * n, axis=-1, keepdims=True))
        dg_ref[...] += _rsum(dy * n)

    blocks = [((TM, D), F32)] * 3 + [((1, D), F32)] * 2
    row = pl.BlockSpec((TM, D), lambda i: (i, 0))
    return pl.pallas_call(
        body, name="loss_head", grid=(S // TM,),
        in_specs=[row, row, pl.BlockSpec((1, D), lambda i: (0, 0))],
        out_specs=[pl.BlockSpec((1, 1), lambda i: (0, 0)), pl.BlockSpec((1, D), lambda i: (0, 0)), row],
        out_shape=[jax.ShapeDtypeStruct((1, 1), F32), jax.ShapeDtypeStruct((1, D), F32),
                   jax.ShapeDtypeStruct((S, D), F32)],
        compiler_params=_params(("arbitrary",), blocks, temps=[((TM, D), F32)] * 8),
    )(h3, target, g_final)


def _ffn_bwd_act(u0, dact, conv_w, S, TM, F, TC):
    H = 16
    K = conv_w.shape[0]
    nF = F // TC

    def body(sc_ref, sh_ref, pc_ref, ph_ref, ws_ref, wp_ref, da_ref, o_ref, ext_s, ext_p):
        j = pl.program_id(1)
        live = (pl.program_id(0) > 0).astype(F32)
        ext_s[pl.ds(0, H), :] = sh_ref[...].astype(F32) * live
        ext_s[pl.ds(H, TM), :] = sc_ref[...].astype(F32)
        ext_p[pl.ds(0, H), :] = ph_ref[...].astype(F32) * live
        ext_p[pl.ds(H, TM), :] = pc_ref[...].astype(F32)
        own = _taps_causal(ext_s, ws_ref, K, H, TM, slice(None))
        other = _taps_causal(ext_p, wp_ref, K, H, TM, slice(None))
        da = da_ref[...].astype(F32)

        @pl.when(j < nF)
        def _():
            s = _sigmoid(own)
            o_ref[...] = (da * other * s * (1.0 + own * (1.0 - s))).astype(BF16)

        @pl.when(j >= nF)
        def _():
            o_ref[...] = (da * other * _sigmoid(other)).astype(BF16)

    def prev(col):
        return lambda i, j: (jnp.maximum(i * (TM // H) - 1, 0), col(j))
    partner = lambda j: (j + nF) % (2 * nF)
    blocks = [((TM, TC), BF16)] * 2 + [((H, TC), BF16)] * 2 + [((K, TC), F32)] * 2 + [((TM, TC), BF16)] * 2
    scratch = [((H + TM, TC), F32)] * 2
    return pl.pallas_call(
        body, name="ffn_bwd_act", grid=(S // TM, 2 * nF),
        in_specs=[pl.BlockSpec((TM, TC), lambda i, j: (i, j)),
                  pl.BlockSpec((H, TC), prev(lambda j: j)),
                  pl.BlockSpec((TM, TC), lambda i, j: (i, partner(j))),
                  pl.BlockSpec((H, TC), prev(partner)),
                  pl.BlockSpec((K, TC), lambda i, j: (0, j)),
                  pl.BlockSpec((K, TC), lambda i, j: (0, partner(j))),
                  pl.BlockSpec((TM, TC), lambda i, j: (i, j % nF))],
        out_specs=pl.BlockSpec((TM, TC), lambda i, j: (i, j)),
        out_shape=jax.ShapeDtypeStruct((S, 2 * F), BF16),
        scratch_shapes=[pltpu.VMEM(s, d) for s, d in scratch],
        compiler_params=_params(("arbitrary", "arbitrary"), blocks, scratch, temps=[((TM, TC), F32)] * 10),
    )(u0, u0, u0, u0, conv_w, conv_w, dact)


def _ffn_bwd_conv(u0, du, conv_w, S, TM, TC):
    H = 16
    K = conv_w.shape[0]
    C = u0.shape[1]
    nI = S // TM

    def body(uc_ref, uh_ref, dc_ref, dn_ref, w_ref, o_ref, dw_ref, ext_u, ext_d):
        i = pl.program_id(1)
        @pl.when(i == 0)
        def _():
            dw_ref[...] = jnp.zeros_like(dw_ref)
        ext_u[pl.ds(0, H), :] = uh_ref[...].astype(F32) * (i > 0).astype(F32)
        ext_u[pl.ds(H, TM), :] = uc_ref[...].astype(F32)
        d = dc_ref[...].astype(F32)
        ext_d[pl.ds(0, TM), :] = d
        ext_d[pl.ds(TM, H), :] = dn_ref[...].astype(F32) * (i < nI - 1).astype(F32)
        o_ref[...] = _taps_anticausal(ext_d, w_ref, K, TM, slice(None)).astype(BF16)
        grads = _tap_grads(ext_u, d, K, H, TM, slice(None))
        for k in range(K):
            dw_ref[pl.ds(k, 1), :] += grads[k]

    blocks = [((TM, TC), BF16), ((H, TC), BF16)] * 2 + [((K, TC), F32)] * 2 + [((TM, TC), BF16)]
    scratch = [((H + TM, TC), F32)] * 2
    return pl.pallas_call(
        body, name="ffn_bwd_conv", grid=(C // TC, nI),
        in_specs=[pl.BlockSpec((TM, TC), lambda j, i: (i, j)),
                  pl.BlockSpec((H, TC), lambda j, i: (jnp.maximum(i * (TM // H) - 1, 0), j)),
                  pl.BlockSpec((TM, TC), lambda j, i: (i, j)),
                  pl.BlockSpec((H, TC), lambda j, i: (jnp.minimum((i + 1) * (TM // H), S // H - 1), j)),
                  pl.BlockSpec((K, TC), lambda j, i: (0, j))],
        out_specs=[pl.BlockSpec((TM, TC), lambda j, i: (i, j)), pl.BlockSpec((K, TC), lambda j, i: (0, j))],
        out_shape=[jax.ShapeDtypeStruct((S, C), BF16), jax.ShapeDtypeStruct((K, C), F32)],
        scratch_shapes=[pltpu.VMEM(s, d) for s, d in scratch],
        compiler_params=_params(("arbitrary", "arbitrary"), blocks, scratch, temps=[((TM, TC), F32)] * 8),
    )(u0, u0, du, du, conv_w)


def _mixer_bwd_ln(dcat, a1, ln_g, ln_b, S, TM, A):
    def body(dc_ref, a1_ref, g_ref, b_ref, da1_ref, acc_ref):
        @pl.when(pl.program_id(0) == 0)
        def _():
            acc_ref[...] = jnp.zeros_like(acc_ref)
        a1 = a1_ref[...]
        g = g_ref[...]
        mu = jnp.mean(a1, axis=-1, keepdims=True)
        d = a1 - mu
        rstd = lax.rsqrt(jnp.mean(d * d, axis=-1, keepdims=True) + EPS)
        nh = d * rstd
        a2 = nh * g + b_ref[...]
        s = _sigmoid(a2)
        da2 = dc_ref[...].astype(F32) * s * (1.0 + a2 * (1.0 - s))
        dnh = da2 * g
        da1 = rstd * (dnh - jnp.mean(dnh, axis=-1, keepdims=True)
                      - nh * jnp.mean(dnh * nh, axis=-1, keepdims=True))
        da1_ref[...] = da1
        acc_ref[pl.ds(0, 1), :] += _rsum(da2 * nh)
        acc_ref[pl.ds(1, 1), :] += _rsum(da2)
        acc_ref[pl.ds(2, 1), :] += _rsum(da1)

    blocks = [((TM, A), BF16), ((TM, A), F32), ((TM, A), F32), ((4, A), F32)]
    return pl.pallas_call(
        body, name="mixer_bwd_ln", grid=(S // TM,),
        in_specs=[pl.BlockSpec((TM, A), lambda i: (i, 0)), pl.BlockSpec((TM, A), lambda i: (i, 0)),
                  pl.BlockSpec((1, A), lambda i: (0, 0)), pl.BlockSpec((1, A), lambda i: (0, 0))],
        out_specs=[pl.BlockSpec((TM, A), lambda i: (i, 0)), pl.BlockSpec((4, A), lambda i: (0, 0))],
        out_shape=[jax.ShapeDtypeStruct((S, A), F32), jax.ShapeDtypeStruct((4, A), F32)],
        compiler_params=_params(("arbitrary",), blocks, temps=[((TM, A), F32)] * 12),
    )(dcat, a1, ln_g, ln_b)


def _mixer_bwd_conv(z, dcat, da1, conv_a_w, conv_b_w, S, TM, A):
    H = 32
    KA, KB = conv_a_w.shape[0], conv_b_w.shape[0]
    nI = S // TM
    n_chunks = A // LANES
    RB = _pick(TM, (64, 32))

    def body(zc_ref, zp_ref, zn_ref, dbc_ref, dbn_ref, d1c_ref, d1n_ref, wa_ref, wb_ref,
             dz_ref, dwa_ref, dwb_ref, ext_a0, ext_d1, ext_cb, ext_dc, da0_sc):
        i = pl.program_id(0)
        @pl.when(i == 0)
        def _():
            dwa_ref[...] = jnp.zeros_like(dwa_ref)
            dwb_ref[...] = jnp.zeros_like(dwb_ref)
        first = (i > 0).astype(F32)
        last = (i < nI - 1).astype(F32)
        zc = zc_ref[...].astype(F32)
        zp = zp_ref[...].astype(F32) * first
        a_val, a_gate = zc[:, 0:A], zc[:, A:2 * A]
        b_gate, c_gate, b_h = zc[:, 2 * A:3 * A], zc[:, 3 * A:4 * A], zc[:, 4 * A:5 * A]
        sig = _sigmoid(a_gate)
        ext_a0[pl.ds(0, H), :] = zp[:, 0:A] * _sigmoid(zp[:, A:2 * A])
        ext_a0[pl.ds(H, TM), :] = a_val * sig
        ext_d1[pl.ds(0, TM), :] = d1c_ref[...]
        ext_d1[pl.ds(TM, H), :] = d1n_ref[...] * last
        ext_cb[pl.ds(0, H), :] = zp[:, 3 * A:4 * A] * zp[:, 4 * A:5 * A]
        ext_cb[pl.ds(H, TM), :] = c_gate * b_h
        dbx = dbc_ref[...].astype(F32)
        dcbc = dbx * b_gate
        ext_dc[pl.ds(0, TM), :] = dcbc
        ext_dc[pl.ds(TM, H), :] = dbn_ref[...].astype(F32) * zn_ref[...].astype(F32) * last

        def chunk(c, carry):
            cs = pl.ds(pl.multiple_of(c * LANES, LANES), LANES)
            for r0 in range(0, TM, RB):
                acc = None
                for k in range(KA):
                    term = ext_d1[pl.ds(KA - 1 - k + r0, RB), cs] * wa_ref[pl.ds(k, 1), cs]
                    acc = term if acc is None else acc + term
                da0_sc[pl.ds(r0, RB), cs] = acc
            for k in range(KA):
                acc = None
                for r0 in range(0, TM, RB):
                    term = ext_a0[pl.ds(H - (KA - 1) + k + r0, RB), cs] * ext_d1[pl.ds(r0, RB), cs]
                    acc = term if acc is None else acc + term
                dwa_ref[pl.ds(k, 1), cs] += _rsum(acc)
            return carry
        lax.fori_loop(0, n_chunks, chunk, 0)

        da0 = da0_sc[...]
        dz_ref[:, 0:A] = (da0 * sig).astype(BF16)
        dz_ref[:, A:2 * A] = (da0 * a_val * sig * (1.0 - sig)).astype(BF16)
        cbc = _taps_causal(ext_cb, wb_ref, KB, H, TM, slice(None))
        dz_ref[:, 2 * A:3 * A] = (dbx * cbc).astype(BF16)
        dcb = _taps_anticausal(ext_dc, wb_ref, KB, TM, slice(None))
        dz_ref[:, 3 * A:4 * A] = (dcb * b_h).astype(BF16)
        dz_ref[:, 4 * A:5 * A] = (dcb * c_gate).astype(BF16)
        grads = _tap_grads(ext_cb, dcbc, KB, H, TM, slice(None))
        for k in range(KB):
            dwb_ref[pl.ds(k, 1), :] += grads[k]

    blocks = [((TM, 5 * A), BF16), ((H, 5 * A), BF16), ((H, A), BF16), ((TM, A), BF16), ((H, A), BF16),
              ((TM, A), F32), ((H, A), F32), ((KA, A), F32), ((KB, A), F32),
              ((TM, 5 * A), BF16), ((KA, A), F32), ((KB, A), F32)]
    scratch = [((H + TM, A), F32)] * 4 + [((TM, A), F32)]
    vec = lambda r: pl.BlockSpec((r, A), lambda i: (0, 0))
    return pl.pallas_call(
        body, name="mixer_bwd_conv", grid=(nI,),
        in_specs=[pl.BlockSpec((TM, 5 * A), lambda i: (i, 0)),
                  pl.BlockSpec((H, 5 * A), _prev_rows(TM, H, 0)),
                  pl.BlockSpec((H, A), _next_rows(S, TM, H, 2)),
                  pl.BlockSpec((TM, A), lambda i: (i, 1)),
                  pl.BlockSpec((H, A), _next_rows(S, TM, H, 1)),
                  pl.BlockSpec((TM, A), lambda i: (i, 0)),
                  pl.BlockSpec((H, A), _next_rows(S, TM, H, 0)),
                  vec(KA), vec(KB)],
        out_specs=[pl.BlockSpec((TM, 5 * A), lambda i: (i, 0)), vec(KA), vec(KB)],
        out_shape=[jax.ShapeDtypeStruct((S, 5 * A), BF16), jax.ShapeDtypeStruct((KA, A), F32),
                   jax.ShapeDtypeStruct((KB, A), F32)],
        scratch_shapes=[pltpu.VMEM(s, d) for s, d in scratch],
        compiler_params=_params(("arbitrary",), blocks, scratch,
                                temps=[((TM, 5 * A), F32)] * 2 + [((TM, A), F32)] * 14),
    )(z, z, z, dcat, dcat, da1, da1, conv_a_w, conv_b_w)


def _row_tile(R):
    return _pick(R, (256, 128, 64, 32, 16, 8))


def _cast_bf16(name, w):
    R, C = w.shape
    TR = _row_tile(R)

    def body(w_ref, o_ref):
        o_ref[...] = w_ref[...].astype(BF16)

    spec = pl.BlockSpec((TR, C), lambda r: (r, 0))
    return pl.pallas_call(body, name=name, grid=(R // TR,), in_specs=[spec], out_specs=spec,
                          out_shape=jax.ShapeDtypeStruct((R, C), BF16),
                          compiler_params=_params(("arbitrary",), [((TR, C), F32), ((TR, C), BF16)]))(w)


def _add_pair(name, dw, recv, c):
    _, _, Rh, C = dw.shape
    TR = _row_tile(Rh)

    def body(c_ref, a_ref, b_ref, o_ref):
        o_ref[...] = a_ref[...] + b_ref[...]

    grid_spec = pltpu.PrefetchScalarGridSpec(
        num_scalar_prefetch=1, grid=(N_CHIPS, Rh // TR),
        in_specs=[pl.BlockSpec((None, None, TR, C), lambda k, r, c_ref: (k, c_ref[0], r, 0)),
                  pl.BlockSpec((None, TR, C), lambda k, r, c_ref: (k, r, 0))],
        out_specs=pl.BlockSpec((None, TR, C), lambda k, r, c_ref: (k, r, 0)))
    return pl.pallas_call(body, name=name, grid_spec=grid_spec,
                          out_shape=jax.ShapeDtypeStruct((N_CHIPS, Rh, C), F32),
                          compiler_params=_params(("arbitrary", "arbitrary"), [((TR, C), F32)] * 3),
                          )(jnp.reshape(c, (1,)).astype(jnp.int32), dw, recv)


def _add_chips(name, parts):
    _, Rh, C = parts.shape
    TR = _row_tile(Rh)

    def body(p_ref, o_ref):
        o_ref[...] = ((p_ref[0] + p_ref[1]) + p_ref[2]) + p_ref[3]

    return pl.pallas_call(body, name=name, grid=(Rh // TR,),
                          in_specs=[pl.BlockSpec((N_CHIPS, TR, C), lambda r: (0, r, 0))],
                          out_specs=pl.BlockSpec((TR, C), lambda r: (r, 0)),
                          out_shape=jax.ShapeDtypeStruct((Rh, C), F32),
                          compiler_params=_params(("arbitrary",), [((N_CHIPS, TR, C), F32), ((TR, C), F32)]),
                          )(parts)


def _sum_devices(name, parts):
    _, R, C = parts.shape

    def body(p_ref, o_ref):
        acc = p_ref[0]
        for d in range(1, N_DEV):
            acc = acc + p_ref[d]
        o_ref[...] = acc

    return pl.pallas_call(body, name=name, out_shape=jax.ShapeDtypeStruct((R, C), F32),
                          in_specs=[pl.BlockSpec(memory_space=pltpu.VMEM)],
                          out_specs=pl.BlockSpec(memory_space=pltpu.VMEM))(parts)


def _adamw(name, w, g, m, v):
    R, C = w.shape
    TR = _pick(R, (128, 64, 32, 16, 8))
    c1 = 1.0 - ADAM_B1 ** ADAM_STEP
    c2 = 1.0 - ADAM_B2 ** ADAM_STEP

    def body(w_ref, g_ref, m_ref, v_ref, d_ref, nm_ref, nv_ref):
        g_ = g_ref[...]
        nm = ADAM_B1 * m_ref[...] + (1.0 - ADAM_B1) * g_
        nv = ADAM_B2 * v_ref[...] + (1.0 - ADAM_B2) * (g_ * g_)
        d_ref[...] = -ADAM_LR * ((nm / c1) / (jnp.sqrt(nv / c2) + ADAM_EPS) + ADAM_WD * w_ref[...])
        nm_ref[...] = nm
        nv_ref[...] = nv

    spec = pl.BlockSpec((TR, C), lambda r: (r, 0))
    shp = jax.ShapeDtypeStruct((R, C), F32)
    return pl.pallas_call(body, name=name, grid=(R // TR,), in_specs=[spec] * 4, out_specs=[spec] * 3,
                          out_shape=[shp] * 3,
                          compiler_params=_params(("arbitrary",), [((TR, C), F32)] * 7))(w, g, m, v)


def _place():
    x, y, c = lax.axis_index("x"), lax.axis_index("y"), lax.axis_index("c")
    others = [(1 - x, y), (x, 1 - y), (1 - x, 1 - y)]
    return x, y, c, others


def _allgather_small(name, block):
    R, C = block.shape

    def body(x_ref, out_ref, send_sems, recv_sems, local_sem):
        x, y, c, chips = _place()
        me, sibling = (x, y, c), (x, y, 1 - c)

        def rows(px, py, pc):
            return out_ref.at[4 * px + 2 * py + pc]

        def copy(k, blk, to, src=None):
            return pltpu.make_async_remote_copy(
                src_ref=rows(*blk) if src is None else src, dst_ref=rows(*blk),
                send_sem=send_sems.at[k], recv_sem=recv_sems.at[k], device_id=to, device_id_type=MESH)

        mine = pltpu.make_async_copy(x_ref, rows(*me), local_sem)
        mine.start()
        first = [copy(0, me, sibling, src=x_ref)]
        first += [copy(1 + j, me, (*chip, c), src=x_ref) for j, chip in enumerate(chips)]
        for cp in first:
            cp.start()
        passed = [copy(4 + j, (*chip, c), sibling) for j, chip in enumerate(chips)]
        for j, chip in enumerate(chips):
            copy(1 + j, (*chip, c), me).wait_recv()
            passed[j].start()
        copy(0, sibling, me).wait_recv()
        for j, chip in enumerate(chips):
            copy(4 + j, (*chip, 1 - c), me).wait_recv()
        for cp in first + passed:
            cp.wait_send()
        mine.wait()

    return pl.pallas_call(
        body, name=name, out_shape=jax.ShapeDtypeStruct((N_DEV, R, C), F32),
        in_specs=[pl.BlockSpec(memory_space=pltpu.VMEM)], out_specs=pl.BlockSpec(memory_space=pltpu.VMEM),
        scratch_shapes=[pltpu.SemaphoreType.DMA((7,)), pltpu.SemaphoreType.DMA((7,)), pltpu.SemaphoreType.DMA],
    )(block)


def _allgather_weights(shards):
    n = len(shards)

    def body(*refs):
        ins, outs = refs[:n], refs[n:2 * n]
        send_sems, recv_sems, local_sems = refs[2 * n:]
        x, y, c, chips = _place()
        me, sibling = (x, y, c), (x, y, 1 - c)

        def half_of(a, chip, half):
            rh = ins[a].shape[0] // 2
            return outs[a].at[2 * chip[0] + chip[1], pl.ds(half * rh, rh)]

        def copy(a, k, src, dst, to):
            return pltpu.make_async_remote_copy(
                src_ref=src, dst_ref=dst, send_sem=send_sems.at[6 * a + k], recv_sem=recv_sems.at[6 * a + k],
                device_id=to, device_id_type=MESH)

        local, first, passed = [], [], []
        for a in range(n):
            rh = ins[a].shape[0] // 2
            local.append(pltpu.make_async_copy(ins[a], outs[a].at[2 * x + y], local_sems.at[a]))
            local[a].start()
            my_half = ins[a].at[pl.ds(c * rh, rh)]
            first.append([copy(a, j, my_half, half_of(a, (x, y), c), (*chip, c)) for j, chip in enumerate(chips)])
            for cp in first[a]:
                cp.start()
        for a in range(n):
            passed.append([copy(a, 3 + j, half_of(a, chip, c), half_of(a, chip, c), sibling)
                           for j, chip in enumerate(chips)])
            for j, chip in enumerate(chips):
                copy(a, j, half_of(a, chip, c), half_of(a, chip, c), me).wait_recv()
                passed[a][j].start()
        for a in range(n):
            for j, chip in enumerate(chips):
                copy(a, 3 + j, half_of(a, chip, 1 - c), half_of(a, chip, 1 - c), me).wait_recv()
            for cp in first[a] + passed[a]:
                cp.wait_send()
            local[a].wait()

    return pl.pallas_call(
        body, name="allgather_weights",
        out_shape=[jax.ShapeDtypeStruct((N_CHIPS,) + s.shape, BF16) for s in shards],
        in_specs=[ANY] * n, out_specs=[ANY] * n,
        scratch_shapes=[pltpu.SemaphoreType.DMA((6 * n,)), pltpu.SemaphoreType.DMA((6 * n,)),
                        pltpu.SemaphoreType.DMA((n,))],
    )(*shards)


def _exchange_pairs(grads):
    n = len(grads)

    def body(*refs):
        ins, outs = refs[:n], refs[n:2 * n]
        send_sems, recv_sems = refs[2 * n:]
        x, y, c, _ = _place()
        copies = [pltpu.make_async_remote_copy(
            src_ref=ins[a].at[:, 1 - c], dst_ref=outs[a], send_sem=send_sems.at[a], recv_sem=recv_sems.at[a],
            device_id=(x, y, 1 - c), device_id_type=MESH) for a in range(n)]
        for cp in copies:
            cp.start()
        for cp in copies:
            cp.wait()

    return pl.pallas_call(
        body, name="grads_exchange_pairs",
        out_shape=[jax.ShapeDtypeStruct((N_CHIPS,) + g.shape[2:], F32) for g in grads],
        in_specs=[ANY] * n, out_specs=[ANY] * n,
        scratch_shapes=[pltpu.SemaphoreType.DMA((n,)), pltpu.SemaphoreType.DMA((n,))],
    )(*grads)


def _exchange_chips(parts):
    n = len(parts)

    def body(*refs):
        ins, outs = refs[:n], refs[n:2 * n]
        send_sems, recv_sems, local_sems = refs[2 * n:]
        x, y, c, chips = _place()
        mine = 2 * x + y
        local, sends = [], []
        for a in range(n):
            local.append(pltpu.make_async_copy(ins[a].at[mine], outs[a].at[mine], local_sems.at[a]))
            local[a].start()
            for j, chip in enumerate(chips):
                sends.append(pltpu.make_async_remote_copy(
                    src_ref=ins[a].at[2 * chip[0] + chip[1]], dst_ref=outs[a].at[mine],
                    send_sem=send_sems.at[3 * a + j], recv_sem=recv_sems.at[3 * a + j],
                    device_id=(*chip, c), device_id_type=MESH))
                sends[-1].start()
        for a in range(n):
            for j, chip in enumerate(chips):
                theirs = outs[a].at[2 * chip[0] + chip[1]]
                pltpu.make_async_remote_copy(
                    src_ref=theirs, dst_ref=theirs, send_sem=send_sems.at[3 * a + j],
                    recv_sem=recv_sems.at[3 * a + j], device_id=(x, y, c), device_id_type=MESH).wait_recv()
        for cp in sends:
            cp.wait_send()
        for cp in local:
            cp.wait()

    return pl.pallas_call(
        body, name="grads_exchange_chips",
        out_shape=[jax.ShapeDtypeStruct(p.shape, F32) for p in parts],
        in_specs=[ANY] * n, out_specs=[ANY] * n,
        scratch_shapes=[pltpu.SemaphoreType.DMA((3 * n,)), pltpu.SemaphoreType.DMA((3 * n,)),
                        pltpu.SemaphoreType.DMA((n,))],
    )(*parts)


def _share_halves(halves):
    n = len(halves)

    def body(*refs):
        ins, outs = refs[:n], refs[n:2 * n]
        send_sems, recv_sems, local_sems = refs[2 * n:]
        x, y, c, _ = _place()
        local, sends = [], []
        for a in range(n):
            local.append(pltpu.make_async_copy(ins[a], outs[a].at[c], local_sems.at[a]))
            local[a].start()
            sends.append(pltpu.make_async_remote_copy(
                src_ref=ins[a], dst_ref=outs[a].at[c], send_sem=send_sems.at[a], recv_sem=recv_sems.at[a],
                device_id=(x, y, 1 - c), device_id_type=MESH))
            sends[a].start()
        for a in range(n):
            theirs = outs[a].at[1 - c]
            pltpu.make_async_remote_copy(
                src_ref=theirs, dst_ref=theirs, send_sem=send_sems.at[a], recv_sem=recv_sems.at[a],
                device_id=(x, y, c), device_id_type=MESH).wait_recv()
        for cp in sends:
            cp.wait_send()
        for cp in local:
            cp.wait()

    return pl.pallas_call(
        body, name="grads_share_halves",
        out_shape=[jax.ShapeDtypeStruct((2,) + h.shape, F32) for h in halves],
        in_specs=[ANY] * n, out_specs=[ANY] * n,
        scratch_shapes=[pltpu.SemaphoreType.DMA((n,)), pltpu.SemaphoreType.DMA((n,)),
                        pltpu.SemaphoreType.DMA((n,))],
    )(*halves)


def _pack(arrays):
    pieces = []
    for a in arrays:
        flat = a.reshape(-1).astype(F32)
        pieces.append(jnp.pad(flat, (0, (-flat.size) % PACK_ALIGN)))
    return jnp.concatenate(pieces).reshape(-1, LANES)


def _unpack(buf, shapes):
    lead = buf.shape[:-2]
    flat = buf.reshape(lead + (-1,))
    out, off = [], 0
    for shp in shapes:
        size = 1
        for s in shp:
            size *= s
        out.append(flat[..., off:off + size].reshape(lead + tuple(shp)))
        off += size + (-size) % PACK_ALIGN
    return out


def _gather_channels(buf, shapes):
    per_chip = _unpack(buf[0::2], shapes)
    return [jnp.transpose(a, (1, 0, 2)).reshape(a.shape[1], -1) for a in per_chip]


def _mm_tile(n, rows, limit_bytes=6 * 1024 * 1024):
    for t in (1408, 1280, 1024, 640, 512, 384, 256, 128):
        if n % t == 0 and rows * t * 2 <= limit_bytes:
            return t
    raise ValueError(f"no column tile for {n} x {rows}")


def kernel(x, p, norm_mix_g, w_in, conv_a_w, conv_a_b, ln_a_g, ln_a_b, conv_b_w, w_out, norm_ffn_g, w_up, conv_ffn_w, w_down, w_ple_gate, b_ple_gate, w_ple_proj, norm_final_g, loss_target, m_norm_mix_g, m_w_in, m_conv_a_w, m_conv_a_b, m_ln_a_g, m_ln_a_b, m_conv_b_w, m_w_out, m_norm_ffn_g, m_w_up, m_conv_ffn_w, m_w_down, m_w_ple_gate, m_b_ple_gate, m_w_ple_proj, m_norm_final_g, v_norm_mix_g, v_w_in, v_conv_a_w, v_conv_a_b, v_ln_a_g, v_ln_a_b, v_conv_b_w, v_w_out, v_norm_ffn_g, v_w_up, v_conv_ffn_w, v_w_down, v_w_ple_gate, v_b_ple_gate, v_w_ple_proj, v_norm_final_g):
    S, D = x.shape[1], x.shape[2]
    P = p.shape[3]
    A = conv_a_b.shape[1]
    F = w_down.shape[1] * N_CHIPS
    KA, KB, KF = conv_a_w.shape[1], conv_b_w.shape[1], conv_ffn_w.shape[1]
    xi, yi, ci = lax.axis_index("x"), lax.axis_index("y"), lax.axis_index("c")
    chip = 2 * xi + yi

    TM = _pick(S, (512, 256, 128))
    TE = _pick(S, (256, 128))
    TC = _pick(F, (1408, 1024, 512, 256, 128))

    x2, p2, t2 = x.reshape(S, D), p.reshape(S, P), loss_target.reshape(S, D)
    gfin = norm_final_g.reshape(1, D)

    big = dict(w_in=w_in[0], w_out=w_out[0], w_up=w_up[0], w_down=w_down[0],
               w_ple_gate=w_ple_gate[0], w_ple_proj=w_ple_proj[0])
    names = list(big)
    gathered = _allgather_weights([_cast_bf16("cast_" + n, big[n]) for n in names])
    w_in3, w_out3, w_up3, w_down3, w_gate3, w_proj3 = gathered
    w_out_f = w_out3.reshape(2 * A, D)
    w_down_f = w_down3.reshape(F, D)
    w_gate_f = w_gate3.reshape(D, D)

    tap_shapes = [(KA, A // N_CHIPS), (KB, A // N_CHIPS), (KF, 2 * F // N_CHIPS)]
    taps = _allgather_small("allgather_taps", _pack([conv_a_w[0], conv_b_w[0], conv_ffn_w[0]]))
    conv_a_f, conv_b_f, conv_ffn_f = _gather_channels(taps, tap_shapes)

    def rms_prologue(row_r, vec_r, ro_r, ao_r):
        h = row_r[0][...]
        hn = (h * _rms_stats(h) * vec_r[0][...]).astype(BF16)
        ro_r[0][...] = hn
        return [hn]

    def pass_prologue(row_r, vec_r, ro_r, ao_r):
        return [row_r[0][...]]

    def cast_prologue(row_r, vec_r, ro_r, ao_r):
        hb = row_r[0][...].astype(BF16)
        ro_r[0][...] = hb
        return [hb]

    plain = lambda accs, tile_r, cv_r: [accs[0]]
    residual = lambda accs, tile_r, cv_r: [tile_r[0][...] + accs[0]]

    z, hn1 = _rows_mm("in_proj", S, TM, 5 * A, _mm_tile(5 * A // N_CHIPS, D), row_ins=[x2], vec_ins=[norm_mix_g],
                      weights=[(w_in3, "nn3")], tile_outs=[BF16], row_outs=[(D, BF16)],
                      prologue=rms_prologue, epilogue=plain, prologue_temps=4)
    a1, cat = _mixer_fwd(z, conv_a_f, conv_a_b, ln_a_g, ln_a_b, conv_b_f, S, TE, A)
    (h1,) = _rows_mm("out_proj", S, TM, D, _mm_tile(D, 2 * A), row_ins=[cat], weights=[(w_out_f, "nn2")],
                     tile_ins=[x2], tile_outs=[F32], prologue=pass_prologue, epilogue=residual)
    u0, hn2 = _rows_mm("up_proj", S, TM, 2 * F, _mm_tile(2 * F // N_CHIPS, D), row_ins=[h1], vec_ins=[norm_ffn_g],
                       weights=[(w_up3, "nn3")], tile_outs=[BF16], row_outs=[(D, BF16)],
                       prologue=rms_prologue, epilogue=plain, prologue_temps=4)
    act = _ffn_act(u0, conv_ffn_f, S, TM, F, TC)
    (h2,) = _rows_mm("down_proj", S, TM, D, _mm_tile(D, F), row_ins=[act], weights=[(w_down_f, "nn2")],
                     tile_ins=[h1], tile_outs=[F32], prologue=pass_prologue, epilogue=residual)

    def ple_prologue(row_r, vec_r, ro_r, ao_r):
        hb = row_r[0][...].astype(BF16)
        pb = row_r[1][...].astype(BF16)
        ro_r[0][...] = hb
        ro_r[1][...] = pb
        return [hb, pb]

    def ple_epilogue(accs, tile_r, cv_r):
        gate = _sigmoid(accs[0] + cv_r[0][...])
        return [tile_r[0][...] + accs[1] * gate, gate, accs[1]]

    h3, gate, pp, h2b, pb = _rows_mm(
        "ple_fwd", S, TM, D, _mm_tile(D // N_CHIPS, D), row_ins=[h2, p2], colvec_ins=[b_ple_gate],
        weights=[(w_gate_f, "nn2"), (w_proj3, "nn3")], tile_ins=[h2], tile_outs=[F32, BF16, BF16],
        row_outs=[(D, BF16), (P, BF16)], prologue=ple_prologue, epilogue=ple_epilogue, prologue_temps=2)
    loss_part, g_norm_final, dh3 = _loss_head(h3, t2, gfin, S, TE, D)

    def ple_bwd_prologue(row_r, vec_r, ro_r, ao_r):
        d = row_r[0][...]
        gt = row_r[2][...].astype(F32)
        dpre = d * row_r[1][...].astype(F32) * gt * (1.0 - gt)
        ro_r[0][...] = dpre.astype(BF16)
        ro_r[1][...] = (d * gt).astype(BF16)
        ao_r[0][...] += _rsum(dpre)
        return [dpre.astype(BF16)]

    dh2, dpre, dpp, g_b_gate = _rows_mm(
        "ple_bwd", S, TE, D, _mm_tile(D, D), row_ins=[dh3, pp, gate], weights=[(w_gate_f, "nt2")],
        tile_ins=[dh3], tile_outs=[F32], row_outs=[(D, BF16), (D, BF16)], acc_outs=[(1, D)],
        prologue=ple_bwd_prologue, epilogue=residual, prologue_temps=6)
    dact, dh2b = _rows_mm("down_bwd", S, TM, F, _mm_tile(F, D), row_ins=[dh2], weights=[(w_down_f, "nt2")],
                          tile_outs=[BF16], row_outs=[(D, BF16)], prologue=cast_prologue, epilogue=plain,
                          prologue_temps=1)
    du = _ffn_bwd_act(u0, dact, conv_ffn_f, S, TM, F, TC)
    du0, g_conv_ffn = _ffn_bwd_conv(u0, du, conv_ffn_f, S, TM, TC)

    def up_bwd_epilogue(acc, row_r, vec_r, ro_r, ao_r):
        dh, dg = _rms_bwd(row_r[0][...], vec_r[0][...], acc)
        dh1_ = row_r[1][...] + dh
        ro_r[0][...] = dh1_
        ro_r[1][...] = dh1_.astype(BF16)
        ao_r[0][...] += dg

    dh1, dh1b, g_norm_ffn = _kloop_mm(
        "up_bwd", S, TE, du0, w_up3, _mm_tile(2 * F // N_CHIPS, D), row_ins=[h1, dh2],
        vec_ins=[norm_ffn_g], row_outs=[(D, F32), (D, BF16)], acc_outs=[(1, D)], epilogue=up_bwd_epilogue)
    (dcat,) = _rows_mm("out_bwd", S, TM, 2 * A, _mm_tile(2 * A, D), row_ins=[dh1b], weights=[(w_out_f, "nt2")],
                       tile_outs=[BF16], prologue=pass_prologue, epilogue=plain)
    da1, ln_sums = _mixer_bwd_ln(dcat, a1, ln_a_g, ln_a_b, S, TE, A)
    dz, g_conv_a, g_conv_b = _mixer_bwd_conv(z, dcat, da1, conv_a_f, conv_b_f, S, TE, A)

    def in_bwd_epilogue(acc, row_r, vec_r, ro_r, ao_r):
        dh, dg = _rms_bwd(row_r[0][...], vec_r[0][...], acc)
        ro_r[0][...] = row_r[1][...] + dh
        ao_r[0][...] += dg

    dx, g_norm_mix = _kloop_mm(
        "in_bwd", S, TE, dz, w_in3, _mm_tile(5 * A // N_CHIPS, D), row_ins=[x2, dh1],
        vec_ins=[norm_mix_g], row_outs=[(D, F32)], acc_outs=[(1, D)], epilogue=in_bwd_epilogue)

    TK = _pick(S, (512, 256, 128))
    wt = lambda n: _pick(n, (1408, 1280, 1024, 512, 256, 128))
    part = dict(
        w_in=_tn_mm("dw_in", hn1, dz, wt(D), wt(5 * A // N_CHIPS), TK, cols_per_chip=5 * A // N_CHIPS),
        w_out=_tn_mm("dw_out", cat, dh1b, wt(2 * A // N_CHIPS), wt(D), TK),
        w_up=_tn_mm("dw_up", hn2, du0, wt(D), wt(2 * F // N_CHIPS), TK, cols_per_chip=2 * F // N_CHIPS),
        w_down=_tn_mm("dw_down", act, dh2b, wt(F // N_CHIPS), wt(D), TK),
        w_ple_gate=_tn_mm("dw_ple_gate", h2b, dpre, wt(D // N_CHIPS), wt(D), TK),
        w_ple_proj=_tn_mm("dw_ple_proj", pb, dpp, wt(P), wt(D // N_CHIPS), TK, cols_per_chip=D // N_CHIPS),
    )

    def halves(n):
        R, C = big[n].shape
        return part[n].reshape(N_CHIPS, 2, R // 2, C)

    from_sibling = _exchange_pairs([halves(n) for n in names])
    chip_sums = [_add_pair("pair_sum_" + n, halves(n), r, ci) for n, r in zip(names, from_sibling)]
    from_chips = _exchange_chips(chip_sums)
    reduced = _share_halves([_add_chips("chip_sum_" + n, r) for n, r in zip(names, from_chips)])
    moments = dict(w_in=(m_w_in, v_w_in), w_out=(m_w_out, v_w_out), w_up=(m_w_up, v_w_up),
                   w_down=(m_w_down, v_w_down), w_ple_gate=(m_w_ple_gate, v_w_ple_gate),
                   w_ple_proj=(m_w_ple_proj, v_w_ple_proj))
    grads, deltas, new_m, new_v = {}, {}, {}, {}
    for n, g in zip(names, reduced):
        g = g.reshape(big[n].shape)
        d_, m_, v_ = _adamw("adamw_" + n, big[n], g, moments[n][0][0], moments[n][1][0])
        grads[n], deltas[n], new_m[n], new_v[n] = g[None], d_[None], m_[None], v_[None]

    small = ["norm_mix_g", "conv_a_w", "conv_a_b", "ln_a_g", "ln_a_b", "conv_b_w", "norm_ffn_g",
             "conv_ffn_w", "b_ple_gate", "norm_final_g"]
    small_part = [g_norm_mix, g_conv_a, ln_sums[2:3], ln_sums[0:1], ln_sums[1:2], g_conv_b, g_norm_ffn,
                  g_conv_ffn, g_b_gate, g_norm_final]
    full_shapes = [a.shape for a in small_part]
    summed = _sum_devices("small_grads_sum", _allgather_small("allgather_small_grads", _pack(small_part)))
    small_g = dict(zip(small, _unpack(summed, full_shapes)))
    for n, width in (("conv_a_w", A), ("conv_b_w", A), ("conv_ffn_w", 2 * F)):
        small_g[n] = lax.dynamic_slice_in_dim(small_g[n], chip * (width // N_CHIPS), width // N_CHIPS, axis=1)
    small_w = dict(norm_mix_g=(norm_mix_g, m_norm_mix_g, v_norm_mix_g), conv_a_w=(conv_a_w, m_conv_a_w, v_conv_a_w),
                   conv_a_b=(conv_a_b, m_conv_a_b, v_conv_a_b), ln_a_g=(ln_a_g, m_ln_a_g, v_ln_a_g),
                   ln_a_b=(ln_a_b, m_ln_a_b, v_ln_a_b), conv_b_w=(conv_b_w, m_conv_b_w, v_conv_b_w),
                   norm_ffn_g=(norm_ffn_g, m_norm_ffn_g, v_norm_ffn_g),
                   conv_ffn_w=(conv_ffn_w, m_conv_ffn_w, v_conv_ffn_w),
                   b_ple_gate=(b_ple_gate, m_b_ple_gate, v_b_ple_gate),
                   norm_final_g=(norm_final_g, m_norm_final_g, v_norm_final_g))
    out_shapes = [small_w[n][0].shape for n in small]
    packed_g = _pack([small_g[n] for n in small])
    packed = [_pack([small_w[n][k] for n in small]) for k in range(3)]
    d_s, m_s, v_s = _adamw("adamw_small", packed[0], packed_g, packed[1], packed[2])
    for n, g, d_, m_, v_ in zip(small, _unpack(packed_g, out_shapes), _unpack(d_s, out_shapes),
                                _unpack(m_s, out_shapes), _unpack(v_s, out_shapes)):
        grads[n], deltas[n], new_m[n], new_v[n] = g, d_, m_, v_

    order = ["norm_mix_g", "w_in", "conv_a_w", "conv_a_b", "ln_a_g", "ln_a_b", "conv_b_w", "w_out", "norm_ffn_g",
             "w_up", "conv_ffn_w", "w_down", "w_ple_gate", "b_ple_gate", "w_ple_proj", "norm_final_g"]
    loss = lax.psum(loss_part[0, 0], ("x", "y", "c"))
    return (loss, dx.reshape(x.shape), *[grads[n] for n in order], *[deltas[n] for n in order],
            *[new_m[n] for n in order], *[new_v[n] for n in order])
```

```python
import functools

import jax
import jax.numpy as jnp
from jax import lax
from jax.experimental import pallas as pl
from jax.experimental.pallas import tpu as pltpu

F32 = jnp.float32
BF16 = jnp.bfloat16
MESH = pl.DeviceIdType.MESH
ANY = pl.BlockSpec(memory_space=pl.ANY)

EPS = 1e-6
ADAM_LR = 0.001
ADAM_B1 = 0.9
ADAM_B2 = 0.999
ADAM_EPS = 1e-08
ADAM_WD = 0.01
ADAM_STEP = 10

N_CHIPS = 4
N_DEV = 8
LANES = 128
SUBLANES = 8
PACK_ALIGN = LANES * SUBLANES
ROW_CHUNK = 32
VMEM_CAP = 60 * 1024 * 1024
VMEM_SLACK = 6 * 1024 * 1024


def _pick(n, cands):
    for c in cands:
        if n % c == 0:
            return c
    raise ValueError(f"no tile of {cands} divides {n}")


def _nbytes(shape, dtype):
    n = 1
    for s in shape:
        if s is not None:
            n *= s
    return n * jnp.dtype(dtype).itemsize


def _params(sem, blocks, scratch=(), temps=()):
    est = (2 * sum(_nbytes(s, d) for s, d in blocks) + sum(_nbytes(s, d) for s, d in scratch)
           + sum(_nbytes(s, d) for s, d in temps))
    return pltpu.CompilerParams(dimension_semantics=sem,
                                vmem_limit_bytes=min(est + VMEM_SLACK, VMEM_CAP))


def _sigmoid(x):
    return 1.0 / (1.0 + jnp.exp(-x))


def _rsum(x):
    return jnp.sum(x, axis=0, keepdims=True)


def _rms_stats(x):
    return lax.rsqrt(jnp.mean(x * x, axis=-1, keepdims=True) + EPS)


def _rms_bwd(h, g, dout):
    r = _rms_stats(h)
    n = h * r
    dn = dout * g
    dh = r * (dn - n * jnp.mean(dn * n, axis=-1, keepdims=True))
    return dh, _rsum(dout * n)


def _chip_major(nb):
    return lambda i, j: (j // nb, 0, j % nb)


def _rows_mm(name, S, TM, N, TN, *, row_ins, vec_ins=(), colvec_ins=(), weights, tile_ins=(),
             tile_outs, row_outs=(), acc_outs=(), prologue=None, epilogue):
    nI, nJ = S // TM, N // TN
    n_row, n_vec, n_cv, n_w, n_tile = len(row_ins), len(vec_ins), len(colvec_ins), len(weights), len(tile_ins)
    n_to, n_ro, n_ao = len(tile_outs), len(row_outs), len(acc_outs)

    in_specs, blocks, scratch, ks = [], [], [], []
    for a in row_ins:
        in_specs.append(pl.BlockSpec((TM, a.shape[1]), lambda i, j: (i, 0)))
        blocks.append(((TM, a.shape[1]), a.dtype))
    for a in vec_ins:
        in_specs.append(pl.BlockSpec(a.shape, lambda i, j: (0, 0)))
        blocks.append((a.shape, a.dtype))
    for a in colvec_ins:
        in_specs.append(pl.BlockSpec((1, TN), lambda i, j: (0, j)))
        blocks.append(((1, TN), a.dtype))
    for w, mode in weights:
        if mode == "nn2":
            k = w.shape[0]
            in_specs.append(pl.BlockSpec((k, TN), lambda i, j: (0, j)))
        elif mode == "nn3":
            k = w.shape[1]
            in_specs.append(pl.BlockSpec((None, k, TN), _chip_major(w.shape[2] // TN)))
        else:
            k = w.shape[1]
            in_specs.append(pl.BlockSpec((TN, k), lambda i, j: (j, 0)))
        ks.append(k)
        blocks.append(((k, TN), BF16))
        if prologue is not None:
            scratch.append(((TM, k), BF16))
    for a in tile_ins:
        in_specs.append(pl.BlockSpec((TM, TN), lambda i, j: (i, j)))
        blocks.append(((TM, TN), a.dtype))

    out_shape, out_specs = [], []
    for dt in tile_outs:
        out_shape.append(jax.ShapeDtypeStruct((S, N), dt))
        out_specs.append(pl.BlockSpec((TM, TN), lambda i, j: (i, j)))
        blocks.append(((TM, TN), dt))
    for width, dt in row_outs:
        out_shape.append(jax.ShapeDtypeStruct((S, width), dt))
        out_specs.append(pl.BlockSpec((TM, width), lambda i, j: (i, 0)))
        blocks.append(((TM, width), dt))
    for rows, width in acc_outs:
        out_shape.append(jax.ShapeDtypeStruct((rows, width), F32))
        out_specs.append(pl.BlockSpec((rows, width), lambda i, j: (0, 0)))
        blocks.append(((rows, width), F32))

    modes = [m for _, m in weights]

    def body(*refs):
        pos = 0
        def take(n):
            nonlocal pos
            out = refs[pos:pos + n]
            pos += n
            return out
        row_r, vec_r, cv_r, w_r, tile_r = take(n_row), take(n_vec), take(n_cv), take(n_w), take(n_tile)
        to_r, ro_r, ao_r, a_sc = take(n_to), take(n_ro), take(n_ao), take(len(scratch))
        i, j = pl.program_id(0), pl.program_id(1)

        if prologue is None:
            a_sc = row_r[:n_w]
        else:
            @pl.when(j == 0)
            def _():
                if n_ao:
                    @pl.when(i == 0)
                    def _():
                        for r in ao_r:
                            r[...] = jnp.zeros_like(r)

                def chunk(ci, carry):
                    rows = pl.ds(pl.multiple_of(ci * ROW_CHUNK, ROW_CHUNK), ROW_CHUNK)
                    for sc, a in zip(a_sc, prologue(rows, row_r, vec_r, ro_r, ao_r)):
                        sc[rows, :] = a
                    return carry
                lax.fori_loop(0, TM // ROW_CHUNK, chunk, 0)

        accs = []
        for w_ref, sc, mode in zip(w_r, a_sc, modes):
            if mode == "nt2":
                accs.append(lax.dot_general(sc[...], w_ref[...], (((1,), (1,)), ((), ())),
                                            preferred_element_type=F32))
            else:
                accs.append(jnp.dot(sc[...], w_ref[...], preferred_element_type=F32))
        outs = epilogue(accs, tile_r, cv_r)
        for r, o in zip(to_r, outs):
            r[...] = o.astype(r.dtype)

    return pl.pallas_call(
        body, name=name, grid=(nI, nJ), in_specs=in_specs, out_specs=out_specs, out_shape=out_shape,
        scratch_shapes=[pltpu.VMEM(s, d) for s, d in scratch],
        compiler_params=_params(("arbitrary", "arbitrary"), blocks, scratch, temps=[((TM, TN), F32)] * 3),
    )(*row_ins, *vec_ins, *colvec_ins, *[w for w, _ in weights], *tile_ins)


def _kloop_mm(name, S, TM, a, w3, TK, *, row_ins, vec_ins, row_outs, acc_outs, epilogue):
    _, N, Ks = w3.shape
    nb = Ks // TK
    nK = N_CHIPS * nb
    n_row, n_vec, n_ro, n_ao = len(row_ins), len(vec_ins), len(row_outs), len(acc_outs)

    in_specs = [pl.BlockSpec((TM, TK), lambda i, k: (i, k)),
                pl.BlockSpec((None, N, TK), lambda i, k: (k // nb, 0, k % nb))]
    blocks = [((TM, TK), BF16), ((N, TK), BF16)]
    for r in row_ins:
        in_specs.append(pl.BlockSpec((TM, r.shape[1]), lambda i, k: (i, 0)))
        blocks.append(((TM, r.shape[1]), r.dtype))
    for v in vec_ins:
        in_specs.append(pl.BlockSpec(v.shape, lambda i, k: (0, 0)))
        blocks.append((v.shape, v.dtype))
    out_shape, out_specs = [], []
    for width, dt in row_outs:
        out_shape.append(jax.ShapeDtypeStruct((S, width), dt))
        out_specs.append(pl.BlockSpec((TM, width), lambda i, k: (i, 0)))
        blocks.append(((TM, width), dt))
    for rows, width in acc_outs:
        out_shape.append(jax.ShapeDtypeStruct((rows, width), F32))
        out_specs.append(pl.BlockSpec((rows, width), lambda i, k: (0, 0)))
        blocks.append(((rows, width), F32))
    scratch = [((TM, N), F32)]

    def body(*refs):
        a_ref, w_ref = refs[0], refs[1]
        row_r = refs[2:2 + n_row]
        vec_r = refs[2 + n_row:2 + n_row + n_vec]
        pos = 2 + n_row + n_vec
        ro_r = refs[pos:pos + n_ro]
        ao_r = refs[pos + n_ro:pos + n_ro + n_ao]
        acc_sc = refs[pos + n_ro + n_ao]
        i, k = pl.program_id(0), pl.program_id(1)
        d = lax.dot_general(a_ref[...], w_ref[...], (((1,), (1,)), ((), ())), preferred_element_type=F32)

        @pl.when(k == 0)
        def _():
            acc_sc[...] = d

        @pl.when(k > 0)
        def _():
            acc_sc[...] += d

        @pl.when(k == nK - 1)
        def _():
            @pl.when(i == 0)
            def _():
                for r in ao_r:
                    r[...] = jnp.zeros_like(r)

            def chunk(ci, carry):
                rows = pl.ds(pl.multiple_of(ci * ROW_CHUNK, ROW_CHUNK), ROW_CHUNK)
                epilogue(acc_sc[rows, :], rows, row_r, vec_r, ro_r, ao_r)
                return carry
            lax.fori_loop(0, TM // ROW_CHUNK, chunk, 0)

    return pl.pallas_call(
        body, name=name, grid=(S // TM, nK), in_specs=in_specs, out_specs=out_specs, out_shape=out_shape,
        scratch_shapes=[pltpu.VMEM(s, d) for s, d in scratch],
        compiler_params=_params(("arbitrary", "arbitrary"), blocks, scratch, temps=[((TM, N), F32)]),
    )(a, w3, *row_ins, *vec_ins)


def _tn_mm(name, a, b, TMw, TNw, TK, cols_per_chip=None):
    S, M = a.shape
    N = b.shape[1]
    nK = S // TK
    if cols_per_chip is None:
        out_shape = jax.ShapeDtypeStruct((M, N), F32)
        out_spec = pl.BlockSpec((TMw, TNw), lambda i, j, k: (i, j))
    else:
        nb = cols_per_chip // TNw
        out_shape = jax.ShapeDtypeStruct((N_CHIPS, M, cols_per_chip), F32)
        out_spec = pl.BlockSpec((None, TMw, TNw), lambda i, j, k: (j // nb, i, j % nb))

    def body(a_ref, b_ref, o_ref):
        k = pl.program_id(2)
        d = lax.dot_general(a_ref[...], b_ref[...], (((0,), (0,)), ((), ())), preferred_element_type=F32)

        @pl.when(k == 0)
        def _():
            o_ref[...] = d

        @pl.when(k > 0)
        def _():
            o_ref[...] += d

    blocks = [((TK, TMw), BF16), ((TK, TNw), BF16), ((TMw, TNw), F32)]
    return pl.pallas_call(
        body, name=name, grid=(M // TMw, N // TNw, nK),
        in_specs=[pl.BlockSpec((TK, TMw), lambda i, j, k: (k, i)),
                  pl.BlockSpec((TK, TNw), lambda i, j, k: (k, j))],
        out_specs=out_spec, out_shape=out_shape,
        compiler_params=_params(("arbitrary", "arbitrary", "arbitrary"), blocks,
                                temps=[((TMw, TNw), F32), ((TK, TMw), BF16)]),
    )(a, b)


def _prev_rows(TM, H, col):
    return lambda i: (jnp.maximum(i * (TM // H) - 1, 0), col)


def _next_rows(S, TM, H, col):
    return lambda i: (jnp.minimum((i + 1) * (TM // H), S // H - 1), col)


def _taps_causal(ext_ref, w_ref, K, H, TM, cs):
    acc = None
    for k in range(K):
        term = ext_ref[pl.ds(H - (K - 1) + k, TM), cs] * w_ref[pl.ds(k, 1), cs]
        acc = term if acc is None else acc + term
    return acc


def _taps_anticausal(ext_ref, w_ref, K, TM, cs):
    acc = None
    for k in range(K):
        term = ext_ref[pl.ds(K - 1 - k, TM), cs] * w_ref[pl.ds(k, 1), cs]
        acc = term if acc is None else acc + term
    return acc


def _tap_grads(ext_ref, g, K, H, TM, cs):
    return [_rsum(ext_ref[pl.ds(H - (K - 1) + k, TM), cs] * g) for k in range(K)]


def _mixer_fwd(z, conv_a_w, conv_a_b, ln_g, ln_b, conv_b_w, S, TM, A):
    H = 32
    KA, KB = conv_a_w.shape[0], conv_b_w.shape[0]
    n_chunks = A // LANES
    RB = _pick(TM, (64, 32))

    def body(zc_ref, zh_ref, wa_ref, ba_ref, g_ref, b_ref, wb_ref, a1_ref, cat_ref, ext_a, ext_b):
        i = pl.program_id(0)
        live = (i > 0).astype(F32)
        zc = zc_ref[...].astype(F32)
        zh = zh_ref[...].astype(F32) * live
        ext_a[pl.ds(0, H), :] = zh[:, 0:A] * _sigmoid(zh[:, A:2 * A])
        ext_a[pl.ds(H, TM), :] = zc[:, 0:A] * _sigmoid(zc[:, A:2 * A])
        ext_b[pl.ds(0, H), :] = zh[:, 3 * A:4 * A] * zh[:, 4 * A:5 * A]
        ext_b[pl.ds(H, TM), :] = zc[:, 3 * A:4 * A] * zc[:, 4 * A:5 * A]

        def chunk(c, carry):
            cs = pl.ds(pl.multiple_of(c * LANES, LANES), LANES)
            for r0 in range(0, TM, RB):
                acc = None
                for k in range(KA):
                    term = ext_a[pl.ds(H - (KA - 1) + k + r0, RB), cs] * wa_ref[pl.ds(k, 1), cs]
                    acc = term if acc is None else acc + term
                a1_ref[pl.ds(r0, RB), cs] = acc + ba_ref[:, cs]
            return carry
        lax.fori_loop(0, n_chunks, chunk, 0)

        a1 = a1_ref[...]
        mu = jnp.mean(a1, axis=-1, keepdims=True)
        d = a1 - mu
        var = jnp.mean(d * d, axis=-1, keepdims=True)
        a2 = d * lax.rsqrt(var + EPS) * g_ref[...] + b_ref[...]
        cat_ref[:, 0:A] = (a2 * _sigmoid(a2)).astype(BF16)
        cbc = _taps_causal(ext_b, wb_ref, KB, H, TM, slice(None))
        cat_ref[:, A:2 * A] = (zc[:, 2 * A:3 * A] * cbc).astype(BF16)

    blocks = [((TM, 5 * A), BF16), ((H, 5 * A), BF16), ((KA, A), F32), ((KB, A), F32),
              ((TM, A), F32), ((TM, 2 * A), BF16)]
    scratch = [((H + TM, A), F32), ((H + TM, A), F32)]
    vec = lambda r: pl.BlockSpec((r, A), lambda i: (0, 0))
    return pl.pallas_call(
        body, name="mixer_fwd", grid=(S // TM,),
        in_specs=[pl.BlockSpec((TM, 5 * A), lambda i: (i, 0)),
                  pl.BlockSpec((H, 5 * A), _prev_rows(TM, H, 0)),
                  vec(KA), vec(1), vec(1), vec(1), vec(KB)],
        out_specs=[pl.BlockSpec((TM, A), lambda i: (i, 0)), pl.BlockSpec((TM, 2 * A), lambda i: (i, 0))],
        out_shape=[jax.ShapeDtypeStruct((S, A), F32), jax.ShapeDtypeStruct((S, 2 * A), BF16)],
        scratch_shapes=[pltpu.VMEM(s, d) for s, d in scratch],
        compiler_params=_params(("arbitrary",), blocks, scratch,
                                temps=[((TM, 5 * A), F32)] * 2 + [((TM, A), F32)] * 10),
    )(z, z, conv_a_w, conv_a_b, ln_g, ln_b, conv_b_w)


def _ffn_act(u0, conv_w, S, TM, F, TC):
    H = 16
    K = conv_w.shape[0]
    nF = F // TC

    def body(gc_ref, gh_ref, uc_ref, uh_ref, wg_ref, wu_ref, o_ref, ext_g, ext_u):
        live = (pl.program_id(0) > 0).astype(F32)
        ext_g[pl.ds(0, H), :] = gh_ref[...].astype(F32) * live
        ext_g[pl.ds(H, TM), :] = gc_ref[...].astype(F32)
        ext_u[pl.ds(0, H), :] = uh_ref[...].astype(F32) * live
        ext_u[pl.ds(H, TM), :] = uc_ref[...].astype(F32)
        g = _taps_causal(ext_g, wg_ref, K, H, TM, slice(None))
        up = _taps_causal(ext_u, wu_ref, K, H, TM, slice(None))
        o_ref[...] = (g * _sigmoid(g) * up).astype(BF16)

    blocks = [((TM, TC), BF16)] * 2 + [((H, TC), BF16)] * 2 + [((K, TC), F32)] * 2 + [((TM, TC), BF16)]
    scratch = [((H + TM, TC), F32)] * 2
    return pl.pallas_call(
        body, name="ffn_act", grid=(S // TM, nF),
        in_specs=[pl.BlockSpec((TM, TC), lambda i, j: (i, j)),
                  pl.BlockSpec((H, TC), lambda i, j: (jnp.maximum(i * (TM // H) - 1, 0), j)),
                  pl.BlockSpec((TM, TC), lambda i, j: (i, j + nF)),
                  pl.BlockSpec((H, TC), lambda i, j: (jnp.maximum(i * (TM // H) - 1, 0), j + nF)),
                  pl.BlockSpec((K, TC), lambda i, j: (0, j)),
                  pl.BlockSpec((K, TC), lambda i, j: (0, j + nF))],
        out_specs=pl.BlockSpec((TM, TC), lambda i, j: (i, j)),
        out_shape=jax.ShapeDtypeStruct((S, F), BF16),
        scratch_shapes=[pltpu.VMEM(s, d) for s, d in scratch],
        compiler_params=_params(("arbitrary", "arbitrary"), blocks, scratch, temps=[((TM, TC), F32)] * 8),
    )(u0, u0, u0, u0, conv_w, conv_w)


def _loss_head(h3, target, g_final, S, TM, D):
    def body(h_ref, t_ref, g_ref, loss_ref, dg_ref, dh_ref):
        @pl.when(pl.program_id(0) == 0)
        def _():
            loss_ref[...] = jnp.zeros_like(loss_ref)
            dg_ref[...] = jnp.zeros_like(dg_ref)
        h = h_ref[...]
        g = g_ref[...]
        r = _rms_stats(h)
        n = h * r
        diff = n * g - t_ref[...]
        loss_ref[...] += 0.5 * jnp.sum(jnp.mean(diff * diff, axis=-1, keepdims=True), axis=0, keepdims=True)
        dy = diff * (1.0 / D)
        dn = dy * g
        dh_ref[...] = r * (dn - n * jnp.mean(dn * n, axis=-1, keepdims=True))
        dg_ref[...] += _rsum(dy * n)

    blocks = [((TM, D), F32)] * 3 + [((1, D), F32)] * 2
    row = pl.BlockSpec((TM, D), lambda i: (i, 0))
    return pl.pallas_call(
        body, name="loss_head", grid=(S // TM,),
        in_specs=[row, row, pl.BlockSpec((1, D), lambda i: (0, 0))],
        out_specs=[pl.BlockSpec((1, 1), lambda i: (0, 0)), pl.BlockSpec((1, D), lambda i: (0, 0)), row],
        out_shape=[jax.ShapeDtypeStruct((1, 1), F32), jax.ShapeDtypeStruct((1, D), F32),
                   jax.ShapeDtypeStruct((S, D), F32)],
        compiler_params=_params(("arbitrary",), blocks, temps=[((TM, D), F32)] * 8),
    )(h3, target, g_final)


def _ffn_bwd_act(u0, dact, conv_w, S, TM, F, TC):
    H = 16
    K = conv_w.shape[0]
    nF = F // TC

    def body(sc_ref, sh_ref, pc_ref, ph_ref, ws_ref, wp_ref, da_ref, o_ref, ext_s, ext_p):
        j = pl.program_id(1)
        live = (pl.program_id(0) > 0).astype(F32)
        ext_s[pl.ds(0, H), :] = sh_ref[...].astype(F32) * live
        ext_s[pl.ds(H, TM), :] = sc_ref[...].astype(F32)
        ext_p[pl.ds(0, H), :] = ph_ref[...].astype(F32) * live
        ext_p[pl.ds(H, TM), :] = pc_ref[...].astype(F32)
        own = _taps_causal(ext_s, ws_ref, K, H, TM, slice(None))
        other = _taps_causal(ext_p, wp_ref, K, H, TM, slice(None))
        da = da_ref[...].astype(F32)

        @pl.when(j < nF)
        def _():
            s = _sigmoid(own)
            o_ref[...] = (da * other * s * (1.0 + own * (1.0 - s))).astype(BF16)

        @pl.when(j >= nF)
        def _():
            o_ref[...] = (da * other * _sigmoid(other)).astype(BF16)

    def prev(col):
        return lambda i, j: (jnp.maximum(i * (TM // H) - 1, 0), col(j))
    partner = lambda j: (j + nF) % (2 * nF)
    blocks = [((TM, TC), BF16)] * 2 + [((H, TC), BF16)] * 2 + [((K, TC), F32)] * 2 + [((TM, TC), BF16)] * 2
    scratch = [((H + TM, TC), F32)] * 2
    return pl.pallas_call(
        body, name="ffn_bwd_act", grid=(S // TM, 2 * nF),
        in_specs=[pl.BlockSpec((TM, TC), lambda i, j: (i, j)),
                  pl.BlockSpec((H, TC), prev(lambda j: j)),
                  pl.BlockSpec((TM, TC), lambda i, j: (i, partner(j))),
                  pl.BlockSpec((H, TC), prev(partner)),
                  pl.BlockSpec((K, TC), lambda i, j: (0, j)),
                  pl.BlockSpec((K, TC), lambda i, j: (0, partner(j))),
                  pl.BlockSpec((TM, TC), lambda i, j: (i, j % nF))],
        out_specs=pl.BlockSpec((TM, TC), lambda i, j: (i, j)),
        out_shape=jax.ShapeDtypeStruct((S, 2 * F), BF16),
        scratch_shapes=[pltpu.VMEM(s, d) for s, d in scratch],
        compiler_params=_params(("arbitrary", "arbitrary"), blocks, scratch, temps=[((TM, TC), F32)] * 10),
    )(u0, u0, u0, u0, conv_w, conv_w, dact)


def _ffn_bwd_conv(u0, du, conv_w, S, TM, TC):
    H = 16
    K = conv_w.shape[0]
    C = u0.shape[1]
    nI = S // TM

    def body(uc_ref, uh_ref, dc_ref, dn_ref, w_ref, o_ref, dw_ref, ext_u, ext_d):
        i = pl.program_id(1)
        @pl.when(i == 0)
        def _():
            dw_ref[...] = jnp.zeros_like(dw_ref)
        ext_u[pl.ds(0, H), :] = uh_ref[...].astype(F32) * (i > 0).astype(F32)
        ext_u[pl.ds(H, TM), :] = uc_ref[...].astype(F32)
        d = dc_ref[...].astype(F32)
        ext_d[pl.ds(0, TM), :] = d
        ext_d[pl.ds(TM, H), :] = dn_ref[...].astype(F32) * (i < nI - 1).astype(F32)
        o_ref[...] = _taps_anticausal(ext_d, w_ref, K, TM, slice(None)).astype(BF16)
        grads = _tap_grads(ext_u, d, K, H, TM, slice(None))
        for k in range(K):
            dw_ref[pl.ds(k, 1), :] += grads[k]

    blocks = [((TM, TC), BF16), ((H, TC), BF16)] * 2 + [((K, TC), F32)] * 2 + [((TM, TC), BF16)]
    scratch = [((H + TM, TC), F32)] * 2
    return pl.pallas_call(
        body, name="ffn_bwd_conv", grid=(C // TC, nI),
        in_specs=[pl.BlockSpec((TM, TC), lambda j, i: (i, j)),
                  pl.BlockSpec((H, TC), lambda j, i: (jnp.maximum(i * (TM // H) - 1, 0), j)),
                  pl.BlockSpec((TM, TC), lambda j, i: (i, j)),
                  pl.BlockSpec((H, TC), lambda j, i: (jnp.minimum((i + 1) * (TM // H), S // H - 1), j)),
                  pl.BlockSpec((K, TC), lambda j, i: (0, j))],
        out_specs=[pl.BlockSpec((TM, TC), lambda j, i: (i, j)), pl.BlockSpec((K, TC), lambda j, i: (0, j))],
        out_shape=[jax.ShapeDtypeStruct((S, C), BF16), jax.ShapeDtypeStruct((K, C), F32)],
        scratch_shapes=[pltpu.VMEM(s, d) for s, d in scratch],
        compiler_params=_params(("arbitrary", "arbitrary"), blocks, scratch, temps=[((TM, TC), F32)] * 8),
    )(u0, u0, du, du, conv_w)


def _mixer_bwd_ln(dcat, a1, ln_g, ln_b, S, TM, A):
    def body(dc_ref, a1_ref, g_ref, b_ref, da1_ref, acc_ref):
        @pl.when(pl.program_id(0) == 0)
        def _():
            acc_ref[...] = jnp.zeros_like(acc_ref)
        a1 = a1_ref[...]
        g = g_ref[...]
        mu = jnp.mean(a1, axis=-1, keepdims=True)
        d = a1 - mu
        rstd = lax.rsqrt(jnp.mean(d * d, axis=-1, keepdims=True) + EPS)
        nh = d * rstd
        a2 = nh * g + b_ref[...]
        s = _sigmoid(a2)
        da2 = dc_ref[...].astype(F32) * s * (1.0 + a2 * (1.0 - s))
        dnh = da2 * g
        da1 = rstd * (dnh - jnp.mean(dnh, axis=-1, keepdims=True)
                      - nh * jnp.mean(dnh * nh, axis=-1, keepdims=True))
        da1_ref[...] = da1
        acc_ref[pl.ds(0, 1), :] += _rsum(da2 * nh)
        acc_ref[pl.ds(1, 1), :] += _rsum(da2)
        acc_ref[pl.ds(2, 1), :] += _rsum(da1)

    blocks = [((TM, A), BF16), ((TM, A), F32), ((TM, A), F32), ((4, A), F32)]
    return pl.pallas_call(
        body, name="mixer_bwd_ln", grid=(S // TM,),
        in_specs=[pl.BlockSpec((TM, A), lambda i: (i, 0)), pl.BlockSpec((TM, A), lambda i: (i, 0)),
                  pl.BlockSpec((1, A), lambda i: (0, 0)), pl.BlockSpec((1, A), lambda i: (0, 0))],
        out_specs=[pl.BlockSpec((TM, A), lambda i: (i, 0)), pl.BlockSpec((4, A), lambda i: (0, 0))],
        out_shape=[jax.ShapeDtypeStruct((S, A), F32), jax.ShapeDtypeStruct((4, A), F32)],
        compiler_params=_params(("arbitrary",), blocks, temps=[((TM, A), F32)] * 12),
    )(dcat, a1, ln_g, ln_b)


def _mixer_bwd_conv(z, dcat, da1, conv_a_w, conv_b_w, S, TM, A):
    H = 32
    KA, KB = conv_a_w.shape[0], conv_b_w.shape[0]
    nI = S // TM
    n_chunks = A // LANES
    RB = _pick(TM, (64, 32))

    def body(zc_ref, zp_ref, zn_ref, dbc_ref, dbn_ref, d1c_ref, d1n_ref, wa_ref, wb_ref,
             dz_ref, dwa_ref, dwb_ref, ext_a0, ext_d1, ext_cb, ext_dc, da0_sc):
        i = pl.program_id(0)
        @pl.when(i == 0)
        def _():
            dwa_ref[...] = jnp.zeros_like(dwa_ref)
            dwb_ref[...] = jnp.zeros_like(dwb_ref)
        first = (i > 0).astype(F32)
        last = (i < nI - 1).astype(F32)
        zc = zc_ref[...].astype(F32)
        zp = zp_ref[...].astype(F32) * first
        a_val, a_gate = zc[:, 0:A], zc[:, A:2 * A]
        b_gate, c_gate, b_h = zc[:, 2 * A:3 * A], zc[:, 3 * A:4 * A], zc[:, 4 * A:5 * A]
        sig = _sigmoid(a_gate)
        ext_a0[pl.ds(0, H), :] = zp[:, 0:A] * _sigmoid(zp[:, A:2 * A])
        ext_a0[pl.ds(H, TM), :] = a_val * sig
        ext_d1[pl.ds(0, TM), :] = d1c_ref[...]
        ext_d1[pl.ds(TM, H), :] = d1n_ref[...] * last
        ext_cb[pl.ds(0, H), :] = zp[:, 3 * A:4 * A] * zp[:, 4 * A:5 * A]
        ext_cb[pl.ds(H, TM), :] = c_gate * b_h
        dbx = dbc_ref[...].astype(F32)
        dcbc = dbx * b_gate
        ext_dc[pl.ds(0, TM), :] = dcbc
        ext_dc[pl.ds(TM, H), :] = dbn_ref[...].astype(F32) * zn_ref[...].astype(F32) * last

        def chunk(c, carry):
            cs = pl.ds(pl.multiple_of(c * LANES, LANES), LANES)
            for r0 in range(0, TM, RB):
                acc = None
                for k in range(KA):
                    term = ext_d1[pl.ds(KA - 1 - k + r0, RB), cs] * wa_ref[pl.ds(k, 1), cs]
                    acc = term if acc is None else acc + term
                da0_sc[pl.ds(r0, RB), cs] = acc
            for k in range(KA):
                acc = None
                for r0 in range(0, TM, RB):
                    term = ext_a0[pl.ds(H - (KA - 1) + k + r0, RB), cs] * ext_d1[pl.ds(r0, RB), cs]
                    acc = term if acc is None else acc + term
                dwa_ref[pl.ds(k, 1), cs] += _rsum(acc)
            return carry
        lax.fori_loop(0, n_chunks, chunk, 0)

        da0 = da0_sc[...]
        dz_ref[:, 0:A] = (da0 * sig).astype(BF16)
        dz_ref[:, A:2 * A] = (da0 * a_val * sig * (1.0 - sig)).astype(BF16)
        cbc = _taps_causal(ext_cb, wb_ref, KB, H, TM, slice(None))
        dz_ref[:, 2 * A:3 * A] = (dbx * cbc).astype(BF16)
        dcb = _taps_anticausal(ext_dc, wb_ref, KB, TM, slice(None))
        dz_ref[:, 3 * A:4 * A] = (dcb * b_h).astype(BF16)
        dz_ref[:, 4 * A:5 * A] = (dcb * c_gate).astype(BF16)
        grads = _tap_grads(ext_cb, dcbc, KB, H, TM, slice(None))
        for k in range(KB):
            dwb_ref[pl.ds(k, 1), :] += grads[k]

    blocks = [((TM, 5 * A), BF16), ((H, 5 * A), BF16), ((H, A), BF16), ((TM, A), BF16), ((H, A), BF16),
              ((TM, A), F32), ((H, A), F32), ((KA, A), F32), ((KB, A), F32),
              ((TM, 5 * A), BF16), ((KA, A), F32), ((KB, A), F32)]
    scratch = [((H + TM, A), F32)] * 4 + [((TM, A), F32)]
    vec = lambda r: pl.BlockSpec((r, A), lambda i: (0, 0))
    return pl.pallas_call(
        body, name="mixer_bwd_conv", grid=(nI,),
        in_specs=[pl.BlockSpec((TM, 5 * A), lambda i: (i, 0)),
                  pl.BlockSpec((H, 5 * A), _prev_rows(TM, H, 0)),
                  pl.BlockSpec((H, A), _next_rows(S, TM, H, 2)),
                  pl.BlockSpec((TM, A), lambda i: (i, 1)),
                  pl.BlockSpec((H, A), _next_rows(S, TM, H, 1)),
                  pl.BlockSpec((TM, A), lambda i: (i, 0)),
                  pl.BlockSpec((H, A), _next_rows(S, TM, H, 0)),
                  vec(KA), vec(KB)],
        out_specs=[pl.BlockSpec((TM, 5 * A), lambda i: (i, 0)), vec(KA), vec(KB)],
        out_shape=[jax.ShapeDtypeStruct((S, 5 * A), BF16), jax.ShapeDtypeStruct((KA, A), F32),
                   jax.ShapeDtypeStruct((KB, A), F32)],
        scratch_shapes=[pltpu.VMEM(s, d) for s, d in scratch],
        compiler_params=_params(("arbitrary",), blocks, scratch,
                                temps=[((TM, 5 * A), F32)] * 2 + [((TM, A), F32)] * 14),
    )(z, z, z, dcat, dcat, da1, da1, conv_a_w, conv_b_w)


def _row_tile(R):
    return _pick(R, (256, 128, 64, 32, 16, 8))


def _scalars(*vals):
    return jnp.stack([jnp.asarray(v, jnp.int32) for v in vals])


def _cast_into_gathered(name, w, chip):
    R, C = w.shape
    TR = _row_tile(R)

    def body(s_ref, w_ref, o_ref):
        o_ref[...] = w_ref[...].astype(BF16)

    grid_spec = pltpu.PrefetchScalarGridSpec(
        num_scalar_prefetch=1, grid=(R // TR,),
        in_specs=[pl.BlockSpec((TR, C), lambda r, s: (r, 0))],
        out_specs=pl.BlockSpec((None, TR, C), lambda r, s: (s[0], r, 0)))
    return pl.pallas_call(body, name=name, grid_spec=grid_spec,
                          out_shape=jax.ShapeDtypeStruct((N_CHIPS, R, C), BF16),
                          compiler_params=_params(("arbitrary",), [((TR, C), F32), ((TR, C), BF16)]),
                          )(_scalars(chip), w)


def _add_pair(name, dw, recv, c):
    _, _, Rh, C = dw.shape
    TR = _row_tile(Rh)

    def body(c_ref, a_ref, b_ref, o_ref):
        o_ref[...] = a_ref[...] + b_ref[...]

    grid_spec = pltpu.PrefetchScalarGridSpec(
        num_scalar_prefetch=1, grid=(N_CHIPS, Rh // TR),
        in_specs=[pl.BlockSpec((None, None, TR, C), lambda k, r, c_ref: (k, c_ref[0], r, 0)),
                  pl.BlockSpec((None, TR, C), lambda k, r, c_ref: (k, r, 0))],
        out_specs=pl.BlockSpec((None, TR, C), lambda k, r, c_ref: (k, r, 0)))
    return pl.pallas_call(body, name=name, grid_spec=grid_spec,
                          out_shape=jax.ShapeDtypeStruct((N_CHIPS, Rh, C), F32),
                          compiler_params=_params(("arbitrary", "arbitrary"), [((TR, C), F32)] * 3),
                          )(_scalars(c), dw, recv)


def _add_chips(name, parts, recv, chip, c):
    _, Rh, C = parts.shape
    TR = _row_tile(Rh)

    def body(s_ref, p_ref, r_ref, o_ref):
        o_ref[...] = ((p_ref[...] + r_ref[0]) + r_ref[1]) + r_ref[2]

    grid_spec = pltpu.PrefetchScalarGridSpec(
        num_scalar_prefetch=1, grid=(Rh // TR,),
        in_specs=[pl.BlockSpec((None, TR, C), lambda r, s: (s[0], r, 0)),
                  pl.BlockSpec((N_CHIPS - 1, TR, C), lambda r, s: (0, r, 0))],
        out_specs=pl.BlockSpec((None, TR, C), lambda r, s: (s[1], r, 0)))
    return pl.pallas_call(body, name=name, grid_spec=grid_spec,
                          out_shape=jax.ShapeDtypeStruct((2, Rh, C), F32),
                          compiler_params=_params(("arbitrary",), [((N_CHIPS + 1, TR, C), F32)]),
                          )(_scalars(chip, c), parts, recv)


def _sum_devices(name, parts):
    _, R, C = parts.shape

    def body(p_ref, o_ref):
        acc = p_ref[0]
        for d in range(1, N_DEV):
            acc = acc + p_ref[d]
        o_ref[...] = acc

    return pl.pallas_call(body, name=name, out_shape=jax.ShapeDtypeStruct((R, C), F32),
                          in_specs=[pl.BlockSpec(memory_space=pltpu.VMEM)],
                          out_specs=pl.BlockSpec(memory_space=pltpu.VMEM))(parts)


def _adamw(name, w, g, m, v):
    R, C = w.shape
    TR = _pick(R, (128, 64, 32, 16, 8))
    c1 = 1.0 - ADAM_B1 ** ADAM_STEP
    c2 = 1.0 - ADAM_B2 ** ADAM_STEP

    def body(w_ref, g_ref, m_ref, v_ref, d_ref, nm_ref, nv_ref):
        g_ = g_ref[...]
        nm = ADAM_B1 * m_ref[...] + (1.0 - ADAM_B1) * g_
        nv = ADAM_B2 * v_ref[...] + (1.0 - ADAM_B2) * (g_ * g_)
        d_ref[...] = -ADAM_LR * ((nm / c1) / (jnp.sqrt(nv / c2) + ADAM_EPS) + ADAM_WD * w_ref[...])
        nm_ref[...] = nm
        nv_ref[...] = nv

    spec = pl.BlockSpec((TR, C), lambda r: (r, 0))
    shp = jax.ShapeDtypeStruct((R, C), F32)
    return pl.pallas_call(body, name=name, grid=(R // TR,), in_specs=[spec] * 4, out_specs=[spec] * 3,
                          out_shape=[shp] * 3,
                          compiler_params=_params(("arbitrary",), [((TR, C), F32)] * 7))(w, g, m, v)


def _place():
    x, y, c = lax.axis_index("x"), lax.axis_index("y"), lax.axis_index("c")
    others = [(1 - x, y), (x, 1 - y), (1 - x, 1 - y)]
    return x, y, c, others


def _allgather_small(name, block):
    R, C = block.shape

    def body(x_ref, out_ref, send_sems, recv_sems, local_sem):
        x, y, c, chips = _place()
        me, sibling = (x, y, c), (x, y, 1 - c)

        def rows(px, py, pc):
            return out_ref.at[4 * px + 2 * py + pc]

        def copy(k, blk, to, src=None):
            return pltpu.make_async_remote_copy(
                src_ref=rows(*blk) if src is None else src, dst_ref=rows(*blk),
                send_sem=send_sems.at[k], recv_sem=recv_sems.at[k], device_id=to, device_id_type=MESH)

        mine = pltpu.make_async_copy(x_ref, rows(*me), local_sem)
        mine.start()
        first = [copy(0, me, sibling, src=x_ref)]
        first += [copy(1 + j, me, (*chip, c), src=x_ref) for j, chip in enumerate(chips)]
        for cp in first:
            cp.start()
        passed = [copy(4 + j, (*chip, c), sibling) for j, chip in enumerate(chips)]
        for j, chip in enumerate(chips):
            copy(1 + j, (*chip, c), me).wait_recv()
            passed[j].start()
        copy(0, sibling, me).wait_recv()
        for j, chip in enumerate(chips):
            copy(4 + j, (*chip, 1 - c), me).wait_recv()
        for cp in first + passed:
            cp.wait_send()
        mine.wait()

    return pl.pallas_call(
        body, name=name, out_shape=jax.ShapeDtypeStruct((N_DEV, R, C), F32),
        in_specs=[pl.BlockSpec(memory_space=pltpu.VMEM)], out_specs=pl.BlockSpec(memory_space=pltpu.VMEM),
        scratch_shapes=[pltpu.SemaphoreType.DMA((7,)), pltpu.SemaphoreType.DMA((7,)), pltpu.SemaphoreType.DMA],
    )(block)


def _allgather_weights(bufs):
    n = len(bufs)

    def body(*refs):
        ins, outs = refs[:n], refs[n:2 * n]
        send_sems, recv_sems = refs[2 * n:]
        x, y, c, chips = _place()
        me, sibling = (x, y, c), (x, y, 1 - c)

        def half_of(ref, chip, half):
            rh = ref.shape[1] // 2
            return ref.at[2 * chip[0] + chip[1], pl.ds(half * rh, rh)]

        def copy(a, k, src, dst, to):
            return pltpu.make_async_remote_copy(
                src_ref=src, dst_ref=dst, send_sem=send_sems.at[6 * a + k], recv_sem=recv_sems.at[6 * a + k],
                device_id=to, device_id_type=MESH)

        first, passed = [], []
        for a in range(n):
            first.append([copy(a, j, half_of(ins[a], (x, y), c), half_of(outs[a], (x, y), c), (*chip, c))
                          for j, chip in enumerate(chips)])
            for cp in first[a]:
                cp.start()
        for a in range(n):
            passed.append([copy(a, 3 + j, half_of(outs[a], chip, c), half_of(outs[a], chip, c), sibling)
                           for j, chip in enumerate(chips)])
            for j, chip in enumerate(chips):
                copy(a, j, half_of(outs[a], chip, c), half_of(outs[a], chip, c), me).wait_recv()
                passed[a][j].start()
        for a in range(n):
            for j, chip in enumerate(chips):
                copy(a, 3 + j, half_of(outs[a], chip, 1 - c), half_of(outs[a], chip, 1 - c), me).wait_recv()
            for cp in first[a] + passed[a]:
                cp.wait_send()

    return pl.pallas_call(
        body, name="allgather_weights",
        out_shape=[jax.ShapeDtypeStruct(b.shape, BF16) for b in bufs],
        in_specs=[ANY] * n, out_specs=[ANY] * n, input_output_aliases={a: a for a in range(n)},
        scratch_shapes=[pltpu.SemaphoreType.DMA((6 * n,)), pltpu.SemaphoreType.DMA((6 * n,))],
    )(*bufs)


def _exchange_pairs(grads):
    n = len(grads)

    def body(*refs):
        ins, outs = refs[:n], refs[n:2 * n]
        send_sems, recv_sems = refs[2 * n:]
        x, y, c, _ = _place()
        copies = [pltpu.make_async_remote_copy(
            src_ref=ins[a].at[:, 1 - c], dst_ref=outs[a], send_sem=send_sems.at[a], recv_sem=recv_sems.at[a],
            device_id=(x, y, 1 - c), device_id_type=MESH) for a in range(n)]
        for cp in copies:
            cp.start()
        for cp in copies:
            cp.wait()

    return pl.pallas_call(
        body, name="grads_exchange_pairs",
        out_shape=[jax.ShapeDtypeStruct((N_CHIPS,) + g.shape[2:], F32) for g in grads],
        in_specs=[ANY] * n, out_specs=[ANY] * n,
        scratch_shapes=[pltpu.SemaphoreType.DMA((n,)), pltpu.SemaphoreType.DMA((n,))],
    )(*grads)


def _exchange_chips(parts):
    n = len(parts)

    def body(*refs):
        ins, outs = refs[:n], refs[n:2 * n]
        send_sems, recv_sems = refs[2 * n:]
        x, y, c, chips = _place()
        sends = []
        for a in range(n):
            for j, chip in enumerate(chips):
                sends.append(pltpu.make_async_remote_copy(
                    src_ref=ins[a].at[2 * chip[0] + chip[1]], dst_ref=outs[a].at[j],
                    send_sem=send_sems.at[3 * a + j], recv_sem=recv_sems.at[3 * a + j],
                    device_id=(*chip, c), device_id_type=MESH))
                sends[-1].start()
        for cp in sends:
            cp.wait_recv()
        for cp in sends:
            cp.wait_send()

    return pl.pallas_call(
        body, name="grads_exchange_chips",
        out_shape=[jax.ShapeDtypeStruct((N_CHIPS - 1,) + p.shape[1:], F32) for p in parts],
        in_specs=[ANY] * n, out_specs=[ANY] * n,
        scratch_shapes=[pltpu.SemaphoreType.DMA((3 * n,)), pltpu.SemaphoreType.DMA((3 * n,))],
    )(*parts)


def _share_halves(halves):
    n = len(halves)

    def body(*refs):
        ins, outs = refs[:n], refs[n:2 * n]
        send_sems, recv_sems = refs[2 * n:]
        x, y, c, _ = _place()
        sends = []
        for a in range(n):
            sends.append(pltpu.make_async_remote_copy(
                src_ref=ins[a].at[c], dst_ref=outs[a].at[c], send_sem=send_sems.at[a], recv_sem=recv_sems.at[a],
                device_id=(x, y, 1 - c), device_id_type=MESH))
            sends[a].start()
        for a in range(n):
            theirs = outs[a].at[1 - c]
            pltpu.make_async_remote_copy(
                src_ref=theirs, dst_ref=theirs, send_sem=send_sems.at[a], recv_sem=recv_sems.at[a],
                device_id=(x, y, c), device_id_type=MESH).wait_recv()
        for cp in sends:
            cp.wait_send()

    return pl.pallas_call(
        body, name="grads_share_halves",
        out_shape=[jax.ShapeDtypeStruct(h.shape, F32) for h in halves],
        in_specs=[ANY] * n, out_specs=[ANY] * n, input_output_aliases={a: a for a in range(n)},
        scratch_shapes=[pltpu.SemaphoreType.DMA((n,)), pltpu.SemaphoreType.DMA((n,))],
    )(*halves)


def _pack(arrays):
    pieces = []
    for a in arrays:
        flat = a.reshape(-1).astype(F32)
        pieces.append(jnp.pad(flat, (0, (-flat.size) % PACK_ALIGN)))
    return jnp.concatenate(pieces).reshape(-1, LANES)


def _unpack(buf, shapes):
    lead = buf.shape[:-2]
    flat = buf.reshape(lead + (-1,))
    out, off = [], 0
    for shp in shapes:
        size = 1
        for s in shp:
            size *= s
        out.append(flat[..., off:off + size].reshape(lead + tuple(shp)))
        off += size + (-size) % PACK_ALIGN
    return out


def _gather_channels(buf, shapes):
    per_chip = _unpack(buf[0::2], shapes)
    return [jnp.transpose(a, (1, 0, 2)).reshape(a.shape[1], -1) for a in per_chip]


def _mm_tile(n, rows, limit_bytes=6 * 1024 * 1024):
    for t in (1408, 1280, 1024, 640, 512, 384, 256, 128):
        if n % t == 0 and rows * t * 2 <= limit_bytes:
            return t
    raise ValueError(f"no column tile for {n} x {rows}")


def kernel(x, p, norm_mix_g, w_in, conv_a_w, conv_a_b, ln_a_g, ln_a_b, conv_b_w, w_out, norm_ffn_g, w_up, conv_ffn_w, w_down, w_ple_gate, b_ple_gate, w_ple_proj, norm_final_g, loss_target, m_norm_mix_g, m_w_in, m_conv_a_w, m_conv_a_b, m_ln_a_g, m_ln_a_b, m_conv_b_w, m_w_out, m_norm_ffn_g, m_w_up, m_conv_ffn_w, m_w_down, m_w_ple_gate, m_b_ple_gate, m_w_ple_proj, m_norm_final_g, v_norm_mix_g, v_w_in, v_conv_a_w, v_conv_a_b, v_ln_a_g, v_ln_a_b, v_conv_b_w, v_w_out, v_norm_ffn_g, v_w_up, v_conv_ffn_w, v_w_down, v_w_ple_gate, v_b_ple_gate, v_w_ple_proj, v_norm_final_g):
    S, D = x.shape[1], x.shape[2]
    P = p.shape[3]
    A = conv_a_b.shape[1]
    F = w_down.shape[1] * N_CHIPS
    KA, KB, KF = conv_a_w.shape[1], conv_b_w.shape[1], conv_ffn_w.shape[1]
    xi, yi, ci = lax.axis_index("x"), lax.axis_index("y"), lax.axis_index("c")
    chip = 2 * xi + yi

    TM = _pick(S, (512, 256, 128))
    TE = _pick(S, (256, 128))
    TC = _pick(F, (1408, 1024, 512, 256, 128))

    x2, p2, t2 = x.reshape(S, D), p.reshape(S, P), loss_target.reshape(S, D)
    gfin = norm_final_g.reshape(1, D)

    big = dict(w_in=w_in[0], w_out=w_out[0], w_up=w_up[0], w_down=w_down[0],
               w_ple_gate=w_ple_gate[0], w_ple_proj=w_ple_proj[0])
    names = list(big)
    gathered = _allgather_weights([_cast_into_gathered("cast_" + n, big[n], chip) for n in names])
    w_in3, w_out3, w_up3, w_down3, w_gate3, w_proj3 = gathered
    w_out_f = w_out3.reshape(2 * A, D)
    w_down_f = w_down3.reshape(F, D)
    w_gate_f = w_gate3.reshape(D, D)

    tap_shapes = [(KA, A // N_CHIPS), (KB, A // N_CHIPS), (KF, 2 * F // N_CHIPS)]
    taps = _allgather_small("allgather_taps", _pack([conv_a_w[0], conv_b_w[0], conv_ffn_w[0]]))
    conv_a_f, conv_b_f, conv_ffn_f = _gather_channels(taps, tap_shapes)

    def rms_prologue(rows, row_r, vec_r, ro_r, ao_r):
        h = row_r[0][rows, :]
        hn = (h * _rms_stats(h) * vec_r[0][...]).astype(BF16)
        ro_r[0][rows, :] = hn
        return [hn]

    def cast_prologue(rows, row_r, vec_r, ro_r, ao_r):
        hb = row_r[0][rows, :].astype(BF16)
        ro_r[0][rows, :] = hb
        return [hb]

    plain = lambda accs, tile_r, cv_r: [accs[0]]
    residual = lambda accs, tile_r, cv_r: [tile_r[0][...] + accs[0]]

    z, hn1 = _rows_mm("in_proj", S, TM, 5 * A, _mm_tile(5 * A // N_CHIPS, D), row_ins=[x2], vec_ins=[norm_mix_g],
                      weights=[(w_in3, "nn3")], tile_outs=[BF16], row_outs=[(D, BF16)],
                      prologue=rms_prologue, epilogue=plain)
    a1, cat = _mixer_fwd(z, conv_a_f, conv_a_b, ln_a_g, ln_a_b, conv_b_f, S, TE, A)
    (h1,) = _rows_mm("out_proj", S, TM, D, _mm_tile(D, 2 * A), row_ins=[cat], weights=[(w_out_f, "nn2")],
                     tile_ins=[x2], tile_outs=[F32], epilogue=residual)
    u0, hn2 = _rows_mm("up_proj", S, TM, 2 * F, _mm_tile(2 * F // N_CHIPS, D), row_ins=[h1], vec_ins=[norm_ffn_g],
                       weights=[(w_up3, "nn3")], tile_outs=[BF16], row_outs=[(D, BF16)],
                       prologue=rms_prologue, epilogue=plain)
    act = _ffn_act(u0, conv_ffn_f, S, TM, F, TC)
    (h2,) = _rows_mm("down_proj", S, TM, D, _mm_tile(D, F), row_ins=[act], weights=[(w_down_f, "nn2")],
                     tile_ins=[h1], tile_outs=[F32], epilogue=residual)

    def ple_prologue(rows, row_r, vec_r, ro_r, ao_r):
        hb = row_r[0][rows, :].astype(BF16)
        pb = row_r[1][rows, :].astype(BF16)
        ro_r[0][rows, :] = hb
        ro_r[1][rows, :] = pb
        return [hb, pb]

    def ple_epilogue(accs, tile_r, cv_r):
        gate = _sigmoid(accs[0] + cv_r[0][...])
        return [tile_r[0][...] + accs[1] * gate, gate, accs[1]]

    h3, gate, pp, h2b, pb = _rows_mm(
        "ple_fwd", S, TM, D, _mm_tile(D // N_CHIPS, D), row_ins=[h2, p2], colvec_ins=[b_ple_gate],
        weights=[(w_gate_f, "nn2"), (w_proj3, "nn3")], tile_ins=[h2], tile_outs=[F32, BF16, BF16],
        row_outs=[(D, BF16), (P, BF16)], prologue=ple_prologue, epilogue=ple_epilogue)
    loss_part, g_norm_final, dh3 = _loss_head(h3, t2, gfin, S, TE, D)

    def ple_bwd_prologue(rows, row_r, vec_r, ro_r, ao_r):
        d = row_r[0][rows, :]
        gt = row_r[2][rows, :].astype(F32)
        dpre = d * row_r[1][rows, :].astype(F32) * gt * (1.0 - gt)
        ro_r[0][rows, :] = dpre.astype(BF16)
        ro_r[1][rows, :] = (d * gt).astype(BF16)
        ao_r[0][...] += _rsum(dpre)
        return [dpre.astype(BF16)]

    dh2, dpre, dpp, g_b_gate = _rows_mm(
        "ple_bwd", S, TM, D, _mm_tile(D, D), row_ins=[dh3, pp, gate], weights=[(w_gate_f, "nt2")],
        tile_ins=[dh3], tile_outs=[F32], row_outs=[(D, BF16), (D, BF16)], acc_outs=[(1, D)],
        prologue=ple_bwd_prologue, epilogue=residual)
    dact, dh2b = _rows_mm("down_bwd", S, TM, F, _mm_tile(F, D), row_ins=[dh2], weights=[(w_down_f, "nt2")],
                          tile_outs=[BF16], row_outs=[(D, BF16)], prologue=cast_prologue, epilogue=plain)
    du = _ffn_bwd_act(u0, dact, conv_ffn_f, S, TM, F, TC)
    du0, g_conv_ffn = _ffn_bwd_conv(u0, du, conv_ffn_f, S, TM, TC)

    def up_bwd_epilogue(acc, rows, row_r, vec_r, ro_r, ao_r):
        dh, dg = _rms_bwd(row_r[0][rows, :], vec_r[0][...], acc)
        dh1_ = row_r[1][rows, :] + dh
        ro_r[0][rows, :] = dh1_
        ro_r[1][rows, :] = dh1_.astype(BF16)
        ao_r[0][...] += dg

    dh1, dh1b, g_norm_ffn = _kloop_mm(
        "up_bwd", S, TM, du0, w_up3, _mm_tile(2 * F // N_CHIPS, D), row_ins=[h1, dh2],
        vec_ins=[norm_ffn_g], row_outs=[(D, F32), (D, BF16)], acc_outs=[(1, D)], epilogue=up_bwd_epilogue)
    (dcat,) = _rows_mm("out_bwd", S, TM, 2 * A, _mm_tile(2 * A, D), row_ins=[dh1b], weights=[(w_out_f, "nt2")],
                       tile_outs=[BF16], epilogue=plain)
    da1, ln_sums = _mixer_bwd_ln(dcat, a1, ln_a_g, ln_a_b, S, TE, A)
    dz, g_conv_a, g_conv_b = _mixer_bwd_conv(z, dcat, da1, conv_a_f, conv_b_f, S, TE, A)

    def in_bwd_epilogue(acc, rows, row_r, vec_r, ro_r, ao_r):
        dh, dg = _rms_bwd(row_r[0][rows, :], vec_r[0][...], acc)
        ro_r[0][rows, :] = row_r[1][rows, :] + dh
        ao_r[0][...] += dg

    dx, g_norm_mix = _kloop_mm(
        "in_bwd", S, TM, dz, w_in3, _mm_tile(5 * A // N_CHIPS, D), row_ins=[x2, dh1],
        vec_ins=[norm_mix_g], row_outs=[(D, F32)], acc_outs=[(1, D)], epilogue=in_bwd_epilogue)

    TK = _pick(S, (1024, 512, 256, 128))
    wt = lambda n: _pick(n, (1408, 1280, 1024, 512, 256, 128))
    part = dict(
        w_in=_tn_mm("dw_in", hn1, dz, wt(D), wt(5 * A // N_CHIPS), TK, cols_per_chip=5 * A // N_CHIPS),
        w_out=_tn_mm("dw_out", cat, dh1b, wt(2 * A), wt(D), TK),
        w_up=_tn_mm("dw_up", hn2, du0, wt(D), wt(2 * F // N_CHIPS), TK, cols_per_chip=2 * F // N_CHIPS),
        w_down=_tn_mm("dw_down", act, dh2b, wt(F), wt(D), TK),
        w_ple_gate=_tn_mm("dw_ple_gate", h2b, dpre, wt(D), wt(D), TK),
        w_ple_proj=_tn_mm("dw_ple_proj", pb, dpp, wt(P), wt(D // N_CHIPS), TK, cols_per_chip=D // N_CHIPS),
    )

    def halves(n):
        R, C = big[n].shape
        return part[n].reshape(N_CHIPS, 2, R // 2, C)

    from_sibling = _exchange_pairs([halves(n) for n in names])
    chip_sums = [_add_pair("pair_sum_" + n, halves(n), r, ci) for n, r in zip(names, from_sibling)]
    from_chips = _exchange_chips(chip_sums)
    reduced = _share_halves([_add_chips("chip_sum_" + n, s, r, chip, ci)
                             for n, s, r in zip(names, chip_sums, from_chips)])
    moments = dict(w_in=(m_w_in, v_w_in), w_out=(m_w_out, v_w_out), w_up=(m_w_up, v_w_up),
                   w_down=(m_w_down, v_w_down), w_ple_gate=(m_w_ple_gate, v_w_ple_gate),
                   w_ple_proj=(m_w_ple_proj, v_w_ple_proj))
    grads, deltas, new_m, new_v = {}, {}, {}, {}
    for n, g in zip(names, reduced):
        g = g.reshape(big[n].shape)
        d_, m_, v_ = _adamw("adamw_" + n, big[n], g, moments[n][0][0], moments[n][1][0])
        grads[n], deltas[n], new_m[n], new_v[n] = g[None], d_[None], m_[None], v_[None]

    small = ["norm_mix_g", "conv_a_w", "conv_a_b", "ln_a_g", "ln_a_b", "conv_b_w", "norm_ffn_g",
             "conv_ffn_w", "b_ple_gate", "norm_final_g"]
    small_part = [g_norm_mix, g_conv_a, ln_sums[2:3], ln_sums[0:1], ln_sums[1:2], g_conv_b, g_norm_ffn,
                  g_conv_ffn, g_b_gate, g_norm_final]
    full_shapes = [a.shape for a in small_part]
    summed = _sum_devices("small_grads_sum", _allgather_small("allgather_small_grads", _pack(small_part)))
    small_g = dict(zip(small, _unpack(summed, full_shapes)))
    for n, width in (("conv_a_w", A), ("conv_b_w", A), ("conv_ffn_w", 2 * F)):
        small_g[n] = lax.dynamic_slice_in_dim(small_g[n], chip * (width // N_CHIPS), width // N_CHIPS, axis=1)
    small_w = dict(norm_mix_g=(norm_mix_g, m_norm_mix_g, v_norm_mix_g), conv_a_w=(conv_a_w, m_conv_a_w, v_conv_a_w),
                   conv_a_b=(conv_a_b, m_conv_a_b, v_conv_a_b), ln_a_g=(ln_a_g, m_ln_a_g, v_ln_a_g),
                   ln_a_b=(ln_a_b, m_ln_a_b, v_ln_a_b), conv_b_w=(conv_b_w, m_conv_b_w, v_conv_b_w),
                   norm_ffn_g=(norm_ffn_g, m_norm_ffn_g, v_norm_ffn_g),
                   conv_ffn_w=(conv_ffn_w, m_conv_ffn_w, v_conv_ffn_w),
                   b_ple_gate=(b_ple_gate, m_b_ple_gate, v_b_ple_gate),
                   norm_final_g=(norm_final_g, m_norm_final_g, v_norm_final_g))
    out_shapes = [small_w[n][0].shape for n in small]
    packed_g = _pack([small_g[n] for n in small])
    packed = [_pack([small_w[n][k] for n in small]) for k in range(3)]
    d_s, m_s, v_s = _adamw("adamw_small", packed[0], packed_g, packed[1], packed[2])
    for n, g, d_, m_, v_ in zip(small, _unpack(packed_g, out_shapes), _unpack(d_s, out_shapes),
                                _unpack(m_s, out_shapes), _unpack(v_s, out_shapes)):
        grads[n], deltas[n], new_m[n], new_v[n] = g, d_, m_, v_

    order = ["norm_mix_g", "w_in", "conv_a_w", "conv_a_b", "ln_a_g", "ln_a_b", "conv_b_w", "w_out", "norm_ffn_g",
             "w_up", "conv_ffn_w", "w_down", "w_ple_gate", "b_ple_gate", "w_ple_proj", "norm_final_g"]
    loss = lax.psum(loss_part[0, 0], ("x", "y", "c"))
    return (loss, dx.reshape(x.shape), *[grads[n] for n in order], *[deltas[n] for n in order],
            *[new_m[n] for n in order], *[new_v[n] for n in order])
```

```python
import functools

import jax
import jax.numpy as jnp
from jax import lax
from jax.experimental import pallas as pl
from jax.experimental.pallas import tpu as pltpu

F32 = jnp.float32
BF16 = jnp.bfloat16
MESH = pl.DeviceIdType.MESH
ANY = pl.BlockSpec(memory_space=pl.ANY)

EPS = 1e-6
ADAM_LR = 0.001
ADAM_B1 = 0.9
ADAM_B2 = 0.999
ADAM_EPS = 1e-08
ADAM_WD = 0.01
ADAM_STEP = 10

N_CHIPS = 4
N_DEV = 8
LANES = 128
SUBLANES = 8
PACK_ALIGN = LANES * SUBLANES
ROW_CHUNK = 32
VMEM_CAP = 60 * 1024 * 1024
VMEM_SLACK = 6 * 1024 * 1024


def _pick(n, cands):
    for c in cands:
        if n % c == 0:
            return c
    raise ValueError(f"no tile of {cands} divides {n}")


def _nbytes(shape, dtype):
    n = 1
    for s in shape:
        if s is not None:
            n *= s
    return n * jnp.dtype(dtype).itemsize


def _params(sem, blocks, scratch=(), temps=()):
    est = (2 * sum(_nbytes(s, d) for s, d in blocks) + sum(_nbytes(s, d) for s, d in scratch)
           + sum(_nbytes(s, d) for s, d in temps))
    return pltpu.CompilerParams(dimension_semantics=sem,
                                vmem_limit_bytes=min(est + VMEM_SLACK, VMEM_CAP))


def _sigmoid(x):
    return 1.0 / (1.0 + jnp.exp(-x))


def _rsum(x):
    return jnp.sum(x, axis=0, keepdims=True)


def _rms_stats(x):
    return lax.rsqrt(jnp.mean(x * x, axis=-1, keepdims=True) + EPS)


def _rms_bwd(h, g, dout):
    r = _rms_stats(h)
    n = h * r
    dn = dout * g
    dh = r * (dn - n * jnp.mean(dn * n, axis=-1, keepdims=True))
    return dh, _rsum(dout * n)


def _identity(t):
    return t


def _chip_major(nb, place=_identity):
    return lambda i, j: (place(j) // nb, 0, place(j) % nb)


def _rows_mm(name, S, TM, N, TN, *, row_ins, vec_ins=(), colvec_ins=(), weights, tile_ins=(),
             tile_outs, row_outs=(), acc_outs=(), prologue=None, epilogue, place=_identity):
    nI, nJ = S // TM, N // TN
    n_row, n_vec, n_cv, n_w, n_tile = len(row_ins), len(vec_ins), len(colvec_ins), len(weights), len(tile_ins)
    n_to, n_ro, n_ao = len(tile_outs), len(row_outs), len(acc_outs)

    in_specs, blocks, scratch, ks = [], [], [], []
    for a in row_ins:
        in_specs.append(pl.BlockSpec((TM, a.shape[1]), lambda i, j: (i, 0)))
        blocks.append(((TM, a.shape[1]), a.dtype))
    for a in vec_ins:
        in_specs.append(pl.BlockSpec(a.shape, lambda i, j: (0, 0)))
        blocks.append((a.shape, a.dtype))
    for a in colvec_ins:
        in_specs.append(pl.BlockSpec((1, TN), lambda i, j: (0, j)))
        blocks.append(((1, TN), a.dtype))
    for w, mode in weights:
        if mode == "nn2":
            k = w.shape[0]
            in_specs.append(pl.BlockSpec((k, TN), lambda i, j: (0, j)))
        elif mode == "nn3":
            k = w.shape[1]
            in_specs.append(pl.BlockSpec((None, k, TN), _chip_major(w.shape[2] // TN, place)))
        else:
            k = w.shape[1]
            in_specs.append(pl.BlockSpec((TN, k), lambda i, j: (j, 0)))
        ks.append(k)
        blocks.append(((k, TN), BF16))
        if prologue is not None:
            scratch.append(((TM, k), BF16))
    for a in tile_ins:
        in_specs.append(pl.BlockSpec((TM, TN), lambda i, j: (i, j)))
        blocks.append(((TM, TN), a.dtype))

    out_shape, out_specs = [], []
    for dt in tile_outs:
        out_shape.append(jax.ShapeDtypeStruct((S, N), dt))
        out_specs.append(pl.BlockSpec((TM, TN), lambda i, j: (i, j)))
        blocks.append(((TM, TN), dt))
    for width, dt in row_outs:
        out_shape.append(jax.ShapeDtypeStruct((S, width), dt))
        out_specs.append(pl.BlockSpec((TM, width), lambda i, j: (i, 0)))
        blocks.append(((TM, width), dt))
    for rows, width in acc_outs:
        out_shape.append(jax.ShapeDtypeStruct((rows, width), F32))
        out_specs.append(pl.BlockSpec((rows, width), lambda i, j: (0, 0)))
        blocks.append(((rows, width), F32))

    modes = [m for _, m in weights]

    def body(*refs):
        pos = 0
        def take(n):
            nonlocal pos
            out = refs[pos:pos + n]
            pos += n
            return out
        row_r, vec_r, cv_r, w_r, tile_r = take(n_row), take(n_vec), take(n_cv), take(n_w), take(n_tile)
        to_r, ro_r, ao_r, a_sc = take(n_to), take(n_ro), take(n_ao), take(len(scratch))
        i, j = pl.program_id(0), pl.program_id(1)

        if prologue is None:
            a_sc = row_r[:n_w]
        else:
            @pl.when(j == 0)
            def _():
                if n_ao:
                    @pl.when(i == 0)
                    def _():
                        for r in ao_r:
                            r[...] = jnp.zeros_like(r)

                def chunk(ci, carry):
                    rows = pl.ds(pl.multiple_of(ci * ROW_CHUNK, ROW_CHUNK), ROW_CHUNK)
                    for sc, a in zip(a_sc, prologue(rows, row_r, vec_r, ro_r, ao_r)):
                        sc[rows, :] = a
                    return carry
                lax.fori_loop(0, TM // ROW_CHUNK, chunk, 0)

        accs = []
        for w_ref, sc, mode in zip(w_r, a_sc, modes):
            if mode == "nt2":
                accs.append(lax.dot_general(sc[...], w_ref[...], (((1,), (1,)), ((), ())),
                                            preferred_element_type=F32))
            else:
                accs.append(jnp.dot(sc[...], w_ref[...], preferred_element_type=F32))
        outs = epilogue(accs, tile_r, cv_r)
        for r, o in zip(to_r, outs):
            r[...] = o.astype(r.dtype)

    return pl.pallas_call(
        body, name=name, grid=(nI, nJ), in_specs=in_specs, out_specs=out_specs, out_shape=out_shape,
        scratch_shapes=[pltpu.VMEM(s, d) for s, d in scratch],
        compiler_params=_params(("arbitrary", "arbitrary"), blocks, scratch, temps=[((TM, TN), F32)] * 3),
    )(*row_ins, *vec_ins, *colvec_ins, *[w for w, _ in weights], *tile_ins)


def _kloop_mm(name, S, TM, a, w3, TK, *, row_ins, vec_ins, row_outs, acc_outs, epilogue, place=_identity):
    _, N, Ks = w3.shape
    nb = Ks // TK
    nK = N_CHIPS * nb
    n_row, n_vec, n_ro, n_ao = len(row_ins), len(vec_ins), len(row_outs), len(acc_outs)

    in_specs = [pl.BlockSpec((TM, TK), lambda i, k: (i, k)),
                pl.BlockSpec((None, N, TK), _chip_major(nb, place))]
    blocks = [((TM, TK), BF16), ((N, TK), BF16)]
    for r in row_ins:
        in_specs.append(pl.BlockSpec((TM, r.shape[1]), lambda i, k: (i, 0)))
        blocks.append(((TM, r.shape[1]), r.dtype))
    for v in vec_ins:
        in_specs.append(pl.BlockSpec(v.shape, lambda i, k: (0, 0)))
        blocks.append((v.shape, v.dtype))
    out_shape, out_specs = [], []
    for width, dt in row_outs:
        out_shape.append(jax.ShapeDtypeStruct((S, width), dt))
        out_specs.append(pl.BlockSpec((TM, width), lambda i, k: (i, 0)))
        blocks.append(((TM, width), dt))
    for rows, width in acc_outs:
        out_shape.append(jax.ShapeDtypeStruct((rows, width), F32))
        out_specs.append(pl.BlockSpec((rows, width), lambda i, k: (0, 0)))
        blocks.append(((rows, width), F32))
    scratch = [((TM, N), F32)]

    def body(*refs):
        a_ref, w_ref = refs[0], refs[1]
        row_r = refs[2:2 + n_row]
        vec_r = refs[2 + n_row:2 + n_row + n_vec]
        pos = 2 + n_row + n_vec
        ro_r = refs[pos:pos + n_ro]
        ao_r = refs[pos + n_ro:pos + n_ro + n_ao]
        acc_sc = refs[pos + n_ro + n_ao]
        i, k = pl.program_id(0), pl.program_id(1)
        d = lax.dot_general(a_ref[...], w_ref[...], (((1,), (1,)), ((), ())), preferred_element_type=F32)

        @pl.when(k == 0)
        def _():
            acc_sc[...] = d

        @pl.when(k > 0)
        def _():
            acc_sc[...] += d

        @pl.when(k == nK - 1)
        def _():
            @pl.when(i == 0)
            def _():
                for r in ao_r:
                    r[...] = jnp.zeros_like(r)

            def chunk(ci, carry):
                rows = pl.ds(pl.multiple_of(ci * ROW_CHUNK, ROW_CHUNK), ROW_CHUNK)
                epilogue(acc_sc[rows, :], rows, row_r, vec_r, ro_r, ao_r)
                return carry
            lax.fori_loop(0, TM // ROW_CHUNK, chunk, 0)

    return pl.pallas_call(
        body, name=name, grid=(S // TM, nK), in_specs=in_specs, out_specs=out_specs, out_shape=out_shape,
        scratch_shapes=[pltpu.VMEM(s, d) for s, d in scratch],
        compiler_params=_params(("arbitrary", "arbitrary"), blocks, scratch, temps=[((TM, N), F32)]),
    )(a, w3, *row_ins, *vec_ins)


def _tn_mm(name, a, b, TMw, TNw, TK, cols_per_chip=None, place=_identity):
    S, M = a.shape
    N = b.shape[1]
    nK = S // TK
    if cols_per_chip is None:
        out_shape = jax.ShapeDtypeStruct((M, N), F32)
        out_spec = pl.BlockSpec((TMw, TNw), lambda i, j, k: (i, j))
    else:
        nb = cols_per_chip // TNw
        out_shape = jax.ShapeDtypeStruct((N_CHIPS, M, cols_per_chip), F32)
        out_spec = pl.BlockSpec((None, TMw, TNw), lambda i, j, k: (place(j) // nb, i, place(j) % nb))

    def body(a_ref, b_ref, o_ref):
        k = pl.program_id(2)
        d = lax.dot_general(a_ref[...], b_ref[...], (((0,), (0,)), ((), ())), preferred_element_type=F32)

        @pl.when(k == 0)
        def _():
            o_ref[...] = d

        @pl.when(k > 0)
        def _():
            o_ref[...] += d

    blocks = [((TK, TMw), BF16), ((TK, TNw), BF16), ((TMw, TNw), F32)]
    return pl.pallas_call(
        body, name=name, grid=(M // TMw, N // TNw, nK),
        in_specs=[pl.BlockSpec((TK, TMw), lambda i, j, k: (k, i)),
                  pl.BlockSpec((TK, TNw), lambda i, j, k: (k, j))],
        out_specs=out_spec, out_shape=out_shape,
        compiler_params=_params(("arbitrary", "arbitrary", "arbitrary"), blocks,
                                temps=[((TMw, TNw), F32), ((TK, TMw), BF16)]),
    )(a, b)


def _prev_rows(TM, H, col):
    return lambda i: (jnp.maximum(i * (TM // H) - 1, 0), col)


def _next_rows(S, TM, H, col):
    return lambda i: (jnp.minimum((i + 1) * (TM // H), S // H - 1), col)


def _taps_causal(ext_ref, w_ref, K, H, TM, cs):
    acc = None
    for k in range(K):
        term = ext_ref[pl.ds(H - (K - 1) + k, TM), cs] * w_ref[pl.ds(k, 1), cs]
        acc = term if acc is None else acc + term
    return acc


def _taps_anticausal(ext_ref, w_ref, K, TM, cs):
    acc = None
    for k in range(K):
        term = ext_ref[pl.ds(K - 1 - k, TM), cs] * w_ref[pl.ds(k, 1), cs]
        acc = term if acc is None else acc + term
    return acc


def _tap_grads(ext_ref, g, K, H, TM, cs):
    return [_rsum(ext_ref[pl.ds(H - (K - 1) + k, TM), cs] * g) for k in range(K)]


def _mixer_fwd(z, conv_a_w, conv_a_b, ln_g, ln_b, conv_b_w, S, TM, A):
    H = 32
    KA, KB = conv_a_w.shape[0], conv_b_w.shape[0]
    n_chunks = A // LANES
    RB = _pick(TM, (64, 32))

    def body(zc_ref, zh_ref, wa_ref, ba_ref, g_ref, b_ref, wb_ref, a1_ref, cat_ref, ext_a, ext_b):
        i = pl.program_id(0)
        live = (i > 0).astype(F32)
        zc = zc_ref[...].astype(F32)
        zh = zh_ref[...].astype(F32) * live
        ext_a[pl.ds(0, H), :] = zh[:, 0:A] * _sigmoid(zh[:, A:2 * A])
        ext_a[pl.ds(H, TM), :] = zc[:, 0:A] * _sigmoid(zc[:, A:2 * A])
        ext_b[pl.ds(0, H), :] = zh[:, 3 * A:4 * A] * zh[:, 4 * A:5 * A]
        ext_b[pl.ds(H, TM), :] = zc[:, 3 * A:4 * A] * zc[:, 4 * A:5 * A]

        def chunk(c, carry):
            cs = pl.ds(pl.multiple_of(c * LANES, LANES), LANES)
            for r0 in range(0, TM, RB):
                acc = None
                for k in range(KA):
                    term = ext_a[pl.ds(H - (KA - 1) + k + r0, RB), cs] * wa_ref[pl.ds(k, 1), cs]
                    acc = term if acc is None else acc + term
                a1_ref[pl.ds(r0, RB), cs] = acc + ba_ref[:, cs]
            return carry
        lax.fori_loop(0, n_chunks, chunk, 0)

        a1 = a1_ref[...]
        mu = jnp.mean(a1, axis=-1, keepdims=True)
        d = a1 - mu
        var = jnp.mean(d * d, axis=-1, keepdims=True)
        a2 = d * lax.rsqrt(var + EPS) * g_ref[...] + b_ref[...]
        cat_ref[:, 0:A] = (a2 * _sigmoid(a2)).astype(BF16)
        cbc = _taps_causal(ext_b, wb_ref, KB, H, TM, slice(None))
        cat_ref[:, A:2 * A] = (zc[:, 2 * A:3 * A] * cbc).astype(BF16)

    blocks = [((TM, 5 * A), BF16), ((H, 5 * A), BF16), ((KA, A), F32), ((KB, A), F32),
              ((TM, A), F32), ((TM, 2 * A), BF16)]
    scratch = [((H + TM, A), F32), ((H + TM, A), F32)]
    vec = lambda r: pl.BlockSpec((r, A), lambda i: (0, 0))
    return pl.pallas_call(
        body, name="mixer_fwd", grid=(S // TM,),
        in_specs=[pl.BlockSpec((TM, 5 * A), lambda i: (i, 0)),
                  pl.BlockSpec((H, 5 * A), _prev_rows(TM, H, 0)),
                  vec(KA), vec(1), vec(1), vec(1), vec(KB)],
        out_specs=[pl.BlockSpec((TM, A), lambda i: (i, 0)), pl.BlockSpec((TM, 2 * A), lambda i: (i, 0))],
        out_shape=[jax.ShapeDtypeStruct((S, A), F32), jax.ShapeDtypeStruct((S, 2 * A), BF16)],
        scratch_shapes=[pltpu.VMEM(s, d) for s, d in scratch],
        compiler_params=_params(("arbitrary",), blocks, scratch,
                                temps=[((TM, 5 * A), F32)] * 2 + [((TM, A), F32)] * 10),
    )(z, z, conv_a_w, conv_a_b, ln_g, ln_b, conv_b_w)


def _pair_tile(nF):
    return lambda t: (t % 2) * nF + t // 2


FFN_ROWS = 16


def _bcast_taps(w_ref, K, lanes):
    return [jnp.broadcast_to(w_ref[pl.ds(k, 1), lanes], (FFN_ROWS, LANES)) for k in range(K)]


def _ffn_act(u0, conv_w, S, TM, F, TC):
    H = 16
    K = conv_w.shape[0]
    nF = F // TC

    def body(uc_ref, uh_ref, wg_ref, wu_ref, o_ref, ext):
        live = (pl.program_id(0) > 0).astype(F32)
        ext[pl.ds(0, H), :] = uh_ref[...].astype(F32) * live
        ext[pl.ds(H, TM), :] = uc_ref[...].astype(F32)

        def lane_chunk(c, carry):
            lo = pl.ds(pl.multiple_of(c * LANES, LANES), LANES)
            lg, lu = lo, pl.ds(pl.multiple_of(TC + c * LANES, LANES), LANES)
            wg, wu = _bcast_taps(wg_ref, K, lo), _bcast_taps(wu_ref, K, lo)
            for r0 in range(0, TM, FFN_ROWS):
                g = u = None
                for k in range(K):
                    rows = pl.ds(H - (K - 1) + k + r0, FFN_ROWS)
                    tg, tu = ext[rows, lg] * wg[k], ext[rows, lu] * wu[k]
                    g, u = (tg, tu) if g is None else (g + tg, u + tu)
                o_ref[pl.ds(r0, FFN_ROWS), lo] = (g * _sigmoid(g) * u).astype(BF16)
            return carry
        lax.fori_loop(0, TC // LANES, lane_chunk, 0)

    blocks = [((TM, 2 * TC), BF16), ((H, 2 * TC), BF16), ((K, TC), F32), ((K, TC), F32), ((TM, TC), BF16)]
    scratch = [((H + TM, 2 * TC), F32)]
    return pl.pallas_call(
        body, name="ffn_act", grid=(S // TM, nF),
        in_specs=[pl.BlockSpec((TM, 2 * TC), lambda i, j: (i, j)),
                  pl.BlockSpec((H, 2 * TC), lambda i, j: (jnp.maximum(i * (TM // H) - 1, 0), j)),
                  pl.BlockSpec((K, TC), lambda i, j: (0, j)),
                  pl.BlockSpec((K, TC), lambda i, j: (0, j + nF))],
        out_specs=pl.BlockSpec((TM, TC), lambda i, j: (i, j)),
        out_shape=jax.ShapeDtypeStruct((S, F), BF16),
        scratch_shapes=[pltpu.VMEM(s, d) for s, d in scratch],
        compiler_params=_params(("arbitrary", "arbitrary"), blocks, scratch, temps=[((TM, 2 * TC), F32)]),
    )(u0, u0, conv_w, conv_w)


def _loss_head(h3, target, g_final, S, TM, D):
    def body(h_ref, t_ref, g_ref, loss_ref, dg_ref, dh_ref):
        @pl.when(pl.program_id(0) == 0)
        def _():
            loss_ref[...] = jnp.zeros_like(loss_ref)
            dg_ref[...] = jnp.zeros_like(dg_ref)
        h = h_ref[...]
        g = g_ref[...]
        r = _rms_stats(h)
        n = h * r
        diff = n * g - t_ref[...]
        loss_ref[...] += 0.5 * jnp.sum(jnp.mean(diff * diff, axis=-1, keepdims=True), axis=0, keepdims=True)
        dy = diff * (1.0 / D)
        dn = dy * g
        dh_ref[...] = r * (dn - n * jnp.mean(dn * n, axis=-1, keepdims=True))
        dg_ref[...] += _rsum(dy * n)

    blocks = [((TM, D), F32)] * 3 + [((1, D), F32)] * 2
    row = pl.BlockSpec((TM, D), lambda i: (i, 0))
    return pl.pallas_call(
        body, name="loss_head", grid=(S // TM,),
        in_specs=[row, row, pl.BlockSpec((1, D), lambda i: (0, 0))],
        out_specs=[pl.BlockSpec((1, 1), lambda i: (0, 0)), pl.BlockSpec((1, D), lambda i: (0, 0)), row],
        out_shape=[jax.ShapeDtypeStruct((1, 1), F32), jax.ShapeDtypeStruct((1, D), F32),
                   jax.ShapeDtypeStruct((S, D), F32)],
        compiler_params=_params(("arbitrary",), blocks, temps=[((TM, D), F32)] * 8),
    )(h3, target, g_final)


def _ffn_bwd(u0, dact, conv_w, S, TM, F, TC):
    H = FFN_ROWS
    K = conv_w.shape[0]
    nF, nI = F // TC, S // TM

    def body(up_ref, uc_ref, un_ref, dc_ref, dn_ref, wg_ref, wu_ref, o_ref, dwg_ref, dwu_ref,
             ext_u, ext_d, ext_a):
        i = pl.program_id(1)
        @pl.when(i == 0)
        def _():
            dwg_ref[...] = jnp.zeros_like(dwg_ref)
            dwu_ref[...] = jnp.zeros_like(dwu_ref)
        last = (i < nI - 1).astype(F32)
        ext_u[pl.ds(0, H), :] = up_ref[...].astype(F32) * (i > 0).astype(F32)
        ext_u[pl.ds(H, TM), :] = uc_ref[...].astype(F32)
        ext_u[pl.ds(H + TM, H), :] = un_ref[...].astype(F32) * last
        ext_a[pl.ds(0, TM), :] = dc_ref[...].astype(F32)
        ext_a[pl.ds(TM, H), :] = dn_ref[...].astype(F32) * last

        def lane_chunk(c, carry):
            lo = pl.ds(pl.multiple_of(c * LANES, LANES), LANES)
            lg, lu = lo, pl.ds(pl.multiple_of(TC + c * LANES, LANES), LANES)
            wg, wu = _bcast_taps(wg_ref, K, lo), _bcast_taps(wu_ref, K, lo)
            sums_g, sums_u = [None] * K, [None] * K
            for r0 in range(0, TM + H, FFN_ROWS):
                xg = [ext_u[pl.ds(H - (K - 1) + k + r0, FFN_ROWS), lg] for k in range(K)]
                xu = [ext_u[pl.ds(H - (K - 1) + k + r0, FFN_ROWS), lu] for k in range(K)]
                g, u = xg[0] * wg[0], xu[0] * wu[0]
                for k in range(1, K):
                    g, u = g + xg[k] * wg[k], u + xu[k] * wu[k]
                da = ext_a[pl.ds(r0, FFN_ROWS), lo]
                s = _sigmoid(g)
                dg = da * u * s * (1.0 + g * (1.0 - s))
                du = da * g * s
                ext_d[pl.ds(r0, FFN_ROWS), lg] = dg
                ext_d[pl.ds(r0, FFN_ROWS), lu] = du
                if r0 < TM:
                    for k in range(K):
                        tg, tu = xg[k] * dg, xu[k] * du
                        sums_g[k] = tg if sums_g[k] is None else sums_g[k] + tg
                        sums_u[k] = tu if sums_u[k] is None else sums_u[k] + tu
            for k in range(K):
                dwg_ref[pl.ds(k, 1), lo] += _rsum(sums_g[k])
                dwu_ref[pl.ds(k, 1), lo] += _rsum(sums_u[k])
            for r0 in range(0, TM, FFN_ROWS):
                g = u = None
                for k in range(K):
                    rows = pl.ds(K - 1 - k + r0, FFN_ROWS)
                    tg, tu = ext_d[rows, lg] * wg[k], ext_d[rows, lu] * wu[k]
                    g, u = (tg, tu) if g is None else (g + tg, u + tu)
                o_ref[pl.ds(r0, FFN_ROWS), lg] = g.astype(BF16)
                o_ref[pl.ds(r0, FFN_ROWS), lu] = u.astype(BF16)
            return carry
        lax.fori_loop(0, TC // LANES, lane_chunk, 0)

    blocks = [((H, 2 * TC), BF16), ((TM, 2 * TC), BF16), ((H, 2 * TC), BF16), ((TM, TC), BF16), ((H, TC), BF16),
              ((K, TC), F32), ((K, TC), F32), ((TM, 2 * TC), BF16), ((K, TC), F32), ((K, TC), F32)]
    scratch = [((TM + 2 * H, 2 * TC), F32), ((TM + H, 2 * TC), F32), ((TM + H, TC), F32)]
    prev = lambda j, i: (jnp.maximum(i * (TM // H) - 1, 0), j)
    nxt = lambda j, i: (jnp.minimum((i + 1) * (TM // H), S // H - 1), j)
    taps_out = pl.BlockSpec((K, TC), lambda j, i: (0, j))
    return pl.pallas_call(
        body, name="ffn_bwd", grid=(nF, nI),
        in_specs=[pl.BlockSpec((H, 2 * TC), prev), pl.BlockSpec((TM, 2 * TC), lambda j, i: (i, j)),
                  pl.BlockSpec((H, 2 * TC), nxt),
                  pl.BlockSpec((TM, TC), lambda j, i: (i, j)), pl.BlockSpec((H, TC), nxt),
                  pl.BlockSpec((K, TC), lambda j, i: (0, j)), pl.BlockSpec((K, TC), lambda j, i: (0, j + nF))],
        out_specs=[pl.BlockSpec((TM, 2 * TC), lambda j, i: (i, j)), taps_out, taps_out],
        out_shape=[jax.ShapeDtypeStruct((S, 2 * F), BF16), jax.ShapeDtypeStruct((K, F), F32),
                   jax.ShapeDtypeStruct((K, F), F32)],
        scratch_shapes=[pltpu.VMEM(s, d) for s, d in scratch],
        compiler_params=_params(("arbitrary", "arbitrary"), blocks, scratch, temps=[((TM, 2 * TC), F32)]),
    )(u0, u0, u0, dact, dact, conv_w, conv_w)


def _mixer_bwd_ln(dcat, a1, ln_g, ln_b, S, TM, A):
    def body(dc_ref, a1_ref, g_ref, b_ref, da1_ref, acc_ref):
        @pl.when(pl.program_id(0) == 0)
        def _():
            acc_ref[...] = jnp.zeros_like(acc_ref)
        a1 = a1_ref[...]
        g = g_ref[...]
        mu = jnp.mean(a1, axis=-1, keepdims=True)
        d = a1 - mu
        rstd = lax.rsqrt(jnp.mean(d * d, axis=-1, keepdims=True) + EPS)
        nh = d * rstd
        a2 = nh * g + b_ref[...]
        s = _sigmoid(a2)
        da2 = dc_ref[...].astype(F32) * s * (1.0 + a2 * (1.0 - s))
        dnh = da2 * g
        da1 = rstd * (dnh - jnp.mean(dnh, axis=-1, keepdims=True)
                      - nh * jnp.mean(dnh * nh, axis=-1, keepdims=True))
        da1_ref[...] = da1
        acc_ref[pl.ds(0, 1), :] += _rsum(da2 * nh)
        acc_ref[pl.ds(1, 1), :] += _rsum(da2)
        acc_ref[pl.ds(2, 1), :] += _rsum(da1)

    blocks = [((TM, A), BF16), ((TM, A), F32), ((TM, A), F32), ((4, A), F32)]
    return pl.pallas_call(
        body, name="mixer_bwd_ln", grid=(S // TM,),
        in_specs=[pl.BlockSpec((TM, A), lambda i: (i, 0)), pl.BlockSpec((TM, A), lambda i: (i, 0)),
                  pl.BlockSpec((1, A), lambda i: (0, 0)), pl.BlockSpec((1, A), lambda i: (0, 0))],
        out_specs=[pl.BlockSpec((TM, A), lambda i: (i, 0)), pl.BlockSpec((4, A), lambda i: (0, 0))],
        out_shape=[jax.ShapeDtypeStruct((S, A), F32), jax.ShapeDtypeStruct((4, A), F32)],
        compiler_params=_params(("arbitrary",), blocks, temps=[((TM, A), F32)] * 12),
    )(dcat, a1, ln_g, ln_b)


def _mixer_bwd_conv(z, dcat, da1, conv_a_w, conv_b_w, S, TM, A):
    H = 32
    KA, KB = conv_a_w.shape[0], conv_b_w.shape[0]
    nI = S // TM
    n_chunks = A // LANES
    RB = _pick(TM, (64, 32))

    def body(zc_ref, zp_ref, zn_ref, dbc_ref, dbn_ref, d1c_ref, d1n_ref, wa_ref, wb_ref,
             dz_ref, dwa_ref, dwb_ref, ext_a0, ext_d1, ext_cb, ext_dc, da0_sc):
        i = pl.program_id(0)
        @pl.when(i == 0)
        def _():
            dwa_ref[...] = jnp.zeros_like(dwa_ref)
            dwb_ref[...] = jnp.zeros_like(dwb_ref)
        first = (i > 0).astype(F32)
        last = (i < nI - 1).astype(F32)
        zc = zc_ref[...].astype(F32)
        zp = zp_ref[...].astype(F32) * first
        a_val, a_gate = zc[:, 0:A], zc[:, A:2 * A]
        b_gate, c_gate, b_h = zc[:, 2 * A:3 * A], zc[:, 3 * A:4 * A], zc[:, 4 * A:5 * A]
        sig = _sigmoid(a_gate)
        ext_a0[pl.ds(0, H), :] = zp[:, 0:A] * _sigmoid(zp[:, A:2 * A])
        ext_a0[pl.ds(H, TM), :] = a_val * sig
        ext_d1[pl.ds(0, TM), :] = d1c_ref[...]
        ext_d1[pl.ds(TM, H), :] = d1n_ref[...] * last
        ext_cb[pl.ds(0, H), :] = zp[:, 3 * A:4 * A] * zp[:, 4 * A:5 * A]
        ext_cb[pl.ds(H, TM), :] = c_gate * b_h
        dbx = dbc_ref[...].astype(F32)
        dcbc = dbx * b_gate
        ext_dc[pl.ds(0, TM), :] = dcbc
        ext_dc[pl.ds(TM, H), :] = dbn_ref[...].astype(F32) * zn_ref[...].astype(F32) * last

        def chunk(c, carry):
            cs = pl.ds(pl.multiple_of(c * LANES, LANES), LANES)
            for r0 in range(0, TM, RB):
                acc = None
                for k in range(KA):
                    term = ext_d1[pl.ds(KA - 1 - k + r0, RB), cs] * wa_ref[pl.ds(k, 1), cs]
                    acc = term if acc is None else acc + term
                da0_sc[pl.ds(r0, RB), cs] = acc
            for k in range(KA):
                acc = None
                for r0 in range(0, TM, RB):
                    term = ext_a0[pl.ds(H - (KA - 1) + k + r0, RB), cs] * ext_d1[pl.ds(r0, RB), cs]
                    acc = term if acc is None else acc + term
                dwa_ref[pl.ds(k, 1), cs] += _rsum(acc)
            return carry
        lax.fori_loop(0, n_chunks, chunk, 0)

        da0 = da0_sc[...]
        dz_ref[:, 0:A] = (da0 * sig).astype(BF16)
        dz_ref[:, A:2 * A] = (da0 * a_val * sig * (1.0 - sig)).astype(BF16)
        cbc = _taps_causal(ext_cb, wb_ref, KB, H, TM, slice(None))
        dz_ref[:, 2 * A:3 * A] = (dbx * cbc).astype(BF16)
        dcb = _taps_anticausal(ext_dc, wb_ref, KB, TM, slice(None))
        dz_ref[:, 3 * A:4 * A] = (dcb * b_h).astype(BF16)
        dz_ref[:, 4 * A:5 * A] = (dcb * c_gate).astype(BF16)
        grads = _tap_grads(ext_cb, dcbc, KB, H, TM, slice(None))
        for k in range(KB):
            dwb_ref[pl.ds(k, 1), :] += grads[k]

    blocks = [((TM, 5 * A), BF16), ((H, 5 * A), BF16), ((H, A), BF16), ((TM, A), BF16), ((H, A), BF16),
              ((TM, A), F32), ((H, A), F32), ((KA, A), F32), ((KB, A), F32),
              ((TM, 5 * A), BF16), ((KA, A), F32), ((KB, A), F32)]
    scratch = [((H + TM, A), F32)] * 4 + [((TM, A), F32)]
    vec = lambda r: pl.BlockSpec((r, A), lambda i: (0, 0))
    return pl.pallas_call(
        body, name="mixer_bwd_conv", grid=(nI,),
        in_specs=[pl.BlockSpec((TM, 5 * A), lambda i: (i, 0)),
                  pl.BlockSpec((H, 5 * A), _prev_rows(TM, H, 0)),
                  pl.BlockSpec((H, A), _next_rows(S, TM, H, 2)),
                  pl.BlockSpec((TM, A), lambda i: (i, 1)),
                  pl.BlockSpec((H, A), _next_rows(S, TM, H, 1)),
                  pl.BlockSpec((TM, A), lambda i: (i, 0)),
                  pl.BlockSpec((H, A), _next_rows(S, TM, H, 0)),
                  vec(KA), vec(KB)],
        out_specs=[pl.BlockSpec((TM, 5 * A), lambda i: (i, 0)), vec(KA), vec(KB)],
        out_shape=[jax.ShapeDtypeStruct((S, 5 * A), BF16), jax.ShapeDtypeStruct((KA, A), F32),
                   jax.ShapeDtypeStruct((KB, A), F32)],
        scratch_shapes=[pltpu.VMEM(s, d) for s, d in scratch],
        compiler_params=_params(("arbitrary",), blocks, scratch,
                                temps=[((TM, 5 * A), F32)] * 2 + [((TM, A), F32)] * 14),
    )(z, z, z, dcat, dcat, da1, da1, conv_a_w, conv_b_w)


def _row_tile(R):
    return _pick(R, (256, 128, 64, 32, 16, 8))


def _scalars(*vals):
    return jnp.stack([jnp.asarray(v, jnp.int32) for v in vals])


def _cast_into_gathered(name, w, chip):
    R, C = w.shape
    TR = _row_tile(R)

    def body(s_ref, w_ref, o_ref):
        o_ref[...] = w_ref[...].astype(BF16)

    grid_spec = pltpu.PrefetchScalarGridSpec(
        num_scalar_prefetch=1, grid=(R // TR,),
        in_specs=[pl.BlockSpec((TR, C), lambda r, s: (r, 0))],
        out_specs=pl.BlockSpec((None, TR, C), lambda r, s: (s[0], r, 0)))
    return pl.pallas_call(body, name=name, grid_spec=grid_spec,
                          out_shape=jax.ShapeDtypeStruct((N_CHIPS, R, C), BF16),
                          compiler_params=_params(("arbitrary",), [((TR, C), F32), ((TR, C), BF16)]),
                          )(_scalars(chip), w)


def _add_pair(name, dw, recv, c):
    _, _, Rh, C = dw.shape
    TR = _row_tile(Rh)

    def body(c_ref, a_ref, b_ref, o_ref, ob_ref):
        s = a_ref[...] + b_ref[...]
        o_ref[...] = s
        ob_ref[...] = s.astype(BF16)

    out_spec = pl.BlockSpec((None, TR, C), lambda k, r, c_ref: (k, r, 0))
    grid_spec = pltpu.PrefetchScalarGridSpec(
        num_scalar_prefetch=1, grid=(N_CHIPS, Rh // TR),
        in_specs=[pl.BlockSpec((None, None, TR, C), lambda k, r, c_ref: (k, c_ref[0], r, 0)),
                  pl.BlockSpec((None, TR, C), lambda k, r, c_ref: (k, r, 0))],
        out_specs=[out_spec, out_spec])
    return pl.pallas_call(body, name=name, grid_spec=grid_spec,
                          out_shape=[jax.ShapeDtypeStruct((N_CHIPS, Rh, C), F32),
                                     jax.ShapeDtypeStruct((N_CHIPS, Rh, C), BF16)],
                          compiler_params=_params(("arbitrary", "arbitrary"), [((TR, C), F32)] * 4),
                          )(_scalars(c), dw, recv)


def _add_chips(name, parts, recv, chip, c):
    _, Rh, C = parts.shape
    TR = _row_tile(Rh)

    def body(s_ref, p_ref, r_ref, o_ref):
        o_ref[...] = ((p_ref[...] + r_ref[0].astype(F32)) + r_ref[1].astype(F32)) + r_ref[2].astype(F32)

    grid_spec = pltpu.PrefetchScalarGridSpec(
        num_scalar_prefetch=1, grid=(Rh // TR,),
        in_specs=[pl.BlockSpec((None, TR, C), lambda r, s: (s[0], r, 0)),
                  pl.BlockSpec((N_CHIPS - 1, TR, C), lambda r, s: (0, r, 0))],
        out_specs=pl.BlockSpec((None, TR, C), lambda r, s: (s[1], r, 0)))
    return pl.pallas_call(body, name=name, grid_spec=grid_spec,
                          out_shape=jax.ShapeDtypeStruct((2, Rh, C), F32),
                          compiler_params=_params(("arbitrary",), [((N_CHIPS + 1, TR, C), F32)]),
                          )(_scalars(chip, c), parts, recv)


def _sum_devices(name, parts):
    _, R, C = parts.shape

    def body(p_ref, o_ref):
        acc = p_ref[0]
        for d in range(1, N_DEV):
            acc = acc + p_ref[d]
        o_ref[...] = acc

    return pl.pallas_call(body, name=name, out_shape=jax.ShapeDtypeStruct((R, C), F32),
                          in_specs=[pl.BlockSpec(memory_space=pltpu.VMEM)],
                          out_specs=pl.BlockSpec(memory_space=pltpu.VMEM))(parts)


def _adamw(name, w, g, m, v):
    R, C = w.shape
    TR = _pick(R, (128, 64, 32, 16, 8))
    c1 = 1.0 - ADAM_B1 ** ADAM_STEP
    c2 = 1.0 - ADAM_B2 ** ADAM_STEP

    def body(w_ref, g_ref, m_ref, v_ref, d_ref, nm_ref, nv_ref):
        g_ = g_ref[...]
        nm = ADAM_B1 * m_ref[...] + (1.0 - ADAM_B1) * g_
        nv = ADAM_B2 * v_ref[...] + (1.0 - ADAM_B2) * (g_ * g_)
        d_ref[...] = -ADAM_LR * ((nm / c1) / (jnp.sqrt(nv / c2) + ADAM_EPS) + ADAM_WD * w_ref[...])
        nm_ref[...] = nm
        nv_ref[...] = nv

    spec = pl.BlockSpec((TR, C), lambda r: (r, 0))
    shp = jax.ShapeDtypeStruct((R, C), F32)
    return pl.pallas_call(body, name=name, grid=(R // TR,), in_specs=[spec] * 4, out_specs=[spec] * 3,
                          out_shape=[shp] * 3,
                          compiler_params=_params(("arbitrary",), [((TR, C), F32)] * 7))(w, g, m, v)


def _place():
    x, y, c = lax.axis_index("x"), lax.axis_index("y"), lax.axis_index("c")
    others = [(1 - x, y), (x, 1 - y), (1 - x, 1 - y)]
    return x, y, c, others


def _allgather_small(name, block):
    R, C = block.shape

    def body(x_ref, out_ref, send_sems, recv_sems, local_sem):
        x, y, c, chips = _place()
        me, sibling = (x, y, c), (x, y, 1 - c)

        def rows(px, py, pc):
            return out_ref.at[4 * px + 2 * py + pc]

        def copy(k, blk, to, src=None):
            return pltpu.make_async_remote_copy(
                src_ref=rows(*blk) if src is None else src, dst_ref=rows(*blk),
                send_sem=send_sems.at[k], recv_sem=recv_sems.at[k], device_id=to, device_id_type=MESH)

        mine = pltpu.make_async_copy(x_ref, rows(*me), local_sem)
        mine.start()
        first = [copy(0, me, sibling, src=x_ref)]
        first += [copy(1 + j, me, (*chip, c), src=x_ref) for j, chip in enumerate(chips)]
        for cp in first:
            cp.start()
        passed = [copy(4 + j, (*chip, c), sibling) for j, chip in enumerate(chips)]
        for j, chip in enumerate(chips):
            copy(1 + j, (*chip, c), me).wait_recv()
            passed[j].start()
        copy(0, sibling, me).wait_recv()
        for j, chip in enumerate(chips):
            copy(4 + j, (*chip, 1 - c), me).wait_recv()
        for cp in first + passed:
            cp.wait_send()
        mine.wait()

    return pl.pallas_call(
        body, name=name, out_shape=jax.ShapeDtypeStruct((N_DEV, R, C), F32),
        in_specs=[pl.BlockSpec(memory_space=pltpu.VMEM)], out_specs=pl.BlockSpec(memory_space=pltpu.VMEM),
        scratch_shapes=[pltpu.SemaphoreType.DMA((7,)), pltpu.SemaphoreType.DMA((7,)), pltpu.SemaphoreType.DMA],
    )(block)


def _allgather_weights(bufs):
    n = len(bufs)

    def body(*refs):
        ins, outs = refs[:n], refs[n:2 * n]
        send_sems, recv_sems = refs[2 * n:]
        x, y, c, chips = _place()
        me, sibling = (x, y, c), (x, y, 1 - c)

        def half_of(ref, chip, half):
            rh = ref.shape[1] // 2
            return ref.at[2 * chip[0] + chip[1], pl.ds(half * rh, rh)]

        def copy(a, k, src, dst, to):
            return pltpu.make_async_remote_copy(
                src_ref=src, dst_ref=dst, send_sem=send_sems.at[6 * a + k], recv_sem=recv_sems.at[6 * a + k],
                device_id=to, device_id_type=MESH)

        first, passed = [], []
        for a in range(n):
            first.append([copy(a, j, half_of(ins[a], (x, y), c), half_of(outs[a], (x, y), c), (*chip, c))
                          for j, chip in enumerate(chips)])
            for cp in first[a]:
                cp.start()
        for a in range(n):
            passed.append([copy(a, 3 + j, half_of(outs[a], chip, c), half_of(outs[a], chip, c), sibling)
                           for j, chip in enumerate(chips)])
            for j, chip in enumerate(chips):
                copy(a, j, half_of(outs[a], chip, c), half_of(outs[a], chip, c), me).wait_recv()
                passed[a][j].start()
        for a in range(n):
            for j, chip in enumerate(chips):
                copy(a, 3 + j, half_of(outs[a], chip, 1 - c), half_of(outs[a], chip, 1 - c), me).wait_recv()
            for cp in first[a] + passed[a]:
                cp.wait_send()

    return pl.pallas_call(
        body, name="allgather_weights",
        out_shape=[jax.ShapeDtypeStruct(b.shape, BF16) for b in bufs],
        in_specs=[ANY] * n, out_specs=[ANY] * n, input_output_aliases={a: a for a in range(n)},
        scratch_shapes=[pltpu.SemaphoreType.DMA((6 * n,)), pltpu.SemaphoreType.DMA((6 * n,))],
    )(*bufs)


def _exchange_pairs(grads):
    n = len(grads)

    def body(*refs):
        ins, outs = refs[:n], refs[n:2 * n]
        send_sems, recv_sems = refs[2 * n:]
        x, y, c, _ = _place()
        copies = [pltpu.make_async_remote_copy(
            src_ref=ins[a].at[:, 1 - c], dst_ref=outs[a], send_sem=send_sems.at[a], recv_sem=recv_sems.at[a],
            device_id=(x, y, 1 - c), device_id_type=MESH) for a in range(n)]
        for cp in copies:
            cp.start()
        for cp in copies:
            cp.wait()

    return pl.pallas_call(
        body, name="grads_exchange_pairs",
        out_shape=[jax.ShapeDtypeStruct((N_CHIPS,) + g.shape[2:], F32) for g in grads],
        in_specs=[ANY] * n, out_specs=[ANY] * n,
        scratch_shapes=[pltpu.SemaphoreType.DMA((n,)), pltpu.SemaphoreType.DMA((n,))],
    )(*grads)


def _exchange_chips(parts):
    n = len(parts)

    def body(*refs):
        ins, outs = refs[:n], refs[n:2 * n]
        send_sems, recv_sems = refs[2 * n:]
        x, y, c, chips = _place()
        sends = []
        for a in range(n):
            for j, chip in enumerate(chips):
                sends.append(pltpu.make_async_remote_copy(
                    src_ref=ins[a].at[2 * chip[0] + chip[1]], dst_ref=outs[a].at[j],
                    send_sem=send_sems.at[3 * a + j], recv_sem=recv_sems.at[3 * a + j],
                    device_id=(*chip, c), device_id_type=MESH))
                sends[-1].start()
        for cp in sends:
            cp.wait_recv()
        for cp in sends:
            cp.wait_send()

    return pl.pallas_call(
        body, name="grads_exchange_chips",
        out_shape=[jax.ShapeDtypeStruct((N_CHIPS - 1,) + p.shape[1:], p.dtype) for p in parts],
        in_specs=[ANY] * n, out_specs=[ANY] * n,
        scratch_shapes=[pltpu.SemaphoreType.DMA((3 * n,)), pltpu.SemaphoreType.DMA((3 * n,))],
    )(*parts)


def _share_halves(halves):
    n = len(halves)

    def body(*refs):
        ins, outs = refs[:n], refs[n:2 * n]
        send_sems, recv_sems = refs[2 * n:]
        x, y, c, _ = _place()
        sends = []
        for a in range(n):
            sends.append(pltpu.make_async_remote_copy(
                src_ref=ins[a].at[c], dst_ref=outs[a].at[c], send_sem=send_sems.at[a], recv_sem=recv_sems.at[a],
                device_id=(x, y, 1 - c), device_id_type=MESH))
            sends[a].start()
        for a in range(n):
            theirs = outs[a].at[1 - c]
            pltpu.make_async_remote_copy(
                src_ref=theirs, dst_ref=theirs, send_sem=send_sems.at[a], recv_sem=recv_sems.at[a],
                device_id=(x, y, c), device_id_type=MESH).wait_recv()
        for cp in sends:
            cp.wait_send()

    return pl.pallas_call(
        body, name="grads_share_halves",
        out_shape=[jax.ShapeDtypeStruct(h.shape, F32) for h in halves],
        in_specs=[ANY] * n, out_specs=[ANY] * n, input_output_aliases={a: a for a in range(n)},
        scratch_shapes=[pltpu.SemaphoreType.DMA((n,)), pltpu.SemaphoreType.DMA((n,))],
    )(*halves)


def _pack(arrays):
    pieces = []
    for a in arrays:
        flat = a.reshape(-1).astype(F32)
        pieces.append(jnp.pad(flat, (0, (-flat.size) % PACK_ALIGN)))
    return jnp.concatenate(pieces).reshape(-1, LANES)


def _unpack(buf, shapes):
    lead = buf.shape[:-2]
    flat = buf.reshape(lead + (-1,))
    out, off = [], 0
    for shp in shapes:
        size = 1
        for s in shp:
            size *= s
        out.append(flat[..., off:off + size].reshape(lead + tuple(shp)))
        off += size + (-size) % PACK_ALIGN
    return out


def _gather_channels(buf, shapes):
    per_chip = _unpack(buf[0::2], shapes)
    return [jnp.transpose(a, (1, 0, 2)).reshape(a.shape[1], -1) for a in per_chip]


def _mm_tile(n, rows, limit_bytes=6 * 1024 * 1024):
    for t in (1408, 1280, 1024, 640, 512, 384, 256, 128):
        if n % t == 0 and rows * t * 2 <= limit_bytes:
            return t
    raise ValueError(f"no column tile for {n} x {rows}")


def kernel(x, p, norm_mix_g, w_in, conv_a_w, conv_a_b, ln_a_g, ln_a_b, conv_b_w, w_out, norm_ffn_g, w_up, conv_ffn_w, w_down, w_ple_gate, b_ple_gate, w_ple_proj, norm_final_g, loss_target, m_norm_mix_g, m_w_in, m_conv_a_w, m_conv_a_b, m_ln_a_g, m_ln_a_b, m_conv_b_w, m_w_out, m_norm_ffn_g, m_w_up, m_conv_ffn_w, m_w_down, m_w_ple_gate, m_b_ple_gate, m_w_ple_proj, m_norm_final_g, v_norm_mix_g, v_w_in, v_conv_a_w, v_conv_a_b, v_ln_a_g, v_ln_a_b, v_conv_b_w, v_w_out, v_norm_ffn_g, v_w_up, v_conv_ffn_w, v_w_down, v_w_ple_gate, v_b_ple_gate, v_w_ple_proj, v_norm_final_g):
    S, D = x.shape[1], x.shape[2]
    P = p.shape[3]
    A = conv_a_b.shape[1]
    F = w_down.shape[1] * N_CHIPS
    KA, KB, KF = conv_a_w.shape[1], conv_b_w.shape[1], conv_ffn_w.shape[1]
    xi, yi, ci = lax.axis_index("x"), lax.axis_index("y"), lax.axis_index("c")
    chip = 2 * xi + yi

    TM = _pick(S, (512, 256, 128))
    TE = _pick(S, (256, 128))
    TC = _pick(2 * F // N_CHIPS, (1408, 1024, 512, 256, 128))
    ffn_place = _pair_tile(F // TC)

    x2, p2, t2 = x.reshape(S, D), p.reshape(S, P), loss_target.reshape(S, D)
    gfin = norm_final_g.reshape(1, D)

    big = dict(w_in=w_in[0], w_out=w_out[0], w_up=w_up[0], w_down=w_down[0],
               w_ple_gate=w_ple_gate[0], w_ple_proj=w_ple_proj[0])
    names = list(big)
    gathered = _allgather_weights([_cast_into_gathered("cast_" + n, big[n], chip) for n in names])
    w_in3, w_out3, w_up3, w_down3, w_gate3, w_proj3 = gathered
    w_out_f = w_out3.reshape(2 * A, D)
    w_down_f = w_down3.reshape(F, D)
    w_gate_f = w_gate3.reshape(D, D)

    tap_shapes = [(KA, A // N_CHIPS), (KB, A // N_CHIPS), (KF, 2 * F // N_CHIPS)]
    taps = _allgather_small("allgather_taps", _pack([conv_a_w[0], conv_b_w[0], conv_ffn_w[0]]))
    conv_a_f, conv_b_f, conv_ffn_f = _gather_channels(taps, tap_shapes)

    def rms_prologue(rows, row_r, vec_r, ro_r, ao_r):
        h = row_r[0][rows, :]
        hn = (h * _rms_stats(h) * vec_r[0][...]).astype(BF16)
        ro_r[0][rows, :] = hn
        return [hn]

    def cast_prologue(rows, row_r, vec_r, ro_r, ao_r):
        hb = row_r[0][rows, :].astype(BF16)
        ro_r[0][rows, :] = hb
        return [hb]

    plain = lambda accs, tile_r, cv_r: [accs[0]]
    residual = lambda accs, tile_r, cv_r: [tile_r[0][...] + accs[0]]

    z, hn1 = _rows_mm("in_proj", S, TM, 5 * A, _mm_tile(5 * A // N_CHIPS, D), row_ins=[x2], vec_ins=[norm_mix_g],
                      weights=[(w_in3, "nn3")], tile_outs=[BF16], row_outs=[(D, BF16)],
                      prologue=rms_prologue, epilogue=plain)
    a1, cat = _mixer_fwd(z, conv_a_f, conv_a_b, ln_a_g, ln_a_b, conv_b_f, S, TE, A)
    (h1,) = _rows_mm("out_proj", S, TM, D, _mm_tile(D, 2 * A), row_ins=[cat], weights=[(w_out_f, "nn2")],
                     tile_ins=[x2], tile_outs=[F32], epilogue=residual)
    u0, hn2 = _rows_mm("up_proj", S, TM, 2 * F, TC, row_ins=[h1], vec_ins=[norm_ffn_g],
                       weights=[(w_up3, "nn3")], tile_outs=[BF16], row_outs=[(D, BF16)],
                       prologue=rms_prologue, epilogue=plain, place=ffn_place)
    act = _ffn_act(u0, conv_ffn_f, S, TM, F, TC)
    (h2,) = _rows_mm("down_proj", S, TM, D, _mm_tile(D, F), row_ins=[act], weights=[(w_down_f, "nn2")],
                     tile_ins=[h1], tile_outs=[F32], epilogue=residual)

    def ple_prologue(rows, row_r, vec_r, ro_r, ao_r):
        hb = row_r[0][rows, :].astype(BF16)
        pb = row_r[1][rows, :].astype(BF16)
        ro_r[0][rows, :] = hb
        ro_r[1][rows, :] = pb
        return [hb, pb]

    def ple_epilogue(accs, tile_r, cv_r):
        gate = _sigmoid(accs[0] + cv_r[0][...])
        return [tile_r[0][...] + accs[1] * gate, gate, accs[1]]

    h3, gate, pp, h2b, pb = _rows_mm(
        "ple_fwd", S, TM, D, _mm_tile(D // N_CHIPS, D), row_ins=[h2, p2], colvec_ins=[b_ple_gate],
        weights=[(w_gate_f, "nn2"), (w_proj3, "nn3")], tile_ins=[h2], tile_outs=[F32, BF16, BF16],
        row_outs=[(D, BF16), (P, BF16)], prologue=ple_prologue, epilogue=ple_epilogue)
    loss_part, g_norm_final, dh3 = _loss_head(h3, t2, gfin, S, TE, D)

    def ple_bwd_prologue(rows, row_r, vec_r, ro_r, ao_r):
        d = row_r[0][rows, :]
        gt = row_r[2][rows, :].astype(F32)
        dpre = d * row_r[1][rows, :].astype(F32) * gt * (1.0 - gt)
        ro_r[0][rows, :] = dpre.astype(BF16)
        ro_r[1][rows, :] = (d * gt).astype(BF16)
        ao_r[0][...] += _rsum(dpre)
        return [dpre.astype(BF16)]

    dh2, dpre, dpp, g_b_gate = _rows_mm(
        "ple_bwd", S, TM, D, _mm_tile(D, D), row_ins=[dh3, pp, gate], weights=[(w_gate_f, "nt2")],
        tile_ins=[dh3], tile_outs=[F32], row_outs=[(D, BF16), (D, BF16)], acc_outs=[(1, D)],
        prologue=ple_bwd_prologue, epilogue=residual)
    dact, dh2b = _rows_mm("down_bwd", S, TM, F, _mm_tile(F, D), row_ins=[dh2], weights=[(w_down_f, "nt2")],
                          tile_outs=[BF16], row_outs=[(D, BF16)], prologue=cast_prologue, epilogue=plain)
    du0, g_conv_gate, g_conv_up = _ffn_bwd(u0, dact, conv_ffn_f, S, TM, F, TC)
    g_conv_ffn = jnp.concatenate([g_conv_gate, g_conv_up], axis=1)

    def up_bwd_epilogue(acc, rows, row_r, vec_r, ro_r, ao_r):
        dh, dg = _rms_bwd(row_r[0][rows, :], vec_r[0][...], acc)
        dh1_ = row_r[1][rows, :] + dh
        ro_r[0][rows, :] = dh1_
        ro_r[1][rows, :] = dh1_.astype(BF16)
        ao_r[0][...] += dg

    dh1, dh1b, g_norm_ffn = _kloop_mm(
        "up_bwd", S, TM, du0, w_up3, TC, row_ins=[h1, dh2], vec_ins=[norm_ffn_g],
        row_outs=[(D, F32), (D, BF16)], acc_outs=[(1, D)], epilogue=up_bwd_epilogue, place=ffn_place)
    (dcat,) = _rows_mm("out_bwd", S, TM, 2 * A, _mm_tile(2 * A, D), row_ins=[dh1b], weights=[(w_out_f, "nt2")],
                       tile_outs=[BF16], epilogue=plain)
    da1, ln_sums = _mixer_bwd_ln(dcat, a1, ln_a_g, ln_a_b, S, TE, A)
    dz, g_conv_a, g_conv_b = _mixer_bwd_conv(z, dcat, da1, conv_a_f, conv_b_f, S, TE, A)

    def in_bwd_epilogue(acc, rows, row_r, vec_r, ro_r, ao_r):
        dh, dg = _rms_bwd(row_r[0][rows, :], vec_r[0][...], acc)
        ro_r[0][rows, :] = row_r[1][rows, :] + dh
        ao_r[0][...] += dg

    dx, g_norm_mix = _kloop_mm(
        "in_bwd", S, TM, dz, w_in3, _mm_tile(5 * A // N_CHIPS, D), row_ins=[x2, dh1],
        vec_ins=[norm_mix_g], row_outs=[(D, F32)], acc_outs=[(1, D)], epilogue=in_bwd_epilogue)

    TK = _pick(S, (1024, 512, 256, 128))
    wt = lambda n: _pick(n, (1408, 1280, 1024, 512, 256, 128))
    part = dict(
        w_in=_tn_mm("dw_in", hn1, dz, wt(D), wt(5 * A // N_CHIPS), TK, cols_per_chip=5 * A // N_CHIPS),
        w_out=_tn_mm("dw_out", cat, dh1b, wt(2 * A), wt(D), TK),
        w_up=_tn_mm("dw_up", hn2, du0, wt(D), TC, TK, cols_per_chip=2 * F // N_CHIPS, place=ffn_place),
        w_down=_tn_mm("dw_down", act, dh2b, wt(F), wt(D), TK),
        w_ple_gate=_tn_mm("dw_ple_gate", h2b, dpre, wt(D), wt(D), TK),
        w_ple_proj=_tn_mm("dw_ple_proj", pb, dpp, wt(P), wt(D // N_CHIPS), TK, cols_per_chip=D // N_CHIPS),
    )

    def halves(n):
        R, C = big[n].shape
        return part[n].reshape(N_CHIPS, 2, R // 2, C)

    from_sibling = _exchange_pairs([halves(n) for n in names])
    pair_sums = [_add_pair("pair_sum_" + n, halves(n), r, ci) for n, r in zip(names, from_sibling)]
    chip_sums = [s for s, _ in pair_sums]
    from_chips = _exchange_chips([b for _, b in pair_sums])
    reduced = _share_halves([_add_chips("chip_sum_" + n, s, r, chip, ci)
                             for n, s, r in zip(names, chip_sums, from_chips)])
    moments = dict(w_in=(m_w_in, v_w_in), w_out=(m_w_out, v_w_out), w_up=(m_w_up, v_w_up),
                   w_down=(m_w_down, v_w_down), w_ple_gate=(m_w_ple_gate, v_w_ple_gate),
                   w_ple_proj=(m_w_ple_proj, v_w_ple_proj))
    grads, deltas, new_m, new_v = {}, {}, {}, {}
    for n, g in zip(names, reduced):
        g = g.reshape(big[n].shape)
        d_, m_, v_ = _adamw("adamw_" + n, big[n], g, moments[n][0][0], moments[n][1][0])
        grads[n], deltas[n], new_m[n], new_v[n] = g[None], d_[None], m_[None], v_[None]

    small = ["norm_mix_g", "conv_a_w", "conv_a_b", "ln_a_g", "ln_a_b", "conv_b_w", "norm_ffn_g",
             "conv_ffn_w", "b_ple_gate", "norm_final_g"]
    small_part = [g_norm_mix, g_conv_a, ln_sums[2:3], ln_sums[0:1], ln_sums[1:2], g_conv_b, g_norm_ffn,
                  g_conv_ffn, g_b_gate, g_norm_final]
    full_shapes = [a.shape for a in small_part]
    summed = _sum_devices("small_grads_sum", _allgather_small("allgather_small_grads", _pack(small_part)))
    small_g = dict(zip(small, _unpack(summed, full_shapes)))
    for n, width in (("conv_a_w", A), ("conv_b_w", A), ("conv_ffn_w", 2 * F)):
        small_g[n] = lax.dynamic_slice_in_dim(small_g[n], chip * (width // N_CHIPS), width // N_CHIPS, axis=1)
    small_w = dict(norm_mix_g=(norm_mix_g, m_norm_mix_g, v_norm_mix_g), conv_a_w=(conv_a_w, m_conv_a_w, v_conv_a_w),
                   conv_a_b=(conv_a_b, m_conv_a_b, v_conv_a_b), ln_a_g=(ln_a_g, m_ln_a_g, v_ln_a_g),
                   ln_a_b=(ln_a_b, m_ln_a_b, v_ln_a_b), conv_b_w=(conv_b_w, m_conv_b_w, v_conv_b_w),
                   norm_ffn_g=(norm_ffn_g, m_norm_ffn_g, v_norm_ffn_g),
                   conv_ffn_w=(conv_ffn_w, m_conv_ffn_w, v_conv_ffn_w),
                   b_ple_gate=(b_ple_gate, m_b_ple_gate, v_b_ple_gate),
                   norm_final_g=(norm_final_g, m_norm_final_g, v_norm_final_g))
    out_shapes = [small_w[n][0].shape for n in small]
    packed_g = _pack([small_g[n] for n in small])
    packed = [_pack([small_w[n][k] for n in small]) for k in range(3)]
    d_s, m_s, v_s = _adamw("adamw_small", packed[0], packed_g, packed[1], packed[2])
    for n, g, d_, m_, v_ in zip(small, _unpack(packed_g, out_shapes), _unpack(d_s, out_shapes),
                                _unpack(m_s, out_shapes), _unpack(v_s, out_shapes)):
        grads[n], deltas[n], new_m[n], new_v[n] = g, d_, m_, v_

    order = ["norm_mix_g", "w_in", "conv_a_w", "conv_a_b", "ln_a_g", "ln_a_b", "conv_b_w", "w_out", "norm_ffn_g",
             "w_up", "conv_ffn_w", "w_down", "w_ple_gate", "b_ple_gate", "w_ple_proj", "norm_final_g"]
    loss = lax.psum(loss_part[0, 0], ("x", "y", "c"))
    return (loss, dx.reshape(x.shape), *[grads[n] for n in order], *[deltas[n] for n in order],
            *[new_m[n] for n in order], *[new_v[n] for n in order])
```

```python
from typing import Callable, NamedTuple

import jax
import jax.numpy as jnp
from jax import lax
from jax.experimental import pallas as pl
from jax.experimental.pallas import tpu as pltpu

F32 = jnp.float32
BF16 = jnp.bfloat16
MESH = pl.DeviceIdType.MESH
ANY = pl.BlockSpec(memory_space=pl.ANY)

EPS = 1e-6
ADAM_LR = 0.001
ADAM_B1 = 0.9
ADAM_B2 = 0.999
ADAM_EPS = 1e-08
ADAM_WD = 0.01
ADAM_STEP = 10

N_CHIPS = 4
N_DEV = 8
LANES = 128
SUBLANES = 8
PACK_ALIGN = LANES * SUBLANES
ROW_CHUNK = 32
VMEM_CAP = 60 * 1024 * 1024
VMEM_SLACK = 6 * 1024 * 1024


def _pick(n, cands):
    for c in cands:
        if n % c == 0:
            return c
    raise ValueError(f"no tile of {cands} divides {n}")


def _nbytes(shape, dtype):
    n = 1
    for s in shape:
        if s is not None:
            n *= s
    return n * jnp.dtype(dtype).itemsize


def _params(sem, blocks, scratch=(), temps=()):
    est = (2 * sum(_nbytes(s, d) for s, d in blocks) + sum(_nbytes(s, d) for s, d in scratch)
           + sum(_nbytes(s, d) for s, d in temps))
    return pltpu.CompilerParams(dimension_semantics=sem,
                                vmem_limit_bytes=min(est + VMEM_SLACK, VMEM_CAP))


def _sigmoid(x):
    return 1.0 / (1.0 + jnp.exp(-x))


def _rsum(x):
    return jnp.sum(x, axis=0, keepdims=True)


class _Side(NamedTuple):
    ins: list
    out_shapes: list
    n_sems: int
    start: Callable
    wait: Callable


def _call(body, side, *, name, grid, in_specs, out_specs, out_shape, scratch, params, args):
    vmem = [pltpu.VMEM(s, d) for s, d in scratch]
    if side is None:
        outs = pl.pallas_call(body, name=name, grid=grid, in_specs=in_specs, out_specs=out_specs,
                              out_shape=out_shape, scratch_shapes=vmem, compiler_params=params)(*args)
        return list(outs), []
    n_in, n_out, n_sc = len(in_specs), len(out_specs), len(scratch)
    ns_in, ns_out = len(side.ins), len(side.out_shapes)

    def carrier(*refs):
        pos = [0]
        def take(n):
            pos[0] += n
            return refs[pos[0] - n:pos[0]]
        ins, s_ins, outs, s_outs, scr = take(n_in), take(ns_in), take(n_out), take(ns_out), take(n_sc)
        send_sems, recv_sems = take(2)
        first = last = None
        for axis, extent in enumerate(grid):
            at_start, at_end = pl.program_id(axis) == 0, pl.program_id(axis) == extent - 1
            first = at_start if first is None else first & at_start
            last = at_end if last is None else last & at_end

        @pl.when(first)
        def _():
            side.start(s_ins, s_outs, send_sems, recv_sems)
        body(*ins, *outs, *scr)

        @pl.when(last)
        def _():
            side.wait(s_ins, s_outs, send_sems, recv_sems)

    outs = pl.pallas_call(
        carrier, name=name, grid=grid, in_specs=list(in_specs) + [ANY] * ns_in,
        out_specs=list(out_specs) + [ANY] * ns_out, out_shape=list(out_shape) + list(side.out_shapes),
        scratch_shapes=vmem + [pltpu.SemaphoreType.DMA((side.n_sems,)), pltpu.SemaphoreType.DMA((side.n_sems,))],
        compiler_params=params)(*args, *side.ins)
    return list(outs[:n_out]), list(outs[n_out:])


def _rms_stats(x):
    return lax.rsqrt(jnp.mean(x * x, axis=-1, keepdims=True) + EPS)


def _rms_bwd(h, g, dout):
    r = _rms_stats(h)
    n = h * r
    dn = dout * g
    dh = r * (dn - n * jnp.mean(dn * n, axis=-1, keepdims=True))
    return dh, _rsum(dout * n)


def _identity(t):
    return t


def _chip_major(nb, place=_identity):
    return lambda i, j: (place(j) // nb, 0, place(j) % nb)


def _rows_mm(name, S, TM, N, TN, *, row_ins, vec_ins=(), colvec_ins=(), weights, tile_ins=(),
             tile_outs, row_outs=(), acc_outs=(), prologue=None, epilogue, place=_identity, side=None):
    nI, nJ = S // TM, N // TN
    n_row, n_vec, n_cv, n_w, n_tile = len(row_ins), len(vec_ins), len(colvec_ins), len(weights), len(tile_ins)
    n_to, n_ro, n_ao = len(tile_outs), len(row_outs), len(acc_outs)

    in_specs, blocks, scratch, ks = [], [], [], []
    for a in row_ins:
        in_specs.append(pl.BlockSpec((TM, a.shape[1]), lambda i, j: (i, 0)))
        blocks.append(((TM, a.shape[1]), a.dtype))
    for a in vec_ins:
        in_specs.append(pl.BlockSpec(a.shape, lambda i, j: (0, 0)))
        blocks.append((a.shape, a.dtype))
    for a in colvec_ins:
        in_specs.append(pl.BlockSpec((1, TN), lambda i, j: (0, j)))
        blocks.append(((1, TN), a.dtype))
    for w, mode in weights:
        if mode == "nn2":
            k = w.shape[0]
            in_specs.append(pl.BlockSpec((k, TN), lambda i, j: (0, j)))
        elif mode == "nn3":
            k = w.shape[1]
            in_specs.append(pl.BlockSpec((None, k, TN), _chip_major(w.shape[2] // TN, place)))
        else:
            k = w.shape[1]
            in_specs.append(pl.BlockSpec((TN, k), lambda i, j: (j, 0)))
        ks.append(k)
        blocks.append(((k, TN), BF16))
        if prologue is not None:
            scratch.append(((TM, k), BF16))
    for a in tile_ins:
        in_specs.append(pl.BlockSpec((TM, TN), lambda i, j: (i, j)))
        blocks.append(((TM, TN), a.dtype))

    out_shape, out_specs = [], []
    for dt in tile_outs:
        out_shape.append(jax.ShapeDtypeStruct((S, N), dt))
        out_specs.append(pl.BlockSpec((TM, TN), lambda i, j: (i, j)))
        blocks.append(((TM, TN), dt))
    for width, dt in row_outs:
        out_shape.append(jax.ShapeDtypeStruct((S, width), dt))
        out_specs.append(pl.BlockSpec((TM, width), lambda i, j: (i, 0)))
        blocks.append(((TM, width), dt))
    for rows, width in acc_outs:
        out_shape.append(jax.ShapeDtypeStruct((rows, width), F32))
        out_specs.append(pl.BlockSpec((rows, width), lambda i, j: (0, 0)))
        blocks.append(((rows, width), F32))

    modes = [m for _, m in weights]

    def body(*refs):
        pos = 0
        def take(n):
            nonlocal pos
            out = refs[pos:pos + n]
            pos += n
            return out
        row_r, vec_r, cv_r, w_r, tile_r = take(n_row), take(n_vec), take(n_cv), take(n_w), take(n_tile)
        to_r, ro_r, ao_r, a_sc = take(n_to), take(n_ro), take(n_ao), take(len(scratch))
        i, j = pl.program_id(0), pl.program_id(1)

        if prologue is None:
            a_sc = row_r[:n_w]
        else:
            @pl.when(j == 0)
            def _():
                if n_ao:
                    @pl.when(i == 0)
                    def _():
                        for r in ao_r:
                            r[...] = jnp.zeros_like(r)

                def chunk(ci, carry):
                    rows = pl.ds(pl.multiple_of(ci * ROW_CHUNK, ROW_CHUNK), ROW_CHUNK)
                    for sc, a in zip(a_sc, prologue(rows, row_r, vec_r, ro_r, ao_r)):
                        sc[rows, :] = a
                    return carry
                lax.fori_loop(0, TM // ROW_CHUNK, chunk, 0)

        accs = []
        for w_ref, sc, mode in zip(w_r, a_sc, modes):
            if mode == "nt2":
                accs.append(lax.dot_general(sc[...], w_ref[...], (((1,), (1,)), ((), ())),
                                            preferred_element_type=F32))
            else:
                accs.append(jnp.dot(sc[...], w_ref[...], preferred_element_type=F32))
        outs = epilogue(accs, tile_r, cv_r)
        for r, o in zip(to_r, outs):
            r[...] = o.astype(r.dtype)

    outs, side_outs = _call(
        body, side, name=name, grid=(nI, nJ), in_specs=in_specs, out_specs=out_specs, out_shape=out_shape,
        scratch=scratch, params=_params(("arbitrary", "arbitrary"), blocks, scratch, temps=[((TM, TN), F32)] * 3),
        args=[*row_ins, *vec_ins, *colvec_ins, *[w for w, _ in weights], *tile_ins])
    return outs if side is None else (outs, side_outs)


def _kloop_mm(name, S, TM, a, w3, TK, *, row_ins, vec_ins, row_outs, acc_outs, epilogue, place=_identity,
              side=None):
    _, N, Ks = w3.shape
    nb = Ks // TK
    nK = N_CHIPS * nb
    n_row, n_vec, n_ro, n_ao = len(row_ins), len(vec_ins), len(row_outs), len(acc_outs)

    in_specs = [pl.BlockSpec((TM, TK), lambda i, k: (i, k)),
                pl.BlockSpec((None, N, TK), _chip_major(nb, place))]
    blocks = [((TM, TK), BF16), ((N, TK), BF16)]
    for r in row_ins:
        in_specs.append(pl.BlockSpec((TM, r.shape[1]), lambda i, k: (i, 0)))
        blocks.append(((TM, r.shape[1]), r.dtype))
    for v in vec_ins:
        in_specs.append(pl.BlockSpec(v.shape, lambda i, k: (0, 0)))
        blocks.append((v.shape, v.dtype))
    out_shape, out_specs = [], []
    for width, dt in row_outs:
        out_shape.append(jax.ShapeDtypeStruct((S, width), dt))
        out_specs.append(pl.BlockSpec((TM, width), lambda i, k: (i, 0)))
        blocks.append(((TM, width), dt))
    for rows, width in acc_outs:
        out_shape.append(jax.ShapeDtypeStruct((rows, width), F32))
        out_specs.append(pl.BlockSpec((rows, width), lambda i, k: (0, 0)))
        blocks.append(((rows, width), F32))
    scratch = [((TM, N), F32)]

    def body(*refs):
        a_ref, w_ref = refs[0], refs[1]
        row_r = refs[2:2 + n_row]
        vec_r = refs[2 + n_row:2 + n_row + n_vec]
        pos = 2 + n_row + n_vec
        ro_r = refs[pos:pos + n_ro]
        ao_r = refs[pos + n_ro:pos + n_ro + n_ao]
        acc_sc = refs[pos + n_ro + n_ao]
        i, k = pl.program_id(0), pl.program_id(1)
        d = lax.dot_general(a_ref[...], w_ref[...], (((1,), (1,)), ((), ())), preferred_element_type=F32)

        @pl.when(k == 0)
        def _():
            acc_sc[...] = d

        @pl.when(k > 0)
        def _():
            acc_sc[...] += d

        @pl.when(k == nK - 1)
        def _():
            @pl.when(i == 0)
            def _():
                for r in ao_r:
                    r[...] = jnp.zeros_like(r)

            def chunk(ci, carry):
                rows = pl.ds(pl.multiple_of(ci * ROW_CHUNK, ROW_CHUNK), ROW_CHUNK)
                epilogue(acc_sc[rows, :], rows, row_r, vec_r, ro_r, ao_r)
                return carry
            lax.fori_loop(0, TM // ROW_CHUNK, chunk, 0)

    outs, side_outs = _call(
        body, side, name=name, grid=(S // TM, nK), in_specs=in_specs, out_specs=out_specs, out_shape=out_shape,
        scratch=scratch, params=_params(("arbitrary", "arbitrary"), blocks, scratch, temps=[((TM, N), F32)]),
        args=[a, w3, *row_ins, *vec_ins])
    return outs if side is None else (outs, side_outs)


def _tn_mm(name, a, b, TMw, TNw, TK, cols_per_chip=None, place=_identity):
    S, M = a.shape
    N = b.shape[1]
    nK = S // TK
    if cols_per_chip is None:
        out_shape = jax.ShapeDtypeStruct((M, N), F32)
        out_spec = pl.BlockSpec((TMw, TNw), lambda i, j, k: (i, j))
    else:
        nb = cols_per_chip // TNw
        out_shape = jax.ShapeDtypeStruct((N_CHIPS, M, cols_per_chip), F32)
        out_spec = pl.BlockSpec((None, TMw, TNw), lambda i, j, k: (place(j) // nb, i, place(j) % nb))

    def body(a_ref, b_ref, o_ref):
        k = pl.program_id(2)
        d = lax.dot_general(a_ref[...], b_ref[...], (((0,), (0,)), ((), ())), preferred_element_type=F32)

        @pl.when(k == 0)
        def _():
            o_ref[...] = d

        @pl.when(k > 0)
        def _():
            o_ref[...] += d

    blocks = [((TK, TMw), BF16), ((TK, TNw), BF16), ((TMw, TNw), F32)]
    return pl.pallas_call(
        body, name=name, grid=(M // TMw, N // TNw, nK),
        in_specs=[pl.BlockSpec((TK, TMw), lambda i, j, k: (k, i)),
                  pl.BlockSpec((TK, TNw), lambda i, j, k: (k, j))],
        out_specs=out_spec, out_shape=out_shape,
        compiler_params=_params(("arbitrary", "arbitrary", "arbitrary"), blocks,
                                temps=[((TMw, TNw), F32), ((TK, TMw), BF16)]),
    )(a, b)


def _prev_rows(TM, H, col):
    return lambda i: (jnp.maximum(i * (TM // H) - 1, 0), col)


def _next_rows(S, TM, H, col):
    return lambda i: (jnp.minimum((i + 1) * (TM // H), S // H - 1), col)


def _taps_causal(ext_ref, w_ref, K, H, TM, cs):
    acc = None
    for k in range(K):
        term = ext_ref[pl.ds(H - (K - 1) + k, TM), cs] * w_ref[pl.ds(k, 1), cs]
        acc = term if acc is None else acc + term
    return acc


def _taps_anticausal(ext_ref, w_ref, K, TM, cs):
    acc = None
    for k in range(K):
        term = ext_ref[pl.ds(K - 1 - k, TM), cs] * w_ref[pl.ds(k, 1), cs]
        acc = term if acc is None else acc + term
    return acc


def _tap_grads(ext_ref, g, K, H, TM, cs):
    return [_rsum(ext_ref[pl.ds(H - (K - 1) + k, TM), cs] * g) for k in range(K)]


def _mixer_fwd(z, conv_a_w, conv_a_b, ln_g, ln_b, conv_b_w, S, TM, A):
    H = 32
    KA, KB = conv_a_w.shape[0], conv_b_w.shape[0]
    n_chunks = A // LANES
    RB = _pick(TM, (64, 32))

    def body(zc_ref, zh_ref, wa_ref, ba_ref, g_ref, b_ref, wb_ref, a1_ref, cat_ref, ext_a, ext_b):
        i = pl.program_id(0)
        live = (i > 0).astype(F32)
        zc = zc_ref[...].astype(F32)
        zh = zh_ref[...].astype(F32) * live
        ext_a[pl.ds(0, H), :] = zh[:, 0:A] * _sigmoid(zh[:, A:2 * A])
        ext_a[pl.ds(H, TM), :] = zc[:, 0:A] * _sigmoid(zc[:, A:2 * A])
        ext_b[pl.ds(0, H), :] = zh[:, 3 * A:4 * A] * zh[:, 4 * A:5 * A]
        ext_b[pl.ds(H, TM), :] = zc[:, 3 * A:4 * A] * zc[:, 4 * A:5 * A]

        def chunk(c, carry):
            cs = pl.ds(pl.multiple_of(c * LANES, LANES), LANES)
            for r0 in range(0, TM, RB):
                acc = None
                for k in range(KA):
                    term = ext_a[pl.ds(H - (KA - 1) + k + r0, RB), cs] * wa_ref[pl.ds(k, 1), cs]
                    acc = term if acc is None else acc + term
                a1_ref[pl.ds(r0, RB), cs] = acc + ba_ref[:, cs]
            return carry
        lax.fori_loop(0, n_chunks, chunk, 0)

        a1 = a1_ref[...]
        mu = jnp.mean(a1, axis=-1, keepdims=True)
        d = a1 - mu
        var = jnp.mean(d * d, axis=-1, keepdims=True)
        a2 = d * lax.rsqrt(var + EPS) * g_ref[...] + b_ref[...]
        cat_ref[:, 0:A] = (a2 * _sigmoid(a2)).astype(BF16)
        cbc = _taps_causal(ext_b, wb_ref, KB, H, TM, slice(None))
        cat_ref[:, A:2 * A] = (zc[:, 2 * A:3 * A] * cbc).astype(BF16)

    blocks = [((TM, 5 * A), BF16), ((H, 5 * A), BF16), ((KA, A), F32), ((KB, A), F32),
              ((TM, A), F32), ((TM, 2 * A), BF16)]
    scratch = [((H + TM, A), F32), ((H + TM, A), F32)]
    vec = lambda r: pl.BlockSpec((r, A), lambda i: (0, 0))
    return pl.pallas_call(
        body, name="mixer_fwd", grid=(S // TM,),
        in_specs=[pl.BlockSpec((TM, 5 * A), lambda i: (i, 0)),
                  pl.BlockSpec((H, 5 * A), _prev_rows(TM, H, 0)),
                  vec(KA), vec(1), vec(1), vec(1), vec(KB)],
        out_specs=[pl.BlockSpec((TM, A), lambda i: (i, 0)), pl.BlockSpec((TM, 2 * A), lambda i: (i, 0))],
        out_shape=[jax.ShapeDtypeStruct((S, A), F32), jax.ShapeDtypeStruct((S, 2 * A), BF16)],
        scratch_shapes=[pltpu.VMEM(s, d) for s, d in scratch],
        compiler_params=_params(("arbitrary",), blocks, scratch,
                                temps=[((TM, 5 * A), F32)] * 2 + [((TM, A), F32)] * 10),
    )(z, z, conv_a_w, conv_a_b, ln_g, ln_b, conv_b_w)


def _pair_tile(nF):
    return lambda t: (t % 2) * nF + t // 2


FFN_ROWS = 16


def _bcast_taps(w_ref, K, lanes):
    return [jnp.broadcast_to(w_ref[pl.ds(k, 1), lanes], (FFN_ROWS, LANES)) for k in range(K)]


def _ffn_act(u0, conv_w, S, TM, F, TC):
    H = 16
    K = conv_w.shape[0]
    nF = F // TC

    def body(uc_ref, uh_ref, wg_ref, wu_ref, o_ref, ext):
        live = (pl.program_id(0) > 0).astype(F32)
        ext[pl.ds(0, H), :] = uh_ref[...].astype(F32) * live
        ext[pl.ds(H, TM), :] = uc_ref[...].astype(F32)

        def lane_chunk(c, carry):
            lo = pl.ds(pl.multiple_of(c * LANES, LANES), LANES)
            lg, lu = lo, pl.ds(pl.multiple_of(TC + c * LANES, LANES), LANES)
            wg, wu = _bcast_taps(wg_ref, K, lo), _bcast_taps(wu_ref, K, lo)
            for r0 in range(0, TM, FFN_ROWS):
                g = u = None
                for k in range(K):
                    rows = pl.ds(H - (K - 1) + k + r0, FFN_ROWS)
                    tg, tu = ext[rows, lg] * wg[k], ext[rows, lu] * wu[k]
                    g, u = (tg, tu) if g is None else (g + tg, u + tu)
                o_ref[pl.ds(r0, FFN_ROWS), lo] = (g * _sigmoid(g) * u).astype(BF16)
            return carry
        lax.fori_loop(0, TC // LANES, lane_chunk, 0)

    blocks = [((TM, 2 * TC), BF16), ((H, 2 * TC), BF16), ((K, TC), F32), ((K, TC), F32), ((TM, TC), BF16)]
    scratch = [((H + TM, 2 * TC), F32)]
    return pl.pallas_call(
        body, name="ffn_act", grid=(S // TM, nF),
        in_specs=[pl.BlockSpec((TM, 2 * TC), lambda i, j: (i, j)),
                  pl.BlockSpec((H, 2 * TC), lambda i, j: (jnp.maximum(i * (TM // H) - 1, 0), j)),
                  pl.BlockSpec((K, TC), lambda i, j: (0, j)),
                  pl.BlockSpec((K, TC), lambda i, j: (0, j + nF))],
        out_specs=pl.BlockSpec((TM, TC), lambda i, j: (i, j)),
        out_shape=jax.ShapeDtypeStruct((S, F), BF16),
        scratch_shapes=[pltpu.VMEM(s, d) for s, d in scratch],
        compiler_params=_params(("arbitrary", "arbitrary"), blocks, scratch, temps=[((TM, 2 * TC), F32)]),
    )(u0, u0, conv_w, conv_w)


def _loss_head(h3, target, g_final, S, TM, D):
    def body(h_ref, t_ref, g_ref, loss_ref, dg_ref, dh_ref):
        @pl.when(pl.program_id(0) == 0)
        def _():
            loss_ref[...] = jnp.zeros_like(loss_ref)
            dg_ref[...] = jnp.zeros_like(dg_ref)
        h = h_ref[...]
        g = g_ref[...]
        r = _rms_stats(h)
        n = h * r
        diff = n * g - t_ref[...]
        loss_ref[...] += 0.5 * jnp.sum(jnp.mean(diff * diff, axis=-1, keepdims=True), axis=0, keepdims=True)
        dy = diff * (1.0 / D)
        dn = dy * g
        dh_ref[...] = r * (dn - n * jnp.mean(dn * n, axis=-1, keepdims=True))
        dg_ref[...] += _rsum(dy * n)

    blocks = [((TM, D), F32)] * 3 + [((1, D), F32)] * 2
    row = pl.BlockSpec((TM, D), lambda i: (i, 0))
    return pl.pallas_call(
        body, name="loss_head", grid=(S // TM,),
        in_specs=[row, row, pl.BlockSpec((1, D), lambda i: (0, 0))],
        out_specs=[pl.BlockSpec((1, 1), lambda i: (0, 0)), pl.BlockSpec((1, D), lambda i: (0, 0)), row],
        out_shape=[jax.ShapeDtypeStruct((1, 1), F32), jax.ShapeDtypeStruct((1, D), F32),
                   jax.ShapeDtypeStruct((S, D), F32)],
        compiler_params=_params(("arbitrary",), blocks, temps=[((TM, D), F32)] * 8),
    )(h3, target, g_final)


def _ffn_bwd(u0, dact, conv_w, S, TM, F, TC, side=None):
    H = FFN_ROWS
    K = conv_w.shape[0]
    nF, nI = F // TC, S // TM

    def body(up_ref, uc_ref, un_ref, dc_ref, dn_ref, wg_ref, wu_ref, o_ref, dwg_ref, dwu_ref,
             ext_u, ext_d, ext_a):
        i = pl.program_id(1)
        @pl.when(i == 0)
        def _():
            dwg_ref[...] = jnp.zeros_like(dwg_ref)
            dwu_ref[...] = jnp.zeros_like(dwu_ref)
        last = (i < nI - 1).astype(F32)
        ext_u[pl.ds(0, H), :] = up_ref[...].astype(F32) * (i > 0).astype(F32)
        ext_u[pl.ds(H, TM), :] = uc_ref[...].astype(F32)
        ext_u[pl.ds(H + TM, H), :] = un_ref[...].astype(F32) * last
        ext_a[pl.ds(0, TM), :] = dc_ref[...].astype(F32)
        ext_a[pl.ds(TM, H), :] = dn_ref[...].astype(F32) * last

        def lane_chunk(c, carry):
            lo = pl.ds(pl.multiple_of(c * LANES, LANES), LANES)
            lg, lu = lo, pl.ds(pl.multiple_of(TC + c * LANES, LANES), LANES)
            wg, wu = _bcast_taps(wg_ref, K, lo), _bcast_taps(wu_ref, K, lo)
            sums_g, sums_u = [None] * K, [None] * K
            for r0 in range(0, TM + H, FFN_ROWS):
                xg = [ext_u[pl.ds(H - (K - 1) + k + r0, FFN_ROWS), lg] for k in range(K)]
                xu = [ext_u[pl.ds(H - (K - 1) + k + r0, FFN_ROWS), lu] for k in range(K)]
                g, u = xg[0] * wg[0], xu[0] * wu[0]
                for k in range(1, K):
                    g, u = g + xg[k] * wg[k], u + xu[k] * wu[k]
                da = ext_a[pl.ds(r0, FFN_ROWS), lo]
                s = _sigmoid(g)
                dg = da * u * s * (1.0 + g * (1.0 - s))
                du = da * g * s
                ext_d[pl.ds(r0, FFN_ROWS), lg] = dg
                ext_d[pl.ds(r0, FFN_ROWS), lu] = du
                if r0 < TM:
                    for k in range(K):
                        tg, tu = xg[k] * dg, xu[k] * du
                        sums_g[k] = tg if sums_g[k] is None else sums_g[k] + tg
                        sums_u[k] = tu if sums_u[k] is None else sums_u[k] + tu
            for k in range(K):
                dwg_ref[pl.ds(k, 1), lo] += _rsum(sums_g[k])
                dwu_ref[pl.ds(k, 1), lo] += _rsum(sums_u[k])
            for r0 in range(0, TM, FFN_ROWS):
                g = u = None
                for k in range(K):
                    rows = pl.ds(K - 1 - k + r0, FFN_ROWS)
                    tg, tu = ext_d[rows, lg] * wg[k], ext_d[rows, lu] * wu[k]
                    g, u = (tg, tu) if g is None else (g + tg, u + tu)
                o_ref[pl.ds(r0, FFN_ROWS), lg] = g.astype(BF16)
                o_ref[pl.ds(r0, FFN_ROWS), lu] = u.astype(BF16)
            return carry
        lax.fori_loop(0, TC // LANES, lane_chunk, 0)

    blocks = [((H, 2 * TC), BF16), ((TM, 2 * TC), BF16), ((H, 2 * TC), BF16), ((TM, TC), BF16), ((H, TC), BF16),
              ((K, TC), F32), ((K, TC), F32), ((TM, 2 * TC), BF16), ((K, TC), F32), ((K, TC), F32)]
    scratch = [((TM + 2 * H, 2 * TC), F32), ((TM + H, 2 * TC), F32), ((TM + H, TC), F32)]
    prev = lambda j, i: (jnp.maximum(i * (TM // H) - 1, 0), j)
    nxt = lambda j, i: (jnp.minimum((i + 1) * (TM // H), S // H - 1), j)
    taps_out = pl.BlockSpec((K, TC), lambda j, i: (0, j))
    outs, side_outs = _call(
        body, side, name="ffn_bwd", grid=(nF, nI),
        in_specs=[pl.BlockSpec((H, 2 * TC), prev), pl.BlockSpec((TM, 2 * TC), lambda j, i: (i, j)),
                  pl.BlockSpec((H, 2 * TC), nxt),
                  pl.BlockSpec((TM, TC), lambda j, i: (i, j)), pl.BlockSpec((H, TC), nxt),
                  pl.BlockSpec((K, TC), lambda j, i: (0, j)), pl.BlockSpec((K, TC), lambda j, i: (0, j + nF))],
        out_specs=[pl.BlockSpec((TM, 2 * TC), lambda j, i: (i, j)), taps_out, taps_out],
        out_shape=[jax.ShapeDtypeStruct((S, 2 * F), BF16), jax.ShapeDtypeStruct((K, F), F32),
                   jax.ShapeDtypeStruct((K, F), F32)],
        scratch=scratch,
        params=_params(("arbitrary", "arbitrary"), blocks, scratch, temps=[((TM, 2 * TC), F32)]),
        args=[u0, u0, u0, dact, dact, conv_w, conv_w])
    return outs if side is None else (outs, side_outs)


def _mixer_bwd_ln(dcat, a1, ln_g, ln_b, S, TM, A):
    def body(dc_ref, a1_ref, g_ref, b_ref, da1_ref, acc_ref):
        @pl.when(pl.program_id(0) == 0)
        def _():
            acc_ref[...] = jnp.zeros_like(acc_ref)
        a1 = a1_ref[...]
        g = g_ref[...]
        mu = jnp.mean(a1, axis=-1, keepdims=True)
        d = a1 - mu
        rstd = lax.rsqrt(jnp.mean(d * d, axis=-1, keepdims=True) + EPS)
        nh = d * rstd
        a2 = nh * g + b_ref[...]
        s = _sigmoid(a2)
        da2 = dc_ref[...].astype(F32) * s * (1.0 + a2 * (1.0 - s))
        dnh = da2 * g
        da1 = rstd * (dnh - jnp.mean(dnh, axis=-1, keepdims=True)
                      - nh * jnp.mean(dnh * nh, axis=-1, keepdims=True))
        da1_ref[...] = da1
        acc_ref[pl.ds(0, 1), :] += _rsum(da2 * nh)
        acc_ref[pl.ds(1, 1), :] += _rsum(da2)
        acc_ref[pl.ds(2, 1), :] += _rsum(da1)

    blocks = [((TM, A), BF16), ((TM, A), F32), ((TM, A), F32), ((4, A), F32)]
    return pl.pallas_call(
        body, name="mixer_bwd_ln", grid=(S // TM,),
        in_specs=[pl.BlockSpec((TM, A), lambda i: (i, 0)), pl.BlockSpec((TM, A), lambda i: (i, 0)),
                  pl.BlockSpec((1, A), lambda i: (0, 0)), pl.BlockSpec((1, A), lambda i: (0, 0))],
        out_specs=[pl.BlockSpec((TM, A), lambda i: (i, 0)), pl.BlockSpec((4, A), lambda i: (0, 0))],
        out_shape=[jax.ShapeDtypeStruct((S, A), F32), jax.ShapeDtypeStruct((4, A), F32)],
        compiler_params=_params(("arbitrary",), blocks, temps=[((TM, A), F32)] * 12),
    )(dcat, a1, ln_g, ln_b)


def _mixer_bwd_conv(z, dcat, da1, conv_a_w, conv_b_w, S, TM, A, side=None):
    H = 32
    KA, KB = conv_a_w.shape[0], conv_b_w.shape[0]
    nI = S // TM
    n_chunks = A // LANES
    RB = _pick(TM, (64, 32))

    def body(zc_ref, zp_ref, zn_ref, dbc_ref, dbn_ref, d1c_ref, d1n_ref, wa_ref, wb_ref,
             dz_ref, dwa_ref, dwb_ref, ext_a0, ext_d1, ext_cb, ext_dc, da0_sc):
        i = pl.program_id(0)
        @pl.when(i == 0)
        def _():
            dwa_ref[...] = jnp.zeros_like(dwa_ref)
            dwb_ref[...] = jnp.zeros_like(dwb_ref)
        first = (i > 0).astype(F32)
        last = (i < nI - 1).astype(F32)
        zc = zc_ref[...].astype(F32)
        zp = zp_ref[...].astype(F32) * first
        a_val, a_gate = zc[:, 0:A], zc[:, A:2 * A]
        b_gate, c_gate, b_h = zc[:, 2 * A:3 * A], zc[:, 3 * A:4 * A], zc[:, 4 * A:5 * A]
        sig = _sigmoid(a_gate)
        ext_a0[pl.ds(0, H), :] = zp[:, 0:A] * _sigmoid(zp[:, A:2 * A])
        ext_a0[pl.ds(H, TM), :] = a_val * sig
        ext_d1[pl.ds(0, TM), :] = d1c_ref[...]
        ext_d1[pl.ds(TM, H), :] = d1n_ref[...] * last
        ext_cb[pl.ds(0, H), :] = zp[:, 3 * A:4 * A] * zp[:, 4 * A:5 * A]
        ext_cb[pl.ds(H, TM), :] = c_gate * b_h
        dbx = dbc_ref[...].astype(F32)
        dcbc = dbx * b_gate
        ext_dc[pl.ds(0, TM), :] = dcbc
        ext_dc[pl.ds(TM, H), :] = dbn_ref[...].astype(F32) * zn_ref[...].astype(F32) * last

        def chunk(c, carry):
            cs = pl.ds(pl.multiple_of(c * LANES, LANES), LANES)
            for r0 in range(0, TM, RB):
                acc = None
                for k in range(KA):
                    term = ext_d1[pl.ds(KA - 1 - k + r0, RB), cs] * wa_ref[pl.ds(k, 1), cs]
                    acc = term if acc is None else acc + term
                da0_sc[pl.ds(r0, RB), cs] = acc
            for k in range(KA):
                acc = None
                for r0 in range(0, TM, RB):
                    term = ext_a0[pl.ds(H - (KA - 1) + k + r0, RB), cs] * ext_d1[pl.ds(r0, RB), cs]
                    acc = term if acc is None else acc + term
                dwa_ref[pl.ds(k, 1), cs] += _rsum(acc)
            return carry
        lax.fori_loop(0, n_chunks, chunk, 0)

        da0 = da0_sc[...]
        dz_ref[:, 0:A] = (da0 * sig).astype(BF16)
        dz_ref[:, A:2 * A] = (da0 * a_val * sig * (1.0 - sig)).astype(BF16)
        cbc = _taps_causal(ext_cb, wb_ref, KB, H, TM, slice(None))
        dz_ref[:, 2 * A:3 * A] = (dbx * cbc).astype(BF16)
        dcb = _taps_anticausal(ext_dc, wb_ref, KB, TM, slice(None))
        dz_ref[:, 3 * A:4 * A] = (dcb * b_h).astype(BF16)
        dz_ref[:, 4 * A:5 * A] = (dcb * c_gate).astype(BF16)
        grads = _tap_grads(ext_cb, dcbc, KB, H, TM, slice(None))
        for k in range(KB):
            dwb_ref[pl.ds(k, 1), :] += grads[k]

    blocks = [((TM, 5 * A), BF16), ((H, 5 * A), BF16), ((H, A), BF16), ((TM, A), BF16), ((H, A), BF16),
              ((TM, A), F32), ((H, A), F32), ((KA, A), F32), ((KB, A), F32),
              ((TM, 5 * A), BF16), ((KA, A), F32), ((KB, A), F32)]
    scratch = [((H + TM, A), F32)] * 4 + [((TM, A), F32)]
    vec = lambda r: pl.BlockSpec((r, A), lambda i: (0, 0))
    outs, side_outs = _call(
        body, side, name="mixer_bwd_conv", grid=(nI,),
        in_specs=[pl.BlockSpec((TM, 5 * A), lambda i: (i, 0)),
                  pl.BlockSpec((H, 5 * A), _prev_rows(TM, H, 0)),
                  pl.BlockSpec((H, A), _next_rows(S, TM, H, 2)),
                  pl.BlockSpec((TM, A), lambda i: (i, 1)),
                  pl.BlockSpec((H, A), _next_rows(S, TM, H, 1)),
                  pl.BlockSpec((TM, A), lambda i: (i, 0)),
                  pl.BlockSpec((H, A), _next_rows(S, TM, H, 0)),
                  vec(KA), vec(KB)],
        out_specs=[pl.BlockSpec((TM, 5 * A), lambda i: (i, 0)), vec(KA), vec(KB)],
        out_shape=[jax.ShapeDtypeStruct((S, 5 * A), BF16), jax.ShapeDtypeStruct((KA, A), F32),
                   jax.ShapeDtypeStruct((KB, A), F32)],
        scratch=scratch,
        params=_params(("arbitrary",), blocks, scratch, temps=[((TM, 5 * A), F32)] * 2 + [((TM, A), F32)] * 14),
        args=[z, z, z, dcat, dcat, da1, da1, conv_a_w, conv_b_w])
    return outs if side is None else (outs, side_outs)


def _row_tile(R):
    return _pick(R, (256, 128, 64, 32, 16, 8))


def _scalars(*vals):
    return jnp.stack([jnp.asarray(v, jnp.int32) for v in vals])


def _cast_into_gathered(name, w, chip):
    R, C = w.shape
    TR = _row_tile(R)

    def body(s_ref, w_ref, o_ref):
        o_ref[...] = w_ref[...].astype(BF16)

    grid_spec = pltpu.PrefetchScalarGridSpec(
        num_scalar_prefetch=1, grid=(R // TR,),
        in_specs=[pl.BlockSpec((TR, C), lambda r, s: (r, 0))],
        out_specs=pl.BlockSpec((None, TR, C), lambda r, s: (s[0], r, 0)))
    return pl.pallas_call(body, name=name, grid_spec=grid_spec,
                          out_shape=jax.ShapeDtypeStruct((N_CHIPS, R, C), BF16),
                          compiler_params=_params(("arbitrary",), [((TR, C), F32), ((TR, C), BF16)]),
                          )(_scalars(chip), w)


def _add_pair(name, dw, recv, c):
    _, _, Rh, C = dw.shape
    TR = _row_tile(Rh)

    def body(c_ref, a_ref, b_ref, o_ref, ob_ref):
        s = a_ref[...] + b_ref[...]
        o_ref[...] = s
        ob_ref[...] = s.astype(BF16)

    out_spec = pl.BlockSpec((None, TR, C), lambda k, r, c_ref: (k, r, 0))
    grid_spec = pltpu.PrefetchScalarGridSpec(
        num_scalar_prefetch=1, grid=(N_CHIPS, Rh // TR),
        in_specs=[pl.BlockSpec((None, None, TR, C), lambda k, r, c_ref: (k, c_ref[0], r, 0)),
                  pl.BlockSpec((None, TR, C), lambda k, r, c_ref: (k, r, 0))],
        out_specs=[out_spec, out_spec])
    return pl.pallas_call(body, name=name, grid_spec=grid_spec,
                          out_shape=[jax.ShapeDtypeStruct((N_CHIPS, Rh, C), F32),
                                     jax.ShapeDtypeStruct((N_CHIPS, Rh, C), BF16)],
                          compiler_params=_params(("arbitrary", "arbitrary"), [((TR, C), F32)] * 4),
                          )(_scalars(c), dw, recv)


def _add_chips(name, parts, recv, chip, c):
    _, Rh, C = parts.shape
    TR = _row_tile(Rh)

    def body(s_ref, p_ref, r_ref, o_ref):
        o_ref[...] = ((p_ref[...] + r_ref[0].astype(F32)) + r_ref[1].astype(F32)) + r_ref[2].astype(F32)

    grid_spec = pltpu.PrefetchScalarGridSpec(
        num_scalar_prefetch=1, grid=(Rh // TR,),
        in_specs=[pl.BlockSpec((None, TR, C), lambda r, s: (s[0], r, 0)),
                  pl.BlockSpec((N_CHIPS - 1, TR, C), lambda r, s: (0, r, 0))],
        out_specs=pl.BlockSpec((None, TR, C), lambda r, s: (s[1], r, 0)))
    return pl.pallas_call(body, name=name, grid_spec=grid_spec,
                          out_shape=jax.ShapeDtypeStruct((2, Rh, C), F32),
                          compiler_params=_params(("arbitrary",), [((N_CHIPS + 1, TR, C), F32)]),
                          )(_scalars(chip, c), parts, recv)


def _sum_devices(name, parts):
    _, R, C = parts.shape

    def body(p_ref, o_ref):
        acc = p_ref[0]
        for d in range(1, N_DEV):
            acc = acc + p_ref[d]
        o_ref[...] = acc

    return pl.pallas_call(body, name=name, out_shape=jax.ShapeDtypeStruct((R, C), F32),
                          in_specs=[pl.BlockSpec(memory_space=pltpu.VMEM)],
                          out_specs=pl.BlockSpec(memory_space=pltpu.VMEM))(parts)


def _adamw(name, w, g, m, v):
    R, C = w.shape
    TR = _pick(R, (128, 64, 32, 16, 8))
    c1 = 1.0 - ADAM_B1 ** ADAM_STEP
    c2 = 1.0 - ADAM_B2 ** ADAM_STEP

    def body(w_ref, g_ref, m_ref, v_ref, d_ref, nm_ref, nv_ref):
        g_ = g_ref[...]
        nm = ADAM_B1 * m_ref[...] + (1.0 - ADAM_B1) * g_
        nv = ADAM_B2 * v_ref[...] + (1.0 - ADAM_B2) * (g_ * g_)
        d_ref[...] = -ADAM_LR * ((nm / c1) / (jnp.sqrt(nv / c2) + ADAM_EPS) + ADAM_WD * w_ref[...])
        nm_ref[...] = nm
        nv_ref[...] = nv

    spec = pl.BlockSpec((TR, C), lambda r: (r, 0))
    shp = jax.ShapeDtypeStruct((R, C), F32)
    return pl.pallas_call(body, name=name, grid=(R // TR,), in_specs=[spec] * 4, out_specs=[spec] * 3,
                          out_shape=[shp] * 3,
                          compiler_params=_params(("arbitrary",), [((TR, C), F32)] * 7))(w, g, m, v)


def _place():
    x, y, c = lax.axis_index("x"), lax.axis_index("y"), lax.axis_index("c")
    others = [(1 - x, y), (x, 1 - y), (1 - x, 1 - y)]
    return x, y, c, others


def _allgather_small(name, block):
    R, C = block.shape

    def body(x_ref, out_ref, send_sems, recv_sems, local_sem):
        x, y, c, chips = _place()
        me, sibling = (x, y, c), (x, y, 1 - c)

        def rows(px, py, pc):
            return out_ref.at[4 * px + 2 * py + pc]

        def copy(k, blk, to, src=None):
            return pltpu.make_async_remote_copy(
                src_ref=rows(*blk) if src is None else src, dst_ref=rows(*blk),
                send_sem=send_sems.at[k], recv_sem=recv_sems.at[k], device_id=to, device_id_type=MESH)

        mine = pltpu.make_async_copy(x_ref, rows(*me), local_sem)
        mine.start()
        first = [copy(0, me, sibling, src=x_ref)]
        first += [copy(1 + j, me, (*chip, c), src=x_ref) for j, chip in enumerate(chips)]
        for cp in first:
            cp.start()
        passed = [copy(4 + j, (*chip, c), sibling) for j, chip in enumerate(chips)]
        for j, chip in enumerate(chips):
            copy(1 + j, (*chip, c), me).wait_recv()
            passed[j].start()
        copy(0, sibling, me).wait_recv()
        for j, chip in enumerate(chips):
            copy(4 + j, (*chip, 1 - c), me).wait_recv()
        for cp in first + passed:
            cp.wait_send()
        mine.wait()

    return pl.pallas_call(
        body, name=name, out_shape=jax.ShapeDtypeStruct((N_DEV, R, C), F32),
        in_specs=[pl.BlockSpec(memory_space=pltpu.VMEM)], out_specs=pl.BlockSpec(memory_space=pltpu.VMEM),
        scratch_shapes=[pltpu.SemaphoreType.DMA((7,)), pltpu.SemaphoreType.DMA((7,)), pltpu.SemaphoreType.DMA],
    )(block)


def _allgather_weights(bufs):
    n = len(bufs)

    def body(*refs):
        ins, outs = refs[:n], refs[n:2 * n]
        send_sems, recv_sems = refs[2 * n:]
        x, y, c, chips = _place()
        me, sibling = (x, y, c), (x, y, 1 - c)

        def half_of(ref, chip, half):
            rh = ref.shape[1] // 2
            return ref.at[2 * chip[0] + chip[1], pl.ds(half * rh, rh)]

        def copy(a, k, src, dst, to):
            return pltpu.make_async_remote_copy(
                src_ref=src, dst_ref=dst, send_sem=send_sems.at[6 * a + k], recv_sem=recv_sems.at[6 * a + k],
                device_id=to, device_id_type=MESH)

        first, passed = [], []
        for a in range(n):
            first.append([copy(a, j, half_of(ins[a], (x, y), c), half_of(outs[a], (x, y), c), (*chip, c))
                          for j, chip in enumerate(chips)])
            for cp in first[a]:
                cp.start()
        for a in range(n):
            passed.append([copy(a, 3 + j, half_of(outs[a], chip, c), half_of(outs[a], chip, c), sibling)
                           for j, chip in enumerate(chips)])
            for j, chip in enumerate(chips):
                copy(a, j, half_of(outs[a], chip, c), half_of(outs[a], chip, c), me).wait_recv()
                passed[a][j].start()
        for a in range(n):
            for j, chip in enumerate(chips):
                copy(a, 3 + j, half_of(outs[a], chip, 1 - c), half_of(outs[a], chip, 1 - c), me).wait_recv()
            for cp in first[a] + passed[a]:
                cp.wait_send()

    return pl.pallas_call(
        body, name="allgather_weights",
        out_shape=[jax.ShapeDtypeStruct(b.shape, BF16) for b in bufs],
        in_specs=[ANY] * n, out_specs=[ANY] * n, input_output_aliases={a: a for a in range(n)},
        scratch_shapes=[pltpu.SemaphoreType.DMA((6 * n,)), pltpu.SemaphoreType.DMA((6 * n,))],
    )(*bufs)


def _exchange_pairs(name, grads):
    n = len(grads)

    def body(*refs):
        ins, outs = refs[:n], refs[n:2 * n]
        send_sems, recv_sems = refs[2 * n:]
        x, y, c, _ = _place()
        copies = [pltpu.make_async_remote_copy(
            src_ref=ins[a].at[:, 1 - c], dst_ref=outs[a], send_sem=send_sems.at[a], recv_sem=recv_sems.at[a],
            device_id=(x, y, 1 - c), device_id_type=MESH) for a in range(n)]
        for cp in copies:
            cp.start()
        for cp in copies:
            cp.wait()

    return pl.pallas_call(
        body, name=name,
        out_shape=[jax.ShapeDtypeStruct((N_CHIPS,) + g.shape[2:], F32) for g in grads],
        in_specs=[ANY] * n, out_specs=[ANY] * n,
        scratch_shapes=[pltpu.SemaphoreType.DMA((n,)), pltpu.SemaphoreType.DMA((n,))],
    )(*grads)


def _chip_exchange(parts):
    n = len(parts)

    def copies(ins, outs, send_sems, recv_sems):
        x, y, c, chips = _place()
        return [pltpu.make_async_remote_copy(
            src_ref=ins[a].at[2 * chip[0] + chip[1]], dst_ref=outs[a].at[j],
            send_sem=send_sems.at[3 * a + j], recv_sem=recv_sems.at[3 * a + j],
            device_id=(*chip, c), device_id_type=MESH) for a in range(n) for j, chip in enumerate(chips)]

    def start(*refs):
        for cp in copies(*refs):
            cp.start()

    def wait(*refs):
        cps = copies(*refs)
        for cp in cps:
            cp.wait_recv()
        for cp in cps:
            cp.wait_send()

    return _Side(list(parts), [jax.ShapeDtypeStruct((N_CHIPS - 1,) + p.shape[1:], p.dtype) for p in parts],
                 3 * n, start, wait)


def _share_halves(halves):
    n = len(halves)

    def body(*refs):
        ins, outs = refs[:n], refs[n:2 * n]
        send_sems, recv_sems = refs[2 * n:]
        x, y, c, _ = _place()
        sends = []
        for a in range(n):
            sends.append(pltpu.make_async_remote_copy(
                src_ref=ins[a].at[c], dst_ref=outs[a].at[c], send_sem=send_sems.at[a], recv_sem=recv_sems.at[a],
                device_id=(x, y, 1 - c), device_id_type=MESH))
            sends[a].start()
        for a in range(n):
            theirs = outs[a].at[1 - c]
            pltpu.make_async_remote_copy(
                src_ref=theirs, dst_ref=theirs, send_sem=send_sems.at[a], recv_sem=recv_sems.at[a],
                device_id=(x, y, c), device_id_type=MESH).wait_recv()
        for cp in sends:
            cp.wait_send()

    return pl.pallas_call(
        body, name="grads_share_halves",
        out_shape=[jax.ShapeDtypeStruct(h.shape, F32) for h in halves],
        in_specs=[ANY] * n, out_specs=[ANY] * n, input_output_aliases={a: a for a in range(n)},
        scratch_shapes=[pltpu.SemaphoreType.DMA((n,)), pltpu.SemaphoreType.DMA((n,))],
    )(*halves)


def _pack(arrays):
    pieces = []
    for a in arrays:
        flat = a.reshape(-1).astype(F32)
        pieces.append(jnp.pad(flat, (0, (-flat.size) % PACK_ALIGN)))
    return jnp.concatenate(pieces).reshape(-1, LANES)


def _unpack(buf, shapes):
    lead = buf.shape[:-2]
    flat = buf.reshape(lead + (-1,))
    out, off = [], 0
    for shp in shapes:
        size = 1
        for s in shp:
            size *= s
        out.append(flat[..., off:off + size].reshape(lead + tuple(shp)))
        off += size + (-size) % PACK_ALIGN
    return out


def _gather_channels(buf, shapes):
    per_chip = _unpack(buf[0::2], shapes)
    return [jnp.transpose(a, (1, 0, 2)).reshape(a.shape[1], -1) for a in per_chip]


def _mm_tile(n, rows, limit_bytes=6 * 1024 * 1024):
    for t in (1408, 1280, 1024, 640, 512, 384, 256, 128):
        if n % t == 0 and rows * t * 2 <= limit_bytes:
            return t
    raise ValueError(f"no column tile for {n} x {rows}")


def kernel(x, p, norm_mix_g, w_in, conv_a_w, conv_a_b, ln_a_g, ln_a_b, conv_b_w, w_out, norm_ffn_g, w_up, conv_ffn_w, w_down, w_ple_gate, b_ple_gate, w_ple_proj, norm_final_g, loss_target, m_norm_mix_g, m_w_in, m_conv_a_w, m_conv_a_b, m_ln_a_g, m_ln_a_b, m_conv_b_w, m_w_out, m_norm_ffn_g, m_w_up, m_conv_ffn_w, m_w_down, m_w_ple_gate, m_b_ple_gate, m_w_ple_proj, m_norm_final_g, v_norm_mix_g, v_w_in, v_conv_a_w, v_conv_a_b, v_ln_a_g, v_ln_a_b, v_conv_b_w, v_w_out, v_norm_ffn_g, v_w_up, v_conv_ffn_w, v_w_down, v_w_ple_gate, v_b_ple_gate, v_w_ple_proj, v_norm_final_g):
    S, D = x.shape[1], x.shape[2]
    P = p.shape[3]
    A = conv_a_b.shape[1]
    F = w_down.shape[1] * N_CHIPS
    KA, KB, KF = conv_a_w.shape[1], conv_b_w.shape[1], conv_ffn_w.shape[1]
    xi, yi, ci = lax.axis_index("x"), lax.axis_index("y"), lax.axis_index("c")
    chip = 2 * xi + yi

    TM = _pick(S, (512, 256, 128))
    TE = _pick(S, (256, 128))
    TC = _pick(2 * F // N_CHIPS, (1408, 1024, 512, 256, 128))
    ffn_place = _pair_tile(F // TC)

    x2, p2, t2 = x.reshape(S, D), p.reshape(S, P), loss_target.reshape(S, D)
    gfin = norm_final_g.reshape(1, D)

    big = dict(w_in=w_in[0], w_out=w_out[0], w_up=w_up[0], w_down=w_down[0],
               w_ple_gate=w_ple_gate[0], w_ple_proj=w_ple_proj[0])
    names = list(big)
    gathered = _allgather_weights([_cast_into_gathered("cast_" + n, big[n], chip) for n in names])
    w_in3, w_out3, w_up3, w_down3, w_gate3, w_proj3 = gathered
    w_out_f = w_out3.reshape(2 * A, D)
    w_down_f = w_down3.reshape(F, D)
    w_gate_f = w_gate3.reshape(D, D)

    tap_shapes = [(KA, A // N_CHIPS), (KB, A // N_CHIPS), (KF, 2 * F // N_CHIPS)]
    taps = _allgather_small("allgather_taps", _pack([conv_a_w[0], conv_b_w[0], conv_ffn_w[0]]))
    conv_a_f, conv_b_f, conv_ffn_f = _gather_channels(taps, tap_shapes)

    def rms_prologue(rows, row_r, vec_r, ro_r, ao_r):
        h = row_r[0][rows, :]
        hn = (h * _rms_stats(h) * vec_r[0][...]).astype(BF16)
        ro_r[0][rows, :] = hn
        return [hn]

    def cast_prologue(rows, row_r, vec_r, ro_r, ao_r):
        hb = row_r[0][rows, :].astype(BF16)
        ro_r[0][rows, :] = hb
        return [hb]

    plain = lambda accs, tile_r, cv_r: [accs[0]]
    residual = lambda accs, tile_r, cv_r: [tile_r[0][...] + accs[0]]

    z, hn1 = _rows_mm("in_proj", S, TM, 5 * A, _mm_tile(5 * A // N_CHIPS, D), row_ins=[x2], vec_ins=[norm_mix_g],
                      weights=[(w_in3, "nn3")], tile_outs=[BF16], row_outs=[(D, BF16)],
                      prologue=rms_prologue, epilogue=plain)
    a1, cat = _mixer_fwd(z, conv_a_f, conv_a_b, ln_a_g, ln_a_b, conv_b_f, S, TE, A)
    (h1,) = _rows_mm("out_proj", S, TM, D, _mm_tile(D, 2 * A), row_ins=[cat], weights=[(w_out_f, "nn2")],
                     tile_ins=[x2], tile_outs=[F32], epilogue=residual)
    u0, hn2 = _rows_mm("up_proj", S, TM, 2 * F, TC, row_ins=[h1], vec_ins=[norm_ffn_g],
                       weights=[(w_up3, "nn3")], tile_outs=[BF16], row_outs=[(D, BF16)],
                       prologue=rms_prologue, epilogue=plain, place=ffn_place)
    act = _ffn_act(u0, conv_ffn_f, S, TM, F, TC)
    (h2,) = _rows_mm("down_proj", S, TM, D, _mm_tile(D, F), row_ins=[act], weights=[(w_down_f, "nn2")],
                     tile_ins=[h1], tile_outs=[F32], epilogue=residual)

    def ple_prologue(rows, row_r, vec_r, ro_r, ao_r):
        hb = row_r[0][rows, :].astype(BF16)
        pb = row_r[1][rows, :].astype(BF16)
        ro_r[0][rows, :] = hb
        ro_r[1][rows, :] = pb
        return [hb, pb]

    def ple_epilogue(accs, tile_r, cv_r):
        gate = _sigmoid(accs[0] + cv_r[0][...])
        return [tile_r[0][...] + accs[1] * gate, gate, accs[1]]

    h3, gate, pp, h2b, pb = _rows_mm(
        "ple_fwd", S, TM, D, _mm_tile(D // N_CHIPS, D), row_ins=[h2, p2], colvec_ins=[b_ple_gate],
        weights=[(w_gate_f, "nn2"), (w_proj3, "nn3")], tile_ins=[h2], tile_outs=[F32, BF16, BF16],
        row_outs=[(D, BF16), (P, BF16)], prologue=ple_prologue, epilogue=ple_epilogue)
    loss_part, g_norm_final, dh3 = _loss_head(h3, t2, gfin, S, TE, D)

    def ple_bwd_prologue(rows, row_r, vec_r, ro_r, ao_r):
        d = row_r[0][rows, :]
        gt = row_r[2][rows, :].astype(F32)
        dpre = d * row_r[1][rows, :].astype(F32) * gt * (1.0 - gt)
        ro_r[0][rows, :] = dpre.astype(BF16)
        ro_r[1][rows, :] = (d * gt).astype(BF16)
        ao_r[0][...] += _rsum(dpre)
        return [dpre.astype(BF16)]

    dh2, dpre, dpp, g_b_gate = _rows_mm(
        "ple_bwd", S, TM, D, _mm_tile(D, D), row_ins=[dh3, pp, gate], weights=[(w_gate_f, "nt2")],
        tile_ins=[dh3], tile_outs=[F32], row_outs=[(D, BF16), (D, BF16)], acc_outs=[(1, D)],
        prologue=ple_bwd_prologue, epilogue=residual)
    TK = _pick(S, (1024, 512, 256, 128))
    wt = lambda n: _pick(n, (1408, 1280, 1024, 512, 256, 128))
    chip_sums, from_chips = {}, {}

    def pair_reduce(tag, parts):
        ns = list(parts)
        halves = [parts[n].reshape(N_CHIPS, 2, big[n].shape[0] // 2, big[n].shape[1]) for n in ns]
        from_sibling = _exchange_pairs("grads_exchange_pairs_" + tag, halves)
        sums = [_add_pair("pair_sum_" + n, h, r, ci) for n, h, r in zip(ns, halves, from_sibling)]
        for n, (s, _) in zip(ns, sums):
            chip_sums[n] = s
        return ns, _chip_exchange([b for _, b in sums])

    def landed(ns, side_outs):
        for n, r in zip(ns, side_outs):
            from_chips[n] = r

    ns, side = pair_reduce("ple", dict(
        w_ple_gate=_tn_mm("dw_ple_gate", h2b, dpre, wt(D), wt(D), TK),
        w_ple_proj=_tn_mm("dw_ple_proj", pb, dpp, wt(P), wt(D // N_CHIPS), TK, cols_per_chip=D // N_CHIPS)))
    (dact, dh2b), got = _rows_mm("down_bwd", S, TM, F, _mm_tile(F, D), row_ins=[dh2], weights=[(w_down_f, "nt2")],
                                 tile_outs=[BF16], row_outs=[(D, BF16)], prologue=cast_prologue, epilogue=plain,
                                 side=side)
    landed(ns, got)
    ns, side = pair_reduce("down", dict(w_down=_tn_mm("dw_down", act, dh2b, wt(F), wt(D), TK)))
    (du0, g_conv_gate, g_conv_up), got = _ffn_bwd(u0, dact, conv_ffn_f, S, TM, F, TC, side=side)
    landed(ns, got)
    g_conv_ffn = jnp.concatenate([g_conv_gate, g_conv_up], axis=1)
    ns, side = pair_reduce("up", dict(
        w_up=_tn_mm("dw_up", hn2, du0, wt(D), TC, TK, cols_per_chip=2 * F // N_CHIPS, place=ffn_place)))

    def up_bwd_epilogue(acc, rows, row_r, vec_r, ro_r, ao_r):
        dh, dg = _rms_bwd(row_r[0][rows, :], vec_r[0][...], acc)
        dh1_ = row_r[1][rows, :] + dh
        ro_r[0][rows, :] = dh1_
        ro_r[1][rows, :] = dh1_.astype(BF16)
        ao_r[0][...] += dg

    (dh1, dh1b, g_norm_ffn), got = _kloop_mm(
        "up_bwd", S, TM, du0, w_up3, TC, row_ins=[h1, dh2], vec_ins=[norm_ffn_g],
        row_outs=[(D, F32), (D, BF16)], acc_outs=[(1, D)], epilogue=up_bwd_epilogue, place=ffn_place, side=side)
    landed(ns, got)
    ns, side = pair_reduce("out", dict(w_out=_tn_mm("dw_out", cat, dh1b, wt(2 * A), wt(D), TK)))
    (dcat,) = _rows_mm("out_bwd", S, TM, 2 * A, _mm_tile(2 * A, D), row_ins=[dh1b], weights=[(w_out_f, "nt2")],
                       tile_outs=[BF16], epilogue=plain)
    da1, ln_sums = _mixer_bwd_ln(dcat, a1, ln_a_g, ln_a_b, S, TE, A)
    (dz, g_conv_a, g_conv_b), got = _mixer_bwd_conv(z, dcat, da1, conv_a_f, conv_b_f, S, TE, A, side=side)
    landed(ns, got)
    ns, side = pair_reduce("in", dict(
        w_in=_tn_mm("dw_in", hn1, dz, wt(D), wt(5 * A // N_CHIPS), TK, cols_per_chip=5 * A // N_CHIPS)))

    def in_bwd_epilogue(acc, rows, row_r, vec_r, ro_r, ao_r):
        dh, dg = _rms_bwd(row_r[0][rows, :], vec_r[0][...], acc)
        ro_r[0][rows, :] = row_r[1][rows, :] + dh
        ao_r[0][...] += dg

    (dx, g_norm_mix), got = _kloop_mm(
        "in_bwd", S, TM, dz, w_in3, _mm_tile(5 * A // N_CHIPS, D), row_ins=[x2, dh1],
        vec_ins=[norm_mix_g], row_outs=[(D, F32)], acc_outs=[(1, D)], epilogue=in_bwd_epilogue, side=side)
    landed(ns, got)

    reduced = _share_halves([_add_chips("chip_sum_" + n, chip_sums[n], from_chips[n], chip, ci) for n in names])
    moments = dict(w_in=(m_w_in, v_w_in), w_out=(m_w_out, v_w_out), w_up=(m_w_up, v_w_up),
                   w_down=(m_w_down, v_w_down), w_ple_gate=(m_w_ple_gate, v_w_ple_gate),
                   w_ple_proj=(m_w_ple_proj, v_w_ple_proj))
    grads, deltas, new_m, new_v = {}, {}, {}, {}
    for n, g in zip(names, reduced):
        g = g.reshape(big[n].shape)
        d_, m_, v_ = _adamw("adamw_" + n, big[n], g, moments[n][0][0], moments[n][1][0])
        grads[n], deltas[n], new_m[n], new_v[n] = g[None], d_[None], m_[None], v_[None]

    small = ["norm_mix_g", "conv_a_w", "conv_a_b", "ln_a_g", "ln_a_b", "conv_b_w", "norm_ffn_g",
             "conv_ffn_w", "b_ple_gate", "norm_final_g"]
    small_part = [g_norm_mix, g_conv_a, ln_sums[2:3], ln_sums[0:1], ln_sums[1:2], g_conv_b, g_norm_ffn,
                  g_conv_ffn, g_b_gate, g_norm_final]
    full_shapes = [a.shape for a in small_part]
    summed = _sum_devices("small_grads_sum", _allgather_small("allgather_small_grads", _pack(small_part)))
    small_g = dict(zip(small, _unpack(summed, full_shapes)))
    for n, width in (("conv_a_w", A), ("conv_b_w", A), ("conv_ffn_w", 2 * F)):
        small_g[n] = lax.dynamic_slice_in_dim(small_g[n], chip * (width // N_CHIPS), width // N_CHIPS, axis=1)
    small_w = dict(norm_mix_g=(norm_mix_g, m_norm_mix_g, v_norm_mix_g), conv_a_w=(conv_a_w, m_conv_a_w, v_conv_a_w),
                   conv_a_b=(conv_a_b, m_conv_a_b, v_conv_a_b), ln_a_g=(ln_a_g, m_ln_a_g, v_ln_a_g),
                   ln_a_b=(ln_a_b, m_ln_a_b, v_ln_a_b), conv_b_w=(conv_b_w, m_conv_b_w, v_conv_b_w),
                   norm_ffn_g=(norm_ffn_g, m_norm_ffn_g, v_norm_ffn_g),
                   conv_ffn_w=(conv_ffn_w, m_conv_ffn_w, v_conv_ffn_w),
                   b_ple_gate=(b_ple_gate, m_b_ple_gate, v_b_ple_gate),
                   norm_final_g=(norm_final_g, m_norm_final_g, v_norm_final_g))
    out_shapes = [small_w[n][0].shape for n in small]
    packed_g = _pack([small_g[n] for n in small])
    packed = [_pack([small_w[n][k] for n in small]) for k in range(3)]
    d_s, m_s, v_s = _adamw("adamw_small", packed[0], packed_g, packed[1], packed[2])
    for n, g, d_, m_, v_ in zip(small, _unpack(packed_g, out_shapes), _unpack(d_s, out_shapes),
                                _unpack(m_s, out_shapes), _unpack(v_s, out_shapes)):
        grads[n], deltas[n], new_m[n], new_v[n] = g, d_, m_, v_

    order = ["norm_mix_g", "w_in", "conv_a_w", "conv_a_b", "ln_a_g", "ln_a_b", "conv_b_w", "w_out", "norm_ffn_g",
             "w_up", "conv_ffn_w", "w_down", "w_ple_gate", "b_ple_gate", "w_ple_proj", "norm_final_g"]
    loss = lax.psum(loss_part[0, 0], ("x", "y", "c"))
    return (loss, dx.reshape(x.shape), *[grads[n] for n in order], *[deltas[n] for n in order],
            *[new_m[n] for n in order], *[new_v[n] for n in order])
```

```python
from typing import Callable, NamedTuple

import jax
import jax.numpy as jnp
from jax import lax
from jax.experimental import pallas as pl
from jax.experimental.pallas import tpu as pltpu

F32 = jnp.float32
BF16 = jnp.bfloat16
MESH = pl.DeviceIdType.MESH
ANY = pl.BlockSpec(memory_space=pl.ANY)

EPS = 1e-6
ADAM_LR = 0.001
ADAM_B1 = 0.9
ADAM_B2 = 0.999
ADAM_EPS = 1e-08
ADAM_WD = 0.01
ADAM_STEP = 10

N_CHIPS = 4
N_DEV = 8
LANES = 128
SUBLANES = 8
PACK_ALIGN = LANES * SUBLANES
ROW_CHUNK = 32
VMEM_CAP = 60 * 1024 * 1024
VMEM_SLACK = 6 * 1024 * 1024


def _pick(n, cands):
    for c in cands:
        if n % c == 0:
            return c
    raise ValueError(f"no tile of {cands} divides {n}")


def _nbytes(shape, dtype):
    n = 1
    for s in shape:
        if s is not None:
            n *= s
    return n * jnp.dtype(dtype).itemsize


def _params(sem, blocks, scratch=(), temps=()):
    est = (2 * sum(_nbytes(s, d) for s, d in blocks) + sum(_nbytes(s, d) for s, d in scratch)
           + sum(_nbytes(s, d) for s, d in temps))
    return pltpu.CompilerParams(dimension_semantics=sem,
                                vmem_limit_bytes=min(est + VMEM_SLACK, VMEM_CAP))


def _sigmoid(x):
    return 1.0 / (1.0 + jnp.exp(-x))


def _rsum(x):
    return jnp.sum(x, axis=0, keepdims=True)


class _Side(NamedTuple):
    ins: list
    out_shapes: list
    n_sems: int
    start: Callable
    wait: Callable
    aliased: bool = False


def _call(body, side, *, name, grid, in_specs, out_specs, out_shape, scratch, params, args):
    vmem = [pltpu.VMEM(s, d) for s, d in scratch]
    if side is None:
        outs = pl.pallas_call(body, name=name, grid=grid, in_specs=in_specs, out_specs=out_specs,
                              out_shape=out_shape, scratch_shapes=vmem, compiler_params=params)(*args)
        return list(outs), []
    n_in, n_out, n_sc = len(in_specs), len(out_specs), len(scratch)
    ns_in, ns_out = len(side.ins), len(side.out_shapes)

    def carrier(*refs):
        pos = [0]
        def take(n):
            pos[0] += n
            return refs[pos[0] - n:pos[0]]
        ins, s_ins, outs, s_outs, scr = take(n_in), take(ns_in), take(n_out), take(ns_out), take(n_sc)
        send_sems, recv_sems = take(2)
        first = last = None
        for axis, extent in enumerate(grid):
            at_start, at_end = pl.program_id(axis) == 0, pl.program_id(axis) == extent - 1
            first = at_start if first is None else first & at_start
            last = at_end if last is None else last & at_end

        @pl.when(first)
        def _():
            side.start(s_ins, s_outs, send_sems, recv_sems)
        body(*ins, *outs, *scr)

        @pl.when(last)
        def _():
            side.wait(s_ins, s_outs, send_sems, recv_sems)

    outs = pl.pallas_call(
        carrier, name=name, grid=grid, in_specs=list(in_specs) + [ANY] * ns_in,
        out_specs=list(out_specs) + [ANY] * ns_out, out_shape=list(out_shape) + list(side.out_shapes),
        scratch_shapes=vmem + [pltpu.SemaphoreType.DMA((side.n_sems,)), pltpu.SemaphoreType.DMA((side.n_sems,))],
        input_output_aliases={n_in + i: n_out + i for i in range(ns_in)} if side.aliased else {},
        compiler_params=params)(*args, *side.ins)
    return list(outs[:n_out]), list(outs[n_out:])


def _comm_only(name, side):
    n_in = len(side.ins)

    def body(*refs):
        ins, outs = refs[:n_in], refs[n_in:n_in + len(side.out_shapes)]
        send_sems, recv_sems = refs[n_in + len(side.out_shapes):]
        side.start(ins, outs, send_sems, recv_sems)
        side.wait(ins, outs, send_sems, recv_sems)

    return pl.pallas_call(
        body, name=name, out_shape=list(side.out_shapes), in_specs=[ANY] * n_in,
        out_specs=[ANY] * len(side.out_shapes),
        scratch_shapes=[pltpu.SemaphoreType.DMA((side.n_sems,)), pltpu.SemaphoreType.DMA((side.n_sems,))],
        input_output_aliases={i: i for i in range(n_in)} if side.aliased else {},
    )(*side.ins)


def _rms_stats(x):
    return lax.rsqrt(jnp.mean(x * x, axis=-1, keepdims=True) + EPS)


def _rms_bwd(h, g, dout):
    r = _rms_stats(h)
    n = h * r
    dn = dout * g
    dh = r * (dn - n * jnp.mean(dn * n, axis=-1, keepdims=True))
    return dh, _rsum(dout * n)


def _identity(t):
    return t


def _chip_major(nb, place=_identity):
    return lambda i, j: (place(j) // nb, 0, place(j) % nb)


def _rows_mm(name, S, TM, N, TN, *, row_ins, vec_ins=(), colvec_ins=(), weights, tile_ins=(),
             tile_outs, row_outs=(), acc_outs=(), prologue=None, epilogue, place=_identity, side=None):
    nI, nJ = S // TM, N // TN
    n_row, n_vec, n_cv, n_w, n_tile = len(row_ins), len(vec_ins), len(colvec_ins), len(weights), len(tile_ins)
    n_to, n_ro, n_ao = len(tile_outs), len(row_outs), len(acc_outs)

    in_specs, blocks, scratch, ks = [], [], [], []
    for a in row_ins:
        in_specs.append(pl.BlockSpec((TM, a.shape[1]), lambda i, j: (i, 0)))
        blocks.append(((TM, a.shape[1]), a.dtype))
    for a in vec_ins:
        in_specs.append(pl.BlockSpec(a.shape, lambda i, j: (0, 0)))
        blocks.append((a.shape, a.dtype))
    for a in colvec_ins:
        in_specs.append(pl.BlockSpec((1, TN), lambda i, j: (0, j)))
        blocks.append(((1, TN), a.dtype))
    for w, mode in weights:
        if mode == "nn2":
            k = w.shape[0]
            in_specs.append(pl.BlockSpec((k, TN), lambda i, j: (0, j)))
        elif mode == "nn3":
            k = w.shape[1]
            in_specs.append(pl.BlockSpec((None, k, TN), _chip_major(w.shape[2] // TN, place)))
        else:
            k = w.shape[1]
            in_specs.append(pl.BlockSpec((TN, k), lambda i, j: (j, 0)))
        ks.append(k)
        blocks.append(((k, TN), BF16))
        if prologue is not None:
            scratch.append(((TM, k), BF16))
    for a in tile_ins:
        in_specs.append(pl.BlockSpec((TM, TN), lambda i, j: (i, j)))
        blocks.append(((TM, TN), a.dtype))

    out_shape, out_specs = [], []
    for dt in tile_outs:
        out_shape.append(jax.ShapeDtypeStruct((S, N), dt))
        out_specs.append(pl.BlockSpec((TM, TN), lambda i, j: (i, j)))
        blocks.append(((TM, TN), dt))
    for width, dt in row_outs:
        out_shape.append(jax.ShapeDtypeStruct((S, width), dt))
        out_specs.append(pl.BlockSpec((TM, width), lambda i, j: (i, 0)))
        blocks.append(((TM, width), dt))
    for rows, width in acc_outs:
        out_shape.append(jax.ShapeDtypeStruct((rows, width), F32))
        out_specs.append(pl.BlockSpec((rows, width), lambda i, j: (0, 0)))
        blocks.append(((rows, width), F32))

    modes = [m for _, m in weights]

    def body(*refs):
        pos = 0
        def take(n):
            nonlocal pos
            out = refs[pos:pos + n]
            pos += n
            return out
        row_r, vec_r, cv_r, w_r, tile_r = take(n_row), take(n_vec), take(n_cv), take(n_w), take(n_tile)
        to_r, ro_r, ao_r, a_sc = take(n_to), take(n_ro), take(n_ao), take(len(scratch))
        i, j = pl.program_id(0), pl.program_id(1)

        if prologue is None:
            a_sc = row_r[:n_w]
        else:
            @pl.when(j == 0)
            def _():
                if n_ao:
                    @pl.when(i == 0)
                    def _():
                        for r in ao_r:
                            r[...] = jnp.zeros_like(r)

                def chunk(ci, carry):
                    rows = pl.ds(pl.multiple_of(ci * ROW_CHUNK, ROW_CHUNK), ROW_CHUNK)
                    for sc, a in zip(a_sc, prologue(rows, row_r, vec_r, ro_r, ao_r)):
                        sc[rows, :] = a
                    return carry
                lax.fori_loop(0, TM // ROW_CHUNK, chunk, 0)

        accs = []
        for w_ref, sc, mode in zip(w_r, a_sc, modes):
            if mode == "nt2":
                accs.append(lax.dot_general(sc[...], w_ref[...], (((1,), (1,)), ((), ())),
                                            preferred_element_type=F32))
            else:
                accs.append(jnp.dot(sc[...], w_ref[...], preferred_element_type=F32))
        outs = epilogue(accs, tile_r, cv_r)
        for r, o in zip(to_r, outs):
            r[...] = o.astype(r.dtype)

    outs, side_outs = _call(
        body, side, name=name, grid=(nI, nJ), in_specs=in_specs, out_specs=out_specs, out_shape=out_shape,
        scratch=scratch, params=_params(("arbitrary", "arbitrary"), blocks, scratch, temps=[((TM, TN), F32)] * 3),
        args=[*row_ins, *vec_ins, *colvec_ins, *[w for w, _ in weights], *tile_ins])
    return outs if side is None else (outs, side_outs)


def _kloop_mm(name, S, TM, a, w3, TK, *, row_ins, vec_ins, row_outs, acc_outs, epilogue, place=_identity,
              side=None):
    _, N, Ks = w3.shape
    nb = Ks // TK
    nK = N_CHIPS * nb
    n_row, n_vec, n_ro, n_ao = len(row_ins), len(vec_ins), len(row_outs), len(acc_outs)

    in_specs = [pl.BlockSpec((TM, TK), lambda i, k: (i, k)),
                pl.BlockSpec((None, N, TK), _chip_major(nb, place))]
    blocks = [((TM, TK), BF16), ((N, TK), BF16)]
    for r in row_ins:
        in_specs.append(pl.BlockSpec((TM, r.shape[1]), lambda i, k: (i, 0)))
        blocks.append(((TM, r.shape[1]), r.dtype))
    for v in vec_ins:
        in_specs.append(pl.BlockSpec(v.shape, lambda i, k: (0, 0)))
        blocks.append((v.shape, v.dtype))
    out_shape, out_specs = [], []
    for width, dt in row_outs:
        out_shape.append(jax.ShapeDtypeStruct((S, width), dt))
        out_specs.append(pl.BlockSpec((TM, width), lambda i, k: (i, 0)))
        blocks.append(((TM, width), dt))
    for rows, width in acc_outs:
        out_shape.append(jax.ShapeDtypeStruct((rows, width), F32))
        out_specs.append(pl.BlockSpec((rows, width), lambda i, k: (0, 0)))
        blocks.append(((rows, width), F32))
    scratch = [((TM, N), F32)]

    def body(*refs):
        a_ref, w_ref = refs[0], refs[1]
        row_r = refs[2:2 + n_row]
        vec_r = refs[2 + n_row:2 + n_row + n_vec]
        pos = 2 + n_row + n_vec
        ro_r = refs[pos:pos + n_ro]
        ao_r = refs[pos + n_ro:pos + n_ro + n_ao]
        acc_sc = refs[pos + n_ro + n_ao]
        i, k = pl.program_id(0), pl.program_id(1)
        d = lax.dot_general(a_ref[...], w_ref[...], (((1,), (1,)), ((), ())), preferred_element_type=F32)

        @pl.when(k == 0)
        def _():
            acc_sc[...] = d

        @pl.when(k > 0)
        def _():
            acc_sc[...] += d

        @pl.when(k == nK - 1)
        def _():
            @pl.when(i == 0)
            def _():
                for r in ao_r:
                    r[...] = jnp.zeros_like(r)

            def chunk(ci, carry):
                rows = pl.ds(pl.multiple_of(ci * ROW_CHUNK, ROW_CHUNK), ROW_CHUNK)
                epilogue(acc_sc[rows, :], rows, row_r, vec_r, ro_r, ao_r)
                return carry
            lax.fori_loop(0, TM // ROW_CHUNK, chunk, 0)

    outs, side_outs = _call(
        body, side, name=name, grid=(S // TM, nK), in_specs=in_specs, out_specs=out_specs, out_shape=out_shape,
        scratch=scratch, params=_params(("arbitrary", "arbitrary"), blocks, scratch, temps=[((TM, N), F32)]),
        args=[a, w3, *row_ins, *vec_ins])
    return outs if side is None else (outs, side_outs)


def _tn_mm(name, a, b, TMw, TNw, TK, cols_per_chip=None, place=_identity):
    S, M = a.shape
    N = b.shape[1]
    nK = S // TK
    if cols_per_chip is None:
        out_shape = jax.ShapeDtypeStruct((M, N), F32)
        out_spec = pl.BlockSpec((TMw, TNw), lambda i, j, k: (i, j))
    else:
        nb = cols_per_chip // TNw
        out_shape = jax.ShapeDtypeStruct((N_CHIPS, M, cols_per_chip), F32)
        out_spec = pl.BlockSpec((None, TMw, TNw), lambda i, j, k: (place(j) // nb, i, place(j) % nb))

    def body(a_ref, b_ref, o_ref):
        k = pl.program_id(2)
        d = lax.dot_general(a_ref[...], b_ref[...], (((0,), (0,)), ((), ())), preferred_element_type=F32)

        @pl.when(k == 0)
        def _():
            o_ref[...] = d

        @pl.when(k > 0)
        def _():
            o_ref[...] += d

    blocks = [((TK, TMw), BF16), ((TK, TNw), BF16), ((TMw, TNw), F32)]
    return pl.pallas_call(
        body, name=name, grid=(M // TMw, N // TNw, nK),
        in_specs=[pl.BlockSpec((TK, TMw), lambda i, j, k: (k, i)),
                  pl.BlockSpec((TK, TNw), lambda i, j, k: (k, j))],
        out_specs=out_spec, out_shape=out_shape,
        compiler_params=_params(("arbitrary", "arbitrary", "arbitrary"), blocks,
                                temps=[((TMw, TNw), F32), ((TK, TMw), BF16)]),
    )(a, b)


def _prev_rows(TM, H, col):
    return lambda i: (jnp.maximum(i * (TM // H) - 1, 0), col)


def _next_rows(S, TM, H, col):
    return lambda i: (jnp.minimum((i + 1) * (TM // H), S // H - 1), col)


def _taps_causal(ext_ref, w_ref, K, H, TM, cs):
    acc = None
    for k in range(K):
        term = ext_ref[pl.ds(H - (K - 1) + k, TM), cs] * w_ref[pl.ds(k, 1), cs]
        acc = term if acc is None else acc + term
    return acc


def _taps_anticausal(ext_ref, w_ref, K, TM, cs):
    acc = None
    for k in range(K):
        term = ext_ref[pl.ds(K - 1 - k, TM), cs] * w_ref[pl.ds(k, 1), cs]
        acc = term if acc is None else acc + term
    return acc


def _tap_grads(ext_ref, g, K, H, TM, cs):
    return [_rsum(ext_ref[pl.ds(H - (K - 1) + k, TM), cs] * g) for k in range(K)]


def _mixer_fwd(z, conv_a_w, conv_a_b, ln_g, ln_b, conv_b_w, S, TM, A, side=None):
    H = 32
    KA, KB = conv_a_w.shape[0], conv_b_w.shape[0]
    n_chunks = A // LANES
    RB = _pick(TM, (64, 32))

    def body(zc_ref, zh_ref, wa_ref, ba_ref, g_ref, b_ref, wb_ref, a1_ref, cat_ref, ext_a, ext_b):
        i = pl.program_id(0)
        live = (i > 0).astype(F32)
        zc = zc_ref[...].astype(F32)
        zh = zh_ref[...].astype(F32) * live
        ext_a[pl.ds(0, H), :] = zh[:, 0:A] * _sigmoid(zh[:, A:2 * A])
        ext_a[pl.ds(H, TM), :] = zc[:, 0:A] * _sigmoid(zc[:, A:2 * A])
        ext_b[pl.ds(0, H), :] = zh[:, 3 * A:4 * A] * zh[:, 4 * A:5 * A]
        ext_b[pl.ds(H, TM), :] = zc[:, 3 * A:4 * A] * zc[:, 4 * A:5 * A]

        def chunk(c, carry):
            cs = pl.ds(pl.multiple_of(c * LANES, LANES), LANES)
            for r0 in range(0, TM, RB):
                acc = None
                for k in range(KA):
                    term = ext_a[pl.ds(H - (KA - 1) + k + r0, RB), cs] * wa_ref[pl.ds(k, 1), cs]
                    acc = term if acc is None else acc + term
                a1_ref[pl.ds(r0, RB), cs] = acc + ba_ref[:, cs]
            return carry
        lax.fori_loop(0, n_chunks, chunk, 0)

        a1 = a1_ref[...]
        mu = jnp.mean(a1, axis=-1, keepdims=True)
        d = a1 - mu
        var = jnp.mean(d * d, axis=-1, keepdims=True)
        a2 = d * lax.rsqrt(var + EPS) * g_ref[...] + b_ref[...]
        cat_ref[:, 0:A] = (a2 * _sigmoid(a2)).astype(BF16)
        cbc = _taps_causal(ext_b, wb_ref, KB, H, TM, slice(None))
        cat_ref[:, A:2 * A] = (zc[:, 2 * A:3 * A] * cbc).astype(BF16)

    blocks = [((TM, 5 * A), BF16), ((H, 5 * A), BF16), ((KA, A), F32), ((KB, A), F32),
              ((TM, A), F32), ((TM, 2 * A), BF16)]
    scratch = [((H + TM, A), F32), ((H + TM, A), F32)]
    vec = lambda r: pl.BlockSpec((r, A), lambda i: (0, 0))
    outs, side_outs = _call(
        body, side, name="mixer_fwd", grid=(S // TM,),
        in_specs=[pl.BlockSpec((TM, 5 * A), lambda i: (i, 0)),
                  pl.BlockSpec((H, 5 * A), _prev_rows(TM, H, 0)),
                  vec(KA), vec(1), vec(1), vec(1), vec(KB)],
        out_specs=[pl.BlockSpec((TM, A), lambda i: (i, 0)), pl.BlockSpec((TM, 2 * A), lambda i: (i, 0))],
        out_shape=[jax.ShapeDtypeStruct((S, A), F32), jax.ShapeDtypeStruct((S, 2 * A), BF16)],
        scratch=scratch,
        params=_params(("arbitrary",), blocks, scratch, temps=[((TM, 5 * A), F32)] * 2 + [((TM, A), F32)] * 10),
        args=[z, z, conv_a_w, conv_a_b, ln_g, ln_b, conv_b_w])
    return outs if side is None else (outs, side_outs)


def _pair_tile(nF):
    return lambda t: (t % 2) * nF + t // 2


FFN_ROWS = 16


def _bcast_taps(w_ref, K, lanes):
    return [jnp.broadcast_to(w_ref[pl.ds(k, 1), lanes], (FFN_ROWS, LANES)) for k in range(K)]


def _ffn_act(u0, conv_w, S, TM, F, TC, side=None):
    H = 16
    K = conv_w.shape[0]
    nF = F // TC

    def body(uc_ref, uh_ref, wg_ref, wu_ref, o_ref, ext):
        live = (pl.program_id(0) > 0).astype(F32)
        ext[pl.ds(0, H), :] = uh_ref[...].astype(F32) * live
        ext[pl.ds(H, TM), :] = uc_ref[...].astype(F32)

        def lane_chunk(c, carry):
            lo = pl.ds(pl.multiple_of(c * LANES, LANES), LANES)
            lg, lu = lo, pl.ds(pl.multiple_of(TC + c * LANES, LANES), LANES)
            wg, wu = _bcast_taps(wg_ref, K, lo), _bcast_taps(wu_ref, K, lo)
            for r0 in range(0, TM, FFN_ROWS):
                g = u = None
                for k in range(K):
                    rows = pl.ds(H - (K - 1) + k + r0, FFN_ROWS)
                    tg, tu = ext[rows, lg] * wg[k], ext[rows, lu] * wu[k]
                    g, u = (tg, tu) if g is None else (g + tg, u + tu)
                o_ref[pl.ds(r0, FFN_ROWS), lo] = (g * _sigmoid(g) * u).astype(BF16)
            return carry
        lax.fori_loop(0, TC // LANES, lane_chunk, 0)

    blocks = [((TM, 2 * TC), BF16), ((H, 2 * TC), BF16), ((K, TC), F32), ((K, TC), F32), ((TM, TC), BF16)]
    scratch = [((H + TM, 2 * TC), F32)]
    outs, side_outs = _call(
        body, side, name="ffn_act", grid=(S // TM, nF),
        in_specs=[pl.BlockSpec((TM, 2 * TC), lambda i, j: (i, j)),
                  pl.BlockSpec((H, 2 * TC), lambda i, j: (jnp.maximum(i * (TM // H) - 1, 0), j)),
                  pl.BlockSpec((K, TC), lambda i, j: (0, j)),
                  pl.BlockSpec((K, TC), lambda i, j: (0, j + nF))],
        out_specs=[pl.BlockSpec((TM, TC), lambda i, j: (i, j))],
        out_shape=[jax.ShapeDtypeStruct((S, F), BF16)],
        scratch=scratch,
        params=_params(("arbitrary", "arbitrary"), blocks, scratch, temps=[((TM, 2 * TC), F32)]),
        args=[u0, u0, conv_w, conv_w])
    return outs if side is None else (outs, side_outs)


def _loss_head(h3, target, g_final, S, TM, D):
    def body(h_ref, t_ref, g_ref, loss_ref, dg_ref, dh_ref):
        @pl.when(pl.program_id(0) == 0)
        def _():
            loss_ref[...] = jnp.zeros_like(loss_ref)
            dg_ref[...] = jnp.zeros_like(dg_ref)
        h = h_ref[...]
        g = g_ref[...]
        r = _rms_stats(h)
        n = h * r
        diff = n * g - t_ref[...]
        loss_ref[...] += 0.5 * jnp.sum(jnp.mean(diff * diff, axis=-1, keepdims=True), axis=0, keepdims=True)
        dy = diff * (1.0 / D)
        dn = dy * g
        dh_ref[...] = r * (dn - n * jnp.mean(dn * n, axis=-1, keepdims=True))
        dg_ref[...] += _rsum(dy * n)

    blocks = [((TM, D), F32)] * 3 + [((1, D), F32)] * 2
    row = pl.BlockSpec((TM, D), lambda i: (i, 0))
    return pl.pallas_call(
        body, name="loss_head", grid=(S // TM,),
        in_specs=[row, row, pl.BlockSpec((1, D), lambda i: (0, 0))],
        out_specs=[pl.BlockSpec((1, 1), lambda i: (0, 0)), pl.BlockSpec((1, D), lambda i: (0, 0)), row],
        out_shape=[jax.ShapeDtypeStruct((1, 1), F32), jax.ShapeDtypeStruct((1, D), F32),
                   jax.ShapeDtypeStruct((S, D), F32)],
        compiler_params=_params(("arbitrary",), blocks, temps=[((TM, D), F32)] * 8),
    )(h3, target, g_final)


def _ffn_bwd(u0, dact, conv_w, S, TM, F, TC, side=None):
    H = FFN_ROWS
    K = conv_w.shape[0]
    nF, nI = F // TC, S // TM

    def body(up_ref, uc_ref, un_ref, dc_ref, dn_ref, wg_ref, wu_ref, o_ref, dwg_ref, dwu_ref,
             ext_u, ext_d, ext_a):
        i = pl.program_id(1)
        @pl.when(i == 0)
        def _():
            dwg_ref[...] = jnp.zeros_like(dwg_ref)
            dwu_ref[...] = jnp.zeros_like(dwu_ref)
        last = (i < nI - 1).astype(F32)
        ext_u[pl.ds(0, H), :] = up_ref[...].astype(F32) * (i > 0).astype(F32)
        ext_u[pl.ds(H, TM), :] = uc_ref[...].astype(F32)
        ext_u[pl.ds(H + TM, H), :] = un_ref[...].astype(F32) * last
        ext_a[pl.ds(0, TM), :] = dc_ref[...].astype(F32)
        ext_a[pl.ds(TM, H), :] = dn_ref[...].astype(F32) * last

        def lane_chunk(c, carry):
            lo = pl.ds(pl.multiple_of(c * LANES, LANES), LANES)
            lg, lu = lo, pl.ds(pl.multiple_of(TC + c * LANES, LANES), LANES)
            wg, wu = _bcast_taps(wg_ref, K, lo), _bcast_taps(wu_ref, K, lo)
            sums_g, sums_u = [None] * K, [None] * K
            for r0 in range(0, TM + H, FFN_ROWS):
                xg = [ext_u[pl.ds(H - (K - 1) + k + r0, FFN_ROWS), lg] for k in range(K)]
                xu = [ext_u[pl.ds(H - (K - 1) + k + r0, FFN_ROWS), lu] for k in range(K)]
                g, u = xg[0] * wg[0], xu[0] * wu[0]
                for k in range(1, K):
                    g, u = g + xg[k] * wg[k], u + xu[k] * wu[k]
                da = ext_a[pl.ds(r0, FFN_ROWS), lo]
                s = _sigmoid(g)
                dg = da * u * s * (1.0 + g * (1.0 - s))
                du = da * g * s
                ext_d[pl.ds(r0, FFN_ROWS), lg] = dg
                ext_d[pl.ds(r0, FFN_ROWS), lu] = du
                if r0 < TM:
                    for k in range(K):
                        tg, tu = xg[k] * dg, xu[k] * du
                        sums_g[k] = tg if sums_g[k] is None else sums_g[k] + tg
                        sums_u[k] = tu if sums_u[k] is None else sums_u[k] + tu
            for k in range(K):
                dwg_ref[pl.ds(k, 1), lo] += _rsum(sums_g[k])
                dwu_ref[pl.ds(k, 1), lo] += _rsum(sums_u[k])
            for r0 in range(0, TM, FFN_ROWS):
                g = u = None
                for k in range(K):
                    rows = pl.ds(K - 1 - k + r0, FFN_ROWS)
                    tg, tu = ext_d[rows, lg] * wg[k], ext_d[rows, lu] * wu[k]
                    g, u = (tg, tu) if g is None else (g + tg, u + tu)
                o_ref[pl.ds(r0, FFN_ROWS), lg] = g.astype(BF16)
                o_ref[pl.ds(r0, FFN_ROWS), lu] = u.astype(BF16)
            return carry
        lax.fori_loop(0, TC // LANES, lane_chunk, 0)

    blocks = [((H, 2 * TC), BF16), ((TM, 2 * TC), BF16), ((H, 2 * TC), BF16), ((TM, TC), BF16), ((H, TC), BF16),
              ((K, TC), F32), ((K, TC), F32), ((TM, 2 * TC), BF16), ((K, TC), F32), ((K, TC), F32)]
    scratch = [((TM + 2 * H, 2 * TC), F32), ((TM + H, 2 * TC), F32), ((TM + H, TC), F32)]
    prev = lambda j, i: (jnp.maximum(i * (TM // H) - 1, 0), j)
    nxt = lambda j, i: (jnp.minimum((i + 1) * (TM // H), S // H - 1), j)
    taps_out = pl.BlockSpec((K, TC), lambda j, i: (0, j))
    outs, side_outs = _call(
        body, side, name="ffn_bwd", grid=(nF, nI),
        in_specs=[pl.BlockSpec((H, 2 * TC), prev), pl.BlockSpec((TM, 2 * TC), lambda j, i: (i, j)),
                  pl.BlockSpec((H, 2 * TC), nxt),
                  pl.BlockSpec((TM, TC), lambda j, i: (i, j)), pl.BlockSpec((H, TC), nxt),
                  pl.BlockSpec((K, TC), lambda j, i: (0, j)), pl.BlockSpec((K, TC), lambda j, i: (0, j + nF))],
        out_specs=[pl.BlockSpec((TM, 2 * TC), lambda j, i: (i, j)), taps_out, taps_out],
        out_shape=[jax.ShapeDtypeStruct((S, 2 * F), BF16), jax.ShapeDtypeStruct((K, F), F32),
                   jax.ShapeDtypeStruct((K, F), F32)],
        scratch=scratch,
        params=_params(("arbitrary", "arbitrary"), blocks, scratch, temps=[((TM, 2 * TC), F32)]),
        args=[u0, u0, u0, dact, dact, conv_w, conv_w])
    return outs if side is None else (outs, side_outs)


def _mixer_bwd_ln(dcat, a1, ln_g, ln_b, S, TM, A):
    def body(dc_ref, a1_ref, g_ref, b_ref, da1_ref, acc_ref):
        @pl.when(pl.program_id(0) == 0)
        def _():
            acc_ref[...] = jnp.zeros_like(acc_ref)
        a1 = a1_ref[...]
        g = g_ref[...]
        mu = jnp.mean(a1, axis=-1, keepdims=True)
        d = a1 - mu
        rstd = lax.rsqrt(jnp.mean(d * d, axis=-1, keepdims=True) + EPS)
        nh = d * rstd
        a2 = nh * g + b_ref[...]
        s = _sigmoid(a2)
        da2 = dc_ref[...].astype(F32) * s * (1.0 + a2 * (1.0 - s))
        dnh = da2 * g
        da1 = rstd * (dnh - jnp.mean(dnh, axis=-1, keepdims=True)
                      - nh * jnp.mean(dnh * nh, axis=-1, keepdims=True))
        da1_ref[...] = da1
        acc_ref[pl.ds(0, 1), :] += _rsum(da2 * nh)
        acc_ref[pl.ds(1, 1), :] += _rsum(da2)
        acc_ref[pl.ds(2, 1), :] += _rsum(da1)

    blocks = [((TM, A), BF16), ((TM, A), F32), ((TM, A), F32), ((4, A), F32)]
    return pl.pallas_call(
        body, name="mixer_bwd_ln", grid=(S // TM,),
        in_specs=[pl.BlockSpec((TM, A), lambda i: (i, 0)), pl.BlockSpec((TM, A), lambda i: (i, 0)),
                  pl.BlockSpec((1, A), lambda i: (0, 0)), pl.BlockSpec((1, A), lambda i: (0, 0))],
        out_specs=[pl.BlockSpec((TM, A), lambda i: (i, 0)), pl.BlockSpec((4, A), lambda i: (0, 0))],
        out_shape=[jax.ShapeDtypeStruct((S, A), F32), jax.ShapeDtypeStruct((4, A), F32)],
        compiler_params=_params(("arbitrary",), blocks, temps=[((TM, A), F32)] * 12),
    )(dcat, a1, ln_g, ln_b)


def _mixer_bwd_conv(z, dcat, da1, conv_a_w, conv_b_w, S, TM, A, side=None):
    H = 32
    KA, KB = conv_a_w.shape[0], conv_b_w.shape[0]
    nI = S // TM
    n_chunks = A // LANES
    RB = _pick(TM, (64, 32))

    def body(zc_ref, zp_ref, zn_ref, dbc_ref, dbn_ref, d1c_ref, d1n_ref, wa_ref, wb_ref,
             dz_ref, dwa_ref, dwb_ref, ext_a0, ext_d1, ext_cb, ext_dc, da0_sc):
        i = pl.program_id(0)
        @pl.when(i == 0)
        def _():
            dwa_ref[...] = jnp.zeros_like(dwa_ref)
            dwb_ref[...] = jnp.zeros_like(dwb_ref)
        first = (i > 0).astype(F32)
        last = (i < nI - 1).astype(F32)
        zc = zc_ref[...].astype(F32)
        zp = zp_ref[...].astype(F32) * first
        a_val, a_gate = zc[:, 0:A], zc[:, A:2 * A]
        b_gate, c_gate, b_h = zc[:, 2 * A:3 * A], zc[:, 3 * A:4 * A], zc[:, 4 * A:5 * A]
        sig = _sigmoid(a_gate)
        ext_a0[pl.ds(0, H), :] = zp[:, 0:A] * _sigmoid(zp[:, A:2 * A])
        ext_a0[pl.ds(H, TM), :] = a_val * sig
        ext_d1[pl.ds(0, TM), :] = d1c_ref[...]
        ext_d1[pl.ds(TM, H), :] = d1n_ref[...] * last
        ext_cb[pl.ds(0, H), :] = zp[:, 3 * A:4 * A] * zp[:, 4 * A:5 * A]
        ext_cb[pl.ds(H, TM), :] = c_gate * b_h
        dbx = dbc_ref[...].astype(F32)
        dcbc = dbx * b_gate
        ext_dc[pl.ds(0, TM), :] = dcbc
        ext_dc[pl.ds(TM, H), :] = dbn_ref[...].astype(F32) * zn_ref[...].astype(F32) * last

        def chunk(c, carry):
            cs = pl.ds(pl.multiple_of(c * LANES, LANES), LANES)
            for r0 in range(0, TM, RB):
                acc = None
                for k in range(KA):
                    term = ext_d1[pl.ds(KA - 1 - k + r0, RB), cs] * wa_ref[pl.ds(k, 1), cs]
                    acc = term if acc is None else acc + term
                da0_sc[pl.ds(r0, RB), cs] = acc
            for k in range(KA):
                acc = None
                for r0 in range(0, TM, RB):
                    term = ext_a0[pl.ds(H - (KA - 1) + k + r0, RB), cs] * ext_d1[pl.ds(r0, RB), cs]
                    acc = term if acc is None else acc + term
                dwa_ref[pl.ds(k, 1), cs] += _rsum(acc)
            return carry
        lax.fori_loop(0, n_chunks, chunk, 0)

        da0 = da0_sc[...]
        dz_ref[:, 0:A] = (da0 * sig).astype(BF16)
        dz_ref[:, A:2 * A] = (da0 * a_val * sig * (1.0 - sig)).astype(BF16)
        cbc = _taps_causal(ext_cb, wb_ref, KB, H, TM, slice(None))
        dz_ref[:, 2 * A:3 * A] = (dbx * cbc).astype(BF16)
        dcb = _taps_anticausal(ext_dc, wb_ref, KB, TM, slice(None))
        dz_ref[:, 3 * A:4 * A] = (dcb * b_h).astype(BF16)
        dz_ref[:, 4 * A:5 * A] = (dcb * c_gate).astype(BF16)
        grads = _tap_grads(ext_cb, dcbc, KB, H, TM, slice(None))
        for k in range(KB):
            dwb_ref[pl.ds(k, 1), :] += grads[k]

    blocks = [((TM, 5 * A), BF16), ((H, 5 * A), BF16), ((H, A), BF16), ((TM, A), BF16), ((H, A), BF16),
              ((TM, A), F32), ((H, A), F32), ((KA, A), F32), ((KB, A), F32),
              ((TM, 5 * A), BF16), ((KA, A), F32), ((KB, A), F32)]
    scratch = [((H + TM, A), F32)] * 4 + [((TM, A), F32)]
    vec = lambda r: pl.BlockSpec((r, A), lambda i: (0, 0))
    outs, side_outs = _call(
        body, side, name="mixer_bwd_conv", grid=(nI,),
        in_specs=[pl.BlockSpec((TM, 5 * A), lambda i: (i, 0)),
                  pl.BlockSpec((H, 5 * A), _prev_rows(TM, H, 0)),
                  pl.BlockSpec((H, A), _next_rows(S, TM, H, 2)),
                  pl.BlockSpec((TM, A), lambda i: (i, 1)),
                  pl.BlockSpec((H, A), _next_rows(S, TM, H, 1)),
                  pl.BlockSpec((TM, A), lambda i: (i, 0)),
                  pl.BlockSpec((H, A), _next_rows(S, TM, H, 0)),
                  vec(KA), vec(KB)],
        out_specs=[pl.BlockSpec((TM, 5 * A), lambda i: (i, 0)), vec(KA), vec(KB)],
        out_shape=[jax.ShapeDtypeStruct((S, 5 * A), BF16), jax.ShapeDtypeStruct((KA, A), F32),
                   jax.ShapeDtypeStruct((KB, A), F32)],
        scratch=scratch,
        params=_params(("arbitrary",), blocks, scratch, temps=[((TM, 5 * A), F32)] * 2 + [((TM, A), F32)] * 14),
        args=[z, z, z, dcat, dcat, da1, da1, conv_a_w, conv_b_w])
    return outs if side is None else (outs, side_outs)


def _row_tile(R):
    return _pick(R, (256, 128, 64, 32, 16, 8))


def _scalars(*vals):
    return jnp.stack([jnp.asarray(v, jnp.int32) for v in vals])


def _cast_into_gathered(name, w, chip):
    R, C = w.shape
    TR = _row_tile(R)

    def body(s_ref, w_ref, o_ref):
        o_ref[...] = w_ref[...].astype(BF16)

    grid_spec = pltpu.PrefetchScalarGridSpec(
        num_scalar_prefetch=1, grid=(R // TR,),
        in_specs=[pl.BlockSpec((TR, C), lambda r, s: (r, 0))],
        out_specs=pl.BlockSpec((None, TR, C), lambda r, s: (s[0], r, 0)))
    return pl.pallas_call(body, name=name, grid_spec=grid_spec,
                          out_shape=jax.ShapeDtypeStruct((N_CHIPS, R, C), BF16),
                          compiler_params=_params(("arbitrary",), [((TR, C), F32), ((TR, C), BF16)]),
                          )(_scalars(chip), w)


def _add_pair(name, dw, recv, c):
    _, _, Rh, C = dw.shape
    TR = _row_tile(Rh)

    def body(c_ref, a_ref, b_ref, o_ref, ob_ref):
        s = a_ref[...] + b_ref[...]
        o_ref[...] = s
        ob_ref[...] = s.astype(BF16)

    out_spec = pl.BlockSpec((None, TR, C), lambda k, r, c_ref: (k, r, 0))
    grid_spec = pltpu.PrefetchScalarGridSpec(
        num_scalar_prefetch=1, grid=(N_CHIPS, Rh // TR),
        in_specs=[pl.BlockSpec((None, None, TR, C), lambda k, r, c_ref: (k, c_ref[0], r, 0)),
                  pl.BlockSpec((None, TR, C), lambda k, r, c_ref: (k, r, 0))],
        out_specs=[out_spec, out_spec])
    return pl.pallas_call(body, name=name, grid_spec=grid_spec,
                          out_shape=[jax.ShapeDtypeStruct((N_CHIPS, Rh, C), F32),
                                     jax.ShapeDtypeStruct((N_CHIPS, Rh, C), BF16)],
                          compiler_params=_params(("arbitrary", "arbitrary"), [((TR, C), F32)] * 4),
                          )(_scalars(c), dw, recv)


def _add_chips(name, parts, recv, chip, c):
    _, Rh, C = parts.shape
    TR = _row_tile(Rh)

    def body(s_ref, p_ref, r_ref, o_ref):
        o_ref[...] = ((p_ref[...] + r_ref[0].astype(F32)) + r_ref[1].astype(F32)) + r_ref[2].astype(F32)

    grid_spec = pltpu.PrefetchScalarGridSpec(
        num_scalar_prefetch=1, grid=(Rh // TR,),
        in_specs=[pl.BlockSpec((None, TR, C), lambda r, s: (s[0], r, 0)),
                  pl.BlockSpec((N_CHIPS - 1, TR, C), lambda r, s: (0, r, 0))],
        out_specs=pl.BlockSpec((None, TR, C), lambda r, s: (s[1], r, 0)))
    return pl.pallas_call(body, name=name, grid_spec=grid_spec,
                          out_shape=jax.ShapeDtypeStruct((2, Rh, C), F32),
                          compiler_params=_params(("arbitrary",), [((N_CHIPS + 1, TR, C), F32)]),
                          )(_scalars(chip, c), parts, recv)


def _sum_devices(name, parts):
    _, R, C = parts.shape

    def body(p_ref, o_ref):
        acc = p_ref[0]
        for d in range(1, N_DEV):
            acc = acc + p_ref[d]
        o_ref[...] = acc

    return pl.pallas_call(body, name=name, out_shape=jax.ShapeDtypeStruct((R, C), F32),
                          in_specs=[pl.BlockSpec(memory_space=pltpu.VMEM)],
                          out_specs=pl.BlockSpec(memory_space=pltpu.VMEM))(parts)


def _adamw(name, w, g, m, v):
    R, C = w.shape
    TR = _pick(R, (128, 64, 32, 16, 8))
    c1 = 1.0 - ADAM_B1 ** ADAM_STEP
    c2 = 1.0 - ADAM_B2 ** ADAM_STEP

    def body(w_ref, g_ref, m_ref, v_ref, d_ref, nm_ref, nv_ref):
        g_ = g_ref[...]
        nm = ADAM_B1 * m_ref[...] + (1.0 - ADAM_B1) * g_
        nv = ADAM_B2 * v_ref[...] + (1.0 - ADAM_B2) * (g_ * g_)
        d_ref[...] = -ADAM_LR * ((nm / c1) / (jnp.sqrt(nv / c2) + ADAM_EPS) + ADAM_WD * w_ref[...])
        nm_ref[...] = nm
        nv_ref[...] = nv

    spec = pl.BlockSpec((TR, C), lambda r: (r, 0))
    shp = jax.ShapeDtypeStruct((R, C), F32)
    return pl.pallas_call(body, name=name, grid=(R // TR,), in_specs=[spec] * 4, out_specs=[spec] * 3,
                          out_shape=[shp] * 3,
                          compiler_params=_params(("arbitrary",), [((TR, C), F32)] * 7))(w, g, m, v)


def _place():
    x, y, c = lax.axis_index("x"), lax.axis_index("y"), lax.axis_index("c")
    others = [(1 - x, y), (x, 1 - y), (1 - x, 1 - y)]
    return x, y, c, others


def _allgather_small(name, block):
    R, C = block.shape

    def body(x_ref, out_ref, send_sems, recv_sems, local_sem):
        x, y, c, chips = _place()
        me, sibling = (x, y, c), (x, y, 1 - c)

        def rows(px, py, pc):
            return out_ref.at[4 * px + 2 * py + pc]

        def copy(k, blk, to, src=None):
            return pltpu.make_async_remote_copy(
                src_ref=rows(*blk) if src is None else src, dst_ref=rows(*blk),
                send_sem=send_sems.at[k], recv_sem=recv_sems.at[k], device_id=to, device_id_type=MESH)

        mine = pltpu.make_async_copy(x_ref, rows(*me), local_sem)
        mine.start()
        first = [copy(0, me, sibling, src=x_ref)]
        first += [copy(1 + j, me, (*chip, c), src=x_ref) for j, chip in enumerate(chips)]
        for cp in first:
            cp.start()
        passed = [copy(4 + j, (*chip, c), sibling) for j, chip in enumerate(chips)]
        for j, chip in enumerate(chips):
            copy(1 + j, (*chip, c), me).wait_recv()
            passed[j].start()
        copy(0, sibling, me).wait_recv()
        for j, chip in enumerate(chips):
            copy(4 + j, (*chip, 1 - c), me).wait_recv()
        for cp in first + passed:
            cp.wait_send()
        mine.wait()

    return pl.pallas_call(
        body, name=name, out_shape=jax.ShapeDtypeStruct((N_DEV, R, C), F32),
        in_specs=[pl.BlockSpec(memory_space=pltpu.VMEM)], out_specs=pl.BlockSpec(memory_space=pltpu.VMEM),
        scratch_shapes=[pltpu.SemaphoreType.DMA((7,)), pltpu.SemaphoreType.DMA((7,)), pltpu.SemaphoreType.DMA],
    )(block)


def _gather_side(bufs, across, within):
    def rows(ref, chip, half, piece):
        _, r0, n = piece
        return ref.at[2 * chip[0] + chip[1], pl.ds(half * (ref.shape[1] // 2) + r0, n)]

    def copies(ins, outs, send_sems, recv_sems):
        x, y, c, chips = _place()
        sibling = (x, y, 1 - c)
        pairs = []

        def add(k, src, dst, to, arrival):
            mk = lambda s, d, dev: pltpu.make_async_remote_copy(
                src_ref=s, dst_ref=d, send_sem=send_sems.at[k], recv_sem=recv_sems.at[k],
                device_id=dev, device_id_type=MESH)
            pairs.append((mk(src, dst, to), mk(arrival, arrival, (x, y, c))))

        for p, piece in enumerate(across):
            ref = outs[piece[0]]
            for j, chip in enumerate(chips):
                mine = rows(ref, (x, y), c, piece)
                add(3 * p + j, mine, mine, (*chip, c), rows(ref, chip, c, piece))
        for q, piece in enumerate(within):
            ref = outs[piece[0]]
            for j, chip in enumerate(chips):
                held = rows(ref, chip, c, piece)
                add(3 * (len(across) + q) + j, held, held, sibling, rows(ref, chip, 1 - c, piece))
        return pairs

    def start(*refs):
        for send, _ in copies(*refs):
            send.start()

    def wait(*refs):
        pairs = copies(*refs)
        for _, arrival in pairs:
            arrival.wait_recv()
        for send, _ in pairs:
            send.wait_send()

    return _Side(list(bufs), [jax.ShapeDtypeStruct(b.shape, b.dtype) for b in bufs],
                 3 * (len(across) + len(within)), start, wait, aliased=True)


def _exchange_pairs(name, grads):
    n = len(grads)

    def body(*refs):
        ins, outs = refs[:n], refs[n:2 * n]
        send_sems, recv_sems = refs[2 * n:]
        x, y, c, _ = _place()
        copies = [pltpu.make_async_remote_copy(
            src_ref=ins[a].at[:, 1 - c], dst_ref=outs[a], send_sem=send_sems.at[a], recv_sem=recv_sems.at[a],
            device_id=(x, y, 1 - c), device_id_type=MESH) for a in range(n)]
        for cp in copies:
            cp.start()
        for cp in copies:
            cp.wait()

    return pl.pallas_call(
        body, name=name,
        out_shape=[jax.ShapeDtypeStruct((N_CHIPS,) + g.shape[2:], F32) for g in grads],
        in_specs=[ANY] * n, out_specs=[ANY] * n,
        scratch_shapes=[pltpu.SemaphoreType.DMA((n,)), pltpu.SemaphoreType.DMA((n,))],
    )(*grads)


def _chip_exchange(parts):
    n = len(parts)

    def copies(ins, outs, send_sems, recv_sems):
        x, y, c, chips = _place()
        return [pltpu.make_async_remote_copy(
            src_ref=ins[a].at[2 * chip[0] + chip[1]], dst_ref=outs[a].at[j],
            send_sem=send_sems.at[3 * a + j], recv_sem=recv_sems.at[3 * a + j],
            device_id=(*chip, c), device_id_type=MESH) for a in range(n) for j, chip in enumerate(chips)]

    def start(*refs):
        for cp in copies(*refs):
            cp.start()

    def wait(*refs):
        cps = copies(*refs)
        for cp in cps:
            cp.wait_recv()
        for cp in cps:
            cp.wait_send()

    return _Side(list(parts), [jax.ShapeDtypeStruct((N_CHIPS - 1,) + p.shape[1:], p.dtype) for p in parts],
                 3 * n, start, wait)


def _share_halves(halves):
    n = len(halves)

    def body(*refs):
        ins, outs = refs[:n], refs[n:2 * n]
        send_sems, recv_sems = refs[2 * n:]
        x, y, c, _ = _place()
        sends = []
        for a in range(n):
            sends.append(pltpu.make_async_remote_copy(
                src_ref=ins[a].at[c], dst_ref=outs[a].at[c], send_sem=send_sems.at[a], recv_sem=recv_sems.at[a],
                device_id=(x, y, 1 - c), device_id_type=MESH))
            sends[a].start()
        for a in range(n):
            theirs = outs[a].at[1 - c]
            pltpu.make_async_remote_copy(
                src_ref=theirs, dst_ref=theirs, send_sem=send_sems.at[a], recv_sem=recv_sems.at[a],
                device_id=(x, y, c), device_id_type=MESH).wait_recv()
        for cp in sends:
            cp.wait_send()

    return pl.pallas_call(
        body, name="grads_share_halves",
        out_shape=[jax.ShapeDtypeStruct(h.shape, F32) for h in halves],
        in_specs=[ANY] * n, out_specs=[ANY] * n, input_output_aliases={a: a for a in range(n)},
        scratch_shapes=[pltpu.SemaphoreType.DMA((n,)), pltpu.SemaphoreType.DMA((n,))],
    )(*halves)


def _pack(arrays):
    pieces = []
    for a in arrays:
        flat = a.reshape(-1).astype(F32)
        pieces.append(jnp.pad(flat, (0, (-flat.size) % PACK_ALIGN)))
    return jnp.concatenate(pieces).reshape(-1, LANES)


def _unpack(buf, shapes):
    lead = buf.shape[:-2]
    flat = buf.reshape(lead + (-1,))
    out, off = [], 0
    for shp in shapes:
        size = 1
        for s in shp:
            size *= s
        out.append(flat[..., off:off + size].reshape(lead + tuple(shp)))
        off += size + (-size) % PACK_ALIGN
    return out


def _gather_channels(buf, shapes):
    per_chip = _unpack(buf[0::2], shapes)
    return [jnp.transpose(a, (1, 0, 2)).reshape(a.shape[1], -1) for a in per_chip]


def _mm_tile(n, rows, limit_bytes=6 * 1024 * 1024):
    for t in (1408, 1280, 1024, 640, 512, 384, 256, 128):
        if n % t == 0 and rows * t * 2 <= limit_bytes:
            return t
    raise ValueError(f"no column tile for {n} x {rows}")


def kernel(x, p, norm_mix_g, w_in, conv_a_w, conv_a_b, ln_a_g, ln_a_b, conv_b_w, w_out, norm_ffn_g, w_up, conv_ffn_w, w_down, w_ple_gate, b_ple_gate, w_ple_proj, norm_final_g, loss_target, m_norm_mix_g, m_w_in, m_conv_a_w, m_conv_a_b, m_ln_a_g, m_ln_a_b, m_conv_b_w, m_w_out, m_norm_ffn_g, m_w_up, m_conv_ffn_w, m_w_down, m_w_ple_gate, m_b_ple_gate, m_w_ple_proj, m_norm_final_g, v_norm_mix_g, v_w_in, v_conv_a_w, v_conv_a_b, v_ln_a_g, v_ln_a_b, v_conv_b_w, v_w_out, v_norm_ffn_g, v_w_up, v_conv_ffn_w, v_w_down, v_w_ple_gate, v_b_ple_gate, v_w_ple_proj, v_norm_final_g):
    S, D = x.shape[1], x.shape[2]
    P = p.shape[3]
    A = conv_a_b.shape[1]
    F = w_down.shape[1] * N_CHIPS
    KA, KB, KF = conv_a_w.shape[1], conv_b_w.shape[1], conv_ffn_w.shape[1]
    xi, yi, ci = lax.axis_index("x"), lax.axis_index("y"), lax.axis_index("c")
    chip = 2 * xi + yi

    TM = _pick(S, (512, 256, 128))
    TE = _pick(S, (256, 128))
    TC = _pick(2 * F // N_CHIPS, (1408, 1024, 512, 256, 128))
    ffn_place = _pair_tile(F // TC)

    x2, p2, t2 = x.reshape(S, D), p.reshape(S, P), loss_target.reshape(S, D)
    gfin = norm_final_g.reshape(1, D)

    big = dict(w_in=w_in[0], w_out=w_out[0], w_up=w_up[0], w_down=w_down[0],
               w_ple_gate=w_ple_gate[0], w_ple_proj=w_ple_proj[0])
    names = list(big)
    buf = {n: _cast_into_gathered("cast_" + n, big[n], chip) for n in names}
    half = {n: big[n].shape[0] // 2 for n in names}
    up_a = half["w_up"] // 2
    (w_in3,) = _comm_only("gather_w_in_across", _gather_side([buf["w_in"]], [(0, 0, half["w_in"])], []))
    (w_in3,) = _comm_only("gather_w_in_within", _gather_side([w_in3], [], [(0, 0, half["w_in"])]))

    tap_shapes = [(KA, A // N_CHIPS), (KB, A // N_CHIPS), (KF, 2 * F // N_CHIPS)]
    taps = _allgather_small("allgather_taps", _pack([conv_a_w[0], conv_b_w[0], conv_ffn_w[0]]))
    conv_a_f, conv_b_f, conv_ffn_f = _gather_channels(taps, tap_shapes)

    def rms_prologue(rows, row_r, vec_r, ro_r, ao_r):
        h = row_r[0][rows, :]
        hn = (h * _rms_stats(h) * vec_r[0][...]).astype(BF16)
        ro_r[0][rows, :] = hn
        return [hn]

    def cast_prologue(rows, row_r, vec_r, ro_r, ao_r):
        hb = row_r[0][rows, :].astype(BF16)
        ro_r[0][rows, :] = hb
        return [hb]

    plain = lambda accs, tile_r, cv_r: [accs[0]]
    residual = lambda accs, tile_r, cv_r: [tile_r[0][...] + accs[0]]

    (z, hn1), (w_out_t, w_up_t) = _rows_mm(
        "in_proj", S, TM, 5 * A, _mm_tile(5 * A // N_CHIPS, D), row_ins=[x2], vec_ins=[norm_mix_g],
        weights=[(w_in3, "nn3")], tile_outs=[BF16], row_outs=[(D, BF16)], prologue=rms_prologue, epilogue=plain,
        side=_gather_side([buf["w_out"], buf["w_up"]], [(0, 0, half["w_out"]), (1, 0, up_a)], []))
    (a1, cat), (w_out3, w_up_t) = _mixer_fwd(
        z, conv_a_f, conv_a_b, ln_a_g, ln_a_b, conv_b_f, S, TE, A,
        side=_gather_side([w_out_t, w_up_t], [(1, up_a, half["w_up"] - up_a)],
                          [(0, 0, half["w_out"]), (1, 0, up_a)]))
    w_out_f = w_out3.reshape(2 * A, D)
    (h1,), (w_up3, w_proj_t) = _rows_mm(
        "out_proj", S, TM, D, _mm_tile(D, 2 * A), row_ins=[cat], weights=[(w_out_f, "nn2")],
        tile_ins=[x2], tile_outs=[F32], epilogue=residual,
        side=_gather_side([w_up_t, buf["w_ple_proj"]], [(1, 0, half["w_ple_proj"])],
                          [(0, up_a, half["w_up"] - up_a)]))
    (u0, hn2), (w_down_t, w_gate_t, w_proj3) = _rows_mm(
        "up_proj", S, TM, 2 * F, TC, row_ins=[h1], vec_ins=[norm_ffn_g],
        weights=[(w_up3, "nn3")], tile_outs=[BF16], row_outs=[(D, BF16)],
        prologue=rms_prologue, epilogue=plain, place=ffn_place,
        side=_gather_side([buf["w_down"], buf["w_ple_gate"], w_proj_t],
                          [(0, 0, half["w_down"]), (1, 0, half["w_ple_gate"])], [(2, 0, half["w_ple_proj"])]))
    (act,), (w_down3, w_gate3) = _ffn_act(
        u0, conv_ffn_f, S, TM, F, TC,
        side=_gather_side([w_down_t, w_gate_t], [], [(0, 0, half["w_down"]), (1, 0, half["w_ple_gate"])]))
    w_down_f = w_down3.reshape(F, D)
    w_gate_f = w_gate3.reshape(D, D)
    (h2,) = _rows_mm("down_proj", S, TM, D, _mm_tile(D, F), row_ins=[act], weights=[(w_down_f, "nn2")],
                     tile_ins=[h1], tile_outs=[F32], epilogue=residual)

    def ple_prologue(rows, row_r, vec_r, ro_r, ao_r):
        hb = row_r[0][rows, :].astype(BF16)
        pb = row_r[1][rows, :].astype(BF16)
        ro_r[0][rows, :] = hb
        ro_r[1][rows, :] = pb
        return [hb, pb]

    def ple_epilogue(accs, tile_r, cv_r):
        gate = _sigmoid(accs[0] + cv_r[0][...])
        return [tile_r[0][...] + accs[1] * gate, gate, accs[1]]

    h3, gate, pp, h2b, pb = _rows_mm(
        "ple_fwd", S, TM, D, _mm_tile(D // N_CHIPS, D), row_ins=[h2, p2], colvec_ins=[b_ple_gate],
        weights=[(w_gate_f, "nn2"), (w_proj3, "nn3")], tile_ins=[h2], tile_outs=[F32, BF16, BF16],
        row_outs=[(D, BF16), (P, BF16)], prologue=ple_prologue, epilogue=ple_epilogue)
    loss_part, g_norm_final, dh3 = _loss_head(h3, t2, gfin, S, TE, D)

    def ple_bwd_prologue(rows, row_r, vec_r, ro_r, ao_r):
        d = row_r[0][rows, :]
        gt = row_r[2][rows, :].astype(F32)
        dpre = d * row_r[1][rows, :].astype(F32) * gt * (1.0 - gt)
        ro_r[0][rows, :] = dpre.astype(BF16)
        ro_r[1][rows, :] = (d * gt).astype(BF16)
        ao_r[0][...] += _rsum(dpre)
        return [dpre.astype(BF16)]

    dh2, dpre, dpp, g_b_gate = _rows_mm(
        "ple_bwd", S, TM, D, _mm_tile(D, D), row_ins=[dh3, pp, gate], weights=[(w_gate_f, "nt2")],
        tile_ins=[dh3], tile_outs=[F32], row_outs=[(D, BF16), (D, BF16)], acc_outs=[(1, D)],
        prologue=ple_bwd_prologue, epilogue=residual)
    TK = _pick(S, (1024, 512, 256, 128))
    wt = lambda n: _pick(n, (1408, 1280, 1024, 512, 256, 128))
    chip_sums, from_chips = {}, {}

    def pair_reduce(tag, parts):
        ns = list(parts)
        halves = [parts[n].reshape(N_CHIPS, 2, big[n].shape[0] // 2, big[n].shape[1]) for n in ns]
        from_sibling = _exchange_pairs("grads_exchange_pairs_" + tag, halves)
        sums = [_add_pair("pair_sum_" + n, h, r, ci) for n, h, r in zip(ns, halves, from_sibling)]
        for n, (s, _) in zip(ns, sums):
            chip_sums[n] = s
        return ns, _chip_exchange([b for _, b in sums])

    def landed(ns, side_outs):
        for n, r in zip(ns, side_outs):
            from_chips[n] = r

    ns, side = pair_reduce("ple", dict(
        w_ple_gate=_tn_mm("dw_ple_gate", h2b, dpre, wt(D), wt(D), TK),
        w_ple_proj=_tn_mm("dw_ple_proj", pb, dpp, wt(P), wt(D // N_CHIPS), TK, cols_per_chip=D // N_CHIPS)))
    (dact, dh2b), got = _rows_mm("down_bwd", S, TM, F, _mm_tile(F, D), row_ins=[dh2], weights=[(w_down_f, "nt2")],
                                 tile_outs=[BF16], row_outs=[(D, BF16)], prologue=cast_prologue, epilogue=plain,
                                 side=side)
    landed(ns, got)
    ns, side = pair_reduce("down", dict(w_down=_tn_mm("dw_down", act, dh2b, wt(F), wt(D), TK)))
    (du0, g_conv_gate, g_conv_up), got = _ffn_bwd(u0, dact, conv_ffn_f, S, TM, F, TC, side=side)
    landed(ns, got)
    g_conv_ffn = jnp.concatenate([g_conv_gate, g_conv_up], axis=1)
    ns, side = pair_reduce("up", dict(
        w_up=_tn_mm("dw_up", hn2, du0, wt(D), TC, TK, cols_per_chip=2 * F // N_CHIPS, place=ffn_place)))

    def up_bwd_epilogue(acc, rows, row_r, vec_r, ro_r, ao_r):
        dh, dg = _rms_bwd(row_r[0][rows, :], vec_r[0][...], acc)
        dh1_ = row_r[1][rows, :] + dh
        ro_r[0][rows, :] = dh1_
        ro_r[1][rows, :] = dh1_.astype(BF16)
        ao_r[0][...] += dg

    (dh1, dh1b, g_norm_ffn), got = _kloop_mm(
        "up_bwd", S, TM, du0, w_up3, TC, row_ins=[h1, dh2], vec_ins=[norm_ffn_g],
        row_outs=[(D, F32), (D, BF16)], acc_outs=[(1, D)], epilogue=up_bwd_epilogue, place=ffn_place, side=side)
    landed(ns, got)
    ns, side = pair_reduce("out", dict(w_out=_tn_mm("dw_out", cat, dh1b, wt(2 * A), wt(D), TK)))
    (dcat,) = _rows_mm("out_bwd", S, TM, 2 * A, _mm_tile(2 * A, D), row_ins=[dh1b], weights=[(w_out_f, "nt2")],
                       tile_outs=[BF16], epilogue=plain)
    da1, ln_sums = _mixer_bwd_ln(dcat, a1, ln_a_g, ln_a_b, S, TE, A)
    (dz, g_conv_a, g_conv_b), got = _mixer_bwd_conv(z, dcat, da1, conv_a_f, conv_b_f, S, TE, A, side=side)
    landed(ns, got)
    ns, side = pair_reduce("in", dict(
        w_in=_tn_mm("dw_in", hn1, dz, wt(D), wt(5 * A // N_CHIPS), TK, cols_per_chip=5 * A // N_CHIPS)))

    def in_bwd_epilogue(acc, rows, row_r, vec_r, ro_r, ao_r):
        dh, dg = _rms_bwd(row_r[0][rows, :], vec_r[0][...], acc)
        ro_r[0][rows, :] = row_r[1][rows, :] + dh
        ao_r[0][...] += dg

    (dx, g_norm_mix), got = _kloop_mm(
        "in_bwd", S, TM, dz, w_in3, _mm_tile(5 * A // N_CHIPS, D), row_ins=[x2, dh1],
        vec_ins=[norm_mix_g], row_outs=[(D, F32)], acc_outs=[(1, D)], epilogue=in_bwd_epilogue, side=side)
    landed(ns, got)

    reduced = _share_halves([_add_chips("chip_sum_" + n, chip_sums[n], from_chips[n], chip, ci) for n in names])
    moments = dict(w_in=(m_w_in, v_w_in), w_out=(m_w_out, v_w_out), w_up=(m_w_up, v_w_up),
                   w_down=(m_w_down, v_w_down), w_ple_gate=(m_w_ple_gate, v_w_ple_gate),
                   w_ple_proj=(m_w_ple_proj, v_w_ple_proj))
    grads, deltas, new_m, new_v = {}, {}, {}, {}
    for n, g in zip(names, reduced):
        g = g.reshape(big[n].shape)
        d_, m_, v_ = _adamw("adamw_" + n, big[n], g, moments[n][0][0], moments[n][1][0])
        grads[n], deltas[n], new_m[n], new_v[n] = g[None], d_[None], m_[None], v_[None]

    small = ["norm_mix_g", "conv_a_w", "conv_a_b", "ln_a_g", "ln_a_b", "conv_b_w", "norm_ffn_g",
             "conv_ffn_w", "b_ple_gate", "norm_final_g"]
    small_part = [g_norm_mix, g_conv_a, ln_sums[2:3], ln_sums[0:1], ln_sums[1:2], g_conv_b, g_norm_ffn,
                  g_conv_ffn, g_b_gate, g_norm_final]
    full_shapes = [a.shape for a in small_part]
    summed = _sum_devices("small_grads_sum", _allgather_small("allgather_small_grads", _pack(small_part)))
    small_g = dict(zip(small, _unpack(summed, full_shapes)))
    for n, width in (("conv_a_w", A), ("conv_b_w", A), ("conv_ffn_w", 2 * F)):
        small_g[n] = lax.dynamic_slice_in_dim(small_g[n], chip * (width // N_CHIPS), width // N_CHIPS, axis=1)
    small_w = dict(norm_mix_g=(norm_mix_g, m_norm_mix_g, v_norm_mix_g), conv_a_w=(conv_a_w, m_conv_a_w, v_conv_a_w),
                   conv_a_b=(conv_a_b, m_conv_a_b, v_conv_a_b), ln_a_g=(ln_a_g, m_ln_a_g, v_ln_a_g),
                   ln_a_b=(ln_a_b, m_ln_a_b, v_ln_a_b), conv_b_w=(conv_b_w, m_conv_b_w, v_conv_b_w),
                   norm_ffn_g=(norm_ffn_g, m_norm_ffn_g, v_norm_ffn_g),
                   conv_ffn_w=(conv_ffn_w, m_conv_ffn_w, v_conv_ffn_w),
                   b_ple_gate=(b_ple_gate, m_b_ple_gate, v_b_ple_gate),
                   norm_final_g=(norm_final_g, m_norm_final_g, v_norm_final_g))
    out_shapes = [small_w[n][0].shape for n in small]
    packed_g = _pack([small_g[n] for n in small])
    packed = [_pack([small_w[n][k] for n in small]) for k in range(3)]
    d_s, m_s, v_s = _adamw("adamw_small", packed[0], packed_g, packed[1], packed[2])
    for n, g, d_, m_, v_ in zip(small, _unpack(packed_g, out_shapes), _unpack(d_s, out_shapes),
                                _unpack(m_s, out_shapes), _unpack(v_s, out_shapes)):
        grads[n], deltas[n], new_m[n], new_v[n] = g, d_, m_, v_

    order = ["norm_mix_g", "w_in", "conv_a_w", "conv_a_b", "ln_a_g", "ln_a_b", "conv_b_w", "w_out", "norm_ffn_g",
             "w_up", "conv_ffn_w", "w_down", "w_ple_gate", "b_ple_gate", "w_ple_proj", "norm_final_g"]
    loss = lax.psum(loss_part[0, 0], ("x", "y", "c"))
    return (loss, dx.reshape(x.shape), *[grads[n] for n in order], *[deltas[n] for n in order],
            *[new_m[n] for n in order], *[new_v[n] for n in order])
```

```python
from typing import Callable, NamedTuple

import jax
import jax.numpy as jnp
from jax import lax
from jax.experimental import pallas as pl
from jax.experimental.pallas import tpu as pltpu

F32 = jnp.float32
BF16 = jnp.bfloat16
MESH = pl.DeviceIdType.MESH
ANY = pl.BlockSpec(memory_space=pl.ANY)

EPS = 1e-6
ADAM_LR = 0.001
ADAM_B1 = 0.9
ADAM_B2 = 0.999
ADAM_EPS = 1e-08
ADAM_WD = 0.01
ADAM_STEP = 10

N_CHIPS = 4
N_DEV = 8
LANES = 128
SUBLANES = 8
PACK_ALIGN = LANES * SUBLANES
ROW_CHUNK = 32
VMEM_CAP = 60 * 1024 * 1024
VMEM_SLACK = 6 * 1024 * 1024


def _pick(n, cands):
    for c in cands:
        if n % c == 0:
            return c
    raise ValueError(f"no tile of {cands} divides {n}")


def _nbytes(shape, dtype):
    n = 1
    for s in shape:
        if s is not None:
            n *= s
    return n * jnp.dtype(dtype).itemsize


def _params(sem, blocks, scratch=(), temps=()):
    est = (2 * sum(_nbytes(s, d) for s, d in blocks) + sum(_nbytes(s, d) for s, d in scratch)
           + sum(_nbytes(s, d) for s, d in temps))
    return pltpu.CompilerParams(dimension_semantics=sem,
                                vmem_limit_bytes=min(est + VMEM_SLACK, VMEM_CAP))


def _sigmoid(x):
    return 1.0 / (1.0 + jnp.exp(-x))


def _rsum(x):
    return jnp.sum(x, axis=0, keepdims=True)


class _Side(NamedTuple):
    ins: list
    out_shapes: list
    n_sems: int
    start: Callable
    wait: Callable
    aliased: bool = False


def _call(body, side, *, name, grid, in_specs, out_specs, out_shape, scratch, params, args):
    vmem = [pltpu.VMEM(s, d) for s, d in scratch]
    if side is None:
        outs = pl.pallas_call(body, name=name, grid=grid, in_specs=in_specs, out_specs=out_specs,
                              out_shape=out_shape, scratch_shapes=vmem, compiler_params=params)(*args)
        return list(outs), []
    n_in, n_out, n_sc = len(in_specs), len(out_specs), len(scratch)
    ns_in, ns_out = len(side.ins), len(side.out_shapes)

    def carrier(*refs):
        pos = [0]
        def take(n):
            pos[0] += n
            return refs[pos[0] - n:pos[0]]
        ins, s_ins, outs, s_outs, scr = take(n_in), take(ns_in), take(n_out), take(ns_out), take(n_sc)
        send_sems, recv_sems = take(2)
        first = last = None
        for axis, extent in enumerate(grid):
            at_start, at_end = pl.program_id(axis) == 0, pl.program_id(axis) == extent - 1
            first = at_start if first is None else first & at_start
            last = at_end if last is None else last & at_end

        @pl.when(first)
        def _():
            side.start(s_ins, s_outs, send_sems, recv_sems, 0)
        body(*ins, *outs, *scr)

        @pl.when(last)
        def _():
            side.wait(s_ins, s_outs, send_sems, recv_sems, 0)

    outs = pl.pallas_call(
        carrier, name=name, grid=grid, in_specs=list(in_specs) + [ANY] * ns_in,
        out_specs=list(out_specs) + [ANY] * ns_out, out_shape=list(out_shape) + list(side.out_shapes),
        scratch_shapes=vmem + [pltpu.SemaphoreType.DMA((side.n_sems,)), pltpu.SemaphoreType.DMA((side.n_sems,))],
        input_output_aliases={n_in + i: n_out + i for i in range(ns_in)} if side.aliased else {},
        compiler_params=params)(*args, *side.ins)
    return list(outs[:n_out]), list(outs[n_out:])


def _comm_only(name, side):
    n_in = len(side.ins)

    def body(*refs):
        ins, outs = refs[:n_in], refs[n_in:n_in + len(side.out_shapes)]
        send_sems, recv_sems = refs[n_in + len(side.out_shapes):]
        side.start(ins, outs, send_sems, recv_sems, 0)
        side.wait(ins, outs, send_sems, recv_sems, 0)

    return pl.pallas_call(
        body, name=name, out_shape=list(side.out_shapes), in_specs=[ANY] * n_in,
        out_specs=[ANY] * len(side.out_shapes),
        scratch_shapes=[pltpu.SemaphoreType.DMA((side.n_sems,)), pltpu.SemaphoreType.DMA((side.n_sems,))],
        input_output_aliases={i: i for i in range(n_in)} if side.aliased else {},
    )(*side.ins)


def _rms_stats(x):
    return lax.rsqrt(jnp.mean(x * x, axis=-1, keepdims=True) + EPS)


def _rms_bwd(h, g, dout):
    r = _rms_stats(h)
    n = h * r
    dn = dout * g
    dh = r * (dn - n * jnp.mean(dn * n, axis=-1, keepdims=True))
    return dh, _rsum(dout * n)


def _identity(t):
    return t


def _chip_major(nb, place=_identity):
    return lambda i, j: (place(j) // nb, 0, place(j) % nb)


def _rows_mm(name, S, TM, N, TN, *, row_ins, vec_ins=(), colvec_ins=(), weights, tile_ins=(),
             tile_outs, row_outs=(), acc_outs=(), prologue=None, epilogue, place=_identity, side=None):
    nI, nJ = S // TM, N // TN
    n_row, n_vec, n_cv, n_w, n_tile = len(row_ins), len(vec_ins), len(colvec_ins), len(weights), len(tile_ins)
    n_to, n_ro, n_ao = len(tile_outs), len(row_outs), len(acc_outs)

    in_specs, blocks, scratch, ks = [], [], [], []
    for a in row_ins:
        in_specs.append(pl.BlockSpec((TM, a.shape[1]), lambda i, j: (i, 0)))
        blocks.append(((TM, a.shape[1]), a.dtype))
    for a in vec_ins:
        in_specs.append(pl.BlockSpec(a.shape, lambda i, j: (0, 0)))
        blocks.append((a.shape, a.dtype))
    for a in colvec_ins:
        in_specs.append(pl.BlockSpec((1, TN), lambda i, j: (0, j)))
        blocks.append(((1, TN), a.dtype))
    for w, mode in weights:
        if mode == "nn2":
            k = w.shape[0]
            in_specs.append(pl.BlockSpec((k, TN), lambda i, j: (0, j)))
        elif mode == "nn3":
            k = w.shape[1]
            in_specs.append(pl.BlockSpec((None, k, TN), _chip_major(w.shape[2] // TN, place)))
        else:
            k = w.shape[1]
            in_specs.append(pl.BlockSpec((TN, k), lambda i, j: (j, 0)))
        ks.append(k)
        blocks.append(((k, TN), BF16))
        if prologue is not None:
            scratch.append(((TM, k), BF16))
    for a in tile_ins:
        in_specs.append(pl.BlockSpec((TM, TN), lambda i, j: (i, j)))
        blocks.append(((TM, TN), a.dtype))

    out_shape, out_specs = [], []
    for dt in tile_outs:
        out_shape.append(jax.ShapeDtypeStruct((S, N), dt))
        out_specs.append(pl.BlockSpec((TM, TN), lambda i, j: (i, j)))
        blocks.append(((TM, TN), dt))
    for width, dt in row_outs:
        out_shape.append(jax.ShapeDtypeStruct((S, width), dt))
        out_specs.append(pl.BlockSpec((TM, width), lambda i, j: (i, 0)))
        blocks.append(((TM, width), dt))
    for rows, width in acc_outs:
        out_shape.append(jax.ShapeDtypeStruct((rows, width), F32))
        out_specs.append(pl.BlockSpec((rows, width), lambda i, j: (0, 0)))
        blocks.append(((rows, width), F32))

    modes = [m for _, m in weights]

    def body(*refs):
        pos = 0
        def take(n):
            nonlocal pos
            out = refs[pos:pos + n]
            pos += n
            return out
        row_r, vec_r, cv_r, w_r, tile_r = take(n_row), take(n_vec), take(n_cv), take(n_w), take(n_tile)
        to_r, ro_r, ao_r, a_sc = take(n_to), take(n_ro), take(n_ao), take(len(scratch))
        i, j = pl.program_id(0), pl.program_id(1)

        if prologue is None:
            a_sc = row_r[:n_w]
        else:
            @pl.when(j == 0)
            def _():
                if n_ao:
                    @pl.when(i == 0)
                    def _():
                        for r in ao_r:
                            r[...] = jnp.zeros_like(r)

                def chunk(ci, carry):
                    rows = pl.ds(pl.multiple_of(ci * ROW_CHUNK, ROW_CHUNK), ROW_CHUNK)
                    for sc, a in zip(a_sc, prologue(rows, row_r, vec_r, ro_r, ao_r)):
                        sc[rows, :] = a
                    return carry
                lax.fori_loop(0, TM // ROW_CHUNK, chunk, 0)

        accs = []
        for w_ref, sc, mode in zip(w_r, a_sc, modes):
            if mode == "nt2":
                accs.append(lax.dot_general(sc[...], w_ref[...], (((1,), (1,)), ((), ())),
                                            preferred_element_type=F32))
            else:
                accs.append(jnp.dot(sc[...], w_ref[...], preferred_element_type=F32))
        outs = epilogue(accs, tile_r, cv_r)
        for r, o in zip(to_r, outs):
            r[...] = o.astype(r.dtype)

    outs, side_outs = _call(
        body, side, name=name, grid=(nI, nJ), in_specs=in_specs, out_specs=out_specs, out_shape=out_shape,
        scratch=scratch, params=_params(("arbitrary", "arbitrary"), blocks, scratch, temps=[((TM, TN), F32)] * 3),
        args=[*row_ins, *vec_ins, *colvec_ins, *[w for w, _ in weights], *tile_ins])
    return outs if side is None else (outs, side_outs)


def _kloop_mm(name, S, TM, a, w3, TK, *, row_ins, vec_ins, row_outs, acc_outs, epilogue, place=_identity,
              side=None):
    _, N, Ks = w3.shape
    nb = Ks // TK
    nK = N_CHIPS * nb
    n_row, n_vec, n_ro, n_ao = len(row_ins), len(vec_ins), len(row_outs), len(acc_outs)

    in_specs = [pl.BlockSpec((TM, TK), lambda i, k: (i, k)),
                pl.BlockSpec((None, N, TK), _chip_major(nb, place))]
    blocks = [((TM, TK), BF16), ((N, TK), BF16)]
    for r in row_ins:
        in_specs.append(pl.BlockSpec((TM, r.shape[1]), lambda i, k: (i, 0)))
        blocks.append(((TM, r.shape[1]), r.dtype))
    for v in vec_ins:
        in_specs.append(pl.BlockSpec(v.shape, lambda i, k: (0, 0)))
        blocks.append((v.shape, v.dtype))
    out_shape, out_specs = [], []
    for width, dt in row_outs:
        out_shape.append(jax.ShapeDtypeStruct((S, width), dt))
        out_specs.append(pl.BlockSpec((TM, width), lambda i, k: (i, 0)))
        blocks.append(((TM, width), dt))
    for rows, width in acc_outs:
        out_shape.append(jax.ShapeDtypeStruct((rows, width), F32))
        out_specs.append(pl.BlockSpec((rows, width), lambda i, k: (0, 0)))
        blocks.append(((rows, width), F32))
    scratch = [((TM, N), F32)]

    def body(*refs):
        a_ref, w_ref = refs[0], refs[1]
        row_r = refs[2:2 + n_row]
        vec_r = refs[2 + n_row:2 + n_row + n_vec]
        pos = 2 + n_row + n_vec
        ro_r = refs[pos:pos + n_ro]
        ao_r = refs[pos + n_ro:pos + n_ro + n_ao]
        acc_sc = refs[pos + n_ro + n_ao]
        i, k = pl.program_id(0), pl.program_id(1)
        @pl.when(k == 0)
        def _():
            acc_sc[...] = jnp.zeros_like(acc_sc)
        acc_sc[...] += lax.dot_general(a_ref[...], w_ref[...], (((1,), (1,)), ((), ())),
                                       preferred_element_type=F32)

        @pl.when(k == nK - 1)
        def _():
            @pl.when(i == 0)
            def _():
                for r in ao_r:
                    r[...] = jnp.zeros_like(r)

            def chunk(ci, carry):
                rows = pl.ds(pl.multiple_of(ci * ROW_CHUNK, ROW_CHUNK), ROW_CHUNK)
                epilogue(acc_sc[rows, :], rows, row_r, vec_r, ro_r, ao_r)
                return carry
            lax.fori_loop(0, TM // ROW_CHUNK, chunk, 0)

    outs, side_outs = _call(
        body, side, name=name, grid=(S // TM, nK), in_specs=in_specs, out_specs=out_specs, out_shape=out_shape,
        scratch=scratch, params=_params(("arbitrary", "arbitrary"), blocks, scratch, temps=[((TM, N), F32)]),
        args=[a, w3, *row_ins, *vec_ins])
    return outs if side is None else (outs, side_outs)


def _tn_mm(name, a, b, TMw, TNw, TK, cols_per_chip=None, place=_identity):
    S, M = a.shape
    N = b.shape[1]
    nK = S // TK
    if cols_per_chip is None:
        out_shape = jax.ShapeDtypeStruct((M, N), F32)
        out_spec = pl.BlockSpec((TMw, TNw), lambda i, j, k: (i, j))
    else:
        nb = cols_per_chip // TNw
        out_shape = jax.ShapeDtypeStruct((N_CHIPS, M, cols_per_chip), F32)
        out_spec = pl.BlockSpec((None, TMw, TNw), lambda i, j, k: (place(j) // nb, i, place(j) % nb))

    def body(a_ref, b_ref, o_ref):
        @pl.when(pl.program_id(2) == 0)
        def _():
            o_ref[...] = jnp.zeros_like(o_ref)
        o_ref[...] += lax.dot_general(a_ref[...], b_ref[...], (((0,), (0,)), ((), ())),
                                      preferred_element_type=F32)

    blocks = [((TK, TMw), BF16), ((TK, TNw), BF16), ((TMw, TNw), F32)]
    return pl.pallas_call(
        body, name=name, grid=(M // TMw, N // TNw, nK),
        in_specs=[pl.BlockSpec((TK, TMw), lambda i, j, k: (k, i)),
                  pl.BlockSpec((TK, TNw), lambda i, j, k: (k, j))],
        out_specs=out_spec, out_shape=out_shape,
        compiler_params=_params(("arbitrary", "arbitrary", "arbitrary"), blocks,
                                temps=[((TMw, TNw), F32), ((TK, TMw), BF16)]),
    )(a, b)


def _prev_rows(TM, H, col):
    return lambda i: (jnp.maximum(i * (TM // H) - 1, 0), col)


def _next_rows(S, TM, H, col):
    return lambda i: (jnp.minimum((i + 1) * (TM // H), S // H - 1), col)


def _taps_causal(ext_ref, w_ref, K, H, TM, cs):
    acc = None
    for k in range(K):
        term = ext_ref[pl.ds(H - (K - 1) + k, TM), cs] * w_ref[pl.ds(k, 1), cs]
        acc = term if acc is None else acc + term
    return acc


def _taps_anticausal(ext_ref, w_ref, K, TM, cs):
    acc = None
    for k in range(K):
        term = ext_ref[pl.ds(K - 1 - k, TM), cs] * w_ref[pl.ds(k, 1), cs]
        acc = term if acc is None else acc + term
    return acc


def _tap_grads(ext_ref, g, K, H, TM, cs):
    return [_rsum(ext_ref[pl.ds(H - (K - 1) + k, TM), cs] * g) for k in range(K)]


def _mixer_fwd(z, conv_a_w, conv_a_b, ln_g, ln_b, conv_b_w, S, TM, A, side=None):
    H = 32
    KA, KB = conv_a_w.shape[0], conv_b_w.shape[0]
    n_chunks = A // LANES
    RB = _pick(TM, (64, 32))

    def body(zc_ref, zh_ref, wa_ref, ba_ref, g_ref, b_ref, wb_ref, a1_ref, cat_ref, ext_a, ext_b):
        i = pl.program_id(0)
        live = (i > 0).astype(F32)
        zc = zc_ref[...].astype(F32)
        zh = zh_ref[...].astype(F32) * live
        ext_a[pl.ds(0, H), :] = zh[:, 0:A] * _sigmoid(zh[:, A:2 * A])
        ext_a[pl.ds(H, TM), :] = zc[:, 0:A] * _sigmoid(zc[:, A:2 * A])
        ext_b[pl.ds(0, H), :] = zh[:, 3 * A:4 * A] * zh[:, 4 * A:5 * A]
        ext_b[pl.ds(H, TM), :] = zc[:, 3 * A:4 * A] * zc[:, 4 * A:5 * A]

        def chunk(c, carry):
            cs = pl.ds(pl.multiple_of(c * LANES, LANES), LANES)
            for r0 in range(0, TM, RB):
                acc = None
                for k in range(KA):
                    term = ext_a[pl.ds(H - (KA - 1) + k + r0, RB), cs] * wa_ref[pl.ds(k, 1), cs]
                    acc = term if acc is None else acc + term
                a1_ref[pl.ds(r0, RB), cs] = acc + ba_ref[:, cs]
            return carry
        lax.fori_loop(0, n_chunks, chunk, 0)

        a1 = a1_ref[...]
        mu = jnp.mean(a1, axis=-1, keepdims=True)
        d = a1 - mu
        var = jnp.mean(d * d, axis=-1, keepdims=True)
        a2 = d * lax.rsqrt(var + EPS) * g_ref[...] + b_ref[...]
        cat_ref[:, 0:A] = (a2 * _sigmoid(a2)).astype(BF16)
        cbc = _taps_causal(ext_b, wb_ref, KB, H, TM, slice(None))
        cat_ref[:, A:2 * A] = (zc[:, 2 * A:3 * A] * cbc).astype(BF16)

    blocks = [((TM, 5 * A), BF16), ((H, 5 * A), BF16), ((KA, A), F32), ((KB, A), F32),
              ((TM, A), F32), ((TM, 2 * A), BF16)]
    scratch = [((H + TM, A), F32), ((H + TM, A), F32)]
    vec = lambda r: pl.BlockSpec((r, A), lambda i: (0, 0))
    outs, side_outs = _call(
        body, side, name="mixer_fwd", grid=(S // TM,),
        in_specs=[pl.BlockSpec((TM, 5 * A), lambda i: (i, 0)),
                  pl.BlockSpec((H, 5 * A), _prev_rows(TM, H, 0)),
                  vec(KA), vec(1), vec(1), vec(1), vec(KB)],
        out_specs=[pl.BlockSpec((TM, A), lambda i: (i, 0)), pl.BlockSpec((TM, 2 * A), lambda i: (i, 0))],
        out_shape=[jax.ShapeDtypeStruct((S, A), F32), jax.ShapeDtypeStruct((S, 2 * A), BF16)],
        scratch=scratch,
        params=_params(("arbitrary",), blocks, scratch, temps=[((TM, 5 * A), F32)] * 2 + [((TM, A), F32)] * 10),
        args=[z, z, conv_a_w, conv_a_b, ln_g, ln_b, conv_b_w])
    return outs if side is None else (outs, side_outs)


def _pair_tile(nF):
    return lambda t: (t % 2) * nF + t // 2


FFN_ROWS = 16


def _bcast_taps(w_ref, K, lanes):
    return [jnp.broadcast_to(w_ref[pl.ds(k, 1), lanes], (FFN_ROWS, LANES)) for k in range(K)]


def _ffn_act(u0, conv_w, S, TM, F, TC, side=None):
    H = 16
    K = conv_w.shape[0]
    nF = F // TC

    def body(uc_ref, uh_ref, wg_ref, wu_ref, o_ref, ext):
        live = (pl.program_id(0) > 0).astype(F32)
        ext[pl.ds(0, H), :] = uh_ref[...].astype(F32) * live
        ext[pl.ds(H, TM), :] = uc_ref[...].astype(F32)

        def lane_chunk(c, carry):
            lo = pl.ds(pl.multiple_of(c * LANES, LANES), LANES)
            lg, lu = lo, pl.ds(pl.multiple_of(TC + c * LANES, LANES), LANES)
            wg, wu = _bcast_taps(wg_ref, K, lo), _bcast_taps(wu_ref, K, lo)
            for r0 in range(0, TM, FFN_ROWS):
                g = u = None
                for k in range(K):
                    rows = pl.ds(H - (K - 1) + k + r0, FFN_ROWS)
                    tg, tu = ext[rows, lg] * wg[k], ext[rows, lu] * wu[k]
                    g, u = (tg, tu) if g is None else (g + tg, u + tu)
                o_ref[pl.ds(r0, FFN_ROWS), lo] = (g * _sigmoid(g) * u).astype(BF16)
            return carry
        lax.fori_loop(0, TC // LANES, lane_chunk, 0)

    blocks = [((TM, 2 * TC), BF16), ((H, 2 * TC), BF16), ((K, TC), F32), ((K, TC), F32), ((TM, TC), BF16)]
    scratch = [((H + TM, 2 * TC), F32)]
    outs, side_outs = _call(
        body, side, name="ffn_act", grid=(S // TM, nF),
        in_specs=[pl.BlockSpec((TM, 2 * TC), lambda i, j: (i, j)),
                  pl.BlockSpec((H, 2 * TC), lambda i, j: (jnp.maximum(i * (TM // H) - 1, 0), j)),
                  pl.BlockSpec((K, TC), lambda i, j: (0, j)),
                  pl.BlockSpec((K, TC), lambda i, j: (0, j + nF))],
        out_specs=[pl.BlockSpec((TM, TC), lambda i, j: (i, j))],
        out_shape=[jax.ShapeDtypeStruct((S, F), BF16)],
        scratch=scratch,
        params=_params(("arbitrary", "arbitrary"), blocks, scratch, temps=[((TM, 2 * TC), F32)]),
        args=[u0, u0, conv_w, conv_w])
    return outs if side is None else (outs, side_outs)


def _loss_head(h3, target, g_final, pp, gate, S, TM, D):
    def body(h_ref, t_ref, g_ref, pp_ref, gt_ref, loss_ref, dg_ref, dh_ref, dpre_ref, dpp_ref, db_ref):
        @pl.when(pl.program_id(0) == 0)
        def _():
            loss_ref[...] = jnp.zeros_like(loss_ref)
            dg_ref[...] = jnp.zeros_like(dg_ref)
            db_ref[...] = jnp.zeros_like(db_ref)
        h = h_ref[...]
        g = g_ref[...]
        r = _rms_stats(h)
        n = h * r
        diff = n * g - t_ref[...]
        loss_ref[...] += 0.5 * jnp.sum(jnp.mean(diff * diff, axis=-1, keepdims=True), axis=0, keepdims=True)
        dy = diff * (1.0 / D)
        dn = dy * g
        dh = r * (dn - n * jnp.mean(dn * n, axis=-1, keepdims=True))
        dh_ref[...] = dh
        dg_ref[...] += _rsum(dy * n)
        gt = gt_ref[...].astype(F32)
        dpre = dh * pp_ref[...].astype(F32) * gt * (1.0 - gt)
        dpre_ref[...] = dpre.astype(BF16)
        dpp_ref[...] = (dh * gt).astype(BF16)
        db_ref[...] += _rsum(dpre)

    blocks = [((TM, D), F32)] * 3 + [((TM, D), BF16)] * 4 + [((1, D), F32)] * 3
    row = pl.BlockSpec((TM, D), lambda i: (i, 0))
    vec = pl.BlockSpec((1, D), lambda i: (0, 0))
    return pl.pallas_call(
        body, name="loss_head", grid=(S // TM,),
        in_specs=[row, row, vec, row, row],
        out_specs=[pl.BlockSpec((1, 1), lambda i: (0, 0)), vec, row, row, row, vec],
        out_shape=[jax.ShapeDtypeStruct((1, 1), F32), jax.ShapeDtypeStruct((1, D), F32),
                   jax.ShapeDtypeStruct((S, D), F32), jax.ShapeDtypeStruct((S, D), BF16),
                   jax.ShapeDtypeStruct((S, D), BF16), jax.ShapeDtypeStruct((1, D), F32)],
        compiler_params=_params(("arbitrary",), blocks, temps=[((TM, D), F32)] * 10),
    )(h3, target, g_final, pp, gate)


def _ffn_bwd(u0, dact, conv_w, S, TM, F, TC, side=None):
    H = FFN_ROWS
    K = conv_w.shape[0]
    nF, nI = F // TC, S // TM

    def body(up_ref, uc_ref, un_ref, dc_ref, dn_ref, wg_ref, wu_ref, o_ref, dwg_ref, dwu_ref,
             ext_u, ext_d, ext_a):
        i = pl.program_id(1)
        @pl.when(i == 0)
        def _():
            dwg_ref[...] = jnp.zeros_like(dwg_ref)
            dwu_ref[...] = jnp.zeros_like(dwu_ref)
        last = (i < nI - 1).astype(F32)
        ext_u[pl.ds(0, H), :] = up_ref[...].astype(F32) * (i > 0).astype(F32)
        ext_u[pl.ds(H, TM), :] = uc_ref[...].astype(F32)
        ext_u[pl.ds(H + TM, H), :] = un_ref[...].astype(F32) * last
        ext_a[pl.ds(0, TM), :] = dc_ref[...].astype(F32)
        ext_a[pl.ds(TM, H), :] = dn_ref[...].astype(F32) * last

        def lane_chunk(c, carry):
            lo = pl.ds(pl.multiple_of(c * LANES, LANES), LANES)
            lg, lu = lo, pl.ds(pl.multiple_of(TC + c * LANES, LANES), LANES)
            wg, wu = _bcast_taps(wg_ref, K, lo), _bcast_taps(wu_ref, K, lo)
            sums_g, sums_u = [None] * K, [None] * K
            for r0 in range(0, TM + H, FFN_ROWS):
                xg = [ext_u[pl.ds(H - (K - 1) + k + r0, FFN_ROWS), lg] for k in range(K)]
                xu = [ext_u[pl.ds(H - (K - 1) + k + r0, FFN_ROWS), lu] for k in range(K)]
                g, u = xg[0] * wg[0], xu[0] * wu[0]
                for k in range(1, K):
                    g, u = g + xg[k] * wg[k], u + xu[k] * wu[k]
                da = ext_a[pl.ds(r0, FFN_ROWS), lo]
                s = _sigmoid(g)
                dg = da * u * s * (1.0 + g * (1.0 - s))
                du = da * g * s
                ext_d[pl.ds(r0, FFN_ROWS), lg] = dg
                ext_d[pl.ds(r0, FFN_ROWS), lu] = du
                if r0 < TM:
                    for k in range(K):
                        tg, tu = xg[k] * dg, xu[k] * du
                        sums_g[k] = tg if sums_g[k] is None else sums_g[k] + tg
                        sums_u[k] = tu if sums_u[k] is None else sums_u[k] + tu
            for k in range(K):
                dwg_ref[pl.ds(k, 1), lo] += _rsum(sums_g[k])
                dwu_ref[pl.ds(k, 1), lo] += _rsum(sums_u[k])
            for r0 in range(0, TM, FFN_ROWS):
                g = u = None
                for k in range(K):
                    rows = pl.ds(K - 1 - k + r0, FFN_ROWS)
                    tg, tu = ext_d[rows, lg] * wg[k], ext_d[rows, lu] * wu[k]
                    g, u = (tg, tu) if g is None else (g + tg, u + tu)
                o_ref[pl.ds(r0, FFN_ROWS), lg] = g.astype(BF16)
                o_ref[pl.ds(r0, FFN_ROWS), lu] = u.astype(BF16)
            return carry
        lax.fori_loop(0, TC // LANES, lane_chunk, 0)

    blocks = [((H, 2 * TC), BF16), ((TM, 2 * TC), BF16), ((H, 2 * TC), BF16), ((TM, TC), BF16), ((H, TC), BF16),
              ((K, TC), F32), ((K, TC), F32), ((TM, 2 * TC), BF16), ((K, TC), F32), ((K, TC), F32)]
    scratch = [((TM + 2 * H, 2 * TC), F32), ((TM + H, 2 * TC), F32), ((TM + H, TC), F32)]
    prev = lambda j, i: (jnp.maximum(i * (TM // H) - 1, 0), j)
    nxt = lambda j, i: (jnp.minimum((i + 1) * (TM // H), S // H - 1), j)
    taps_out = pl.BlockSpec((K, TC), lambda j, i: (0, j))
    outs, side_outs = _call(
        body, side, name="ffn_bwd", grid=(nF, nI),
        in_specs=[pl.BlockSpec((H, 2 * TC), prev), pl.BlockSpec((TM, 2 * TC), lambda j, i: (i, j)),
                  pl.BlockSpec((H, 2 * TC), nxt),
                  pl.BlockSpec((TM, TC), lambda j, i: (i, j)), pl.BlockSpec((H, TC), nxt),
                  pl.BlockSpec((K, TC), lambda j, i: (0, j)), pl.BlockSpec((K, TC), lambda j, i: (0, j + nF))],
        out_specs=[pl.BlockSpec((TM, 2 * TC), lambda j, i: (i, j)), taps_out, taps_out],
        out_shape=[jax.ShapeDtypeStruct((S, 2 * F), BF16), jax.ShapeDtypeStruct((K, F), F32),
                   jax.ShapeDtypeStruct((K, F), F32)],
        scratch=scratch,
        params=_params(("arbitrary", "arbitrary"), blocks, scratch, temps=[((TM, 2 * TC), F32)]),
        args=[u0, u0, u0, dact, dact, conv_w, conv_w])
    return outs if side is None else (outs, side_outs)


def _mixer_bwd_ln(dcat, a1, ln_g, ln_b, S, TM, A):
    def body(dc_ref, a1_ref, g_ref, b_ref, da1_ref, acc_ref):
        @pl.when(pl.program_id(0) == 0)
        def _():
            acc_ref[...] = jnp.zeros_like(acc_ref)
        a1 = a1_ref[...]
        g = g_ref[...]
        mu = jnp.mean(a1, axis=-1, keepdims=True)
        d = a1 - mu
        rstd = lax.rsqrt(jnp.mean(d * d, axis=-1, keepdims=True) + EPS)
        nh = d * rstd
        a2 = nh * g + b_ref[...]
        s = _sigmoid(a2)
        da2 = dc_ref[...].astype(F32) * s * (1.0 + a2 * (1.0 - s))
        dnh = da2 * g
        da1 = rstd * (dnh - jnp.mean(dnh, axis=-1, keepdims=True)
                      - nh * jnp.mean(dnh * nh, axis=-1, keepdims=True))
        da1_ref[...] = da1
        acc_ref[pl.ds(0, 1), :] += _rsum(da2 * nh)
        acc_ref[pl.ds(1, 1), :] += _rsum(da2)
        acc_ref[pl.ds(2, 1), :] += _rsum(da1)

    blocks = [((TM, A), BF16), ((TM, A), F32), ((TM, A), F32), ((4, A), F32)]
    return pl.pallas_call(
        body, name="mixer_bwd_ln", grid=(S // TM,),
        in_specs=[pl.BlockSpec((TM, A), lambda i: (i, 0)), pl.BlockSpec((TM, A), lambda i: (i, 0)),
                  pl.BlockSpec((1, A), lambda i: (0, 0)), pl.BlockSpec((1, A), lambda i: (0, 0))],
        out_specs=[pl.BlockSpec((TM, A), lambda i: (i, 0)), pl.BlockSpec((4, A), lambda i: (0, 0))],
        out_shape=[jax.ShapeDtypeStruct((S, A), F32), jax.ShapeDtypeStruct((4, A), F32)],
        compiler_params=_params(("arbitrary",), blocks, temps=[((TM, A), F32)] * 12),
    )(dcat, a1, ln_g, ln_b)


def _mixer_bwd_conv(z, dcat, da1, conv_a_w, conv_b_w, S, TM, A, side=None):
    H = 32
    KA, KB = conv_a_w.shape[0], conv_b_w.shape[0]
    nI = S // TM
    n_chunks = A // LANES
    RB = _pick(TM, (64, 32))

    def body(zc_ref, zp_ref, zn_ref, dbc_ref, dbn_ref, d1c_ref, d1n_ref, wa_ref, wb_ref,
             dz_ref, dwa_ref, dwb_ref, ext_a0, ext_d1, ext_cb, ext_dc, da0_sc):
        i = pl.program_id(0)
        @pl.when(i == 0)
        def _():
            dwa_ref[...] = jnp.zeros_like(dwa_ref)
            dwb_ref[...] = jnp.zeros_like(dwb_ref)
        first = (i > 0).astype(F32)
        last = (i < nI - 1).astype(F32)
        zc = zc_ref[...].astype(F32)
        zp = zp_ref[...].astype(F32) * first
        a_val, a_gate = zc[:, 0:A], zc[:, A:2 * A]
        b_gate, c_gate, b_h = zc[:, 2 * A:3 * A], zc[:, 3 * A:4 * A], zc[:, 4 * A:5 * A]
        sig = _sigmoid(a_gate)
        ext_a0[pl.ds(0, H), :] = zp[:, 0:A] * _sigmoid(zp[:, A:2 * A])
        ext_a0[pl.ds(H, TM), :] = a_val * sig
        ext_d1[pl.ds(0, TM), :] = d1c_ref[...]
        ext_d1[pl.ds(TM, H), :] = d1n_ref[...] * last
        ext_cb[pl.ds(0, H), :] = zp[:, 3 * A:4 * A] * zp[:, 4 * A:5 * A]
        ext_cb[pl.ds(H, TM), :] = c_gate * b_h
        dbx = dbc_ref[...].astype(F32)
        dcbc = dbx * b_gate
        ext_dc[pl.ds(0, TM), :] = dcbc
        ext_dc[pl.ds(TM, H), :] = dbn_ref[...].astype(F32) * zn_ref[...].astype(F32) * last

        def chunk(c, carry):
            cs = pl.ds(pl.multiple_of(c * LANES, LANES), LANES)
            for r0 in range(0, TM, RB):
                acc = None
                for k in range(KA):
                    term = ext_d1[pl.ds(KA - 1 - k + r0, RB), cs] * wa_ref[pl.ds(k, 1), cs]
                    acc = term if acc is None else acc + term
                da0_sc[pl.ds(r0, RB), cs] = acc
            for k in range(KA):
                acc = None
                for r0 in range(0, TM, RB):
                    term = ext_a0[pl.ds(H - (KA - 1) + k + r0, RB), cs] * ext_d1[pl.ds(r0, RB), cs]
                    acc = term if acc is None else acc + term
                dwa_ref[pl.ds(k, 1), cs] += _rsum(acc)
            return carry
        lax.fori_loop(0, n_chunks, chunk, 0)

        da0 = da0_sc[...]
        dz_ref[:, 0:A] = (da0 * sig).astype(BF16)
        dz_ref[:, A:2 * A] = (da0 * a_val * sig * (1.0 - sig)).astype(BF16)
        cbc = _taps_causal(ext_cb, wb_ref, KB, H, TM, slice(None))
        dz_ref[:, 2 * A:3 * A] = (dbx * cbc).astype(BF16)
        dcb = _taps_anticausal(ext_dc, wb_ref, KB, TM, slice(None))
        dz_ref[:, 3 * A:4 * A] = (dcb * b_h).astype(BF16)
        dz_ref[:, 4 * A:5 * A] = (dcb * c_gate).astype(BF16)
        grads = _tap_grads(ext_cb, dcbc, KB, H, TM, slice(None))
        for k in range(KB):
            dwb_ref[pl.ds(k, 1), :] += grads[k]

    blocks = [((TM, 5 * A), BF16), ((H, 5 * A), BF16), ((H, A), BF16), ((TM, A), BF16), ((H, A), BF16),
              ((TM, A), F32), ((H, A), F32), ((KA, A), F32), ((KB, A), F32),
              ((TM, 5 * A), BF16), ((KA, A), F32), ((KB, A), F32)]
    scratch = [((H + TM, A), F32)] * 4 + [((TM, A), F32)]
    vec = lambda r: pl.BlockSpec((r, A), lambda i: (0, 0))
    outs, side_outs = _call(
        body, side, name="mixer_bwd_conv", grid=(nI,),
        in_specs=[pl.BlockSpec((TM, 5 * A), lambda i: (i, 0)),
                  pl.BlockSpec((H, 5 * A), _prev_rows(TM, H, 0)),
                  pl.BlockSpec((H, A), _next_rows(S, TM, H, 2)),
                  pl.BlockSpec((TM, A), lambda i: (i, 1)),
                  pl.BlockSpec((H, A), _next_rows(S, TM, H, 1)),
                  pl.BlockSpec((TM, A), lambda i: (i, 0)),
                  pl.BlockSpec((H, A), _next_rows(S, TM, H, 0)),
                  vec(KA), vec(KB)],
        out_specs=[pl.BlockSpec((TM, 5 * A), lambda i: (i, 0)), vec(KA), vec(KB)],
        out_shape=[jax.ShapeDtypeStruct((S, 5 * A), BF16), jax.ShapeDtypeStruct((KA, A), F32),
                   jax.ShapeDtypeStruct((KB, A), F32)],
        scratch=scratch,
        params=_params(("arbitrary",), blocks, scratch, temps=[((TM, 5 * A), F32)] * 2 + [((TM, A), F32)] * 14),
        args=[z, z, z, dcat, dcat, da1, da1, conv_a_w, conv_b_w])
    return outs if side is None else (outs, side_outs)


def _row_tile(R):
    return _pick(R, (256, 128, 64, 32, 16, 8))


def _scalars(*vals):
    return jnp.stack([jnp.asarray(v, jnp.int32) for v in vals])


def _cast_into_gathered(name, w, chip):
    R, C = w.shape
    TR = _row_tile(R)

    def body(s_ref, w_ref, o_ref):
        o_ref[...] = w_ref[...].astype(BF16)

    grid_spec = pltpu.PrefetchScalarGridSpec(
        num_scalar_prefetch=1, grid=(R // TR,),
        in_specs=[pl.BlockSpec((TR, C), lambda r, s: (r, 0))],
        out_specs=pl.BlockSpec((None, TR, C), lambda r, s: (s[0], r, 0)))
    return pl.pallas_call(body, name=name, grid_spec=grid_spec,
                          out_shape=jax.ShapeDtypeStruct((N_CHIPS, R, C), BF16),
                          compiler_params=_params(("arbitrary",), [((TR, C), F32), ((TR, C), BF16)]),
                          )(_scalars(chip), w)


def _add_pair(name, dw, recv, c):
    _, _, Rh, C = dw.shape
    TR = _row_tile(Rh)

    def body(c_ref, a_ref, b_ref, o_ref, ob_ref):
        s = a_ref[...] + b_ref[...]
        o_ref[...] = s
        ob_ref[...] = s.astype(BF16)

    out_spec = pl.BlockSpec((None, TR, C), lambda k, r, c_ref: (k, r, 0))
    grid_spec = pltpu.PrefetchScalarGridSpec(
        num_scalar_prefetch=1, grid=(N_CHIPS, Rh // TR),
        in_specs=[pl.BlockSpec((None, None, TR, C), lambda k, r, c_ref: (k, c_ref[0], r, 0)),
                  pl.BlockSpec((None, TR, C), lambda k, r, c_ref: (k, r, 0))],
        out_specs=[out_spec, out_spec])
    return pl.pallas_call(body, name=name, grid_spec=grid_spec,
                          out_shape=[jax.ShapeDtypeStruct((N_CHIPS, Rh, C), F32),
                                     jax.ShapeDtypeStruct((N_CHIPS, Rh, C), BF16)],
                          compiler_params=_params(("arbitrary", "arbitrary"), [((TR, C), F32)] * 4),
                          )(_scalars(c), dw, recv)


def _add_chips(name, parts, recv, chip, c):
    _, Rh, C = parts.shape
    TR = _row_tile(Rh)

    def body(s_ref, p_ref, r_ref, o_ref):
        o_ref[...] = ((p_ref[...] + r_ref[0].astype(F32)) + r_ref[1].astype(F32)) + r_ref[2].astype(F32)

    grid_spec = pltpu.PrefetchScalarGridSpec(
        num_scalar_prefetch=1, grid=(Rh // TR,),
        in_specs=[pl.BlockSpec((None, TR, C), lambda r, s: (s[0], r, 0)),
                  pl.BlockSpec((N_CHIPS - 1, TR, C), lambda r, s: (0, r, 0))],
        out_specs=pl.BlockSpec((None, TR, C), lambda r, s: (s[1], r, 0)))
    return pl.pallas_call(body, name=name, grid_spec=grid_spec,
                          out_shape=jax.ShapeDtypeStruct((2, Rh, C), F32),
                          compiler_params=_params(("arbitrary",), [((N_CHIPS + 1, TR, C), F32)]),
                          )(_scalars(chip, c), parts, recv)


def _sum_devices(name, parts):
    _, R, C = parts.shape

    def body(p_ref, o_ref):
        acc = p_ref[0]
        for d in range(1, N_DEV):
            acc = acc + p_ref[d]
        o_ref[...] = acc

    return pl.pallas_call(body, name=name, out_shape=jax.ShapeDtypeStruct((R, C), F32),
                          in_specs=[pl.BlockSpec(memory_space=pltpu.VMEM)],
                          out_specs=pl.BlockSpec(memory_space=pltpu.VMEM))(parts)


def _adamw(name, w, g, m, v):
    R, C = w.shape
    TR = _pick(R, (128, 64, 32, 16, 8))
    c1 = 1.0 - ADAM_B1 ** ADAM_STEP
    c2 = 1.0 - ADAM_B2 ** ADAM_STEP

    def body(w_ref, g_ref, m_ref, v_ref, d_ref, nm_ref, nv_ref):
        g_ = g_ref[...]
        nm = ADAM_B1 * m_ref[...] + (1.0 - ADAM_B1) * g_
        nv = ADAM_B2 * v_ref[...] + (1.0 - ADAM_B2) * (g_ * g_)
        d_ref[...] = -ADAM_LR * ((nm / c1) / (jnp.sqrt(nv / c2) + ADAM_EPS) + ADAM_WD * w_ref[...])
        nm_ref[...] = nm
        nv_ref[...] = nv

    spec = pl.BlockSpec((TR, C), lambda r: (r, 0))
    shp = jax.ShapeDtypeStruct((R, C), F32)
    return pl.pallas_call(body, name=name, grid=(R // TR,), in_specs=[spec] * 4, out_specs=[spec] * 3,
                          out_shape=[shp] * 3,
                          compiler_params=_params(("arbitrary",), [((TR, C), F32)] * 7))(w, g, m, v)


def _place():
    x, y, c = lax.axis_index("x"), lax.axis_index("y"), lax.axis_index("c")
    others = [(1 - x, y), (x, 1 - y), (1 - x, 1 - y)]
    return x, y, c, others


def _allgather_small(name, block):
    R, C = block.shape

    def body(x_ref, out_ref, send_sems, recv_sems, local_sem):
        x, y, c, chips = _place()
        me, sibling = (x, y, c), (x, y, 1 - c)

        def rows(px, py, pc):
            return out_ref.at[4 * px + 2 * py + pc]

        def copy(k, blk, to, src=None):
            return pltpu.make_async_remote_copy(
                src_ref=rows(*blk) if src is None else src, dst_ref=rows(*blk),
                send_sem=send_sems.at[k], recv_sem=recv_sems.at[k], device_id=to, device_id_type=MESH)

        mine = pltpu.make_async_copy(x_ref, rows(*me), local_sem)
        mine.start()
        first = [copy(0, me, sibling, src=x_ref)]
        first += [copy(1 + j, me, (*chip, c), src=x_ref) for j, chip in enumerate(chips)]
        for cp in first:
            cp.start()
        passed = [copy(4 + j, (*chip, c), sibling) for j, chip in enumerate(chips)]
        for j, chip in enumerate(chips):
            copy(1 + j, (*chip, c), me).wait_recv()
            passed[j].start()
        copy(0, sibling, me).wait_recv()
        for j, chip in enumerate(chips):
            copy(4 + j, (*chip, 1 - c), me).wait_recv()
        for cp in first + passed:
            cp.wait_send()
        mine.wait()

    return pl.pallas_call(
        body, name=name, out_shape=jax.ShapeDtypeStruct((N_DEV, R, C), F32),
        in_specs=[pl.BlockSpec(memory_space=pltpu.VMEM)], out_specs=pl.BlockSpec(memory_space=pltpu.VMEM),
        scratch_shapes=[pltpu.SemaphoreType.DMA((7,)), pltpu.SemaphoreType.DMA((7,)), pltpu.SemaphoreType.DMA],
    )(block)


def _gather_side(bufs, across, within):
    def rows(ref, chip, half, piece):
        _, r0, n = piece
        return ref.at[2 * chip[0] + chip[1], pl.ds(half * (ref.shape[1] // 2) + r0, n)]

    def copies(ins, outs, send_sems, recv_sems, base):
        x, y, c, chips = _place()
        sibling = (x, y, 1 - c)
        pairs = []

        def add(k, src, dst, to, arrival):
            mk = lambda s, d, dev: pltpu.make_async_remote_copy(
                src_ref=s, dst_ref=d, send_sem=send_sems.at[base + k], recv_sem=recv_sems.at[base + k],
                device_id=dev, device_id_type=MESH)
            pairs.append((mk(src, dst, to), mk(arrival, arrival, (x, y, c))))

        for p, piece in enumerate(across):
            ref = outs[piece[0]]
            for j, chip in enumerate(chips):
                mine = rows(ref, (x, y), c, piece)
                add(3 * p + j, mine, mine, (*chip, c), rows(ref, chip, c, piece))
        for q, piece in enumerate(within):
            ref = outs[piece[0]]
            for j, chip in enumerate(chips):
                held = rows(ref, chip, c, piece)
                add(3 * (len(across) + q) + j, held, held, sibling, rows(ref, chip, 1 - c, piece))
        return pairs

    def start(*refs):
        for send, _ in copies(*refs):
            send.start()

    def wait(*refs):
        pairs = copies(*refs)
        for _, arrival in pairs:
            arrival.wait_recv()
        for send, _ in pairs:
            send.wait_send()

    return _Side(list(bufs), [jax.ShapeDtypeStruct(b.shape, b.dtype) for b in bufs],
                 3 * (len(across) + len(within)), start, wait, aliased=True)


def _chip_exchange(parts):
    n = len(parts)

    def copies(ins, outs, send_sems, recv_sems, base):
        x, y, c, chips = _place()
        return [pltpu.make_async_remote_copy(
            src_ref=ins[a].at[2 * chip[0] + chip[1]], dst_ref=outs[a].at[j],
            send_sem=send_sems.at[base + 3 * a + j], recv_sem=recv_sems.at[base + 3 * a + j],
            device_id=(*chip, c), device_id_type=MESH) for a in range(n) for j, chip in enumerate(chips)]

    return _Side(list(parts), [jax.ShapeDtypeStruct((N_CHIPS - 1,) + p.shape[1:], p.dtype) for p in parts],
                 3 * n, *_start_wait(copies))


def _pair_exchange(grads):
    def copies(ins, outs, send_sems, recv_sems, base):
        x, y, c, _ = _place()
        return [pltpu.make_async_remote_copy(
            src_ref=ins[a].at[:, 1 - c], dst_ref=outs[a], send_sem=send_sems.at[base + a],
            recv_sem=recv_sems.at[base + a], device_id=(x, y, 1 - c), device_id_type=MESH)
            for a in range(len(grads))]

    return _Side(list(grads), [jax.ShapeDtypeStruct((N_CHIPS,) + g.shape[2:], F32) for g in grads],
                 len(grads), *_start_wait(copies))


def _start_wait(copies):
    def start(*refs):
        for cp in copies(*refs):
            cp.start()

    def wait(*refs):
        cps = copies(*refs)
        for cp in cps:
            cp.wait_recv()
        for cp in cps:
            cp.wait_send()
    return start, wait


def _both(first, second):
    n_in, n_out = len(first.ins), len(first.out_shapes)

    def run(which):
        def go(ins, outs, send_sems, recv_sems, base):
            getattr(first, which)(ins[:n_in], outs[:n_out], send_sems, recv_sems, base)
            getattr(second, which)(ins[n_in:], outs[n_out:], send_sems, recv_sems, base + first.n_sems)
        return go

    return _Side(first.ins + second.ins, first.out_shapes + second.out_shapes,
                 first.n_sems + second.n_sems, run("start"), run("wait"))


def _share_halves(halves):
    n = len(halves)

    def body(*refs):
        ins, outs = refs[:n], refs[n:2 * n]
        send_sems, recv_sems = refs[2 * n:]
        x, y, c, _ = _place()
        sends = []
        for a in range(n):
            sends.append(pltpu.make_async_remote_copy(
                src_ref=ins[a].at[c], dst_ref=outs[a].at[c], send_sem=send_sems.at[a], recv_sem=recv_sems.at[a],
                device_id=(x, y, 1 - c), device_id_type=MESH))
            sends[a].start()
        for a in range(n):
            theirs = outs[a].at[1 - c]
            pltpu.make_async_remote_copy(
                src_ref=theirs, dst_ref=theirs, send_sem=send_sems.at[a], recv_sem=recv_sems.at[a],
                device_id=(x, y, c), device_id_type=MESH).wait_recv()
        for cp in sends:
            cp.wait_send()

    return pl.pallas_call(
        body, name="grads_share_halves",
        out_shape=[jax.ShapeDtypeStruct(h.shape, F32) for h in halves],
        in_specs=[ANY] * n, out_specs=[ANY] * n, input_output_aliases={a: a for a in range(n)},
        scratch_shapes=[pltpu.SemaphoreType.DMA((n,)), pltpu.SemaphoreType.DMA((n,))],
    )(*halves)


def _pack(arrays):
    pieces = []
    for a in arrays:
        flat = a.reshape(-1).astype(F32)
        pieces.append(jnp.pad(flat, (0, (-flat.size) % PACK_ALIGN)))
    return jnp.concatenate(pieces).reshape(-1, LANES)


def _unpack(buf, shapes):
    lead = buf.shape[:-2]
    flat = buf.reshape(lead + (-1,))
    out, off = [], 0
    for shp in shapes:
        size = 1
        for s in shp:
            size *= s
        out.append(flat[..., off:off + size].reshape(lead + tuple(shp)))
        off += size + (-size) % PACK_ALIGN
    return out


def _gather_channels(buf, shapes):
    per_chip = _unpack(buf[0::2], shapes)
    return [jnp.transpose(a, (1, 0, 2)).reshape(a.shape[1], -1) for a in per_chip]


def _mm_tile(n, rows, limit_bytes=6 * 1024 * 1024):
    for t in (1408, 1280, 1024, 640, 512, 384, 256, 128):
        if n % t == 0 and rows * t * 2 <= limit_bytes:
            return t
    raise ValueError(f"no column tile for {n} x {rows}")


def kernel(x, p, norm_mix_g, w_in, conv_a_w, conv_a_b, ln_a_g, ln_a_b, conv_b_w, w_out, norm_ffn_g, w_up, conv_ffn_w, w_down, w_ple_gate, b_ple_gate, w_ple_proj, norm_final_g, loss_target, m_norm_mix_g, m_w_in, m_conv_a_w, m_conv_a_b, m_ln_a_g, m_ln_a_b, m_conv_b_w, m_w_out, m_norm_ffn_g, m_w_up, m_conv_ffn_w, m_w_down, m_w_ple_gate, m_b_ple_gate, m_w_ple_proj, m_norm_final_g, v_norm_mix_g, v_w_in, v_conv_a_w, v_conv_a_b, v_ln_a_g, v_ln_a_b, v_conv_b_w, v_w_out, v_norm_ffn_g, v_w_up, v_conv_ffn_w, v_w_down, v_w_ple_gate, v_b_ple_gate, v_w_ple_proj, v_norm_final_g):
    S, D = x.shape[1], x.shape[2]
    P = p.shape[3]
    A = conv_a_b.shape[1]
    F = w_down.shape[1] * N_CHIPS
    KA, KB, KF = conv_a_w.shape[1], conv_b_w.shape[1], conv_ffn_w.shape[1]
    xi, yi, ci = lax.axis_index("x"), lax.axis_index("y"), lax.axis_index("c")
    chip = 2 * xi + yi

    TM = _pick(S, (512, 256, 128))
    TE = _pick(S, (256, 128))
    TC = _pick(2 * F // N_CHIPS, (1408, 1024, 512, 256, 128))
    ffn_place = _pair_tile(F // TC)

    x2, p2, t2 = x.reshape(S, D), p.reshape(S, P), loss_target.reshape(S, D)
    gfin = norm_final_g.reshape(1, D)

    big = dict(w_in=w_in[0], w_out=w_out[0], w_up=w_up[0], w_down=w_down[0],
               w_ple_gate=w_ple_gate[0], w_ple_proj=w_ple_proj[0])
    names = list(big)
    buf = {n: _cast_into_gathered("cast_" + n, big[n], chip) for n in names}
    half = {n: big[n].shape[0] // 2 for n in names}
    up_a = half["w_up"] // 2
    (w_in3,) = _comm_only("gather_w_in_across", _gather_side([buf["w_in"]], [(0, 0, half["w_in"])], []))
    (w_in3,) = _comm_only("gather_w_in_within", _gather_side([w_in3], [], [(0, 0, half["w_in"])]))

    tap_shapes = [(KA, A // N_CHIPS), (KB, A // N_CHIPS), (KF, 2 * F // N_CHIPS)]
    taps = _allgather_small("allgather_taps", _pack([conv_a_w[0], conv_b_w[0], conv_ffn_w[0]]))
    conv_a_f, conv_b_f, conv_ffn_f = _gather_channels(taps, tap_shapes)

    def rms_prologue(rows, row_r, vec_r, ro_r, ao_r):
        h = row_r[0][rows, :]
        hn = (h * _rms_stats(h) * vec_r[0][...]).astype(BF16)
        ro_r[0][rows, :] = hn
        return [hn]

    def cast_prologue(rows, row_r, vec_r, ro_r, ao_r):
        hb = row_r[0][rows, :].astype(BF16)
        ro_r[0][rows, :] = hb
        return [hb]

    plain = lambda accs, tile_r, cv_r: [accs[0]]
    residual = lambda accs, tile_r, cv_r: [tile_r[0][...] + accs[0]]

    (z, hn1), (w_out_t, w_up_t) = _rows_mm(
        "in_proj", S, TM, 5 * A, _mm_tile(5 * A // N_CHIPS, D), row_ins=[x2], vec_ins=[norm_mix_g],
        weights=[(w_in3, "nn3")], tile_outs=[BF16], row_outs=[(D, BF16)], prologue=rms_prologue, epilogue=plain,
        side=_gather_side([buf["w_out"], buf["w_up"]], [(0, 0, half["w_out"]), (1, 0, up_a)], []))
    (a1, cat), (w_out3, w_up_t) = _mixer_fwd(
        z, conv_a_f, conv_a_b, ln_a_g, ln_a_b, conv_b_f, S, TE, A,
        side=_gather_side([w_out_t, w_up_t], [(1, up_a, half["w_up"] - up_a)],
                          [(0, 0, half["w_out"]), (1, 0, up_a)]))
    w_out_f = w_out3.reshape(2 * A, D)
    (h1,), (w_up3, w_proj_t) = _rows_mm(
        "out_proj", S, TM, D, _mm_tile(D, 2 * A), row_ins=[cat], weights=[(w_out_f, "nn2")],
        tile_ins=[x2], tile_outs=[F32], epilogue=residual,
        side=_gather_side([w_up_t, buf["w_ple_proj"]], [(1, 0, half["w_ple_proj"])],
                          [(0, up_a, half["w_up"] - up_a)]))
    (u0, hn2), (w_down_t, w_gate_t, w_proj3) = _rows_mm(
        "up_proj", S, TM, 2 * F, TC, row_ins=[h1], vec_ins=[norm_ffn_g],
        weights=[(w_up3, "nn3")], tile_outs=[BF16], row_outs=[(D, BF16)],
        prologue=rms_prologue, epilogue=plain, place=ffn_place,
        side=_gather_side([buf["w_down"], buf["w_ple_gate"], w_proj_t],
                          [(0, 0, half["w_down"]), (1, 0, half["w_ple_gate"])], [(2, 0, half["w_ple_proj"])]))
    (act,), (w_down3, w_gate3) = _ffn_act(
        u0, conv_ffn_f, S, TM, F, TC,
        side=_gather_side([w_down_t, w_gate_t], [], [(0, 0, half["w_down"]), (1, 0, half["w_ple_gate"])]))
    w_down_f = w_down3.reshape(F, D)
    w_gate_f = w_gate3.reshape(D, D)
    (h2,) = _rows_mm("down_proj", S, TM, D, _mm_tile(D, F), row_ins=[act], weights=[(w_down_f, "nn2")],
                     tile_ins=[h1], tile_outs=[F32], epilogue=residual)

    def ple_prologue(rows, row_r, vec_r, ro_r, ao_r):
        hb = row_r[0][rows, :].astype(BF16)
        pb = row_r[1][rows, :].astype(BF16)
        ro_r[0][rows, :] = hb
        ro_r[1][rows, :] = pb
        return [hb, pb]

    def ple_epilogue(accs, tile_r, cv_r):
        gate = _sigmoid(accs[0] + cv_r[0][...])
        return [tile_r[0][...] + accs[1] * gate, gate, accs[1]]

    h3, gate, pp, h2b, pb = _rows_mm(
        "ple_fwd", S, TM, D, _mm_tile(D // N_CHIPS, D), row_ins=[h2, p2], colvec_ins=[b_ple_gate],
        weights=[(w_gate_f, "nn2"), (w_proj3, "nn3")], tile_ins=[h2], tile_outs=[F32, BF16, BF16],
        row_outs=[(D, BF16), (P, BF16)], prologue=ple_prologue, epilogue=ple_epilogue)
    loss_part, g_norm_final, dh3, dpre, dpp, g_b_gate = _loss_head(h3, t2, gfin, pp, gate, S, TE, D)

    TK = _pick(S, (1024, 512, 256, 128))
    wt = lambda n: _pick(n, (1408, 1280, 1024, 512, 256, 128))
    chip_sums, from_chips = {}, {}

    def to_sibling(parts):
        ns = list(parts)
        halves = [parts[n].reshape(N_CHIPS, 2, big[n].shape[0] // 2, big[n].shape[1]) for n in ns]
        return ns, halves, _pair_exchange(halves)

    def to_chips(ns, halves, from_sibling):
        sums = [_add_pair("pair_sum_" + n, h, r, ci) for n, h, r in zip(ns, halves, from_sibling)]
        for n, (s, _) in zip(ns, sums):
            chip_sums[n] = s
        return ns, _chip_exchange([b for _, b in sums])

    def landed(ns, side_outs):
        for n, r in zip(ns, side_outs):
            from_chips[n] = r

    ns, halves, side = to_sibling(dict(
        w_ple_gate=_tn_mm("dw_ple_gate", h2b, dpre, wt(D), wt(D), TK),
        w_ple_proj=_tn_mm("dw_ple_proj", pb, dpp, wt(P), wt(D // N_CHIPS), TK, cols_per_chip=D // N_CHIPS)))
    (dh2,), got = _rows_mm("ple_bwd", S, TM, D, _mm_tile(D, D), row_ins=[dpre], weights=[(w_gate_f, "nt2")],
                           tile_ins=[dh3], tile_outs=[F32], epilogue=residual, side=side)
    ple_ns, ple_chips = to_chips(ns, halves, got)
    (dact, dh2b), got = _rows_mm("down_bwd", S, TM, F, _mm_tile(F, D), row_ins=[dh2], weights=[(w_down_f, "nt2")],
                                 tile_outs=[BF16], row_outs=[(D, BF16)], prologue=cast_prologue, epilogue=plain,
                                 side=ple_chips)
    landed(ple_ns, got)
    ns, halves, side = to_sibling(dict(w_down=_tn_mm("dw_down", act, dh2b, wt(F), wt(D), TK)))
    (du0, g_conv_gate, g_conv_up), got = _ffn_bwd(u0, dact, conv_ffn_f, S, TM, F, TC, side=side)
    down_ns, down_chips = to_chips(ns, halves, got)
    g_conv_ffn = jnp.concatenate([g_conv_gate, g_conv_up], axis=1)
    ns, halves, side = to_sibling(dict(
        w_up=_tn_mm("dw_up", hn2, du0, wt(D), TC, TK, cols_per_chip=2 * F // N_CHIPS, place=ffn_place)))

    def up_bwd_epilogue(acc, rows, row_r, vec_r, ro_r, ao_r):
        dh, dg = _rms_bwd(row_r[0][rows, :], vec_r[0][...], acc)
        dh1_ = row_r[1][rows, :] + dh
        ro_r[0][rows, :] = dh1_
        ro_r[1][rows, :] = dh1_.astype(BF16)
        ao_r[0][...] += dg

    (dh1, dh1b, g_norm_ffn), got = _kloop_mm(
        "up_bwd", S, TM, du0, w_up3, TC, row_ins=[h1, dh2], vec_ins=[norm_ffn_g],
        row_outs=[(D, F32), (D, BF16)], acc_outs=[(1, D)], epilogue=up_bwd_epilogue, place=ffn_place,
        side=_both(down_chips, side))
    landed(down_ns, got[:len(down_ns)])
    up_ns, up_chips = to_chips(ns, halves, got[len(down_ns):])
    ns, halves, side = to_sibling(dict(w_out=_tn_mm("dw_out", cat, dh1b, wt(2 * A), wt(D), TK)))
    (dcat,), got = _rows_mm("out_bwd", S, TM, 2 * A, _mm_tile(2 * A, D), row_ins=[dh1b],
                            weights=[(w_out_f, "nt2")], tile_outs=[BF16], epilogue=plain, side=side)
    out_ns, out_chips = to_chips(ns, halves, got)
    da1, ln_sums = _mixer_bwd_ln(dcat, a1, ln_a_g, ln_a_b, S, TE, A)
    (dz, g_conv_a, g_conv_b), got = _mixer_bwd_conv(z, dcat, da1, conv_a_f, conv_b_f, S, TE, A,
                                                    side=_both(up_chips, out_chips))
    landed(up_ns + out_ns, got)
    ns, halves, side = to_sibling(dict(
        w_in=_tn_mm("dw_in", hn1, dz, wt(D), wt(5 * A // N_CHIPS), TK, cols_per_chip=5 * A // N_CHIPS)))
    ns, side = to_chips(ns, halves, _comm_only("grads_exchange_pairs_in", side))

    def in_bwd_epilogue(acc, rows, row_r, vec_r, ro_r, ao_r):
        dh, dg = _rms_bwd(row_r[0][rows, :], vec_r[0][...], acc)
        ro_r[0][rows, :] = row_r[1][rows, :] + dh
        ao_r[0][...] += dg

    (dx, g_norm_mix), got = _kloop_mm(
        "in_bwd", S, TM, dz, w_in3, _mm_tile(5 * A // N_CHIPS, D), row_ins=[x2, dh1],
        vec_ins=[norm_mix_g], row_outs=[(D, F32)], acc_outs=[(1, D)], epilogue=in_bwd_epilogue, side=side)
    landed(ns, got)

    reduced = _share_halves([_add_chips("chip_sum_" + n, chip_sums[n], from_chips[n], chip, ci) for n in names])
    moments = dict(w_in=(m_w_in, v_w_in), w_out=(m_w_out, v_w_out), w_up=(m_w_up, v_w_up),
                   w_down=(m_w_down, v_w_down), w_ple_gate=(m_w_ple_gate, v_w_ple_gate),
                   w_ple_proj=(m_w_ple_proj, v_w_ple_proj))
    grads, deltas, new_m, new_v = {}, {}, {}, {}
    for n, g in zip(names, reduced):
        g = g.reshape(big[n].shape)
        d_, m_, v_ = _adamw("adamw_" + n, big[n], g, moments[n][0][0], moments[n][1][0])
        grads[n], deltas[n], new_m[n], new_v[n] = g[None], d_[None], m_[None], v_[None]

    small = ["norm_mix_g", "conv_a_w", "conv_a_b", "ln_a_g", "ln_a_b", "conv_b_w", "norm_ffn_g",
             "conv_ffn_w", "b_ple_gate", "norm_final_g"]
    small_part = [g_norm_mix, g_conv_a, ln_sums[2:3], ln_sums[0:1], ln_sums[1:2], g_conv_b, g_norm_ffn,
                  g_conv_ffn, g_b_gate, g_norm_final]
    full_shapes = [a.shape for a in small_part]
    summed = _sum_devices("small_grads_sum", _allgather_small("allgather_small_grads", _pack(small_part)))
    small_g = dict(zip(small, _unpack(summed, full_shapes)))
    for n, width in (("conv_a_w", A), ("conv_b_w", A), ("conv_ffn_w", 2 * F)):
        small_g[n] = lax.dynamic_slice_in_dim(small_g[n], chip * (width // N_CHIPS), width // N_CHIPS, axis=1)
    small_w = dict(norm_mix_g=(norm_mix_g, m_norm_mix_g, v_norm_mix_g), conv_a_w=(conv_a_w, m_conv_a_w, v_conv_a_w),
                   conv_a_b=(conv_a_b, m_conv_a_b, v_conv_a_b), ln_a_g=(ln_a_g, m_ln_a_g, v_ln_a_g),
                   ln_a_b=(ln_a_b, m_ln_a_b, v_ln_a_b), conv_b_w=(conv_b_w, m_conv_b_w, v_conv_b_w),
                   norm_ffn_g=(norm_ffn_g, m_norm_ffn_g, v_norm_ffn_g),
                   conv_ffn_w=(conv_ffn_w, m_conv_ffn_w, v_conv_ffn_w),
                   b_ple_gate=(b_ple_gate, m_b_ple_gate, v_b_ple_gate),
                   norm_final_g=(norm_final_g, m_norm_final_g, v_norm_final_g))
    out_shapes = [small_w[n][0].shape for n in small]
    packed_g = _pack([small_g[n] for n in small])
    packed = [_pack([small_w[n][k] for n in small]) for k in range(3)]
    d_s, m_s, v_s = _adamw("adamw_small", packed[0], packed_g, packed[1], packed[2])
    for n, g, d_, m_, v_ in zip(small, _unpack(packed_g, out_shapes), _unpack(d_s, out_shapes),
                                _unpack(m_s, out_shapes), _unpack(v_s, out_shapes)):
        grads[n], deltas[n], new_m[n], new_v[n] = g, d_, m_, v_

    order = ["norm_mix_g", "w_in", "conv_a_w", "conv_a_b", "ln_a_g", "ln_a_b", "conv_b_w", "w_out", "norm_ffn_g",
             "w_up", "conv_ffn_w", "w_down", "w_ple_gate", "b_ple_gate", "w_ple_proj", "norm_final_g"]
    loss = lax.psum(loss_part[0, 0], ("x", "y", "c"))
    return (loss, dx.reshape(x.shape), *[grads[n] for n in order], *[deltas[n] for n in order],
            *[new_m[n] for n in order], *[new_v[n] for n in order])
```

```python
from typing import Callable, NamedTuple

import jax
import jax.numpy as jnp
from jax import lax
from jax.experimental import pallas as pl
from jax.experimental.pallas import tpu as pltpu

F32 = jnp.float32
BF16 = jnp.bfloat16
MESH = pl.DeviceIdType.MESH
ANY = pl.BlockSpec(memory_space=pl.ANY)

EPS = 1e-6
ADAM_LR = 0.001
ADAM_B1 = 0.9
ADAM_B2 = 0.999
ADAM_EPS = 1e-08
ADAM_WD = 0.01
ADAM_STEP = 10

N_CHIPS = 4
N_DEV = 8
LANES = 128
SUBLANES = 8
PACK_ALIGN = LANES * SUBLANES
ROW_CHUNK = 32
VMEM_CAP = 60 * 1024 * 1024
VMEM_SLACK = 6 * 1024 * 1024


def _pick(n, cands):
    for c in cands:
        if n % c == 0:
            return c
    raise ValueError(f"no tile of {cands} divides {n}")


def _nbytes(shape, dtype):
    n = 1
    for s in shape:
        if s is not None:
            n *= s
    return n * jnp.dtype(dtype).itemsize


def _params(sem, blocks, scratch=(), temps=()):
    est = (2 * sum(_nbytes(s, d) for s, d in blocks) + sum(_nbytes(s, d) for s, d in scratch)
           + sum(_nbytes(s, d) for s, d in temps))
    return pltpu.CompilerParams(dimension_semantics=sem,
                                vmem_limit_bytes=min(est + VMEM_SLACK, VMEM_CAP))


def _sigmoid(x):
    return 1.0 / (1.0 + jnp.exp(-x))


def _rsum(x):
    return jnp.sum(x, axis=0, keepdims=True)


class _Side(NamedTuple):
    ins: list
    out_shapes: list
    n_sems: int
    start: Callable
    wait: Callable
    aliased: bool = False


def _call(body, side, *, name, grid, in_specs, out_specs, out_shape, scratch, params, args):
    vmem = [pltpu.VMEM(s, d) for s, d in scratch]
    if side is None:
        outs = pl.pallas_call(body, name=name, grid=grid, in_specs=in_specs, out_specs=out_specs,
                              out_shape=out_shape, scratch_shapes=vmem, compiler_params=params)(*args)
        return list(outs), []
    n_in, n_out, n_sc = len(in_specs), len(out_specs), len(scratch)
    ns_in, ns_out = len(side.ins), len(side.out_shapes)

    def carrier(*refs):
        pos = [0]
        def take(n):
            pos[0] += n
            return refs[pos[0] - n:pos[0]]
        ins, s_ins, outs, s_outs, scr = take(n_in), take(ns_in), take(n_out), take(ns_out), take(n_sc)
        send_sems, recv_sems = take(2)
        first = last = None
        for axis, extent in enumerate(grid):
            at_start, at_end = pl.program_id(axis) == 0, pl.program_id(axis) == extent - 1
            first = at_start if first is None else first & at_start
            last = at_end if last is None else last & at_end

        @pl.when(first)
        def _():
            side.start(s_ins, s_outs, send_sems, recv_sems, 0)
        body(*ins, *outs, *scr)

        @pl.when(last)
        def _():
            side.wait(s_ins, s_outs, send_sems, recv_sems, 0)

    outs = pl.pallas_call(
        carrier, name=name, grid=grid, in_specs=list(in_specs) + [ANY] * ns_in,
        out_specs=list(out_specs) + [ANY] * ns_out, out_shape=list(out_shape) + list(side.out_shapes),
        scratch_shapes=vmem + [pltpu.SemaphoreType.DMA((side.n_sems,)), pltpu.SemaphoreType.DMA((side.n_sems,))],
        input_output_aliases={n_in + i: n_out + i for i in range(ns_in)} if side.aliased else {},
        compiler_params=params)(*args, *side.ins)
    return list(outs[:n_out]), list(outs[n_out:])


def _comm_only(name, side):
    n_in = len(side.ins)

    def body(*refs):
        ins, outs = refs[:n_in], refs[n_in:n_in + len(side.out_shapes)]
        send_sems, recv_sems = refs[n_in + len(side.out_shapes):]
        side.start(ins, outs, send_sems, recv_sems, 0)
        side.wait(ins, outs, send_sems, recv_sems, 0)

    return pl.pallas_call(
        body, name=name, out_shape=list(side.out_shapes), in_specs=[ANY] * n_in,
        out_specs=[ANY] * len(side.out_shapes),
        scratch_shapes=[pltpu.SemaphoreType.DMA((side.n_sems,)), pltpu.SemaphoreType.DMA((side.n_sems,))],
        input_output_aliases={i: i for i in range(n_in)} if side.aliased else {},
    )(*side.ins)


def _rms_stats(x):
    return lax.rsqrt(jnp.mean(x * x, axis=-1, keepdims=True) + EPS)


def _rms_bwd(h, g, dout):
    r = _rms_stats(h)
    n = h * r
    dn = dout * g
    dh = r * (dn - n * jnp.mean(dn * n, axis=-1, keepdims=True))
    return dh, _rsum(dout * n)


def _identity(t):
    return t


def _chip_major(nb, place=_identity):
    return lambda i, j: (place(j) // nb, 0, place(j) % nb)


def _rows_mm(name, S, TM, N, TN, *, row_ins, vec_ins=(), colvec_ins=(), weights, tile_ins=(),
             tile_outs, row_outs=(), acc_outs=(), prologue=None, epilogue, place=_identity, side=None):
    nI, nJ = S // TM, N // TN
    n_row, n_vec, n_cv, n_w, n_tile = len(row_ins), len(vec_ins), len(colvec_ins), len(weights), len(tile_ins)
    n_to, n_ro, n_ao = len(tile_outs), len(row_outs), len(acc_outs)

    in_specs, blocks, scratch, ks = [], [], [], []
    for a in row_ins:
        in_specs.append(pl.BlockSpec((TM, a.shape[1]), lambda i, j: (i, 0)))
        blocks.append(((TM, a.shape[1]), a.dtype))
    for a in vec_ins:
        in_specs.append(pl.BlockSpec(a.shape, lambda i, j: (0, 0)))
        blocks.append((a.shape, a.dtype))
    for a in colvec_ins:
        in_specs.append(pl.BlockSpec((1, TN), lambda i, j: (0, j)))
        blocks.append(((1, TN), a.dtype))
    for w, mode in weights:
        if mode == "nn2":
            k = w.shape[0]
            in_specs.append(pl.BlockSpec((k, TN), lambda i, j: (0, j)))
        elif mode == "nn3":
            k = w.shape[1]
            in_specs.append(pl.BlockSpec((None, k, TN), _chip_major(w.shape[2] // TN, place)))
        else:
            k = w.shape[1]
            in_specs.append(pl.BlockSpec((TN, k), lambda i, j: (j, 0)))
        ks.append(k)
        blocks.append(((k, TN), BF16))
        if prologue is not None:
            scratch.append(((TM, k), BF16))
    for a in tile_ins:
        in_specs.append(pl.BlockSpec((TM, TN), lambda i, j: (i, j)))
        blocks.append(((TM, TN), a.dtype))

    out_shape, out_specs = [], []
    for dt in tile_outs:
        out_shape.append(jax.ShapeDtypeStruct((S, N), dt))
        out_specs.append(pl.BlockSpec((TM, TN), lambda i, j: (i, j)))
        blocks.append(((TM, TN), dt))
    for width, dt in row_outs:
        out_shape.append(jax.ShapeDtypeStruct((S, width), dt))
        out_specs.append(pl.BlockSpec((TM, width), lambda i, j: (i, 0)))
        blocks.append(((TM, width), dt))
    for rows, width in acc_outs:
        out_shape.append(jax.ShapeDtypeStruct((rows, width), F32))
        out_specs.append(pl.BlockSpec((rows, width), lambda i, j: (0, 0)))
        blocks.append(((rows, width), F32))

    modes = [m for _, m in weights]

    def body(*refs):
        pos = 0
        def take(n):
            nonlocal pos
            out = refs[pos:pos + n]
            pos += n
            return out
        row_r, vec_r, cv_r, w_r, tile_r = take(n_row), take(n_vec), take(n_cv), take(n_w), take(n_tile)
        to_r, ro_r, ao_r, a_sc = take(n_to), take(n_ro), take(n_ao), take(len(scratch))
        i, j = pl.program_id(0), pl.program_id(1)

        if prologue is None:
            a_sc = row_r[:n_w]
        else:
            @pl.when(j == 0)
            def _():
                if n_ao:
                    @pl.when(i == 0)
                    def _():
                        for r in ao_r:
                            r[...] = jnp.zeros_like(r)

                def chunk(ci, carry):
                    rows = pl.ds(pl.multiple_of(ci * ROW_CHUNK, ROW_CHUNK), ROW_CHUNK)
                    for sc, a in zip(a_sc, prologue(rows, row_r, vec_r, ro_r, ao_r)):
                        sc[rows, :] = a
                    return carry
                lax.fori_loop(0, TM // ROW_CHUNK, chunk, 0)

        accs = []
        for w_ref, sc, mode in zip(w_r, a_sc, modes):
            if mode == "nt2":
                accs.append(lax.dot_general(sc[...], w_ref[...], (((1,), (1,)), ((), ())),
                                            preferred_element_type=F32))
            else:
                accs.append(jnp.dot(sc[...], w_ref[...], preferred_element_type=F32))
        outs = epilogue(accs, tile_r, cv_r)
        for r, o in zip(to_r, outs):
            r[...] = o.astype(r.dtype)

    outs, side_outs = _call(
        body, side, name=name, grid=(nI, nJ), in_specs=in_specs, out_specs=out_specs, out_shape=out_shape,
        scratch=scratch, params=_params(("arbitrary", "arbitrary"), blocks, scratch, temps=[((TM, TN), F32)] * 3),
        args=[*row_ins, *vec_ins, *colvec_ins, *[w for w, _ in weights], *tile_ins])
    return outs if side is None else (outs, side_outs)


def _kloop_mm(name, S, TM, a, w3, TK, *, row_ins, vec_ins, row_outs, acc_outs, epilogue, place=_identity,
              side=None):
    _, N, Ks = w3.shape
    nb = Ks // TK
    nK = N_CHIPS * nb
    n_row, n_vec, n_ro, n_ao = len(row_ins), len(vec_ins), len(row_outs), len(acc_outs)

    in_specs = [pl.BlockSpec((TM, TK), lambda i, k: (i, k)),
                pl.BlockSpec((None, N, TK), _chip_major(nb, place))]
    blocks = [((TM, TK), BF16), ((N, TK), BF16)]
    for r in row_ins:
        in_specs.append(pl.BlockSpec((TM, r.shape[1]), lambda i, k: (i, 0)))
        blocks.append(((TM, r.shape[1]), r.dtype))
    for v in vec_ins:
        in_specs.append(pl.BlockSpec(v.shape, lambda i, k: (0, 0)))
        blocks.append((v.shape, v.dtype))
    out_shape, out_specs = [], []
    for width, dt in row_outs:
        out_shape.append(jax.ShapeDtypeStruct((S, width), dt))
        out_specs.append(pl.BlockSpec((TM, width), lambda i, k: (i, 0)))
        blocks.append(((TM, width), dt))
    for rows, width in acc_outs:
        out_shape.append(jax.ShapeDtypeStruct((rows, width), F32))
        out_specs.append(pl.BlockSpec((rows, width), lambda i, k: (0, 0)))
        blocks.append(((rows, width), F32))
    scratch = [((TM, N), F32)]

    def body(*refs):
        a_ref, w_ref = refs[0], refs[1]
        row_r = refs[2:2 + n_row]
        vec_r = refs[2 + n_row:2 + n_row + n_vec]
        pos = 2 + n_row + n_vec
        ro_r = refs[pos:pos + n_ro]
        ao_r = refs[pos + n_ro:pos + n_ro + n_ao]
        acc_sc = refs[pos + n_ro + n_ao]
        i, k = pl.program_id(0), pl.program_id(1)
        @pl.when(k == 0)
        def _():
            acc_sc[...] = jnp.zeros_like(acc_sc)
        acc_sc[...] += lax.dot_general(a_ref[...], w_ref[...], (((1,), (1,)), ((), ())),
                                       preferred_element_type=F32)

        @pl.when(k == nK - 1)
        def _():
            @pl.when(i == 0)
            def _():
                for r in ao_r:
                    r[...] = jnp.zeros_like(r)

            def chunk(ci, carry):
                rows = pl.ds(pl.multiple_of(ci * ROW_CHUNK, ROW_CHUNK), ROW_CHUNK)
                epilogue(acc_sc[rows, :], rows, row_r, vec_r, ro_r, ao_r)
                return carry
            lax.fori_loop(0, TM // ROW_CHUNK, chunk, 0)

    outs, side_outs = _call(
        body, side, name=name, grid=(S // TM, nK), in_specs=in_specs, out_specs=out_specs, out_shape=out_shape,
        scratch=scratch, params=_params(("arbitrary", "arbitrary"), blocks, scratch, temps=[((TM, N), F32)]),
        args=[a, w3, *row_ins, *vec_ins])
    return outs if side is None else (outs, side_outs)


def _tn_mm(name, a, b, TMw, TNw, TK, cols_per_chip=None, place=_identity):
    S, M = a.shape
    N = b.shape[1]
    nK = S // TK
    if cols_per_chip is None:
        out_shape = jax.ShapeDtypeStruct((M, N), F32)
        out_spec = pl.BlockSpec((TMw, TNw), lambda i, j, k: (i, j))
    else:
        nb = cols_per_chip // TNw
        out_shape = jax.ShapeDtypeStruct((N_CHIPS, M, cols_per_chip), F32)
        out_spec = pl.BlockSpec((None, TMw, TNw), lambda i, j, k: (place(j) // nb, i, place(j) % nb))

    def body(a_ref, b_ref, o_ref):
        @pl.when(pl.program_id(2) == 0)
        def _():
            o_ref[...] = jnp.zeros_like(o_ref)
        o_ref[...] += lax.dot_general(a_ref[...], b_ref[...], (((0,), (0,)), ((), ())),
                                      preferred_element_type=F32)

    blocks = [((TK, TMw), BF16), ((TK, TNw), BF16), ((TMw, TNw), F32)]
    return pl.pallas_call(
        body, name=name, grid=(M // TMw, N // TNw, nK),
        in_specs=[pl.BlockSpec((TK, TMw), lambda i, j, k: (k, i)),
                  pl.BlockSpec((TK, TNw), lambda i, j, k: (k, j))],
        out_specs=out_spec, out_shape=out_shape,
        compiler_params=_params(("arbitrary", "arbitrary", "arbitrary"), blocks,
                                temps=[((TMw, TNw), F32), ((TK, TMw), BF16)]),
    )(a, b)


def _prev_rows(TM, H, col):
    return lambda i: (jnp.maximum(i * (TM // H) - 1, 0), col)


def _next_rows(S, TM, H, col):
    return lambda i: (jnp.minimum((i + 1) * (TM // H), S // H - 1), col)


def _taps_causal(ext_ref, w_ref, K, H, TM, cs):
    acc = None
    for k in range(K):
        term = ext_ref[pl.ds(H - (K - 1) + k, TM), cs] * w_ref[pl.ds(k, 1), cs]
        acc = term if acc is None else acc + term
    return acc


def _taps_anticausal(ext_ref, w_ref, K, TM, cs):
    acc = None
    for k in range(K):
        term = ext_ref[pl.ds(K - 1 - k, TM), cs] * w_ref[pl.ds(k, 1), cs]
        acc = term if acc is None else acc + term
    return acc


def _tap_grads(ext_ref, g, K, H, TM, cs):
    return [_rsum(ext_ref[pl.ds(H - (K - 1) + k, TM), cs] * g) for k in range(K)]


def _mixer_fwd(z, conv_a_w, conv_a_b, ln_g, ln_b, conv_b_w, S, TM, A, side=None):
    H = 32
    KA, KB = conv_a_w.shape[0], conv_b_w.shape[0]
    n_chunks = A // LANES
    RB = _pick(TM, (64, 32))

    def body(zc_ref, zh_ref, wa_ref, ba_ref, g_ref, b_ref, wb_ref, a1_ref, cat_ref, ext_a, ext_b):
        i = pl.program_id(0)
        live = (i > 0).astype(F32)
        zc = zc_ref[...].astype(F32)
        zh = zh_ref[...].astype(F32) * live
        ext_a[pl.ds(0, H), :] = zh[:, 0:A] * _sigmoid(zh[:, A:2 * A])
        ext_a[pl.ds(H, TM), :] = zc[:, 0:A] * _sigmoid(zc[:, A:2 * A])
        ext_b[pl.ds(0, H), :] = zh[:, 3 * A:4 * A] * zh[:, 4 * A:5 * A]
        ext_b[pl.ds(H, TM), :] = zc[:, 3 * A:4 * A] * zc[:, 4 * A:5 * A]

        def chunk(c, carry):
            cs = pl.ds(pl.multiple_of(c * LANES, LANES), LANES)
            for r0 in range(0, TM, RB):
                acc = None
                for k in range(KA):
                    term = ext_a[pl.ds(H - (KA - 1) + k + r0, RB), cs] * wa_ref[pl.ds(k, 1), cs]
                    acc = term if acc is None else acc + term
                a1_ref[pl.ds(r0, RB), cs] = acc + ba_ref[:, cs]
            return carry
        lax.fori_loop(0, n_chunks, chunk, 0)

        a1 = a1_ref[...]
        mu = jnp.mean(a1, axis=-1, keepdims=True)
        d = a1 - mu
        var = jnp.mean(d * d, axis=-1, keepdims=True)
        a2 = d * lax.rsqrt(var + EPS) * g_ref[...] + b_ref[...]
        cat_ref[:, 0:A] = (a2 * _sigmoid(a2)).astype(BF16)
        cbc = _taps_causal(ext_b, wb_ref, KB, H, TM, slice(None))
        cat_ref[:, A:2 * A] = (zc[:, 2 * A:3 * A] * cbc).astype(BF16)

    blocks = [((TM, 5 * A), BF16), ((H, 5 * A), BF16), ((KA, A), F32), ((KB, A), F32),
              ((TM, A), F32), ((TM, 2 * A), BF16)]
    scratch = [((H + TM, A), F32), ((H + TM, A), F32)]
    vec = lambda r: pl.BlockSpec((r, A), lambda i: (0, 0))
    outs, side_outs = _call(
        body, side, name="mixer_fwd", grid=(S // TM,),
        in_specs=[pl.BlockSpec((TM, 5 * A), lambda i: (i, 0)),
                  pl.BlockSpec((H, 5 * A), _prev_rows(TM, H, 0)),
                  vec(KA), vec(1), vec(1), vec(1), vec(KB)],
        out_specs=[pl.BlockSpec((TM, A), lambda i: (i, 0)), pl.BlockSpec((TM, 2 * A), lambda i: (i, 0))],
        out_shape=[jax.ShapeDtypeStruct((S, A), F32), jax.ShapeDtypeStruct((S, 2 * A), BF16)],
        scratch=scratch,
        params=_params(("arbitrary",), blocks, scratch, temps=[((TM, 5 * A), F32)] * 2 + [((TM, A), F32)] * 10),
        args=[z, z, conv_a_w, conv_a_b, ln_g, ln_b, conv_b_w])
    return outs if side is None else (outs, side_outs)


def _pair_tile(nF):
    return lambda t: (t % 2) * nF + t // 2


FFN_ROWS = 16


def _bcast_taps(w_ref, K, lanes):
    return [jnp.broadcast_to(w_ref[pl.ds(k, 1), lanes], (FFN_ROWS, LANES)) for k in range(K)]


def _ffn_act(u0, conv_w, S, TM, F, TC, side=None):
    H = 16
    K = conv_w.shape[0]
    nF = F // TC

    def body(uc_ref, uh_ref, wg_ref, wu_ref, o_ref, ext):
        live = (pl.program_id(0) > 0).astype(F32)
        ext[pl.ds(0, H), :] = uh_ref[...].astype(F32) * live
        ext[pl.ds(H, TM), :] = uc_ref[...].astype(F32)

        def lane_chunk(c, carry):
            lo = pl.ds(pl.multiple_of(c * LANES, LANES), LANES)
            lg, lu = lo, pl.ds(pl.multiple_of(TC + c * LANES, LANES), LANES)
            wg, wu = _bcast_taps(wg_ref, K, lo), _bcast_taps(wu_ref, K, lo)
            for r0 in range(0, TM, FFN_ROWS):
                g = u = None
                for k in range(K):
                    rows = pl.ds(H - (K - 1) + k + r0, FFN_ROWS)
                    tg, tu = ext[rows, lg] * wg[k], ext[rows, lu] * wu[k]
                    g, u = (tg, tu) if g is None else (g + tg, u + tu)
                o_ref[pl.ds(r0, FFN_ROWS), lo] = (g * _sigmoid(g) * u).astype(BF16)
            return carry
        lax.fori_loop(0, TC // LANES, lane_chunk, 0)

    blocks = [((TM, 2 * TC), BF16), ((H, 2 * TC), BF16), ((K, TC), F32), ((K, TC), F32), ((TM, TC), BF16)]
    scratch = [((H + TM, 2 * TC), F32)]
    outs, side_outs = _call(
        body, side, name="ffn_act", grid=(S // TM, nF),
        in_specs=[pl.BlockSpec((TM, 2 * TC), lambda i, j: (i, j)),
                  pl.BlockSpec((H, 2 * TC), lambda i, j: (jnp.maximum(i * (TM // H) - 1, 0), j)),
                  pl.BlockSpec((K, TC), lambda i, j: (0, j)),
                  pl.BlockSpec((K, TC), lambda i, j: (0, j + nF))],
        out_specs=[pl.BlockSpec((TM, TC), lambda i, j: (i, j))],
        out_shape=[jax.ShapeDtypeStruct((S, F), BF16)],
        scratch=scratch,
        params=_params(("arbitrary", "arbitrary"), blocks, scratch, temps=[((TM, 2 * TC), F32)]),
        args=[u0, u0, conv_w, conv_w])
    return outs if side is None else (outs, side_outs)


def _loss_head(h3, target, g_final, pp, gate, S, TM, D):
    def body(h_ref, t_ref, g_ref, pp_ref, gt_ref, loss_ref, dg_ref, dh_ref, dpre_ref, dpp_ref, db_ref):
        @pl.when(pl.program_id(0) == 0)
        def _():
            loss_ref[...] = jnp.zeros_like(loss_ref)
            dg_ref[...] = jnp.zeros_like(dg_ref)
            db_ref[...] = jnp.zeros_like(db_ref)
        h = h_ref[...]
        g = g_ref[...]
        r = _rms_stats(h)
        n = h * r
        diff = n * g - t_ref[...]
        loss_ref[...] += 0.5 * jnp.sum(jnp.mean(diff * diff, axis=-1, keepdims=True), axis=0, keepdims=True)
        dy = diff * (1.0 / D)
        dn = dy * g
        dh = r * (dn - n * jnp.mean(dn * n, axis=-1, keepdims=True))
        dh_ref[...] = dh
        dg_ref[...] += _rsum(dy * n)
        gt = gt_ref[...].astype(F32)
        dpre = dh * pp_ref[...].astype(F32) * gt * (1.0 - gt)
        dpre_ref[...] = dpre.astype(BF16)
        dpp_ref[...] = (dh * gt).astype(BF16)
        db_ref[...] += _rsum(dpre)

    blocks = [((TM, D), F32)] * 3 + [((TM, D), BF16)] * 4 + [((1, D), F32)] * 3
    row = pl.BlockSpec((TM, D), lambda i: (i, 0))
    vec = pl.BlockSpec((1, D), lambda i: (0, 0))
    return pl.pallas_call(
        body, name="loss_head", grid=(S // TM,),
        in_specs=[row, row, vec, row, row],
        out_specs=[pl.BlockSpec((1, 1), lambda i: (0, 0)), vec, row, row, row, vec],
        out_shape=[jax.ShapeDtypeStruct((1, 1), F32), jax.ShapeDtypeStruct((1, D), F32),
                   jax.ShapeDtypeStruct((S, D), F32), jax.ShapeDtypeStruct((S, D), BF16),
                   jax.ShapeDtypeStruct((S, D), BF16), jax.ShapeDtypeStruct((1, D), F32)],
        compiler_params=_params(("arbitrary",), blocks, temps=[((TM, D), F32)] * 10),
    )(h3, target, g_final, pp, gate)


def _ffn_bwd(u0, dact, conv_w, S, TM, F, TC, side=None):
    H = FFN_ROWS
    K = conv_w.shape[0]
    nF, nI = F // TC, S // TM

    def body(up_ref, uc_ref, un_ref, dc_ref, dn_ref, wg_ref, wu_ref, o_ref, dwg_ref, dwu_ref,
             ext_u, ext_d, ext_a):
        i = pl.program_id(1)
        @pl.when(i == 0)
        def _():
            dwg_ref[...] = jnp.zeros_like(dwg_ref)
            dwu_ref[...] = jnp.zeros_like(dwu_ref)
        last = (i < nI - 1).astype(F32)
        ext_u[pl.ds(0, H), :] = up_ref[...].astype(F32) * (i > 0).astype(F32)
        ext_u[pl.ds(H, TM), :] = uc_ref[...].astype(F32)
        ext_u[pl.ds(H + TM, H), :] = un_ref[...].astype(F32) * last
        ext_a[pl.ds(0, TM), :] = dc_ref[...].astype(F32)
        ext_a[pl.ds(TM, H), :] = dn_ref[...].astype(F32) * last

        def lane_chunk(c, carry):
            lo = pl.ds(pl.multiple_of(c * LANES, LANES), LANES)
            lg, lu = lo, pl.ds(pl.multiple_of(TC + c * LANES, LANES), LANES)
            wg, wu = _bcast_taps(wg_ref, K, lo), _bcast_taps(wu_ref, K, lo)
            sums_g, sums_u = [None] * K, [None] * K
            for r0 in range(0, TM + H, FFN_ROWS):
                xg = [ext_u[pl.ds(H - (K - 1) + k + r0, FFN_ROWS), lg] for k in range(K)]
                xu = [ext_u[pl.ds(H - (K - 1) + k + r0, FFN_ROWS), lu] for k in range(K)]
                g, u = xg[0] * wg[0], xu[0] * wu[0]
                for k in range(1, K):
                    g, u = g + xg[k] * wg[k], u + xu[k] * wu[k]
                da = ext_a[pl.ds(r0, FFN_ROWS), lo]
                s = _sigmoid(g)
                dg = da * u * s * (1.0 + g * (1.0 - s))
                du = da * g * s
                ext_d[pl.ds(r0, FFN_ROWS), lg] = dg
                ext_d[pl.ds(r0, FFN_ROWS), lu] = du
                if r0 < TM:
                    for k in range(K):
                        tg, tu = xg[k] * dg, xu[k] * du
                        sums_g[k] = tg if sums_g[k] is None else sums_g[k] + tg
                        sums_u[k] = tu if sums_u[k] is None else sums_u[k] + tu
            for k in range(K):
                dwg_ref[pl.ds(k, 1), lo] += _rsum(sums_g[k])
                dwu_ref[pl.ds(k, 1), lo] += _rsum(sums_u[k])
            for r0 in range(0, TM, FFN_ROWS):
                g = u = None
                for k in range(K):
                    rows = pl.ds(K - 1 - k + r0, FFN_ROWS)
                    tg, tu = ext_d[rows, lg] * wg[k], ext_d[rows, lu] * wu[k]
                    g, u = (tg, tu) if g is None else (g + tg, u + tu)
                o_ref[pl.ds(r0, FFN_ROWS), lg] = g.astype(BF16)
                o_ref[pl.ds(r0, FFN_ROWS), lu] = u.astype(BF16)
            return carry
        lax.fori_loop(0, TC // LANES, lane_chunk, 0)

    blocks = [((H, 2 * TC), BF16), ((TM, 2 * TC), BF16), ((H, 2 * TC), BF16), ((TM, TC), BF16), ((H, TC), BF16),
              ((K, TC), F32), ((K, TC), F32), ((TM, 2 * TC), BF16), ((K, TC), F32), ((K, TC), F32)]
    scratch = [((TM + 2 * H, 2 * TC), F32), ((TM + H, 2 * TC), F32), ((TM + H, TC), F32)]
    prev = lambda j, i: (jnp.maximum(i * (TM // H) - 1, 0), j)
    nxt = lambda j, i: (jnp.minimum((i + 1) * (TM // H), S // H - 1), j)
    taps_out = pl.BlockSpec((K, TC), lambda j, i: (0, j))
    outs, side_outs = _call(
        body, side, name="ffn_bwd", grid=(nF, nI),
        in_specs=[pl.BlockSpec((H, 2 * TC), prev), pl.BlockSpec((TM, 2 * TC), lambda j, i: (i, j)),
                  pl.BlockSpec((H, 2 * TC), nxt),
                  pl.BlockSpec((TM, TC), lambda j, i: (i, j)), pl.BlockSpec((H, TC), nxt),
                  pl.BlockSpec((K, TC), lambda j, i: (0, j)), pl.BlockSpec((K, TC), lambda j, i: (0, j + nF))],
        out_specs=[pl.BlockSpec((TM, 2 * TC), lambda j, i: (i, j)), taps_out, taps_out],
        out_shape=[jax.ShapeDtypeStruct((S, 2 * F), BF16), jax.ShapeDtypeStruct((K, F), F32),
                   jax.ShapeDtypeStruct((K, F), F32)],
        scratch=scratch,
        params=_params(("arbitrary", "arbitrary"), blocks, scratch, temps=[((TM, 2 * TC), F32)]),
        args=[u0, u0, u0, dact, dact, conv_w, conv_w])
    return outs if side is None else (outs, side_outs)


def _ffn_up_bwd(u0, dact, conv_w, w_up3, h1, dh2, g_ffn, S, TM, F, TC, side=None):
    H = FFN_ROWS
    K = conv_w.shape[0]
    D = h1.shape[1]
    nF, nI = F // TC, S // TM
    nb = w_up3.shape[2] // TC
    n_lane = TC // LANES

    def body(up_ref, uc_ref, un_ref, dc_ref, dn_ref, wg_ref, wu_ref, mg_ref, mu_ref, h_ref, r_ref, g_ref,
             o_ref, dwg_ref, dwu_ref, dh_ref, dhb_ref, dgain_ref, ext_u, ext_d, acc):
        i, m = pl.program_id(0), pl.program_id(1)

        @pl.when((i == 0) & (m == 0))
        def _():
            dwg_ref[...] = jnp.zeros_like(dwg_ref)
            dwu_ref[...] = jnp.zeros_like(dwu_ref)
            dgain_ref[...] = jnp.zeros_like(dgain_ref)

        @pl.when(m == 0)
        def _():
            acc[...] = jnp.zeros_like(acc)
        last = (i < nI - 1).astype(F32)
        ext_u[pl.ds(0, H), :] = up_ref[...].astype(F32) * (i > 0).astype(F32)
        ext_u[pl.ds(H, TM), :] = uc_ref[...].astype(F32)
        ext_u[pl.ds(H + TM, H), :] = un_ref[...].astype(F32) * last

        for c in range(n_lane):
            lo = pl.ds(c * LANES, LANES)
            lg, lu = lo, pl.ds(TC + c * LANES, LANES)
            taps_at = pl.ds(pl.multiple_of(m * TC + c * LANES, LANES), LANES)
            wg, wu = _bcast_taps(wg_ref, K, lo), _bcast_taps(wu_ref, K, lo)
            sums_g, sums_u = [None] * K, [None] * K
            for r0 in range(0, TM + H, FFN_ROWS):
                xg = [ext_u[pl.ds(H - (K - 1) + k + r0, FFN_ROWS), lg] for k in range(K)]
                xu = [ext_u[pl.ds(H - (K - 1) + k + r0, FFN_ROWS), lu] for k in range(K)]
                g, u = xg[0] * wg[0], xu[0] * wu[0]
                for k in range(1, K):
                    g, u = g + xg[k] * wg[k], u + xu[k] * wu[k]
                if r0 < TM:
                    da = dc_ref[pl.ds(r0, FFN_ROWS), lo].astype(F32)
                else:
                    da = dn_ref[:, lo].astype(F32) * last
                s = _sigmoid(g)
                dg = da * u * s * (1.0 + g * (1.0 - s))
                du = da * g * s
                ext_d[pl.ds(r0, FFN_ROWS), lg] = dg
                ext_d[pl.ds(r0, FFN_ROWS), lu] = du
                if r0 < TM:
                    for k in range(K):
                        tg, tu = xg[k] * dg, xu[k] * du
                        sums_g[k] = tg if sums_g[k] is None else sums_g[k] + tg
                        sums_u[k] = tu if sums_u[k] is None else sums_u[k] + tu
            for k in range(K):
                dwg_ref[pl.ds(k, 1), taps_at] += _rsum(sums_g[k])
                dwu_ref[pl.ds(k, 1), taps_at] += _rsum(sums_u[k])
            for r0 in range(0, TM, FFN_ROWS):
                g = u = None
                for k in range(K):
                    rows = pl.ds(K - 1 - k + r0, FFN_ROWS)
                    tg, tu = ext_d[rows, lg] * wg[k], ext_d[rows, lu] * wu[k]
                    g, u = (tg, tu) if g is None else (g + tg, u + tu)
                o_ref[pl.ds(r0, FFN_ROWS), lg] = g.astype(BF16)
                o_ref[pl.ds(r0, FFN_ROWS), lu] = u.astype(BF16)
            if c % 2 == 1 or c == n_lane - 1:
                first = c - (c % 2)
                kg = pl.ds(first * LANES, (c + 1 - first) * LANES)
                ku = pl.ds(TC + first * LANES, (c + 1 - first) * LANES)
                nt = (((1,), (1,)), ((), ()))
                acc[...] += (lax.dot_general(o_ref[:, kg], mg_ref[:, kg], nt, preferred_element_type=F32)
                             + lax.dot_general(o_ref[:, ku], mu_ref[:, kg], nt, preferred_element_type=F32))

        @pl.when(m == nF - 1)
        def _():
            def chunk(ci, carry):
                rows = pl.ds(pl.multiple_of(ci * ROW_CHUNK, ROW_CHUNK), ROW_CHUNK)
                dh, dgn = _rms_bwd(h_ref[rows, :], g_ref[...], acc[rows, :])
                dh1 = r_ref[rows, :] + dh
                dh_ref[rows, :] = dh1
                dhb_ref[rows, :] = dh1.astype(BF16)
                dgain_ref[...] += dgn
                return carry
            lax.fori_loop(0, TM // ROW_CHUNK, chunk, 0)

    place = _pair_tile(nF)
    prev = lambda i, m: (jnp.maximum(i * (TM // H) - 1, 0), m)
    nxt = lambda i, m: (jnp.minimum((i + 1) * (TM // H), S // H - 1), m)
    row = pl.BlockSpec((TM, D), lambda i, m: (i, 0), pipeline_mode=pl.Buffered(1))
    whole = lambda shape: pl.BlockSpec(shape, lambda i, m: (0, 0))
    blocks = [((H, 2 * TC), BF16), ((TM, 2 * TC), BF16), ((H, 2 * TC), BF16), ((TM, TC), BF16), ((H, TC), BF16),
              ((K, TC), F32), ((K, TC), F32), ((D, TC), BF16), ((D, TC), BF16),
              ((TM, 2 * TC), BF16), ((K, F), F32), ((K, F), F32)]
    scratch = [((TM + 2 * H, 2 * TC), F32), ((TM + H, 2 * TC), F32), ((TM, D), F32)]
    single = [((TM, D), F32)] * 3 + [((TM, D), BF16)]
    outs, side_outs = _call(
        body, side, name="ffn_up_bwd", grid=(nI, nF),
        in_specs=[pl.BlockSpec((H, 2 * TC), prev), pl.BlockSpec((TM, 2 * TC), lambda i, m: (i, m)),
                  pl.BlockSpec((H, 2 * TC), nxt),
                  pl.BlockSpec((TM, TC), lambda i, m: (i, m)), pl.BlockSpec((H, TC), nxt),
                  pl.BlockSpec((K, TC), lambda i, m: (0, m)), pl.BlockSpec((K, TC), lambda i, m: (0, m + nF)),
                  pl.BlockSpec((None, D, TC), lambda i, m: (place(2 * m) // nb, 0, place(2 * m) % nb)),
                  pl.BlockSpec((None, D, TC), lambda i, m: (place(2 * m + 1) // nb, 0, place(2 * m + 1) % nb)),
                  row, row, whole((1, D))],
        out_specs=[pl.BlockSpec((TM, 2 * TC), lambda i, m: (i, m)), whole((K, F)), whole((K, F)),
                   row, row, whole((1, D))],
        out_shape=[jax.ShapeDtypeStruct((S, 2 * F), BF16), jax.ShapeDtypeStruct((K, F), F32),
                   jax.ShapeDtypeStruct((K, F), F32), jax.ShapeDtypeStruct((S, D), F32),
                   jax.ShapeDtypeStruct((S, D), BF16), jax.ShapeDtypeStruct((1, D), F32)],
        scratch=scratch,
        params=_params(("arbitrary", "arbitrary"), blocks, scratch + single, temps=[((TM, D), F32)] * 2),
        args=[u0, u0, u0, dact, dact, conv_w, conv_w, w_up3, w_up3, h1, dh2, g_ffn])
    return outs if side is None else (outs, side_outs)


def _mixer_bwd_ln(dcat, a1, ln_g, ln_b, S, TM, A):
    def body(dc_ref, a1_ref, g_ref, b_ref, da1_ref, acc_ref):
        @pl.when(pl.program_id(0) == 0)
        def _():
            acc_ref[...] = jnp.zeros_like(acc_ref)
        a1 = a1_ref[...]
        g = g_ref[...]
        mu = jnp.mean(a1, axis=-1, keepdims=True)
        d = a1 - mu
        rstd = lax.rsqrt(jnp.mean(d * d, axis=-1, keepdims=True) + EPS)
        nh = d * rstd
        a2 = nh * g + b_ref[...]
        s = _sigmoid(a2)
        da2 = dc_ref[...].astype(F32) * s * (1.0 + a2 * (1.0 - s))
        dnh = da2 * g
        da1 = rstd * (dnh - jnp.mean(dnh, axis=-1, keepdims=True)
                      - nh * jnp.mean(dnh * nh, axis=-1, keepdims=True))
        da1_ref[...] = da1
        acc_ref[pl.ds(0, 1), :] += _rsum(da2 * nh)
        acc_ref[pl.ds(1, 1), :] += _rsum(da2)
        acc_ref[pl.ds(2, 1), :] += _rsum(da1)

    blocks = [((TM, A), BF16), ((TM, A), F32), ((TM, A), F32), ((4, A), F32)]
    return pl.pallas_call(
        body, name="mixer_bwd_ln", grid=(S // TM,),
        in_specs=[pl.BlockSpec((TM, A), lambda i: (i, 0)), pl.BlockSpec((TM, A), lambda i: (i, 0)),
                  pl.BlockSpec((1, A), lambda i: (0, 0)), pl.BlockSpec((1, A), lambda i: (0, 0))],
        out_specs=[pl.BlockSpec((TM, A), lambda i: (i, 0)), pl.BlockSpec((4, A), lambda i: (0, 0))],
        out_shape=[jax.ShapeDtypeStruct((S, A), F32), jax.ShapeDtypeStruct((4, A), F32)],
        compiler_params=_params(("arbitrary",), blocks, temps=[((TM, A), F32)] * 12),
    )(dcat, a1, ln_g, ln_b)


def _mixer_bwd_conv(z, dcat, da1, conv_a_w, conv_b_w, S, TM, A, side=None):
    H = 32
    KA, KB = conv_a_w.shape[0], conv_b_w.shape[0]
    nI = S // TM
    n_chunks = A // LANES
    RB = _pick(TM, (64, 32))

    def body(zc_ref, zp_ref, zn_ref, dbc_ref, dbn_ref, d1c_ref, d1n_ref, wa_ref, wb_ref,
             dz_ref, dwa_ref, dwb_ref, ext_a0, ext_d1, ext_cb, ext_dc, da0_sc):
        i = pl.program_id(0)
        @pl.when(i == 0)
        def _():
            dwa_ref[...] = jnp.zeros_like(dwa_ref)
            dwb_ref[...] = jnp.zeros_like(dwb_ref)
        first = (i > 0).astype(F32)
        last = (i < nI - 1).astype(F32)
        zc = zc_ref[...].astype(F32)
        zp = zp_ref[...].astype(F32) * first
        a_val, a_gate = zc[:, 0:A], zc[:, A:2 * A]
        b_gate, c_gate, b_h = zc[:, 2 * A:3 * A], zc[:, 3 * A:4 * A], zc[:, 4 * A:5 * A]
        sig = _sigmoid(a_gate)
        ext_a0[pl.ds(0, H), :] = zp[:, 0:A] * _sigmoid(zp[:, A:2 * A])
        ext_a0[pl.ds(H, TM), :] = a_val * sig
        ext_d1[pl.ds(0, TM), :] = d1c_ref[...]
        ext_d1[pl.ds(TM, H), :] = d1n_ref[...] * last
        ext_cb[pl.ds(0, H), :] = zp[:, 3 * A:4 * A] * zp[:, 4 * A:5 * A]
        ext_cb[pl.ds(H, TM), :] = c_gate * b_h
        dbx = dbc_ref[...].astype(F32)
        dcbc = dbx * b_gate
        ext_dc[pl.ds(0, TM), :] = dcbc
        ext_dc[pl.ds(TM, H), :] = dbn_ref[...].astype(F32) * zn_ref[...].astype(F32) * last

        def chunk(c, carry):
            cs = pl.ds(pl.multiple_of(c * LANES, LANES), LANES)
            for r0 in range(0, TM, RB):
                acc = None
                for k in range(KA):
                    term = ext_d1[pl.ds(KA - 1 - k + r0, RB), cs] * wa_ref[pl.ds(k, 1), cs]
                    acc = term if acc is None else acc + term
                da0_sc[pl.ds(r0, RB), cs] = acc
            for k in range(KA):
                acc = None
                for r0 in range(0, TM, RB):
                    term = ext_a0[pl.ds(H - (KA - 1) + k + r0, RB), cs] * ext_d1[pl.ds(r0, RB), cs]
                    acc = term if acc is None else acc + term
                dwa_ref[pl.ds(k, 1), cs] += _rsum(acc)
            return carry
        lax.fori_loop(0, n_chunks, chunk, 0)

        da0 = da0_sc[...]
        dz_ref[:, 0:A] = (da0 * sig).astype(BF16)
        dz_ref[:, A:2 * A] = (da0 * a_val * sig * (1.0 - sig)).astype(BF16)
        cbc = _taps_causal(ext_cb, wb_ref, KB, H, TM, slice(None))
        dz_ref[:, 2 * A:3 * A] = (dbx * cbc).astype(BF16)
        dcb = _taps_anticausal(ext_dc, wb_ref, KB, TM, slice(None))
        dz_ref[:, 3 * A:4 * A] = (dcb * b_h).astype(BF16)
        dz_ref[:, 4 * A:5 * A] = (dcb * c_gate).astype(BF16)
        grads = _tap_grads(ext_cb, dcbc, KB, H, TM, slice(None))
        for k in range(KB):
            dwb_ref[pl.ds(k, 1), :] += grads[k]

    blocks = [((TM, 5 * A), BF16), ((H, 5 * A), BF16), ((H, A), BF16), ((TM, A), BF16), ((H, A), BF16),
              ((TM, A), F32), ((H, A), F32), ((KA, A), F32), ((KB, A), F32),
              ((TM, 5 * A), BF16), ((KA, A), F32), ((KB, A), F32)]
    scratch = [((H + TM, A), F32)] * 4 + [((TM, A), F32)]
    vec = lambda r: pl.BlockSpec((r, A), lambda i: (0, 0))
    outs, side_outs = _call(
        body, side, name="mixer_bwd_conv", grid=(nI,),
        in_specs=[pl.BlockSpec((TM, 5 * A), lambda i: (i, 0)),
                  pl.BlockSpec((H, 5 * A), _prev_rows(TM, H, 0)),
                  pl.BlockSpec((H, A), _next_rows(S, TM, H, 2)),
                  pl.BlockSpec((TM, A), lambda i: (i, 1)),
                  pl.BlockSpec((H, A), _next_rows(S, TM, H, 1)),
                  pl.BlockSpec((TM, A), lambda i: (i, 0)),
                  pl.BlockSpec((H, A), _next_rows(S, TM, H, 0)),
                  vec(KA), vec(KB)],
        out_specs=[pl.BlockSpec((TM, 5 * A), lambda i: (i, 0)), vec(KA), vec(KB)],
        out_shape=[jax.ShapeDtypeStruct((S, 5 * A), BF16), jax.ShapeDtypeStruct((KA, A), F32),
                   jax.ShapeDtypeStruct((KB, A), F32)],
        scratch=scratch,
        params=_params(("arbitrary",), blocks, scratch, temps=[((TM, 5 * A), F32)] * 2 + [((TM, A), F32)] * 14),
        args=[z, z, z, dcat, dcat, da1, da1, conv_a_w, conv_b_w])
    return outs if side is None else (outs, side_outs)


def _row_tile(R):
    return _pick(R, (256, 128, 64, 32, 16, 8))


def _scalars(*vals):
    return jnp.stack([jnp.asarray(v, jnp.int32) for v in vals])


def _cast_into_gathered(name, w, chip):
    R, C = w.shape
    TR = _row_tile(R)

    def body(s_ref, w_ref, o_ref):
        o_ref[...] = w_ref[...].astype(BF16)

    grid_spec = pltpu.PrefetchScalarGridSpec(
        num_scalar_prefetch=1, grid=(R // TR,),
        in_specs=[pl.BlockSpec((TR, C), lambda r, s: (r, 0))],
        out_specs=pl.BlockSpec((None, TR, C), lambda r, s: (s[0], r, 0)))
    return pl.pallas_call(body, name=name, grid_spec=grid_spec,
                          out_shape=jax.ShapeDtypeStruct((N_CHIPS, R, C), BF16),
                          compiler_params=_params(("arbitrary",), [((TR, C), F32), ((TR, C), BF16)]),
                          )(_scalars(chip), w)


def _add_pair(name, dw, recv, c):
    _, _, Rh, C = dw.shape
    TR = _row_tile(Rh)

    def body(c_ref, a_ref, b_ref, o_ref, ob_ref):
        s = a_ref[...] + b_ref[...]
        o_ref[...] = s
        ob_ref[...] = s.astype(BF16)

    out_spec = pl.BlockSpec((None, TR, C), lambda k, r, c_ref: (k, r, 0))
    grid_spec = pltpu.PrefetchScalarGridSpec(
        num_scalar_prefetch=1, grid=(N_CHIPS, Rh // TR),
        in_specs=[pl.BlockSpec((None, None, TR, C), lambda k, r, c_ref: (k, c_ref[0], r, 0)),
                  pl.BlockSpec((None, TR, C), lambda k, r, c_ref: (k, r, 0))],
        out_specs=[out_spec, out_spec])
    return pl.pallas_call(body, name=name, grid_spec=grid_spec,
                          out_shape=[jax.ShapeDtypeStruct((N_CHIPS, Rh, C), F32),
                                     jax.ShapeDtypeStruct((N_CHIPS, Rh, C), BF16)],
                          compiler_params=_params(("arbitrary", "arbitrary"), [((TR, C), F32)] * 4),
                          )(_scalars(c), dw, recv)


def _add_chips(name, parts, recv, chip, c):
    _, Rh, C = parts.shape
    TR = _row_tile(Rh)

    def body(s_ref, p_ref, r_ref, o_ref):
        o_ref[...] = ((p_ref[...] + r_ref[0].astype(F32)) + r_ref[1].astype(F32)) + r_ref[2].astype(F32)

    grid_spec = pltpu.PrefetchScalarGridSpec(
        num_scalar_prefetch=1, grid=(Rh // TR,),
        in_specs=[pl.BlockSpec((None, TR, C), lambda r, s: (s[0], r, 0)),
                  pl.BlockSpec((N_CHIPS - 1, TR, C), lambda r, s: (0, r, 0))],
        out_specs=pl.BlockSpec((None, TR, C), lambda r, s: (s[1], r, 0)))
    return pl.pallas_call(body, name=name, grid_spec=grid_spec,
                          out_shape=jax.ShapeDtypeStruct((2, Rh, C), F32),
                          compiler_params=_params(("arbitrary",), [((N_CHIPS + 1, TR, C), F32)]),
                          )(_scalars(chip, c), parts, recv)


def _sum_devices(name, parts):
    _, R, C = parts.shape

    def body(p_ref, o_ref):
        acc = p_ref[0]
        for d in range(1, N_DEV):
            acc = acc + p_ref[d]
        o_ref[...] = acc

    return pl.pallas_call(body, name=name, out_shape=jax.ShapeDtypeStruct((R, C), F32),
                          in_specs=[pl.BlockSpec(memory_space=pltpu.VMEM)],
                          out_specs=pl.BlockSpec(memory_space=pltpu.VMEM))(parts)


def _adamw(name, w, g, m, v):
    R, C = w.shape
    TR = _pick(R, (128, 64, 32, 16, 8))
    c1 = 1.0 - ADAM_B1 ** ADAM_STEP
    c2 = 1.0 - ADAM_B2 ** ADAM_STEP

    def body(w_ref, g_ref, m_ref, v_ref, d_ref, nm_ref, nv_ref):
        g_ = g_ref[...]
        nm = ADAM_B1 * m_ref[...] + (1.0 - ADAM_B1) * g_
        nv = ADAM_B2 * v_ref[...] + (1.0 - ADAM_B2) * (g_ * g_)
        d_ref[...] = -ADAM_LR * ((nm / c1) / (jnp.sqrt(nv / c2) + ADAM_EPS) + ADAM_WD * w_ref[...])
        nm_ref[...] = nm
        nv_ref[...] = nv

    spec = pl.BlockSpec((TR, C), lambda r: (r, 0))
    shp = jax.ShapeDtypeStruct((R, C), F32)
    return pl.pallas_call(body, name=name, grid=(R // TR,), in_specs=[spec] * 4, out_specs=[spec] * 3,
                          out_shape=[shp] * 3,
                          compiler_params=_params(("arbitrary",), [((TR, C), F32)] * 7))(w, g, m, v)


def _place():
    x, y, c = lax.axis_index("x"), lax.axis_index("y"), lax.axis_index("c")
    others = [(1 - x, y), (x, 1 - y), (1 - x, 1 - y)]
    return x, y, c, others


def _allgather_small(name, block):
    R, C = block.shape

    def body(x_ref, out_ref, send_sems, recv_sems, local_sem):
        x, y, c, chips = _place()
        me, sibling = (x, y, c), (x, y, 1 - c)

        def rows(px, py, pc):
            return out_ref.at[4 * px + 2 * py + pc]

        def copy(k, blk, to, src=None):
            return pltpu.make_async_remote_copy(
                src_ref=rows(*blk) if src is None else src, dst_ref=rows(*blk),
                send_sem=send_sems.at[k], recv_sem=recv_sems.at[k], device_id=to, device_id_type=MESH)

        mine = pltpu.make_async_copy(x_ref, rows(*me), local_sem)
        mine.start()
        first = [copy(0, me, sibling, src=x_ref)]
        first += [copy(1 + j, me, (*chip, c), src=x_ref) for j, chip in enumerate(chips)]
        for cp in first:
            cp.start()
        passed = [copy(4 + j, (*chip, c), sibling) for j, chip in enumerate(chips)]
        for j, chip in enumerate(chips):
            copy(1 + j, (*chip, c), me).wait_recv()
            passed[j].start()
        copy(0, sibling, me).wait_recv()
        for j, chip in enumerate(chips):
            copy(4 + j, (*chip, 1 - c), me).wait_recv()
        for cp in first + passed:
            cp.wait_send()
        mine.wait()

    return pl.pallas_call(
        body, name=name, out_shape=jax.ShapeDtypeStruct((N_DEV, R, C), F32),
        in_specs=[pl.BlockSpec(memory_space=pltpu.VMEM)], out_specs=pl.BlockSpec(memory_space=pltpu.VMEM),
        scratch_shapes=[pltpu.SemaphoreType.DMA((7,)), pltpu.SemaphoreType.DMA((7,)), pltpu.SemaphoreType.DMA],
    )(block)


def _gather_side(bufs, across, within):
    def rows(ref, chip, half, piece):
        _, r0, n = piece
        return ref.at[2 * chip[0] + chip[1], pl.ds(half * (ref.shape[1] // 2) + r0, n)]

    def copies(ins, outs, send_sems, recv_sems, base):
        x, y, c, chips = _place()
        sibling = (x, y, 1 - c)
        pairs = []

        def add(k, src, dst, to, arrival):
            mk = lambda s, d, dev: pltpu.make_async_remote_copy(
                src_ref=s, dst_ref=d, send_sem=send_sems.at[base + k], recv_sem=recv_sems.at[base + k],
                device_id=dev, device_id_type=MESH)
            pairs.append((mk(src, dst, to), mk(arrival, arrival, (x, y, c))))

        for p, piece in enumerate(across):
            ref = outs[piece[0]]
            for j, chip in enumerate(chips):
                mine = rows(ref, (x, y), c, piece)
                add(3 * p + j, mine, mine, (*chip, c), rows(ref, chip, c, piece))
        for q, piece in enumerate(within):
            ref = outs[piece[0]]
            for j, chip in enumerate(chips):
                held = rows(ref, chip, c, piece)
                add(3 * (len(across) + q) + j, held, held, sibling, rows(ref, chip, 1 - c, piece))
        return pairs

    def start(*refs):
        for send, _ in copies(*refs):
            send.start()

    def wait(*refs):
        pairs = copies(*refs)
        for _, arrival in pairs:
            arrival.wait_recv()
        for send, _ in pairs:
            send.wait_send()

    return _Side(list(bufs), [jax.ShapeDtypeStruct(b.shape, b.dtype) for b in bufs],
                 3 * (len(across) + len(within)), start, wait, aliased=True)


def _chip_exchange(parts):
    n = len(parts)

    def copies(ins, outs, send_sems, recv_sems, base):
        x, y, c, chips = _place()
        return [pltpu.make_async_remote_copy(
            src_ref=ins[a].at[2 * chip[0] + chip[1]], dst_ref=outs[a].at[j],
            send_sem=send_sems.at[base + 3 * a + j], recv_sem=recv_sems.at[base + 3 * a + j],
            device_id=(*chip, c), device_id_type=MESH) for a in range(n) for j, chip in enumerate(chips)]

    return _Side(list(parts), [jax.ShapeDtypeStruct((N_CHIPS - 1,) + p.shape[1:], p.dtype) for p in parts],
                 3 * n, *_start_wait(copies))


def _pair_exchange(grads):
    def copies(ins, outs, send_sems, recv_sems, base):
        x, y, c, _ = _place()
        return [pltpu.make_async_remote_copy(
            src_ref=ins[a].at[:, 1 - c], dst_ref=outs[a], send_sem=send_sems.at[base + a],
            recv_sem=recv_sems.at[base + a], device_id=(x, y, 1 - c), device_id_type=MESH)
            for a in range(len(grads))]

    return _Side(list(grads), [jax.ShapeDtypeStruct((N_CHIPS,) + g.shape[2:], F32) for g in grads],
                 len(grads), *_start_wait(copies))


def _start_wait(copies):
    def start(*refs):
        for cp in copies(*refs):
            cp.start()

    def wait(*refs):
        cps = copies(*refs)
        for cp in cps:
            cp.wait_recv()
        for cp in cps:
            cp.wait_send()
    return start, wait


def _both(first, second):
    n_in, n_out = len(first.ins), len(first.out_shapes)

    def run(which):
        def go(ins, outs, send_sems, recv_sems, base):
            getattr(first, which)(ins[:n_in], outs[:n_out], send_sems, recv_sems, base)
            getattr(second, which)(ins[n_in:], outs[n_out:], send_sems, recv_sems, base + first.n_sems)
        return go

    return _Side(first.ins + second.ins, first.out_shapes + second.out_shapes,
                 first.n_sems + second.n_sems, run("start"), run("wait"))


def _share_halves(halves):
    n = len(halves)

    def body(*refs):
        ins, outs = refs[:n], refs[n:2 * n]
        send_sems, recv_sems = refs[2 * n:]
        x, y, c, _ = _place()
        sends = []
        for a in range(n):
            sends.append(pltpu.make_async_remote_copy(
                src_ref=ins[a].at[c], dst_ref=outs[a].at[c], send_sem=send_sems.at[a], recv_sem=recv_sems.at[a],
                device_id=(x, y, 1 - c), device_id_type=MESH))
            sends[a].start()
        for a in range(n):
            theirs = outs[a].at[1 - c]
            pltpu.make_async_remote_copy(
                src_ref=theirs, dst_ref=theirs, send_sem=send_sems.at[a], recv_sem=recv_sems.at[a],
                device_id=(x, y, c), device_id_type=MESH).wait_recv()
        for cp in sends:
            cp.wait_send()

    return pl.pallas_call(
        body, name="grads_share_halves",
        out_shape=[jax.ShapeDtypeStruct(h.shape, F32) for h in halves],
        in_specs=[ANY] * n, out_specs=[ANY] * n, input_output_aliases={a: a for a in range(n)},
        scratch_shapes=[pltpu.SemaphoreType.DMA((n,)), pltpu.SemaphoreType.DMA((n,))],
    )(*halves)


def _pack(arrays):
    pieces = []
    for a in arrays:
        flat = a.reshape(-1).astype(F32)
        pieces.append(jnp.pad(flat, (0, (-flat.size) % PACK_ALIGN)))
    return jnp.concatenate(pieces).reshape(-1, LANES)


def _unpack(buf, shapes):
    lead = buf.shape[:-2]
    flat = buf.reshape(lead + (-1,))
    out, off = [], 0
    for shp in shapes:
        size = 1
        for s in shp:
            size *= s
        out.append(flat[..., off:off + size].reshape(lead + tuple(shp)))
        off += size + (-size) % PACK_ALIGN
    return out


def _gather_channels(buf, shapes):
    per_chip = _unpack(buf[0::2], shapes)
    return [jnp.transpose(a, (1, 0, 2)).reshape(a.shape[1], -1) for a in per_chip]


def _mm_tile(n, rows, limit_bytes=6 * 1024 * 1024):
    for t in (1408, 1280, 1024, 640, 512, 384, 256, 128):
        if n % t == 0 and rows * t * 2 <= limit_bytes:
            return t
    raise ValueError(f"no column tile for {n} x {rows}")


def kernel(x, p, norm_mix_g, w_in, conv_a_w, conv_a_b, ln_a_g, ln_a_b, conv_b_w, w_out, norm_ffn_g, w_up, conv_ffn_w, w_down, w_ple_gate, b_ple_gate, w_ple_proj, norm_final_g, loss_target, m_norm_mix_g, m_w_in, m_conv_a_w, m_conv_a_b, m_ln_a_g, m_ln_a_b, m_conv_b_w, m_w_out, m_norm_ffn_g, m_w_up, m_conv_ffn_w, m_w_down, m_w_ple_gate, m_b_ple_gate, m_w_ple_proj, m_norm_final_g, v_norm_mix_g, v_w_in, v_conv_a_w, v_conv_a_b, v_ln_a_g, v_ln_a_b, v_conv_b_w, v_w_out, v_norm_ffn_g, v_w_up, v_conv_ffn_w, v_w_down, v_w_ple_gate, v_b_ple_gate, v_w_ple_proj, v_norm_final_g):
    S, D = x.shape[1], x.shape[2]
    P = p.shape[3]
    A = conv_a_b.shape[1]
    F = w_down.shape[1] * N_CHIPS
    KA, KB, KF = conv_a_w.shape[1], conv_b_w.shape[1], conv_ffn_w.shape[1]
    xi, yi, ci = lax.axis_index("x"), lax.axis_index("y"), lax.axis_index("c")
    chip = 2 * xi + yi

    TM = _pick(S, (512, 256, 128))
    TE = _pick(S, (256, 128))
    TC = _pick(2 * F // N_CHIPS, (1408, 1024, 512, 256, 128))
    ffn_place = _pair_tile(F // TC)

    x2, p2, t2 = x.reshape(S, D), p.reshape(S, P), loss_target.reshape(S, D)
    gfin = norm_final_g.reshape(1, D)

    big = dict(w_in=w_in[0], w_out=w_out[0], w_up=w_up[0], w_down=w_down[0],
               w_ple_gate=w_ple_gate[0], w_ple_proj=w_ple_proj[0])
    names = list(big)
    buf = {n: _cast_into_gathered("cast_" + n, big[n], chip) for n in names}
    half = {n: big[n].shape[0] // 2 for n in names}
    up_a = half["w_up"] // 2
    (w_in3,) = _comm_only("gather_w_in_across", _gather_side([buf["w_in"]], [(0, 0, half["w_in"])], []))
    (w_in3,) = _comm_only("gather_w_in_within", _gather_side([w_in3], [], [(0, 0, half["w_in"])]))

    tap_shapes = [(KA, A // N_CHIPS), (KB, A // N_CHIPS), (KF, 2 * F // N_CHIPS)]
    taps = _allgather_small("allgather_taps", _pack([conv_a_w[0], conv_b_w[0], conv_ffn_w[0]]))
    conv_a_f, conv_b_f, conv_ffn_f = _gather_channels(taps, tap_shapes)

    def rms_prologue(rows, row_r, vec_r, ro_r, ao_r):
        h = row_r[0][rows, :]
        hn = (h * _rms_stats(h) * vec_r[0][...]).astype(BF16)
        ro_r[0][rows, :] = hn
        return [hn]

    def cast_prologue(rows, row_r, vec_r, ro_r, ao_r):
        hb = row_r[0][rows, :].astype(BF16)
        ro_r[0][rows, :] = hb
        return [hb]

    plain = lambda accs, tile_r, cv_r: [accs[0]]
    residual = lambda accs, tile_r, cv_r: [tile_r[0][...] + accs[0]]

    (z, hn1), (w_out_t, w_up_t) = _rows_mm(
        "in_proj", S, TM, 5 * A, _mm_tile(5 * A // N_CHIPS, D), row_ins=[x2], vec_ins=[norm_mix_g],
        weights=[(w_in3, "nn3")], tile_outs=[BF16], row_outs=[(D, BF16)], prologue=rms_prologue, epilogue=plain,
        side=_gather_side([buf["w_out"], buf["w_up"]], [(0, 0, half["w_out"]), (1, 0, up_a)], []))
    (a1, cat), (w_out3, w_up_t) = _mixer_fwd(
        z, conv_a_f, conv_a_b, ln_a_g, ln_a_b, conv_b_f, S, TE, A,
        side=_gather_side([w_out_t, w_up_t], [(1, up_a, half["w_up"] - up_a)],
                          [(0, 0, half["w_out"]), (1, 0, up_a)]))
    w_out_f = w_out3.reshape(2 * A, D)
    (h1,), (w_up3, w_proj_t) = _rows_mm(
        "out_proj", S, TM, D, _mm_tile(D, 2 * A), row_ins=[cat], weights=[(w_out_f, "nn2")],
        tile_ins=[x2], tile_outs=[F32], epilogue=residual,
        side=_gather_side([w_up_t, buf["w_ple_proj"]], [(1, 0, half["w_ple_proj"])],
                          [(0, up_a, half["w_up"] - up_a)]))
    (u0, hn2), (w_down_t, w_gate_t, w_proj3) = _rows_mm(
        "up_proj", S, TM, 2 * F, TC, row_ins=[h1], vec_ins=[norm_ffn_g],
        weights=[(w_up3, "nn3")], tile_outs=[BF16], row_outs=[(D, BF16)],
        prologue=rms_prologue, epilogue=plain, place=ffn_place,
        side=_gather_side([buf["w_down"], buf["w_ple_gate"], w_proj_t],
                          [(0, 0, half["w_down"]), (1, 0, half["w_ple_gate"])], [(2, 0, half["w_ple_proj"])]))
    (act,), (w_down3, w_gate3) = _ffn_act(
        u0, conv_ffn_f, S, TM, F, TC,
        side=_gather_side([w_down_t, w_gate_t], [], [(0, 0, half["w_down"]), (1, 0, half["w_ple_gate"])]))
    w_down_f = w_down3.reshape(F, D)
    w_gate_f = w_gate3.reshape(D, D)
    (h2,) = _rows_mm("down_proj", S, TM, D, _mm_tile(D, F), row_ins=[act], weights=[(w_down_f, "nn2")],
                     tile_ins=[h1], tile_outs=[F32], epilogue=residual)

    def ple_prologue(rows, row_r, vec_r, ro_r, ao_r):
        hb = row_r[0][rows, :].astype(BF16)
        pb = row_r[1][rows, :].astype(BF16)
        ro_r[0][rows, :] = hb
        ro_r[1][rows, :] = pb
        return [hb, pb]

    def ple_epilogue(accs, tile_r, cv_r):
        gate = _sigmoid(accs[0] + cv_r[0][...])
        return [tile_r[0][...] + accs[1] * gate, gate, accs[1]]

    h3, gate, pp, h2b, pb = _rows_mm(
        "ple_fwd", S, TM, D, _mm_tile(D // N_CHIPS, D), row_ins=[h2, p2], colvec_ins=[b_ple_gate],
        weights=[(w_gate_f, "nn2"), (w_proj3, "nn3")], tile_ins=[h2], tile_outs=[F32, BF16, BF16],
        row_outs=[(D, BF16), (P, BF16)], prologue=ple_prologue, epilogue=ple_epilogue)
    loss_part, g_norm_final, dh3, dpre, dpp, g_b_gate = _loss_head(h3, t2, gfin, pp, gate, S, TE, D)

    TK = _pick(S, (1024, 512, 256, 128))
    wt = lambda n: _pick(n, (1408, 1280, 1024, 512, 256, 128))
    chip_sums, from_chips = {}, {}

    def to_sibling(parts):
        ns = list(parts)
        halves = [parts[n].reshape(N_CHIPS, 2, big[n].shape[0] // 2, big[n].shape[1]) for n in ns]
        return ns, halves, _pair_exchange(halves)

    def to_chips(ns, halves, from_sibling):
        sums = [_add_pair("pair_sum_" + n, h, r, ci) for n, h, r in zip(ns, halves, from_sibling)]
        for n, (s, _) in zip(ns, sums):
            chip_sums[n] = s
        return ns, _chip_exchange([b for _, b in sums])

    def landed(ns, side_outs):
        for n, r in zip(ns, side_outs):
            from_chips[n] = r

    ns, halves, side = to_sibling(dict(
        w_ple_gate=_tn_mm("dw_ple_gate", h2b, dpre, wt(D), wt(D), TK),
        w_ple_proj=_tn_mm("dw_ple_proj", pb, dpp, wt(P), wt(D // N_CHIPS), TK, cols_per_chip=D // N_CHIPS)))
    (dh2,), got = _rows_mm("ple_bwd", S, TM, D, _mm_tile(D, D), row_ins=[dpre], weights=[(w_gate_f, "nt2")],
                           tile_ins=[dh3], tile_outs=[F32], epilogue=residual, side=side)
    ple_ns, ple_chips = to_chips(ns, halves, got)
    (dact, dh2b), got = _rows_mm("down_bwd", S, TM, F, _mm_tile(F, D), row_ins=[dh2], weights=[(w_down_f, "nt2")],
                                 tile_outs=[BF16], row_outs=[(D, BF16)], prologue=cast_prologue, epilogue=plain,
                                 side=ple_chips)
    landed(ple_ns, got)
    ns, halves, side = to_sibling(dict(w_down=_tn_mm("dw_down", act, dh2b, wt(F), wt(D), TK)))
    (du0, g_conv_gate, g_conv_up, dh1, dh1b, g_norm_ffn), got = _ffn_up_bwd(
        u0, dact, conv_ffn_f, w_up3, h1, dh2, norm_ffn_g, S, TE, F, TC, side=side)
    down_ns, down_chips = to_chips(ns, halves, got)
    g_conv_ffn = jnp.concatenate([g_conv_gate, g_conv_up], axis=1)
    ns, halves, side = to_sibling(dict(
        w_up=_tn_mm("dw_up", hn2, du0, wt(D), TC, TK, cols_per_chip=2 * F // N_CHIPS, place=ffn_place),
        w_out=_tn_mm("dw_out", cat, dh1b, wt(2 * A), wt(D), TK)))
    (dcat,), got = _rows_mm("out_bwd", S, TM, 2 * A, _mm_tile(2 * A, D), row_ins=[dh1b],
                            weights=[(w_out_f, "nt2")], tile_outs=[BF16], epilogue=plain, side=side)
    up_ns, up_chips = to_chips(ns, halves, got)
    da1, ln_sums = _mixer_bwd_ln(dcat, a1, ln_a_g, ln_a_b, S, TE, A)
    (dz, g_conv_a, g_conv_b), got = _mixer_bwd_conv(z, dcat, da1, conv_a_f, conv_b_f, S, TE, A,
                                                    side=_both(down_chips, up_chips))
    landed(down_ns + up_ns, got)
    ns, halves, side = to_sibling(dict(
        w_in=_tn_mm("dw_in", hn1, dz, wt(D), wt(5 * A // N_CHIPS), TK, cols_per_chip=5 * A // N_CHIPS)))
    ns, side = to_chips(ns, halves, _comm_only("grads_exchange_pairs_in", side))

    def in_bwd_epilogue(acc, rows, row_r, vec_r, ro_r, ao_r):
        dh, dg = _rms_bwd(row_r[0][rows, :], vec_r[0][...], acc)
        ro_r[0][rows, :] = row_r[1][rows, :] + dh
        ao_r[0][...] += dg

    (dx, g_norm_mix), got = _kloop_mm(
        "in_bwd", S, TM, dz, w_in3, _mm_tile(5 * A // N_CHIPS, D), row_ins=[x2, dh1],
        vec_ins=[norm_mix_g], row_outs=[(D, F32)], acc_outs=[(1, D)], epilogue=in_bwd_epilogue, side=side)
    landed(ns, got)

    reduced = _share_halves([_add_chips("chip_sum_" + n, chip_sums[n], from_chips[n], chip, ci) for n in names])
    moments = dict(w_in=(m_w_in, v_w_in), w_out=(m_w_out, v_w_out), w_up=(m_w_up, v_w_up),
                   w_down=(m_w_down, v_w_down), w_ple_gate=(m_w_ple_gate, v_w_ple_gate),
                   w_ple_proj=(m_w_ple_proj, v_w_ple_proj))
    grads, deltas, new_m, new_v = {}, {}, {}, {}
    for n, g in zip(names, reduced):
        g = g.reshape(big[n].shape)
        d_, m_, v_ = _adamw("adamw_" + n, big[n], g, moments[n][0][0], moments[n][1][0])
        grads[n], deltas[n], new_m[n], new_v[n] = g[None], d_[None], m_[None], v_[None]

    small = ["norm_mix_g", "conv_a_w", "conv_a_b", "ln_a_g", "ln_a_b", "conv_b_w", "norm_ffn_g",
             "conv_ffn_w", "b_ple_gate", "norm_final_g"]
    small_part = [g_norm_mix, g_conv_a, ln_sums[2:3], ln_sums[0:1], ln_sums[1:2], g_conv_b, g_norm_ffn,
                  g_conv_ffn, g_b_gate, g_norm_final]
    full_shapes = [a.shape for a in small_part]
    summed = _sum_devices("small_grads_sum", _allgather_small("allgather_small_grads", _pack(small_part)))
    small_g = dict(zip(small, _unpack(summed, full_shapes)))
    for n, width in (("conv_a_w", A), ("conv_b_w", A), ("conv_ffn_w", 2 * F)):
        small_g[n] = lax.dynamic_slice_in_dim(small_g[n], chip * (width // N_CHIPS), width // N_CHIPS, axis=1)
    small_w = dict(norm_mix_g=(norm_mix_g, m_norm_mix_g, v_norm_mix_g), conv_a_w=(conv_a_w, m_conv_a_w, v_conv_a_w),
                   conv_a_b=(conv_a_b, m_conv_a_b, v_conv_a_b), ln_a_g=(ln_a_g, m_ln_a_g, v_ln_a_g),
                   ln_a_b=(ln_a_b, m_ln_a_b, v_ln_a_b), conv_b_w=(conv_b_w, m_conv_b_w, v_conv_b_w),
                   norm_ffn_g=(norm_ffn_g, m_norm_ffn_g, v_norm_ffn_g),
                   conv_ffn_w=(conv_ffn_w, m_conv_ffn_w, v_conv_ffn_w),
                   b_ple_gate=(b_ple_gate, m_b_ple_gate, v_b_ple_gate),
                   norm_final_g=(norm_final_g, m_norm_final_g, v_norm_final_g))
    out_shapes = [small_w[n][0].shape for n in small]
    packed_g = _pack([small_g[n] for n in small])
    packed = [_pack([small_w[n][k] for n in small]) for k in range(3)]
    d_s, m_s, v_s = _adamw("adamw_small", packed[0], packed_g, packed[1], packed[2])
    for n, g, d_, m_, v_ in zip(small, _unpack(packed_g, out_shapes), _unpack(d_s, out_shapes),
                                _unpack(m_s, out_shapes), _unpack(v_s, out_shapes)):
        grads[n], deltas[n], new_m[n], new_v[n] = g, d_, m_, v_

    order = ["norm_mix_g", "w_in", "conv_a_w", "conv_a_b", "ln_a_g", "ln_a_b", "conv_b_w", "w_out", "norm_ffn_g",
             "w_up", "conv_ffn_w", "w_down", "w_ple_gate", "b_ple_gate", "w_ple_proj", "norm_final_g"]
    loss = lax.psum(loss_part[0, 0], ("x", "y", "c"))
    return (loss, dx.reshape(x.shape), *[grads[n] for n in order], *[deltas[n] for n in order],
            *[new_m[n] for n in order], *[new_v[n] for n in order])
```

```python
from typing import Callable, NamedTuple

import jax
import jax.numpy as jnp
from jax import lax
from jax.experimental import pallas as pl
from jax.experimental.pallas import tpu as pltpu

F32 = jnp.float32
BF16 = jnp.bfloat16
MESH = pl.DeviceIdType.MESH
ANY = pl.BlockSpec(memory_space=pl.ANY)

EPS = 1e-6
ADAM_LR = 0.001
ADAM_B1 = 0.9
ADAM_B2 = 0.999
ADAM_EPS = 1e-08
ADAM_WD = 0.01
ADAM_STEP = 10

N_CHIPS = 4
N_DEV = 8
LANES = 128
SUBLANES = 8
PACK_ALIGN = LANES * SUBLANES
ROW_CHUNK = 32
VMEM_CAP = 60 * 1024 * 1024
VMEM_SLACK = 6 * 1024 * 1024


def _pick(n, cands):
    for c in cands:
        if n % c == 0:
            return c
    raise ValueError(f"no tile of {cands} divides {n}")


def _nbytes(shape, dtype):
    n = 1
    for s in shape:
        if s is not None:
            n *= s
    return n * jnp.dtype(dtype).itemsize


def _params(sem, blocks, scratch=(), temps=()):
    est = (2 * sum(_nbytes(s, d) for s, d in blocks) + sum(_nbytes(s, d) for s, d in scratch)
           + sum(_nbytes(s, d) for s, d in temps))
    return pltpu.CompilerParams(dimension_semantics=sem,
                                vmem_limit_bytes=min(est + VMEM_SLACK, VMEM_CAP))


def _sigmoid(x):
    return 1.0 / (1.0 + jnp.exp(-x))


def _rsum(x):
    return jnp.sum(x, axis=0, keepdims=True)


class _Side(NamedTuple):
    ins: list
    out_shapes: list
    n_sems: int
    start: Callable
    wait: Callable
    aliased: bool = False


def _call(body, side, *, name, grid, in_specs, out_specs, out_shape, scratch, params, args):
    vmem = [pltpu.VMEM(s, d) for s, d in scratch]
    if side is None:
        outs = pl.pallas_call(body, name=name, grid=grid, in_specs=in_specs, out_specs=out_specs,
                              out_shape=out_shape, scratch_shapes=vmem, compiler_params=params)(*args)
        return list(outs), []
    n_in, n_out, n_sc = len(in_specs), len(out_specs), len(scratch)
    ns_in, ns_out = len(side.ins), len(side.out_shapes)

    def carrier(*refs):
        pos = [0]
        def take(n):
            pos[0] += n
            return refs[pos[0] - n:pos[0]]
        ins, s_ins, outs, s_outs, scr = take(n_in), take(ns_in), take(n_out), take(ns_out), take(n_sc)
        send_sems, recv_sems = take(2)
        first = last = None
        for axis, extent in enumerate(grid):
            at_start, at_end = pl.program_id(axis) == 0, pl.program_id(axis) == extent - 1
            first = at_start if first is None else first & at_start
            last = at_end if last is None else last & at_end

        @pl.when(first)
        def _():
            side.start(s_ins, s_outs, send_sems, recv_sems, 0)
        body(*ins, *outs, *scr)

        @pl.when(last)
        def _():
            side.wait(s_ins, s_outs, send_sems, recv_sems, 0)

    outs = pl.pallas_call(
        carrier, name=name, grid=grid, in_specs=list(in_specs) + [ANY] * ns_in,
        out_specs=list(out_specs) + [ANY] * ns_out, out_shape=list(out_shape) + list(side.out_shapes),
        scratch_shapes=vmem + [pltpu.SemaphoreType.DMA((side.n_sems,)), pltpu.SemaphoreType.DMA((side.n_sems,))],
        input_output_aliases={n_in + i: n_out + i for i in range(ns_in)} if side.aliased else {},
        compiler_params=params)(*args, *side.ins)
    return list(outs[:n_out]), list(outs[n_out:])


def _comm_only(name, side):
    n_in = len(side.ins)

    def body(*refs):
        ins, outs = refs[:n_in], refs[n_in:n_in + len(side.out_shapes)]
        send_sems, recv_sems = refs[n_in + len(side.out_shapes):]
        side.start(ins, outs, send_sems, recv_sems, 0)
        side.wait(ins, outs, send_sems, recv_sems, 0)

    return pl.pallas_call(
        body, name=name, out_shape=list(side.out_shapes), in_specs=[ANY] * n_in,
        out_specs=[ANY] * len(side.out_shapes),
        scratch_shapes=[pltpu.SemaphoreType.DMA((side.n_sems,)), pltpu.SemaphoreType.DMA((side.n_sems,))],
        input_output_aliases={i: i for i in range(n_in)} if side.aliased else {},
    )(*side.ins)


def _rms_stats(x):
    return lax.rsqrt(jnp.mean(x * x, axis=-1, keepdims=True) + EPS)


def _rms_bwd(h, g, dout):
    r = _rms_stats(h)
    n = h * r
    dn = dout * g
    dh = r * (dn - n * jnp.mean(dn * n, axis=-1, keepdims=True))
    return dh, _rsum(dout * n)


def _identity(t):
    return t


def _chip_major(nb, place=_identity):
    return lambda i, j: (place(j) // nb, 0, place(j) % nb)


def _rows_mm(name, S, TM, N, TN, *, row_ins, vec_ins=(), colvec_ins=(), weights, tile_ins=(),
             tile_outs, row_outs=(), acc_outs=(), prologue=None, epilogue, place=_identity, side=None):
    nI, nJ = S // TM, N // TN
    n_row, n_vec, n_cv, n_w, n_tile = len(row_ins), len(vec_ins), len(colvec_ins), len(weights), len(tile_ins)
    n_to, n_ro, n_ao = len(tile_outs), len(row_outs), len(acc_outs)

    in_specs, blocks, scratch, ks = [], [], [], []
    for a in row_ins:
        in_specs.append(pl.BlockSpec((TM, a.shape[1]), lambda i, j: (i, 0)))
        blocks.append(((TM, a.shape[1]), a.dtype))
    for a in vec_ins:
        in_specs.append(pl.BlockSpec(a.shape, lambda i, j: (0, 0)))
        blocks.append((a.shape, a.dtype))
    for a in colvec_ins:
        in_specs.append(pl.BlockSpec((1, TN), lambda i, j: (0, j)))
        blocks.append(((1, TN), a.dtype))
    for w, mode in weights:
        if mode == "nn2":
            k = w.shape[0]
            in_specs.append(pl.BlockSpec((k, TN), lambda i, j: (0, j)))
        elif mode == "nn3":
            k = w.shape[1]
            in_specs.append(pl.BlockSpec((None, k, TN), _chip_major(w.shape[2] // TN, place)))
        else:
            k = w.shape[1]
            in_specs.append(pl.BlockSpec((TN, k), lambda i, j: (j, 0)))
        ks.append(k)
        blocks.append(((k, TN), BF16))
        if prologue is not None:
            scratch.append(((TM, k), BF16))
    for a in tile_ins:
        in_specs.append(pl.BlockSpec((TM, TN), lambda i, j: (i, j)))
        blocks.append(((TM, TN), a.dtype))

    out_shape, out_specs = [], []
    for dt in tile_outs:
        out_shape.append(jax.ShapeDtypeStruct((S, N), dt))
        out_specs.append(pl.BlockSpec((TM, TN), lambda i, j: (i, j)))
        blocks.append(((TM, TN), dt))
    for width, dt in row_outs:
        out_shape.append(jax.ShapeDtypeStruct((S, width), dt))
        out_specs.append(pl.BlockSpec((TM, width), lambda i, j: (i, 0)))
        blocks.append(((TM, width), dt))
    for rows, width in acc_outs:
        out_shape.append(jax.ShapeDtypeStruct((rows, width), F32))
        out_specs.append(pl.BlockSpec((rows, width), lambda i, j: (0, 0)))
        blocks.append(((rows, width), F32))

    modes = [m for _, m in weights]

    def body(*refs):
        pos = 0
        def take(n):
            nonlocal pos
            out = refs[pos:pos + n]
            pos += n
            return out
        row_r, vec_r, cv_r, w_r, tile_r = take(n_row), take(n_vec), take(n_cv), take(n_w), take(n_tile)
        to_r, ro_r, ao_r, a_sc = take(n_to), take(n_ro), take(n_ao), take(len(scratch))
        i, j = pl.program_id(0), pl.program_id(1)

        if prologue is None:
            a_sc = row_r[:n_w]
        else:
            @pl.when(j == 0)
            def _():
                if n_ao:
                    @pl.when(i == 0)
                    def _():
                        for r in ao_r:
                            r[...] = jnp.zeros_like(r)

                def chunk(ci, carry):
                    rows = pl.ds(pl.multiple_of(ci * ROW_CHUNK, ROW_CHUNK), ROW_CHUNK)
                    for sc, a in zip(a_sc, prologue(rows, row_r, vec_r, ro_r, ao_r)):
                        sc[rows, :] = a
                    return carry
                lax.fori_loop(0, TM // ROW_CHUNK, chunk, 0)

        accs = []
        for w_ref, sc, mode in zip(w_r, a_sc, modes):
            if mode == "nt2":
                accs.append(lax.dot_general(sc[...], w_ref[...], (((1,), (1,)), ((), ())),
                                            preferred_element_type=F32))
            else:
                accs.append(jnp.dot(sc[...], w_ref[...], preferred_element_type=F32))
        outs = epilogue(accs, tile_r, cv_r)
        for r, o in zip(to_r, outs):
            r[...] = o.astype(r.dtype)

    outs, side_outs = _call(
        body, side, name=name, grid=(nI, nJ), in_specs=in_specs, out_specs=out_specs, out_shape=out_shape,
        scratch=scratch, params=_params(("arbitrary", "arbitrary"), blocks, scratch, temps=[((TM, TN), F32)] * 3),
        args=[*row_ins, *vec_ins, *colvec_ins, *[w for w, _ in weights], *tile_ins])
    return outs if side is None else (outs, side_outs)


def _kloop_mm(name, S, TM, a, w3, TK, *, row_ins, vec_ins, row_outs, acc_outs, epilogue, place=_identity,
              side=None):
    _, N, Ks = w3.shape
    nb = Ks // TK
    nK = N_CHIPS * nb
    n_row, n_vec, n_ro, n_ao = len(row_ins), len(vec_ins), len(row_outs), len(acc_outs)

    in_specs = [pl.BlockSpec((TM, TK), lambda i, k: (i, k)),
                pl.BlockSpec((None, N, TK), _chip_major(nb, place))]
    blocks = [((TM, TK), BF16), ((N, TK), BF16)]
    for r in row_ins:
        in_specs.append(pl.BlockSpec((TM, r.shape[1]), lambda i, k: (i, 0)))
        blocks.append(((TM, r.shape[1]), r.dtype))
    for v in vec_ins:
        in_specs.append(pl.BlockSpec(v.shape, lambda i, k: (0, 0)))
        blocks.append((v.shape, v.dtype))
    out_shape, out_specs = [], []
    for width, dt in row_outs:
        out_shape.append(jax.ShapeDtypeStruct((S, width), dt))
        out_specs.append(pl.BlockSpec((TM, width), lambda i, k: (i, 0)))
        blocks.append(((TM, width), dt))
    for rows, width in acc_outs:
        out_shape.append(jax.ShapeDtypeStruct((rows, width), F32))
        out_specs.append(pl.BlockSpec((rows, width), lambda i, k: (0, 0)))
        blocks.append(((rows, width), F32))
    scratch = [((TM, N), F32)]

    def body(*refs):
        a_ref, w_ref = refs[0], refs[1]
        row_r = refs[2:2 + n_row]
        vec_r = refs[2 + n_row:2 + n_row + n_vec]
        pos = 2 + n_row + n_vec
        ro_r = refs[pos:pos + n_ro]
        ao_r = refs[pos + n_ro:pos + n_ro + n_ao]
        acc_sc = refs[pos + n_ro + n_ao]
        i, k = pl.program_id(0), pl.program_id(1)
        @pl.when(k == 0)
        def _():
            acc_sc[...] = jnp.zeros_like(acc_sc)
        acc_sc[...] += lax.dot_general(a_ref[...], w_ref[...], (((1,), (1,)), ((), ())),
                                       preferred_element_type=F32)

        @pl.when(k == nK - 1)
        def _():
            @pl.when(i == 0)
            def _():
                for r in ao_r:
                    r[...] = jnp.zeros_like(r)

            def chunk(ci, carry):
                rows = pl.ds(pl.multiple_of(ci * ROW_CHUNK, ROW_CHUNK), ROW_CHUNK)
                epilogue(acc_sc[rows, :], rows, row_r, vec_r, ro_r, ao_r)
                return carry
            lax.fori_loop(0, TM // ROW_CHUNK, chunk, 0)

    outs, side_outs = _call(
        body, side, name=name, grid=(S // TM, nK), in_specs=in_specs, out_specs=out_specs, out_shape=out_shape,
        scratch=scratch, params=_params(("arbitrary", "arbitrary"), blocks, scratch, temps=[((TM, N), F32)]),
        args=[a, w3, *row_ins, *vec_ins])
    return outs if side is None else (outs, side_outs)


def _tn_mm(name, a, b, TMw, TNw, TK, cols_per_chip=None, place=_identity):
    S, M = a.shape
    N = b.shape[1]
    nK = S // TK
    if cols_per_chip is None:
        out_shape = jax.ShapeDtypeStruct((M, N), F32)
        out_spec = pl.BlockSpec((TMw, TNw), lambda i, j, k: (i, j))
    else:
        nb = cols_per_chip // TNw
        out_shape = jax.ShapeDtypeStruct((N_CHIPS, M, cols_per_chip), F32)
        out_spec = pl.BlockSpec((None, TMw, TNw), lambda i, j, k: (place(j) // nb, i, place(j) % nb))

    def body(a_ref, b_ref, o_ref):
        @pl.when(pl.program_id(2) == 0)
        def _():
            o_ref[...] = jnp.zeros_like(o_ref)
        o_ref[...] += lax.dot_general(a_ref[...], b_ref[...], (((0,), (0,)), ((), ())),
                                      preferred_element_type=F32)

    blocks = [((TK, TMw), BF16), ((TK, TNw), BF16), ((TMw, TNw), F32)]
    return pl.pallas_call(
        body, name=name, grid=(M // TMw, N // TNw, nK),
        in_specs=[pl.BlockSpec((TK, TMw), lambda i, j, k: (k, i)),
                  pl.BlockSpec((TK, TNw), lambda i, j, k: (k, j))],
        out_specs=out_spec, out_shape=out_shape,
        compiler_params=_params(("arbitrary", "arbitrary", "arbitrary"), blocks,
                                temps=[((TMw, TNw), F32), ((TK, TMw), BF16)]),
    )(a, b)


def _prev_rows(TM, H, col):
    return lambda i: (jnp.maximum(i * (TM // H) - 1, 0), col)


def _next_rows(S, TM, H, col):
    return lambda i: (jnp.minimum((i + 1) * (TM // H), S // H - 1), col)


def _taps_causal(ext_ref, w_ref, K, H, TM, cs):
    acc = None
    for k in range(K):
        term = ext_ref[pl.ds(H - (K - 1) + k, TM), cs] * w_ref[pl.ds(k, 1), cs]
        acc = term if acc is None else acc + term
    return acc


def _taps_anticausal(ext_ref, w_ref, K, TM, cs):
    acc = None
    for k in range(K):
        term = ext_ref[pl.ds(K - 1 - k, TM), cs] * w_ref[pl.ds(k, 1), cs]
        acc = term if acc is None else acc + term
    return acc


def _tap_grads(ext_ref, g, K, H, TM, cs):
    return [_rsum(ext_ref[pl.ds(H - (K - 1) + k, TM), cs] * g) for k in range(K)]


def _shift_copies(ext_ref, shifted, cs):
    n = shifted.shape[1]
    for r in range(1, SUBLANES):
        shifted[r - 1] = ext_ref[pl.ds(r, n), cs]


def _rows_at(ext_ref, shifted, start, n, cs):
    q, r = divmod(start, SUBLANES)
    if r == 0:
        return ext_ref[pl.ds(start, n), cs]
    return shifted[r - 1, pl.ds(SUBLANES * q, n), :]


def _mixer_fwd(z, conv_a_w, conv_a_b, ln_g, ln_b, conv_b_w, S, TM, A, side=None):
    H = 32
    KA, KB = conv_a_w.shape[0], conv_b_w.shape[0]
    n_chunks = A // LANES
    RB = _pick(TM, (64, 32))

    def body(zc_ref, zh_ref, wa_ref, ba_ref, g_ref, b_ref, wb_ref, a1_ref, cat_ref, ext_a, ext_b, shifted):
        i = pl.program_id(0)
        live = (i > 0).astype(F32)
        zc = zc_ref[...].astype(F32)
        zh = zh_ref[...].astype(F32) * live
        ext_a[pl.ds(0, H), :] = zh[:, 0:A] * _sigmoid(zh[:, A:2 * A])
        ext_a[pl.ds(H, TM), :] = zc[:, 0:A] * _sigmoid(zc[:, A:2 * A])
        ext_b[pl.ds(0, H), :] = zh[:, 3 * A:4 * A] * zh[:, 4 * A:5 * A]
        ext_b[pl.ds(H, TM), :] = zc[:, 3 * A:4 * A] * zc[:, 4 * A:5 * A]

        def chunk(c, carry):
            cs = pl.ds(pl.multiple_of(c * LANES, LANES), LANES)
            _shift_copies(ext_a, shifted, cs)
            for r0 in range(0, TM, RB):
                acc = None
                for k in range(KA):
                    term = _rows_at(ext_a, shifted, H - (KA - 1) + k + r0, RB, cs) * wa_ref[pl.ds(k, 1), cs]
                    acc = term if acc is None else acc + term
                a1_ref[pl.ds(r0, RB), cs] = acc + ba_ref[:, cs]
            return carry
        lax.fori_loop(0, n_chunks, chunk, 0)

        a1 = a1_ref[...]
        mu = jnp.mean(a1, axis=-1, keepdims=True)
        d = a1 - mu
        var = jnp.mean(d * d, axis=-1, keepdims=True)
        a2 = d * lax.rsqrt(var + EPS) * g_ref[...] + b_ref[...]
        cat_ref[:, 0:A] = (a2 * _sigmoid(a2)).astype(BF16)
        cbc = _taps_causal(ext_b, wb_ref, KB, H, TM, slice(None))
        cat_ref[:, A:2 * A] = (zc[:, 2 * A:3 * A] * cbc).astype(BF16)

    blocks = [((TM, 5 * A), BF16), ((H, 5 * A), BF16), ((KA, A), F32), ((KB, A), F32),
              ((TM, A), F32), ((TM, 2 * A), BF16)]
    scratch = [((H + TM, A), F32), ((H + TM, A), F32), ((SUBLANES - 1, H + TM - SUBLANES, LANES), F32)]
    vec = lambda r: pl.BlockSpec((r, A), lambda i: (0, 0))
    outs, side_outs = _call(
        body, side, name="mixer_fwd", grid=(S // TM,),
        in_specs=[pl.BlockSpec((TM, 5 * A), lambda i: (i, 0)),
                  pl.BlockSpec((H, 5 * A), _prev_rows(TM, H, 0)),
                  vec(KA), vec(1), vec(1), vec(1), vec(KB)],
        out_specs=[pl.BlockSpec((TM, A), lambda i: (i, 0)), pl.BlockSpec((TM, 2 * A), lambda i: (i, 0))],
        out_shape=[jax.ShapeDtypeStruct((S, A), F32), jax.ShapeDtypeStruct((S, 2 * A), BF16)],
        scratch=scratch,
        params=_params(("arbitrary",), blocks, scratch, temps=[((TM, 5 * A), F32)] * 2 + [((TM, A), F32)] * 10),
        args=[z, z, conv_a_w, conv_a_b, ln_g, ln_b, conv_b_w])
    return outs if side is None else (outs, side_outs)


def _pair_tile(nF):
    return lambda t: (t % 2) * nF + t // 2


FFN_ROWS = 16


def _bcast_taps(w_ref, K, lanes):
    return [jnp.broadcast_to(w_ref[pl.ds(k, 1), lanes], (FFN_ROWS, LANES)) for k in range(K)]


def _ffn_act(u0, conv_w, S, TM, F, TC, side=None):
    H = 16
    K = conv_w.shape[0]
    nF = F // TC

    def body(uc_ref, uh_ref, wg_ref, wu_ref, o_ref, ext):
        live = (pl.program_id(0) > 0).astype(F32)
        ext[pl.ds(0, H), :] = uh_ref[...].astype(F32) * live
        ext[pl.ds(H, TM), :] = uc_ref[...].astype(F32)

        def lane_chunk(c, carry):
            lo = pl.ds(pl.multiple_of(c * LANES, LANES), LANES)
            lg, lu = lo, pl.ds(pl.multiple_of(TC + c * LANES, LANES), LANES)
            wg, wu = _bcast_taps(wg_ref, K, lo), _bcast_taps(wu_ref, K, lo)
            for r0 in range(0, TM, FFN_ROWS):
                g = u = None
                for k in range(K):
                    rows = pl.ds(H - (K - 1) + k + r0, FFN_ROWS)
                    tg, tu = ext[rows, lg] * wg[k], ext[rows, lu] * wu[k]
                    g, u = (tg, tu) if g is None else (g + tg, u + tu)
                o_ref[pl.ds(r0, FFN_ROWS), lo] = (g * _sigmoid(g) * u).astype(BF16)
            return carry
        lax.fori_loop(0, TC // LANES, lane_chunk, 0)

    blocks = [((TM, 2 * TC), BF16), ((H, 2 * TC), BF16), ((K, TC), F32), ((K, TC), F32), ((TM, TC), BF16)]
    scratch = [((H + TM, 2 * TC), F32)]
    outs, side_outs = _call(
        body, side, name="ffn_act", grid=(S // TM, nF),
        in_specs=[pl.BlockSpec((TM, 2 * TC), lambda i, j: (i, j)),
                  pl.BlockSpec((H, 2 * TC), lambda i, j: (jnp.maximum(i * (TM // H) - 1, 0), j)),
                  pl.BlockSpec((K, TC), lambda i, j: (0, j)),
                  pl.BlockSpec((K, TC), lambda i, j: (0, j + nF))],
        out_specs=[pl.BlockSpec((TM, TC), lambda i, j: (i, j))],
        out_shape=[jax.ShapeDtypeStruct((S, F), BF16)],
        scratch=scratch,
        params=_params(("arbitrary", "arbitrary"), blocks, scratch, temps=[((TM, 2 * TC), F32)]),
        args=[u0, u0, conv_w, conv_w])
    return outs if side is None else (outs, side_outs)


def _loss_head(h3, target, g_final, pp, gate, S, TM, D):
    def body(h_ref, t_ref, g_ref, pp_ref, gt_ref, loss_ref, dg_ref, dh_ref, dpre_ref, dpp_ref, db_ref):
        @pl.when(pl.program_id(0) == 0)
        def _():
            loss_ref[...] = jnp.zeros_like(loss_ref)
            dg_ref[...] = jnp.zeros_like(dg_ref)
            db_ref[...] = jnp.zeros_like(db_ref)
        h = h_ref[...]
        g = g_ref[...]
        r = _rms_stats(h)
        n = h * r
        diff = n * g - t_ref[...]
        loss_ref[...] += 0.5 * jnp.sum(jnp.mean(diff * diff, axis=-1, keepdims=True), axis=0, keepdims=True)
        dy = diff * (1.0 / D)
        dn = dy * g
        dh = r * (dn - n * jnp.mean(dn * n, axis=-1, keepdims=True))
        dh_ref[...] = dh
        dg_ref[...] += _rsum(dy * n)
        gt = gt_ref[...].astype(F32)
        dpre = dh * pp_ref[...].astype(F32) * gt * (1.0 - gt)
        dpre_ref[...] = dpre.astype(BF16)
        dpp_ref[...] = (dh * gt).astype(BF16)
        db_ref[...] += _rsum(dpre)

    blocks = [((TM, D), F32)] * 3 + [((TM, D), BF16)] * 4 + [((1, D), F32)] * 3
    row = pl.BlockSpec((TM, D), lambda i: (i, 0))
    vec = pl.BlockSpec((1, D), lambda i: (0, 0))
    return pl.pallas_call(
        body, name="loss_head", grid=(S // TM,),
        in_specs=[row, row, vec, row, row],
        out_specs=[pl.BlockSpec((1, 1), lambda i: (0, 0)), vec, row, row, row, vec],
        out_shape=[jax.ShapeDtypeStruct((1, 1), F32), jax.ShapeDtypeStruct((1, D), F32),
                   jax.ShapeDtypeStruct((S, D), F32), jax.ShapeDtypeStruct((S, D), BF16),
                   jax.ShapeDtypeStruct((S, D), BF16), jax.ShapeDtypeStruct((1, D), F32)],
        compiler_params=_params(("arbitrary",), blocks, temps=[((TM, D), F32)] * 10),
    )(h3, target, g_final, pp, gate)


def _ffn_bwd(u0, dact, conv_w, S, TM, F, TC, side=None):
    H = FFN_ROWS
    K = conv_w.shape[0]
    nF, nI = F // TC, S // TM

    def body(up_ref, uc_ref, un_ref, dc_ref, dn_ref, wg_ref, wu_ref, o_ref, dwg_ref, dwu_ref,
             ext_u, ext_d, ext_a):
        i = pl.program_id(1)
        @pl.when(i == 0)
        def _():
            dwg_ref[...] = jnp.zeros_like(dwg_ref)
            dwu_ref[...] = jnp.zeros_like(dwu_ref)
        last = (i < nI - 1).astype(F32)
        ext_u[pl.ds(0, H), :] = up_ref[...].astype(F32) * (i > 0).astype(F32)
        ext_u[pl.ds(H, TM), :] = uc_ref[...].astype(F32)
        ext_u[pl.ds(H + TM, H), :] = un_ref[...].astype(F32) * last
        ext_a[pl.ds(0, TM), :] = dc_ref[...].astype(F32)
        ext_a[pl.ds(TM, H), :] = dn_ref[...].astype(F32) * last

        def lane_chunk(c, carry):
            lo = pl.ds(pl.multiple_of(c * LANES, LANES), LANES)
            lg, lu = lo, pl.ds(pl.multiple_of(TC + c * LANES, LANES), LANES)
            wg, wu = _bcast_taps(wg_ref, K, lo), _bcast_taps(wu_ref, K, lo)
            sums_g, sums_u = [None] * K, [None] * K
            for r0 in range(0, TM + H, FFN_ROWS):
                xg = [ext_u[pl.ds(H - (K - 1) + k + r0, FFN_ROWS), lg] for k in range(K)]
                xu = [ext_u[pl.ds(H - (K - 1) + k + r0, FFN_ROWS), lu] for k in range(K)]
                g, u = xg[0] * wg[0], xu[0] * wu[0]
                for k in range(1, K):
                    g, u = g + xg[k] * wg[k], u + xu[k] * wu[k]
                da = ext_a[pl.ds(r0, FFN_ROWS), lo]
                s = _sigmoid(g)
                dg = da * u * s * (1.0 + g * (1.0 - s))
                du = da * g * s
                ext_d[pl.ds(r0, FFN_ROWS), lg] = dg
                ext_d[pl.ds(r0, FFN_ROWS), lu] = du
                if r0 < TM:
                    for k in range(K):
                        tg, tu = xg[k] * dg, xu[k] * du
                        sums_g[k] = tg if sums_g[k] is None else sums_g[k] + tg
                        sums_u[k] = tu if sums_u[k] is None else sums_u[k] + tu
            for k in range(K):
                dwg_ref[pl.ds(k, 1), lo] += _rsum(sums_g[k])
                dwu_ref[pl.ds(k, 1), lo] += _rsum(sums_u[k])
            for r0 in range(0, TM, FFN_ROWS):
                g = u = None
                for k in range(K):
                    rows = pl.ds(K - 1 - k + r0, FFN_ROWS)
                    tg, tu = ext_d[rows, lg] * wg[k], ext_d[rows, lu] * wu[k]
                    g, u = (tg, tu) if g is None else (g + tg, u + tu)
                o_ref[pl.ds(r0, FFN_ROWS), lg] = g.astype(BF16)
                o_ref[pl.ds(r0, FFN_ROWS), lu] = u.astype(BF16)
            return carry
        lax.fori_loop(0, TC // LANES, lane_chunk, 0)

    blocks = [((H, 2 * TC), BF16), ((TM, 2 * TC), BF16), ((H, 2 * TC), BF16), ((TM, TC), BF16), ((H, TC), BF16),
              ((K, TC), F32), ((K, TC), F32), ((TM, 2 * TC), BF16), ((K, TC), F32), ((K, TC), F32)]
    scratch = [((TM + 2 * H, 2 * TC), F32), ((TM + H, 2 * TC), F32), ((TM + H, TC), F32)]
    prev = lambda j, i: (jnp.maximum(i * (TM // H) - 1, 0), j)
    nxt = lambda j, i: (jnp.minimum((i + 1) * (TM // H), S // H - 1), j)
    taps_out = pl.BlockSpec((K, TC), lambda j, i: (0, j))
    outs, side_outs = _call(
        body, side, name="ffn_bwd", grid=(nF, nI),
        in_specs=[pl.BlockSpec((H, 2 * TC), prev), pl.BlockSpec((TM, 2 * TC), lambda j, i: (i, j)),
                  pl.BlockSpec((H, 2 * TC), nxt),
                  pl.BlockSpec((TM, TC), lambda j, i: (i, j)), pl.BlockSpec((H, TC), nxt),
                  pl.BlockSpec((K, TC), lambda j, i: (0, j)), pl.BlockSpec((K, TC), lambda j, i: (0, j + nF))],
        out_specs=[pl.BlockSpec((TM, 2 * TC), lambda j, i: (i, j)), taps_out, taps_out],
        out_shape=[jax.ShapeDtypeStruct((S, 2 * F), BF16), jax.ShapeDtypeStruct((K, F), F32),
                   jax.ShapeDtypeStruct((K, F), F32)],
        scratch=scratch,
        params=_params(("arbitrary", "arbitrary"), blocks, scratch, temps=[((TM, 2 * TC), F32)]),
        args=[u0, u0, u0, dact, dact, conv_w, conv_w])
    return outs if side is None else (outs, side_outs)


def _mixer_bwd_ln(dcat, a1, ln_g, ln_b, S, TM, A):
    def body(dc_ref, a1_ref, g_ref, b_ref, da1_ref, acc_ref):
        @pl.when(pl.program_id(0) == 0)
        def _():
            acc_ref[...] = jnp.zeros_like(acc_ref)
        a1 = a1_ref[...]
        g = g_ref[...]
        mu = jnp.mean(a1, axis=-1, keepdims=True)
        d = a1 - mu
        rstd = lax.rsqrt(jnp.mean(d * d, axis=-1, keepdims=True) + EPS)
        nh = d * rstd
        a2 = nh * g + b_ref[...]
        s = _sigmoid(a2)
        da2 = dc_ref[...].astype(F32) * s * (1.0 + a2 * (1.0 - s))
        dnh = da2 * g
        da1 = rstd * (dnh - jnp.mean(dnh, axis=-1, keepdims=True)
                      - nh * jnp.mean(dnh * nh, axis=-1, keepdims=True))
        da1_ref[...] = da1
        acc_ref[pl.ds(0, 1), :] += _rsum(da2 * nh)
        acc_ref[pl.ds(1, 1), :] += _rsum(da2)
        acc_ref[pl.ds(2, 1), :] += _rsum(da1)

    blocks = [((TM, A), BF16), ((TM, A), F32), ((TM, A), F32), ((4, A), F32)]
    return pl.pallas_call(
        body, name="mixer_bwd_ln", grid=(S // TM,),
        in_specs=[pl.BlockSpec((TM, A), lambda i: (i, 0)), pl.BlockSpec((TM, A), lambda i: (i, 0)),
                  pl.BlockSpec((1, A), lambda i: (0, 0)), pl.BlockSpec((1, A), lambda i: (0, 0))],
        out_specs=[pl.BlockSpec((TM, A), lambda i: (i, 0)), pl.BlockSpec((4, A), lambda i: (0, 0))],
        out_shape=[jax.ShapeDtypeStruct((S, A), F32), jax.ShapeDtypeStruct((4, A), F32)],
        compiler_params=_params(("arbitrary",), blocks, temps=[((TM, A), F32)] * 12),
    )(dcat, a1, ln_g, ln_b)


def _mixer_bwd_conv(z, dcat, da1, conv_a_w, conv_b_w, S, TM, A, side=None):
    H = 32
    KA, KB = conv_a_w.shape[0], conv_b_w.shape[0]
    nI = S // TM
    n_chunks = A // LANES
    RB = _pick(TM, (64, 32))

    def body(zc_ref, zp_ref, zn_ref, dbc_ref, dbn_ref, d1c_ref, d1n_ref, wa_ref, wb_ref,
             dz_ref, dwa_ref, dwb_ref, ext_a0, ext_d1, ext_cb, ext_dc, da0_sc, shifted_d, shifted_a):
        i = pl.program_id(0)
        @pl.when(i == 0)
        def _():
            dwa_ref[...] = jnp.zeros_like(dwa_ref)
            dwb_ref[...] = jnp.zeros_like(dwb_ref)
        first = (i > 0).astype(F32)
        last = (i < nI - 1).astype(F32)
        zc = zc_ref[...].astype(F32)
        zp = zp_ref[...].astype(F32) * first
        a_val, a_gate = zc[:, 0:A], zc[:, A:2 * A]
        b_gate, c_gate, b_h = zc[:, 2 * A:3 * A], zc[:, 3 * A:4 * A], zc[:, 4 * A:5 * A]
        sig = _sigmoid(a_gate)
        ext_a0[pl.ds(0, H), :] = zp[:, 0:A] * _sigmoid(zp[:, A:2 * A])
        ext_a0[pl.ds(H, TM), :] = a_val * sig
        ext_d1[pl.ds(0, TM), :] = d1c_ref[...]
        ext_d1[pl.ds(TM, H), :] = d1n_ref[...] * last
        ext_cb[pl.ds(0, H), :] = zp[:, 3 * A:4 * A] * zp[:, 4 * A:5 * A]
        ext_cb[pl.ds(H, TM), :] = c_gate * b_h
        dbx = dbc_ref[...].astype(F32)
        dcbc = dbx * b_gate
        ext_dc[pl.ds(0, TM), :] = dcbc
        ext_dc[pl.ds(TM, H), :] = dbn_ref[...].astype(F32) * zn_ref[...].astype(F32) * last

        def chunk(c, carry):
            cs = pl.ds(pl.multiple_of(c * LANES, LANES), LANES)
            _shift_copies(ext_d1, shifted_d, cs)
            _shift_copies(ext_a0, shifted_a, cs)
            for r0 in range(0, TM, RB):
                acc = None
                for k in range(KA):
                    term = _rows_at(ext_d1, shifted_d, KA - 1 - k + r0, RB, cs) * wa_ref[pl.ds(k, 1), cs]
                    acc = term if acc is None else acc + term
                da0_sc[pl.ds(r0, RB), cs] = acc
            for k in range(KA):
                acc = None
                for r0 in range(0, TM, RB):
                    term = (_rows_at(ext_a0, shifted_a, H - (KA - 1) + k + r0, RB, cs)
                            * ext_d1[pl.ds(r0, RB), cs])
                    acc = term if acc is None else acc + term
                dwa_ref[pl.ds(k, 1), cs] += _rsum(acc)
            return carry
        lax.fori_loop(0, n_chunks, chunk, 0)

        da0 = da0_sc[...]
        dz_ref[:, 0:A] = (da0 * sig).astype(BF16)
        dz_ref[:, A:2 * A] = (da0 * a_val * sig * (1.0 - sig)).astype(BF16)
        cbc = _taps_causal(ext_cb, wb_ref, KB, H, TM, slice(None))
        dz_ref[:, 2 * A:3 * A] = (dbx * cbc).astype(BF16)
        dcb = _taps_anticausal(ext_dc, wb_ref, KB, TM, slice(None))
        dz_ref[:, 3 * A:4 * A] = (dcb * b_h).astype(BF16)
        dz_ref[:, 4 * A:5 * A] = (dcb * c_gate).astype(BF16)
        grads = _tap_grads(ext_cb, dcbc, KB, H, TM, slice(None))
        for k in range(KB):
            dwb_ref[pl.ds(k, 1), :] += grads[k]

    blocks = [((TM, 5 * A), BF16), ((H, 5 * A), BF16), ((H, A), BF16), ((TM, A), BF16), ((H, A), BF16),
              ((TM, A), F32), ((H, A), F32), ((KA, A), F32), ((KB, A), F32),
              ((TM, 5 * A), BF16), ((KA, A), F32), ((KB, A), F32)]
    scratch = ([((H + TM, A), F32)] * 4 + [((TM, A), F32)]
               + [((SUBLANES - 1, H + TM - SUBLANES, LANES), F32)] * 2)
    vec = lambda r: pl.BlockSpec((r, A), lambda i: (0, 0))
    outs, side_outs = _call(
        body, side, name="mixer_bwd_conv", grid=(nI,),
        in_specs=[pl.BlockSpec((TM, 5 * A), lambda i: (i, 0)),
                  pl.BlockSpec((H, 5 * A), _prev_rows(TM, H, 0)),
                  pl.BlockSpec((H, A), _next_rows(S, TM, H, 2)),
                  pl.BlockSpec((TM, A), lambda i: (i, 1)),
                  pl.BlockSpec((H, A), _next_rows(S, TM, H, 1)),
                  pl.BlockSpec((TM, A), lambda i: (i, 0)),
                  pl.BlockSpec((H, A), _next_rows(S, TM, H, 0)),
                  vec(KA), vec(KB)],
        out_specs=[pl.BlockSpec((TM, 5 * A), lambda i: (i, 0)), vec(KA), vec(KB)],
        out_shape=[jax.ShapeDtypeStruct((S, 5 * A), BF16), jax.ShapeDtypeStruct((KA, A), F32),
                   jax.ShapeDtypeStruct((KB, A), F32)],
        scratch=scratch,
        params=_params(("arbitrary",), blocks, scratch, temps=[((TM, 5 * A), F32)] * 2 + [((TM, A), F32)] * 14),
        args=[z, z, z, dcat, dcat, da1, da1, conv_a_w, conv_b_w])
    return outs if side is None else (outs, side_outs)


def _row_tile(R):
    return _pick(R, (256, 128, 64, 32, 16, 8))


def _scalars(*vals):
    return jnp.stack([jnp.asarray(v, jnp.int32) for v in vals])


def _cast_into_gathered(name, w, chip):
    R, C = w.shape
    TR = _row_tile(R)

    def body(s_ref, w_ref, o_ref):
        o_ref[...] = w_ref[...].astype(BF16)

    grid_spec = pltpu.PrefetchScalarGridSpec(
        num_scalar_prefetch=1, grid=(R // TR,),
        in_specs=[pl.BlockSpec((TR, C), lambda r, s: (r, 0))],
        out_specs=pl.BlockSpec((None, TR, C), lambda r, s: (s[0], r, 0)))
    return pl.pallas_call(body, name=name, grid_spec=grid_spec,
                          out_shape=jax.ShapeDtypeStruct((N_CHIPS, R, C), BF16),
                          compiler_params=_params(("arbitrary",), [((TR, C), F32), ((TR, C), BF16)]),
                          )(_scalars(chip), w)


def _add_pair(name, dw, recv, c):
    _, _, Rh, C = dw.shape
    TR = _row_tile(Rh)

    def body(c_ref, a_ref, b_ref, o_ref, ob_ref):
        s = a_ref[...] + b_ref[...]
        o_ref[...] = s
        ob_ref[...] = s.astype(BF16)

    out_spec = pl.BlockSpec((None, TR, C), lambda k, r, c_ref: (k, r, 0))
    grid_spec = pltpu.PrefetchScalarGridSpec(
        num_scalar_prefetch=1, grid=(N_CHIPS, Rh // TR),
        in_specs=[pl.BlockSpec((None, None, TR, C), lambda k, r, c_ref: (k, c_ref[0], r, 0)),
                  pl.BlockSpec((None, TR, C), lambda k, r, c_ref: (k, r, 0))],
        out_specs=[out_spec, out_spec])
    return pl.pallas_call(body, name=name, grid_spec=grid_spec,
                          out_shape=[jax.ShapeDtypeStruct((N_CHIPS, Rh, C), F32),
                                     jax.ShapeDtypeStruct((N_CHIPS, Rh, C), BF16)],
                          compiler_params=_params(("arbitrary", "arbitrary"), [((TR, C), F32)] * 4),
                          )(_scalars(c), dw, recv)


def _add_chips(name, parts, recv, chip, c):
    _, Rh, C = parts.shape
    TR = _row_tile(Rh)

    def body(s_ref, p_ref, r_ref, o_ref):
        o_ref[...] = ((p_ref[...] + r_ref[0].astype(F32)) + r_ref[1].astype(F32)) + r_ref[2].astype(F32)

    grid_spec = pltpu.PrefetchScalarGridSpec(
        num_scalar_prefetch=1, grid=(Rh // TR,),
        in_specs=[pl.BlockSpec((None, TR, C), lambda r, s: (s[0], r, 0)),
                  pl.BlockSpec((N_CHIPS - 1, TR, C), lambda r, s: (0, r, 0))],
        out_specs=pl.BlockSpec((None, TR, C), lambda r, s: (s[1], r, 0)))
    return pl.pallas_call(body, name=name, grid_spec=grid_spec,
                          out_shape=jax.ShapeDtypeStruct((2, Rh, C), F32),
                          compiler_params=_params(("arbitrary",), [((N_CHIPS + 1, TR, C), F32)]),
                          )(_scalars(chip, c), parts, recv)


def _sum_devices(name, parts):
    _, R, C = parts.shape

    def body(p_ref, o_ref):
        acc = p_ref[0]
        for d in range(1, N_DEV):
            acc = acc + p_ref[d]
        o_ref[...] = acc

    return pl.pallas_call(body, name=name, out_shape=jax.ShapeDtypeStruct((R, C), F32),
                          in_specs=[pl.BlockSpec(memory_space=pltpu.VMEM)],
                          out_specs=pl.BlockSpec(memory_space=pltpu.VMEM))(parts)


def _adamw(name, w, g, m, v):
    R, C = w.shape
    TR = _pick(R, (128, 64, 32, 16, 8))
    c1 = 1.0 - ADAM_B1 ** ADAM_STEP
    c2 = 1.0 - ADAM_B2 ** ADAM_STEP

    def body(w_ref, g_ref, m_ref, v_ref, d_ref, nm_ref, nv_ref):
        g_ = g_ref[...]
        nm = ADAM_B1 * m_ref[...] + (1.0 - ADAM_B1) * g_
        nv = ADAM_B2 * v_ref[...] + (1.0 - ADAM_B2) * (g_ * g_)
        d_ref[...] = -ADAM_LR * ((nm / c1) / (jnp.sqrt(nv / c2) + ADAM_EPS) + ADAM_WD * w_ref[...])
        nm_ref[...] = nm
        nv_ref[...] = nv

    spec = pl.BlockSpec((TR, C), lambda r: (r, 0))
    shp = jax.ShapeDtypeStruct((R, C), F32)
    return pl.pallas_call(body, name=name, grid=(R // TR,), in_specs=[spec] * 4, out_specs=[spec] * 3,
                          out_shape=[shp] * 3,
                          compiler_params=_params(("arbitrary",), [((TR, C), F32)] * 7))(w, g, m, v)


def _place():
    x, y, c = lax.axis_index("x"), lax.axis_index("y"), lax.axis_index("c")
    others = [(1 - x, y), (x, 1 - y), (1 - x, 1 - y)]
    return x, y, c, others


def _allgather_small(name, block):
    R, C = block.shape

    def body(x_ref, out_ref, send_sems, recv_sems, local_sem):
        x, y, c, chips = _place()
        me, sibling = (x, y, c), (x, y, 1 - c)

        def rows(px, py, pc):
            return out_ref.at[4 * px + 2 * py + pc]

        def copy(k, blk, to, src=None):
            return pltpu.make_async_remote_copy(
                src_ref=rows(*blk) if src is None else src, dst_ref=rows(*blk),
                send_sem=send_sems.at[k], recv_sem=recv_sems.at[k], device_id=to, device_id_type=MESH)

        mine = pltpu.make_async_copy(x_ref, rows(*me), local_sem)
        mine.start()
        first = [copy(0, me, sibling, src=x_ref)]
        first += [copy(1 + j, me, (*chip, c), src=x_ref) for j, chip in enumerate(chips)]
        for cp in first:
            cp.start()
        passed = [copy(4 + j, (*chip, c), sibling) for j, chip in enumerate(chips)]
        for j, chip in enumerate(chips):
            copy(1 + j, (*chip, c), me).wait_recv()
            passed[j].start()
        copy(0, sibling, me).wait_recv()
        for j, chip in enumerate(chips):
            copy(4 + j, (*chip, 1 - c), me).wait_recv()
        for cp in first + passed:
            cp.wait_send()
        mine.wait()

    return pl.pallas_call(
        body, name=name, out_shape=jax.ShapeDtypeStruct((N_DEV, R, C), F32),
        in_specs=[pl.BlockSpec(memory_space=pltpu.VMEM)], out_specs=pl.BlockSpec(memory_space=pltpu.VMEM),
        scratch_shapes=[pltpu.SemaphoreType.DMA((7,)), pltpu.SemaphoreType.DMA((7,)), pltpu.SemaphoreType.DMA],
    )(block)


def _gather_side(bufs, across, within):
    def rows(ref, chip, half, piece):
        _, r0, n = piece
        return ref.at[2 * chip[0] + chip[1], pl.ds(half * (ref.shape[1] // 2) + r0, n)]

    def copies(ins, outs, send_sems, recv_sems, base):
        x, y, c, chips = _place()
        sibling = (x, y, 1 - c)
        pairs = []

        def add(k, src, dst, to, arrival):
            mk = lambda s, d, dev: pltpu.make_async_remote_copy(
                src_ref=s, dst_ref=d, send_sem=send_sems.at[base + k], recv_sem=recv_sems.at[base + k],
                device_id=dev, device_id_type=MESH)
            pairs.append((mk(src, dst, to), mk(arrival, arrival, (x, y, c))))

        for p, piece in enumerate(across):
            ref = outs[piece[0]]
            for j, chip in enumerate(chips):
                mine = rows(ref, (x, y), c, piece)
                add(3 * p + j, mine, mine, (*chip, c), rows(ref, chip, c, piece))
        for q, piece in enumerate(within):
            ref = outs[piece[0]]
            for j, chip in enumerate(chips):
                held = rows(ref, chip, c, piece)
                add(3 * (len(across) + q) + j, held, held, sibling, rows(ref, chip, 1 - c, piece))
        return pairs

    def start(*refs):
        for send, _ in copies(*refs):
            send.start()

    def wait(*refs):
        pairs = copies(*refs)
        for _, arrival in pairs:
            arrival.wait_recv()
        for send, _ in pairs:
            send.wait_send()

    return _Side(list(bufs), [jax.ShapeDtypeStruct(b.shape, b.dtype) for b in bufs],
                 3 * (len(across) + len(within)), start, wait, aliased=True)


def _chip_exchange(parts):
    n = len(parts)

    def copies(ins, outs, send_sems, recv_sems, base):
        x, y, c, chips = _place()
        return [pltpu.make_async_remote_copy(
            src_ref=ins[a].at[2 * chip[0] + chip[1]], dst_ref=outs[a].at[j],
            send_sem=send_sems.at[base + 3 * a + j], recv_sem=recv_sems.at[base + 3 * a + j],
            device_id=(*chip, c), device_id_type=MESH) for a in range(n) for j, chip in enumerate(chips)]

    return _Side(list(parts), [jax.ShapeDtypeStruct((N_CHIPS - 1,) + p.shape[1:], p.dtype) for p in parts],
                 3 * n, *_start_wait(copies))


def _pair_exchange(grads):
    def copies(ins, outs, send_sems, recv_sems, base):
        x, y, c, _ = _place()
        return [pltpu.make_async_remote_copy(
            src_ref=ins[a].at[:, 1 - c], dst_ref=outs[a], send_sem=send_sems.at[base + a],
            recv_sem=recv_sems.at[base + a], device_id=(x, y, 1 - c), device_id_type=MESH)
            for a in range(len(grads))]

    return _Side(list(grads), [jax.ShapeDtypeStruct((N_CHIPS,) + g.shape[2:], F32) for g in grads],
                 len(grads), *_start_wait(copies))


def _start_wait(copies):
    def start(*refs):
        for cp in copies(*refs):
            cp.start()

    def wait(*refs):
        cps = copies(*refs)
        for cp in cps:
            cp.wait_recv()
        for cp in cps:
            cp.wait_send()
    return start, wait


def _both(first, second):
    n_in, n_out = len(first.ins), len(first.out_shapes)

    def run(which):
        def go(ins, outs, send_sems, recv_sems, base):
            getattr(first, which)(ins[:n_in], outs[:n_out], send_sems, recv_sems, base)
            getattr(second, which)(ins[n_in:], outs[n_out:], send_sems, recv_sems, base + first.n_sems)
        return go

    return _Side(first.ins + second.ins, first.out_shapes + second.out_shapes,
                 first.n_sems + second.n_sems, run("start"), run("wait"))


def _share_halves(halves):
    n = len(halves)

    def body(*refs):
        ins, outs = refs[:n], refs[n:2 * n]
        send_sems, recv_sems = refs[2 * n:]
        x, y, c, _ = _place()
        sends = []
        for a in range(n):
            sends.append(pltpu.make_async_remote_copy(
                src_ref=ins[a].at[c], dst_ref=outs[a].at[c], send_sem=send_sems.at[a], recv_sem=recv_sems.at[a],
                device_id=(x, y, 1 - c), device_id_type=MESH))
            sends[a].start()
        for a in range(n):
            theirs = outs[a].at[1 - c]
            pltpu.make_async_remote_copy(
                src_ref=theirs, dst_ref=theirs, send_sem=send_sems.at[a], recv_sem=recv_sems.at[a],
                device_id=(x, y, c), device_id_type=MESH).wait_recv()
        for cp in sends:
            cp.wait_send()

    return pl.pallas_call(
        body, name="grads_share_halves",
        out_shape=[jax.ShapeDtypeStruct(h.shape, F32) for h in halves],
        in_specs=[ANY] * n, out_specs=[ANY] * n, input_output_aliases={a: a for a in range(n)},
        scratch_shapes=[pltpu.SemaphoreType.DMA((n,)), pltpu.SemaphoreType.DMA((n,))],
    )(*halves)


def _pack(arrays):
    pieces = []
    for a in arrays:
        flat = a.reshape(-1).astype(F32)
        pieces.append(jnp.pad(flat, (0, (-flat.size) % PACK_ALIGN)))
    return jnp.concatenate(pieces).reshape(-1, LANES)


def _unpack(buf, shapes):
    lead = buf.shape[:-2]
    flat = buf.reshape(lead + (-1,))
    out, off = [], 0
    for shp in shapes:
        size = 1
        for s in shp:
            size *= s
        out.append(flat[..., off:off + size].reshape(lead + tuple(shp)))
        off += size + (-size) % PACK_ALIGN
    return out


def _gather_channels(buf, shapes):
    per_chip = _unpack(buf[0::2], shapes)
    return [jnp.transpose(a, (1, 0, 2)).reshape(a.shape[1], -1) for a in per_chip]


def _mm_tile(n, rows, limit_bytes=6 * 1024 * 1024):
    for t in (1408, 1280, 1024, 640, 512, 384, 256, 128):
        if n % t == 0 and rows * t * 2 <= limit_bytes:
            return t
    raise ValueError(f"no column tile for {n} x {rows}")


def kernel(x, p, norm_mix_g, w_in, conv_a_w, conv_a_b, ln_a_g, ln_a_b, conv_b_w, w_out, norm_ffn_g, w_up, conv_ffn_w, w_down, w_ple_gate, b_ple_gate, w_ple_proj, norm_final_g, loss_target, m_norm_mix_g, m_w_in, m_conv_a_w, m_conv_a_b, m_ln_a_g, m_ln_a_b, m_conv_b_w, m_w_out, m_norm_ffn_g, m_w_up, m_conv_ffn_w, m_w_down, m_w_ple_gate, m_b_ple_gate, m_w_ple_proj, m_norm_final_g, v_norm_mix_g, v_w_in, v_conv_a_w, v_conv_a_b, v_ln_a_g, v_ln_a_b, v_conv_b_w, v_w_out, v_norm_ffn_g, v_w_up, v_conv_ffn_w, v_w_down, v_w_ple_gate, v_b_ple_gate, v_w_ple_proj, v_norm_final_g):
    S, D = x.shape[1], x.shape[2]
    P = p.shape[3]
    A = conv_a_b.shape[1]
    F = w_down.shape[1] * N_CHIPS
    KA, KB, KF = conv_a_w.shape[1], conv_b_w.shape[1], conv_ffn_w.shape[1]
    xi, yi, ci = lax.axis_index("x"), lax.axis_index("y"), lax.axis_index("c")
    chip = 2 * xi + yi

    TM = _pick(S, (512, 256, 128))
    TE = _pick(S, (256, 128))
    TC = _pick(2 * F // N_CHIPS, (1408, 1024, 512, 256, 128))
    ffn_place = _pair_tile(F // TC)

    x2, p2, t2 = x.reshape(S, D), p.reshape(S, P), loss_target.reshape(S, D)
    gfin = norm_final_g.reshape(1, D)

    big = dict(w_in=w_in[0], w_out=w_out[0], w_up=w_up[0], w_down=w_down[0],
               w_ple_gate=w_ple_gate[0], w_ple_proj=w_ple_proj[0])
    names = list(big)
    buf = {n: _cast_into_gathered("cast_" + n, big[n], chip) for n in names}
    half = {n: big[n].shape[0] // 2 for n in names}
    up_a = half["w_up"] // 2
    (w_in3,) = _comm_only("gather_w_in_across", _gather_side([buf["w_in"]], [(0, 0, half["w_in"])], []))
    (w_in3,) = _comm_only("gather_w_in_within", _gather_side([w_in3], [], [(0, 0, half["w_in"])]))

    tap_shapes = [(KA, A // N_CHIPS), (KB, A // N_CHIPS), (KF, 2 * F // N_CHIPS)]
    taps = _allgather_small("allgather_taps", _pack([conv_a_w[0], conv_b_w[0], conv_ffn_w[0]]))
    conv_a_f, conv_b_f, conv_ffn_f = _gather_channels(taps, tap_shapes)

    def rms_prologue(rows, row_r, vec_r, ro_r, ao_r):
        h = row_r[0][rows, :]
        hn = (h * _rms_stats(h) * vec_r[0][...]).astype(BF16)
        ro_r[0][rows, :] = hn
        return [hn]

    def cast_prologue(rows, row_r, vec_r, ro_r, ao_r):
        hb = row_r[0][rows, :].astype(BF16)
        ro_r[0][rows, :] = hb
        return [hb]

    plain = lambda accs, tile_r, cv_r: [accs[0]]
    residual = lambda accs, tile_r, cv_r: [tile_r[0][...] + accs[0]]

    (z, hn1), (w_out_t, w_up_t) = _rows_mm(
        "in_proj", S, TM, 5 * A, _mm_tile(5 * A // N_CHIPS, D), row_ins=[x2], vec_ins=[norm_mix_g],
        weights=[(w_in3, "nn3")], tile_outs=[BF16], row_outs=[(D, BF16)], prologue=rms_prologue, epilogue=plain,
        side=_gather_side([buf["w_out"], buf["w_up"]], [(0, 0, half["w_out"]), (1, 0, up_a)], []))
    (a1, cat), (w_out3, w_up_t) = _mixer_fwd(
        z, conv_a_f, conv_a_b, ln_a_g, ln_a_b, conv_b_f, S, TE, A,
        side=_gather_side([w_out_t, w_up_t], [(1, up_a, half["w_up"] - up_a)],
                          [(0, 0, half["w_out"]), (1, 0, up_a)]))
    w_out_f = w_out3.reshape(2 * A, D)
    (h1,), (w_up3, w_proj_t) = _rows_mm(
        "out_proj", S, TM, D, _mm_tile(D, 2 * A), row_ins=[cat], weights=[(w_out_f, "nn2")],
        tile_ins=[x2], tile_outs=[F32], epilogue=residual,
        side=_gather_side([w_up_t, buf["w_ple_proj"]], [(1, 0, half["w_ple_proj"])],
                          [(0, up_a, half["w_up"] - up_a)]))
    (u0, hn2), (w_down_t, w_gate_t, w_proj3) = _rows_mm(
        "up_proj", S, TM, 2 * F, TC, row_ins=[h1], vec_ins=[norm_ffn_g],
        weights=[(w_up3, "nn3")], tile_outs=[BF16], row_outs=[(D, BF16)],
        prologue=rms_prologue, epilogue=plain, place=ffn_place,
        side=_gather_side([buf["w_down"], buf["w_ple_gate"], w_proj_t],
                          [(0, 0, half["w_down"]), (1, 0, half["w_ple_gate"])], [(2, 0, half["w_ple_proj"])]))
    (act,), (w_down3, w_gate3) = _ffn_act(
        u0, conv_ffn_f, S, TM, F, TC,
        side=_gather_side([w_down_t, w_gate_t], [], [(0, 0, half["w_down"]), (1, 0, half["w_ple_gate"])]))
    w_down_f = w_down3.reshape(F, D)
    w_gate_f = w_gate3.reshape(D, D)
    w_proj_f = jnp.transpose(w_proj3, (1, 0, 2)).reshape(P, D)
    (h2,) = _rows_mm("down_proj", S, TM, D, _mm_tile(D, F), row_ins=[act], weights=[(w_down_f, "nn2")],
                     tile_ins=[h1], tile_outs=[F32], epilogue=residual)

    def ple_prologue(rows, row_r, vec_r, ro_r, ao_r):
        hb = row_r[0][rows, :].astype(BF16)
        pb = row_r[1][rows, :].astype(BF16)
        ro_r[0][rows, :] = hb
        ro_r[1][rows, :] = pb
        return [hb, pb]

    def ple_epilogue(accs, tile_r, cv_r):
        gate = _sigmoid(accs[0] + cv_r[0][...])
        return [tile_r[0][...] + accs[1] * gate, gate, accs[1]]

    h3, gate, pp, h2b, pb = _rows_mm(
        "ple_fwd", S, TM, D, _mm_tile(D, D), row_ins=[h2, p2], colvec_ins=[b_ple_gate],
        weights=[(w_gate_f, "nn2"), (w_proj_f, "nn2")], tile_ins=[h2], tile_outs=[F32, BF16, BF16],
        row_outs=[(D, BF16), (P, BF16)], prologue=ple_prologue, epilogue=ple_epilogue)
    loss_part, g_norm_final, dh3, dpre, dpp, g_b_gate = _loss_head(h3, t2, gfin, pp, gate, S, TE, D)

    TK = _pick(S, (1024, 512, 256, 128))
    wt = lambda n: _pick(n, (1408, 1280, 1024, 512, 256, 128))
    chip_sums, from_chips = {}, {}

    def to_sibling(parts):
        ns = list(parts)
        halves = [parts[n].reshape(N_CHIPS, 2, big[n].shape[0] // 2, big[n].shape[1]) for n in ns]
        return ns, halves, _pair_exchange(halves)

    def to_chips(ns, halves, from_sibling):
        sums = [_add_pair("pair_sum_" + n, h, r, ci) for n, h, r in zip(ns, halves, from_sibling)]
        for n, (s, _) in zip(ns, sums):
            chip_sums[n] = s
        return ns, _chip_exchange([b for _, b in sums])

    def landed(ns, side_outs):
        for n, r in zip(ns, side_outs):
            from_chips[n] = r

    ns, halves, side = to_sibling(dict(
        w_ple_gate=_tn_mm("dw_ple_gate", h2b, dpre, wt(D), wt(D), TK),
        w_ple_proj=_tn_mm("dw_ple_proj", pb, dpp, wt(P), wt(D // N_CHIPS), TK, cols_per_chip=D // N_CHIPS)))
    (dh2,), got = _rows_mm("ple_bwd", S, TM, D, _mm_tile(D, D), row_ins=[dpre], weights=[(w_gate_f, "nt2")],
                           tile_ins=[dh3], tile_outs=[F32], epilogue=residual, side=side)
    ple_ns, ple_chips = to_chips(ns, halves, got)
    (dact, dh2b), got = _rows_mm("down_bwd", S, TM, F, _mm_tile(F, D), row_ins=[dh2], weights=[(w_down_f, "nt2")],
                                 tile_outs=[BF16], row_outs=[(D, BF16)], prologue=cast_prologue, epilogue=plain,
                                 side=ple_chips)
    landed(ple_ns, got)
    ns, halves, side = to_sibling(dict(w_down=_tn_mm("dw_down", act, dh2b, wt(F), wt(D), TK)))
    (du0, g_conv_gate, g_conv_up), got = _ffn_bwd(u0, dact, conv_ffn_f, S, TM, F, TC, side=side)
    down_ns, down_chips = to_chips(ns, halves, got)
    g_conv_ffn = jnp.concatenate([g_conv_gate, g_conv_up], axis=1)
    ns, halves, side = to_sibling(dict(
        w_up=_tn_mm("dw_up", hn2, du0, wt(D), TC, TK, cols_per_chip=2 * F // N_CHIPS, place=ffn_place)))

    def up_bwd_epilogue(acc, rows, row_r, vec_r, ro_r, ao_r):
        dh, dg = _rms_bwd(row_r[0][rows, :], vec_r[0][...], acc)
        dh1_ = row_r[1][rows, :] + dh
        ro_r[0][rows, :] = dh1_
        ro_r[1][rows, :] = dh1_.astype(BF16)
        ao_r[0][...] += dg

    (dh1, dh1b, g_norm_ffn), got = _kloop_mm(
        "up_bwd", S, TM, du0, w_up3, TC, row_ins=[h1, dh2], vec_ins=[norm_ffn_g],
        row_outs=[(D, F32), (D, BF16)], acc_outs=[(1, D)], epilogue=up_bwd_epilogue, place=ffn_place,
        side=_both(down_chips, side))
    landed(down_ns, got[:len(down_ns)])
    up_ns, up_chips = to_chips(ns, halves, got[len(down_ns):])
    ns, halves, side = to_sibling(dict(w_out=_tn_mm("dw_out", cat, dh1b, wt(2 * A), wt(D), TK)))
    (dcat,), got = _rows_mm("out_bwd", S, TM, 2 * A, _mm_tile(2 * A, D), row_ins=[dh1b],
                            weights=[(w_out_f, "nt2")], tile_outs=[BF16], epilogue=plain, side=side)
    out_ns, out_chips = to_chips(ns, halves, got)
    da1, ln_sums = _mixer_bwd_ln(dcat, a1, ln_a_g, ln_a_b, S, TE, A)
    (dz, g_conv_a, g_conv_b), got = _mixer_bwd_conv(z, dcat, da1, conv_a_f, conv_b_f, S, TE, A,
                                                    side=_both(up_chips, out_chips))
    landed(up_ns + out_ns, got)
    ns, halves, side = to_sibling(dict(
        w_in=_tn_mm("dw_in", hn1, dz, wt(D), wt(5 * A // N_CHIPS), TK, cols_per_chip=5 * A // N_CHIPS)))
    ns, side = to_chips(ns, halves, _comm_only("grads_exchange_pairs_in", side))

    def in_bwd_epilogue(acc, rows, row_r, vec_r, ro_r, ao_r):
        dh, dg = _rms_bwd(row_r[0][rows, :], vec_r[0][...], acc)
        ro_r[0][rows, :] = row_r[1][rows, :] + dh
        ao_r[0][...] += dg

    (dx, g_norm_mix), got = _kloop_mm(
        "in_bwd", S, TM, dz, w_in3, _mm_tile(5 * A // N_CHIPS, D), row_ins=[x2, dh1],
        vec_ins=[norm_mix_g], row_outs=[(D, F32)], acc_outs=[(1, D)], epilogue=in_bwd_epilogue, side=side)
    landed(ns, got)

    reduced = _share_halves([_add_chips("chip_sum_" + n, chip_sums[n], from_chips[n], chip, ci) for n in names])
    moments = dict(w_in=(m_w_in, v_w_in), w_out=(m_w_out, v_w_out), w_up=(m_w_up, v_w_up),
                   w_down=(m_w_down, v_w_down), w_ple_gate=(m_w_ple_gate, v_w_ple_gate),
                   w_ple_proj=(m_w_ple_proj, v_w_ple_proj))
    grads, deltas, new_m, new_v = {}, {}, {}, {}
    for n, g in zip(names, reduced):
        g = g.reshape(big[n].shape)
        d_, m_, v_ = _adamw("adamw_" + n, big[n], g, moments[n][0][0], moments[n][1][0])
        grads[n], deltas[n], new_m[n], new_v[n] = g[None], d_[None], m_[None], v_[None]

    small = ["norm_mix_g", "conv_a_w", "conv_a_b", "ln_a_g", "ln_a_b", "conv_b_w", "norm_ffn_g",
             "conv_ffn_w", "b_ple_gate", "norm_final_g"]
    small_part = [g_norm_mix, g_conv_a, ln_sums[2:3], ln_sums[0:1], ln_sums[1:2], g_conv_b, g_norm_ffn,
                  g_conv_ffn, g_b_gate, g_norm_final]
    full_shapes = [a.shape for a in small_part]
    summed = _sum_devices("small_grads_sum", _allgather_small("allgather_small_grads", _pack(small_part)))
    small_g = dict(zip(small, _unpack(summed, full_shapes)))
    for n, width in (("conv_a_w", A), ("conv_b_w", A), ("conv_ffn_w", 2 * F)):
        small_g[n] = lax.dynamic_slice_in_dim(small_g[n], chip * (width // N_CHIPS), width // N_CHIPS, axis=1)
    small_w = dict(norm_mix_g=(norm_mix_g, m_norm_mix_g, v_norm_mix_g), conv_a_w=(conv_a_w, m_conv_a_w, v_conv_a_w),
                   conv_a_b=(conv_a_b, m_conv_a_b, v_conv_a_b), ln_a_g=(ln_a_g, m_ln_a_g, v_ln_a_g),
                   ln_a_b=(ln_a_b, m_ln_a_b, v_ln_a_b), conv_b_w=(conv_b_w, m_conv_b_w, v_conv_b_w),
                   norm_ffn_g=(norm_ffn_g, m_norm_ffn_g, v_norm_ffn_g),
                   conv_ffn_w=(conv_ffn_w, m_conv_ffn_w, v_conv_ffn_w),
                   b_ple_gate=(b_ple_gate, m_b_ple_gate, v_b_ple_gate),
                   norm_final_g=(norm_final_g, m_norm_final_g, v_norm_final_g))
    out_shapes = [small_w[n][0].shape for n in small]
    packed_g = _pack([small_g[n] for n in small])
    packed = [_pack([small_w[n][k] for n in small]) for k in range(3)]
    d_s, m_s, v_s = _adamw("adamw_small", packed[0], packed_g, packed[1], packed[2])
    for n, g, d_, m_, v_ in zip(small, _unpack(packed_g, out_shapes), _unpack(d_s, out_shapes),
                                _unpack(m_s, out_shapes), _unpack(v_s, out_shapes)):
        grads[n], deltas[n], new_m[n], new_v[n] = g, d_, m_, v_

    order = ["norm_mix_g", "w_in", "conv_a_w", "conv_a_b", "ln_a_g", "ln_a_b", "conv_b_w", "w_out", "norm_ffn_g",
             "w_up", "conv_ffn_w", "w_down", "w_ple_gate", "b_ple_gate", "w_ple_proj", "norm_final_g"]
    loss = lax.psum(loss_part[0, 0], ("x", "y", "c"))
    return (loss, dx.reshape(x.shape), *[grads[n] for n in order], *[deltas[n] for n in order],
            *[new_m[n] for n in order], *[new_v[n] for n in order])
```

```python
from typing import Callable, NamedTuple

import jax
import jax.numpy as jnp
from jax import lax
from jax.experimental import pallas as pl
from jax.experimental.pallas import tpu as pltpu

F32 = jnp.float32
BF16 = jnp.bfloat16
MESH = pl.DeviceIdType.MESH
ANY = pl.BlockSpec(memory_space=pl.ANY)

EPS = 1e-6
ADAM_LR = 0.001
ADAM_B1 = 0.9
ADAM_B2 = 0.999
ADAM_EPS = 1e-08
ADAM_WD = 0.01
ADAM_STEP = 10

N_CHIPS = 4
N_DEV = 8
LANES = 128
SUBLANES = 8
PACK_ALIGN = LANES * SUBLANES
ROW_CHUNK = 32
VMEM_CAP = 60 * 1024 * 1024
VMEM_SLACK = 6 * 1024 * 1024


def _pick(n, cands):
    for c in cands:
        if n % c == 0:
            return c
    raise ValueError(f"no tile of {cands} divides {n}")


def _nbytes(shape, dtype):
    n = 1
    for s in shape:
        if s is not None:
            n *= s
    return n * jnp.dtype(dtype).itemsize


def _params(sem, blocks, scratch=(), temps=()):
    est = (2 * sum(_nbytes(s, d) for s, d in blocks) + sum(_nbytes(s, d) for s, d in scratch)
           + sum(_nbytes(s, d) for s, d in temps))
    return pltpu.CompilerParams(dimension_semantics=sem,
                                vmem_limit_bytes=min(est + VMEM_SLACK, VMEM_CAP))


def _sigmoid(x):
    return 1.0 / (1.0 + jnp.exp(-x))


def _rsum(x):
    return jnp.sum(x, axis=0, keepdims=True)


class _Side(NamedTuple):
    ins: list
    out_shapes: list
    n_sems: int
    start: Callable
    wait: Callable
    aliases: tuple = ()


def _call(body, side, *, name, grid, in_specs, out_specs, out_shape, scratch, params, args):
    vmem = [pltpu.VMEM(s, d) for s, d in scratch]
    if side is None:
        outs = pl.pallas_call(body, name=name, grid=grid, in_specs=in_specs, out_specs=out_specs,
                              out_shape=out_shape, scratch_shapes=vmem, compiler_params=params)(*args)
        return list(outs), []
    n_in, n_out, n_sc = len(in_specs), len(out_specs), len(scratch)
    ns_in, ns_out = len(side.ins), len(side.out_shapes)

    def carrier(*refs):
        pos = [0]
        def take(n):
            pos[0] += n
            return refs[pos[0] - n:pos[0]]
        ins, s_ins, outs, s_outs, scr = take(n_in), take(ns_in), take(n_out), take(ns_out), take(n_sc)
        send_sems, recv_sems = take(2)
        first = last = None
        for axis, extent in enumerate(grid):
            at_start, at_end = pl.program_id(axis) == 0, pl.program_id(axis) == extent - 1
            first = at_start if first is None else first & at_start
            last = at_end if last is None else last & at_end

        @pl.when(first)
        def _():
            side.start(s_ins, s_outs, send_sems, recv_sems, 0)
        body(*ins, *outs, *scr)

        @pl.when(last)
        def _():
            side.wait(s_ins, s_outs, send_sems, recv_sems, 0)

    outs = pl.pallas_call(
        carrier, name=name, grid=grid, in_specs=list(in_specs) + [ANY] * ns_in,
        out_specs=list(out_specs) + [ANY] * ns_out, out_shape=list(out_shape) + list(side.out_shapes),
        scratch_shapes=vmem + [pltpu.SemaphoreType.DMA((side.n_sems,)), pltpu.SemaphoreType.DMA((side.n_sems,))],
        input_output_aliases={n_in + a: n_out + b for a, b in side.aliases},
        compiler_params=params)(*args, *side.ins)
    return list(outs[:n_out]), list(outs[n_out:])


def _comm_only(name, side):
    n_in = len(side.ins)

    def body(*refs):
        ins, outs = refs[:n_in], refs[n_in:n_in + len(side.out_shapes)]
        send_sems, recv_sems = refs[n_in + len(side.out_shapes):]
        side.start(ins, outs, send_sems, recv_sems, 0)
        side.wait(ins, outs, send_sems, recv_sems, 0)

    return pl.pallas_call(
        body, name=name, out_shape=list(side.out_shapes), in_specs=[ANY] * n_in,
        out_specs=[ANY] * len(side.out_shapes),
        scratch_shapes=[pltpu.SemaphoreType.DMA((side.n_sems,)), pltpu.SemaphoreType.DMA((side.n_sems,))],
        input_output_aliases=dict(side.aliases),
    )(*side.ins)


def _rms_stats(x):
    return lax.rsqrt(jnp.mean(x * x, axis=-1, keepdims=True) + EPS)


def _rms_bwd(h, g, dout):
    r = _rms_stats(h)
    n = h * r
    dn = dout * g
    dh = r * (dn - n * jnp.mean(dn * n, axis=-1, keepdims=True))
    return dh, _rsum(dout * n)


def _identity(t):
    return t


def _chip_major(nb, place=_identity):
    return lambda i, j: (place(j) // nb, 0, place(j) % nb)


def _rows_mm(name, S, TM, N, TN, *, row_ins, vec_ins=(), colvec_ins=(), weights, tile_ins=(),
             tile_outs, row_outs=(), acc_outs=(), prologue=None, epilogue, place=_identity, side=None):
    nI, nJ = S // TM, N // TN
    n_row, n_vec, n_cv, n_w, n_tile = len(row_ins), len(vec_ins), len(colvec_ins), len(weights), len(tile_ins)
    n_to, n_ro, n_ao = len(tile_outs), len(row_outs), len(acc_outs)

    in_specs, blocks, scratch, ks = [], [], [], []
    for a in row_ins:
        in_specs.append(pl.BlockSpec((TM, a.shape[1]), lambda i, j: (i, 0)))
        blocks.append(((TM, a.shape[1]), a.dtype))
    for a in vec_ins:
        in_specs.append(pl.BlockSpec(a.shape, lambda i, j: (0, 0)))
        blocks.append((a.shape, a.dtype))
    for a in colvec_ins:
        in_specs.append(pl.BlockSpec((1, TN), lambda i, j: (0, j)))
        blocks.append(((1, TN), a.dtype))
    for w, mode in weights:
        if mode == "nn2":
            k = w.shape[0]
            in_specs.append(pl.BlockSpec((k, TN), lambda i, j: (0, j)))
        elif mode == "nn3":
            k = w.shape[1]
            in_specs.append(pl.BlockSpec((None, k, TN), _chip_major(w.shape[2] // TN, place)))
        else:
            k = w.shape[1]
            in_specs.append(pl.BlockSpec((TN, k), lambda i, j: (j, 0)))
        ks.append(k)
        blocks.append(((k, TN), BF16))
        if prologue is not None:
            scratch.append(((TM, k), BF16))
    for a in tile_ins:
        in_specs.append(pl.BlockSpec((TM, TN), lambda i, j: (i, j)))
        blocks.append(((TM, TN), a.dtype))

    out_shape, out_specs = [], []
    for dt in tile_outs:
        out_shape.append(jax.ShapeDtypeStruct((S, N), dt))
        out_specs.append(pl.BlockSpec((TM, TN), lambda i, j: (i, j)))
        blocks.append(((TM, TN), dt))
    for width, dt in row_outs:
        out_shape.append(jax.ShapeDtypeStruct((S, width), dt))
        out_specs.append(pl.BlockSpec((TM, width), lambda i, j: (i, 0)))
        blocks.append(((TM, width), dt))
    for rows, width in acc_outs:
        out_shape.append(jax.ShapeDtypeStruct((rows, width), F32))
        out_specs.append(pl.BlockSpec((rows, width), lambda i, j: (0, 0)))
        blocks.append(((rows, width), F32))

    modes = [m for _, m in weights]

    def body(*refs):
        pos = 0
        def take(n):
            nonlocal pos
            out = refs[pos:pos + n]
            pos += n
            return out
        row_r, vec_r, cv_r, w_r, tile_r = take(n_row), take(n_vec), take(n_cv), take(n_w), take(n_tile)
        to_r, ro_r, ao_r, a_sc = take(n_to), take(n_ro), take(n_ao), take(len(scratch))
        i, j = pl.program_id(0), pl.program_id(1)

        if prologue is None:
            a_sc = row_r[:n_w]
        else:
            @pl.when(j == 0)
            def _():
                if n_ao:
                    @pl.when(i == 0)
                    def _():
                        for r in ao_r:
                            r[...] = jnp.zeros_like(r)

                def chunk(ci, carry):
                    rows = pl.ds(pl.multiple_of(ci * ROW_CHUNK, ROW_CHUNK), ROW_CHUNK)
                    for sc, a in zip(a_sc, prologue(rows, row_r, vec_r, ro_r, ao_r)):
                        sc[rows, :] = a
                    return carry
                lax.fori_loop(0, TM // ROW_CHUNK, chunk, 0)

        accs = []
        for w_ref, sc, mode in zip(w_r, a_sc, modes):
            if mode == "nt2":
                accs.append(lax.dot_general(sc[...], w_ref[...], (((1,), (1,)), ((), ())),
                                            preferred_element_type=F32))
            else:
                accs.append(jnp.dot(sc[...], w_ref[...], preferred_element_type=F32))
        outs = epilogue(accs, tile_r, cv_r)
        for r, o in zip(to_r, outs):
            r[...] = o.astype(r.dtype)

    outs, side_outs = _call(
        body, side, name=name, grid=(nI, nJ), in_specs=in_specs, out_specs=out_specs, out_shape=out_shape,
        scratch=scratch, params=_params(("arbitrary", "arbitrary"), blocks, scratch, temps=[((TM, TN), F32)] * 3),
        args=[*row_ins, *vec_ins, *colvec_ins, *[w for w, _ in weights], *tile_ins])
    return outs if side is None else (outs, side_outs)


def _kloop_mm(name, S, TM, a, w3, TK, *, row_ins, vec_ins, row_outs, acc_outs, epilogue, place=_identity,
              side=None):
    _, N, Ks = w3.shape
    nb = Ks // TK
    nK = N_CHIPS * nb
    n_row, n_vec, n_ro, n_ao = len(row_ins), len(vec_ins), len(row_outs), len(acc_outs)

    in_specs = [pl.BlockSpec((TM, TK), lambda i, k: (i, k)),
                pl.BlockSpec((None, N, TK), _chip_major(nb, place))]
    blocks = [((TM, TK), BF16), ((N, TK), BF16)]
    for r in row_ins:
        in_specs.append(pl.BlockSpec((TM, r.shape[1]), lambda i, k: (i, 0)))
        blocks.append(((TM, r.shape[1]), r.dtype))
    for v in vec_ins:
        in_specs.append(pl.BlockSpec(v.shape, lambda i, k: (0, 0)))
        blocks.append((v.shape, v.dtype))
    out_shape, out_specs = [], []
    for width, dt in row_outs:
        out_shape.append(jax.ShapeDtypeStruct((S, width), dt))
        out_specs.append(pl.BlockSpec((TM, width), lambda i, k: (i, 0)))
        blocks.append(((TM, width), dt))
    for rows, width in acc_outs:
        out_shape.append(jax.ShapeDtypeStruct((rows, width), F32))
        out_specs.append(pl.BlockSpec((rows, width), lambda i, k: (0, 0)))
        blocks.append(((rows, width), F32))
    scratch = [((TM, N), F32)]

    def body(*refs):
        a_ref, w_ref = refs[0], refs[1]
        row_r = refs[2:2 + n_row]
        vec_r = refs[2 + n_row:2 + n_row + n_vec]
        pos = 2 + n_row + n_vec
        ro_r = refs[pos:pos + n_ro]
        ao_r = refs[pos + n_ro:pos + n_ro + n_ao]
        acc_sc = refs[pos + n_ro + n_ao]
        i, k = pl.program_id(0), pl.program_id(1)
        @pl.when(k == 0)
        def _():
            acc_sc[...] = jnp.zeros_like(acc_sc)
        acc_sc[...] += lax.dot_general(a_ref[...], w_ref[...], (((1,), (1,)), ((), ())),
                                       preferred_element_type=F32)

        @pl.when(k == nK - 1)
        def _():
            @pl.when(i == 0)
            def _():
                for r in ao_r:
                    r[...] = jnp.zeros_like(r)

            def chunk(ci, carry):
                rows = pl.ds(pl.multiple_of(ci * ROW_CHUNK, ROW_CHUNK), ROW_CHUNK)
                epilogue(acc_sc[rows, :], rows, row_r, vec_r, ro_r, ao_r)
                return carry
            lax.fori_loop(0, TM // ROW_CHUNK, chunk, 0)

    outs, side_outs = _call(
        body, side, name=name, grid=(S // TM, nK), in_specs=in_specs, out_specs=out_specs, out_shape=out_shape,
        scratch=scratch, params=_params(("arbitrary", "arbitrary"), blocks, scratch, temps=[((TM, N), F32)]),
        args=[a, w3, *row_ins, *vec_ins])
    return outs if side is None else (outs, side_outs)


def _tn_mm(name, a, b, TMw, TNw, TK, cols_per_chip=None, place=_identity):
    S, M = a.shape
    N = b.shape[1]
    nK = S // TK
    if cols_per_chip is None:
        out_shape = jax.ShapeDtypeStruct((M, N), F32)
        out_spec = pl.BlockSpec((TMw, TNw), lambda i, j, k: (i, j))
    else:
        nb = cols_per_chip // TNw
        out_shape = jax.ShapeDtypeStruct((N_CHIPS, M, cols_per_chip), F32)
        out_spec = pl.BlockSpec((None, TMw, TNw), lambda i, j, k: (place(j) // nb, i, place(j) % nb))

    def body(a_ref, b_ref, o_ref):
        @pl.when(pl.program_id(2) == 0)
        def _():
            o_ref[...] = jnp.zeros_like(o_ref)
        o_ref[...] += lax.dot_general(a_ref[...], b_ref[...], (((0,), (0,)), ((), ())),
                                      preferred_element_type=F32)

    blocks = [((TK, TMw), BF16), ((TK, TNw), BF16), ((TMw, TNw), F32)]
    return pl.pallas_call(
        body, name=name, grid=(M // TMw, N // TNw, nK),
        in_specs=[pl.BlockSpec((TK, TMw), lambda i, j, k: (k, i)),
                  pl.BlockSpec((TK, TNw), lambda i, j, k: (k, j))],
        out_specs=out_spec, out_shape=out_shape,
        compiler_params=_params(("arbitrary", "arbitrary", "arbitrary"), blocks,
                                temps=[((TMw, TNw), F32), ((TK, TMw), BF16)]),
    )(a, b)


def _prev_rows(TM, H, col):
    return lambda i: (jnp.maximum(i * (TM // H) - 1, 0), col)


def _next_rows(S, TM, H, col):
    return lambda i: (jnp.minimum((i + 1) * (TM // H), S // H - 1), col)


def _taps_causal(ext_ref, w_ref, K, H, TM, cs):
    acc = None
    for k in range(K):
        term = ext_ref[pl.ds(H - (K - 1) + k, TM), cs] * w_ref[pl.ds(k, 1), cs]
        acc = term if acc is None else acc + term
    return acc


def _taps_anticausal(ext_ref, w_ref, K, TM, cs):
    acc = None
    for k in range(K):
        term = ext_ref[pl.ds(K - 1 - k, TM), cs] * w_ref[pl.ds(k, 1), cs]
        acc = term if acc is None else acc + term
    return acc


def _tap_grads(ext_ref, g, K, H, TM, cs):
    return [_rsum(ext_ref[pl.ds(H - (K - 1) + k, TM), cs] * g) for k in range(K)]


def _shift_copies(ext_ref, shifted, cs):
    n = shifted.shape[1]
    for r in range(1, SUBLANES):
        shifted[r - 1] = ext_ref[pl.ds(r, n), cs]


def _rows_at(ext_ref, shifted, start, n, cs):
    q, r = divmod(start, SUBLANES)
    if r == 0:
        return ext_ref[pl.ds(start, n), cs]
    return shifted[r - 1, pl.ds(SUBLANES * q, n), :]


def _mixer_fwd(z, conv_a_w, conv_a_b, ln_g, ln_b, conv_b_w, S, TM, A, side=None):
    H = 32
    KA, KB = conv_a_w.shape[0], conv_b_w.shape[0]
    n_chunks = A // LANES
    RB = _pick(TM, (64, 32))

    def body(zc_ref, zh_ref, wa_ref, ba_ref, g_ref, b_ref, wb_ref, a1_ref, cat_ref, ext_a, ext_b, shifted):
        i = pl.program_id(0)
        live = (i > 0).astype(F32)
        zc = zc_ref[...].astype(F32)
        zh = zh_ref[...].astype(F32) * live
        ext_a[pl.ds(0, H), :] = zh[:, 0:A] * _sigmoid(zh[:, A:2 * A])
        ext_a[pl.ds(H, TM), :] = zc[:, 0:A] * _sigmoid(zc[:, A:2 * A])
        ext_b[pl.ds(0, H), :] = zh[:, 3 * A:4 * A] * zh[:, 4 * A:5 * A]
        ext_b[pl.ds(H, TM), :] = zc[:, 3 * A:4 * A] * zc[:, 4 * A:5 * A]

        def chunk(c, carry):
            cs = pl.ds(pl.multiple_of(c * LANES, LANES), LANES)
            _shift_copies(ext_a, shifted, cs)
            for r0 in range(0, TM, RB):
                acc = None
                for k in range(KA):
                    term = _rows_at(ext_a, shifted, H - (KA - 1) + k + r0, RB, cs) * wa_ref[pl.ds(k, 1), cs]
                    acc = term if acc is None else acc + term
                a1_ref[pl.ds(r0, RB), cs] = acc + ba_ref[:, cs]
            return carry
        lax.fori_loop(0, n_chunks, chunk, 0)

        a1 = a1_ref[...]
        mu = jnp.mean(a1, axis=-1, keepdims=True)
        d = a1 - mu
        var = jnp.mean(d * d, axis=-1, keepdims=True)
        a2 = d * lax.rsqrt(var + EPS) * g_ref[...] + b_ref[...]
        cat_ref[:, 0:A] = (a2 * _sigmoid(a2)).astype(BF16)
        cbc = _taps_causal(ext_b, wb_ref, KB, H, TM, slice(None))
        cat_ref[:, A:2 * A] = (zc[:, 2 * A:3 * A] * cbc).astype(BF16)

    blocks = [((TM, 5 * A), BF16), ((H, 5 * A), BF16), ((KA, A), F32), ((KB, A), F32),
              ((TM, A), F32), ((TM, 2 * A), BF16)]
    scratch = [((H + TM, A), F32), ((H + TM, A), F32), ((SUBLANES - 1, H + TM - SUBLANES, LANES), F32)]
    vec = lambda r: pl.BlockSpec((r, A), lambda i: (0, 0))
    outs, side_outs = _call(
        body, side, name="mixer_fwd", grid=(S // TM,),
        in_specs=[pl.BlockSpec((TM, 5 * A), lambda i: (i, 0)),
                  pl.BlockSpec((H, 5 * A), _prev_rows(TM, H, 0)),
                  vec(KA), vec(1), vec(1), vec(1), vec(KB)],
        out_specs=[pl.BlockSpec((TM, A), lambda i: (i, 0)), pl.BlockSpec((TM, 2 * A), lambda i: (i, 0))],
        out_shape=[jax.ShapeDtypeStruct((S, A), F32), jax.ShapeDtypeStruct((S, 2 * A), BF16)],
        scratch=scratch,
        params=_params(("arbitrary",), blocks, scratch, temps=[((TM, 5 * A), F32)] * 2 + [((TM, A), F32)] * 10),
        args=[z, z, conv_a_w, conv_a_b, ln_g, ln_b, conv_b_w])
    return outs if side is None else (outs, side_outs)


def _pair_tile(nF):
    return lambda t: (t % 2) * nF + t // 2


FFN_ROWS = 16


def _bcast_taps(w_ref, K, lanes):
    return [jnp.broadcast_to(w_ref[pl.ds(k, 1), lanes], (FFN_ROWS, LANES)) for k in range(K)]


def _ffn_act(u0, conv_w, S, TM, F, TC, side=None):
    H = 16
    K = conv_w.shape[0]
    nF = F // TC

    def body(uc_ref, uh_ref, wg_ref, wu_ref, o_ref, conv_ref, ext):
        live = (pl.program_id(0) > 0).astype(F32)
        ext[pl.ds(0, H), :] = uh_ref[...].astype(F32) * live
        ext[pl.ds(H, TM), :] = uc_ref[...].astype(F32)

        def lane_chunk(c, carry):
            lo = pl.ds(pl.multiple_of(c * LANES, LANES), LANES)
            lg, lu = lo, pl.ds(pl.multiple_of(TC + c * LANES, LANES), LANES)
            wg, wu = _bcast_taps(wg_ref, K, lo), _bcast_taps(wu_ref, K, lo)
            for r0 in range(0, TM, FFN_ROWS):
                g = u = None
                for k in range(K):
                    rows = pl.ds(H - (K - 1) + k + r0, FFN_ROWS)
                    tg, tu = ext[rows, lg] * wg[k], ext[rows, lu] * wu[k]
                    g, u = (tg, tu) if g is None else (g + tg, u + tu)
                o_ref[pl.ds(r0, FFN_ROWS), lo] = (g * _sigmoid(g) * u).astype(BF16)
                conv_ref[pl.ds(r0, FFN_ROWS), lg] = g.astype(BF16)
                conv_ref[pl.ds(r0, FFN_ROWS), lu] = u.astype(BF16)
            return carry
        lax.fori_loop(0, TC // LANES, lane_chunk, 0)

    blocks = [((TM, 2 * TC), BF16), ((H, 2 * TC), BF16), ((K, TC), F32), ((K, TC), F32), ((TM, TC), BF16),
              ((TM, 2 * TC), BF16)]
    scratch = [((H + TM, 2 * TC), F32)]
    outs, side_outs = _call(
        body, side, name="ffn_act", grid=(S // TM, nF),
        in_specs=[pl.BlockSpec((TM, 2 * TC), lambda i, j: (i, j)),
                  pl.BlockSpec((H, 2 * TC), lambda i, j: (jnp.maximum(i * (TM // H) - 1, 0), j)),
                  pl.BlockSpec((K, TC), lambda i, j: (0, j)),
                  pl.BlockSpec((K, TC), lambda i, j: (0, j + nF))],
        out_specs=[pl.BlockSpec((TM, TC), lambda i, j: (i, j)), pl.BlockSpec((TM, 2 * TC), lambda i, j: (i, j))],
        out_shape=[jax.ShapeDtypeStruct((S, F), BF16), jax.ShapeDtypeStruct((S, 2 * F), BF16)],
        scratch=scratch,
        params=_params(("arbitrary", "arbitrary"), blocks, scratch, temps=[((TM, 2 * TC), F32)]),
        args=[u0, u0, conv_w, conv_w])
    return outs if side is None else (outs, side_outs)


def _loss_head(h3, target, g_final, pp, gate, S, TM, D):
    def body(h_ref, t_ref, g_ref, pp_ref, gt_ref, loss_ref, dg_ref, dh_ref, dpre_ref, dpp_ref, db_ref):
        @pl.when(pl.program_id(0) == 0)
        def _():
            loss_ref[...] = jnp.zeros_like(loss_ref)
            dg_ref[...] = jnp.zeros_like(dg_ref)
            db_ref[...] = jnp.zeros_like(db_ref)
        h = h_ref[...]
        g = g_ref[...]
        r = _rms_stats(h)
        n = h * r
        diff = n * g - t_ref[...]
        loss_ref[...] += 0.5 * jnp.sum(jnp.mean(diff * diff, axis=-1, keepdims=True), axis=0, keepdims=True)
        dy = diff * (1.0 / D)
        dn = dy * g
        dh = r * (dn - n * jnp.mean(dn * n, axis=-1, keepdims=True))
        dh_ref[...] = dh
        dg_ref[...] += _rsum(dy * n)
        gt = gt_ref[...].astype(F32)
        dpre = dh * pp_ref[...].astype(F32) * gt * (1.0 - gt)
        dpre_ref[...] = dpre.astype(BF16)
        dpp_ref[...] = (dh * gt).astype(BF16)
        db_ref[...] += _rsum(dpre)

    blocks = [((TM, D), F32)] * 3 + [((TM, D), BF16)] * 4 + [((1, D), F32)] * 3
    row = pl.BlockSpec((TM, D), lambda i: (i, 0))
    vec = pl.BlockSpec((1, D), lambda i: (0, 0))
    return pl.pallas_call(
        body, name="loss_head", grid=(S // TM,),
        in_specs=[row, row, vec, row, row],
        out_specs=[pl.BlockSpec((1, 1), lambda i: (0, 0)), vec, row, row, row, vec],
        out_shape=[jax.ShapeDtypeStruct((1, 1), F32), jax.ShapeDtypeStruct((1, D), F32),
                   jax.ShapeDtypeStruct((S, D), F32), jax.ShapeDtypeStruct((S, D), BF16),
                   jax.ShapeDtypeStruct((S, D), BF16), jax.ShapeDtypeStruct((1, D), F32)],
        compiler_params=_params(("arbitrary",), blocks, temps=[((TM, D), F32)] * 10),
    )(h3, target, g_final, pp, gate)


def _ffn_bwd(u0, conv_u0, dact, conv_w, S, TM, F, TC, side=None):
    H = FFN_ROWS
    K = conv_w.shape[0]
    nF, nI = F // TC, S // TM

    def body(xc_ref, cc_ref, cn_ref, dc_ref, dn_ref, wg_ref, wu_ref, o_ref, dwg_ref, dwu_ref, ext_d):
        i = pl.program_id(1)
        @pl.when(i == 0)
        def _():
            dwg_ref[...] = jnp.zeros_like(dwg_ref)
            dwu_ref[...] = jnp.zeros_like(dwu_ref)
        last = (i < nI - 1).astype(F32)

        def lane_chunk(c, carry):
            lo = pl.ds(pl.multiple_of(c * LANES, LANES), LANES)
            lg, lu = lo, pl.ds(pl.multiple_of(TC + c * LANES, LANES), LANES)
            wg, wu = _bcast_taps(wg_ref, K, lo), _bcast_taps(wu_ref, K, lo)
            for r0 in range(0, TM + H, FFN_ROWS):
                if r0 < TM:
                    rows = pl.ds(r0, FFN_ROWS)
                    g, u, da = cc_ref[rows, lg], cc_ref[rows, lu], dc_ref[rows, lo].astype(F32)
                else:
                    g, u, da = cn_ref[:, lg], cn_ref[:, lu], dn_ref[:, lo].astype(F32) * last
                g, u = g.astype(F32), u.astype(F32)
                s = _sigmoid(g)
                ext_d[pl.ds(r0, FFN_ROWS), lg] = da * u * s * (1.0 + g * (1.0 - s))
                ext_d[pl.ds(r0, FFN_ROWS), lu] = da * g * s
            sums_g, sums_u = [None] * K, [None] * K
            for r0 in range(0, TM, FFN_ROWS):
                xg = xc_ref[pl.ds(r0, FFN_ROWS), lg].astype(F32)
                xu = xc_ref[pl.ds(r0, FFN_ROWS), lu].astype(F32)
                g = u = None
                for k in range(K):
                    rows = pl.ds(K - 1 - k + r0, FFN_ROWS)
                    dg, du = ext_d[rows, lg], ext_d[rows, lu]
                    tg, tu = dg * wg[k], du * wu[k]
                    g, u = (tg, tu) if g is None else (g + tg, u + tu)
                    pg, pu = xg * dg, xu * du
                    sums_g[k] = pg if sums_g[k] is None else sums_g[k] + pg
                    sums_u[k] = pu if sums_u[k] is None else sums_u[k] + pu
                o_ref[pl.ds(r0, FFN_ROWS), lg] = g.astype(BF16)
                o_ref[pl.ds(r0, FFN_ROWS), lu] = u.astype(BF16)
            for k in range(K):
                dwg_ref[pl.ds(k, 1), lo] += _rsum(sums_g[k])
                dwu_ref[pl.ds(k, 1), lo] += _rsum(sums_u[k])
            return carry
        lax.fori_loop(0, TC // LANES, lane_chunk, 0)

    blocks = [((TM, 2 * TC), BF16), ((TM, 2 * TC), BF16), ((H, 2 * TC), BF16), ((TM, TC), BF16), ((H, TC), BF16),
              ((K, TC), F32), ((K, TC), F32), ((TM, 2 * TC), BF16), ((K, TC), F32), ((K, TC), F32)]
    scratch = [((TM + H, 2 * TC), F32)]
    nxt = lambda j, i: (jnp.minimum((i + 1) * (TM // H), S // H - 1), j)
    tile = pl.BlockSpec((TM, 2 * TC), lambda j, i: (i, j))
    taps_out = pl.BlockSpec((K, TC), lambda j, i: (0, j))
    outs, side_outs = _call(
        body, side, name="ffn_bwd", grid=(nF, nI),
        in_specs=[tile, tile, pl.BlockSpec((H, 2 * TC), nxt),
                  pl.BlockSpec((TM, TC), lambda j, i: (i, j)), pl.BlockSpec((H, TC), nxt),
                  pl.BlockSpec((K, TC), lambda j, i: (0, j)), pl.BlockSpec((K, TC), lambda j, i: (0, j + nF))],
        out_specs=[tile, taps_out, taps_out],
        out_shape=[jax.ShapeDtypeStruct((S, 2 * F), BF16), jax.ShapeDtypeStruct((K, F), F32),
                   jax.ShapeDtypeStruct((K, F), F32)],
        scratch=scratch,
        params=_params(("arbitrary", "arbitrary"), blocks, scratch),
        args=[u0, conv_u0, conv_u0, dact, dact, conv_w, conv_w])
    return outs if side is None else (outs, side_outs)


def _mixer_bwd_ln(dcat, a1, ln_g, ln_b, S, TM, A):
    def body(dc_ref, a1_ref, g_ref, b_ref, da1_ref, acc_ref):
        @pl.when(pl.program_id(0) == 0)
        def _():
            acc_ref[...] = jnp.zeros_like(acc_ref)
        a1 = a1_ref[...]
        g = g_ref[...]
        mu = jnp.mean(a1, axis=-1, keepdims=True)
        d = a1 - mu
        rstd = lax.rsqrt(jnp.mean(d * d, axis=-1, keepdims=True) + EPS)
        nh = d * rstd
        a2 = nh * g + b_ref[...]
        s = _sigmoid(a2)
        da2 = dc_ref[...].astype(F32) * s * (1.0 + a2 * (1.0 - s))
        dnh = da2 * g
        da1 = rstd * (dnh - jnp.mean(dnh, axis=-1, keepdims=True)
                      - nh * jnp.mean(dnh * nh, axis=-1, keepdims=True))
        da1_ref[...] = da1
        acc_ref[pl.ds(0, 1), :] += _rsum(da2 * nh)
        acc_ref[pl.ds(1, 1), :] += _rsum(da2)
        acc_ref[pl.ds(2, 1), :] += _rsum(da1)

    blocks = [((TM, A), BF16), ((TM, A), F32), ((TM, A), F32), ((4, A), F32)]
    return pl.pallas_call(
        body, name="mixer_bwd_ln", grid=(S // TM,),
        in_specs=[pl.BlockSpec((TM, A), lambda i: (i, 0)), pl.BlockSpec((TM, A), lambda i: (i, 0)),
                  pl.BlockSpec((1, A), lambda i: (0, 0)), pl.BlockSpec((1, A), lambda i: (0, 0))],
        out_specs=[pl.BlockSpec((TM, A), lambda i: (i, 0)), pl.BlockSpec((4, A), lambda i: (0, 0))],
        out_shape=[jax.ShapeDtypeStruct((S, A), F32), jax.ShapeDtypeStruct((4, A), F32)],
        compiler_params=_params(("arbitrary",), blocks, temps=[((TM, A), F32)] * 12),
    )(dcat, a1, ln_g, ln_b)


def _mixer_bwd_conv(z, dcat, da1, conv_a_w, conv_b_w, S, TM, A, side=None):
    H = 32
    KA, KB = conv_a_w.shape[0], conv_b_w.shape[0]
    nI = S // TM
    n_chunks = A // LANES
    RB = _pick(TM, (64, 32))

    def body(zc_ref, zp_ref, zn_ref, dbc_ref, dbn_ref, d1c_ref, d1n_ref, wa_ref, wb_ref,
             dz_ref, dwa_ref, dwb_ref, ext_a0, ext_d1, ext_cb, ext_dc, da0_sc, shifted_d, shifted_a):
        i = pl.program_id(0)
        @pl.when(i == 0)
        def _():
            dwa_ref[...] = jnp.zeros_like(dwa_ref)
            dwb_ref[...] = jnp.zeros_like(dwb_ref)
        first = (i > 0).astype(F32)
        last = (i < nI - 1).astype(F32)
        zc = zc_ref[...].astype(F32)
        zp = zp_ref[...].astype(F32) * first
        a_val, a_gate = zc[:, 0:A], zc[:, A:2 * A]
        b_gate, c_gate, b_h = zc[:, 2 * A:3 * A], zc[:, 3 * A:4 * A], zc[:, 4 * A:5 * A]
        sig = _sigmoid(a_gate)
        ext_a0[pl.ds(0, H), :] = zp[:, 0:A] * _sigmoid(zp[:, A:2 * A])
        ext_a0[pl.ds(H, TM), :] = a_val * sig
        ext_d1[pl.ds(0, TM), :] = d1c_ref[...]
        ext_d1[pl.ds(TM, H), :] = d1n_ref[...] * last
        ext_cb[pl.ds(0, H), :] = zp[:, 3 * A:4 * A] * zp[:, 4 * A:5 * A]
        ext_cb[pl.ds(H, TM), :] = c_gate * b_h
        dbx = dbc_ref[...].astype(F32)
        dcbc = dbx * b_gate
        ext_dc[pl.ds(0, TM), :] = dcbc
        ext_dc[pl.ds(TM, H), :] = dbn_ref[...].astype(F32) * zn_ref[...].astype(F32) * last

        def chunk(c, carry):
            cs = pl.ds(pl.multiple_of(c * LANES, LANES), LANES)
            _shift_copies(ext_d1, shifted_d, cs)
            _shift_copies(ext_a0, shifted_a, cs)
            for r0 in range(0, TM, RB):
                acc = None
                for k in range(KA):
                    term = _rows_at(ext_d1, shifted_d, KA - 1 - k + r0, RB, cs) * wa_ref[pl.ds(k, 1), cs]
                    acc = term if acc is None else acc + term
                da0_sc[pl.ds(r0, RB), cs] = acc
            for k in range(KA):
                acc = None
                for r0 in range(0, TM, RB):
                    term = (_rows_at(ext_a0, shifted_a, H - (KA - 1) + k + r0, RB, cs)
                            * ext_d1[pl.ds(r0, RB), cs])
                    acc = term if acc is None else acc + term
                dwa_ref[pl.ds(k, 1), cs] += _rsum(acc)
            return carry
        lax.fori_loop(0, n_chunks, chunk, 0)

        da0 = da0_sc[...]
        dz_ref[:, 0:A] = (da0 * sig).astype(BF16)
        dz_ref[:, A:2 * A] = (da0 * a_val * sig * (1.0 - sig)).astype(BF16)
        cbc = _taps_causal(ext_cb, wb_ref, KB, H, TM, slice(None))
        dz_ref[:, 2 * A:3 * A] = (dbx * cbc).astype(BF16)
        dcb = _taps_anticausal(ext_dc, wb_ref, KB, TM, slice(None))
        dz_ref[:, 3 * A:4 * A] = (dcb * b_h).astype(BF16)
        dz_ref[:, 4 * A:5 * A] = (dcb * c_gate).astype(BF16)
        grads = _tap_grads(ext_cb, dcbc, KB, H, TM, slice(None))
        for k in range(KB):
            dwb_ref[pl.ds(k, 1), :] += grads[k]

    blocks = [((TM, 5 * A), BF16), ((H, 5 * A), BF16), ((H, A), BF16), ((TM, A), BF16), ((H, A), BF16),
              ((TM, A), F32), ((H, A), F32), ((KA, A), F32), ((KB, A), F32),
              ((TM, 5 * A), BF16), ((KA, A), F32), ((KB, A), F32)]
    scratch = ([((H + TM, A), F32)] * 4 + [((TM, A), F32)]
               + [((SUBLANES - 1, H + TM - SUBLANES, LANES), F32)] * 2)
    vec = lambda r: pl.BlockSpec((r, A), lambda i: (0, 0))
    outs, side_outs = _call(
        body, side, name="mixer_bwd_conv", grid=(nI,),
        in_specs=[pl.BlockSpec((TM, 5 * A), lambda i: (i, 0)),
                  pl.BlockSpec((H, 5 * A), _prev_rows(TM, H, 0)),
                  pl.BlockSpec((H, A), _next_rows(S, TM, H, 2)),
                  pl.BlockSpec((TM, A), lambda i: (i, 1)),
                  pl.BlockSpec((H, A), _next_rows(S, TM, H, 1)),
                  pl.BlockSpec((TM, A), lambda i: (i, 0)),
                  pl.BlockSpec((H, A), _next_rows(S, TM, H, 0)),
                  vec(KA), vec(KB)],
        out_specs=[pl.BlockSpec((TM, 5 * A), lambda i: (i, 0)), vec(KA), vec(KB)],
        out_shape=[jax.ShapeDtypeStruct((S, 5 * A), BF16), jax.ShapeDtypeStruct((KA, A), F32),
                   jax.ShapeDtypeStruct((KB, A), F32)],
        scratch=scratch,
        params=_params(("arbitrary",), blocks, scratch, temps=[((TM, 5 * A), F32)] * 2 + [((TM, A), F32)] * 14),
        args=[z, z, z, dcat, dcat, da1, da1, conv_a_w, conv_b_w])
    return outs if side is None else (outs, side_outs)


def _row_tile(R):
    return _pick(R, (256, 128, 64, 32, 16, 8))


def _scalars(*vals):
    return jnp.stack([jnp.asarray(v, jnp.int32) for v in vals])


def _cast_into_gathered(name, w, chip):
    R, C = w.shape
    TR = _row_tile(R)

    def body(s_ref, w_ref, o_ref):
        o_ref[...] = w_ref[...].astype(BF16)

    grid_spec = pltpu.PrefetchScalarGridSpec(
        num_scalar_prefetch=1, grid=(R // TR,),
        in_specs=[pl.BlockSpec((TR, C), lambda r, s: (r, 0))],
        out_specs=pl.BlockSpec((None, TR, C), lambda r, s: (s[0], r, 0)))
    return pl.pallas_call(body, name=name, grid_spec=grid_spec,
                          out_shape=jax.ShapeDtypeStruct((N_CHIPS, R, C), BF16),
                          compiler_params=_params(("arbitrary",), [((TR, C), F32), ((TR, C), BF16)]),
                          )(_scalars(chip), w)


def _add_pair(name, dw, recv, c):
    _, _, Rh, C = dw.shape
    TR = _row_tile(Rh)

    def body(c_ref, a_ref, b_ref, o_ref, ob_ref):
        s = a_ref[...] + b_ref[...]
        o_ref[...] = s
        ob_ref[...] = s.astype(BF16)

    out_spec = pl.BlockSpec((None, TR, C), lambda k, r, c_ref: (k, r, 0))
    grid_spec = pltpu.PrefetchScalarGridSpec(
        num_scalar_prefetch=1, grid=(N_CHIPS, Rh // TR),
        in_specs=[pl.BlockSpec((None, None, TR, C), lambda k, r, c_ref: (k, c_ref[0], r, 0)),
                  pl.BlockSpec((None, TR, C), lambda k, r, c_ref: (k, r, 0))],
        out_specs=[out_spec, out_spec])
    return pl.pallas_call(body, name=name, grid_spec=grid_spec,
                          out_shape=[jax.ShapeDtypeStruct((N_CHIPS, Rh, C), F32),
                                     jax.ShapeDtypeStruct((N_CHIPS, Rh, C), BF16)],
                          compiler_params=_params(("arbitrary", "arbitrary"), [((TR, C), F32)] * 4),
                          )(_scalars(c), dw, recv)


def _add_chips(name, parts, recv, chip, c):
    _, Rh, C = parts.shape
    TR = _row_tile(Rh)

    def body(s_ref, p_ref, r_ref, o_ref):
        o_ref[...] = ((p_ref[...] + r_ref[0].astype(F32)) + r_ref[1].astype(F32)) + r_ref[2].astype(F32)

    grid_spec = pltpu.PrefetchScalarGridSpec(
        num_scalar_prefetch=1, grid=(Rh // TR,),
        in_specs=[pl.BlockSpec((None, TR, C), lambda r, s: (s[0], r, 0)),
                  pl.BlockSpec((N_CHIPS - 1, TR, C), lambda r, s: (0, r, 0))],
        out_specs=pl.BlockSpec((None, TR, C), lambda r, s: (s[1], r, 0)))
    return pl.pallas_call(body, name=name, grid_spec=grid_spec,
                          out_shape=jax.ShapeDtypeStruct((2, Rh, C), F32),
                          compiler_params=_params(("arbitrary",), [((N_CHIPS + 1, TR, C), F32)]),
                          )(_scalars(chip, c), parts, recv)


def _sum_devices(name, parts):
    _, R, C = parts.shape

    def body(p_ref, o_ref):
        acc = p_ref[0]
        for d in range(1, N_DEV):
            acc = acc + p_ref[d]
        o_ref[...] = acc

    return pl.pallas_call(body, name=name, out_shape=jax.ShapeDtypeStruct((R, C), F32),
                          in_specs=[pl.BlockSpec(memory_space=pltpu.VMEM)],
                          out_specs=pl.BlockSpec(memory_space=pltpu.VMEM))(parts)


def _adamw(name, w, g, m, v, copy_grad=False):
    R, C = w.shape
    TR = _pick(R, (128, 64, 32, 16, 8))
    c1 = 1.0 - ADAM_B1 ** ADAM_STEP
    c2 = 1.0 - ADAM_B2 ** ADAM_STEP
    n_out = 4 if copy_grad else 3

    def body(w_ref, g_ref, m_ref, v_ref, d_ref, nm_ref, nv_ref, *g_out):
        g_ = g_ref[...]
        nm = ADAM_B1 * m_ref[...] + (1.0 - ADAM_B1) * g_
        nv = ADAM_B2 * v_ref[...] + (1.0 - ADAM_B2) * (g_ * g_)
        d_ref[...] = -ADAM_LR * ((nm / c1) / (jnp.sqrt(nv / c2) + ADAM_EPS) + ADAM_WD * w_ref[...])
        nm_ref[...] = nm
        nv_ref[...] = nv
        for ref in g_out:
            ref[...] = g_

    spec = pl.BlockSpec((TR, C), lambda r: (r, 0))
    shp = jax.ShapeDtypeStruct((R, C), F32)
    return pl.pallas_call(body, name=name, grid=(R // TR,), in_specs=[spec] * 4, out_specs=[spec] * n_out,
                          out_shape=[shp] * n_out,
                          compiler_params=_params(("arbitrary",), [((TR, C), F32)] * (4 + n_out)))(w, g, m, v)


def _place():
    x, y, c = lax.axis_index("x"), lax.axis_index("y"), lax.axis_index("c")
    others = [(1 - x, y), (x, 1 - y), (1 - x, 1 - y)]
    return x, y, c, others


def _allgather_small(name, block):
    R, C = block.shape

    def body(x_ref, out_ref, send_sems, recv_sems, local_sem):
        x, y, c, chips = _place()
        me, sibling = (x, y, c), (x, y, 1 - c)

        def rows(px, py, pc):
            return out_ref.at[4 * px + 2 * py + pc]

        def copy(k, blk, to, src=None):
            return pltpu.make_async_remote_copy(
                src_ref=rows(*blk) if src is None else src, dst_ref=rows(*blk),
                send_sem=send_sems.at[k], recv_sem=recv_sems.at[k], device_id=to, device_id_type=MESH)

        mine = pltpu.make_async_copy(x_ref, rows(*me), local_sem)
        mine.start()
        first = [copy(0, me, sibling, src=x_ref)]
        first += [copy(1 + j, me, (*chip, c), src=x_ref) for j, chip in enumerate(chips)]
        for cp in first:
            cp.start()
        passed = [copy(4 + j, (*chip, c), sibling) for j, chip in enumerate(chips)]
        for j, chip in enumerate(chips):
            copy(1 + j, (*chip, c), me).wait_recv()
            passed[j].start()
        copy(0, sibling, me).wait_recv()
        for j, chip in enumerate(chips):
            copy(4 + j, (*chip, 1 - c), me).wait_recv()
        for cp in first + passed:
            cp.wait_send()
        mine.wait()

    return pl.pallas_call(
        body, name=name, out_shape=jax.ShapeDtypeStruct((N_DEV, R, C), F32),
        in_specs=[pl.BlockSpec(memory_space=pltpu.VMEM)], out_specs=pl.BlockSpec(memory_space=pltpu.VMEM),
        scratch_shapes=[pltpu.SemaphoreType.DMA((7,)), pltpu.SemaphoreType.DMA((7,)), pltpu.SemaphoreType.DMA],
    )(block)


def _gather_side(bufs, across, within):
    def rows(ref, chip, half, piece):
        _, r0, n = piece
        return ref.at[2 * chip[0] + chip[1], pl.ds(half * (ref.shape[1] // 2) + r0, n)]

    def copies(ins, outs, send_sems, recv_sems, base):
        x, y, c, chips = _place()
        sibling = (x, y, 1 - c)
        pairs = []

        def add(k, src, dst, to, arrival):
            mk = lambda s, d, dev: pltpu.make_async_remote_copy(
                src_ref=s, dst_ref=d, send_sem=send_sems.at[base + k], recv_sem=recv_sems.at[base + k],
                device_id=dev, device_id_type=MESH)
            pairs.append((mk(src, dst, to), mk(arrival, arrival, (x, y, c))))

        for p, piece in enumerate(across):
            ref = outs[piece[0]]
            for j, chip in enumerate(chips):
                mine = rows(ref, (x, y), c, piece)
                add(3 * p + j, mine, mine, (*chip, c), rows(ref, chip, c, piece))
        for q, piece in enumerate(within):
            ref = outs[piece[0]]
            for j, chip in enumerate(chips):
                held = rows(ref, chip, c, piece)
                add(3 * (len(across) + q) + j, held, held, sibling, rows(ref, chip, 1 - c, piece))
        return pairs

    def start(*refs):
        for send, _ in copies(*refs):
            send.start()

    def wait(*refs):
        pairs = copies(*refs)
        for _, arrival in pairs:
            arrival.wait_recv()
        for send, _ in pairs:
            send.wait_send()

    return _Side(list(bufs), [jax.ShapeDtypeStruct(b.shape, b.dtype) for b in bufs],
                 3 * (len(across) + len(within)), start, wait, aliases=tuple((i, i) for i in range(len(bufs))))


def _chip_exchange(parts):
    n = len(parts)

    def copies(ins, outs, send_sems, recv_sems, base):
        x, y, c, chips = _place()
        return [pltpu.make_async_remote_copy(
            src_ref=ins[a].at[2 * chip[0] + chip[1]], dst_ref=outs[a].at[j],
            send_sem=send_sems.at[base + 3 * a + j], recv_sem=recv_sems.at[base + 3 * a + j],
            device_id=(*chip, c), device_id_type=MESH) for a in range(n) for j, chip in enumerate(chips)]

    return _Side(list(parts), [jax.ShapeDtypeStruct((N_CHIPS - 1,) + p.shape[1:], p.dtype) for p in parts],
                 3 * n, *_start_wait(copies))


def _pair_exchange(grads):
    def copies(ins, outs, send_sems, recv_sems, base):
        x, y, c, _ = _place()
        return [pltpu.make_async_remote_copy(
            src_ref=ins[a].at[:, 1 - c], dst_ref=outs[a], send_sem=send_sems.at[base + a],
            recv_sem=recv_sems.at[base + a], device_id=(x, y, 1 - c), device_id_type=MESH)
            for a in range(len(grads))]

    return _Side(list(grads), [jax.ShapeDtypeStruct((N_CHIPS,) + g.shape[2:], F32) for g in grads],
                 len(grads), *_start_wait(copies))


def _start_wait(copies):
    def start(*refs):
        for cp in copies(*refs):
            cp.start()

    def wait(*refs):
        cps = copies(*refs)
        for cp in cps:
            cp.wait_recv()
        for cp in cps:
            cp.wait_send()
    return start, wait


def _both(first, second):
    n_in, n_out = len(first.ins), len(first.out_shapes)

    def run(which):
        def go(ins, outs, send_sems, recv_sems, base):
            getattr(first, which)(ins[:n_in], outs[:n_out], send_sems, recv_sems, base)
            getattr(second, which)(ins[n_in:], outs[n_out:], send_sems, recv_sems, base + first.n_sems)
        return go

    aliases = first.aliases + tuple((a + n_in, b + n_out) for a, b in second.aliases)
    return _Side(first.ins + second.ins, first.out_shapes + second.out_shapes,
                 first.n_sems + second.n_sems, run("start"), run("wait"), aliases)


def _share_halves(halves):
    def copies(ins, outs, send_sems, recv_sems, base):
        x, y, c, _ = _place()
        pairs = []
        for a in range(len(halves)):
            mk = lambda s, d, dev, a=a: pltpu.make_async_remote_copy(
                src_ref=s, dst_ref=d, send_sem=send_sems.at[base + a], recv_sem=recv_sems.at[base + a],
                device_id=dev, device_id_type=MESH)
            theirs = outs[a].at[1 - c]
            pairs.append((mk(outs[a].at[c], outs[a].at[c], (x, y, 1 - c)), mk(theirs, theirs, (x, y, c))))
        return pairs

    def start(*refs):
        for send, _ in copies(*refs):
            send.start()

    def wait(*refs):
        pairs = copies(*refs)
        for _, arrival in pairs:
            arrival.wait_recv()
        for send, _ in pairs:
            send.wait_send()

    return _Side(list(halves), [jax.ShapeDtypeStruct(h.shape, F32) for h in halves], len(halves), start, wait,
                 aliases=tuple((i, i) for i in range(len(halves))))


def _pack(arrays):
    pieces = []
    for a in arrays:
        flat = a.reshape(-1).astype(F32)
        pieces.append(jnp.pad(flat, (0, (-flat.size) % PACK_ALIGN)))
    return jnp.concatenate(pieces).reshape(-1, LANES)


def _unpack(buf, shapes):
    lead = buf.shape[:-2]
    flat = buf.reshape(lead + (-1,))
    out, off = [], 0
    for shp in shapes:
        size = 1
        for s in shp:
            size *= s
        out.append(flat[..., off:off + size].reshape(lead + tuple(shp)))
        off += size + (-size) % PACK_ALIGN
    return out


def _gather_channels(buf, shapes):
    per_chip = _unpack(buf[0::2], shapes)
    return [jnp.transpose(a, (1, 0, 2)).reshape(a.shape[1], -1) for a in per_chip]


def _mm_tile(n, rows, limit_bytes=6 * 1024 * 1024):
    for t in (1408, 1280, 1024, 640, 512, 384, 256, 128):
        if n % t == 0 and rows * t * 2 <= limit_bytes:
            return t
    raise ValueError(f"no column tile for {n} x {rows}")


def kernel(x, p, norm_mix_g, w_in, conv_a_w, conv_a_b, ln_a_g, ln_a_b, conv_b_w, w_out, norm_ffn_g, w_up, conv_ffn_w, w_down, w_ple_gate, b_ple_gate, w_ple_proj, norm_final_g, loss_target, m_norm_mix_g, m_w_in, m_conv_a_w, m_conv_a_b, m_ln_a_g, m_ln_a_b, m_conv_b_w, m_w_out, m_norm_ffn_g, m_w_up, m_conv_ffn_w, m_w_down, m_w_ple_gate, m_b_ple_gate, m_w_ple_proj, m_norm_final_g, v_norm_mix_g, v_w_in, v_conv_a_w, v_conv_a_b, v_ln_a_g, v_ln_a_b, v_conv_b_w, v_w_out, v_norm_ffn_g, v_w_up, v_conv_ffn_w, v_w_down, v_w_ple_gate, v_b_ple_gate, v_w_ple_proj, v_norm_final_g):
    S, D = x.shape[1], x.shape[2]
    P = p.shape[3]
    A = conv_a_b.shape[1]
    F = w_down.shape[1] * N_CHIPS
    KA, KB, KF = conv_a_w.shape[1], conv_b_w.shape[1], conv_ffn_w.shape[1]
    xi, yi, ci = lax.axis_index("x"), lax.axis_index("y"), lax.axis_index("c")
    chip = 2 * xi + yi

    TM = _pick(S, (512, 256, 128))
    TE = _pick(S, (256, 128))
    TC = _pick(2 * F // N_CHIPS, (1408, 1024, 512, 256, 128))
    ffn_place = _pair_tile(F // TC)

    x2, p2, t2 = x.reshape(S, D), p.reshape(S, P), loss_target.reshape(S, D)
    gfin = norm_final_g.reshape(1, D)

    big = dict(w_in=w_in[0], w_out=w_out[0], w_up=w_up[0], w_down=w_down[0],
               w_ple_gate=w_ple_gate[0], w_ple_proj=w_ple_proj[0])
    names = list(big)
    buf = {n: _cast_into_gathered("cast_" + n, big[n], chip) for n in names}
    half = {n: big[n].shape[0] // 2 for n in names}
    up_a = half["w_up"] // 2
    (w_in3,) = _comm_only("gather_w_in_across", _gather_side([buf["w_in"]], [(0, 0, half["w_in"])], []))
    (w_in3,) = _comm_only("gather_w_in_within", _gather_side([w_in3], [], [(0, 0, half["w_in"])]))

    tap_shapes = [(KA, A // N_CHIPS), (KB, A // N_CHIPS), (KF, 2 * F // N_CHIPS)]
    taps = _allgather_small("allgather_taps", _pack([conv_a_w[0], conv_b_w[0], conv_ffn_w[0]]))
    conv_a_f, conv_b_f, conv_ffn_f = _gather_channels(taps, tap_shapes)

    def rms_prologue(rows, row_r, vec_r, ro_r, ao_r):
        h = row_r[0][rows, :]
        hn = (h * _rms_stats(h) * vec_r[0][...]).astype(BF16)
        ro_r[0][rows, :] = hn
        return [hn]

    def cast_prologue(rows, row_r, vec_r, ro_r, ao_r):
        hb = row_r[0][rows, :].astype(BF16)
        ro_r[0][rows, :] = hb
        return [hb]

    plain = lambda accs, tile_r, cv_r: [accs[0]]
    residual = lambda accs, tile_r, cv_r: [tile_r[0][...] + accs[0]]

    (z, hn1), (w_out_t, w_up_t) = _rows_mm(
        "in_proj", S, TM, 5 * A, _mm_tile(5 * A // N_CHIPS, D), row_ins=[x2], vec_ins=[norm_mix_g],
        weights=[(w_in3, "nn3")], tile_outs=[BF16], row_outs=[(D, BF16)], prologue=rms_prologue, epilogue=plain,
        side=_gather_side([buf["w_out"], buf["w_up"]], [(0, 0, half["w_out"]), (1, 0, up_a)], []))
    (a1, cat), (w_out3, w_up_t) = _mixer_fwd(
        z, conv_a_f, conv_a_b, ln_a_g, ln_a_b, conv_b_f, S, TE, A,
        side=_gather_side([w_out_t, w_up_t], [(1, up_a, half["w_up"] - up_a)],
                          [(0, 0, half["w_out"]), (1, 0, up_a)]))
    w_out_f = w_out3.reshape(2 * A, D)
    (h1,), (w_up3, w_proj_t) = _rows_mm(
        "out_proj", S, TM, D, _mm_tile(D, 2 * A), row_ins=[cat], weights=[(w_out_f, "nn2")],
        tile_ins=[x2], tile_outs=[F32], epilogue=residual,
        side=_gather_side([w_up_t, buf["w_ple_proj"]], [(1, 0, half["w_ple_proj"])],
                          [(0, up_a, half["w_up"] - up_a)]))
    (u0, hn2), (w_down_t, w_gate_t, w_proj3) = _rows_mm(
        "up_proj", S, TM, 2 * F, TC, row_ins=[h1], vec_ins=[norm_ffn_g],
        weights=[(w_up3, "nn3")], tile_outs=[BF16], row_outs=[(D, BF16)],
        prologue=rms_prologue, epilogue=plain, place=ffn_place,
        side=_gather_side([buf["w_down"], buf["w_ple_gate"], w_proj_t],
                          [(0, 0, half["w_down"]), (1, 0, half["w_ple_gate"])], [(2, 0, half["w_ple_proj"])]))
    (act, conv_u0), (w_down3, w_gate3) = _ffn_act(
        u0, conv_ffn_f, S, TM, F, TC,
        side=_gather_side([w_down_t, w_gate_t], [], [(0, 0, half["w_down"]), (1, 0, half["w_ple_gate"])]))
    w_down_f = w_down3.reshape(F, D)
    w_gate_f = w_gate3.reshape(D, D)
    w_proj_f = jnp.transpose(w_proj3, (1, 0, 2)).reshape(P, D)
    (h2,) = _rows_mm("down_proj", S, TM, D, _mm_tile(D, F), row_ins=[act], weights=[(w_down_f, "nn2")],
                     tile_ins=[h1], tile_outs=[F32], epilogue=residual)

    def ple_prologue(rows, row_r, vec_r, ro_r, ao_r):
        hb = row_r[0][rows, :].astype(BF16)
        pb = row_r[1][rows, :].astype(BF16)
        ro_r[0][rows, :] = hb
        ro_r[1][rows, :] = pb
        return [hb, pb]

    def ple_epilogue(accs, tile_r, cv_r):
        gate = _sigmoid(accs[0] + cv_r[0][...])
        return [tile_r[0][...] + accs[1] * gate, gate, accs[1]]

    h3, gate, pp, h2b, pb = _rows_mm(
        "ple_fwd", S, TM, D, _mm_tile(D, D), row_ins=[h2, p2], colvec_ins=[b_ple_gate],
        weights=[(w_gate_f, "nn2"), (w_proj_f, "nn2")], tile_ins=[h2], tile_outs=[F32, BF16, BF16],
        row_outs=[(D, BF16), (P, BF16)], prologue=ple_prologue, epilogue=ple_epilogue)
    loss_part, g_norm_final, dh3, dpre, dpp, g_b_gate = _loss_head(h3, t2, gfin, pp, gate, S, TE, D)

    TK = _pick(S, (1024, 512, 256, 128))
    wt = lambda n: _pick(n, (1408, 1280, 1024, 512, 256, 128))
    chip_sums, from_chips = {}, {}

    def to_sibling(parts):
        ns = list(parts)
        halves = [parts[n].reshape(N_CHIPS, 2, big[n].shape[0] // 2, big[n].shape[1]) for n in ns]
        return ns, halves, _pair_exchange(halves)

    def to_chips(ns, halves, from_sibling):
        sums = [_add_pair("pair_sum_" + n, h, r, ci) for n, h, r in zip(ns, halves, from_sibling)]
        for n, (s, _) in zip(ns, sums):
            chip_sums[n] = s
        return ns, _chip_exchange([b for _, b in sums])

    def landed(ns, side_outs):
        for n, r in zip(ns, side_outs):
            from_chips[n] = r

    ns, halves, side = to_sibling(dict(
        w_ple_gate=_tn_mm("dw_ple_gate", h2b, dpre, wt(D), wt(D), TK),
        w_ple_proj=_tn_mm("dw_ple_proj", pb, dpp, wt(P), wt(D // N_CHIPS), TK, cols_per_chip=D // N_CHIPS)))
    (dh2,), got = _rows_mm("ple_bwd", S, TM, D, _mm_tile(D, D), row_ins=[dpre], weights=[(w_gate_f, "nt2")],
                           tile_ins=[dh3], tile_outs=[F32], epilogue=residual, side=side)
    ple_ns, ple_chips = to_chips(ns, halves, got)
    (dact, dh2b), got = _rows_mm("down_bwd", S, TM, F, _mm_tile(F, D), row_ins=[dh2], weights=[(w_down_f, "nt2")],
                                 tile_outs=[BF16], row_outs=[(D, BF16)], prologue=cast_prologue, epilogue=plain,
                                 side=ple_chips)
    landed(ple_ns, got)
    ns, halves, side = to_sibling(dict(w_down=_tn_mm("dw_down", act, dh2b, wt(F), wt(D), TK)))
    (du0, g_conv_gate, g_conv_up), got = _ffn_bwd(u0, conv_u0, dact, conv_ffn_f, S, TM, F, TC, side=side)
    down_ns, down_chips = to_chips(ns, halves, got)
    g_conv_ffn = jnp.concatenate([g_conv_gate, g_conv_up], axis=1)
    ns, halves, side = to_sibling(dict(
        w_up=_tn_mm("dw_up", hn2, du0, wt(D), TC, TK, cols_per_chip=2 * F // N_CHIPS, place=ffn_place)))

    def up_bwd_epilogue(acc, rows, row_r, vec_r, ro_r, ao_r):
        dh, dg = _rms_bwd(row_r[0][rows, :], vec_r[0][...], acc)
        dh1_ = row_r[1][rows, :] + dh
        ro_r[0][rows, :] = dh1_
        ro_r[1][rows, :] = dh1_.astype(BF16)
        ao_r[0][...] += dg

    (dh1, dh1b, g_norm_ffn), got = _kloop_mm(
        "up_bwd", S, TM, du0, w_up3, TC, row_ins=[h1, dh2], vec_ins=[norm_ffn_g],
        row_outs=[(D, F32), (D, BF16)], acc_outs=[(1, D)], epilogue=up_bwd_epilogue, place=ffn_place,
        side=_both(down_chips, side))
    landed(down_ns, got[:len(down_ns)])
    up_ns, up_chips = to_chips(ns, halves, got[len(down_ns):])
    ns, halves, side = to_sibling(dict(w_out=_tn_mm("dw_out", cat, dh1b, wt(2 * A), wt(D), TK)))
    (dcat,), got = _rows_mm("out_bwd", S, TM, 2 * A, _mm_tile(2 * A, D), row_ins=[dh1b],
                            weights=[(w_out_f, "nt2")], tile_outs=[BF16], epilogue=plain, side=side)
    out_ns, out_chips = to_chips(ns, halves, got)
    da1, ln_sums = _mixer_bwd_ln(dcat, a1, ln_a_g, ln_a_b, S, TE, A)
    (dz, g_conv_a, g_conv_b), got = _mixer_bwd_conv(z, dcat, da1, conv_a_f, conv_b_f, S, TE, A,
                                                    side=_both(up_chips, out_chips))
    landed(up_ns + out_ns, got)
    ns, halves, side = to_sibling(dict(
        w_in=_tn_mm("dw_in", hn1, dz, wt(D), wt(5 * A // N_CHIPS), TK, cols_per_chip=5 * A // N_CHIPS)))
    ns, side = to_chips(ns, halves, _comm_only("grads_exchange_pairs_in", side))

    def in_bwd_epilogue(acc, rows, row_r, vec_r, ro_r, ao_r):
        dh, dg = _rms_bwd(row_r[0][rows, :], vec_r[0][...], acc)
        ro_r[0][rows, :] = row_r[1][rows, :] + dh
        ao_r[0][...] += dg

    early = [n for n in names if n != "w_in"]
    early_halves = [_add_chips("chip_sum_" + n, chip_sums[n], from_chips[n], chip, ci) for n in early]
    (dx, g_norm_mix), got = _kloop_mm(
        "in_bwd", S, TM, dz, w_in3, _mm_tile(5 * A // N_CHIPS, D), row_ins=[x2, dh1],
        vec_ins=[norm_mix_g], row_outs=[(D, F32)], acc_outs=[(1, D)], epilogue=in_bwd_epilogue,
        side=_both(side, _share_halves(early_halves)))
    landed(ns, got[:1])
    shared = dict(zip(early, got[1:]))
    (shared["w_in"],) = _comm_only("grads_share_w_in", _share_halves(
        [_add_chips("chip_sum_w_in", chip_sums["w_in"], from_chips["w_in"], chip, ci)]))

    reduced = [shared[n] for n in names]
    moments = dict(w_in=(m_w_in, v_w_in), w_out=(m_w_out, v_w_out), w_up=(m_w_up, v_w_up),
                   w_down=(m_w_down, v_w_down), w_ple_gate=(m_w_ple_gate, v_w_ple_gate),
                   w_ple_proj=(m_w_ple_proj, v_w_ple_proj))
    grads, deltas, new_m, new_v = {}, {}, {}, {}
    for n, g in zip(names, reduced):
        d_, m_, v_, g = _adamw("adamw_" + n, big[n], g.reshape(big[n].shape), moments[n][0][0], moments[n][1][0],
                               copy_grad=True)
        grads[n], deltas[n], new_m[n], new_v[n] = g[None], d_[None], m_[None], v_[None]

    small = ["norm_mix_g", "conv_a_w", "conv_a_b", "ln_a_g", "ln_a_b", "conv_b_w", "norm_ffn_g",
             "conv_ffn_w", "b_ple_gate", "norm_final_g"]
    small_part = [g_norm_mix, g_conv_a, ln_sums[2:3], ln_sums[0:1], ln_sums[1:2], g_conv_b, g_norm_ffn,
                  g_conv_ffn, g_b_gate, g_norm_final]
    full_shapes = [a.shape for a in small_part]
    summed = _sum_devices("small_grads_sum", _allgather_small("allgather_small_grads", _pack(small_part)))
    small_g = dict(zip(small, _unpack(summed, full_shapes)))
    for n, width in (("conv_a_w", A), ("conv_b_w", A), ("conv_ffn_w", 2 * F)):
        small_g[n] = lax.dynamic_slice_in_dim(small_g[n], chip * (width // N_CHIPS), width // N_CHIPS, axis=1)
    small_w = dict(norm_mix_g=(norm_mix_g, m_norm_mix_g, v_norm_mix_g), conv_a_w=(conv_a_w, m_conv_a_w, v_conv_a_w),
                   conv_a_b=(conv_a_b, m_conv_a_b, v_conv_a_b), ln_a_g=(ln_a_g, m_ln_a_g, v_ln_a_g),
                   ln_a_b=(ln_a_b, m_ln_a_b, v_ln_a_b), conv_b_w=(conv_b_w, m_conv_b_w, v_conv_b_w),
                   norm_ffn_g=(norm_ffn_g, m_norm_ffn_g, v_norm_ffn_g),
                   conv_ffn_w=(conv_ffn_w, m_conv_ffn_w, v_conv_ffn_w),
                   b_ple_gate=(b_ple_gate, m_b_ple_gate, v_b_ple_gate),
                   norm_final_g=(norm_final_g, m_norm_final_g, v_norm_final_g))
    out_shapes = [small_w[n][0].shape for n in small]
    packed_g = _pack([small_g[n] for n in small])
    packed = [_pack([small_w[n][k] for n in small]) for k in range(3)]
    d_s, m_s, v_s = _adamw("adamw_small", packed[0], packed_g, packed[1], packed[2])
    for n, g, d_, m_, v_ in zip(small, _unpack(packed_g, out_shapes), _unpack(d_s, out_shapes),
                                _unpack(m_s, out_shapes), _unpack(v_s, out_shapes)):
        grads[n], deltas[n], new_m[n], new_v[n] = g, d_, m_, v_

    order = ["norm_mix_g", "w_in", "conv_a_w", "conv_a_b", "ln_a_g", "ln_a_b", "conv_b_w", "w_out", "norm_ffn_g",
             "w_up", "conv_ffn_w", "w_down", "w_ple_gate", "b_ple_gate", "w_ple_proj", "norm_final_g"]
    loss = lax.psum(loss_part[0, 0], ("x", "y", "c"))
    return (loss, dx.reshape(x.shape), *[grads[n] for n in order], *[deltas[n] for n in order],
            *[new_m[n] for n in order], *[new_v[n] for n in order])
```

```python
from typing import Callable, NamedTuple

import jax
import jax.numpy as jnp
from jax import lax
from jax.experimental import pallas as pl
from jax.experimental.pallas import tpu as pltpu

F32 = jnp.float32
BF16 = jnp.bfloat16
MESH = pl.DeviceIdType.MESH
ANY = pl.BlockSpec(memory_space=pl.ANY)

EPS = 1e-6
ADAM_LR = 0.001
ADAM_B1 = 0.9
ADAM_B2 = 0.999
ADAM_EPS = 1e-08
ADAM_WD = 0.01
ADAM_STEP = 10

N_CHIPS = 4
N_DEV = 8
LANES = 128
SUBLANES = 8
PACK_ALIGN = LANES * SUBLANES
ROW_CHUNK = 32
VMEM_CAP = 60 * 1024 * 1024
VMEM_SLACK = 6 * 1024 * 1024


def _pick(n, cands):
    for c in cands:
        if n % c == 0:
            return c
    raise ValueError(f"no tile of {cands} divides {n}")


def _nbytes(shape, dtype):
    n = 1
    for s in shape:
        if s is not None:
            n *= s
    return n * jnp.dtype(dtype).itemsize


def _params(sem, blocks, scratch=(), temps=()):
    est = (2 * sum(_nbytes(s, d) for s, d in blocks) + sum(_nbytes(s, d) for s, d in scratch)
           + sum(_nbytes(s, d) for s, d in temps))
    return pltpu.CompilerParams(dimension_semantics=sem,
                                vmem_limit_bytes=min(est + VMEM_SLACK, VMEM_CAP))


def _sigmoid(x):
    return 1.0 / (1.0 + jnp.exp(-x))


def _rsum(x):
    return jnp.sum(x, axis=0, keepdims=True)


class _Side(NamedTuple):
    ins: list
    out_shapes: list
    n_sems: int
    start: Callable
    wait: Callable
    aliases: tuple = ()


def _call(body, side, *, name, grid, in_specs, out_specs, out_shape, scratch, params, args):
    vmem = [pltpu.VMEM(s, d) for s, d in scratch]
    if side is None:
        outs = pl.pallas_call(body, name=name, grid=grid, in_specs=in_specs, out_specs=out_specs,
                              out_shape=out_shape, scratch_shapes=vmem, compiler_params=params)(*args)
        return list(outs), []
    n_in, n_out, n_sc = len(in_specs), len(out_specs), len(scratch)
    ns_in, ns_out = len(side.ins), len(side.out_shapes)

    def carrier(*refs):
        pos = [0]
        def take(n):
            pos[0] += n
            return refs[pos[0] - n:pos[0]]
        ins, s_ins, outs, s_outs, scr = take(n_in), take(ns_in), take(n_out), take(ns_out), take(n_sc)
        send_sems, recv_sems = take(2)
        first = last = None
        for axis, extent in enumerate(grid):
            at_start, at_end = pl.program_id(axis) == 0, pl.program_id(axis) == extent - 1
            first = at_start if first is None else first & at_start
            last = at_end if last is None else last & at_end

        @pl.when(first)
        def _():
            side.start(s_ins, s_outs, send_sems, recv_sems, 0)
        body(*ins, *outs, *scr)

        @pl.when(last)
        def _():
            side.wait(s_ins, s_outs, send_sems, recv_sems, 0)

    outs = pl.pallas_call(
        carrier, name=name, grid=grid, in_specs=list(in_specs) + [ANY] * ns_in,
        out_specs=list(out_specs) + [ANY] * ns_out, out_shape=list(out_shape) + list(side.out_shapes),
        scratch_shapes=vmem + [pltpu.SemaphoreType.DMA((side.n_sems,)), pltpu.SemaphoreType.DMA((side.n_sems,))],
        input_output_aliases={n_in + a: n_out + b for a, b in side.aliases},
        compiler_params=params)(*args, *side.ins)
    return list(outs[:n_out]), list(outs[n_out:])


def _comm_only(name, side):
    n_in = len(side.ins)

    def body(*refs):
        ins, outs = refs[:n_in], refs[n_in:n_in + len(side.out_shapes)]
        send_sems, recv_sems = refs[n_in + len(side.out_shapes):]
        side.start(ins, outs, send_sems, recv_sems, 0)
        side.wait(ins, outs, send_sems, recv_sems, 0)

    return pl.pallas_call(
        body, name=name, out_shape=list(side.out_shapes), in_specs=[ANY] * n_in,
        out_specs=[ANY] * len(side.out_shapes),
        scratch_shapes=[pltpu.SemaphoreType.DMA((side.n_sems,)), pltpu.SemaphoreType.DMA((side.n_sems,))],
        input_output_aliases=dict(side.aliases),
    )(*side.ins)


def _rms_stats(x):
    return lax.rsqrt(jnp.mean(x * x, axis=-1, keepdims=True) + EPS)


def _rms_bwd(h, g, dout):
    r = _rms_stats(h)
    n = h * r
    dn = dout * g
    dh = r * (dn - n * jnp.mean(dn * n, axis=-1, keepdims=True))
    return dh, _rsum(dout * n)


def _identity(t):
    return t


def _chip_major(nb, place=_identity):
    return lambda i, j: (place(j) // nb, 0, place(j) % nb)


def _rows_mm(name, S, TM, N, TN, *, row_ins, vec_ins=(), colvec_ins=(), weights, tile_ins=(),
             tile_outs, row_outs=(), acc_outs=(), prologue=None, epilogue, place=_identity, side=None):
    nI, nJ = S // TM, N // TN
    n_row, n_vec, n_cv, n_w, n_tile = len(row_ins), len(vec_ins), len(colvec_ins), len(weights), len(tile_ins)
    n_to, n_ro, n_ao = len(tile_outs), len(row_outs), len(acc_outs)

    in_specs, blocks, scratch, ks = [], [], [], []
    for a in row_ins:
        in_specs.append(pl.BlockSpec((TM, a.shape[1]), lambda i, j: (i, 0)))
        blocks.append(((TM, a.shape[1]), a.dtype))
    for a in vec_ins:
        in_specs.append(pl.BlockSpec(a.shape, lambda i, j: (0, 0)))
        blocks.append((a.shape, a.dtype))
    for a in colvec_ins:
        in_specs.append(pl.BlockSpec((1, TN), lambda i, j: (0, j)))
        blocks.append(((1, TN), a.dtype))
    for w, mode in weights:
        if mode == "nn2":
            k = w.shape[0]
            in_specs.append(pl.BlockSpec((k, TN), lambda i, j: (0, j)))
        elif mode == "nn3":
            k = w.shape[1]
            in_specs.append(pl.BlockSpec((None, k, TN), _chip_major(w.shape[2] // TN, place)))
        else:
            k = w.shape[1]
            in_specs.append(pl.BlockSpec((TN, k), lambda i, j: (j, 0)))
        ks.append(k)
        blocks.append(((k, TN), BF16))
        if prologue is not None:
            scratch.append(((TM, k), BF16))
    for a in tile_ins:
        in_specs.append(pl.BlockSpec((TM, TN), lambda i, j: (i, j)))
        blocks.append(((TM, TN), a.dtype))

    out_shape, out_specs = [], []
    for dt in tile_outs:
        out_shape.append(jax.ShapeDtypeStruct((S, N), dt))
        out_specs.append(pl.BlockSpec((TM, TN), lambda i, j: (i, j)))
        blocks.append(((TM, TN), dt))
    for width, dt in row_outs:
        out_shape.append(jax.ShapeDtypeStruct((S, width), dt))
        out_specs.append(pl.BlockSpec((TM, width), lambda i, j: (i, 0)))
        blocks.append(((TM, width), dt))
    for rows, width in acc_outs:
        out_shape.append(jax.ShapeDtypeStruct((rows, width), F32))
        out_specs.append(pl.BlockSpec((rows, width), lambda i, j: (0, 0)))
        blocks.append(((rows, width), F32))

    modes = [m for _, m in weights]

    def body(*refs):
        pos = 0
        def take(n):
            nonlocal pos
            out = refs[pos:pos + n]
            pos += n
            return out
        row_r, vec_r, cv_r, w_r, tile_r = take(n_row), take(n_vec), take(n_cv), take(n_w), take(n_tile)
        to_r, ro_r, ao_r, a_sc = take(n_to), take(n_ro), take(n_ao), take(len(scratch))
        i, j = pl.program_id(0), pl.program_id(1)

        if prologue is None:
            a_sc = row_r[:n_w]
        else:
            @pl.when(j == 0)
            def _():
                if n_ao:
                    @pl.when(i == 0)
                    def _():
                        for r in ao_r:
                            r[...] = jnp.zeros_like(r)

                def chunk(ci, carry):
                    rows = pl.ds(pl.multiple_of(ci * ROW_CHUNK, ROW_CHUNK), ROW_CHUNK)
                    for sc, a in zip(a_sc, prologue(rows, row_r, vec_r, ro_r, ao_r)):
                        sc[rows, :] = a
                    return carry
                lax.fori_loop(0, TM // ROW_CHUNK, chunk, 0)

        accs = []
        for w_ref, sc, mode in zip(w_r, a_sc, modes):
            if mode == "nt2":
                accs.append(lax.dot_general(sc[...], w_ref[...], (((1,), (1,)), ((), ())),
                                            preferred_element_type=F32))
            else:
                accs.append(jnp.dot(sc[...], w_ref[...], preferred_element_type=F32))
        outs = epilogue(accs, tile_r, cv_r)
        for r, o in zip(to_r, outs):
            r[...] = o.astype(r.dtype)

    outs, side_outs = _call(
        body, side, name=name, grid=(nI, nJ), in_specs=in_specs, out_specs=out_specs, out_shape=out_shape,
        scratch=scratch, params=_params(("arbitrary", "arbitrary"), blocks, scratch, temps=[((TM, TN), F32)] * 3),
        args=[*row_ins, *vec_ins, *colvec_ins, *[w for w, _ in weights], *tile_ins])
    return outs if side is None else (outs, side_outs)


def _kloop_mm(name, S, TM, a, w3, TK, *, row_ins, vec_ins, row_outs, acc_outs, epilogue, place=_identity,
              side=None):
    _, N, Ks = w3.shape
    nb = Ks // TK
    nK = N_CHIPS * nb
    n_row, n_vec, n_ro, n_ao = len(row_ins), len(vec_ins), len(row_outs), len(acc_outs)

    in_specs = [pl.BlockSpec((TM, TK), lambda i, k: (i, k)),
                pl.BlockSpec((None, N, TK), _chip_major(nb, place))]
    blocks = [((TM, TK), BF16), ((N, TK), BF16)]
    for r in row_ins:
        in_specs.append(pl.BlockSpec((TM, r.shape[1]), lambda i, k: (i, 0)))
        blocks.append(((TM, r.shape[1]), r.dtype))
    for v in vec_ins:
        in_specs.append(pl.BlockSpec(v.shape, lambda i, k: (0, 0)))
        blocks.append((v.shape, v.dtype))
    out_shape, out_specs = [], []
    for width, dt in row_outs:
        out_shape.append(jax.ShapeDtypeStruct((S, width), dt))
        out_specs.append(pl.BlockSpec((TM, width), lambda i, k: (i, 0)))
        blocks.append(((TM, width), dt))
    for rows, width in acc_outs:
        out_shape.append(jax.ShapeDtypeStruct((rows, width), F32))
        out_specs.append(pl.BlockSpec((rows, width), lambda i, k: (0, 0)))
        blocks.append(((rows, width), F32))
    scratch = [((TM, N), F32)]

    def body(*refs):
        a_ref, w_ref = refs[0], refs[1]
        row_r = refs[2:2 + n_row]
        vec_r = refs[2 + n_row:2 + n_row + n_vec]
        pos = 2 + n_row + n_vec
        ro_r = refs[pos:pos + n_ro]
        ao_r = refs[pos + n_ro:pos + n_ro + n_ao]
        acc_sc = refs[pos + n_ro + n_ao]
        i, k = pl.program_id(0), pl.program_id(1)
        @pl.when(k == 0)
        def _():
            acc_sc[...] = jnp.zeros_like(acc_sc)
        acc_sc[...] += lax.dot_general(a_ref[...], w_ref[...], (((1,), (1,)), ((), ())),
                                       preferred_element_type=F32)

        @pl.when(k == nK - 1)
        def _():
            @pl.when(i == 0)
            def _():
                for r in ao_r:
                    r[...] = jnp.zeros_like(r)

            def chunk(ci, carry):
                rows = pl.ds(pl.multiple_of(ci * ROW_CHUNK, ROW_CHUNK), ROW_CHUNK)
                epilogue(acc_sc[rows, :], rows, row_r, vec_r, ro_r, ao_r)
                return carry
            lax.fori_loop(0, TM // ROW_CHUNK, chunk, 0)

    outs, side_outs = _call(
        body, side, name=name, grid=(S // TM, nK), in_specs=in_specs, out_specs=out_specs, out_shape=out_shape,
        scratch=scratch, params=_params(("arbitrary", "arbitrary"), blocks, scratch, temps=[((TM, N), F32)]),
        args=[a, w3, *row_ins, *vec_ins])
    return outs if side is None else (outs, side_outs)


def _tn_mm(name, a, b, TMw, TNw, TK, cols_per_chip=None, place=_identity):
    S, M = a.shape
    N = b.shape[1]
    nK = S // TK
    if cols_per_chip is None:
        out_shape = jax.ShapeDtypeStruct((M, N), F32)
        out_spec = pl.BlockSpec((TMw, TNw), lambda i, j, k: (i, j))
    else:
        nb = cols_per_chip // TNw
        out_shape = jax.ShapeDtypeStruct((N_CHIPS, M, cols_per_chip), F32)
        out_spec = pl.BlockSpec((None, TMw, TNw), lambda i, j, k: (place(j) // nb, i, place(j) % nb))

    def body(a_ref, b_ref, o_ref):
        @pl.when(pl.program_id(2) == 0)
        def _():
            o_ref[...] = jnp.zeros_like(o_ref)
        o_ref[...] += lax.dot_general(a_ref[...], b_ref[...], (((0,), (0,)), ((), ())),
                                      preferred_element_type=F32)

    blocks = [((TK, TMw), BF16), ((TK, TNw), BF16), ((TMw, TNw), F32)]
    return pl.pallas_call(
        body, name=name, grid=(M // TMw, N // TNw, nK),
        in_specs=[pl.BlockSpec((TK, TMw), lambda i, j, k: (k, i)),
                  pl.BlockSpec((TK, TNw), lambda i, j, k: (k, j))],
        out_specs=out_spec, out_shape=out_shape,
        compiler_params=_params(("arbitrary", "arbitrary", "arbitrary"), blocks,
                                temps=[((TMw, TNw), F32), ((TK, TMw), BF16)]),
    )(a, b)


def _prev_rows(TM, H, col):
    return lambda i: (jnp.maximum(i * (TM // H) - 1, 0), col)


def _next_rows(S, TM, H, col):
    return lambda i: (jnp.minimum((i + 1) * (TM // H), S // H - 1), col)


def _taps_causal(ext_ref, w_ref, K, H, TM, cs):
    acc = None
    for k in range(K):
        term = ext_ref[pl.ds(H - (K - 1) + k, TM), cs] * w_ref[pl.ds(k, 1), cs]
        acc = term if acc is None else acc + term
    return acc


def _taps_anticausal(ext_ref, w_ref, K, TM, cs):
    acc = None
    for k in range(K):
        term = ext_ref[pl.ds(K - 1 - k, TM), cs] * w_ref[pl.ds(k, 1), cs]
        acc = term if acc is None else acc + term
    return acc


def _tap_grads(ext_ref, g, K, H, TM, cs):
    return [_rsum(ext_ref[pl.ds(H - (K - 1) + k, TM), cs] * g) for k in range(K)]


def _shift_copies(ext_ref, shifted, cs):
    n = shifted.shape[1]
    for r in range(1, SUBLANES):
        shifted[r - 1] = ext_ref[pl.ds(r, n), cs]


def _rows_at(ext_ref, shifted, start, n, cs):
    q, r = divmod(start, SUBLANES)
    if r == 0:
        return ext_ref[pl.ds(start, n), cs]
    return shifted[r - 1, pl.ds(SUBLANES * q, n), :]


def _mixer_fwd(z, conv_a_w, conv_a_b, ln_g, ln_b, conv_b_w, S, TM, A, side=None):
    H = 32
    KA, KB = conv_a_w.shape[0], conv_b_w.shape[0]
    n_chunks = A // LANES
    RB = _pick(TM, (64, 32))

    def body(zc_ref, zh_ref, wa_ref, ba_ref, g_ref, b_ref, wb_ref, a1_ref, cat_ref, ext_a, ext_b, shifted):
        i = pl.program_id(0)
        live = (i > 0).astype(F32)
        zc = zc_ref[...].astype(F32)
        zh = zh_ref[...].astype(F32) * live
        ext_a[pl.ds(0, H), :] = zh[:, 0:A] * _sigmoid(zh[:, A:2 * A])
        ext_a[pl.ds(H, TM), :] = zc[:, 0:A] * _sigmoid(zc[:, A:2 * A])
        ext_b[pl.ds(0, H), :] = zh[:, 3 * A:4 * A] * zh[:, 4 * A:5 * A]
        ext_b[pl.ds(H, TM), :] = zc[:, 3 * A:4 * A] * zc[:, 4 * A:5 * A]

        def chunk(c, carry):
            cs = pl.ds(pl.multiple_of(c * LANES, LANES), LANES)
            _shift_copies(ext_a, shifted, cs)
            for r0 in range(0, TM, RB):
                acc = None
                for k in range(KA):
                    term = _rows_at(ext_a, shifted, H - (KA - 1) + k + r0, RB, cs) * wa_ref[pl.ds(k, 1), cs]
                    acc = term if acc is None else acc + term
                a1_ref[pl.ds(r0, RB), cs] = acc + ba_ref[:, cs]
            return carry
        lax.fori_loop(0, n_chunks, chunk, 0)

        a1 = a1_ref[...]
        mu = jnp.mean(a1, axis=-1, keepdims=True)
        d = a1 - mu
        var = jnp.mean(d * d, axis=-1, keepdims=True)
        a2 = d * lax.rsqrt(var + EPS) * g_ref[...] + b_ref[...]
        cat_ref[:, 0:A] = (a2 * _sigmoid(a2)).astype(BF16)
        cbc = _taps_causal(ext_b, wb_ref, KB, H, TM, slice(None))
        cat_ref[:, A:2 * A] = (zc[:, 2 * A:3 * A] * cbc).astype(BF16)

    blocks = [((TM, 5 * A), BF16), ((H, 5 * A), BF16), ((KA, A), F32), ((KB, A), F32),
              ((TM, A), F32), ((TM, 2 * A), BF16)]
    scratch = [((H + TM, A), F32), ((H + TM, A), F32), ((SUBLANES - 1, H + TM - SUBLANES, LANES), F32)]
    vec = lambda r: pl.BlockSpec((r, A), lambda i: (0, 0))
    outs, side_outs = _call(
        body, side, name="mixer_fwd", grid=(S // TM,),
        in_specs=[pl.BlockSpec((TM, 5 * A), lambda i: (i, 0)),
                  pl.BlockSpec((H, 5 * A), _prev_rows(TM, H, 0)),
                  vec(KA), vec(1), vec(1), vec(1), vec(KB)],
        out_specs=[pl.BlockSpec((TM, A), lambda i: (i, 0)), pl.BlockSpec((TM, 2 * A), lambda i: (i, 0))],
        out_shape=[jax.ShapeDtypeStruct((S, A), F32), jax.ShapeDtypeStruct((S, 2 * A), BF16)],
        scratch=scratch,
        params=_params(("arbitrary",), blocks, scratch, temps=[((TM, 5 * A), F32)] * 2 + [((TM, A), F32)] * 10),
        args=[z, z, conv_a_w, conv_a_b, ln_g, ln_b, conv_b_w])
    return outs if side is None else (outs, side_outs)


def _pair_tile(nF):
    return lambda t: (t % 2) * nF + t // 2


FFN_ROWS = 16


def _bcast_taps(w_ref, K, lanes):
    return [jnp.broadcast_to(w_ref[pl.ds(k, 1), lanes], (FFN_ROWS, LANES)) for k in range(K)]


def _ffn_act(u0, conv_w, S, TM, F, TC, side=None):
    H = 16
    K = conv_w.shape[0]
    nF = F // TC

    def body(uc_ref, uh_ref, wg_ref, wu_ref, o_ref, conv_ref, ext):
        live = (pl.program_id(0) > 0).astype(F32)
        ext[pl.ds(0, H), :] = uh_ref[...].astype(F32) * live
        ext[pl.ds(H, TM), :] = uc_ref[...].astype(F32)

        def lane_chunk(c, carry):
            lo = pl.ds(pl.multiple_of(c * LANES, LANES), LANES)
            lg, lu = lo, pl.ds(pl.multiple_of(TC + c * LANES, LANES), LANES)
            wg, wu = _bcast_taps(wg_ref, K, lo), _bcast_taps(wu_ref, K, lo)
            for r0 in range(0, TM, FFN_ROWS):
                g = u = None
                for k in range(K):
                    rows = pl.ds(H - (K - 1) + k + r0, FFN_ROWS)
                    tg, tu = ext[rows, lg] * wg[k], ext[rows, lu] * wu[k]
                    g, u = (tg, tu) if g is None else (g + tg, u + tu)
                o_ref[pl.ds(r0, FFN_ROWS), lo] = (g * _sigmoid(g) * u).astype(BF16)
                conv_ref[pl.ds(r0, FFN_ROWS), lg] = g.astype(BF16)
                conv_ref[pl.ds(r0, FFN_ROWS), lu] = u.astype(BF16)
            return carry
        lax.fori_loop(0, TC // LANES, lane_chunk, 0)

    blocks = [((TM, 2 * TC), BF16), ((H, 2 * TC), BF16), ((K, TC), F32), ((K, TC), F32), ((TM, TC), BF16),
              ((TM, 2 * TC), BF16)]
    scratch = [((H + TM, 2 * TC), F32)]
    outs, side_outs = _call(
        body, side, name="ffn_act", grid=(S // TM, nF),
        in_specs=[pl.BlockSpec((TM, 2 * TC), lambda i, j: (i, j)),
                  pl.BlockSpec((H, 2 * TC), lambda i, j: (jnp.maximum(i * (TM // H) - 1, 0), j)),
                  pl.BlockSpec((K, TC), lambda i, j: (0, j)),
                  pl.BlockSpec((K, TC), lambda i, j: (0, j + nF))],
        out_specs=[pl.BlockSpec((TM, TC), lambda i, j: (i, j)), pl.BlockSpec((TM, 2 * TC), lambda i, j: (i, j))],
        out_shape=[jax.ShapeDtypeStruct((S, F), BF16), jax.ShapeDtypeStruct((S, 2 * F), BF16)],
        scratch=scratch,
        params=_params(("arbitrary", "arbitrary"), blocks, scratch, temps=[((TM, 2 * TC), F32)]),
        args=[u0, u0, conv_w, conv_w])
    return outs if side is None else (outs, side_outs)


def _ple_loss(h2, p, target, w_gate, w_proj, b_gate, g_final, S, TM):
    D, P = h2.shape[1], p.shape[1]

    def body(h_ref, p_ref, t_ref, wg_ref, wp_ref, b_ref, g_ref,
             loss_ref, dg_ref, db_ref, dh_ref, dpre_ref, dpp_ref, hb_ref, pb_ref, pre_sc, pp_sc):
        @pl.when(pl.program_id(0) == 0)
        def _():
            loss_ref[...] = jnp.zeros_like(loss_ref)
            dg_ref[...] = jnp.zeros_like(dg_ref)
            db_ref[...] = jnp.zeros_like(db_ref)
        hb_ref[...] = h_ref[...].astype(BF16)
        pb_ref[...] = p_ref[...].astype(BF16)
        pre_sc[...] = jnp.dot(hb_ref[...], wg_ref[...], preferred_element_type=F32)
        pp_sc[...] = jnp.dot(pb_ref[...], wp_ref[...], preferred_element_type=F32)

        def chunk(ci, carry):
            rows = pl.ds(pl.multiple_of(ci * ROW_CHUNK, ROW_CHUNK), ROW_CHUNK)
            g = g_ref[...]
            gate = _sigmoid(pre_sc[rows, :] + b_ref[...])
            pp = pp_sc[rows, :]
            h = h_ref[rows, :] + pp * gate
            r = _rms_stats(h)
            n = h * r
            diff = n * g - t_ref[rows, :]
            loss_ref[...] += 0.5 * jnp.sum(jnp.mean(diff * diff, axis=-1, keepdims=True), axis=0, keepdims=True)
            dy = diff * (1.0 / D)
            dn = dy * g
            dh = r * (dn - n * jnp.mean(dn * n, axis=-1, keepdims=True))
            dh_ref[rows, :] = dh
            dg_ref[...] += _rsum(dy * n)
            dpre = dh * pp * gate * (1.0 - gate)
            dpre_ref[rows, :] = dpre.astype(BF16)
            dpp_ref[rows, :] = (dh * gate).astype(BF16)
            db_ref[...] += _rsum(dpre)
            return carry
        lax.fori_loop(0, TM // ROW_CHUNK, chunk, 0)

    row = pl.BlockSpec((TM, D), lambda i: (i, 0))
    prow = pl.BlockSpec((TM, P), lambda i: (i, 0))
    vec = pl.BlockSpec((1, D), lambda i: (0, 0))
    whole = lambda a: pl.BlockSpec(a.shape, lambda i: (0, 0))
    blocks = ([((TM, D), F32)] * 3 + [((TM, P), F32), ((D, D), BF16), ((P, D), BF16)]
              + [((TM, D), BF16)] * 3 + [((TM, P), BF16)])
    scratch = [((TM, D), F32)] * 2
    return pl.pallas_call(
        body, name="ple_loss", grid=(S // TM,),
        in_specs=[row, prow, row, whole(w_gate), whole(w_proj), vec, vec],
        out_specs=[pl.BlockSpec((1, 1), lambda i: (0, 0)), vec, vec, row, row, row, row, prow],
        out_shape=[jax.ShapeDtypeStruct((1, 1), F32), jax.ShapeDtypeStruct((1, D), F32),
                   jax.ShapeDtypeStruct((1, D), F32), jax.ShapeDtypeStruct((S, D), F32),
                   jax.ShapeDtypeStruct((S, D), BF16), jax.ShapeDtypeStruct((S, D), BF16),
                   jax.ShapeDtypeStruct((S, D), BF16), jax.ShapeDtypeStruct((S, P), BF16)],
        scratch_shapes=[pltpu.VMEM(s, d) for s, d in scratch],
        compiler_params=_params(("arbitrary",), blocks, scratch, temps=[((TM, D), F32)] * 2),
    )(h2, p, target, w_gate, w_proj, b_gate, g_final)


def _ffn_bwd(u0, conv_u0, dact, conv_w, S, TM, F, TC, side=None):
    H = FFN_ROWS
    K = conv_w.shape[0]
    nF, nI = F // TC, S // TM

    def body(xc_ref, cc_ref, cn_ref, dc_ref, dn_ref, wg_ref, wu_ref, o_ref, dwg_ref, dwu_ref, ext_d):
        i = pl.program_id(1)
        @pl.when(i == 0)
        def _():
            dwg_ref[...] = jnp.zeros_like(dwg_ref)
            dwu_ref[...] = jnp.zeros_like(dwu_ref)
        last = (i < nI - 1).astype(F32)

        def lane_chunk(c, carry):
            lo = pl.ds(pl.multiple_of(c * LANES, LANES), LANES)
            lg, lu = lo, pl.ds(pl.multiple_of(TC + c * LANES, LANES), LANES)
            wg, wu = _bcast_taps(wg_ref, K, lo), _bcast_taps(wu_ref, K, lo)
            for r0 in range(0, TM + H, FFN_ROWS):
                if r0 < TM:
                    rows = pl.ds(r0, FFN_ROWS)
                    g, u, da = cc_ref[rows, lg], cc_ref[rows, lu], dc_ref[rows, lo].astype(F32)
                else:
                    g, u, da = cn_ref[:, lg], cn_ref[:, lu], dn_ref[:, lo].astype(F32) * last
                g, u = g.astype(F32), u.astype(F32)
                s = _sigmoid(g)
                ext_d[pl.ds(r0, FFN_ROWS), lg] = da * u * s * (1.0 + g * (1.0 - s))
                ext_d[pl.ds(r0, FFN_ROWS), lu] = da * g * s
            sums_g, sums_u = [None] * K, [None] * K
            for r0 in range(0, TM, FFN_ROWS):
                xg = xc_ref[pl.ds(r0, FFN_ROWS), lg].astype(F32)
                xu = xc_ref[pl.ds(r0, FFN_ROWS), lu].astype(F32)
                g = u = None
                for k in range(K):
                    rows = pl.ds(K - 1 - k + r0, FFN_ROWS)
                    dg, du = ext_d[rows, lg], ext_d[rows, lu]
                    tg, tu = dg * wg[k], du * wu[k]
                    g, u = (tg, tu) if g is None else (g + tg, u + tu)
                    pg, pu = xg * dg, xu * du
                    sums_g[k] = pg if sums_g[k] is None else sums_g[k] + pg
                    sums_u[k] = pu if sums_u[k] is None else sums_u[k] + pu
                o_ref[pl.ds(r0, FFN_ROWS), lg] = g.astype(BF16)
                o_ref[pl.ds(r0, FFN_ROWS), lu] = u.astype(BF16)
            for k in range(K):
                dwg_ref[pl.ds(k, 1), lo] += _rsum(sums_g[k])
                dwu_ref[pl.ds(k, 1), lo] += _rsum(sums_u[k])
            return carry
        lax.fori_loop(0, TC // LANES, lane_chunk, 0)

    blocks = [((TM, 2 * TC), BF16), ((TM, 2 * TC), BF16), ((H, 2 * TC), BF16), ((TM, TC), BF16), ((H, TC), BF16),
              ((K, TC), F32), ((K, TC), F32), ((TM, 2 * TC), BF16), ((K, TC), F32), ((K, TC), F32)]
    scratch = [((TM + H, 2 * TC), F32)]
    nxt = lambda j, i: (jnp.minimum((i + 1) * (TM // H), S // H - 1), j)
    tile = pl.BlockSpec((TM, 2 * TC), lambda j, i: (i, j))
    taps_out = pl.BlockSpec((K, TC), lambda j, i: (0, j))
    outs, side_outs = _call(
        body, side, name="ffn_bwd", grid=(nF, nI),
        in_specs=[tile, tile, pl.BlockSpec((H, 2 * TC), nxt),
                  pl.BlockSpec((TM, TC), lambda j, i: (i, j)), pl.BlockSpec((H, TC), nxt),
                  pl.BlockSpec((K, TC), lambda j, i: (0, j)), pl.BlockSpec((K, TC), lambda j, i: (0, j + nF))],
        out_specs=[tile, taps_out, taps_out],
        out_shape=[jax.ShapeDtypeStruct((S, 2 * F), BF16), jax.ShapeDtypeStruct((K, F), F32),
                   jax.ShapeDtypeStruct((K, F), F32)],
        scratch=scratch,
        params=_params(("arbitrary", "arbitrary"), blocks, scratch),
        args=[u0, conv_u0, conv_u0, dact, dact, conv_w, conv_w])
    return outs if side is None else (outs, side_outs)


def _mixer_bwd_ln(dcat, a1, ln_g, ln_b, S, TM, A):
    def body(dc_ref, a1_ref, g_ref, b_ref, da1_ref, acc_ref):
        @pl.when(pl.program_id(0) == 0)
        def _():
            acc_ref[...] = jnp.zeros_like(acc_ref)
        a1 = a1_ref[...]
        g = g_ref[...]
        mu = jnp.mean(a1, axis=-1, keepdims=True)
        d = a1 - mu
        rstd = lax.rsqrt(jnp.mean(d * d, axis=-1, keepdims=True) + EPS)
        nh = d * rstd
        a2 = nh * g + b_ref[...]
        s = _sigmoid(a2)
        da2 = dc_ref[...].astype(F32) * s * (1.0 + a2 * (1.0 - s))
        dnh = da2 * g
        da1 = rstd * (dnh - jnp.mean(dnh, axis=-1, keepdims=True)
                      - nh * jnp.mean(dnh * nh, axis=-1, keepdims=True))
        da1_ref[...] = da1
        acc_ref[pl.ds(0, 1), :] += _rsum(da2 * nh)
        acc_ref[pl.ds(1, 1), :] += _rsum(da2)
        acc_ref[pl.ds(2, 1), :] += _rsum(da1)

    blocks = [((TM, A), BF16), ((TM, A), F32), ((TM, A), F32), ((4, A), F32)]
    return pl.pallas_call(
        body, name="mixer_bwd_ln", grid=(S // TM,),
        in_specs=[pl.BlockSpec((TM, A), lambda i: (i, 0)), pl.BlockSpec((TM, A), lambda i: (i, 0)),
                  pl.BlockSpec((1, A), lambda i: (0, 0)), pl.BlockSpec((1, A), lambda i: (0, 0))],
        out_specs=[pl.BlockSpec((TM, A), lambda i: (i, 0)), pl.BlockSpec((4, A), lambda i: (0, 0))],
        out_shape=[jax.ShapeDtypeStruct((S, A), F32), jax.ShapeDtypeStruct((4, A), F32)],
        compiler_params=_params(("arbitrary",), blocks, temps=[((TM, A), F32)] * 12),
    )(dcat, a1, ln_g, ln_b)


def _mixer_bwd_conv(z, dcat, da1, conv_a_w, conv_b_w, S, TM, A, side=None):
    H = 32
    KA, KB = conv_a_w.shape[0], conv_b_w.shape[0]
    nI = S // TM
    n_chunks = A // LANES
    RB = _pick(TM, (64, 32))

    def body(zc_ref, zp_ref, zn_ref, dbc_ref, dbn_ref, d1c_ref, d1n_ref, wa_ref, wb_ref,
             dz_ref, dwa_ref, dwb_ref, ext_a0, ext_d1, ext_cb, ext_dc, da0_sc, shifted_d, shifted_a):
        i = pl.program_id(0)
        @pl.when(i == 0)
        def _():
            dwa_ref[...] = jnp.zeros_like(dwa_ref)
            dwb_ref[...] = jnp.zeros_like(dwb_ref)
        first = (i > 0).astype(F32)
        last = (i < nI - 1).astype(F32)
        zc = zc_ref[...].astype(F32)
        zp = zp_ref[...].astype(F32) * first
        a_val, a_gate = zc[:, 0:A], zc[:, A:2 * A]
        b_gate, c_gate, b_h = zc[:, 2 * A:3 * A], zc[:, 3 * A:4 * A], zc[:, 4 * A:5 * A]
        sig = _sigmoid(a_gate)
        ext_a0[pl.ds(0, H), :] = zp[:, 0:A] * _sigmoid(zp[:, A:2 * A])
        ext_a0[pl.ds(H, TM), :] = a_val * sig
        ext_d1[pl.ds(0, TM), :] = d1c_ref[...]
        ext_d1[pl.ds(TM, H), :] = d1n_ref[...] * last
        ext_cb[pl.ds(0, H), :] = zp[:, 3 * A:4 * A] * zp[:, 4 * A:5 * A]
        ext_cb[pl.ds(H, TM), :] = c_gate * b_h
        dbx = dbc_ref[...].astype(F32)
        dcbc = dbx * b_gate
        ext_dc[pl.ds(0, TM), :] = dcbc
        ext_dc[pl.ds(TM, H), :] = dbn_ref[...].astype(F32) * zn_ref[...].astype(F32) * last

        def chunk(c, carry):
            cs = pl.ds(pl.multiple_of(c * LANES, LANES), LANES)
            _shift_copies(ext_d1, shifted_d, cs)
            _shift_copies(ext_a0, shifted_a, cs)
            for r0 in range(0, TM, RB):
                acc = None
                for k in range(KA):
                    term = _rows_at(ext_d1, shifted_d, KA - 1 - k + r0, RB, cs) * wa_ref[pl.ds(k, 1), cs]
                    acc = term if acc is None else acc + term
                da0_sc[pl.ds(r0, RB), cs] = acc
            for k in range(KA):
                acc = None
                for r0 in range(0, TM, RB):
                    term = (_rows_at(ext_a0, shifted_a, H - (KA - 1) + k + r0, RB, cs)
                            * ext_d1[pl.ds(r0, RB), cs])
                    acc = term if acc is None else acc + term
                dwa_ref[pl.ds(k, 1), cs] += _rsum(acc)
            return carry
        lax.fori_loop(0, n_chunks, chunk, 0)

        da0 = da0_sc[...]
        dz_ref[:, 0:A] = (da0 * sig).astype(BF16)
        dz_ref[:, A:2 * A] = (da0 * a_val * sig * (1.0 - sig)).astype(BF16)
        cbc = _taps_causal(ext_cb, wb_ref, KB, H, TM, slice(None))
        dz_ref[:, 2 * A:3 * A] = (dbx * cbc).astype(BF16)
        dcb = _taps_anticausal(ext_dc, wb_ref, KB, TM, slice(None))
        dz_ref[:, 3 * A:4 * A] = (dcb * b_h).astype(BF16)
        dz_ref[:, 4 * A:5 * A] = (dcb * c_gate).astype(BF16)
        grads = _tap_grads(ext_cb, dcbc, KB, H, TM, slice(None))
        for k in range(KB):
            dwb_ref[pl.ds(k, 1), :] += grads[k]

    blocks = [((TM, 5 * A), BF16), ((H, 5 * A), BF16), ((H, A), BF16), ((TM, A), BF16), ((H, A), BF16),
              ((TM, A), F32), ((H, A), F32), ((KA, A), F32), ((KB, A), F32),
              ((TM, 5 * A), BF16), ((KA, A), F32), ((KB, A), F32)]
    scratch = ([((H + TM, A), F32)] * 4 + [((TM, A), F32)]
               + [((SUBLANES - 1, H + TM - SUBLANES, LANES), F32)] * 2)
    vec = lambda r: pl.BlockSpec((r, A), lambda i: (0, 0))
    outs, side_outs = _call(
        body, side, name="mixer_bwd_conv", grid=(nI,),
        in_specs=[pl.BlockSpec((TM, 5 * A), lambda i: (i, 0)),
                  pl.BlockSpec((H, 5 * A), _prev_rows(TM, H, 0)),
                  pl.BlockSpec((H, A), _next_rows(S, TM, H, 2)),
                  pl.BlockSpec((TM, A), lambda i: (i, 1)),
                  pl.BlockSpec((H, A), _next_rows(S, TM, H, 1)),
                  pl.BlockSpec((TM, A), lambda i: (i, 0)),
                  pl.BlockSpec((H, A), _next_rows(S, TM, H, 0)),
                  vec(KA), vec(KB)],
        out_specs=[pl.BlockSpec((TM, 5 * A), lambda i: (i, 0)), vec(KA), vec(KB)],
        out_shape=[jax.ShapeDtypeStruct((S, 5 * A), BF16), jax.ShapeDtypeStruct((KA, A), F32),
                   jax.ShapeDtypeStruct((KB, A), F32)],
        scratch=scratch,
        params=_params(("arbitrary",), blocks, scratch, temps=[((TM, 5 * A), F32)] * 2 + [((TM, A), F32)] * 14),
        args=[z, z, z, dcat, dcat, da1, da1, conv_a_w, conv_b_w])
    return outs if side is None else (outs, side_outs)


def _row_tile(R):
    return _pick(R, (256, 128, 64, 32, 16, 8))


def _scalars(*vals):
    return jnp.stack([jnp.asarray(v, jnp.int32) for v in vals])


def _cast_into_gathered(name, w, chip):
    R, C = w.shape
    TR = _row_tile(R)

    def body(s_ref, w_ref, o_ref):
        o_ref[...] = w_ref[...].astype(BF16)

    grid_spec = pltpu.PrefetchScalarGridSpec(
        num_scalar_prefetch=1, grid=(R // TR,),
        in_specs=[pl.BlockSpec((TR, C), lambda r, s: (r, 0))],
        out_specs=pl.BlockSpec((None, TR, C), lambda r, s: (s[0], r, 0)))
    return pl.pallas_call(body, name=name, grid_spec=grid_spec,
                          out_shape=jax.ShapeDtypeStruct((N_CHIPS, R, C), BF16),
                          compiler_params=_params(("arbitrary",), [((TR, C), F32), ((TR, C), BF16)]),
                          )(_scalars(chip), w)


def _add_pair(name, dw, recv, c):
    _, _, Rh, C = dw.shape
    TR = _row_tile(Rh)

    def body(c_ref, a_ref, b_ref, o_ref, ob_ref):
        s = a_ref[...] + b_ref[...]
        o_ref[...] = s
        ob_ref[...] = s.astype(BF16)

    out_spec = pl.BlockSpec((None, TR, C), lambda k, r, c_ref: (k, r, 0))
    grid_spec = pltpu.PrefetchScalarGridSpec(
        num_scalar_prefetch=1, grid=(N_CHIPS, Rh // TR),
        in_specs=[pl.BlockSpec((None, None, TR, C), lambda k, r, c_ref: (k, c_ref[0], r, 0)),
                  pl.BlockSpec((None, TR, C), lambda k, r, c_ref: (k, r, 0))],
        out_specs=[out_spec, out_spec])
    return pl.pallas_call(body, name=name, grid_spec=grid_spec,
                          out_shape=[jax.ShapeDtypeStruct((N_CHIPS, Rh, C), F32),
                                     jax.ShapeDtypeStruct((N_CHIPS, Rh, C), BF16)],
                          compiler_params=_params(("arbitrary", "arbitrary"), [((TR, C), F32)] * 4),
                          )(_scalars(c), dw, recv)


def _add_chips(name, parts, recv, chip, c):
    _, Rh, C = parts.shape
    TR = _row_tile(Rh)

    def body(s_ref, p_ref, r_ref, o_ref):
        o_ref[...] = ((p_ref[...] + r_ref[0].astype(F32)) + r_ref[1].astype(F32)) + r_ref[2].astype(F32)

    grid_spec = pltpu.PrefetchScalarGridSpec(
        num_scalar_prefetch=1, grid=(Rh // TR,),
        in_specs=[pl.BlockSpec((None, TR, C), lambda r, s: (s[0], r, 0)),
                  pl.BlockSpec((N_CHIPS - 1, TR, C), lambda r, s: (0, r, 0))],
        out_specs=pl.BlockSpec((None, TR, C), lambda r, s: (s[1], r, 0)))
    return pl.pallas_call(body, name=name, grid_spec=grid_spec,
                          out_shape=jax.ShapeDtypeStruct((2, Rh, C), F32),
                          compiler_params=_params(("arbitrary",), [((N_CHIPS + 1, TR, C), F32)]),
                          )(_scalars(chip, c), parts, recv)


def _sum_devices(name, parts):
    _, R, C = parts.shape

    def body(p_ref, o_ref):
        acc = p_ref[0]
        for d in range(1, N_DEV):
            acc = acc + p_ref[d]
        o_ref[...] = acc

    return pl.pallas_call(body, name=name, out_shape=jax.ShapeDtypeStruct((R, C), F32),
                          in_specs=[pl.BlockSpec(memory_space=pltpu.VMEM)],
                          out_specs=pl.BlockSpec(memory_space=pltpu.VMEM))(parts)


def _adamw(name, w, g, m, v, copy_grad=False):
    R, C = w.shape
    TR = _pick(R, (128, 64, 32, 16, 8))
    c1 = 1.0 - ADAM_B1 ** ADAM_STEP
    c2 = 1.0 - ADAM_B2 ** ADAM_STEP
    n_out = 4 if copy_grad else 3

    def body(w_ref, g_ref, m_ref, v_ref, d_ref, nm_ref, nv_ref, *g_out):
        g_ = g_ref[...]
        nm = ADAM_B1 * m_ref[...] + (1.0 - ADAM_B1) * g_
        nv = ADAM_B2 * v_ref[...] + (1.0 - ADAM_B2) * (g_ * g_)
        d_ref[...] = -ADAM_LR * ((nm / c1) / (jnp.sqrt(nv / c2) + ADAM_EPS) + ADAM_WD * w_ref[...])
        nm_ref[...] = nm
        nv_ref[...] = nv
        for ref in g_out:
            ref[...] = g_

    spec = pl.BlockSpec((TR, C), lambda r: (r, 0))
    shp = jax.ShapeDtypeStruct((R, C), F32)
    return pl.pallas_call(body, name=name, grid=(R // TR,), in_specs=[spec] * 4, out_specs=[spec] * n_out,
                          out_shape=[shp] * n_out,
                          compiler_params=_params(("arbitrary",), [((TR, C), F32)] * (4 + n_out)))(w, g, m, v)


def _place():
    x, y, c = lax.axis_index("x"), lax.axis_index("y"), lax.axis_index("c")
    others = [(1 - x, y), (x, 1 - y), (1 - x, 1 - y)]
    return x, y, c, others


def _allgather_small(name, block):
    R, C = block.shape

    def body(x_ref, out_ref, send_sems, recv_sems, local_sem):
        x, y, c, chips = _place()
        me, sibling = (x, y, c), (x, y, 1 - c)

        def rows(px, py, pc):
            return out_ref.at[4 * px + 2 * py + pc]

        def copy(k, blk, to, src=None):
            return pltpu.make_async_remote_copy(
                src_ref=rows(*blk) if src is None else src, dst_ref=rows(*blk),
                send_sem=send_sems.at[k], recv_sem=recv_sems.at[k], device_id=to, device_id_type=MESH)

        mine = pltpu.make_async_copy(x_ref, rows(*me), local_sem)
        mine.start()
        first = [copy(0, me, sibling, src=x_ref)]
        first += [copy(1 + j, me, (*chip, c), src=x_ref) for j, chip in enumerate(chips)]
        for cp in first:
            cp.start()
        passed = [copy(4 + j, (*chip, c), sibling) for j, chip in enumerate(chips)]
        for j, chip in enumerate(chips):
            copy(1 + j, (*chip, c), me).wait_recv()
            passed[j].start()
        copy(0, sibling, me).wait_recv()
        for j, chip in enumerate(chips):
            copy(4 + j, (*chip, 1 - c), me).wait_recv()
        for cp in first + passed:
            cp.wait_send()
        mine.wait()

    return pl.pallas_call(
        body, name=name, out_shape=jax.ShapeDtypeStruct((N_DEV, R, C), F32),
        in_specs=[pl.BlockSpec(memory_space=pltpu.VMEM)], out_specs=pl.BlockSpec(memory_space=pltpu.VMEM),
        scratch_shapes=[pltpu.SemaphoreType.DMA((7,)), pltpu.SemaphoreType.DMA((7,)), pltpu.SemaphoreType.DMA],
    )(block)


def _gather_side(bufs, across, within):
    def rows(ref, chip, half, piece):
        _, r0, n = piece
        return ref.at[2 * chip[0] + chip[1], pl.ds(half * (ref.shape[1] // 2) + r0, n)]

    def copies(ins, outs, send_sems, recv_sems, base):
        x, y, c, chips = _place()
        sibling = (x, y, 1 - c)
        pairs = []

        def add(k, src, dst, to, arrival):
            mk = lambda s, d, dev: pltpu.make_async_remote_copy(
                src_ref=s, dst_ref=d, send_sem=send_sems.at[base + k], recv_sem=recv_sems.at[base + k],
                device_id=dev, device_id_type=MESH)
            pairs.append((mk(src, dst, to), mk(arrival, arrival, (x, y, c))))

        for p, piece in enumerate(across):
            ref = outs[piece[0]]
            for j, chip in enumerate(chips):
                mine = rows(ref, (x, y), c, piece)
                add(3 * p + j, mine, mine, (*chip, c), rows(ref, chip, c, piece))
        for q, piece in enumerate(within):
            ref = outs[piece[0]]
            for j, chip in enumerate(chips):
                held = rows(ref, chip, c, piece)
                add(3 * (len(across) + q) + j, held, held, sibling, rows(ref, chip, 1 - c, piece))
        return pairs

    def start(*refs):
        for send, _ in copies(*refs):
            send.start()

    def wait(*refs):
        pairs = copies(*refs)
        for _, arrival in pairs:
            arrival.wait_recv()
        for send, _ in pairs:
            send.wait_send()

    return _Side(list(bufs), [jax.ShapeDtypeStruct(b.shape, b.dtype) for b in bufs],
                 3 * (len(across) + len(within)), start, wait, aliases=tuple((i, i) for i in range(len(bufs))))


def _chip_exchange(parts):
    n = len(parts)

    def copies(ins, outs, send_sems, recv_sems, base):
        x, y, c, chips = _place()
        return [pltpu.make_async_remote_copy(
            src_ref=ins[a].at[2 * chip[0] + chip[1]], dst_ref=outs[a].at[j],
            send_sem=send_sems.at[base + 3 * a + j], recv_sem=recv_sems.at[base + 3 * a + j],
            device_id=(*chip, c), device_id_type=MESH) for a in range(n) for j, chip in enumerate(chips)]

    return _Side(list(parts), [jax.ShapeDtypeStruct((N_CHIPS - 1,) + p.shape[1:], p.dtype) for p in parts],
                 3 * n, *_start_wait(copies))


def _pair_exchange(grads):
    def copies(ins, outs, send_sems, recv_sems, base):
        x, y, c, _ = _place()
        return [pltpu.make_async_remote_copy(
            src_ref=ins[a].at[:, 1 - c], dst_ref=outs[a], send_sem=send_sems.at[base + a],
            recv_sem=recv_sems.at[base + a], device_id=(x, y, 1 - c), device_id_type=MESH)
            for a in range(len(grads))]

    return _Side(list(grads), [jax.ShapeDtypeStruct((N_CHIPS,) + g.shape[2:], F32) for g in grads],
                 len(grads), *_start_wait(copies))


def _start_wait(copies):
    def start(*refs):
        for cp in copies(*refs):
            cp.start()

    def wait(*refs):
        cps = copies(*refs)
        for cp in cps:
            cp.wait_recv()
        for cp in cps:
            cp.wait_send()
    return start, wait


def _both(first, second):
    n_in, n_out = len(first.ins), len(first.out_shapes)

    def run(which):
        def go(ins, outs, send_sems, recv_sems, base):
            getattr(first, which)(ins[:n_in], outs[:n_out], send_sems, recv_sems, base)
            getattr(second, which)(ins[n_in:], outs[n_out:], send_sems, recv_sems, base + first.n_sems)
        return go

    aliases = first.aliases + tuple((a + n_in, b + n_out) for a, b in second.aliases)
    return _Side(first.ins + second.ins, first.out_shapes + second.out_shapes,
                 first.n_sems + second.n_sems, run("start"), run("wait"), aliases)


def _share_halves(halves):
    def copies(ins, outs, send_sems, recv_sems, base):
        x, y, c, _ = _place()
        pairs = []
        for a in range(len(halves)):
            mk = lambda s, d, dev, a=a: pltpu.make_async_remote_copy(
                src_ref=s, dst_ref=d, send_sem=send_sems.at[base + a], recv_sem=recv_sems.at[base + a],
                device_id=dev, device_id_type=MESH)
            theirs = outs[a].at[1 - c]
            pairs.append((mk(outs[a].at[c], outs[a].at[c], (x, y, 1 - c)), mk(theirs, theirs, (x, y, c))))
        return pairs

    def start(*refs):
        for send, _ in copies(*refs):
            send.start()

    def wait(*refs):
        pairs = copies(*refs)
        for _, arrival in pairs:
            arrival.wait_recv()
        for send, _ in pairs:
            send.wait_send()

    return _Side(list(halves), [jax.ShapeDtypeStruct(h.shape, F32) for h in halves], len(halves), start, wait,
                 aliases=tuple((i, i) for i in range(len(halves))))


def _pack(arrays):
    pieces = []
    for a in arrays:
        flat = a.reshape(-1).astype(F32)
        pieces.append(jnp.pad(flat, (0, (-flat.size) % PACK_ALIGN)))
    return jnp.concatenate(pieces).reshape(-1, LANES)


def _unpack(buf, shapes):
    lead = buf.shape[:-2]
    flat = buf.reshape(lead + (-1,))
    out, off = [], 0
    for shp in shapes:
        size = 1
        for s in shp:
            size *= s
        out.append(flat[..., off:off + size].reshape(lead + tuple(shp)))
        off += size + (-size) % PACK_ALIGN
    return out


def _gather_channels(buf, shapes):
    per_chip = _unpack(buf[0::2], shapes)
    return [jnp.transpose(a, (1, 0, 2)).reshape(a.shape[1], -1) for a in per_chip]


def _mm_tile(n, rows, limit_bytes=6 * 1024 * 1024):
    for t in (1408, 1280, 1024, 640, 512, 384, 256, 128):
        if n % t == 0 and rows * t * 2 <= limit_bytes:
            return t
    raise ValueError(f"no column tile for {n} x {rows}")


def kernel(x, p, norm_mix_g, w_in, conv_a_w, conv_a_b, ln_a_g, ln_a_b, conv_b_w, w_out, norm_ffn_g, w_up, conv_ffn_w, w_down, w_ple_gate, b_ple_gate, w_ple_proj, norm_final_g, loss_target, m_norm_mix_g, m_w_in, m_conv_a_w, m_conv_a_b, m_ln_a_g, m_ln_a_b, m_conv_b_w, m_w_out, m_norm_ffn_g, m_w_up, m_conv_ffn_w, m_w_down, m_w_ple_gate, m_b_ple_gate, m_w_ple_proj, m_norm_final_g, v_norm_mix_g, v_w_in, v_conv_a_w, v_conv_a_b, v_ln_a_g, v_ln_a_b, v_conv_b_w, v_w_out, v_norm_ffn_g, v_w_up, v_conv_ffn_w, v_w_down, v_w_ple_gate, v_b_ple_gate, v_w_ple_proj, v_norm_final_g):
    S, D = x.shape[1], x.shape[2]
    P = p.shape[3]
    A = conv_a_b.shape[1]
    F = w_down.shape[1] * N_CHIPS
    KA, KB, KF = conv_a_w.shape[1], conv_b_w.shape[1], conv_ffn_w.shape[1]
    xi, yi, ci = lax.axis_index("x"), lax.axis_index("y"), lax.axis_index("c")
    chip = 2 * xi + yi

    TM = _pick(S, (512, 256, 128))
    TE = _pick(S, (256, 128))
    TC = _pick(2 * F // N_CHIPS, (1408, 1024, 512, 256, 128))
    ffn_place = _pair_tile(F // TC)

    x2, p2, t2 = x.reshape(S, D), p.reshape(S, P), loss_target.reshape(S, D)
    gfin = norm_final_g.reshape(1, D)

    big = dict(w_in=w_in[0], w_out=w_out[0], w_up=w_up[0], w_down=w_down[0],
               w_ple_gate=w_ple_gate[0], w_ple_proj=w_ple_proj[0])
    names = list(big)
    buf = {n: _cast_into_gathered("cast_" + n, big[n], chip) for n in names}
    half = {n: big[n].shape[0] // 2 for n in names}
    up_a = half["w_up"] // 2
    (w_in3,) = _comm_only("gather_w_in_across", _gather_side([buf["w_in"]], [(0, 0, half["w_in"])], []))
    (w_in3,) = _comm_only("gather_w_in_within", _gather_side([w_in3], [], [(0, 0, half["w_in"])]))

    tap_shapes = [(KA, A // N_CHIPS), (KB, A // N_CHIPS), (KF, 2 * F // N_CHIPS)]
    taps = _allgather_small("allgather_taps", _pack([conv_a_w[0], conv_b_w[0], conv_ffn_w[0]]))
    conv_a_f, conv_b_f, conv_ffn_f = _gather_channels(taps, tap_shapes)

    def rms_prologue(rows, row_r, vec_r, ro_r, ao_r):
        h = row_r[0][rows, :]
        hn = (h * _rms_stats(h) * vec_r[0][...]).astype(BF16)
        ro_r[0][rows, :] = hn
        return [hn]

    def cast_prologue(rows, row_r, vec_r, ro_r, ao_r):
        hb = row_r[0][rows, :].astype(BF16)
        ro_r[0][rows, :] = hb
        return [hb]

    plain = lambda accs, tile_r, cv_r: [accs[0]]
    residual = lambda accs, tile_r, cv_r: [tile_r[0][...] + accs[0]]

    (z, hn1), (w_out_t, w_up_t) = _rows_mm(
        "in_proj", S, TM, 5 * A, _mm_tile(5 * A // N_CHIPS, D), row_ins=[x2], vec_ins=[norm_mix_g],
        weights=[(w_in3, "nn3")], tile_outs=[BF16], row_outs=[(D, BF16)], prologue=rms_prologue, epilogue=plain,
        side=_gather_side([buf["w_out"], buf["w_up"]], [(0, 0, half["w_out"]), (1, 0, up_a)], []))
    (a1, cat), (w_out3, w_up_t) = _mixer_fwd(
        z, conv_a_f, conv_a_b, ln_a_g, ln_a_b, conv_b_f, S, TE, A,
        side=_gather_side([w_out_t, w_up_t], [(1, up_a, half["w_up"] - up_a)],
                          [(0, 0, half["w_out"]), (1, 0, up_a)]))
    w_out_f = w_out3.reshape(2 * A, D)
    (h1,), (w_up3, w_proj_t) = _rows_mm(
        "out_proj", S, TM, D, _mm_tile(D, 2 * A), row_ins=[cat], weights=[(w_out_f, "nn2")],
        tile_ins=[x2], tile_outs=[F32], epilogue=residual,
        side=_gather_side([w_up_t, buf["w_ple_proj"]], [(1, 0, half["w_ple_proj"])],
                          [(0, up_a, half["w_up"] - up_a)]))
    (u0, hn2), (w_down_t, w_gate_t, w_proj3) = _rows_mm(
        "up_proj", S, TM, 2 * F, TC, row_ins=[h1], vec_ins=[norm_ffn_g],
        weights=[(w_up3, "nn3")], tile_outs=[BF16], row_outs=[(D, BF16)],
        prologue=rms_prologue, epilogue=plain, place=ffn_place,
        side=_gather_side([buf["w_down"], buf["w_ple_gate"], w_proj_t],
                          [(0, 0, half["w_down"]), (1, 0, half["w_ple_gate"])], [(2, 0, half["w_ple_proj"])]))
    (act, conv_u0), (w_down3, w_gate3) = _ffn_act(
        u0, conv_ffn_f, S, TM, F, TC,
        side=_gather_side([w_down_t, w_gate_t], [], [(0, 0, half["w_down"]), (1, 0, half["w_ple_gate"])]))
    w_down_f = w_down3.reshape(F, D)
    w_gate_f = w_gate3.reshape(D, D)
    w_proj_f = jnp.transpose(w_proj3, (1, 0, 2)).reshape(P, D)
    (h2,) = _rows_mm("down_proj", S, TM, D, _mm_tile(D, F), row_ins=[act], weights=[(w_down_f, "nn2")],
                     tile_ins=[h1], tile_outs=[F32], epilogue=residual)

    loss_part, g_norm_final, g_b_gate, dh3, dpre, dpp, h2b, pb = _ple_loss(
        h2, p2, t2, w_gate_f, w_proj_f, b_ple_gate, gfin, S, TE)

    TK = _pick(S, (1024, 512, 256, 128))
    wt = lambda n: _pick(n, (1408, 1280, 1024, 512, 256, 128))
    chip_sums, from_chips = {}, {}

    def to_sibling(parts):
        ns = list(parts)
        halves = [parts[n].reshape(N_CHIPS, 2, big[n].shape[0] // 2, big[n].shape[1]) for n in ns]
        return ns, halves, _pair_exchange(halves)

    def to_chips(ns, halves, from_sibling):
        sums = [_add_pair("pair_sum_" + n, h, r, ci) for n, h, r in zip(ns, halves, from_sibling)]
        for n, (s, _) in zip(ns, sums):
            chip_sums[n] = s
        return ns, _chip_exchange([b for _, b in sums])

    def landed(ns, side_outs):
        for n, r in zip(ns, side_outs):
            from_chips[n] = r

    ns, halves, side = to_sibling(dict(
        w_ple_gate=_tn_mm("dw_ple_gate", h2b, dpre, wt(D), wt(D), TK),
        w_ple_proj=_tn_mm("dw_ple_proj", pb, dpp, wt(P), wt(D // N_CHIPS), TK, cols_per_chip=D // N_CHIPS)))
    (dh2,), got = _rows_mm("ple_bwd", S, TM, D, _mm_tile(D, D), row_ins=[dpre], weights=[(w_gate_f, "nt2")],
                           tile_ins=[dh3], tile_outs=[F32], epilogue=residual, side=side)
    ple_ns, ple_chips = to_chips(ns, halves, got)
    (dact, dh2b), got = _rows_mm("down_bwd", S, TM, F, _mm_tile(F, D), row_ins=[dh2], weights=[(w_down_f, "nt2")],
                                 tile_outs=[BF16], row_outs=[(D, BF16)], prologue=cast_prologue, epilogue=plain,
                                 side=ple_chips)
    landed(ple_ns, got)
    ns, halves, side = to_sibling(dict(w_down=_tn_mm("dw_down", act, dh2b, wt(F), wt(D), TK)))
    (du0, g_conv_gate, g_conv_up), got = _ffn_bwd(u0, conv_u0, dact, conv_ffn_f, S, TM, F, TC, side=side)
    down_ns, down_chips = to_chips(ns, halves, got)
    g_conv_ffn = jnp.concatenate([g_conv_gate, g_conv_up], axis=1)
    ns, halves, side = to_sibling(dict(
        w_up=_tn_mm("dw_up", hn2, du0, wt(D), TC, TK, cols_per_chip=2 * F // N_CHIPS, place=ffn_place)))

    def up_bwd_epilogue(acc, rows, row_r, vec_r, ro_r, ao_r):
        dh, dg = _rms_bwd(row_r[0][rows, :], vec_r[0][...], acc)
        dh1_ = row_r[1][rows, :] + dh
        ro_r[0][rows, :] = dh1_
        ro_r[1][rows, :] = dh1_.astype(BF16)
        ao_r[0][...] += dg

    (dh1, dh1b, g_norm_ffn), got = _kloop_mm(
        "up_bwd", S, TM, du0, w_up3, TC, row_ins=[h1, dh2], vec_ins=[norm_ffn_g],
        row_outs=[(D, F32), (D, BF16)], acc_outs=[(1, D)], epilogue=up_bwd_epilogue, place=ffn_place,
        side=_both(down_chips, side))
    landed(down_ns, got[:len(down_ns)])
    up_ns, up_chips = to_chips(ns, halves, got[len(down_ns):])
    ns, halves, side = to_sibling(dict(w_out=_tn_mm("dw_out", cat, dh1b, wt(2 * A), wt(D), TK)))
    (dcat,), got = _rows_mm("out_bwd", S, TM, 2 * A, _mm_tile(2 * A, D), row_ins=[dh1b],
                            weights=[(w_out_f, "nt2")], tile_outs=[BF16], epilogue=plain, side=side)
    out_ns, out_chips = to_chips(ns, halves, got)
    da1, ln_sums = _mixer_bwd_ln(dcat, a1, ln_a_g, ln_a_b, S, TE, A)
    (dz, g_conv_a, g_conv_b), got = _mixer_bwd_conv(z, dcat, da1, conv_a_f, conv_b_f, S, TE, A,
                                                    side=_both(up_chips, out_chips))
    landed(up_ns + out_ns, got)
    ns, halves, side = to_sibling(dict(
        w_in=_tn_mm("dw_in", hn1, dz, wt(D), wt(5 * A // N_CHIPS), TK, cols_per_chip=5 * A // N_CHIPS)))
    ns, side = to_chips(ns, halves, _comm_only("grads_exchange_pairs_in", side))

    def in_bwd_epilogue(acc, rows, row_r, vec_r, ro_r, ao_r):
        dh, dg = _rms_bwd(row_r[0][rows, :], vec_r[0][...], acc)
        ro_r[0][rows, :] = row_r[1][rows, :] + dh
        ao_r[0][...] += dg

    early = [n for n in names if n != "w_in"]
    early_halves = [_add_chips("chip_sum_" + n, chip_sums[n], from_chips[n], chip, ci) for n in early]
    (dx, g_norm_mix), got = _kloop_mm(
        "in_bwd", S, TM, dz, w_in3, _mm_tile(5 * A // N_CHIPS, D), row_ins=[x2, dh1],
        vec_ins=[norm_mix_g], row_outs=[(D, F32)], acc_outs=[(1, D)], epilogue=in_bwd_epilogue,
        side=_both(side, _share_halves(early_halves)))
    landed(ns, got[:1])
    shared = dict(zip(early, got[1:]))
    (shared["w_in"],) = _comm_only("grads_share_w_in", _share_halves(
        [_add_chips("chip_sum_w_in", chip_sums["w_in"], from_chips["w_in"], chip, ci)]))

    reduced = [shared[n] for n in names]
    moments = dict(w_in=(m_w_in, v_w_in), w_out=(m_w_out, v_w_out), w_up=(m_w_up, v_w_up),
                   w_down=(m_w_down, v_w_down), w_ple_gate=(m_w_ple_gate, v_w_ple_gate),
                   w_ple_proj=(m_w_ple_proj, v_w_ple_proj))
    grads, deltas, new_m, new_v = {}, {}, {}, {}
    for n, g in zip(names, reduced):
        d_, m_, v_, g = _adamw("adamw_" + n, big[n], g.reshape(big[n].shape), moments[n][0][0], moments[n][1][0],
                               copy_grad=True)
        grads[n], deltas[n], new_m[n], new_v[n] = g[None], d_[None], m_[None], v_[None]

    small = ["norm_mix_g", "conv_a_w", "conv_a_b", "ln_a_g", "ln_a_b", "conv_b_w", "norm_ffn_g",
             "conv_ffn_w", "b_ple_gate", "norm_final_g"]
    small_part = [g_norm_mix, g_conv_a, ln_sums[2:3], ln_sums[0:1], ln_sums[1:2], g_conv_b, g_norm_ffn,
                  g_conv_ffn, g_b_gate, g_norm_final]
    full_shapes = [a.shape for a in small_part]
    summed = _sum_devices("small_grads_sum", _allgather_small("allgather_small_grads", _pack(small_part)))
    small_g = dict(zip(small, _unpack(summed, full_shapes)))
    for n, width in (("conv_a_w", A), ("conv_b_w", A), ("conv_ffn_w", 2 * F)):
        small_g[n] = lax.dynamic_slice_in_dim(small_g[n], chip * (width // N_CHIPS), width // N_CHIPS, axis=1)
    small_w = dict(norm_mix_g=(norm_mix_g, m_norm_mix_g, v_norm_mix_g), conv_a_w=(conv_a_w, m_conv_a_w, v_conv_a_w),
                   conv_a_b=(conv_a_b, m_conv_a_b, v_conv_a_b), ln_a_g=(ln_a_g, m_ln_a_g, v_ln_a_g),
                   ln_a_b=(ln_a_b, m_ln_a_b, v_ln_a_b), conv_b_w=(conv_b_w, m_conv_b_w, v_conv_b_w),
                   norm_ffn_g=(norm_ffn_g, m_norm_ffn_g, v_norm_ffn_g),
                   conv_ffn_w=(conv_ffn_w, m_conv_ffn_w, v_conv_ffn_w),
                   b_ple_gate=(b_ple_gate, m_b_ple_gate, v_b_ple_gate),
                   norm_final_g=(norm_final_g, m_norm_final_g, v_norm_final_g))
    out_shapes = [small_w[n][0].shape for n in small]
    packed_g = _pack([small_g[n] for n in small])
    packed = [_pack([small_w[n][k] for n in small]) for k in range(3)]
    d_s, m_s, v_s = _adamw("adamw_small", packed[0], packed_g, packed[1], packed[2])
    for n, g, d_, m_, v_ in zip(small, _unpack(packed_g, out_shapes), _unpack(d_s, out_shapes),
                                _unpack(m_s, out_shapes), _unpack(v_s, out_shapes)):
        grads[n], deltas[n], new_m[n], new_v[n] = g, d_, m_, v_

    order = ["norm_mix_g", "w_in", "conv_a_w", "conv_a_b", "ln_a_g", "ln_a_b", "conv_b_w", "w_out", "norm_ffn_g",
             "w_up", "conv_ffn_w", "w_down", "w_ple_gate", "b_ple_gate", "w_ple_proj", "norm_final_g"]
    loss = lax.psum(loss_part[0, 0], ("x", "y", "c"))
    return (loss, dx.reshape(x.shape), *[grads[n] for n in order], *[deltas[n] for n in order],
            *[new_m[n] for n in order], *[new_v[n] for n in order])
```

```python
from typing import Callable, NamedTuple

import jax
import jax.numpy as jnp
from jax import lax
from jax.experimental import pallas as pl
from jax.experimental.pallas import tpu as pltpu

F32 = jnp.float32
BF16 = jnp.bfloat16
MESH = pl.DeviceIdType.MESH
ANY = pl.BlockSpec(memory_space=pl.ANY)

EPS = 1e-6
ADAM_LR = 0.001
ADAM_B1 = 0.9
ADAM_B2 = 0.999
ADAM_EPS = 1e-08
ADAM_WD = 0.01
ADAM_STEP = 10

N_CHIPS = 4
N_DEV = 8
LANES = 128
SUBLANES = 8
PACK_ALIGN = LANES * SUBLANES
ROW_CHUNK = 32
VMEM_CAP = 60 * 1024 * 1024
VMEM_SLACK = 6 * 1024 * 1024


def _pick(n, cands):
    for c in cands:
        if n % c == 0:
            return c
    raise ValueError(f"no tile of {cands} divides {n}")


def _nbytes(shape, dtype):
    n = 1
    for s in shape:
        if s is not None:
            n *= s
    return n * jnp.dtype(dtype).itemsize


def _params(sem, blocks, scratch=(), temps=()):
    est = (2 * sum(_nbytes(s, d) for s, d in blocks) + sum(_nbytes(s, d) for s, d in scratch)
           + sum(_nbytes(s, d) for s, d in temps))
    return pltpu.CompilerParams(dimension_semantics=sem,
                                vmem_limit_bytes=min(est + VMEM_SLACK, VMEM_CAP))


def _sigmoid(x):
    return 1.0 / (1.0 + jnp.exp(-x))


def _rsum(x):
    return jnp.sum(x, axis=0, keepdims=True)


class _Side(NamedTuple):
    ins: list
    out_shapes: list
    n_sems: int
    start: Callable
    wait: Callable
    aliases: tuple = ()


def _call(body, side, *, name, grid, in_specs, out_specs, out_shape, scratch, params, args):
    vmem = [pltpu.VMEM(s, d) for s, d in scratch]
    if side is None:
        outs = pl.pallas_call(body, name=name, grid=grid, in_specs=in_specs, out_specs=out_specs,
                              out_shape=out_shape, scratch_shapes=vmem, compiler_params=params)(*args)
        return list(outs), []
    n_in, n_out, n_sc = len(in_specs), len(out_specs), len(scratch)
    ns_in, ns_out = len(side.ins), len(side.out_shapes)

    def carrier(*refs):
        pos = [0]
        def take(n):
            pos[0] += n
            return refs[pos[0] - n:pos[0]]
        ins, s_ins, outs, s_outs, scr = take(n_in), take(ns_in), take(n_out), take(ns_out), take(n_sc)
        send_sems, recv_sems = take(2)
        first = last = None
        for axis, extent in enumerate(grid):
            at_start, at_end = pl.program_id(axis) == 0, pl.program_id(axis) == extent - 1
            first = at_start if first is None else first & at_start
            last = at_end if last is None else last & at_end

        @pl.when(first)
        def _():
            side.start(s_ins, s_outs, send_sems, recv_sems, 0)
        body(*ins, *outs, *scr)

        @pl.when(last)
        def _():
            side.wait(s_ins, s_outs, send_sems, recv_sems, 0)

    outs = pl.pallas_call(
        carrier, name=name, grid=grid, in_specs=list(in_specs) + [ANY] * ns_in,
        out_specs=list(out_specs) + [ANY] * ns_out, out_shape=list(out_shape) + list(side.out_shapes),
        scratch_shapes=vmem + [pltpu.SemaphoreType.DMA((side.n_sems,)), pltpu.SemaphoreType.DMA((side.n_sems,))],
        input_output_aliases={n_in + a: n_out + b for a, b in side.aliases},
        compiler_params=params)(*args, *side.ins)
    return list(outs[:n_out]), list(outs[n_out:])


def _comm_only(name, side):
    n_in = len(side.ins)

    def body(*refs):
        ins, outs = refs[:n_in], refs[n_in:n_in + len(side.out_shapes)]
        send_sems, recv_sems = refs[n_in + len(side.out_shapes):]
        side.start(ins, outs, send_sems, recv_sems, 0)
        side.wait(ins, outs, send_sems, recv_sems, 0)

    return pl.pallas_call(
        body, name=name, out_shape=list(side.out_shapes), in_specs=[ANY] * n_in,
        out_specs=[ANY] * len(side.out_shapes),
        scratch_shapes=[pltpu.SemaphoreType.DMA((side.n_sems,)), pltpu.SemaphoreType.DMA((side.n_sems,))],
        input_output_aliases=dict(side.aliases),
    )(*side.ins)


def _rms_stats(x):
    return lax.rsqrt(jnp.mean(x * x, axis=-1, keepdims=True) + EPS)


def _rms_bwd(h, g, dout):
    r = _rms_stats(h)
    n = h * r
    dn = dout * g
    dh = r * (dn - n * jnp.mean(dn * n, axis=-1, keepdims=True))
    return dh, _rsum(dout * n)


def _identity(t):
    return t


def _chip_major(nb, place=_identity):
    return lambda i, j: (place(j) // nb, 0, place(j) % nb)


def _rows_mm(name, S, TM, N, TN, *, row_ins, vec_ins=(), colvec_ins=(), weights, tile_ins=(),
             tile_outs, row_outs=(), acc_outs=(), prologue=None, epilogue, place=_identity, side=None):
    nI, nJ = S // TM, N // TN
    n_row, n_vec, n_cv, n_w, n_tile = len(row_ins), len(vec_ins), len(colvec_ins), len(weights), len(tile_ins)
    n_to, n_ro, n_ao = len(tile_outs), len(row_outs), len(acc_outs)

    in_specs, blocks, scratch, ks = [], [], [], []
    for a in row_ins:
        in_specs.append(pl.BlockSpec((TM, a.shape[1]), lambda i, j: (i, 0)))
        blocks.append(((TM, a.shape[1]), a.dtype))
    for a in vec_ins:
        in_specs.append(pl.BlockSpec(a.shape, lambda i, j: (0, 0)))
        blocks.append((a.shape, a.dtype))
    for a in colvec_ins:
        in_specs.append(pl.BlockSpec((1, TN), lambda i, j: (0, j)))
        blocks.append(((1, TN), a.dtype))
    for w, mode in weights:
        if mode == "nn2":
            k = w.shape[0]
            in_specs.append(pl.BlockSpec((k, TN), lambda i, j: (0, j)))
        elif mode == "nn3":
            k = w.shape[1]
            in_specs.append(pl.BlockSpec((None, k, TN), _chip_major(w.shape[2] // TN, place)))
        else:
            k = w.shape[1]
            in_specs.append(pl.BlockSpec((TN, k), lambda i, j: (j, 0)))
        ks.append(k)
        blocks.append(((k, TN), BF16))
        if prologue is not None:
            scratch.append(((TM, k), BF16))
    for a in tile_ins:
        in_specs.append(pl.BlockSpec((TM, TN), lambda i, j: (i, j)))
        blocks.append(((TM, TN), a.dtype))

    out_shape, out_specs = [], []
    for dt in tile_outs:
        out_shape.append(jax.ShapeDtypeStruct((S, N), dt))
        out_specs.append(pl.BlockSpec((TM, TN), lambda i, j: (i, j)))
        blocks.append(((TM, TN), dt))
    for width, dt in row_outs:
        out_shape.append(jax.ShapeDtypeStruct((S, width), dt))
        out_specs.append(pl.BlockSpec((TM, width), lambda i, j: (i, 0)))
        blocks.append(((TM, width), dt))
    for rows, width in acc_outs:
        out_shape.append(jax.ShapeDtypeStruct((rows, width), F32))
        out_specs.append(pl.BlockSpec((rows, width), lambda i, j: (0, 0)))
        blocks.append(((rows, width), F32))

    modes = [m for _, m in weights]

    def body(*refs):
        pos = 0
        def take(n):
            nonlocal pos
            out = refs[pos:pos + n]
            pos += n
            return out
        row_r, vec_r, cv_r, w_r, tile_r = take(n_row), take(n_vec), take(n_cv), take(n_w), take(n_tile)
        to_r, ro_r, ao_r, a_sc = take(n_to), take(n_ro), take(n_ao), take(len(scratch))
        i, j = pl.program_id(0), pl.program_id(1)

        if prologue is None:
            a_sc = row_r[:n_w]
        else:
            @pl.when(j == 0)
            def _():
                if n_ao:
                    @pl.when(i == 0)
                    def _():
                        for r in ao_r:
                            r[...] = jnp.zeros_like(r)

                def chunk(ci, carry):
                    rows = pl.ds(pl.multiple_of(ci * ROW_CHUNK, ROW_CHUNK), ROW_CHUNK)
                    for sc, a in zip(a_sc, prologue(rows, row_r, vec_r, ro_r, ao_r)):
                        sc[rows, :] = a
                    return carry
                lax.fori_loop(0, TM // ROW_CHUNK, chunk, 0)

        accs = []
        for w_ref, sc, mode in zip(w_r, a_sc, modes):
            if mode == "nt2":
                accs.append(lax.dot_general(sc[...], w_ref[...], (((1,), (1,)), ((), ())),
                                            preferred_element_type=F32))
            else:
                accs.append(jnp.dot(sc[...], w_ref[...], preferred_element_type=F32))
        outs = epilogue(accs, tile_r, cv_r)
        for r, o in zip(to_r, outs):
            r[...] = o.astype(r.dtype)

    outs, side_outs = _call(
        body, side, name=name, grid=(nI, nJ), in_specs=in_specs, out_specs=out_specs, out_shape=out_shape,
        scratch=scratch, params=_params(("arbitrary", "arbitrary"), blocks, scratch, temps=[((TM, TN), F32)] * 3),
        args=[*row_ins, *vec_ins, *colvec_ins, *[w for w, _ in weights], *tile_ins])
    return outs if side is None else (outs, side_outs)


def _kloop_mm(name, S, TM, a, w3, TK, *, row_ins, vec_ins, row_outs, acc_outs, epilogue, place=_identity,
              side=None):
    _, N, Ks = w3.shape
    nb = Ks // TK
    nK = N_CHIPS * nb
    n_row, n_vec, n_ro, n_ao = len(row_ins), len(vec_ins), len(row_outs), len(acc_outs)

    in_specs = [pl.BlockSpec((TM, TK), lambda i, k: (i, k)),
                pl.BlockSpec((None, N, TK), _chip_major(nb, place))]
    blocks = [((TM, TK), BF16), ((N, TK), BF16)]
    for r in row_ins:
        in_specs.append(pl.BlockSpec((TM, r.shape[1]), lambda i, k: (i, 0)))
        blocks.append(((TM, r.shape[1]), r.dtype))
    for v in vec_ins:
        in_specs.append(pl.BlockSpec(v.shape, lambda i, k: (0, 0)))
        blocks.append((v.shape, v.dtype))
    out_shape, out_specs = [], []
    for width, dt in row_outs:
        out_shape.append(jax.ShapeDtypeStruct((S, width), dt))
        out_specs.append(pl.BlockSpec((TM, width), lambda i, k: (i, 0)))
        blocks.append(((TM, width), dt))
    for rows, width in acc_outs:
        out_shape.append(jax.ShapeDtypeStruct((rows, width), F32))
        out_specs.append(pl.BlockSpec((rows, width), lambda i, k: (0, 0)))
        blocks.append(((rows, width), F32))
    scratch = [((TM, N), F32)]

    def body(*refs):
        a_ref, w_ref = refs[0], refs[1]
        row_r = refs[2:2 + n_row]
        vec_r = refs[2 + n_row:2 + n_row + n_vec]
        pos = 2 + n_row + n_vec
        ro_r = refs[pos:pos + n_ro]
        ao_r = refs[pos + n_ro:pos + n_ro + n_ao]
        acc_sc = refs[pos + n_ro + n_ao]
        i, k = pl.program_id(0), pl.program_id(1)
        @pl.when(k == 0)
        def _():
            acc_sc[...] = jnp.zeros_like(acc_sc)
        acc_sc[...] += lax.dot_general(a_ref[...], w_ref[...], (((1,), (1,)), ((), ())),
                                       preferred_element_type=F32)

        @pl.when(k == nK - 1)
        def _():
            @pl.when(i == 0)
            def _():
                for r in ao_r:
                    r[...] = jnp.zeros_like(r)

            def chunk(ci, carry):
                rows = pl.ds(pl.multiple_of(ci * ROW_CHUNK, ROW_CHUNK), ROW_CHUNK)
                epilogue(acc_sc[rows, :], rows, row_r, vec_r, ro_r, ao_r)
                return carry
            lax.fori_loop(0, TM // ROW_CHUNK, chunk, 0)

    outs, side_outs = _call(
        body, side, name=name, grid=(S // TM, nK), in_specs=in_specs, out_specs=out_specs, out_shape=out_shape,
        scratch=scratch, params=_params(("arbitrary", "arbitrary"), blocks, scratch, temps=[((TM, N), F32)]),
        args=[a, w3, *row_ins, *vec_ins])
    return outs if side is None else (outs, side_outs)


def _tn_mm(name, a, b, TMw, TNw, TK, cols_per_chip=None, place=_identity):
    S, M = a.shape
    N = b.shape[1]
    nK = S // TK
    if cols_per_chip is None:
        out_shape = jax.ShapeDtypeStruct((M, N), F32)
        out_spec = pl.BlockSpec((TMw, TNw), lambda i, j, k: (i, j))
    else:
        nb = cols_per_chip // TNw
        out_shape = jax.ShapeDtypeStruct((N_CHIPS, M, cols_per_chip), F32)
        out_spec = pl.BlockSpec((None, TMw, TNw), lambda i, j, k: (place(j) // nb, i, place(j) % nb))

    def body(a_ref, b_ref, o_ref):
        @pl.when(pl.program_id(2) == 0)
        def _():
            o_ref[...] = jnp.zeros_like(o_ref)
        o_ref[...] += lax.dot_general(a_ref[...], b_ref[...], (((0,), (0,)), ((), ())),
                                      preferred_element_type=F32)

    blocks = [((TK, TMw), BF16), ((TK, TNw), BF16), ((TMw, TNw), F32)]
    return pl.pallas_call(
        body, name=name, grid=(M // TMw, N // TNw, nK),
        in_specs=[pl.BlockSpec((TK, TMw), lambda i, j, k: (k, i)),
                  pl.BlockSpec((TK, TNw), lambda i, j, k: (k, j))],
        out_specs=out_spec, out_shape=out_shape,
        compiler_params=_params(("arbitrary", "arbitrary", "arbitrary"), blocks,
                                temps=[((TMw, TNw), F32), ((TK, TMw), BF16)]),
    )(a, b)


def _prev_rows(TM, H, col):
    return lambda i: (jnp.maximum(i * (TM // H) - 1, 0), col)


def _next_rows(S, TM, H, col):
    return lambda i: (jnp.minimum((i + 1) * (TM // H), S // H - 1), col)


def _taps_causal(ext_ref, w_ref, K, H, TM, cs):
    acc = None
    for k in range(K):
        term = ext_ref[pl.ds(H - (K - 1) + k, TM), cs] * w_ref[pl.ds(k, 1), cs]
        acc = term if acc is None else acc + term
    return acc


def _taps_anticausal(ext_ref, w_ref, K, TM, cs):
    acc = None
    for k in range(K):
        term = ext_ref[pl.ds(K - 1 - k, TM), cs] * w_ref[pl.ds(k, 1), cs]
        acc = term if acc is None else acc + term
    return acc


def _tap_grads(ext_ref, g, K, H, TM, cs):
    return [_rsum(ext_ref[pl.ds(H - (K - 1) + k, TM), cs] * g) for k in range(K)]


def _shift_copies(ext_ref, shifted, cs):
    n = shifted.shape[1]
    for r in range(1, SUBLANES):
        shifted[r - 1] = ext_ref[pl.ds(r, n), cs]


def _rows_at(ext_ref, shifted, start, n, cs):
    q, r = divmod(start, SUBLANES)
    if r == 0:
        return ext_ref[pl.ds(start, n), cs]
    return shifted[r - 1, pl.ds(SUBLANES * q, n), :]


def _mixer_fwd(z, conv_a_w, conv_a_b, ln_g, ln_b, conv_b_w, S, TM, A, side=None):
    H = 32
    KA, KB = conv_a_w.shape[0], conv_b_w.shape[0]
    n_chunks = A // LANES
    RB = _pick(TM, (64, 32))

    def body(zc_ref, zh_ref, wa_ref, ba_ref, g_ref, b_ref, wb_ref, a1_ref, cat_ref, ext_a, ext_b, shifted):
        i = pl.program_id(0)
        live = (i > 0).astype(F32)
        zc = zc_ref[...].astype(F32)
        zh = zh_ref[...].astype(F32) * live
        ext_a[pl.ds(0, H), :] = zh[:, 0:A] * _sigmoid(zh[:, A:2 * A])
        ext_a[pl.ds(H, TM), :] = zc[:, 0:A] * _sigmoid(zc[:, A:2 * A])
        ext_b[pl.ds(0, H), :] = zh[:, 3 * A:4 * A] * zh[:, 4 * A:5 * A]
        ext_b[pl.ds(H, TM), :] = zc[:, 3 * A:4 * A] * zc[:, 4 * A:5 * A]

        def chunk(c, carry):
            cs = pl.ds(pl.multiple_of(c * LANES, LANES), LANES)
            _shift_copies(ext_a, shifted, cs)
            for r0 in range(0, TM, RB):
                acc = None
                for k in range(KA):
                    term = _rows_at(ext_a, shifted, H - (KA - 1) + k + r0, RB, cs) * wa_ref[pl.ds(k, 1), cs]
                    acc = term if acc is None else acc + term
                a1_ref[pl.ds(r0, RB), cs] = acc + ba_ref[:, cs]
            return carry
        lax.fori_loop(0, n_chunks, chunk, 0)

        a1 = a1_ref[...]
        mu = jnp.mean(a1, axis=-1, keepdims=True)
        d = a1 - mu
        var = jnp.mean(d * d, axis=-1, keepdims=True)
        a2 = d * lax.rsqrt(var + EPS) * g_ref[...] + b_ref[...]
        cat_ref[:, 0:A] = (a2 * _sigmoid(a2)).astype(BF16)
        cbc = _taps_causal(ext_b, wb_ref, KB, H, TM, slice(None))
        cat_ref[:, A:2 * A] = (zc[:, 2 * A:3 * A] * cbc).astype(BF16)

    blocks = [((TM, 5 * A), BF16), ((H, 5 * A), BF16), ((KA, A), F32), ((KB, A), F32),
              ((TM, A), F32), ((TM, 2 * A), BF16)]
    scratch = [((H + TM, A), F32), ((H + TM, A), F32), ((SUBLANES - 1, H + TM - SUBLANES, LANES), F32)]
    vec = lambda r: pl.BlockSpec((r, A), lambda i: (0, 0))
    outs, side_outs = _call(
        body, side, name="mixer_fwd", grid=(S // TM,),
        in_specs=[pl.BlockSpec((TM, 5 * A), lambda i: (i, 0)),
                  pl.BlockSpec((H, 5 * A), _prev_rows(TM, H, 0)),
                  vec(KA), vec(1), vec(1), vec(1), vec(KB)],
        out_specs=[pl.BlockSpec((TM, A), lambda i: (i, 0)), pl.BlockSpec((TM, 2 * A), lambda i: (i, 0))],
        out_shape=[jax.ShapeDtypeStruct((S, A), F32), jax.ShapeDtypeStruct((S, 2 * A), BF16)],
        scratch=scratch,
        params=_params(("arbitrary",), blocks, scratch, temps=[((TM, 5 * A), F32)] * 2 + [((TM, A), F32)] * 10),
        args=[z, z, conv_a_w, conv_a_b, ln_g, ln_b, conv_b_w])
    return outs if side is None else (outs, side_outs)


def _pair_tile(nF):
    return lambda t: (t % 2) * nF + t // 2


FFN_ROWS = 32


def _bcast_taps(w_ref, K, lanes):
    return [jnp.broadcast_to(w_ref[pl.ds(k, 1), lanes], (FFN_ROWS, LANES)) for k in range(K)]


def _ffn_act(u0, conv_w, S, TM, F, TC, side=None):
    H = 16
    K = conv_w.shape[0]
    nF = F // TC

    def body(uc_ref, uh_ref, wg_ref, wu_ref, o_ref, conv_ref, ext):
        live = (pl.program_id(0) > 0).astype(F32)
        ext[pl.ds(0, H), :] = uh_ref[...].astype(F32) * live
        ext[pl.ds(H, TM), :] = uc_ref[...].astype(F32)

        def lane_chunk(c, carry):
            lo = pl.ds(pl.multiple_of(c * LANES, LANES), LANES)
            lg, lu = lo, pl.ds(pl.multiple_of(TC + c * LANES, LANES), LANES)
            wg, wu = _bcast_taps(wg_ref, K, lo), _bcast_taps(wu_ref, K, lo)
            for r0 in range(0, TM, FFN_ROWS):
                g = u = None
                for k in range(K):
                    rows = pl.ds(H - (K - 1) + k + r0, FFN_ROWS)
                    tg, tu = ext[rows, lg] * wg[k], ext[rows, lu] * wu[k]
                    g, u = (tg, tu) if g is None else (g + tg, u + tu)
                o_ref[pl.ds(r0, FFN_ROWS), lo] = (g * _sigmoid(g) * u).astype(BF16)
                conv_ref[pl.ds(r0, FFN_ROWS), lg] = g.astype(BF16)
                conv_ref[pl.ds(r0, FFN_ROWS), lu] = u.astype(BF16)
            return carry
        lax.fori_loop(0, TC // LANES, lane_chunk, 0)

    blocks = [((TM, 2 * TC), BF16), ((H, 2 * TC), BF16), ((K, TC), F32), ((K, TC), F32), ((TM, TC), BF16),
              ((TM, 2 * TC), BF16)]
    scratch = [((H + TM, 2 * TC), F32)]
    outs, side_outs = _call(
        body, side, name="ffn_act", grid=(S // TM, nF),
        in_specs=[pl.BlockSpec((TM, 2 * TC), lambda i, j: (i, j)),
                  pl.BlockSpec((H, 2 * TC), lambda i, j: (jnp.maximum(i * (TM // H) - 1, 0), j)),
                  pl.BlockSpec((K, TC), lambda i, j: (0, j)),
                  pl.BlockSpec((K, TC), lambda i, j: (0, j + nF))],
        out_specs=[pl.BlockSpec((TM, TC), lambda i, j: (i, j)), pl.BlockSpec((TM, 2 * TC), lambda i, j: (i, j))],
        out_shape=[jax.ShapeDtypeStruct((S, F), BF16), jax.ShapeDtypeStruct((S, 2 * F), BF16)],
        scratch=scratch,
        params=_params(("arbitrary", "arbitrary"), blocks, scratch, temps=[((TM, 2 * TC), F32)]),
        args=[u0, u0, conv_w, conv_w])
    return outs if side is None else (outs, side_outs)


def _ple_loss(h2, p, target, w_gate, w_proj, b_gate, g_final, S, TM):
    D, P = h2.shape[1], p.shape[1]

    def body(h_ref, p_ref, t_ref, wg_ref, wp_ref, b_ref, g_ref,
             loss_ref, dg_ref, db_ref, dh_ref, dpre_ref, dpp_ref, hb_ref, pb_ref, pre_sc, pp_sc):
        @pl.when(pl.program_id(0) == 0)
        def _():
            loss_ref[...] = jnp.zeros_like(loss_ref)
            dg_ref[...] = jnp.zeros_like(dg_ref)
            db_ref[...] = jnp.zeros_like(db_ref)
        hb_ref[...] = h_ref[...].astype(BF16)
        pb_ref[...] = p_ref[...].astype(BF16)
        pre_sc[...] = jnp.dot(hb_ref[...], wg_ref[...], preferred_element_type=F32)
        pp_sc[...] = jnp.dot(pb_ref[...], wp_ref[...], preferred_element_type=F32)

        def chunk(ci, carry):
            rows = pl.ds(pl.multiple_of(ci * ROW_CHUNK, ROW_CHUNK), ROW_CHUNK)
            g = g_ref[...]
            gate = _sigmoid(pre_sc[rows, :] + b_ref[...])
            pp = pp_sc[rows, :]
            h = h_ref[rows, :] + pp * gate
            r = _rms_stats(h)
            n = h * r
            diff = n * g - t_ref[rows, :]
            loss_ref[...] += 0.5 * jnp.sum(jnp.mean(diff * diff, axis=-1, keepdims=True), axis=0, keepdims=True)
            dy = diff * (1.0 / D)
            dn = dy * g
            dh = r * (dn - n * jnp.mean(dn * n, axis=-1, keepdims=True))
            dh_ref[rows, :] = dh
            dg_ref[...] += _rsum(dy * n)
            dpre = dh * pp * gate * (1.0 - gate)
            dpre_ref[rows, :] = dpre.astype(BF16)
            dpp_ref[rows, :] = (dh * gate).astype(BF16)
            db_ref[...] += _rsum(dpre)
            return carry
        lax.fori_loop(0, TM // ROW_CHUNK, chunk, 0)

    row = pl.BlockSpec((TM, D), lambda i: (i, 0))
    prow = pl.BlockSpec((TM, P), lambda i: (i, 0))
    vec = pl.BlockSpec((1, D), lambda i: (0, 0))
    whole = lambda a: pl.BlockSpec(a.shape, lambda i: (0, 0))
    blocks = ([((TM, D), F32)] * 3 + [((TM, P), F32), ((D, D), BF16), ((P, D), BF16)]
              + [((TM, D), BF16)] * 3 + [((TM, P), BF16)])
    scratch = [((TM, D), F32)] * 2
    return pl.pallas_call(
        body, name="ple_loss", grid=(S // TM,),
        in_specs=[row, prow, row, whole(w_gate), whole(w_proj), vec, vec],
        out_specs=[pl.BlockSpec((1, 1), lambda i: (0, 0)), vec, vec, row, row, row, row, prow],
        out_shape=[jax.ShapeDtypeStruct((1, 1), F32), jax.ShapeDtypeStruct((1, D), F32),
                   jax.ShapeDtypeStruct((1, D), F32), jax.ShapeDtypeStruct((S, D), F32),
                   jax.ShapeDtypeStruct((S, D), BF16), jax.ShapeDtypeStruct((S, D), BF16),
                   jax.ShapeDtypeStruct((S, D), BF16), jax.ShapeDtypeStruct((S, P), BF16)],
        scratch_shapes=[pltpu.VMEM(s, d) for s, d in scratch],
        compiler_params=_params(("arbitrary",), blocks, scratch, temps=[((TM, D), F32)] * 2),
    )(h2, p, target, w_gate, w_proj, b_gate, g_final)


def _ffn_bwd(u0, conv_u0, dact, conv_w, S, TM, F, TC, side=None):
    H = FFN_ROWS
    K = conv_w.shape[0]
    nF, nI = F // TC, S // TM

    def body(xc_ref, cc_ref, cn_ref, dc_ref, dn_ref, wg_ref, wu_ref, o_ref, dwg_ref, dwu_ref, ext_d):
        i = pl.program_id(1)
        @pl.when(i == 0)
        def _():
            dwg_ref[...] = jnp.zeros_like(dwg_ref)
            dwu_ref[...] = jnp.zeros_like(dwu_ref)
        last = (i < nI - 1).astype(F32)

        def lane_chunk(c, carry):
            lo = pl.ds(pl.multiple_of(c * LANES, LANES), LANES)
            lg, lu = lo, pl.ds(pl.multiple_of(TC + c * LANES, LANES), LANES)
            wg, wu = _bcast_taps(wg_ref, K, lo), _bcast_taps(wu_ref, K, lo)
            for r0 in range(0, TM + H, FFN_ROWS):
                if r0 < TM:
                    rows = pl.ds(r0, FFN_ROWS)
                    g, u, da = cc_ref[rows, lg], cc_ref[rows, lu], dc_ref[rows, lo].astype(F32)
                else:
                    g, u, da = cn_ref[:, lg], cn_ref[:, lu], dn_ref[:, lo].astype(F32) * last
                g, u = g.astype(F32), u.astype(F32)
                s = _sigmoid(g)
                ext_d[pl.ds(r0, FFN_ROWS), lg] = da * u * s * (1.0 + g * (1.0 - s))
                ext_d[pl.ds(r0, FFN_ROWS), lu] = da * g * s
            sums_g, sums_u = [None] * K, [None] * K
            for r0 in range(0, TM, FFN_ROWS):
                xg = xc_ref[pl.ds(r0, FFN_ROWS), lg].astype(F32)
                xu = xc_ref[pl.ds(r0, FFN_ROWS), lu].astype(F32)
                g = u = None
                for k in range(K):
                    rows = pl.ds(K - 1 - k + r0, FFN_ROWS)
                    dg, du = ext_d[rows, lg], ext_d[rows, lu]
                    tg, tu = dg * wg[k], du * wu[k]
                    g, u = (tg, tu) if g is None else (g + tg, u + tu)
                    pg, pu = xg * dg, xu * du
                    sums_g[k] = pg if sums_g[k] is None else sums_g[k] + pg
                    sums_u[k] = pu if sums_u[k] is None else sums_u[k] + pu
                o_ref[pl.ds(r0, FFN_ROWS), lg] = g.astype(BF16)
                o_ref[pl.ds(r0, FFN_ROWS), lu] = u.astype(BF16)
            for k in range(K):
                dwg_ref[pl.ds(k, 1), lo] += _rsum(sums_g[k])
                dwu_ref[pl.ds(k, 1), lo] += _rsum(sums_u[k])
            return carry
        lax.fori_loop(0, TC // LANES, lane_chunk, 0)

    blocks = [((TM, 2 * TC), BF16), ((TM, 2 * TC), BF16), ((H, 2 * TC), BF16), ((TM, TC), BF16), ((H, TC), BF16),
              ((K, TC), F32), ((K, TC), F32), ((TM, 2 * TC), BF16), ((K, TC), F32), ((K, TC), F32)]
    scratch = [((TM + H, 2 * TC), F32)]
    nxt = lambda j, i: (jnp.minimum((i + 1) * (TM // H), S // H - 1), j)
    tile = pl.BlockSpec((TM, 2 * TC), lambda j, i: (i, j))
    taps_out = pl.BlockSpec((K, TC), lambda j, i: (0, j))
    outs, side_outs = _call(
        body, side, name="ffn_bwd", grid=(nF, nI),
        in_specs=[tile, tile, pl.BlockSpec((H, 2 * TC), nxt),
                  pl.BlockSpec((TM, TC), lambda j, i: (i, j)), pl.BlockSpec((H, TC), nxt),
                  pl.BlockSpec((K, TC), lambda j, i: (0, j)), pl.BlockSpec((K, TC), lambda j, i: (0, j + nF))],
        out_specs=[tile, taps_out, taps_out],
        out_shape=[jax.ShapeDtypeStruct((S, 2 * F), BF16), jax.ShapeDtypeStruct((K, F), F32),
                   jax.ShapeDtypeStruct((K, F), F32)],
        scratch=scratch,
        params=_params(("arbitrary", "arbitrary"), blocks, scratch),
        args=[u0, conv_u0, conv_u0, dact, dact, conv_w, conv_w])
    return outs if side is None else (outs, side_outs)


def _mixer_bwd_ln(dcat, a1, ln_g, ln_b, S, TM, A):
    def body(dc_ref, a1_ref, g_ref, b_ref, da1_ref, acc_ref):
        @pl.when(pl.program_id(0) == 0)
        def _():
            acc_ref[...] = jnp.zeros_like(acc_ref)
        a1 = a1_ref[...]
        g = g_ref[...]
        mu = jnp.mean(a1, axis=-1, keepdims=True)
        d = a1 - mu
        rstd = lax.rsqrt(jnp.mean(d * d, axis=-1, keepdims=True) + EPS)
        nh = d * rstd
        a2 = nh * g + b_ref[...]
        s = _sigmoid(a2)
        da2 = dc_ref[...].astype(F32) * s * (1.0 + a2 * (1.0 - s))
        dnh = da2 * g
        da1 = rstd * (dnh - jnp.mean(dnh, axis=-1, keepdims=True)
                      - nh * jnp.mean(dnh * nh, axis=-1, keepdims=True))
        da1_ref[...] = da1
        acc_ref[pl.ds(0, 1), :] += _rsum(da2 * nh)
        acc_ref[pl.ds(1, 1), :] += _rsum(da2)
        acc_ref[pl.ds(2, 1), :] += _rsum(da1)

    blocks = [((TM, A), BF16), ((TM, A), F32), ((TM, A), F32), ((4, A), F32)]
    return pl.pallas_call(
        body, name="mixer_bwd_ln", grid=(S // TM,),
        in_specs=[pl.BlockSpec((TM, A), lambda i: (i, 0)), pl.BlockSpec((TM, A), lambda i: (i, 0)),
                  pl.BlockSpec((1, A), lambda i: (0, 0)), pl.BlockSpec((1, A), lambda i: (0, 0))],
        out_specs=[pl.BlockSpec((TM, A), lambda i: (i, 0)), pl.BlockSpec((4, A), lambda i: (0, 0))],
        out_shape=[jax.ShapeDtypeStruct((S, A), F32), jax.ShapeDtypeStruct((4, A), F32)],
        compiler_params=_params(("arbitrary",), blocks, temps=[((TM, A), F32)] * 12),
    )(dcat, a1, ln_g, ln_b)


def _mixer_bwd_conv(z, dcat, da1, conv_a_w, conv_b_w, S, TM, A, side=None):
    H = 32
    KA, KB = conv_a_w.shape[0], conv_b_w.shape[0]
    nI = S // TM
    n_chunks = A // LANES
    RB = _pick(TM, (64, 32))

    def body(zc_ref, zp_ref, zn_ref, dbc_ref, dbn_ref, d1c_ref, d1n_ref, wa_ref, wb_ref,
             dz_ref, dwa_ref, dwb_ref, ext_a0, ext_d1, ext_cb, ext_dc, da0_sc, shifted_d, shifted_a):
        i = pl.program_id(0)
        @pl.when(i == 0)
        def _():
            dwa_ref[...] = jnp.zeros_like(dwa_ref)
            dwb_ref[...] = jnp.zeros_like(dwb_ref)
        first = (i > 0).astype(F32)
        last = (i < nI - 1).astype(F32)
        zc = zc_ref[...].astype(F32)
        zp = zp_ref[...].astype(F32) * first
        a_val, a_gate = zc[:, 0:A], zc[:, A:2 * A]
        b_gate, c_gate, b_h = zc[:, 2 * A:3 * A], zc[:, 3 * A:4 * A], zc[:, 4 * A:5 * A]
        sig = _sigmoid(a_gate)
        ext_a0[pl.ds(0, H), :] = zp[:, 0:A] * _sigmoid(zp[:, A:2 * A])
        ext_a0[pl.ds(H, TM), :] = a_val * sig
        ext_d1[pl.ds(0, TM), :] = d1c_ref[...]
        ext_d1[pl.ds(TM, H), :] = d1n_ref[...] * last
        ext_cb[pl.ds(0, H), :] = zp[:, 3 * A:4 * A] * zp[:, 4 * A:5 * A]
        ext_cb[pl.ds(H, TM), :] = c_gate * b_h
        dbx = dbc_ref[...].astype(F32)
        dcbc = dbx * b_gate
        ext_dc[pl.ds(0, TM), :] = dcbc
        ext_dc[pl.ds(TM, H), :] = dbn_ref[...].astype(F32) * zn_ref[...].astype(F32) * last

        def chunk(c, carry):
            cs = pl.ds(pl.multiple_of(c * LANES, LANES), LANES)
            _shift_copies(ext_d1, shifted_d, cs)
            _shift_copies(ext_a0, shifted_a, cs)
            for r0 in range(0, TM, RB):
                acc = None
                for k in range(KA):
                    term = _rows_at(ext_d1, shifted_d, KA - 1 - k + r0, RB, cs) * wa_ref[pl.ds(k, 1), cs]
                    acc = term if acc is None else acc + term
                da0_sc[pl.ds(r0, RB), cs] = acc
            for k in range(KA):
                acc = None
                for r0 in range(0, TM, RB):
                    term = (_rows_at(ext_a0, shifted_a, H - (KA - 1) + k + r0, RB, cs)
                            * ext_d1[pl.ds(r0, RB), cs])
                    acc = term if acc is None else acc + term
                dwa_ref[pl.ds(k, 1), cs] += _rsum(acc)
            return carry
        lax.fori_loop(0, n_chunks, chunk, 0)

        da0 = da0_sc[...]
        dz_ref[:, 0:A] = (da0 * sig).astype(BF16)
        dz_ref[:, A:2 * A] = (da0 * a_val * sig * (1.0 - sig)).astype(BF16)
        cbc = _taps_causal(ext_cb, wb_ref, KB, H, TM, slice(None))
        dz_ref[:, 2 * A:3 * A] = (dbx * cbc).astype(BF16)
        dcb = _taps_anticausal(ext_dc, wb_ref, KB, TM, slice(None))
        dz_ref[:, 3 * A:4 * A] = (dcb * b_h).astype(BF16)
        dz_ref[:, 4 * A:5 * A] = (dcb * c_gate).astype(BF16)
        grads = _tap_grads(ext_cb, dcbc, KB, H, TM, slice(None))
        for k in range(KB):
            dwb_ref[pl.ds(k, 1), :] += grads[k]

    blocks = [((TM, 5 * A), BF16), ((H, 5 * A), BF16), ((H, A), BF16), ((TM, A), BF16), ((H, A), BF16),
              ((TM, A), F32), ((H, A), F32), ((KA, A), F32), ((KB, A), F32),
              ((TM, 5 * A), BF16), ((KA, A), F32), ((KB, A), F32)]
    scratch = ([((H + TM, A), F32)] * 4 + [((TM, A), F32)]
               + [((SUBLANES - 1, H + TM - SUBLANES, LANES), F32)] * 2)
    vec = lambda r: pl.BlockSpec((r, A), lambda i: (0, 0))
    outs, side_outs = _call(
        body, side, name="mixer_bwd_conv", grid=(nI,),
        in_specs=[pl.BlockSpec((TM, 5 * A), lambda i: (i, 0)),
                  pl.BlockSpec((H, 5 * A), _prev_rows(TM, H, 0)),
                  pl.BlockSpec((H, A), _next_rows(S, TM, H, 2)),
                  pl.BlockSpec((TM, A), lambda i: (i, 1)),
                  pl.BlockSpec((H, A), _next_rows(S, TM, H, 1)),
                  pl.BlockSpec((TM, A), lambda i: (i, 0)),
                  pl.BlockSpec((H, A), _next_rows(S, TM, H, 0)),
                  vec(KA), vec(KB)],
        out_specs=[pl.BlockSpec((TM, 5 * A), lambda i: (i, 0)), vec(KA), vec(KB)],
        out_shape=[jax.ShapeDtypeStruct((S, 5 * A), BF16), jax.ShapeDtypeStruct((KA, A), F32),
                   jax.ShapeDtypeStruct((KB, A), F32)],
        scratch=scratch,
        params=_params(("arbitrary",), blocks, scratch, temps=[((TM, 5 * A), F32)] * 2 + [((TM, A), F32)] * 14),
        args=[z, z, z, dcat, dcat, da1, da1, conv_a_w, conv_b_w])
    return outs if side is None else (outs, side_outs)


def _row_tile(R):
    return _pick(R, (256, 128, 64, 32, 16, 8))


def _scalars(*vals):
    return jnp.stack([jnp.asarray(v, jnp.int32) for v in vals])


def _cast_into_gathered(name, w, chip):
    R, C = w.shape
    TR = _row_tile(R)

    def body(s_ref, w_ref, o_ref):
        o_ref[...] = w_ref[...].astype(BF16)

    grid_spec = pltpu.PrefetchScalarGridSpec(
        num_scalar_prefetch=1, grid=(R // TR,),
        in_specs=[pl.BlockSpec((TR, C), lambda r, s: (r, 0))],
        out_specs=pl.BlockSpec((None, TR, C), lambda r, s: (s[0], r, 0)))
    return pl.pallas_call(body, name=name, grid_spec=grid_spec,
                          out_shape=jax.ShapeDtypeStruct((N_CHIPS, R, C), BF16),
                          compiler_params=_params(("arbitrary",), [((TR, C), F32), ((TR, C), BF16)]),
                          )(_scalars(chip), w)


def _add_pair(name, dw, recv, c, chip):
    _, _, Rh, C = dw.shape
    TR = _row_tile(Rh)

    def body(s_ref, a_ref, b_ref, o_ref, ob_ref):
        s = a_ref[...] + b_ref[...]
        ob_ref[...] = s.astype(BF16)

        @pl.when(pl.program_id(1) == s_ref[1])
        def _():
            o_ref[...] = s

    grid_spec = pltpu.PrefetchScalarGridSpec(
        num_scalar_prefetch=1, grid=(Rh // TR, N_CHIPS),
        in_specs=[pl.BlockSpec((None, None, TR, C), lambda r, k, s: (k, s[0], r, 0)),
                  pl.BlockSpec((None, TR, C), lambda r, k, s: (k, r, 0))],
        out_specs=[pl.BlockSpec((TR, C), lambda r, k, s: (r, 0)),
                   pl.BlockSpec((None, TR, C), lambda r, k, s: (k, r, 0))])
    return pl.pallas_call(body, name=name, grid_spec=grid_spec,
                          out_shape=[jax.ShapeDtypeStruct((Rh, C), F32),
                                     jax.ShapeDtypeStruct((N_CHIPS, Rh, C), BF16)],
                          compiler_params=_params(("arbitrary", "arbitrary"), [((TR, C), F32)] * 4),
                          )(_scalars(c, chip), dw, recv)


def _add_chips(name, own, recv, c):
    Rh, C = own.shape
    TR = _row_tile(Rh)

    def body(s_ref, p_ref, r_ref, o_ref):
        o_ref[...] = ((p_ref[...] + r_ref[0].astype(F32)) + r_ref[1].astype(F32)) + r_ref[2].astype(F32)

    grid_spec = pltpu.PrefetchScalarGridSpec(
        num_scalar_prefetch=1, grid=(Rh // TR,),
        in_specs=[pl.BlockSpec((TR, C), lambda r, s: (r, 0)),
                  pl.BlockSpec((N_CHIPS - 1, TR, C), lambda r, s: (0, r, 0))],
        out_specs=pl.BlockSpec((None, TR, C), lambda r, s: (s[0], r, 0)))
    return pl.pallas_call(body, name=name, grid_spec=grid_spec,
                          out_shape=jax.ShapeDtypeStruct((2, Rh, C), F32),
                          compiler_params=_params(("arbitrary",), [((N_CHIPS + 1, TR, C), F32)]),
                          )(_scalars(c), own, recv)


def _sum_devices(name, parts):
    _, R, C = parts.shape

    def body(p_ref, o_ref):
        acc = p_ref[0]
        for d in range(1, N_DEV):
            acc = acc + p_ref[d]
        o_ref[...] = acc

    return pl.pallas_call(body, name=name, out_shape=jax.ShapeDtypeStruct((R, C), F32),
                          in_specs=[pl.BlockSpec(memory_space=pltpu.VMEM)],
                          out_specs=pl.BlockSpec(memory_space=pltpu.VMEM))(parts)


def _adamw(name, w, g, m, v, copy_grad=False):
    R, C = w.shape
    TR = _pick(R, (128, 64, 32, 16, 8))
    c1 = 1.0 - ADAM_B1 ** ADAM_STEP
    c2 = 1.0 - ADAM_B2 ** ADAM_STEP
    n_out = 4 if copy_grad else 3

    def body(w_ref, g_ref, m_ref, v_ref, d_ref, nm_ref, nv_ref, *g_out):
        g_ = g_ref[...]
        nm = ADAM_B1 * m_ref[...] + (1.0 - ADAM_B1) * g_
        nv = ADAM_B2 * v_ref[...] + (1.0 - ADAM_B2) * (g_ * g_)
        d_ref[...] = -ADAM_LR * ((nm / c1) / (jnp.sqrt(nv / c2) + ADAM_EPS) + ADAM_WD * w_ref[...])
        nm_ref[...] = nm
        nv_ref[...] = nv
        for ref in g_out:
            ref[...] = g_

    spec = pl.BlockSpec((TR, C), lambda r: (r, 0))
    shp = jax.ShapeDtypeStruct((R, C), F32)
    return pl.pallas_call(body, name=name, grid=(R // TR,), in_specs=[spec] * 4, out_specs=[spec] * n_out,
                          out_shape=[shp] * n_out,
                          compiler_params=_params(("arbitrary",), [((TR, C), F32)] * (4 + n_out)))(w, g, m, v)


def _place():
    x, y, c = lax.axis_index("x"), lax.axis_index("y"), lax.axis_index("c")
    others = [(1 - x, y), (x, 1 - y), (1 - x, 1 - y)]
    return x, y, c, others


def _allgather_small(name, block):
    R, C = block.shape

    def body(x_ref, out_ref, send_sems, recv_sems, local_sem):
        x, y, c, chips = _place()
        me, sibling = (x, y, c), (x, y, 1 - c)

        def rows(px, py, pc):
            return out_ref.at[4 * px + 2 * py + pc]

        def copy(k, blk, to, src=None):
            return pltpu.make_async_remote_copy(
                src_ref=rows(*blk) if src is None else src, dst_ref=rows(*blk),
                send_sem=send_sems.at[k], recv_sem=recv_sems.at[k], device_id=to, device_id_type=MESH)

        mine = pltpu.make_async_copy(x_ref, rows(*me), local_sem)
        mine.start()
        first = [copy(0, me, sibling, src=x_ref)]
        first += [copy(1 + j, me, (*chip, c), src=x_ref) for j, chip in enumerate(chips)]
        for cp in first:
            cp.start()
        passed = [copy(4 + j, (*chip, c), sibling) for j, chip in enumerate(chips)]
        for j, chip in enumerate(chips):
            copy(1 + j, (*chip, c), me).wait_recv()
            passed[j].start()
        copy(0, sibling, me).wait_recv()
        for j, chip in enumerate(chips):
            copy(4 + j, (*chip, 1 - c), me).wait_recv()
        for cp in first + passed:
            cp.wait_send()
        mine.wait()

    return pl.pallas_call(
        body, name=name, out_shape=jax.ShapeDtypeStruct((N_DEV, R, C), F32),
        in_specs=[pl.BlockSpec(memory_space=pltpu.VMEM)], out_specs=pl.BlockSpec(memory_space=pltpu.VMEM),
        scratch_shapes=[pltpu.SemaphoreType.DMA((7,)), pltpu.SemaphoreType.DMA((7,)), pltpu.SemaphoreType.DMA],
    )(block)


def _gather_side(bufs, across, within):
    def rows(ref, chip, half, piece):
        _, r0, n = piece
        return ref.at[2 * chip[0] + chip[1], pl.ds(half * (ref.shape[1] // 2) + r0, n)]

    def copies(ins, outs, send_sems, recv_sems, base):
        x, y, c, chips = _place()
        sibling = (x, y, 1 - c)
        pairs = []

        def add(k, src, dst, to, arrival):
            mk = lambda s, d, dev: pltpu.make_async_remote_copy(
                src_ref=s, dst_ref=d, send_sem=send_sems.at[base + k], recv_sem=recv_sems.at[base + k],
                device_id=dev, device_id_type=MESH)
            pairs.append((mk(src, dst, to), mk(arrival, arrival, (x, y, c))))

        for p, piece in enumerate(across):
            ref = outs[piece[0]]
            for j, chip in enumerate(chips):
                mine = rows(ref, (x, y), c, piece)
                add(3 * p + j, mine, mine, (*chip, c), rows(ref, chip, c, piece))
        for q, piece in enumerate(within):
            ref = outs[piece[0]]
            for j, chip in enumerate(chips):
                held = rows(ref, chip, c, piece)
                add(3 * (len(across) + q) + j, held, held, sibling, rows(ref, chip, 1 - c, piece))
        return pairs

    def start(*refs):
        for send, _ in copies(*refs):
            send.start()

    def wait(*refs):
        pairs = copies(*refs)
        for _, arrival in pairs:
            arrival.wait_recv()
        for send, _ in pairs:
            send.wait_send()

    return _Side(list(bufs), [jax.ShapeDtypeStruct(b.shape, b.dtype) for b in bufs],
                 3 * (len(across) + len(within)), start, wait, aliases=tuple((i, i) for i in range(len(bufs))))


def _chip_exchange(parts):
    n = len(parts)

    def copies(ins, outs, send_sems, recv_sems, base):
        x, y, c, chips = _place()
        return [pltpu.make_async_remote_copy(
            src_ref=ins[a].at[2 * chip[0] + chip[1]], dst_ref=outs[a].at[j],
            send_sem=send_sems.at[base + 3 * a + j], recv_sem=recv_sems.at[base + 3 * a + j],
            device_id=(*chip, c), device_id_type=MESH) for a in range(n) for j, chip in enumerate(chips)]

    return _Side(list(parts), [jax.ShapeDtypeStruct((N_CHIPS - 1,) + p.shape[1:], p.dtype) for p in parts],
                 3 * n, *_start_wait(copies))


def _pair_exchange(grads):
    def copies(ins, outs, send_sems, recv_sems, base):
        x, y, c, _ = _place()
        return [pltpu.make_async_remote_copy(
            src_ref=ins[a].at[:, 1 - c], dst_ref=outs[a], send_sem=send_sems.at[base + a],
            recv_sem=recv_sems.at[base + a], device_id=(x, y, 1 - c), device_id_type=MESH)
            for a in range(len(grads))]

    return _Side(list(grads), [jax.ShapeDtypeStruct((N_CHIPS,) + g.shape[2:], F32) for g in grads],
                 len(grads), *_start_wait(copies))


def _start_wait(copies):
    def start(*refs):
        for cp in copies(*refs):
            cp.start()

    def wait(*refs):
        cps = copies(*refs)
        for cp in cps:
            cp.wait_recv()
        for cp in cps:
            cp.wait_send()
    return start, wait


def _both(first, second):
    n_in, n_out = len(first.ins), len(first.out_shapes)

    def run(which):
        def go(ins, outs, send_sems, recv_sems, base):
            getattr(first, which)(ins[:n_in], outs[:n_out], send_sems, recv_sems, base)
            getattr(second, which)(ins[n_in:], outs[n_out:], send_sems, recv_sems, base + first.n_sems)
        return go

    aliases = first.aliases + tuple((a + n_in, b + n_out) for a, b in second.aliases)
    return _Side(first.ins + second.ins, first.out_shapes + second.out_shapes,
                 first.n_sems + second.n_sems, run("start"), run("wait"), aliases)


def _share_halves(halves):
    def copies(ins, outs, send_sems, recv_sems, base):
        x, y, c, _ = _place()
        pairs = []
        for a in range(len(halves)):
            mk = lambda s, d, dev, a=a: pltpu.make_async_remote_copy(
                src_ref=s, dst_ref=d, send_sem=send_sems.at[base + a], recv_sem=recv_sems.at[base + a],
                device_id=dev, device_id_type=MESH)
            theirs = outs[a].at[1 - c]
            pairs.append((mk(outs[a].at[c], outs[a].at[c], (x, y, 1 - c)), mk(theirs, theirs, (x, y, c))))
        return pairs

    def start(*refs):
        for send, _ in copies(*refs):
            send.start()

    def wait(*refs):
        pairs = copies(*refs)
        for _, arrival in pairs:
            arrival.wait_recv()
        for send, _ in pairs:
            send.wait_send()

    return _Side(list(halves), [jax.ShapeDtypeStruct(h.shape, F32) for h in halves], len(halves), start, wait,
                 aliases=tuple((i, i) for i in range(len(halves))))


def _pack(arrays):
    pieces = []
    for a in arrays:
        flat = a.reshape(-1).astype(F32)
        pieces.append(jnp.pad(flat, (0, (-flat.size) % PACK_ALIGN)))
    return jnp.concatenate(pieces).reshape(-1, LANES)


def _unpack(buf, shapes):
    lead = buf.shape[:-2]
    flat = buf.reshape(lead + (-1,))
    out, off = [], 0
    for shp in shapes:
        size = 1
        for s in shp:
            size *= s
        out.append(flat[..., off:off + size].reshape(lead + tuple(shp)))
        off += size + (-size) % PACK_ALIGN
    return out


def _gather_channels(buf, shapes):
    per_chip = _unpack(buf[0::2], shapes)
    return [jnp.transpose(a, (1, 0, 2)).reshape(a.shape[1], -1) for a in per_chip]


def _mm_tile(n, rows, limit_bytes=6 * 1024 * 1024):
    for t in (1408, 1280, 1024, 640, 512, 384, 256, 128):
        if n % t == 0 and rows * t * 2 <= limit_bytes:
            return t
    raise ValueError(f"no column tile for {n} x {rows}")


def kernel(x, p, norm_mix_g, w_in, conv_a_w, conv_a_b, ln_a_g, ln_a_b, conv_b_w, w_out, norm_ffn_g, w_up, conv_ffn_w, w_down, w_ple_gate, b_ple_gate, w_ple_proj, norm_final_g, loss_target, m_norm_mix_g, m_w_in, m_conv_a_w, m_conv_a_b, m_ln_a_g, m_ln_a_b, m_conv_b_w, m_w_out, m_norm_ffn_g, m_w_up, m_conv_ffn_w, m_w_down, m_w_ple_gate, m_b_ple_gate, m_w_ple_proj, m_norm_final_g, v_norm_mix_g, v_w_in, v_conv_a_w, v_conv_a_b, v_ln_a_g, v_ln_a_b, v_conv_b_w, v_w_out, v_norm_ffn_g, v_w_up, v_conv_ffn_w, v_w_down, v_w_ple_gate, v_b_ple_gate, v_w_ple_proj, v_norm_final_g):
    S, D = x.shape[1], x.shape[2]
    P = p.shape[3]
    A = conv_a_b.shape[1]
    F = w_down.shape[1] * N_CHIPS
    KA, KB, KF = conv_a_w.shape[1], conv_b_w.shape[1], conv_ffn_w.shape[1]
    xi, yi, ci = lax.axis_index("x"), lax.axis_index("y"), lax.axis_index("c")
    chip = 2 * xi + yi

    TM = _pick(S, (512, 256, 128))
    TL = _pick(S, (1024, 512, 256, 128))
    TE = _pick(S, (256, 128))
    TC = _pick(2 * F // N_CHIPS, (1408, 1024, 512, 256, 128))
    ffn_place = _pair_tile(F // TC)

    x2, p2, t2 = x.reshape(S, D), p.reshape(S, P), loss_target.reshape(S, D)
    gfin = norm_final_g.reshape(1, D)

    big = dict(w_in=w_in[0], w_out=w_out[0], w_up=w_up[0], w_down=w_down[0],
               w_ple_gate=w_ple_gate[0], w_ple_proj=w_ple_proj[0])
    names = list(big)
    buf = {n: _cast_into_gathered("cast_" + n, big[n], chip) for n in names}
    half = {n: big[n].shape[0] // 2 for n in names}
    up_a = half["w_up"] // 2
    (w_in3,) = _comm_only("gather_w_in_across", _gather_side([buf["w_in"]], [(0, 0, half["w_in"])], []))
    (w_in3,) = _comm_only("gather_w_in_within", _gather_side([w_in3], [], [(0, 0, half["w_in"])]))

    tap_shapes = [(KA, A // N_CHIPS), (KB, A // N_CHIPS), (KF, 2 * F // N_CHIPS)]
    taps = _allgather_small("allgather_taps", _pack([conv_a_w[0], conv_b_w[0], conv_ffn_w[0]]))
    conv_a_f, conv_b_f, conv_ffn_f = _gather_channels(taps, tap_shapes)

    def rms_prologue(rows, row_r, vec_r, ro_r, ao_r):
        h = row_r[0][rows, :]
        hn = (h * _rms_stats(h) * vec_r[0][...]).astype(BF16)
        ro_r[0][rows, :] = hn
        return [hn]

    def cast_prologue(rows, row_r, vec_r, ro_r, ao_r):
        hb = row_r[0][rows, :].astype(BF16)
        ro_r[0][rows, :] = hb
        return [hb]

    plain = lambda accs, tile_r, cv_r: [accs[0]]
    residual = lambda accs, tile_r, cv_r: [tile_r[0][...] + accs[0]]

    (z, hn1), (w_out_t, w_up_t) = _rows_mm(
        "in_proj", S, TM, 5 * A, _mm_tile(5 * A // N_CHIPS, D), row_ins=[x2], vec_ins=[norm_mix_g],
        weights=[(w_in3, "nn3")], tile_outs=[BF16], row_outs=[(D, BF16)], prologue=rms_prologue, epilogue=plain,
        side=_gather_side([buf["w_out"], buf["w_up"]], [(0, 0, half["w_out"]), (1, 0, up_a)], []))
    (a1, cat), (w_out3, w_up_t) = _mixer_fwd(
        z, conv_a_f, conv_a_b, ln_a_g, ln_a_b, conv_b_f, S, TE, A,
        side=_gather_side([w_out_t, w_up_t], [(1, up_a, half["w_up"] - up_a)],
                          [(0, 0, half["w_out"]), (1, 0, up_a)]))
    w_out_f = w_out3.reshape(2 * A, D)
    (h1,), (w_up3, w_proj_t) = _rows_mm(
        "out_proj", S, TL, D, _mm_tile(D, 2 * A), row_ins=[cat], weights=[(w_out_f, "nn2")],
        tile_ins=[x2], tile_outs=[F32], epilogue=residual,
        side=_gather_side([w_up_t, buf["w_ple_proj"]], [(1, 0, half["w_ple_proj"])],
                          [(0, up_a, half["w_up"] - up_a)]))
    (u0, hn2), (w_down_t, w_gate_t, w_proj3) = _rows_mm(
        "up_proj", S, TM, 2 * F, TC, row_ins=[h1], vec_ins=[norm_ffn_g],
        weights=[(w_up3, "nn3")], tile_outs=[BF16], row_outs=[(D, BF16)],
        prologue=rms_prologue, epilogue=plain, place=ffn_place,
        side=_gather_side([buf["w_down"], buf["w_ple_gate"], w_proj_t],
                          [(0, 0, half["w_down"]), (1, 0, half["w_ple_gate"])], [(2, 0, half["w_ple_proj"])]))
    (act, conv_u0), (w_down3, w_gate3) = _ffn_act(
        u0, conv_ffn_f, S, TM, F, TC,
        side=_gather_side([w_down_t, w_gate_t], [], [(0, 0, half["w_down"]), (1, 0, half["w_ple_gate"])]))
    w_down_f = w_down3.reshape(F, D)
    w_gate_f = w_gate3.reshape(D, D)
    w_proj_f = jnp.transpose(w_proj3, (1, 0, 2)).reshape(P, D)
    (h2,) = _rows_mm("down_proj", S, TL, D, _mm_tile(D, F), row_ins=[act], weights=[(w_down_f, "nn2")],
                     tile_ins=[h1], tile_outs=[F32], epilogue=residual)

    loss_part, g_norm_final, g_b_gate, dh3, dpre, dpp, h2b, pb = _ple_loss(
        h2, p2, t2, w_gate_f, w_proj_f, b_ple_gate, gfin, S, TE)

    TK = _pick(S, (1024, 512, 256, 128))
    wt = lambda n: _pick(n, (1408, 1280, 1024, 512, 256, 128))
    chip_sums, from_chips = {}, {}

    def to_sibling(parts):
        ns = list(parts)
        halves = [parts[n].reshape(N_CHIPS, 2, big[n].shape[0] // 2, big[n].shape[1]) for n in ns]
        return ns, halves, _pair_exchange(halves)

    def to_chips(ns, halves, from_sibling):
        sums = [_add_pair("pair_sum_" + n, h, r, ci, chip) for n, h, r in zip(ns, halves, from_sibling)]
        for n, (s, _) in zip(ns, sums):
            chip_sums[n] = s
        return ns, _chip_exchange([b for _, b in sums])

    def landed(ns, side_outs):
        for n, r in zip(ns, side_outs):
            from_chips[n] = r

    ns, halves, side = to_sibling(dict(
        w_ple_gate=_tn_mm("dw_ple_gate", h2b, dpre, wt(D), wt(D), TK),
        w_ple_proj=_tn_mm("dw_ple_proj", pb, dpp, wt(P), wt(D // N_CHIPS), TK, cols_per_chip=D // N_CHIPS)))
    (dh2,), got = _rows_mm("ple_bwd", S, TL, D, _mm_tile(D, D), row_ins=[dpre], weights=[(w_gate_f, "nt2")],
                           tile_ins=[dh3], tile_outs=[F32], epilogue=residual, side=side)
    ple_ns, ple_chips = to_chips(ns, halves, got)
    (dact, dh2b), got = _rows_mm("down_bwd", S, TL, F, _mm_tile(F, D), row_ins=[dh2], weights=[(w_down_f, "nt2")],
                                 tile_outs=[BF16], row_outs=[(D, BF16)], prologue=cast_prologue, epilogue=plain,
                                 side=ple_chips)
    landed(ple_ns, got)
    ns, halves, side = to_sibling(dict(w_down=_tn_mm("dw_down", act, dh2b, wt(F), wt(D), TK)))
    (du0, g_conv_gate, g_conv_up), got = _ffn_bwd(u0, conv_u0, dact, conv_ffn_f, S, TM, F, TC, side=side)
    down_ns, down_chips = to_chips(ns, halves, got)
    g_conv_ffn = jnp.concatenate([g_conv_gate, g_conv_up], axis=1)
    ns, halves, side = to_sibling(dict(
        w_up=_tn_mm("dw_up", hn2, du0, wt(D), TC, TK, cols_per_chip=2 * F // N_CHIPS, place=ffn_place)))

    def up_bwd_epilogue(acc, rows, row_r, vec_r, ro_r, ao_r):
        dh, dg = _rms_bwd(row_r[0][rows, :], vec_r[0][...], acc)
        dh1_ = row_r[1][rows, :] + dh
        ro_r[0][rows, :] = dh1_
        ro_r[1][rows, :] = dh1_.astype(BF16)
        ao_r[0][...] += dg

    (dh1, dh1b, g_norm_ffn), got = _kloop_mm(
        "up_bwd", S, TM, du0, w_up3, TC, row_ins=[h1, dh2], vec_ins=[norm_ffn_g],
        row_outs=[(D, F32), (D, BF16)], acc_outs=[(1, D)], epilogue=up_bwd_epilogue, place=ffn_place,
        side=_both(down_chips, side))
    landed(down_ns, got[:len(down_ns)])
    up_ns, up_chips = to_chips(ns, halves, got[len(down_ns):])
    ns, halves, side = to_sibling(dict(w_out=_tn_mm("dw_out", cat, dh1b, wt(2 * A), wt(D), TK)))
    (dcat,), got = _rows_mm("out_bwd", S, TL, 2 * A, _mm_tile(2 * A, D), row_ins=[dh1b],
                            weights=[(w_out_f, "nt2")], tile_outs=[BF16], epilogue=plain, side=side)
    out_ns, out_chips = to_chips(ns, halves, got)
    da1, ln_sums = _mixer_bwd_ln(dcat, a1, ln_a_g, ln_a_b, S, TE, A)
    (dz, g_conv_a, g_conv_b), got = _mixer_bwd_conv(z, dcat, da1, conv_a_f, conv_b_f, S, TE, A,
                                                    side=_both(up_chips, out_chips))
    landed(up_ns + out_ns, got)
    ns, halves, side = to_sibling(dict(
        w_in=_tn_mm("dw_in", hn1, dz, wt(D), wt(5 * A // N_CHIPS), TK, cols_per_chip=5 * A // N_CHIPS)))
    ns, side = to_chips(ns, halves, _comm_only("grads_exchange_pairs_in", side))

    def in_bwd_epilogue(acc, rows, row_r, vec_r, ro_r, ao_r):
        dh, dg = _rms_bwd(row_r[0][rows, :], vec_r[0][...], acc)
        ro_r[0][rows, :] = row_r[1][rows, :] + dh
        ao_r[0][...] += dg

    early = [n for n in names if n != "w_in"]
    early_halves = [_add_chips("chip_sum_" + n, chip_sums[n], from_chips[n], ci) for n in early]
    (dx, g_norm_mix), got = _kloop_mm(
        "in_bwd", S, TM, dz, w_in3, _mm_tile(5 * A // N_CHIPS, D), row_ins=[x2, dh1],
        vec_ins=[norm_mix_g], row_outs=[(D, F32)], acc_outs=[(1, D)], epilogue=in_bwd_epilogue,
        side=_both(side, _share_halves(early_halves)))
    landed(ns, got[:1])
    shared = dict(zip(early, got[1:]))
    (shared["w_in"],) = _comm_only("grads_share_w_in", _share_halves(
        [_add_chips("chip_sum_w_in", chip_sums["w_in"], from_chips["w_in"], ci)]))

    reduced = [shared[n] for n in names]
    moments = dict(w_in=(m_w_in, v_w_in), w_out=(m_w_out, v_w_out), w_up=(m_w_up, v_w_up),
                   w_down=(m_w_down, v_w_down), w_ple_gate=(m_w_ple_gate, v_w_ple_gate),
                   w_ple_proj=(m_w_ple_proj, v_w_ple_proj))
    grads, deltas, new_m, new_v = {}, {}, {}, {}
    for n, g in zip(names, reduced):
        d_, m_, v_, g = _adamw("adamw_" + n, big[n], g.reshape(big[n].shape), moments[n][0][0], moments[n][1][0],
                               copy_grad=True)
        grads[n], deltas[n], new_m[n], new_v[n] = g[None], d_[None], m_[None], v_[None]

    small = ["norm_mix_g", "conv_a_w", "conv_a_b", "ln_a_g", "ln_a_b", "conv_b_w", "norm_ffn_g",
             "conv_ffn_w", "b_ple_gate", "norm_final_g"]
    small_part = [g_norm_mix, g_conv_a, ln_sums[2:3], ln_sums[0:1], ln_sums[1:2], g_conv_b, g_norm_ffn,
                  g_conv_ffn, g_b_gate, g_norm_final]
    full_shapes = [a.shape for a in small_part]
    summed = _sum_devices("small_grads_sum", _allgather_small("allgather_small_grads", _pack(small_part)))
    small_g = dict(zip(small, _unpack(summed, full_shapes)))
    for n, width in (("conv_a_w", A), ("conv_b_w", A), ("conv_ffn_w", 2 * F)):
        small_g[n] = lax.dynamic_slice_in_dim(small_g[n], chip * (width // N_CHIPS), width // N_CHIPS, axis=1)
    small_w = dict(norm_mix_g=(norm_mix_g, m_norm_mix_g, v_norm_mix_g), conv_a_w=(conv_a_w, m_conv_a_w, v_conv_a_w),
                   conv_a_b=(conv_a_b, m_conv_a_b, v_conv_a_b), ln_a_g=(ln_a_g, m_ln_a_g, v_ln_a_g),
                   ln_a_b=(ln_a_b, m_ln_a_b, v_ln_a_b), conv_b_w=(conv_b_w, m_conv_b_w, v_conv_b_w),
                   norm_ffn_g=(norm_ffn_g, m_norm_ffn_g, v_norm_ffn_g),
                   conv_ffn_w=(conv_ffn_w, m_conv_ffn_w, v_conv_ffn_w),
                   b_ple_gate=(b_ple_gate, m_b_ple_gate, v_b_ple_gate),
                   norm_final_g=(norm_final_g, m_norm_final_g, v_norm_final_g))
    out_shapes = [small_w[n][0].shape for n in small]
    packed_g = _pack([small_g[n] for n in small])
    packed = [_pack([small_w[n][k] for n in small]) for k in range(3)]
    d_s, m_s, v_s = _adamw("adamw_small", packed[0], packed_g, packed[1], packed[2])
    for n, g, d_, m_, v_ in zip(small, _unpack(packed_g, out_shapes), _unpack(d_s, out_shapes),
                                _unpack(m_s, out_shapes), _unpack(v_s, out_shapes)):
        grads[n], deltas[n], new_m[n], new_v[n] = g, d_, m_, v_

    order = ["norm_mix_g", "w_in", "conv_a_w", "conv_a_b", "ln_a_g", "ln_a_b", "conv_b_w", "w_out", "norm_ffn_g",
             "w_up", "conv_ffn_w", "w_down", "w_ple_gate", "b_ple_gate", "w_ple_proj", "norm_final_g"]
    loss = lax.psum(loss_part[0, 0], ("x", "y", "c"))
    return (loss, dx.reshape(x.shape), *[grads[n] for n in order], *[deltas[n] for n in order],
            *[new_m[n] for n in order], *[new_v[n] for n in order])
```

```python
from typing import Callable, NamedTuple

import jax
import jax.numpy as jnp
from jax import lax
from jax.experimental import pallas as pl
from jax.experimental.pallas import tpu as pltpu

F32 = jnp.float32
BF16 = jnp.bfloat16
MESH = pl.DeviceIdType.MESH
ANY = pl.BlockSpec(memory_space=pl.ANY)

EPS = 1e-6
ADAM_LR = 0.001
ADAM_B1 = 0.9
ADAM_B2 = 0.999
ADAM_EPS = 1e-08
ADAM_WD = 0.01
ADAM_STEP = 10

N_CHIPS = 4
N_DEV = 8
LANES = 128
SUBLANES = 8
PACK_ALIGN = LANES * SUBLANES
ROW_CHUNK = 32
VMEM_CAP = 60 * 1024 * 1024
VMEM_SLACK = 6 * 1024 * 1024


def _pick(n, cands):
    for c in cands:
        if n % c == 0:
            return c
    raise ValueError(f"no tile of {cands} divides {n}")


def _nbytes(shape, dtype):
    n = 1
    for s in shape:
        if s is not None:
            n *= s
    return n * jnp.dtype(dtype).itemsize


def _params(sem, blocks, scratch=(), temps=()):
    est = (2 * sum(_nbytes(s, d) for s, d in blocks) + sum(_nbytes(s, d) for s, d in scratch)
           + sum(_nbytes(s, d) for s, d in temps))
    return pltpu.CompilerParams(dimension_semantics=sem,
                                vmem_limit_bytes=min(est + VMEM_SLACK, VMEM_CAP))


def _sigmoid(x):
    return 1.0 / (1.0 + jnp.exp(-x))


def _rsum(x):
    return jnp.sum(x, axis=0, keepdims=True)


class _Side(NamedTuple):
    ins: list
    out_shapes: list
    n_sems: int
    start: Callable
    wait: Callable
    aliases: tuple = ()


def _call(body, side, *, name, grid, in_specs, out_specs, out_shape, scratch, params, args):
    vmem = [pltpu.VMEM(s, d) for s, d in scratch]
    if side is None:
        outs = pl.pallas_call(body, name=name, grid=grid, in_specs=in_specs, out_specs=out_specs,
                              out_shape=out_shape, scratch_shapes=vmem, compiler_params=params)(*args)
        return list(outs), []
    n_in, n_out, n_sc = len(in_specs), len(out_specs), len(scratch)
    ns_in, ns_out = len(side.ins), len(side.out_shapes)

    def carrier(*refs):
        pos = [0]
        def take(n):
            pos[0] += n
            return refs[pos[0] - n:pos[0]]
        ins, s_ins, outs, s_outs, scr = take(n_in), take(ns_in), take(n_out), take(ns_out), take(n_sc)
        send_sems, recv_sems = take(2)
        first = last = None
        for axis, extent in enumerate(grid):
            at_start, at_end = pl.program_id(axis) == 0, pl.program_id(axis) == extent - 1
            first = at_start if first is None else first & at_start
            last = at_end if last is None else last & at_end

        @pl.when(first)
        def _():
            side.start(s_ins, s_outs, send_sems, recv_sems, 0)
        body(*ins, *outs, *scr)

        @pl.when(last)
        def _():
            side.wait(s_ins, s_outs, send_sems, recv_sems, 0)

    outs = pl.pallas_call(
        carrier, name=name, grid=grid, in_specs=list(in_specs) + [ANY] * ns_in,
        out_specs=list(out_specs) + [ANY] * ns_out, out_shape=list(out_shape) + list(side.out_shapes),
        scratch_shapes=vmem + [pltpu.SemaphoreType.DMA((side.n_sems,)), pltpu.SemaphoreType.DMA((side.n_sems,))],
        input_output_aliases={n_in + a: n_out + b for a, b in side.aliases},
        compiler_params=params)(*args, *side.ins)
    return list(outs[:n_out]), list(outs[n_out:])


def _comm_only(name, side):
    n_in = len(side.ins)

    def body(*refs):
        ins, outs = refs[:n_in], refs[n_in:n_in + len(side.out_shapes)]
        send_sems, recv_sems = refs[n_in + len(side.out_shapes):]
        side.start(ins, outs, send_sems, recv_sems, 0)
        side.wait(ins, outs, send_sems, recv_sems, 0)

    return pl.pallas_call(
        body, name=name, out_shape=list(side.out_shapes), in_specs=[ANY] * n_in,
        out_specs=[ANY] * len(side.out_shapes),
        scratch_shapes=[pltpu.SemaphoreType.DMA((side.n_sems,)), pltpu.SemaphoreType.DMA((side.n_sems,))],
        input_output_aliases=dict(side.aliases),
    )(*side.ins)


def _rms_stats(x):
    return lax.rsqrt(jnp.mean(x * x, axis=-1, keepdims=True) + EPS)


def _rms_bwd(h, g, dout):
    r = _rms_stats(h)
    n = h * r
    dn = dout * g
    dh = r * (dn - n * jnp.mean(dn * n, axis=-1, keepdims=True))
    return dh, _rsum(dout * n)


def _identity(t):
    return t


def _chip_major(nb, place=_identity):
    return lambda i, j: (place(j) // nb, 0, place(j) % nb)


def _rows_mm(name, S, TM, N, TN, *, row_ins, vec_ins=(), colvec_ins=(), weights, tile_ins=(),
             tile_outs, row_outs=(), acc_outs=(), prologue=None, epilogue, place=_identity, side=None):
    nI, nJ = S // TM, N // TN
    n_row, n_vec, n_cv, n_w, n_tile = len(row_ins), len(vec_ins), len(colvec_ins), len(weights), len(tile_ins)
    n_to, n_ro, n_ao = len(tile_outs), len(row_outs), len(acc_outs)

    in_specs, blocks, scratch, ks = [], [], [], []
    for a in row_ins:
        in_specs.append(pl.BlockSpec((TM, a.shape[1]), lambda i, j: (i, 0)))
        blocks.append(((TM, a.shape[1]), a.dtype))
    for a in vec_ins:
        in_specs.append(pl.BlockSpec(a.shape, lambda i, j: (0, 0)))
        blocks.append((a.shape, a.dtype))
    for a in colvec_ins:
        in_specs.append(pl.BlockSpec((1, TN), lambda i, j: (0, j)))
        blocks.append(((1, TN), a.dtype))
    for w, mode in weights:
        if mode == "nn2":
            k = w.shape[0]
            in_specs.append(pl.BlockSpec((k, TN), lambda i, j: (0, j)))
        elif mode == "nn3":
            k = w.shape[1]
            in_specs.append(pl.BlockSpec((None, k, TN), _chip_major(w.shape[2] // TN, place)))
        else:
            k = w.shape[1]
            in_specs.append(pl.BlockSpec((TN, k), lambda i, j: (j, 0)))
        ks.append(k)
        blocks.append(((k, TN), BF16))
        if prologue is not None:
            scratch.append(((TM, k), BF16))
    for a in tile_ins:
        in_specs.append(pl.BlockSpec((TM, TN), lambda i, j: (i, j)))
        blocks.append(((TM, TN), a.dtype))

    out_shape, out_specs = [], []
    for dt in tile_outs:
        out_shape.append(jax.ShapeDtypeStruct((S, N), dt))
        out_specs.append(pl.BlockSpec((TM, TN), lambda i, j: (i, j)))
        blocks.append(((TM, TN), dt))
    for width, dt in row_outs:
        out_shape.append(jax.ShapeDtypeStruct((S, width), dt))
        out_specs.append(pl.BlockSpec((TM, width), lambda i, j: (i, 0)))
        blocks.append(((TM, width), dt))
    for rows, width in acc_outs:
        out_shape.append(jax.ShapeDtypeStruct((rows, width), F32))
        out_specs.append(pl.BlockSpec((rows, width), lambda i, j: (0, 0)))
        blocks.append(((rows, width), F32))

    modes = [m for _, m in weights]

    def body(*refs):
        pos = 0
        def take(n):
            nonlocal pos
            out = refs[pos:pos + n]
            pos += n
            return out
        row_r, vec_r, cv_r, w_r, tile_r = take(n_row), take(n_vec), take(n_cv), take(n_w), take(n_tile)
        to_r, ro_r, ao_r, a_sc = take(n_to), take(n_ro), take(n_ao), take(len(scratch))
        i, j = pl.program_id(0), pl.program_id(1)

        if prologue is None:
            a_sc = row_r[:n_w]
        else:
            @pl.when(j == 0)
            def _():
                if n_ao:
                    @pl.when(i == 0)
                    def _():
                        for r in ao_r:
                            r[...] = jnp.zeros_like(r)

                def chunk(ci, carry):
                    rows = pl.ds(pl.multiple_of(ci * ROW_CHUNK, ROW_CHUNK), ROW_CHUNK)
                    for sc, a in zip(a_sc, prologue(rows, row_r, vec_r, ro_r, ao_r)):
                        sc[rows, :] = a
                    return carry
                lax.fori_loop(0, TM // ROW_CHUNK, chunk, 0)

        accs = []
        for w_ref, sc, mode in zip(w_r, a_sc, modes):
            if mode == "nt2":
                accs.append(lax.dot_general(sc[...], w_ref[...], (((1,), (1,)), ((), ())),
                                            preferred_element_type=F32))
            else:
                accs.append(jnp.dot(sc[...], w_ref[...], preferred_element_type=F32))
        outs = epilogue(accs, tile_r, cv_r)
        for r, o in zip(to_r, outs):
            r[...] = o.astype(r.dtype)

    outs, side_outs = _call(
        body, side, name=name, grid=(nI, nJ), in_specs=in_specs, out_specs=out_specs, out_shape=out_shape,
        scratch=scratch, params=_params(("arbitrary", "arbitrary"), blocks, scratch, temps=[((TM, TN), F32)] * 3),
        args=[*row_ins, *vec_ins, *colvec_ins, *[w for w, _ in weights], *tile_ins])
    return outs if side is None else (outs, side_outs)


def _kloop_mm(name, S, TM, a, w3, TK, *, row_ins, vec_ins, row_outs, acc_outs, epilogue, place=_identity,
              side=None):
    _, N, Ks = w3.shape
    nb = Ks // TK
    nK = N_CHIPS * nb
    n_row, n_vec, n_ro, n_ao = len(row_ins), len(vec_ins), len(row_outs), len(acc_outs)

    in_specs = [pl.BlockSpec((TM, TK), lambda i, k: (i, k)),
                pl.BlockSpec((None, N, TK), _chip_major(nb, place))]
    blocks = [((TM, TK), BF16), ((N, TK), BF16)]
    for r in row_ins:
        in_specs.append(pl.BlockSpec((TM, r.shape[1]), lambda i, k: (i, 0)))
        blocks.append(((TM, r.shape[1]), r.dtype))
    for v in vec_ins:
        in_specs.append(pl.BlockSpec(v.shape, lambda i, k: (0, 0)))
        blocks.append((v.shape, v.dtype))
    out_shape, out_specs = [], []
    for width, dt in row_outs:
        out_shape.append(jax.ShapeDtypeStruct((S, width), dt))
        out_specs.append(pl.BlockSpec((TM, width), lambda i, k: (i, 0)))
        blocks.append(((TM, width), dt))
    for rows, width in acc_outs:
        out_shape.append(jax.ShapeDtypeStruct((rows, width), F32))
        out_specs.append(pl.BlockSpec((rows, width), lambda i, k: (0, 0)))
        blocks.append(((rows, width), F32))
    scratch = [((TM, N), F32)]

    def body(*refs):
        a_ref, w_ref = refs[0], refs[1]
        row_r = refs[2:2 + n_row]
        vec_r = refs[2 + n_row:2 + n_row + n_vec]
        pos = 2 + n_row + n_vec
        ro_r = refs[pos:pos + n_ro]
        ao_r = refs[pos + n_ro:pos + n_ro + n_ao]
        acc_sc = refs[pos + n_ro + n_ao]
        i, k = pl.program_id(0), pl.program_id(1)
        @pl.when(k == 0)
        def _():
            acc_sc[...] = jnp.zeros_like(acc_sc)
        acc_sc[...] += lax.dot_general(a_ref[...], w_ref[...], (((1,), (1,)), ((), ())),
                                       preferred_element_type=F32)

        @pl.when(k == nK - 1)
        def _():
            @pl.when(i == 0)
            def _():
                for r in ao_r:
                    r[...] = jnp.zeros_like(r)

            def chunk(ci, carry):
                rows = pl.ds(pl.multiple_of(ci * ROW_CHUNK, ROW_CHUNK), ROW_CHUNK)
                epilogue(acc_sc[rows, :], rows, row_r, vec_r, ro_r, ao_r)
                return carry
            lax.fori_loop(0, TM // ROW_CHUNK, chunk, 0)

    outs, side_outs = _call(
        body, side, name=name, grid=(S // TM, nK), in_specs=in_specs, out_specs=out_specs, out_shape=out_shape,
        scratch=scratch, params=_params(("arbitrary", "arbitrary"), blocks, scratch, temps=[((TM, N), F32)]),
        args=[a, w3, *row_ins, *vec_ins])
    return outs if side is None else (outs, side_outs)


def _tn_mm(name, a, b, TMw, TNw, TK, cols_per_chip=None, place=_identity):
    S, M = a.shape
    N = b.shape[1]
    nK = S // TK
    if cols_per_chip is None:
        out_shape = jax.ShapeDtypeStruct((M, N), F32)
        out_spec = pl.BlockSpec((TMw, TNw), lambda i, j, k: (i, j))
    else:
        nb = cols_per_chip // TNw
        out_shape = jax.ShapeDtypeStruct((N_CHIPS, M, cols_per_chip), F32)
        out_spec = pl.BlockSpec((None, TMw, TNw), lambda i, j, k: (place(j) // nb, i, place(j) % nb))

    def body(a_ref, b_ref, o_ref):
        @pl.when(pl.program_id(2) == 0)
        def _():
            o_ref[...] = jnp.zeros_like(o_ref)
        o_ref[...] += lax.dot_general(a_ref[...], b_ref[...], (((0,), (0,)), ((), ())),
                                      preferred_element_type=F32)

    blocks = [((TK, TMw), BF16), ((TK, TNw), BF16), ((TMw, TNw), F32)]
    return pl.pallas_call(
        body, name=name, grid=(M // TMw, N // TNw, nK),
        in_specs=[pl.BlockSpec((TK, TMw), lambda i, j, k: (k, i)),
                  pl.BlockSpec((TK, TNw), lambda i, j, k: (k, j))],
        out_specs=out_spec, out_shape=out_shape,
        compiler_params=_params(("arbitrary", "arbitrary", "arbitrary"), blocks,
                                temps=[((TMw, TNw), F32), ((TK, TMw), BF16)]),
    )(a, b)


def _prev_rows(TM, H, col):
    return lambda i: (jnp.maximum(i * (TM // H) - 1, 0), col)


def _next_rows(S, TM, H, col):
    return lambda i: (jnp.minimum((i + 1) * (TM // H), S // H - 1), col)


def _taps_causal(ext_ref, w_ref, K, H, TM, cs):
    acc = None
    for k in range(K):
        term = ext_ref[pl.ds(H - (K - 1) + k, TM), cs] * w_ref[pl.ds(k, 1), cs]
        acc = term if acc is None else acc + term
    return acc


def _taps_anticausal(ext_ref, w_ref, K, TM, cs):
    acc = None
    for k in range(K):
        term = ext_ref[pl.ds(K - 1 - k, TM), cs] * w_ref[pl.ds(k, 1), cs]
        acc = term if acc is None else acc + term
    return acc


def _tap_grads(ext_ref, g, K, H, TM, cs):
    return [_rsum(ext_ref[pl.ds(H - (K - 1) + k, TM), cs] * g) for k in range(K)]


def _shift_copies(ext_ref, shifted, cs):
    n = shifted.shape[1]
    for r in range(1, SUBLANES):
        shifted[r - 1] = ext_ref[pl.ds(r, n), cs]


def _rows_at(ext_ref, shifted, start, n, cs):
    q, r = divmod(start, SUBLANES)
    if r == 0:
        return ext_ref[pl.ds(start, n), cs]
    return shifted[r - 1, pl.ds(SUBLANES * q, n), :]


def _mixer_fwd(z, conv_a_w, conv_a_b, ln_g, ln_b, conv_b_w, S, TM, A, side=None):
    H = 32
    KA, KB = conv_a_w.shape[0], conv_b_w.shape[0]
    n_chunks = A // LANES
    RB = _pick(TM, (64, 32))

    def body(zc_ref, zh_ref, wa_ref, ba_ref, g_ref, b_ref, wb_ref, a1_ref, cat_ref, ext_a, ext_b, shifted):
        i = pl.program_id(0)
        live = (i > 0).astype(F32)
        zc = zc_ref[...].astype(F32)
        zh = zh_ref[...].astype(F32) * live
        ext_a[pl.ds(0, H), :] = zh[:, 0:A] * _sigmoid(zh[:, A:2 * A])
        ext_a[pl.ds(H, TM), :] = zc[:, 0:A] * _sigmoid(zc[:, A:2 * A])
        ext_b[pl.ds(0, H), :] = zh[:, 3 * A:4 * A] * zh[:, 4 * A:5 * A]
        ext_b[pl.ds(H, TM), :] = zc[:, 3 * A:4 * A] * zc[:, 4 * A:5 * A]

        def chunk(c, carry):
            cs = pl.ds(pl.multiple_of(c * LANES, LANES), LANES)
            _shift_copies(ext_a, shifted, cs)
            for r0 in range(0, TM, RB):
                acc = None
                for k in range(KA):
                    term = _rows_at(ext_a, shifted, H - (KA - 1) + k + r0, RB, cs) * wa_ref[pl.ds(k, 1), cs]
                    acc = term if acc is None else acc + term
                a1_ref[pl.ds(r0, RB), cs] = acc + ba_ref[:, cs]
            return carry
        lax.fori_loop(0, n_chunks, chunk, 0)

        a1 = a1_ref[...]
        mu = jnp.mean(a1, axis=-1, keepdims=True)
        d = a1 - mu
        var = jnp.mean(d * d, axis=-1, keepdims=True)
        a2 = d * lax.rsqrt(var + EPS) * g_ref[...] + b_ref[...]
        cat_ref[:, 0:A] = (a2 * _sigmoid(a2)).astype(BF16)
        cbc = _taps_causal(ext_b, wb_ref, KB, H, TM, slice(None))
        cat_ref[:, A:2 * A] = (zc[:, 2 * A:3 * A] * cbc).astype(BF16)

    blocks = [((TM, 5 * A), BF16), ((H, 5 * A), BF16), ((KA, A), F32), ((KB, A), F32),
              ((TM, A), F32), ((TM, 2 * A), BF16)]
    scratch = [((H + TM, A), F32), ((H + TM, A), F32), ((SUBLANES - 1, H + TM - SUBLANES, LANES), F32)]
    vec = lambda r: pl.BlockSpec((r, A), lambda i: (0, 0))
    outs, side_outs = _call(
        body, side, name="mixer_fwd", grid=(S // TM,),
        in_specs=[pl.BlockSpec((TM, 5 * A), lambda i: (i, 0)),
                  pl.BlockSpec((H, 5 * A), _prev_rows(TM, H, 0)),
                  vec(KA), vec(1), vec(1), vec(1), vec(KB)],
        out_specs=[pl.BlockSpec((TM, A), lambda i: (i, 0)), pl.BlockSpec((TM, 2 * A), lambda i: (i, 0))],
        out_shape=[jax.ShapeDtypeStruct((S, A), F32), jax.ShapeDtypeStruct((S, 2 * A), BF16)],
        scratch=scratch,
        params=_params(("arbitrary",), blocks, scratch, temps=[((TM, 5 * A), F32)] * 2 + [((TM, A), F32)] * 10),
        args=[z, z, conv_a_w, conv_a_b, ln_g, ln_b, conv_b_w])
    return outs if side is None else (outs, side_outs)


def _pair_tile(nF):
    return lambda t: (t % 2) * nF + t // 2


FFN_ROWS = 32


def _bcast_taps(w_ref, K, lanes):
    return [jnp.broadcast_to(w_ref[pl.ds(k, 1), lanes], (FFN_ROWS, LANES)) for k in range(K)]


def _ffn_act(u0, conv_w, S, TM, F, TC, side=None):
    H = 16
    K = conv_w.shape[0]
    nF = F // TC

    def body(uc_ref, uh_ref, wg_ref, wu_ref, o_ref, conv_ref, ext):
        live = (pl.program_id(0) > 0).astype(F32)
        ext[pl.ds(0, H), :] = uh_ref[...].astype(F32) * live
        ext[pl.ds(H, TM), :] = uc_ref[...].astype(F32)

        def lane_chunk(c, carry):
            lo = pl.ds(pl.multiple_of(c * LANES, LANES), LANES)
            lg, lu = lo, pl.ds(pl.multiple_of(TC + c * LANES, LANES), LANES)
            wg, wu = _bcast_taps(wg_ref, K, lo), _bcast_taps(wu_ref, K, lo)
            for r0 in range(0, TM, FFN_ROWS):
                g = u = None
                for k in range(K):
                    rows = pl.ds(H - (K - 1) + k + r0, FFN_ROWS)
                    tg, tu = ext[rows, lg] * wg[k], ext[rows, lu] * wu[k]
                    g, u = (tg, tu) if g is None else (g + tg, u + tu)
                o_ref[pl.ds(r0, FFN_ROWS), lo] = (g * _sigmoid(g) * u).astype(BF16)
                conv_ref[pl.ds(r0, FFN_ROWS), lg] = g.astype(BF16)
                conv_ref[pl.ds(r0, FFN_ROWS), lu] = u.astype(BF16)
            return carry
        lax.fori_loop(0, TC // LANES, lane_chunk, 0)

    blocks = [((TM, 2 * TC), BF16), ((H, 2 * TC), BF16), ((K, TC), F32), ((K, TC), F32), ((TM, TC), BF16),
              ((TM, 2 * TC), BF16)]
    scratch = [((H + TM, 2 * TC), F32)]
    outs, side_outs = _call(
        body, side, name="ffn_act", grid=(S // TM, nF),
        in_specs=[pl.BlockSpec((TM, 2 * TC), lambda i, j: (i, j)),
                  pl.BlockSpec((H, 2 * TC), lambda i, j: (jnp.maximum(i * (TM // H) - 1, 0), j)),
                  pl.BlockSpec((K, TC), lambda i, j: (0, j)),
                  pl.BlockSpec((K, TC), lambda i, j: (0, j + nF))],
        out_specs=[pl.BlockSpec((TM, TC), lambda i, j: (i, j)), pl.BlockSpec((TM, 2 * TC), lambda i, j: (i, j))],
        out_shape=[jax.ShapeDtypeStruct((S, F), BF16), jax.ShapeDtypeStruct((S, 2 * F), BF16)],
        scratch=scratch,
        params=_params(("arbitrary", "arbitrary"), blocks, scratch, temps=[((TM, 2 * TC), F32)]),
        args=[u0, u0, conv_w, conv_w])
    return outs if side is None else (outs, side_outs)


def _ple_loss(h2, p, target, w_gate, w_proj, b_gate, g_final, S, TM):
    D, P = h2.shape[1], p.shape[1]

    def body(h_ref, p_ref, t_ref, wg_ref, wp_ref, b_ref, g_ref,
             loss_ref, dg_ref, db_ref, dh_ref, dpre_ref, dpp_ref, hb_ref, pb_ref, pre_sc, pp_sc):
        @pl.when(pl.program_id(0) == 0)
        def _():
            loss_ref[...] = jnp.zeros_like(loss_ref)
            dg_ref[...] = jnp.zeros_like(dg_ref)
            db_ref[...] = jnp.zeros_like(db_ref)
        hb_ref[...] = h_ref[...].astype(BF16)
        pb_ref[...] = p_ref[...].astype(BF16)
        pre_sc[...] = jnp.dot(hb_ref[...], wg_ref[...], preferred_element_type=F32)
        pp_sc[...] = jnp.dot(pb_ref[...], wp_ref[...], preferred_element_type=F32)

        def chunk(ci, carry):
            rows = pl.ds(pl.multiple_of(ci * ROW_CHUNK, ROW_CHUNK), ROW_CHUNK)
            g = g_ref[...]
            gate = _sigmoid(pre_sc[rows, :] + b_ref[...])
            pp = pp_sc[rows, :]
            h = h_ref[rows, :] + pp * gate
            r = _rms_stats(h)
            n = h * r
            diff = n * g - t_ref[rows, :]
            loss_ref[...] += 0.5 * jnp.sum(jnp.mean(diff * diff, axis=-1, keepdims=True), axis=0, keepdims=True)
            dy = diff * (1.0 / D)
            dn = dy * g
            dh = r * (dn - n * jnp.mean(dn * n, axis=-1, keepdims=True))
            dh_ref[rows, :] = dh
            dg_ref[...] += _rsum(dy * n)
            dpre = dh * pp * gate * (1.0 - gate)
            dpre_ref[rows, :] = dpre.astype(BF16)
            dpp_ref[rows, :] = (dh * gate).astype(BF16)
            db_ref[...] += _rsum(dpre)
            return carry
        lax.fori_loop(0, TM // ROW_CHUNK, chunk, 0)

    row = pl.BlockSpec((TM, D), lambda i: (i, 0))
    prow = pl.BlockSpec((TM, P), lambda i: (i, 0))
    vec = pl.BlockSpec((1, D), lambda i: (0, 0))
    whole = lambda a: pl.BlockSpec(a.shape, lambda i: (0, 0))
    blocks = ([((TM, D), F32)] * 3 + [((TM, P), F32), ((D, D), BF16), ((P, D), BF16)]
              + [((TM, D), BF16)] * 3 + [((TM, P), BF16)])
    scratch = [((TM, D), F32)] * 2
    return pl.pallas_call(
        body, name="ple_loss", grid=(S // TM,),
        in_specs=[row, prow, row, whole(w_gate), whole(w_proj), vec, vec],
        out_specs=[pl.BlockSpec((1, 1), lambda i: (0, 0)), vec, vec, row, row, row, row, prow],
        out_shape=[jax.ShapeDtypeStruct((1, 1), F32), jax.ShapeDtypeStruct((1, D), F32),
                   jax.ShapeDtypeStruct((1, D), F32), jax.ShapeDtypeStruct((S, D), F32),
                   jax.ShapeDtypeStruct((S, D), BF16), jax.ShapeDtypeStruct((S, D), BF16),
                   jax.ShapeDtypeStruct((S, D), BF16), jax.ShapeDtypeStruct((S, P), BF16)],
        scratch_shapes=[pltpu.VMEM(s, d) for s, d in scratch],
        compiler_params=_params(("arbitrary",), blocks, scratch, temps=[((TM, D), F32)] * 2),
    )(h2, p, target, w_gate, w_proj, b_gate, g_final)


def _ffn_bwd(u0, conv_u0, dact, conv_w, S, TM, F, TC, side=None):
    H = FFN_ROWS
    K = conv_w.shape[0]
    nF, nI = F // TC, S // TM

    def body(xc_ref, cc_ref, cn_ref, dc_ref, dn_ref, wg_ref, wu_ref, o_ref, dwg_ref, dwu_ref, ext_d):
        i = pl.program_id(1)
        @pl.when(i == 0)
        def _():
            dwg_ref[...] = jnp.zeros_like(dwg_ref)
            dwu_ref[...] = jnp.zeros_like(dwu_ref)
        last = (i < nI - 1).astype(F32)

        def lane_chunk(c, carry):
            lo = pl.ds(pl.multiple_of(c * LANES, LANES), LANES)
            lg, lu = lo, pl.ds(pl.multiple_of(TC + c * LANES, LANES), LANES)
            wg, wu = _bcast_taps(wg_ref, K, lo), _bcast_taps(wu_ref, K, lo)
            for r0 in range(0, TM + H, FFN_ROWS):
                if r0 < TM:
                    rows = pl.ds(r0, FFN_ROWS)
                    g, u, da = cc_ref[rows, lg], cc_ref[rows, lu], dc_ref[rows, lo].astype(F32)
                else:
                    g, u, da = cn_ref[:, lg], cn_ref[:, lu], dn_ref[:, lo].astype(F32) * last
                g, u = g.astype(F32), u.astype(F32)
                s = _sigmoid(g)
                ext_d[pl.ds(r0, FFN_ROWS), lg] = da * u * s * (1.0 + g * (1.0 - s))
                ext_d[pl.ds(r0, FFN_ROWS), lu] = da * g * s
            sums_g, sums_u = [None] * K, [None] * K
            for r0 in range(0, TM, FFN_ROWS):
                xg = xc_ref[pl.ds(r0, FFN_ROWS), lg].astype(F32)
                xu = xc_ref[pl.ds(r0, FFN_ROWS), lu].astype(F32)
                g = u = None
                for k in range(K):
                    rows = pl.ds(K - 1 - k + r0, FFN_ROWS)
                    dg, du = ext_d[rows, lg], ext_d[rows, lu]
                    tg, tu = dg * wg[k], du * wu[k]
                    g, u = (tg, tu) if g is None else (g + tg, u + tu)
                    pg, pu = xg * dg, xu * du
                    sums_g[k] = pg if sums_g[k] is None else sums_g[k] + pg
                    sums_u[k] = pu if sums_u[k] is None else sums_u[k] + pu
                o_ref[pl.ds(r0, FFN_ROWS), lg] = g.astype(BF16)
                o_ref[pl.ds(r0, FFN_ROWS), lu] = u.astype(BF16)
            for k in range(K):
                dwg_ref[pl.ds(k, 1), lo] += _rsum(sums_g[k])
                dwu_ref[pl.ds(k, 1), lo] += _rsum(sums_u[k])
            return carry
        lax.fori_loop(0, TC // LANES, lane_chunk, 0)

    blocks = [((TM, 2 * TC), BF16), ((TM, 2 * TC), BF16), ((H, 2 * TC), BF16), ((TM, TC), BF16), ((H, TC), BF16),
              ((K, TC), F32), ((K, TC), F32), ((TM, 2 * TC), BF16), ((K, TC), F32), ((K, TC), F32)]
    scratch = [((TM + H, 2 * TC), F32)]
    nxt = lambda j, i: (jnp.minimum((i + 1) * (TM // H), S // H - 1), j)
    tile = pl.BlockSpec((TM, 2 * TC), lambda j, i: (i, j))
    taps_out = pl.BlockSpec((K, TC), lambda j, i: (0, j))
    outs, side_outs = _call(
        body, side, name="ffn_bwd", grid=(nF, nI),
        in_specs=[tile, tile, pl.BlockSpec((H, 2 * TC), nxt),
                  pl.BlockSpec((TM, TC), lambda j, i: (i, j)), pl.BlockSpec((H, TC), nxt),
                  pl.BlockSpec((K, TC), lambda j, i: (0, j)), pl.BlockSpec((K, TC), lambda j, i: (0, j + nF))],
        out_specs=[tile, taps_out, taps_out],
        out_shape=[jax.ShapeDtypeStruct((S, 2 * F), BF16), jax.ShapeDtypeStruct((K, F), F32),
                   jax.ShapeDtypeStruct((K, F), F32)],
        scratch=scratch,
        params=_params(("arbitrary", "arbitrary"), blocks, scratch),
        args=[u0, conv_u0, conv_u0, dact, dact, conv_w, conv_w])
    return outs if side is None else (outs, side_outs)


def _mixer_bwd_ln(dcat, a1, ln_g, ln_b, S, TM, A):
    def body(dc_ref, a1_ref, g_ref, b_ref, da1_ref, acc_ref):
        @pl.when(pl.program_id(0) == 0)
        def _():
            acc_ref[...] = jnp.zeros_like(acc_ref)
        a1 = a1_ref[...]
        g = g_ref[...]
        mu = jnp.mean(a1, axis=-1, keepdims=True)
        d = a1 - mu
        rstd = lax.rsqrt(jnp.mean(d * d, axis=-1, keepdims=True) + EPS)
        nh = d * rstd
        a2 = nh * g + b_ref[...]
        s = _sigmoid(a2)
        da2 = dc_ref[...].astype(F32) * s * (1.0 + a2 * (1.0 - s))
        dnh = da2 * g
        da1 = rstd * (dnh - jnp.mean(dnh, axis=-1, keepdims=True)
                      - nh * jnp.mean(dnh * nh, axis=-1, keepdims=True))
        da1_ref[...] = da1
        acc_ref[pl.ds(0, 1), :] += _rsum(da2 * nh)
        acc_ref[pl.ds(1, 1), :] += _rsum(da2)
        acc_ref[pl.ds(2, 1), :] += _rsum(da1)

    blocks = [((TM, A), BF16), ((TM, A), F32), ((TM, A), F32), ((4, A), F32)]
    return pl.pallas_call(
        body, name="mixer_bwd_ln", grid=(S // TM,),
        in_specs=[pl.BlockSpec((TM, A), lambda i: (i, 0)), pl.BlockSpec((TM, A), lambda i: (i, 0)),
                  pl.BlockSpec((1, A), lambda i: (0, 0)), pl.BlockSpec((1, A), lambda i: (0, 0))],
        out_specs=[pl.BlockSpec((TM, A), lambda i: (i, 0)), pl.BlockSpec((4, A), lambda i: (0, 0))],
        out_shape=[jax.ShapeDtypeStruct((S, A), F32), jax.ShapeDtypeStruct((4, A), F32)],
        compiler_params=_params(("arbitrary",), blocks, temps=[((TM, A), F32)] * 12),
    )(dcat, a1, ln_g, ln_b)


def _mixer_bwd_conv(z, dcat, da1, conv_a_w, conv_b_w, S, TM, A, side=None):
    H = 32
    KA, KB = conv_a_w.shape[0], conv_b_w.shape[0]
    nI = S // TM
    n_chunks = A // LANES
    RB = _pick(TM, (64, 32))

    def body(zc_ref, zp_ref, zn_ref, dbc_ref, dbn_ref, d1c_ref, d1n_ref, wa_ref, wb_ref,
             dz_ref, dwa_ref, dwb_ref, ext_a0, ext_d1, ext_cb, ext_dc, da0_sc, shifted_d, shifted_a):
        i = pl.program_id(0)
        @pl.when(i == 0)
        def _():
            dwa_ref[...] = jnp.zeros_like(dwa_ref)
            dwb_ref[...] = jnp.zeros_like(dwb_ref)
        first = (i > 0).astype(F32)
        last = (i < nI - 1).astype(F32)
        zc = zc_ref[...].astype(F32)
        zp = zp_ref[...].astype(F32) * first
        a_val, a_gate = zc[:, 0:A], zc[:, A:2 * A]
        b_gate, c_gate, b_h = zc[:, 2 * A:3 * A], zc[:, 3 * A:4 * A], zc[:, 4 * A:5 * A]
        sig = _sigmoid(a_gate)
        ext_a0[pl.ds(0, H), :] = zp[:, 0:A] * _sigmoid(zp[:, A:2 * A])
        ext_a0[pl.ds(H, TM), :] = a_val * sig
        ext_d1[pl.ds(0, TM), :] = d1c_ref[...]
        ext_d1[pl.ds(TM, H), :] = d1n_ref[...] * last
        ext_cb[pl.ds(0, H), :] = zp[:, 3 * A:4 * A] * zp[:, 4 * A:5 * A]
        ext_cb[pl.ds(H, TM), :] = c_gate * b_h
        dbx = dbc_ref[...].astype(F32)
        dcbc = dbx * b_gate
        ext_dc[pl.ds(0, TM), :] = dcbc
        ext_dc[pl.ds(TM, H), :] = dbn_ref[...].astype(F32) * zn_ref[...].astype(F32) * last

        def chunk(c, carry):
            cs = pl.ds(pl.multiple_of(c * LANES, LANES), LANES)
            _shift_copies(ext_d1, shifted_d, cs)
            _shift_copies(ext_a0, shifted_a, cs)
            for r0 in range(0, TM, RB):
                acc = None
                for k in range(KA):
                    term = _rows_at(ext_d1, shifted_d, KA - 1 - k + r0, RB, cs) * wa_ref[pl.ds(k, 1), cs]
                    acc = term if acc is None else acc + term
                da0_sc[pl.ds(r0, RB), cs] = acc
            for k in range(KA):
                acc = None
                for r0 in range(0, TM, RB):
                    term = (_rows_at(ext_a0, shifted_a, H - (KA - 1) + k + r0, RB, cs)
                            * ext_d1[pl.ds(r0, RB), cs])
                    acc = term if acc is None else acc + term
                dwa_ref[pl.ds(k, 1), cs] += _rsum(acc)
            return carry
        lax.fori_loop(0, n_chunks, chunk, 0)

        da0 = da0_sc[...]
        dz_ref[:, 0:A] = (da0 * sig).astype(BF16)
        dz_ref[:, A:2 * A] = (da0 * a_val * sig * (1.0 - sig)).astype(BF16)
        cbc = _taps_causal(ext_cb, wb_ref, KB, H, TM, slice(None))
        dz_ref[:, 2 * A:3 * A] = (dbx * cbc).astype(BF16)
        dcb = _taps_anticausal(ext_dc, wb_ref, KB, TM, slice(None))
        dz_ref[:, 3 * A:4 * A] = (dcb * b_h).astype(BF16)
        dz_ref[:, 4 * A:5 * A] = (dcb * c_gate).astype(BF16)
        grads = _tap_grads(ext_cb, dcbc, KB, H, TM, slice(None))
        for k in range(KB):
            dwb_ref[pl.ds(k, 1), :] += grads[k]

    blocks = [((TM, 5 * A), BF16), ((H, 5 * A), BF16), ((H, A), BF16), ((TM, A), BF16), ((H, A), BF16),
              ((TM, A), F32), ((H, A), F32), ((KA, A), F32), ((KB, A), F32),
              ((TM, 5 * A), BF16), ((KA, A), F32), ((KB, A), F32)]
    scratch = ([((H + TM, A), F32)] * 4 + [((TM, A), F32)]
               + [((SUBLANES - 1, H + TM - SUBLANES, LANES), F32)] * 2)
    vec = lambda r: pl.BlockSpec((r, A), lambda i: (0, 0))
    outs, side_outs = _call(
        body, side, name="mixer_bwd_conv", grid=(nI,),
        in_specs=[pl.BlockSpec((TM, 5 * A), lambda i: (i, 0)),
                  pl.BlockSpec((H, 5 * A), _prev_rows(TM, H, 0)),
                  pl.BlockSpec((H, A), _next_rows(S, TM, H, 2)),
                  pl.BlockSpec((TM, A), lambda i: (i, 1)),
                  pl.BlockSpec((H, A), _next_rows(S, TM, H, 1)),
                  pl.BlockSpec((TM, A), lambda i: (i, 0)),
                  pl.BlockSpec((H, A), _next_rows(S, TM, H, 0)),
                  vec(KA), vec(KB)],
        out_specs=[pl.BlockSpec((TM, 5 * A), lambda i: (i, 0)), vec(KA), vec(KB)],
        out_shape=[jax.ShapeDtypeStruct((S, 5 * A), BF16), jax.ShapeDtypeStruct((KA, A), F32),
                   jax.ShapeDtypeStruct((KB, A), F32)],
        scratch=scratch,
        params=_params(("arbitrary",), blocks, scratch, temps=[((TM, 5 * A), F32)] * 2 + [((TM, A), F32)] * 14),
        args=[z, z, z, dcat, dcat, da1, da1, conv_a_w, conv_b_w])
    return outs if side is None else (outs, side_outs)


def _row_tile(R):
    return _pick(R, (256, 128, 64, 32, 16, 8))


def _scalars(*vals):
    return jnp.stack([jnp.asarray(v, jnp.int32) for v in vals])


def _cast_into_gathered(name, w, chip):
    R, C = w.shape
    TR = _row_tile(R)

    def body(s_ref, w_ref, o_ref):
        o_ref[...] = w_ref[...].astype(BF16)

    grid_spec = pltpu.PrefetchScalarGridSpec(
        num_scalar_prefetch=1, grid=(R // TR,),
        in_specs=[pl.BlockSpec((TR, C), lambda r, s: (r, 0))],
        out_specs=pl.BlockSpec((None, TR, C), lambda r, s: (s[0], r, 0)))
    return pl.pallas_call(body, name=name, grid_spec=grid_spec,
                          out_shape=jax.ShapeDtypeStruct((N_CHIPS, R, C), BF16),
                          compiler_params=_params(("arbitrary",), [((TR, C), F32), ((TR, C), BF16)]),
                          )(_scalars(chip), w)


def _add_pair(name, dw, recv, c, chip):
    _, _, Rh, C = dw.shape
    TR = _row_tile(Rh)

    def body(s_ref, a_ref, b_ref, o_ref, ob_ref):
        s = a_ref[...] + b_ref[...]
        ob_ref[...] = s.astype(BF16)

        @pl.when(pl.program_id(1) == s_ref[1])
        def _():
            o_ref[...] = s

    grid_spec = pltpu.PrefetchScalarGridSpec(
        num_scalar_prefetch=1, grid=(Rh // TR, N_CHIPS),
        in_specs=[pl.BlockSpec((None, None, TR, C), lambda r, k, s: (k, s[0], r, 0)),
                  pl.BlockSpec((None, TR, C), lambda r, k, s: (k, r, 0))],
        out_specs=[pl.BlockSpec((TR, C), lambda r, k, s: (r, 0)),
                   pl.BlockSpec((None, TR, C), lambda r, k, s: (k, r, 0))])
    return pl.pallas_call(body, name=name, grid_spec=grid_spec,
                          out_shape=[jax.ShapeDtypeStruct((Rh, C), F32),
                                     jax.ShapeDtypeStruct((N_CHIPS, Rh, C), BF16)],
                          compiler_params=_params(("arbitrary", "arbitrary"), [((TR, C), F32)] * 4),
                          )(_scalars(c, chip), dw, recv)


def _add_chips(name, own, recv, c):
    Rh, C = own.shape
    TR = _row_tile(Rh)

    def body(s_ref, p_ref, r_ref, o_ref):
        o_ref[...] = ((p_ref[...] + r_ref[0].astype(F32)) + r_ref[1].astype(F32)) + r_ref[2].astype(F32)

    grid_spec = pltpu.PrefetchScalarGridSpec(
        num_scalar_prefetch=1, grid=(Rh // TR,),
        in_specs=[pl.BlockSpec((TR, C), lambda r, s: (r, 0)),
                  pl.BlockSpec((N_CHIPS - 1, TR, C), lambda r, s: (0, r, 0))],
        out_specs=pl.BlockSpec((None, TR, C), lambda r, s: (s[0], r, 0)))
    return pl.pallas_call(body, name=name, grid_spec=grid_spec,
                          out_shape=jax.ShapeDtypeStruct((2, Rh, C), F32),
                          compiler_params=_params(("arbitrary",), [((N_CHIPS + 1, TR, C), F32)]),
                          )(_scalars(c), own, recv)


def _sum_devices(name, parts):
    _, R, C = parts.shape

    def body(p_ref, o_ref):
        acc = p_ref[0]
        for d in range(1, N_DEV):
            acc = acc + p_ref[d]
        o_ref[...] = acc

    return pl.pallas_call(body, name=name, out_shape=jax.ShapeDtypeStruct((R, C), F32),
                          in_specs=[pl.BlockSpec(memory_space=pltpu.VMEM)],
                          out_specs=pl.BlockSpec(memory_space=pltpu.VMEM))(parts)


def _adamw(name, w, g, m, v, copy_grad=False):
    R, C = w.shape
    TR = _pick(R, (128, 64, 32, 16, 8))
    c1 = 1.0 - ADAM_B1 ** ADAM_STEP
    c2 = 1.0 - ADAM_B2 ** ADAM_STEP
    n_out = 4 if copy_grad else 3

    def body(w_ref, g_ref, m_ref, v_ref, d_ref, nm_ref, nv_ref, *g_out):
        g_ = g_ref[...]
        nm = ADAM_B1 * m_ref[...] + (1.0 - ADAM_B1) * g_
        nv = ADAM_B2 * v_ref[...] + (1.0 - ADAM_B2) * (g_ * g_)
        d_ref[...] = -ADAM_LR * ((nm / c1) / (jnp.sqrt(nv / c2) + ADAM_EPS) + ADAM_WD * w_ref[...])
        nm_ref[...] = nm
        nv_ref[...] = nv
        for ref in g_out:
            ref[...] = g_

    spec = pl.BlockSpec((TR, C), lambda r: (r, 0))
    shp = jax.ShapeDtypeStruct((R, C), F32)
    return pl.pallas_call(body, name=name, grid=(R // TR,), in_specs=[spec] * 4, out_specs=[spec] * n_out,
                          out_shape=[shp] * n_out,
                          compiler_params=_params(("arbitrary",), [((TR, C), F32)] * (4 + n_out)))(w, g, m, v)


def _place():
    x, y, c = lax.axis_index("x"), lax.axis_index("y"), lax.axis_index("c")
    others = [(1 - x, y), (x, 1 - y), (1 - x, 1 - y)]
    return x, y, c, others


def _allgather_small(name, block):
    R, C = block.shape

    def body(x_ref, out_ref, send_sems, recv_sems, local_sem):
        x, y, c, chips = _place()
        me, sibling = (x, y, c), (x, y, 1 - c)

        def rows(px, py, pc):
            return out_ref.at[4 * px + 2 * py + pc]

        def copy(k, blk, to, src=None):
            return pltpu.make_async_remote_copy(
                src_ref=rows(*blk) if src is None else src, dst_ref=rows(*blk),
                send_sem=send_sems.at[k], recv_sem=recv_sems.at[k], device_id=to, device_id_type=MESH)

        mine = pltpu.make_async_copy(x_ref, rows(*me), local_sem)
        mine.start()
        first = [copy(0, me, sibling, src=x_ref)]
        first += [copy(1 + j, me, (*chip, c), src=x_ref) for j, chip in enumerate(chips)]
        for cp in first:
            cp.start()
        passed = [copy(4 + j, (*chip, c), sibling) for j, chip in enumerate(chips)]
        for j, chip in enumerate(chips):
            copy(1 + j, (*chip, c), me).wait_recv()
            passed[j].start()
        copy(0, sibling, me).wait_recv()
        for j, chip in enumerate(chips):
            copy(4 + j, (*chip, 1 - c), me).wait_recv()
        for cp in first + passed:
            cp.wait_send()
        mine.wait()

    return pl.pallas_call(
        body, name=name, out_shape=jax.ShapeDtypeStruct((N_DEV, R, C), F32),
        in_specs=[pl.BlockSpec(memory_space=pltpu.VMEM)], out_specs=pl.BlockSpec(memory_space=pltpu.VMEM),
        scratch_shapes=[pltpu.SemaphoreType.DMA((7,)), pltpu.SemaphoreType.DMA((7,)), pltpu.SemaphoreType.DMA],
    )(block)


def _gather_side(bufs, across, within):
    def rows(ref, chip, half, piece):
        _, r0, n = piece
        return ref.at[2 * chip[0] + chip[1], pl.ds(half * (ref.shape[1] // 2) + r0, n)]

    def copies(ins, outs, send_sems, recv_sems, base):
        x, y, c, chips = _place()
        sibling = (x, y, 1 - c)
        pairs = []

        def add(k, src, dst, to, arrival):
            mk = lambda s, d, dev: pltpu.make_async_remote_copy(
                src_ref=s, dst_ref=d, send_sem=send_sems.at[base + k], recv_sem=recv_sems.at[base + k],
                device_id=dev, device_id_type=MESH)
            pairs.append((mk(src, dst, to), mk(arrival, arrival, (x, y, c))))

        for p, piece in enumerate(across):
            ref = outs[piece[0]]
            for j, chip in enumerate(chips):
                mine = rows(ref, (x, y), c, piece)
                add(3 * p + j, mine, mine, (*chip, c), rows(ref, chip, c, piece))
        for q, piece in enumerate(within):
            ref = outs[piece[0]]
            for j, chip in enumerate(chips):
                held = rows(ref, chip, c, piece)
                add(3 * (len(across) + q) + j, held, held, sibling, rows(ref, chip, 1 - c, piece))
        return pairs

    def start(*refs):
        for send, _ in copies(*refs):
            send.start()

    def wait(*refs):
        pairs = copies(*refs)
        for _, arrival in pairs:
            arrival.wait_recv()
        for send, _ in pairs:
            send.wait_send()

    return _Side(list(bufs), [jax.ShapeDtypeStruct(b.shape, b.dtype) for b in bufs],
                 3 * (len(across) + len(within)), start, wait, aliases=tuple((i, i) for i in range(len(bufs))))


def _chip_exchange(parts):
    n = len(parts)

    def copies(ins, outs, send_sems, recv_sems, base):
        x, y, c, chips = _place()
        return [pltpu.make_async_remote_copy(
            src_ref=ins[a].at[2 * chip[0] + chip[1]], dst_ref=outs[a].at[j],
            send_sem=send_sems.at[base + 3 * a + j], recv_sem=recv_sems.at[base + 3 * a + j],
            device_id=(*chip, c), device_id_type=MESH) for a in range(n) for j, chip in enumerate(chips)]

    return _Side(list(parts), [jax.ShapeDtypeStruct((N_CHIPS - 1,) + p.shape[1:], p.dtype) for p in parts],
                 3 * n, *_start_wait(copies))


def _pair_exchange(grads):
    def copies(ins, outs, send_sems, recv_sems, base):
        x, y, c, _ = _place()
        return [pltpu.make_async_remote_copy(
            src_ref=ins[a].at[:, 1 - c], dst_ref=outs[a], send_sem=send_sems.at[base + a],
            recv_sem=recv_sems.at[base + a], device_id=(x, y, 1 - c), device_id_type=MESH)
            for a in range(len(grads))]

    return _Side(list(grads), [jax.ShapeDtypeStruct((N_CHIPS,) + g.shape[2:], F32) for g in grads],
                 len(grads), *_start_wait(copies))


def _start_wait(copies):
    def start(*refs):
        for cp in copies(*refs):
            cp.start()

    def wait(*refs):
        cps = copies(*refs)
        for cp in cps:
            cp.wait_recv()
        for cp in cps:
            cp.wait_send()
    return start, wait


def _both(first, second):
    n_in, n_out = len(first.ins), len(first.out_shapes)

    def run(which):
        def go(ins, outs, send_sems, recv_sems, base):
            getattr(first, which)(ins[:n_in], outs[:n_out], send_sems, recv_sems, base)
            getattr(second, which)(ins[n_in:], outs[n_out:], send_sems, recv_sems, base + first.n_sems)
        return go

    aliases = first.aliases + tuple((a + n_in, b + n_out) for a, b in second.aliases)
    return _Side(first.ins + second.ins, first.out_shapes + second.out_shapes,
                 first.n_sems + second.n_sems, run("start"), run("wait"), aliases)


def _share_halves(halves):
    def copies(ins, outs, send_sems, recv_sems, base):
        x, y, c, _ = _place()
        pairs = []
        for a in range(len(halves)):
            mk = lambda s, d, dev, a=a: pltpu.make_async_remote_copy(
                src_ref=s, dst_ref=d, send_sem=send_sems.at[base + a], recv_sem=recv_sems.at[base + a],
                device_id=dev, device_id_type=MESH)
            theirs = outs[a].at[1 - c]
            pairs.append((mk(outs[a].at[c], outs[a].at[c], (x, y, 1 - c)), mk(theirs, theirs, (x, y, c))))
        return pairs

    def start(*refs):
        for send, _ in copies(*refs):
            send.start()

    def wait(*refs):
        pairs = copies(*refs)
        for _, arrival in pairs:
            arrival.wait_recv()
        for send, _ in pairs:
            send.wait_send()

    return _Side(list(halves), [jax.ShapeDtypeStruct(h.shape, F32) for h in halves], len(halves), start, wait,
                 aliases=tuple((i, i) for i in range(len(halves))))


def _pack(arrays):
    pieces = []
    for a in arrays:
        flat = a.reshape(-1).astype(F32)
        pieces.append(jnp.pad(flat, (0, (-flat.size) % PACK_ALIGN)))
    return jnp.concatenate(pieces).reshape(-1, LANES)


def _unpack(buf, shapes):
    lead = buf.shape[:-2]
    flat = buf.reshape(lead + (-1,))
    out, off = [], 0
    for shp in shapes:
        size = 1
        for s in shp:
            size *= s
        out.append(flat[..., off:off + size].reshape(lead + tuple(shp)))
        off += size + (-size) % PACK_ALIGN
    return out


def _gather_channels(buf, shapes):
    per_chip = _unpack(buf[0::2], shapes)
    return [jnp.transpose(a, (1, 0, 2)).reshape(a.shape[1], -1) for a in per_chip]


def _mm_tile(n, rows, limit_bytes=6 * 1024 * 1024):
    for t in (1408, 1280, 1024, 640, 512, 384, 256, 128):
        if n % t == 0 and rows * t * 2 <= limit_bytes:
            return t
    raise ValueError(f"no column tile for {n} x {rows}")


def kernel(x, p, norm_mix_g, w_in, conv_a_w, conv_a_b, ln_a_g, ln_a_b, conv_b_w, w_out, norm_ffn_g, w_up, conv_ffn_w, w_down, w_ple_gate, b_ple_gate, w_ple_proj, norm_final_g, loss_target, m_norm_mix_g, m_w_in, m_conv_a_w, m_conv_a_b, m_ln_a_g, m_ln_a_b, m_conv_b_w, m_w_out, m_norm_ffn_g, m_w_up, m_conv_ffn_w, m_w_down, m_w_ple_gate, m_b_ple_gate, m_w_ple_proj, m_norm_final_g, v_norm_mix_g, v_w_in, v_conv_a_w, v_conv_a_b, v_ln_a_g, v_ln_a_b, v_conv_b_w, v_w_out, v_norm_ffn_g, v_w_up, v_conv_ffn_w, v_w_down, v_w_ple_gate, v_b_ple_gate, v_w_ple_proj, v_norm_final_g):
    S, D = x.shape[1], x.shape[2]
    P = p.shape[3]
    A = conv_a_b.shape[1]
    F = w_down.shape[1] * N_CHIPS
    KA, KB, KF = conv_a_w.shape[1], conv_b_w.shape[1], conv_ffn_w.shape[1]
    xi, yi, ci = lax.axis_index("x"), lax.axis_index("y"), lax.axis_index("c")
    chip = 2 * xi + yi

    TM = _pick(S, (512, 256, 128))
    TL = _pick(S, (1024, 512, 256, 128))
    TE = _pick(S, (256, 128))
    TC = _pick(2 * F // N_CHIPS, (1408, 1024, 512, 256, 128))
    ffn_place = _pair_tile(F // TC)

    x2, p2, t2 = x.reshape(S, D), p.reshape(S, P), loss_target.reshape(S, D)
    gfin = norm_final_g.reshape(1, D)

    big = dict(w_in=w_in[0], w_out=w_out[0], w_up=w_up[0], w_down=w_down[0],
               w_ple_gate=w_ple_gate[0], w_ple_proj=w_ple_proj[0])
    names = list(big)
    buf = {n: _cast_into_gathered("cast_" + n, big[n], chip) for n in names}
    half = {n: big[n].shape[0] // 2 for n in names}
    up_a = half["w_up"] // 2
    (w_in3,) = _comm_only("gather_w_in_across", _gather_side([buf["w_in"]], [(0, 0, half["w_in"])], []))
    (w_in3,) = _comm_only("gather_w_in_within", _gather_side([w_in3], [], [(0, 0, half["w_in"])]))

    tap_shapes = [(KA, A // N_CHIPS), (KB, A // N_CHIPS), (KF, 2 * F // N_CHIPS)]
    taps = _allgather_small("allgather_taps", _pack([conv_a_w[0], conv_b_w[0], conv_ffn_w[0]]))
    conv_a_f, conv_b_f, conv_ffn_f = _gather_channels(taps, tap_shapes)

    def rms_prologue(rows, row_r, vec_r, ro_r, ao_r):
        h = row_r[0][rows, :]
        hn = (h * _rms_stats(h) * vec_r[0][...]).astype(BF16)
        ro_r[0][rows, :] = hn
        return [hn]

    def cast_prologue(rows, row_r, vec_r, ro_r, ao_r):
        hb = row_r[0][rows, :].astype(BF16)
        ro_r[0][rows, :] = hb
        return [hb]

    plain = lambda accs, tile_r, cv_r: [accs[0]]
    residual = lambda accs, tile_r, cv_r: [tile_r[0][...] + accs[0]]

    (z, hn1), (w_out_t, w_up_t) = _rows_mm(
        "in_proj", S, TL, 5 * A, _mm_tile(5 * A // N_CHIPS, D), row_ins=[x2], vec_ins=[norm_mix_g],
        weights=[(w_in3, "nn3")], tile_outs=[BF16], row_outs=[(D, BF16)], prologue=rms_prologue, epilogue=plain,
        side=_gather_side([buf["w_out"], buf["w_up"]], [(0, 0, half["w_out"]), (1, 0, up_a)], []))
    (a1, cat), (w_out3, w_up_t) = _mixer_fwd(
        z, conv_a_f, conv_a_b, ln_a_g, ln_a_b, conv_b_f, S, TE, A,
        side=_gather_side([w_out_t, w_up_t], [(1, up_a, half["w_up"] - up_a)],
                          [(0, 0, half["w_out"]), (1, 0, up_a)]))
    w_out_f = w_out3.reshape(2 * A, D)
    (h1,), (w_up3, w_proj_t) = _rows_mm(
        "out_proj", S, TL, D, _mm_tile(D, 2 * A), row_ins=[cat], weights=[(w_out_f, "nn2")],
        tile_ins=[x2], tile_outs=[F32], epilogue=residual,
        side=_gather_side([w_up_t, buf["w_ple_proj"]], [(1, 0, half["w_ple_proj"])],
                          [(0, up_a, half["w_up"] - up_a)]))
    (u0, hn2), (w_down_t, w_gate_t, w_proj3) = _rows_mm(
        "up_proj", S, TL, 2 * F, TC, row_ins=[h1], vec_ins=[norm_ffn_g],
        weights=[(w_up3, "nn3")], tile_outs=[BF16], row_outs=[(D, BF16)],
        prologue=rms_prologue, epilogue=plain, place=ffn_place,
        side=_gather_side([buf["w_down"], buf["w_ple_gate"], w_proj_t],
                          [(0, 0, half["w_down"]), (1, 0, half["w_ple_gate"])], [(2, 0, half["w_ple_proj"])]))
    (act, conv_u0), (w_down3, w_gate3) = _ffn_act(
        u0, conv_ffn_f, S, TM, F, TC,
        side=_gather_side([w_down_t, w_gate_t], [], [(0, 0, half["w_down"]), (1, 0, half["w_ple_gate"])]))
    w_down_f = w_down3.reshape(F, D)
    w_gate_f = w_gate3.reshape(D, D)
    w_proj_f = jnp.transpose(w_proj3, (1, 0, 2)).reshape(P, D)
    (h2,) = _rows_mm("down_proj", S, TL, D, _mm_tile(D, F), row_ins=[act], weights=[(w_down_f, "nn2")],
                     tile_ins=[h1], tile_outs=[F32], epilogue=residual)

    loss_part, g_norm_final, g_b_gate, dh3, dpre, dpp, h2b, pb = _ple_loss(
        h2, p2, t2, w_gate_f, w_proj_f, b_ple_gate, gfin, S, TE)

    TK = _pick(S, (1024, 512, 256, 128))
    wt = lambda n: _pick(n, (1408, 1280, 1024, 512, 256, 128))
    chip_sums, from_chips = {}, {}

    def to_sibling(parts):
        ns = list(parts)
        halves = [parts[n].reshape(N_CHIPS, 2, big[n].shape[0] // 2, big[n].shape[1]) for n in ns]
        return ns, halves, _pair_exchange(halves)

    def to_chips(ns, halves, from_sibling):
        sums = [_add_pair("pair_sum_" + n, h, r, ci, chip) for n, h, r in zip(ns, halves, from_sibling)]
        for n, (s, _) in zip(ns, sums):
            chip_sums[n] = s
        return ns, _chip_exchange([b for _, b in sums])

    def landed(ns, side_outs):
        for n, r in zip(ns, side_outs):
            from_chips[n] = r

    ns, halves, side = to_sibling(dict(
        w_ple_gate=_tn_mm("dw_ple_gate", h2b, dpre, wt(D), wt(D), TK),
        w_ple_proj=_tn_mm("dw_ple_proj", pb, dpp, wt(P), wt(D // N_CHIPS), TK, cols_per_chip=D // N_CHIPS)))
    (dh2,), got = _rows_mm("ple_bwd", S, TL, D, _mm_tile(D, D), row_ins=[dpre], weights=[(w_gate_f, "nt2")],
                           tile_ins=[dh3], tile_outs=[F32], epilogue=residual, side=side)
    ple_ns, ple_chips = to_chips(ns, halves, got)
    (dact, dh2b), got = _rows_mm("down_bwd", S, TL, F, _mm_tile(F, D), row_ins=[dh2], weights=[(w_down_f, "nt2")],
                                 tile_outs=[BF16], row_outs=[(D, BF16)], prologue=cast_prologue, epilogue=plain,
                                 side=ple_chips)
    landed(ple_ns, got)
    ns, halves, side = to_sibling(dict(w_down=_tn_mm("dw_down", act, dh2b, wt(F), wt(D), TK)))
    (du0, g_conv_gate, g_conv_up), got = _ffn_bwd(u0, conv_u0, dact, conv_ffn_f, S, TM, F, TC, side=side)
    down_ns, down_chips = to_chips(ns, halves, got)
    g_conv_ffn = jnp.concatenate([g_conv_gate, g_conv_up], axis=1)
    ns, halves, side = to_sibling(dict(
        w_up=_tn_mm("dw_up", hn2, du0, wt(D), TC, TK, cols_per_chip=2 * F // N_CHIPS, place=ffn_place)))

    def up_bwd_epilogue(acc, rows, row_r, vec_r, ro_r, ao_r):
        dh, dg = _rms_bwd(row_r[0][rows, :], vec_r[0][...], acc)
        dh1_ = row_r[1][rows, :] + dh
        ro_r[0][rows, :] = dh1_
        ro_r[1][rows, :] = dh1_.astype(BF16)
        ao_r[0][...] += dg

    (dh1, dh1b, g_norm_ffn), got = _kloop_mm(
        "up_bwd", S, TM, du0, w_up3, TC, row_ins=[h1, dh2], vec_ins=[norm_ffn_g],
        row_outs=[(D, F32), (D, BF16)], acc_outs=[(1, D)], epilogue=up_bwd_epilogue, place=ffn_place,
        side=_both(down_chips, side))
    landed(down_ns, got[:len(down_ns)])
    up_ns, up_chips = to_chips(ns, halves, got[len(down_ns):])
    ns, halves, side = to_sibling(dict(w_out=_tn_mm("dw_out", cat, dh1b, wt(2 * A), wt(D), TK)))
    (dcat,), got = _rows_mm("out_bwd", S, TL, 2 * A, _mm_tile(2 * A, D), row_ins=[dh1b],
                            weights=[(w_out_f, "nt2")], tile_outs=[BF16], epilogue=plain, side=side)
    out_ns, out_chips = to_chips(ns, halves, got)
    da1, ln_sums = _mixer_bwd_ln(dcat, a1, ln_a_g, ln_a_b, S, TE, A)
    (dz, g_conv_a, g_conv_b), got = _mixer_bwd_conv(z, dcat, da1, conv_a_f, conv_b_f, S, TE, A,
                                                    side=_both(up_chips, out_chips))
    landed(up_ns + out_ns, got)
    ns, halves, side = to_sibling(dict(
        w_in=_tn_mm("dw_in", hn1, dz, wt(D), wt(5 * A // N_CHIPS), TK, cols_per_chip=5 * A // N_CHIPS)))
    ns, side = to_chips(ns, halves, _comm_only("grads_exchange_pairs_in", side))

    def in_bwd_epilogue(acc, rows, row_r, vec_r, ro_r, ao_r):
        dh, dg = _rms_bwd(row_r[0][rows, :], vec_r[0][...], acc)
        ro_r[0][rows, :] = row_r[1][rows, :] + dh
        ao_r[0][...] += dg

    early = [n for n in names if n != "w_in"]
    early_halves = [_add_chips("chip_sum_" + n, chip_sums[n], from_chips[n], ci) for n in early]
    (dx, g_norm_mix), got = _kloop_mm(
        "in_bwd", S, TM, dz, w_in3, _mm_tile(5 * A // N_CHIPS, D), row_ins=[x2, dh1],
        vec_ins=[norm_mix_g], row_outs=[(D, F32)], acc_outs=[(1, D)], epilogue=in_bwd_epilogue,
        side=_both(side, _share_halves(early_halves)))
    landed(ns, got[:1])
    shared = dict(zip(early, got[1:]))
    (shared["w_in"],) = _comm_only("grads_share_w_in", _share_halves(
        [_add_chips("chip_sum_w_in", chip_sums["w_in"], from_chips["w_in"], ci)]))

    reduced = [shared[n] for n in names]
    moments = dict(w_in=(m_w_in, v_w_in), w_out=(m_w_out, v_w_out), w_up=(m_w_up, v_w_up),
                   w_down=(m_w_down, v_w_down), w_ple_gate=(m_w_ple_gate, v_w_ple_gate),
                   w_ple_proj=(m_w_ple_proj, v_w_ple_proj))
    grads, deltas, new_m, new_v = {}, {}, {}, {}
    for n, g in zip(names, reduced):
        d_, m_, v_, g = _adamw("adamw_" + n, big[n], g.reshape(big[n].shape), moments[n][0][0], moments[n][1][0],
                               copy_grad=True)
        grads[n], deltas[n], new_m[n], new_v[n] = g[None], d_[None], m_[None], v_[None]

    small = ["norm_mix_g", "conv_a_w", "conv_a_b", "ln_a_g", "ln_a_b", "conv_b_w", "norm_ffn_g",
             "conv_ffn_w", "b_ple_gate", "norm_final_g"]
    small_part = [g_norm_mix, g_conv_a, ln_sums[2:3], ln_sums[0:1], ln_sums[1:2], g_conv_b, g_norm_ffn,
                  g_conv_ffn, g_b_gate, g_norm_final]
    full_shapes = [a.shape for a in small_part]
    summed = _sum_devices("small_grads_sum", _allgather_small("allgather_small_grads", _pack(small_part)))
    small_g = dict(zip(small, _unpack(summed, full_shapes)))
    for n, width in (("conv_a_w", A), ("conv_b_w", A), ("conv_ffn_w", 2 * F)):
        small_g[n] = lax.dynamic_slice_in_dim(small_g[n], chip * (width // N_CHIPS), width // N_CHIPS, axis=1)
    small_w = dict(norm_mix_g=(norm_mix_g, m_norm_mix_g, v_norm_mix_g), conv_a_w=(conv_a_w, m_conv_a_w, v_conv_a_w),
                   conv_a_b=(conv_a_b, m_conv_a_b, v_conv_a_b), ln_a_g=(ln_a_g, m_ln_a_g, v_ln_a_g),
                   ln_a_b=(ln_a_b, m_ln_a_b, v_ln_a_b), conv_b_w=(conv_b_w, m_conv_b_w, v_conv_b_w),
                   norm_ffn_g=(norm_ffn_g, m_norm_ffn_g, v_norm_ffn_g),
                   conv_ffn_w=(conv_ffn_w, m_conv_ffn_w, v_conv_ffn_w),
                   b_ple_gate=(b_ple_gate, m_b_ple_gate, v_b_ple_gate),
                   norm_final_g=(norm_final_g, m_norm_final_g, v_norm_final_g))
    out_shapes = [small_w[n][0].shape for n in small]
    packed_g = _pack([small_g[n] for n in small])
    packed = [_pack([small_w[n][k] for n in small]) for k in range(3)]
    d_s, m_s, v_s = _adamw("adamw_small", packed[0], packed_g, packed[1], packed[2])
    for n, g, d_, m_, v_ in zip(small, _unpack(packed_g, out_shapes), _unpack(d_s, out_shapes),
                                _unpack(m_s, out_shapes), _unpack(v_s, out_shapes)):
        grads[n], deltas[n], new_m[n], new_v[n] = g, d_, m_, v_

    order = ["norm_mix_g", "w_in", "conv_a_w", "conv_a_b", "ln_a_g", "ln_a_b", "conv_b_w", "w_out", "norm_ffn_g",
             "w_up", "conv_ffn_w", "w_down", "w_ple_gate", "b_ple_gate", "w_ple_proj", "norm_final_g"]
    loss = lax.psum(loss_part[0, 0], ("x", "y", "c"))
    return (loss, dx.reshape(x.shape), *[grads[n] for n in order], *[deltas[n] for n in order],
            *[new_m[n] for n in order], *[new_v[n] for n in order])
```

```python
from typing import Callable, NamedTuple

import jax
import jax.numpy as jnp
from jax import lax
from jax.experimental import pallas as pl
from jax.experimental.pallas import tpu as pltpu

F32 = jnp.float32
BF16 = jnp.bfloat16
MESH = pl.DeviceIdType.MESH
ANY = pl.BlockSpec(memory_space=pl.ANY)

EPS = 1e-6
ADAM_LR = 0.001
ADAM_B1 = 0.9
ADAM_B2 = 0.999
ADAM_EPS = 1e-08
ADAM_WD = 0.01
ADAM_STEP = 10

N_CHIPS = 4
N_DEV = 8
LANES = 128
SUBLANES = 8
PACK_ALIGN = LANES * SUBLANES
ROW_CHUNK = 32
VMEM_CAP = 60 * 1024 * 1024
VMEM_SLACK = 6 * 1024 * 1024


def _pick(n, cands):
    for c in cands:
        if n % c == 0:
            return c
    raise ValueError(f"no tile of {cands} divides {n}")


def _nbytes(shape, dtype):
    n = 1
    for s in shape:
        if s is not None:
            n *= s
    return n * jnp.dtype(dtype).itemsize


def _params(sem, blocks, scratch=(), temps=()):
    est = (2 * sum(_nbytes(s, d) for s, d in blocks) + sum(_nbytes(s, d) for s, d in scratch)
           + sum(_nbytes(s, d) for s, d in temps))
    return pltpu.CompilerParams(dimension_semantics=sem,
                                vmem_limit_bytes=min(est + VMEM_SLACK, VMEM_CAP))


def _in_hbm(arrays):
    return [pltpu.with_memory_space_constraint(a, pltpu.HBM) for a in arrays]


def _sigmoid(x):
    return 1.0 / (1.0 + jnp.exp(-x))


def _rsum(x):
    return jnp.sum(x, axis=0, keepdims=True)


class _Side(NamedTuple):
    ins: list
    out_shapes: list
    n_sems: int
    start: Callable
    wait: Callable
    aliases: tuple = ()


def _call(body, side, *, name, grid, in_specs, out_specs, out_shape, scratch, params, args):
    vmem = [pltpu.VMEM(s, d) for s, d in scratch]
    if side is None:
        outs = pl.pallas_call(body, name=name, grid=grid, in_specs=in_specs, out_specs=out_specs,
                              out_shape=out_shape, scratch_shapes=vmem, compiler_params=params)(*args)
        return list(outs), []
    n_in, n_out, n_sc = len(in_specs), len(out_specs), len(scratch)
    ns_in, ns_out = len(side.ins), len(side.out_shapes)

    def carrier(*refs):
        pos = [0]
        def take(n):
            pos[0] += n
            return refs[pos[0] - n:pos[0]]
        ins, s_ins, outs, s_outs, scr = take(n_in), take(ns_in), take(n_out), take(ns_out), take(n_sc)
        send_sems, recv_sems = take(2)
        first = last = None
        for axis, extent in enumerate(grid):
            at_start, at_end = pl.program_id(axis) == 0, pl.program_id(axis) == extent - 1
            first = at_start if first is None else first & at_start
            last = at_end if last is None else last & at_end

        @pl.when(first)
        def _():
            side.start(s_ins, s_outs, send_sems, recv_sems, 0)
        body(*ins, *outs, *scr)

        @pl.when(last)
        def _():
            side.wait(s_ins, s_outs, send_sems, recv_sems, 0)

    outs = pl.pallas_call(
        carrier, name=name, grid=grid, in_specs=list(in_specs) + [ANY] * ns_in,
        out_specs=list(out_specs) + [ANY] * ns_out, out_shape=list(out_shape) + list(side.out_shapes),
        scratch_shapes=vmem + [pltpu.SemaphoreType.DMA((side.n_sems,)), pltpu.SemaphoreType.DMA((side.n_sems,))],
        input_output_aliases={n_in + a: n_out + b for a, b in side.aliases},
        compiler_params=params)(*args, *_in_hbm(side.ins))
    return list(outs[:n_out]), list(outs[n_out:])


def _comm_only(name, side):
    n_in = len(side.ins)

    def body(*refs):
        ins, outs = refs[:n_in], refs[n_in:n_in + len(side.out_shapes)]
        send_sems, recv_sems = refs[n_in + len(side.out_shapes):]
        side.start(ins, outs, send_sems, recv_sems, 0)
        side.wait(ins, outs, send_sems, recv_sems, 0)

    return pl.pallas_call(
        body, name=name, out_shape=list(side.out_shapes), in_specs=[ANY] * n_in,
        out_specs=[ANY] * len(side.out_shapes),
        scratch_shapes=[pltpu.SemaphoreType.DMA((side.n_sems,)), pltpu.SemaphoreType.DMA((side.n_sems,))],
        input_output_aliases=dict(side.aliases),
    )(*_in_hbm(side.ins))


def _rms_stats(x):
    return lax.rsqrt(jnp.mean(x * x, axis=-1, keepdims=True) + EPS)


def _rms_bwd(h, g, dout):
    r = _rms_stats(h)
    n = h * r
    dn = dout * g
    dh = r * (dn - n * jnp.mean(dn * n, axis=-1, keepdims=True))
    return dh, _rsum(dout * n)


def _identity(t):
    return t


def _chip_major(nb, place=_identity):
    return lambda i, j: (place(j) // nb, 0, place(j) % nb)


def _rows_mm(name, S, TM, N, TN, *, row_ins, vec_ins=(), colvec_ins=(), weights, tile_ins=(),
             tile_outs, row_outs=(), acc_outs=(), prologue=None, epilogue, place=_identity, side=None):
    nI, nJ = S // TM, N // TN
    n_row, n_vec, n_cv, n_w, n_tile = len(row_ins), len(vec_ins), len(colvec_ins), len(weights), len(tile_ins)
    n_to, n_ro, n_ao = len(tile_outs), len(row_outs), len(acc_outs)

    in_specs, blocks, scratch, ks = [], [], [], []
    for a in row_ins:
        in_specs.append(pl.BlockSpec((TM, a.shape[1]), lambda i, j: (i, 0)))
        blocks.append(((TM, a.shape[1]), a.dtype))
    for a in vec_ins:
        in_specs.append(pl.BlockSpec(a.shape, lambda i, j: (0, 0)))
        blocks.append((a.shape, a.dtype))
    for a in colvec_ins:
        in_specs.append(pl.BlockSpec((1, TN), lambda i, j: (0, j)))
        blocks.append(((1, TN), a.dtype))
    for w, mode in weights:
        if mode == "nn2":
            k = w.shape[0]
            in_specs.append(pl.BlockSpec((k, TN), lambda i, j: (0, j)))
        elif mode == "nn3":
            k = w.shape[1]
            in_specs.append(pl.BlockSpec((None, k, TN), _chip_major(w.shape[2] // TN, place)))
        else:
            k = w.shape[1]
            in_specs.append(pl.BlockSpec((TN, k), lambda i, j: (j, 0)))
        ks.append(k)
        blocks.append(((k, TN), BF16))
        if prologue is not None:
            scratch.append(((TM, k), BF16))
    for a in tile_ins:
        in_specs.append(pl.BlockSpec((TM, TN), lambda i, j: (i, j)))
        blocks.append(((TM, TN), a.dtype))

    out_shape, out_specs = [], []
    for dt in tile_outs:
        out_shape.append(jax.ShapeDtypeStruct((S, N), dt))
        out_specs.append(pl.BlockSpec((TM, TN), lambda i, j: (i, j)))
        blocks.append(((TM, TN), dt))
    for width, dt in row_outs:
        out_shape.append(jax.ShapeDtypeStruct((S, width), dt))
        out_specs.append(pl.BlockSpec((TM, width), lambda i, j: (i, 0)))
        blocks.append(((TM, width), dt))
    for rows, width in acc_outs:
        out_shape.append(jax.ShapeDtypeStruct((rows, width), F32))
        out_specs.append(pl.BlockSpec((rows, width), lambda i, j: (0, 0)))
        blocks.append(((rows, width), F32))

    modes = [m for _, m in weights]

    def body(*refs):
        pos = 0
        def take(n):
            nonlocal pos
            out = refs[pos:pos + n]
            pos += n
            return out
        row_r, vec_r, cv_r, w_r, tile_r = take(n_row), take(n_vec), take(n_cv), take(n_w), take(n_tile)
        to_r, ro_r, ao_r, a_sc = take(n_to), take(n_ro), take(n_ao), take(len(scratch))
        i, j = pl.program_id(0), pl.program_id(1)

        if prologue is None:
            a_sc = row_r[:n_w]
        else:
            @pl.when(j == 0)
            def _():
                if n_ao:
                    @pl.when(i == 0)
                    def _():
                        for r in ao_r:
                            r[...] = jnp.zeros_like(r)

                def chunk(ci, carry):
                    rows = pl.ds(pl.multiple_of(ci * ROW_CHUNK, ROW_CHUNK), ROW_CHUNK)
                    for sc, a in zip(a_sc, prologue(rows, row_r, vec_r, ro_r, ao_r)):
                        sc[rows, :] = a
                    return carry
                lax.fori_loop(0, TM // ROW_CHUNK, chunk, 0)

        accs = []
        for w_ref, sc, mode in zip(w_r, a_sc, modes):
            if mode == "nt2":
                accs.append(lax.dot_general(sc[...], w_ref[...], (((1,), (1,)), ((), ())),
                                            preferred_element_type=F32))
            else:
                accs.append(jnp.dot(sc[...], w_ref[...], preferred_element_type=F32))
        outs = epilogue(accs, tile_r, cv_r)
        for r, o in zip(to_r, outs):
            r[...] = o.astype(r.dtype)

    outs, side_outs = _call(
        body, side, name=name, grid=(nI, nJ), in_specs=in_specs, out_specs=out_specs, out_shape=out_shape,
        scratch=scratch, params=_params(("arbitrary", "arbitrary"), blocks, scratch, temps=[((TM, TN), F32)] * 3),
        args=[*row_ins, *vec_ins, *colvec_ins, *[w for w, _ in weights], *tile_ins])
    return outs if side is None else (outs, side_outs)


def _kloop_mm(name, S, TM, a, w3, TK, *, row_ins, vec_ins, row_outs, acc_outs, epilogue, place=_identity,
              side=None):
    _, N, Ks = w3.shape
    nb = Ks // TK
    nK = N_CHIPS * nb
    n_row, n_vec, n_ro, n_ao = len(row_ins), len(vec_ins), len(row_outs), len(acc_outs)

    in_specs = [pl.BlockSpec((TM, TK), lambda i, k: (i, k)),
                pl.BlockSpec((None, N, TK), _chip_major(nb, place))]
    blocks = [((TM, TK), BF16), ((N, TK), BF16)]
    for r in row_ins:
        in_specs.append(pl.BlockSpec((TM, r.shape[1]), lambda i, k: (i, 0)))
        blocks.append(((TM, r.shape[1]), r.dtype))
    for v in vec_ins:
        in_specs.append(pl.BlockSpec(v.shape, lambda i, k: (0, 0)))
        blocks.append((v.shape, v.dtype))
    out_shape, out_specs = [], []
    for width, dt in row_outs:
        out_shape.append(jax.ShapeDtypeStruct((S, width), dt))
        out_specs.append(pl.BlockSpec((TM, width), lambda i, k: (i, 0)))
        blocks.append(((TM, width), dt))
    for rows, width in acc_outs:
        out_shape.append(jax.ShapeDtypeStruct((rows, width), F32))
        out_specs.append(pl.BlockSpec((rows, width), lambda i, k: (0, 0)))
        blocks.append(((rows, width), F32))
    scratch = [((TM, N), F32)]

    def body(*refs):
        a_ref, w_ref = refs[0], refs[1]
        row_r = refs[2:2 + n_row]
        vec_r = refs[2 + n_row:2 + n_row + n_vec]
        pos = 2 + n_row + n_vec
        ro_r = refs[pos:pos + n_ro]
        ao_r = refs[pos + n_ro:pos + n_ro + n_ao]
        acc_sc = refs[pos + n_ro + n_ao]
        i, k = pl.program_id(0), pl.program_id(1)
        @pl.when(k == 0)
        def _():
            acc_sc[...] = jnp.zeros_like(acc_sc)
        acc_sc[...] += lax.dot_general(a_ref[...], w_ref[...], (((1,), (1,)), ((), ())),
                                       preferred_element_type=F32)

        @pl.when(k == nK - 1)
        def _():
            @pl.when(i == 0)
            def _():
                for r in ao_r:
                    r[...] = jnp.zeros_like(r)

            def chunk(ci, carry):
                rows = pl.ds(pl.multiple_of(ci * ROW_CHUNK, ROW_CHUNK), ROW_CHUNK)
                epilogue(acc_sc[rows, :], rows, row_r, vec_r, ro_r, ao_r)
                return carry
            lax.fori_loop(0, TM // ROW_CHUNK, chunk, 0)

    outs, side_outs = _call(
        body, side, name=name, grid=(S // TM, nK), in_specs=in_specs, out_specs=out_specs, out_shape=out_shape,
        scratch=scratch, params=_params(("arbitrary", "arbitrary"), blocks, scratch, temps=[((TM, N), F32)]),
        args=[a, w3, *row_ins, *vec_ins])
    return outs if side is None else (outs, side_outs)


def _tn_mm(name, a, b, TMw, TNw, TK, cols_per_chip=None, place=_identity):
    S, M = a.shape
    N = b.shape[1]
    nK = S // TK
    if cols_per_chip is None:
        out_shape = jax.ShapeDtypeStruct((M, N), F32)
        out_spec = pl.BlockSpec((TMw, TNw), lambda i, j, k: (i, j))
    else:
        nb = cols_per_chip // TNw
        out_shape = jax.ShapeDtypeStruct((N_CHIPS, M, cols_per_chip), F32)
        out_spec = pl.BlockSpec((None, TMw, TNw), lambda i, j, k: (place(j) // nb, i, place(j) % nb))

    def body(a_ref, b_ref, o_ref):
        @pl.when(pl.program_id(2) == 0)
        def _():
            o_ref[...] = jnp.zeros_like(o_ref)
        o_ref[...] += lax.dot_general(a_ref[...], b_ref[...], (((0,), (0,)), ((), ())),
                                      preferred_element_type=F32)

    blocks = [((TK, TMw), BF16), ((TK, TNw), BF16), ((TMw, TNw), F32)]
    return pl.pallas_call(
        body, name=name, grid=(M // TMw, N // TNw, nK),
        in_specs=[pl.BlockSpec((TK, TMw), lambda i, j, k: (k, i)),
                  pl.BlockSpec((TK, TNw), lambda i, j, k: (k, j))],
        out_specs=out_spec, out_shape=out_shape,
        compiler_params=_params(("arbitrary", "arbitrary", "arbitrary"), blocks,
                                temps=[((TMw, TNw), F32), ((TK, TMw), BF16)]),
    )(a, b)


def _prev_rows(TM, H, col):
    return lambda i: (jnp.maximum(i * (TM // H) - 1, 0), col)


def _next_rows(S, TM, H, col):
    return lambda i: (jnp.minimum((i + 1) * (TM // H), S // H - 1), col)


def _taps_causal(ext_ref, w_ref, K, H, TM, cs):
    acc = None
    for k in range(K):
        term = ext_ref[pl.ds(H - (K - 1) + k, TM), cs] * w_ref[pl.ds(k, 1), cs]
        acc = term if acc is None else acc + term
    return acc


def _taps_anticausal(ext_ref, w_ref, K, TM, cs):
    acc = None
    for k in range(K):
        term = ext_ref[pl.ds(K - 1 - k, TM), cs] * w_ref[pl.ds(k, 1), cs]
        acc = term if acc is None else acc + term
    return acc


def _tap_grads(ext_ref, g, K, H, TM, cs):
    return [_rsum(ext_ref[pl.ds(H - (K - 1) + k, TM), cs] * g) for k in range(K)]


def _shift_copies(ext_ref, shifted, cs):
    n = shifted.shape[1]
    for r in range(1, SUBLANES):
        shifted[r - 1] = ext_ref[pl.ds(r, n), cs]


def _rows_at(ext_ref, shifted, start, n, cs):
    q, r = divmod(start, SUBLANES)
    if r == 0:
        return ext_ref[pl.ds(start, n), cs]
    return shifted[r - 1, pl.ds(SUBLANES * q, n), :]


def _mixer_fwd(z, conv_a_w, conv_a_b, ln_g, ln_b, conv_b_w, S, TM, A, side=None):
    H = 32
    KA, KB = conv_a_w.shape[0], conv_b_w.shape[0]
    n_chunks = A // LANES
    RB = _pick(TM, (64, 32))

    def body(zc_ref, zh_ref, wa_ref, ba_ref, g_ref, b_ref, wb_ref, a1_ref, cat_ref, ext_a, ext_b, shifted):
        i = pl.program_id(0)
        live = (i > 0).astype(F32)
        zc = zc_ref[...].astype(F32)
        zh = zh_ref[...].astype(F32) * live
        ext_a[pl.ds(0, H), :] = zh[:, 0:A] * _sigmoid(zh[:, A:2 * A])
        ext_a[pl.ds(H, TM), :] = zc[:, 0:A] * _sigmoid(zc[:, A:2 * A])
        ext_b[pl.ds(0, H), :] = zh[:, 3 * A:4 * A] * zh[:, 4 * A:5 * A]
        ext_b[pl.ds(H, TM), :] = zc[:, 3 * A:4 * A] * zc[:, 4 * A:5 * A]

        def chunk(c, carry):
            cs = pl.ds(pl.multiple_of(c * LANES, LANES), LANES)
            _shift_copies(ext_a, shifted, cs)
            for r0 in range(0, TM, RB):
                acc = None
                for k in range(KA):
                    term = _rows_at(ext_a, shifted, H - (KA - 1) + k + r0, RB, cs) * wa_ref[pl.ds(k, 1), cs]
                    acc = term if acc is None else acc + term
                a1_ref[pl.ds(r0, RB), cs] = acc + ba_ref[:, cs]
            return carry
        lax.fori_loop(0, n_chunks, chunk, 0)

        a1 = a1_ref[...]
        mu = jnp.mean(a1, axis=-1, keepdims=True)
        d = a1 - mu
        var = jnp.mean(d * d, axis=-1, keepdims=True)
        a2 = d * lax.rsqrt(var + EPS) * g_ref[...] + b_ref[...]
        cat_ref[:, 0:A] = (a2 * _sigmoid(a2)).astype(BF16)
        cbc = _taps_causal(ext_b, wb_ref, KB, H, TM, slice(None))
        cat_ref[:, A:2 * A] = (zc[:, 2 * A:3 * A] * cbc).astype(BF16)

    blocks = [((TM, 5 * A), BF16), ((H, 5 * A), BF16), ((KA, A), F32), ((KB, A), F32),
              ((TM, A), F32), ((TM, 2 * A), BF16)]
    scratch = [((H + TM, A), F32), ((H + TM, A), F32), ((SUBLANES - 1, H + TM - SUBLANES, LANES), F32)]
    vec = lambda r: pl.BlockSpec((r, A), lambda i: (0, 0))
    outs, side_outs = _call(
        body, side, name="mixer_fwd", grid=(S // TM,),
        in_specs=[pl.BlockSpec((TM, 5 * A), lambda i: (i, 0)),
                  pl.BlockSpec((H, 5 * A), _prev_rows(TM, H, 0)),
                  vec(KA), vec(1), vec(1), vec(1), vec(KB)],
        out_specs=[pl.BlockSpec((TM, A), lambda i: (i, 0)), pl.BlockSpec((TM, 2 * A), lambda i: (i, 0))],
        out_shape=[jax.ShapeDtypeStruct((S, A), F32), jax.ShapeDtypeStruct((S, 2 * A), BF16)],
        scratch=scratch,
        params=_params(("arbitrary",), blocks, scratch, temps=[((TM, 5 * A), F32)] * 2 + [((TM, A), F32)] * 10),
        args=[z, z, conv_a_w, conv_a_b, ln_g, ln_b, conv_b_w])
    return outs if side is None else (outs, side_outs)


def _pair_tile(nF):
    return lambda t: (t % 2) * nF + t // 2


FFN_ROWS = 32


def _bcast_taps(w_ref, K, lanes):
    return [jnp.broadcast_to(w_ref[pl.ds(k, 1), lanes], (FFN_ROWS, LANES)) for k in range(K)]


def _ffn_act(u0, conv_w, S, TM, F, TC, side=None):
    H = 16
    K = conv_w.shape[0]
    nF = F // TC

    def body(uc_ref, uh_ref, wg_ref, wu_ref, o_ref, conv_ref, ext):
        live = (pl.program_id(0) > 0).astype(F32)
        ext[pl.ds(0, H), :] = uh_ref[...].astype(F32) * live
        ext[pl.ds(H, TM), :] = uc_ref[...].astype(F32)

        def lane_chunk(c, carry):
            lo = pl.ds(pl.multiple_of(c * LANES, LANES), LANES)
            lg, lu = lo, pl.ds(pl.multiple_of(TC + c * LANES, LANES), LANES)
            wg, wu = _bcast_taps(wg_ref, K, lo), _bcast_taps(wu_ref, K, lo)
            for r0 in range(0, TM, FFN_ROWS):
                g = u = None
                for k in range(K):
                    rows = pl.ds(H - (K - 1) + k + r0, FFN_ROWS)
                    tg, tu = ext[rows, lg] * wg[k], ext[rows, lu] * wu[k]
                    g, u = (tg, tu) if g is None else (g + tg, u + tu)
                o_ref[pl.ds(r0, FFN_ROWS), lo] = (g * _sigmoid(g) * u).astype(BF16)
                conv_ref[pl.ds(r0, FFN_ROWS), lg] = g.astype(BF16)
                conv_ref[pl.ds(r0, FFN_ROWS), lu] = u.astype(BF16)
            return carry
        lax.fori_loop(0, TC // LANES, lane_chunk, 0)

    blocks = [((TM, 2 * TC), BF16), ((H, 2 * TC), BF16), ((K, TC), F32), ((K, TC), F32), ((TM, TC), BF16),
              ((TM, 2 * TC), BF16)]
    scratch = [((H + TM, 2 * TC), F32)]
    outs, side_outs = _call(
        body, side, name="ffn_act", grid=(S // TM, nF),
        in_specs=[pl.BlockSpec((TM, 2 * TC), lambda i, j: (i, j)),
                  pl.BlockSpec((H, 2 * TC), lambda i, j: (jnp.maximum(i * (TM // H) - 1, 0), j)),
                  pl.BlockSpec((K, TC), lambda i, j: (0, j)),
                  pl.BlockSpec((K, TC), lambda i, j: (0, j + nF))],
        out_specs=[pl.BlockSpec((TM, TC), lambda i, j: (i, j)), pl.BlockSpec((TM, 2 * TC), lambda i, j: (i, j))],
        out_shape=[jax.ShapeDtypeStruct((S, F), BF16), jax.ShapeDtypeStruct((S, 2 * F), BF16)],
        scratch=scratch,
        params=_params(("arbitrary", "arbitrary"), blocks, scratch, temps=[((TM, 2 * TC), F32)]),
        args=[u0, u0, conv_w, conv_w])
    return outs if side is None else (outs, side_outs)


def _ple_loss(h2, p, target, w_gate, w_proj, b_gate, g_final, S, TM):
    D, P = h2.shape[1], p.shape[1]

    def body(h_ref, p_ref, t_ref, wg_ref, wp_ref, b_ref, g_ref,
             loss_ref, dg_ref, db_ref, dh_ref, dpre_ref, dpp_ref, hb_ref, pb_ref, pre_sc, pp_sc):
        @pl.when(pl.program_id(0) == 0)
        def _():
            loss_ref[...] = jnp.zeros_like(loss_ref)
            dg_ref[...] = jnp.zeros_like(dg_ref)
            db_ref[...] = jnp.zeros_like(db_ref)
        hb_ref[...] = h_ref[...].astype(BF16)
        pb_ref[...] = p_ref[...].astype(BF16)
        pre_sc[...] = jnp.dot(hb_ref[...], wg_ref[...], preferred_element_type=F32)
        pp_sc[...] = jnp.dot(pb_ref[...], wp_ref[...], preferred_element_type=F32)

        def chunk(ci, carry):
            rows = pl.ds(pl.multiple_of(ci * ROW_CHUNK, ROW_CHUNK), ROW_CHUNK)
            g = g_ref[...]
            gate = _sigmoid(pre_sc[rows, :] + b_ref[...])
            pp = pp_sc[rows, :]
            h = h_ref[rows, :] + pp * gate
            r = _rms_stats(h)
            n = h * r
            diff = n * g - t_ref[rows, :]
            loss_ref[...] += 0.5 * jnp.sum(jnp.mean(diff * diff, axis=-1, keepdims=True), axis=0, keepdims=True)
            dy = diff * (1.0 / D)
            dn = dy * g
            dh = r * (dn - n * jnp.mean(dn * n, axis=-1, keepdims=True))
            dh_ref[rows, :] = dh
            dg_ref[...] += _rsum(dy * n)
            dpre = dh * pp * gate * (1.0 - gate)
            dpre_ref[rows, :] = dpre.astype(BF16)
            dpp_ref[rows, :] = (dh * gate).astype(BF16)
            db_ref[...] += _rsum(dpre)
            return carry
        lax.fori_loop(0, TM // ROW_CHUNK, chunk, 0)

    row = pl.BlockSpec((TM, D), lambda i: (i, 0))
    prow = pl.BlockSpec((TM, P), lambda i: (i, 0))
    vec = pl.BlockSpec((1, D), lambda i: (0, 0))
    whole = lambda a: pl.BlockSpec(a.shape, lambda i: (0, 0))
    blocks = ([((TM, D), F32)] * 3 + [((TM, P), F32), ((D, D), BF16), ((P, D), BF16)]
              + [((TM, D), BF16)] * 3 + [((TM, P), BF16)])
    scratch = [((TM, D), F32)] * 2
    return pl.pallas_call(
        body, name="ple_loss", grid=(S // TM,),
        in_specs=[row, prow, row, whole(w_gate), whole(w_proj), vec, vec],
        out_specs=[pl.BlockSpec((1, 1), lambda i: (0, 0)), vec, vec, row, row, row, row, prow],
        out_shape=[jax.ShapeDtypeStruct((1, 1), F32), jax.ShapeDtypeStruct((1, D), F32),
                   jax.ShapeDtypeStruct((1, D), F32), jax.ShapeDtypeStruct((S, D), F32),
                   jax.ShapeDtypeStruct((S, D), BF16), jax.ShapeDtypeStruct((S, D), BF16),
                   jax.ShapeDtypeStruct((S, D), BF16), jax.ShapeDtypeStruct((S, P), BF16)],
        scratch_shapes=[pltpu.VMEM(s, d) for s, d in scratch],
        compiler_params=_params(("arbitrary",), blocks, scratch, temps=[((TM, D), F32)] * 2),
    )(h2, p, target, w_gate, w_proj, b_gate, g_final)


def _ffn_bwd(u0, conv_u0, dact, conv_w, S, TM, F, TC, side=None):
    H = FFN_ROWS
    K = conv_w.shape[0]
    nF, nI = F // TC, S // TM

    def body(xc_ref, cc_ref, cn_ref, dc_ref, dn_ref, wg_ref, wu_ref, o_ref, dwg_ref, dwu_ref, ext_d):
        i = pl.program_id(1)
        @pl.when(i == 0)
        def _():
            dwg_ref[...] = jnp.zeros_like(dwg_ref)
            dwu_ref[...] = jnp.zeros_like(dwu_ref)
        last = (i < nI - 1).astype(F32)

        def lane_chunk(c, carry):
            lo = pl.ds(pl.multiple_of(c * LANES, LANES), LANES)
            lg, lu = lo, pl.ds(pl.multiple_of(TC + c * LANES, LANES), LANES)
            wg, wu = _bcast_taps(wg_ref, K, lo), _bcast_taps(wu_ref, K, lo)
            for r0 in range(0, TM + H, FFN_ROWS):
                if r0 < TM:
                    rows = pl.ds(r0, FFN_ROWS)
                    g, u, da = cc_ref[rows, lg], cc_ref[rows, lu], dc_ref[rows, lo].astype(F32)
                else:
                    g, u, da = cn_ref[:, lg], cn_ref[:, lu], dn_ref[:, lo].astype(F32) * last
                g, u = g.astype(F32), u.astype(F32)
                s = _sigmoid(g)
                ext_d[pl.ds(r0, FFN_ROWS), lg] = da * u * s * (1.0 + g * (1.0 - s))
                ext_d[pl.ds(r0, FFN_ROWS), lu] = da * g * s
            sums_g, sums_u = [None] * K, [None] * K
            for r0 in range(0, TM, FFN_ROWS):
                xg = xc_ref[pl.ds(r0, FFN_ROWS), lg].astype(F32)
                xu = xc_ref[pl.ds(r0, FFN_ROWS), lu].astype(F32)
                g = u = None
                for k in range(K):
                    rows = pl.ds(K - 1 - k + r0, FFN_ROWS)
                    dg, du = ext_d[rows, lg], ext_d[rows, lu]
                    tg, tu = dg * wg[k], du * wu[k]
                    g, u = (tg, tu) if g is None else (g + tg, u + tu)
                    pg, pu = xg * dg, xu * du
                    sums_g[k] = pg if sums_g[k] is None else sums_g[k] + pg
                    sums_u[k] = pu if sums_u[k] is None else sums_u[k] + pu
                o_ref[pl.ds(r0, FFN_ROWS), lg] = g.astype(BF16)
                o_ref[pl.ds(r0, FFN_ROWS), lu] = u.astype(BF16)
            for k in range(K):
                dwg_ref[pl.ds(k, 1), lo] += _rsum(sums_g[k])
                dwu_ref[pl.ds(k, 1), lo] += _rsum(sums_u[k])
            return carry
        lax.fori_loop(0, TC // LANES, lane_chunk, 0)

    blocks = [((TM, 2 * TC), BF16), ((TM, 2 * TC), BF16), ((H, 2 * TC), BF16), ((TM, TC), BF16), ((H, TC), BF16),
              ((K, TC), F32), ((K, TC), F32), ((TM, 2 * TC), BF16), ((K, TC), F32), ((K, TC), F32)]
    scratch = [((TM + H, 2 * TC), F32)]
    nxt = lambda j, i: (jnp.minimum((i + 1) * (TM // H), S // H - 1), j)
    tile = pl.BlockSpec((TM, 2 * TC), lambda j, i: (i, j))
    taps_out = pl.BlockSpec((K, TC), lambda j, i: (0, j))
    outs, side_outs = _call(
        body, side, name="ffn_bwd", grid=(nF, nI),
        in_specs=[tile, tile, pl.BlockSpec((H, 2 * TC), nxt),
                  pl.BlockSpec((TM, TC), lambda j, i: (i, j)), pl.BlockSpec((H, TC), nxt),
                  pl.BlockSpec((K, TC), lambda j, i: (0, j)), pl.BlockSpec((K, TC), lambda j, i: (0, j + nF))],
        out_specs=[tile, taps_out, taps_out],
        out_shape=[jax.ShapeDtypeStruct((S, 2 * F), BF16), jax.ShapeDtypeStruct((K, F), F32),
                   jax.ShapeDtypeStruct((K, F), F32)],
        scratch=scratch,
        params=_params(("arbitrary", "arbitrary"), blocks, scratch),
        args=[u0, conv_u0, conv_u0, dact, dact, conv_w, conv_w])
    return outs if side is None else (outs, side_outs)


def _mixer_bwd_ln(dcat, a1, ln_g, ln_b, S, TM, A):
    def body(dc_ref, a1_ref, g_ref, b_ref, da1_ref, acc_ref):
        @pl.when(pl.program_id(0) == 0)
        def _():
            acc_ref[...] = jnp.zeros_like(acc_ref)
        a1 = a1_ref[...]
        g = g_ref[...]
        mu = jnp.mean(a1, axis=-1, keepdims=True)
        d = a1 - mu
        rstd = lax.rsqrt(jnp.mean(d * d, axis=-1, keepdims=True) + EPS)
        nh = d * rstd
        a2 = nh * g + b_ref[...]
        s = _sigmoid(a2)
        da2 = dc_ref[...].astype(F32) * s * (1.0 + a2 * (1.0 - s))
        dnh = da2 * g
        da1 = rstd * (dnh - jnp.mean(dnh, axis=-1, keepdims=True)
                      - nh * jnp.mean(dnh * nh, axis=-1, keepdims=True))
        da1_ref[...] = da1
        acc_ref[pl.ds(0, 1), :] += _rsum(da2 * nh)
        acc_ref[pl.ds(1, 1), :] += _rsum(da2)
        acc_ref[pl.ds(2, 1), :] += _rsum(da1)

    blocks = [((TM, A), BF16), ((TM, A), F32), ((TM, A), F32), ((4, A), F32)]
    return pl.pallas_call(
        body, name="mixer_bwd_ln", grid=(S // TM,),
        in_specs=[pl.BlockSpec((TM, A), lambda i: (i, 0)), pl.BlockSpec((TM, A), lambda i: (i, 0)),
                  pl.BlockSpec((1, A), lambda i: (0, 0)), pl.BlockSpec((1, A), lambda i: (0, 0))],
        out_specs=[pl.BlockSpec((TM, A), lambda i: (i, 0)), pl.BlockSpec((4, A), lambda i: (0, 0))],
        out_shape=[jax.ShapeDtypeStruct((S, A), F32), jax.ShapeDtypeStruct((4, A), F32)],
        compiler_params=_params(("arbitrary",), blocks, temps=[((TM, A), F32)] * 12),
    )(dcat, a1, ln_g, ln_b)


def _mixer_bwd_conv(z, dcat, da1, conv_a_w, conv_b_w, S, TM, A, side=None):
    H = 32
    KA, KB = conv_a_w.shape[0], conv_b_w.shape[0]
    nI = S // TM
    n_chunks = A // LANES
    RB = _pick(TM, (64, 32))

    def body(zc_ref, zp_ref, zn_ref, dbc_ref, dbn_ref, d1c_ref, d1n_ref, wa_ref, wb_ref,
             dz_ref, dwa_ref, dwb_ref, ext_a0, ext_d1, ext_cb, ext_dc, da0_sc, shifted_d, shifted_a):
        i = pl.program_id(0)
        @pl.when(i == 0)
        def _():
            dwa_ref[...] = jnp.zeros_like(dwa_ref)
            dwb_ref[...] = jnp.zeros_like(dwb_ref)
        first = (i > 0).astype(F32)
        last = (i < nI - 1).astype(F32)
        zc = zc_ref[...].astype(F32)
        zp = zp_ref[...].astype(F32) * first
        a_val, a_gate = zc[:, 0:A], zc[:, A:2 * A]
        b_gate, c_gate, b_h = zc[:, 2 * A:3 * A], zc[:, 3 * A:4 * A], zc[:, 4 * A:5 * A]
        sig = _sigmoid(a_gate)
        ext_a0[pl.ds(0, H), :] = zp[:, 0:A] * _sigmoid(zp[:, A:2 * A])
        ext_a0[pl.ds(H, TM), :] = a_val * sig
        ext_d1[pl.ds(0, TM), :] = d1c_ref[...]
        ext_d1[pl.ds(TM, H), :] = d1n_ref[...] * last
        ext_cb[pl.ds(0, H), :] = zp[:, 3 * A:4 * A] * zp[:, 4 * A:5 * A]
        ext_cb[pl.ds(H, TM), :] = c_gate * b_h
        dbx = dbc_ref[...].astype(F32)
        dcbc = dbx * b_gate
        ext_dc[pl.ds(0, TM), :] = dcbc
        ext_dc[pl.ds(TM, H), :] = dbn_ref[...].astype(F32) * zn_ref[...].astype(F32) * last

        def chunk(c, carry):
            cs = pl.ds(pl.multiple_of(c * LANES, LANES), LANES)
            _shift_copies(ext_d1, shifted_d, cs)
            _shift_copies(ext_a0, shifted_a, cs)
            for r0 in range(0, TM, RB):
                acc = None
                for k in range(KA):
                    term = _rows_at(ext_d1, shifted_d, KA - 1 - k + r0, RB, cs) * wa_ref[pl.ds(k, 1), cs]
                    acc = term if acc is None else acc + term
                da0_sc[pl.ds(r0, RB), cs] = acc
            for k in range(KA):
                acc = None
                for r0 in range(0, TM, RB):
                    term = (_rows_at(ext_a0, shifted_a, H - (KA - 1) + k + r0, RB, cs)
                            * ext_d1[pl.ds(r0, RB), cs])
                    acc = term if acc is None else acc + term
                dwa_ref[pl.ds(k, 1), cs] += _rsum(acc)
            return carry
        lax.fori_loop(0, n_chunks, chunk, 0)

        da0 = da0_sc[...]
        dz_ref[:, 0:A] = (da0 * sig).astype(BF16)
        dz_ref[:, A:2 * A] = (da0 * a_val * sig * (1.0 - sig)).astype(BF16)
        cbc = _taps_causal(ext_cb, wb_ref, KB, H, TM, slice(None))
        dz_ref[:, 2 * A:3 * A] = (dbx * cbc).astype(BF16)
        dcb = _taps_anticausal(ext_dc, wb_ref, KB, TM, slice(None))
        dz_ref[:, 3 * A:4 * A] = (dcb * b_h).astype(BF16)
        dz_ref[:, 4 * A:5 * A] = (dcb * c_gate).astype(BF16)
        grads = _tap_grads(ext_cb, dcbc, KB, H, TM, slice(None))
        for k in range(KB):
            dwb_ref[pl.ds(k, 1), :] += grads[k]

    blocks = [((TM, 5 * A), BF16), ((H, 5 * A), BF16), ((H, A), BF16), ((TM, A), BF16), ((H, A), BF16),
              ((TM, A), F32), ((H, A), F32), ((KA, A), F32), ((KB, A), F32),
              ((TM, 5 * A), BF16), ((KA, A), F32), ((KB, A), F32)]
    scratch = ([((H + TM, A), F32)] * 4 + [((TM, A), F32)]
               + [((SUBLANES - 1, H + TM - SUBLANES, LANES), F32)] * 2)
    vec = lambda r: pl.BlockSpec((r, A), lambda i: (0, 0))
    outs, side_outs = _call(
        body, side, name="mixer_bwd_conv", grid=(nI,),
        in_specs=[pl.BlockSpec((TM, 5 * A), lambda i: (i, 0)),
                  pl.BlockSpec((H, 5 * A), _prev_rows(TM, H, 0)),
                  pl.BlockSpec((H, A), _next_rows(S, TM, H, 2)),
                  pl.BlockSpec((TM, A), lambda i: (i, 1)),
                  pl.BlockSpec((H, A), _next_rows(S, TM, H, 1)),
                  pl.BlockSpec((TM, A), lambda i: (i, 0)),
                  pl.BlockSpec((H, A), _next_rows(S, TM, H, 0)),
                  vec(KA), vec(KB)],
        out_specs=[pl.BlockSpec((TM, 5 * A), lambda i: (i, 0)), vec(KA), vec(KB)],
        out_shape=[jax.ShapeDtypeStruct((S, 5 * A), BF16), jax.ShapeDtypeStruct((KA, A), F32),
                   jax.ShapeDtypeStruct((KB, A), F32)],
        scratch=scratch,
        params=_params(("arbitrary",), blocks, scratch, temps=[((TM, 5 * A), F32)] * 2 + [((TM, A), F32)] * 14),
        args=[z, z, z, dcat, dcat, da1, da1, conv_a_w, conv_b_w])
    return outs if side is None else (outs, side_outs)


def _row_tile(R):
    return _pick(R, (256, 128, 64, 32, 16, 8))


def _scalars(*vals):
    return jnp.stack([jnp.asarray(v, jnp.int32) for v in vals])


def _cast_into_gathered(name, w, chip):
    R, C = w.shape
    TR = _row_tile(R)

    def body(s_ref, w_ref, o_ref):
        o_ref[...] = w_ref[...].astype(BF16)

    grid_spec = pltpu.PrefetchScalarGridSpec(
        num_scalar_prefetch=1, grid=(R // TR,),
        in_specs=[pl.BlockSpec((TR, C), lambda r, s: (r, 0))],
        out_specs=pl.BlockSpec((None, TR, C), lambda r, s: (s[0], r, 0)))
    return pl.pallas_call(body, name=name, grid_spec=grid_spec,
                          out_shape=jax.ShapeDtypeStruct((N_CHIPS, R, C), BF16),
                          compiler_params=_params(("arbitrary",), [((TR, C), F32), ((TR, C), BF16)]),
                          )(_scalars(chip), *_in_hbm([w]))


def _add_pair(name, dw, recv, c, chip):
    _, _, Rh, C = dw.shape
    TR = _row_tile(Rh)

    def body(s_ref, a_ref, b_ref, o_ref, ob_ref):
        s = a_ref[...] + b_ref[...]
        ob_ref[...] = s.astype(BF16)

        @pl.when(pl.program_id(1) == s_ref[1])
        def _():
            o_ref[...] = s

    grid_spec = pltpu.PrefetchScalarGridSpec(
        num_scalar_prefetch=1, grid=(Rh // TR, N_CHIPS),
        in_specs=[pl.BlockSpec((None, None, TR, C), lambda r, k, s: (k, s[0], r, 0)),
                  pl.BlockSpec((None, TR, C), lambda r, k, s: (k, r, 0))],
        out_specs=[pl.BlockSpec((TR, C), lambda r, k, s: (r, 0)),
                   pl.BlockSpec((None, TR, C), lambda r, k, s: (k, r, 0))])
    return pl.pallas_call(body, name=name, grid_spec=grid_spec,
                          out_shape=[jax.ShapeDtypeStruct((Rh, C), F32),
                                     jax.ShapeDtypeStruct((N_CHIPS, Rh, C), BF16)],
                          compiler_params=_params(("arbitrary", "arbitrary"), [((TR, C), F32)] * 4),
                          )(_scalars(c, chip), *_in_hbm([dw, recv]))


def _add_chips(name, own, recv, c):
    Rh, C = own.shape
    TR = _row_tile(Rh)

    def body(s_ref, p_ref, r_ref, o_ref):
        o_ref[...] = ((p_ref[...] + r_ref[0].astype(F32)) + r_ref[1].astype(F32)) + r_ref[2].astype(F32)

    grid_spec = pltpu.PrefetchScalarGridSpec(
        num_scalar_prefetch=1, grid=(Rh // TR,),
        in_specs=[pl.BlockSpec((TR, C), lambda r, s: (r, 0)),
                  pl.BlockSpec((N_CHIPS - 1, TR, C), lambda r, s: (0, r, 0))],
        out_specs=pl.BlockSpec((None, TR, C), lambda r, s: (s[0], r, 0)))
    return pl.pallas_call(body, name=name, grid_spec=grid_spec,
                          out_shape=jax.ShapeDtypeStruct((2, Rh, C), F32),
                          compiler_params=_params(("arbitrary",), [((N_CHIPS + 1, TR, C), F32)]),
                          )(_scalars(c), *_in_hbm([own, recv]))


def _sum_devices(name, parts):
    _, R, C = parts.shape

    def body(p_ref, o_ref):
        acc = p_ref[0]
        for d in range(1, N_DEV):
            acc = acc + p_ref[d]
        o_ref[...] = acc

    return pl.pallas_call(body, name=name, out_shape=jax.ShapeDtypeStruct((R, C), F32),
                          in_specs=[pl.BlockSpec(memory_space=pltpu.VMEM)],
                          out_specs=pl.BlockSpec(memory_space=pltpu.VMEM))(parts)


def _adamw(name, w, g, m, v, copy_grad=False):
    R, C = w.shape
    TR = _pick(R, (128, 64, 32, 16, 8))
    c1 = 1.0 - ADAM_B1 ** ADAM_STEP
    c2 = 1.0 - ADAM_B2 ** ADAM_STEP
    n_out = 4 if copy_grad else 3

    def body(w_ref, g_ref, m_ref, v_ref, d_ref, nm_ref, nv_ref, *g_out):
        g_ = g_ref[...]
        nm = ADAM_B1 * m_ref[...] + (1.0 - ADAM_B1) * g_
        nv = ADAM_B2 * v_ref[...] + (1.0 - ADAM_B2) * (g_ * g_)
        d_ref[...] = -ADAM_LR * ((nm / c1) / (jnp.sqrt(nv / c2) + ADAM_EPS) + ADAM_WD * w_ref[...])
        nm_ref[...] = nm
        nv_ref[...] = nv
        for ref in g_out:
            ref[...] = g_

    spec = pl.BlockSpec((TR, C), lambda r: (r, 0))
    shp = jax.ShapeDtypeStruct((R, C), F32)
    return pl.pallas_call(body, name=name, grid=(R // TR,), in_specs=[spec] * 4, out_specs=[spec] * n_out,
                          out_shape=[shp] * n_out,
                          compiler_params=_params(("arbitrary",), [((TR, C), F32)] * (4 + n_out)),
                          )(*_in_hbm([w, g, m, v]))


def _place():
    x, y, c = lax.axis_index("x"), lax.axis_index("y"), lax.axis_index("c")
    others = [(1 - x, y), (x, 1 - y), (1 - x, 1 - y)]
    return x, y, c, others


def _allgather_small(name, block):
    R, C = block.shape

    def body(x_ref, out_ref, send_sems, recv_sems, local_sem):
        x, y, c, chips = _place()
        me, sibling = (x, y, c), (x, y, 1 - c)

        def rows(px, py, pc):
            return out_ref.at[4 * px + 2 * py + pc]

        def copy(k, blk, to, src=None):
            return pltpu.make_async_remote_copy(
                src_ref=rows(*blk) if src is None else src, dst_ref=rows(*blk),
                send_sem=send_sems.at[k], recv_sem=recv_sems.at[k], device_id=to, device_id_type=MESH)

        mine = pltpu.make_async_copy(x_ref, rows(*me), local_sem)
        mine.start()
        first = [copy(0, me, sibling, src=x_ref)]
        first += [copy(1 + j, me, (*chip, c), src=x_ref) for j, chip in enumerate(chips)]
        for cp in first:
            cp.start()
        passed = [copy(4 + j, (*chip, c), sibling) for j, chip in enumerate(chips)]
        for j, chip in enumerate(chips):
            copy(1 + j, (*chip, c), me).wait_recv()
            passed[j].start()
        copy(0, sibling, me).wait_recv()
        for j, chip in enumerate(chips):
            copy(4 + j, (*chip, 1 - c), me).wait_recv()
        for cp in first + passed:
            cp.wait_send()
        mine.wait()

    return pl.pallas_call(
        body, name=name, out_shape=jax.ShapeDtypeStruct((N_DEV, R, C), F32),
        in_specs=[pl.BlockSpec(memory_space=pltpu.VMEM)], out_specs=pl.BlockSpec(memory_space=pltpu.VMEM),
        scratch_shapes=[pltpu.SemaphoreType.DMA((7,)), pltpu.SemaphoreType.DMA((7,)), pltpu.SemaphoreType.DMA],
    )(block)


def _gather_side(bufs, across, within):
    def rows(ref, chip, half, piece):
        _, r0, n = piece
        return ref.at[2 * chip[0] + chip[1], pl.ds(half * (ref.shape[1] // 2) + r0, n)]

    def copies(ins, outs, send_sems, recv_sems, base):
        x, y, c, chips = _place()
        sibling = (x, y, 1 - c)
        pairs = []

        def add(k, src, dst, to, arrival):
            mk = lambda s, d, dev: pltpu.make_async_remote_copy(
                src_ref=s, dst_ref=d, send_sem=send_sems.at[base + k], recv_sem=recv_sems.at[base + k],
                device_id=dev, device_id_type=MESH)
            pairs.append((mk(src, dst, to), mk(arrival, arrival, (x, y, c))))

        for p, piece in enumerate(across):
            ref = outs[piece[0]]
            for j, chip in enumerate(chips):
                mine = rows(ref, (x, y), c, piece)
                add(3 * p + j, mine, mine, (*chip, c), rows(ref, chip, c, piece))
        for q, piece in enumerate(within):
            ref = outs[piece[0]]
            for j, chip in enumerate(chips):
                held = rows(ref, chip, c, piece)
                add(3 * (len(across) + q) + j, held, held, sibling, rows(ref, chip, 1 - c, piece))
        return pairs

    def start(*refs):
        for send, _ in copies(*refs):
            send.start()

    def wait(*refs):
        pairs = copies(*refs)
        for _, arrival in pairs:
            arrival.wait_recv()
        for send, _ in pairs:
            send.wait_send()

    return _Side(list(bufs), [jax.ShapeDtypeStruct(b.shape, b.dtype) for b in bufs],
                 3 * (len(across) + len(within)), start, wait, aliases=tuple((i, i) for i in range(len(bufs))))


def _chip_exchange(parts):
    n = len(parts)

    def copies(ins, outs, send_sems, recv_sems, base):
        x, y, c, chips = _place()
        return [pltpu.make_async_remote_copy(
            src_ref=ins[a].at[2 * chip[0] + chip[1]], dst_ref=outs[a].at[j],
            send_sem=send_sems.at[base + 3 * a + j], recv_sem=recv_sems.at[base + 3 * a + j],
            device_id=(*chip, c), device_id_type=MESH) for a in range(n) for j, chip in enumerate(chips)]

    return _Side(list(parts), [jax.ShapeDtypeStruct((N_CHIPS - 1,) + p.shape[1:], p.dtype) for p in parts],
                 3 * n, *_start_wait(copies))


def _pair_exchange(grads):
    def copies(ins, outs, send_sems, recv_sems, base):
        x, y, c, _ = _place()
        return [pltpu.make_async_remote_copy(
            src_ref=ins[a].at[:, 1 - c], dst_ref=outs[a], send_sem=send_sems.at[base + a],
            recv_sem=recv_sems.at[base + a], device_id=(x, y, 1 - c), device_id_type=MESH)
            for a in range(len(grads))]

    return _Side(list(grads), [jax.ShapeDtypeStruct((N_CHIPS,) + g.shape[2:], F32) for g in grads],
                 len(grads), *_start_wait(copies))


def _start_wait(copies):
    def start(*refs):
        for cp in copies(*refs):
            cp.start()

    def wait(*refs):
        cps = copies(*refs)
        for cp in cps:
            cp.wait_recv()
        for cp in cps:
            cp.wait_send()
    return start, wait


def _both(first, second):
    n_in, n_out = len(first.ins), len(first.out_shapes)

    def run(which):
        def go(ins, outs, send_sems, recv_sems, base):
            getattr(first, which)(ins[:n_in], outs[:n_out], send_sems, recv_sems, base)
            getattr(second, which)(ins[n_in:], outs[n_out:], send_sems, recv_sems, base + first.n_sems)
        return go

    aliases = first.aliases + tuple((a + n_in, b + n_out) for a, b in second.aliases)
    return _Side(first.ins + second.ins, first.out_shapes + second.out_shapes,
                 first.n_sems + second.n_sems, run("start"), run("wait"), aliases)


def _share_halves(halves):
    def copies(ins, outs, send_sems, recv_sems, base):
        x, y, c, _ = _place()
        pairs = []
        for a in range(len(halves)):
            mk = lambda s, d, dev, a=a: pltpu.make_async_remote_copy(
                src_ref=s, dst_ref=d, send_sem=send_sems.at[base + a], recv_sem=recv_sems.at[base + a],
                device_id=dev, device_id_type=MESH)
            theirs = outs[a].at[1 - c]
            pairs.append((mk(outs[a].at[c], outs[a].at[c], (x, y, 1 - c)), mk(theirs, theirs, (x, y, c))))
        return pairs

    def start(*refs):
        for send, _ in copies(*refs):
            send.start()

    def wait(*refs):
        pairs = copies(*refs)
        for _, arrival in pairs:
            arrival.wait_recv()
        for send, _ in pairs:
            send.wait_send()

    return _Side(list(halves), [jax.ShapeDtypeStruct(h.shape, F32) for h in halves], len(halves), start, wait,
                 aliases=tuple((i, i) for i in range(len(halves))))


def _pack(arrays):
    pieces = []
    for a in arrays:
        flat = a.reshape(-1).astype(F32)
        pieces.append(jnp.pad(flat, (0, (-flat.size) % PACK_ALIGN)))
    return jnp.concatenate(pieces).reshape(-1, LANES)


def _unpack(buf, shapes):
    lead = buf.shape[:-2]
    flat = buf.reshape(lead + (-1,))
    out, off = [], 0
    for shp in shapes:
        size = 1
        for s in shp:
            size *= s
        out.append(flat[..., off:off + size].reshape(lead + tuple(shp)))
        off += size + (-size) % PACK_ALIGN
    return out


def _gather_channels(buf, shapes):
    per_chip = _unpack(buf[0::2], shapes)
    return [jnp.transpose(a, (1, 0, 2)).reshape(a.shape[1], -1) for a in per_chip]


def _mm_tile(n, rows, limit_bytes=6 * 1024 * 1024):
    for t in (1408, 1280, 1024, 640, 512, 384, 256, 128):
        if n % t == 0 and rows * t * 2 <= limit_bytes:
            return t
    raise ValueError(f"no column tile for {n} x {rows}")


def kernel(x, p, norm_mix_g, w_in, conv_a_w, conv_a_b, ln_a_g, ln_a_b, conv_b_w, w_out, norm_ffn_g, w_up, conv_ffn_w, w_down, w_ple_gate, b_ple_gate, w_ple_proj, norm_final_g, loss_target, m_norm_mix_g, m_w_in, m_conv_a_w, m_conv_a_b, m_ln_a_g, m_ln_a_b, m_conv_b_w, m_w_out, m_norm_ffn_g, m_w_up, m_conv_ffn_w, m_w_down, m_w_ple_gate, m_b_ple_gate, m_w_ple_proj, m_norm_final_g, v_norm_mix_g, v_w_in, v_conv_a_w, v_conv_a_b, v_ln_a_g, v_ln_a_b, v_conv_b_w, v_w_out, v_norm_ffn_g, v_w_up, v_conv_ffn_w, v_w_down, v_w_ple_gate, v_b_ple_gate, v_w_ple_proj, v_norm_final_g):
    S, D = x.shape[1], x.shape[2]
    P = p.shape[3]
    A = conv_a_b.shape[1]
    F = w_down.shape[1] * N_CHIPS
    KA, KB, KF = conv_a_w.shape[1], conv_b_w.shape[1], conv_ffn_w.shape[1]
    xi, yi, ci = lax.axis_index("x"), lax.axis_index("y"), lax.axis_index("c")
    chip = 2 * xi + yi

    TM = _pick(S, (512, 256, 128))
    TL = _pick(S, (1024, 512, 256, 128))
    TE = _pick(S, (256, 128))
    TC = _pick(2 * F // N_CHIPS, (1408, 1024, 512, 256, 128))
    ffn_place = _pair_tile(F // TC)

    x2, p2, t2 = x.reshape(S, D), p.reshape(S, P), loss_target.reshape(S, D)
    gfin = norm_final_g.reshape(1, D)

    big = dict(w_in=w_in[0], w_out=w_out[0], w_up=w_up[0], w_down=w_down[0],
               w_ple_gate=w_ple_gate[0], w_ple_proj=w_ple_proj[0])
    names = list(big)
    buf = {n: _cast_into_gathered("cast_" + n, big[n], chip) for n in names}
    half = {n: big[n].shape[0] // 2 for n in names}
    up_a = half["w_up"] // 2
    (w_in3,) = _comm_only("gather_w_in_across", _gather_side([buf["w_in"]], [(0, 0, half["w_in"])], []))
    (w_in3,) = _comm_only("gather_w_in_within", _gather_side([w_in3], [], [(0, 0, half["w_in"])]))

    tap_shapes = [(KA, A // N_CHIPS), (KB, A // N_CHIPS), (KF, 2 * F // N_CHIPS)]
    taps = _allgather_small("allgather_taps", _pack([conv_a_w[0], conv_b_w[0], conv_ffn_w[0]]))
    conv_a_f, conv_b_f, conv_ffn_f = _gather_channels(taps, tap_shapes)

    def rms_prologue(rows, row_r, vec_r, ro_r, ao_r):
        h = row_r[0][rows, :]
        hn = (h * _rms_stats(h) * vec_r[0][...]).astype(BF16)
        ro_r[0][rows, :] = hn
        return [hn]

    def cast_prologue(rows, row_r, vec_r, ro_r, ao_r):
        hb = row_r[0][rows, :].astype(BF16)
        ro_r[0][rows, :] = hb
        return [hb]

    plain = lambda accs, tile_r, cv_r: [accs[0]]
    residual = lambda accs, tile_r, cv_r: [tile_r[0][...] + accs[0]]

    (z, hn1), (w_out_t, w_up_t) = _rows_mm(
        "in_proj", S, TL, 5 * A, _mm_tile(5 * A // N_CHIPS, D), row_ins=[x2], vec_ins=[norm_mix_g],
        weights=[(w_in3, "nn3")], tile_outs=[BF16], row_outs=[(D, BF16)], prologue=rms_prologue, epilogue=plain,
        side=_gather_side([buf["w_out"], buf["w_up"]], [(0, 0, half["w_out"]), (1, 0, up_a)], []))
    (a1, cat), (w_out3, w_up_t) = _mixer_fwd(
        z, conv_a_f, conv_a_b, ln_a_g, ln_a_b, conv_b_f, S, TE, A,
        side=_gather_side([w_out_t, w_up_t], [(1, up_a, half["w_up"] - up_a)],
                          [(0, 0, half["w_out"]), (1, 0, up_a)]))
    w_out_f = w_out3.reshape(2 * A, D)
    (h1,), (w_up3, w_proj_t) = _rows_mm(
        "out_proj", S, TL, D, _mm_tile(D, 2 * A), row_ins=[cat], weights=[(w_out_f, "nn2")],
        tile_ins=[x2], tile_outs=[F32], epilogue=residual,
        side=_gather_side([w_up_t, buf["w_ple_proj"]], [(1, 0, half["w_ple_proj"])],
                          [(0, up_a, half["w_up"] - up_a)]))
    (u0, hn2), (w_down_t, w_gate_t, w_proj3) = _rows_mm(
        "up_proj", S, TL, 2 * F, TC, row_ins=[h1], vec_ins=[norm_ffn_g],
        weights=[(w_up3, "nn3")], tile_outs=[BF16], row_outs=[(D, BF16)],
        prologue=rms_prologue, epilogue=plain, place=ffn_place,
        side=_gather_side([buf["w_down"], buf["w_ple_gate"], w_proj_t],
                          [(0, 0, half["w_down"]), (1, 0, half["w_ple_gate"])], [(2, 0, half["w_ple_proj"])]))
    (act, conv_u0), (w_down3, w_gate3) = _ffn_act(
        u0, conv_ffn_f, S, TM, F, TC,
        side=_gather_side([w_down_t, w_gate_t], [], [(0, 0, half["w_down"]), (1, 0, half["w_ple_gate"])]))
    w_down_f = w_down3.reshape(F, D)
    w_gate_f = w_gate3.reshape(D, D)
    w_proj_f = jnp.transpose(w_proj3, (1, 0, 2)).reshape(P, D)
    (h2,) = _rows_mm("down_proj", S, TL, D, _mm_tile(D, F), row_ins=[act], weights=[(w_down_f, "nn2")],
                     tile_ins=[h1], tile_outs=[F32], epilogue=residual)

    loss_part, g_norm_final, g_b_gate, dh3, dpre, dpp, h2b, pb = _ple_loss(
        h2, p2, t2, w_gate_f, w_proj_f, b_ple_gate, gfin, S, TE)

    TK = _pick(S, (1024, 512, 256, 128))
    wt = lambda n: _pick(n, (1408, 1280, 1024, 512, 256, 128))
    chip_sums, from_chips = {}, {}

    def to_sibling(parts):
        ns = list(parts)
        halves = [parts[n].reshape(N_CHIPS, 2, big[n].shape[0] // 2, big[n].shape[1]) for n in ns]
        return ns, halves, _pair_exchange(halves)

    def to_chips(ns, halves, from_sibling):
        sums = [_add_pair("pair_sum_" + n, h, r, ci, chip) for n, h, r in zip(ns, halves, from_sibling)]
        for n, (s, _) in zip(ns, sums):
            chip_sums[n] = s
        return ns, _chip_exchange([b for _, b in sums])

    def landed(ns, side_outs):
        for n, r in zip(ns, side_outs):
            from_chips[n] = r

    ns, halves, side = to_sibling(dict(
        w_ple_gate=_tn_mm("dw_ple_gate", h2b, dpre, wt(D), wt(D), TK),
        w_ple_proj=_tn_mm("dw_ple_proj", pb, dpp, wt(P), wt(D // N_CHIPS), TK, cols_per_chip=D // N_CHIPS)))
    (dh2,), got = _rows_mm("ple_bwd", S, TL, D, _mm_tile(D, D), row_ins=[dpre], weights=[(w_gate_f, "nt2")],
                           tile_ins=[dh3], tile_outs=[F32], epilogue=residual, side=side)
    ple_ns, ple_chips = to_chips(ns, halves, got)
    (dact, dh2b), got = _rows_mm("down_bwd", S, TL, F, _mm_tile(F, D), row_ins=[dh2], weights=[(w_down_f, "nt2")],
                                 tile_outs=[BF16], row_outs=[(D, BF16)], prologue=cast_prologue, epilogue=plain,
                                 side=ple_chips)
    landed(ple_ns, got)
    ns, halves, side = to_sibling(dict(w_down=_tn_mm("dw_down", act, dh2b, wt(F), wt(D), TK)))
    (du0, g_conv_gate, g_conv_up), got = _ffn_bwd(u0, conv_u0, dact, conv_ffn_f, S, TM, F, TC, side=side)
    down_ns, down_chips = to_chips(ns, halves, got)
    g_conv_ffn = jnp.concatenate([g_conv_gate, g_conv_up], axis=1)
    ns, halves, side = to_sibling(dict(
        w_up=_tn_mm("dw_up", hn2, du0, wt(D), TC, TK, cols_per_chip=2 * F // N_CHIPS, place=ffn_place)))

    def up_bwd_epilogue(acc, rows, row_r, vec_r, ro_r, ao_r):
        dh, dg = _rms_bwd(row_r[0][rows, :], vec_r[0][...], acc)
        dh1_ = row_r[1][rows, :] + dh
        ro_r[0][rows, :] = dh1_
        ro_r[1][rows, :] = dh1_.astype(BF16)
        ao_r[0][...] += dg

    (dh1, dh1b, g_norm_ffn), got = _kloop_mm(
        "up_bwd", S, TM, du0, w_up3, TC, row_ins=[h1, dh2], vec_ins=[norm_ffn_g],
        row_outs=[(D, F32), (D, BF16)], acc_outs=[(1, D)], epilogue=up_bwd_epilogue, place=ffn_place,
        side=_both(down_chips, side))
    landed(down_ns, got[:len(down_ns)])
    up_ns, up_chips = to_chips(ns, halves, got[len(down_ns):])
    ns, halves, side = to_sibling(dict(w_out=_tn_mm("dw_out", cat, dh1b, wt(2 * A), wt(D), TK)))
    (dcat,), got = _rows_mm("out_bwd", S, TL, 2 * A, _mm_tile(2 * A, D), row_ins=[dh1b],
                            weights=[(w_out_f, "nt2")], tile_outs=[BF16], epilogue=plain, side=side)
    out_ns, out_chips = to_chips(ns, halves, got)
    da1, ln_sums = _mixer_bwd_ln(dcat, a1, ln_a_g, ln_a_b, S, TE, A)
    (dz, g_conv_a, g_conv_b), got = _mixer_bwd_conv(z, dcat, da1, conv_a_f, conv_b_f, S, TE, A,
                                                    side=_both(up_chips, out_chips))
    landed(up_ns + out_ns, got)
    ns, halves, side = to_sibling(dict(
        w_in=_tn_mm("dw_in", hn1, dz, wt(D), wt(5 * A // N_CHIPS), TK, cols_per_chip=5 * A // N_CHIPS)))
    ns, side = to_chips(ns, halves, _comm_only("grads_exchange_pairs_in", side))

    def in_bwd_epilogue(acc, rows, row_r, vec_r, ro_r, ao_r):
        dh, dg = _rms_bwd(row_r[0][rows, :], vec_r[0][...], acc)
        ro_r[0][rows, :] = row_r[1][rows, :] + dh
        ao_r[0][...] += dg

    early = [n for n in names if n != "w_in"]
    early_halves = [_add_chips("chip_sum_" + n, chip_sums[n], from_chips[n], ci) for n in early]
    (dx, g_norm_mix), got = _kloop_mm(
        "in_bwd", S, TM, dz, w_in3, _mm_tile(5 * A // N_CHIPS, D), row_ins=[x2, dh1],
        vec_ins=[norm_mix_g], row_outs=[(D, F32)], acc_outs=[(1, D)], epilogue=in_bwd_epilogue,
        side=_both(side, _share_halves(early_halves)))
    landed(ns, got[:1])
    shared = dict(zip(early, got[1:]))
    (shared["w_in"],) = _comm_only("grads_share_w_in", _share_halves(
        [_add_chips("chip_sum_w_in", chip_sums["w_in"], from_chips["w_in"], ci)]))

    reduced = [shared[n] for n in names]
    moments = dict(w_in=(m_w_in, v_w_in), w_out=(m_w_out, v_w_out), w_up=(m_w_up, v_w_up),
                   w_down=(m_w_down, v_w_down), w_ple_gate=(m_w_ple_gate, v_w_ple_gate),
                   w_ple_proj=(m_w_ple_proj, v_w_ple_proj))
    grads, deltas, new_m, new_v = {}, {}, {}, {}
    for n, g in zip(names, reduced):
        d_, m_, v_, g = _adamw("adamw_" + n, big[n], g.reshape(big[n].shape), moments[n][0][0], moments[n][1][0],
                               copy_grad=True)
        grads[n], deltas[n], new_m[n], new_v[n] = g[None], d_[None], m_[None], v_[None]

    small = ["norm_mix_g", "conv_a_w", "conv_a_b", "ln_a_g", "ln_a_b", "conv_b_w", "norm_ffn_g",
             "conv_ffn_w", "b_ple_gate", "norm_final_g"]
    small_part = [g_norm_mix, g_conv_a, ln_sums[2:3], ln_sums[0:1], ln_sums[1:2], g_conv_b, g_norm_ffn,
                  g_conv_ffn, g_b_gate, g_norm_final]
    full_shapes = [a.shape for a in small_part]
    summed = _sum_devices("small_grads_sum", _allgather_small("allgather_small_grads", _pack(small_part)))
    small_g = dict(zip(small, _unpack(summed, full_shapes)))
    for n, width in (("conv_a_w", A), ("conv_b_w", A), ("conv_ffn_w", 2 * F)):
        small_g[n] = lax.dynamic_slice_in_dim(small_g[n], chip * (width // N_CHIPS), width // N_CHIPS, axis=1)
    small_w = dict(norm_mix_g=(norm_mix_g, m_norm_mix_g, v_norm_mix_g), conv_a_w=(conv_a_w, m_conv_a_w, v_conv_a_w),
                   conv_a_b=(conv_a_b, m_conv_a_b, v_conv_a_b), ln_a_g=(ln_a_g, m_ln_a_g, v_ln_a_g),
                   ln_a_b=(ln_a_b, m_ln_a_b, v_ln_a_b), conv_b_w=(conv_b_w, m_conv_b_w, v_conv_b_w),
                   norm_ffn_g=(norm_ffn_g, m_norm_ffn_g, v_norm_ffn_g),
                   conv_ffn_w=(conv_ffn_w, m_conv_ffn_w, v_conv_ffn_w),
                   b_ple_gate=(b_ple_gate, m_b_ple_gate, v_b_ple_gate),
                   norm_final_g=(norm_final_g, m_norm_final_g, v_norm_final_g))
    out_shapes = [small_w[n][0].shape for n in small]
    packed_g = _pack([small_g[n] for n in small])
    packed = [_pack([small_w[n][k] for n in small]) for k in range(3)]
    d_s, m_s, v_s = _adamw("adamw_small", packed[0], packed_g, packed[1], packed[2])
    for n, g, d_, m_, v_ in zip(small, _unpack(packed_g, out_shapes), _unpack(d_s, out_shapes),
                                _unpack(m_s, out_shapes), _unpack(v_s, out_shapes)):
        grads[n], deltas[n], new_m[n], new_v[n] = g, d_, m_, v_

    order = ["norm_mix_g", "w_in", "conv_a_w", "conv_a_b", "ln_a_g", "ln_a_b", "conv_b_w", "w_out", "norm_ffn_g",
             "w_up", "conv_ffn_w", "w_down", "w_ple_gate", "b_ple_gate", "w_ple_proj", "norm_final_g"]
    loss = lax.psum(loss_part[0, 0], ("x", "y", "c"))
    return (loss, dx.reshape(x.shape), *[grads[n] for n in order], *[deltas[n] for n in order],
            *[new_m[n] for n in order], *[new_v[n] for n in order])
```

```python
from typing import Callable, NamedTuple

import jax
import jax.numpy as jnp
from jax import lax
from jax.experimental import pallas as pl
from jax.experimental.pallas import tpu as pltpu

F32 = jnp.float32
BF16 = jnp.bfloat16
MESH = pl.DeviceIdType.MESH
ANY = pl.BlockSpec(memory_space=pl.ANY)

EPS = 1e-6
ADAM_LR = 0.001
ADAM_B1 = 0.9
ADAM_B2 = 0.999
ADAM_EPS = 1e-08
ADAM_WD = 0.01
ADAM_STEP = 10

N_CHIPS = 4
N_DEV = 8
LANES = 128
SUBLANES = 8
PACK_ALIGN = LANES * SUBLANES
ROW_CHUNK = 32
VMEM_CAP = 60 * 1024 * 1024
VMEM_SLACK = 6 * 1024 * 1024


def _pick(n, cands):
    for c in cands:
        if n % c == 0:
            return c
    raise ValueError(f"no tile of {cands} divides {n}")


def _nbytes(shape, dtype):
    n = 1
    for s in shape:
        if s is not None:
            n *= s
    return n * jnp.dtype(dtype).itemsize


def _params(sem, blocks, scratch=(), temps=()):
    est = (2 * sum(_nbytes(s, d) for s, d in blocks) + sum(_nbytes(s, d) for s, d in scratch)
           + sum(_nbytes(s, d) for s, d in temps))
    return pltpu.CompilerParams(dimension_semantics=sem,
                                vmem_limit_bytes=min(est + VMEM_SLACK, VMEM_CAP))


def _in_hbm(arrays):
    return [pltpu.with_memory_space_constraint(a, pltpu.HBM) for a in arrays]


def _sigmoid(x):
    return 1.0 / (1.0 + jnp.exp(-x))


def _rsum(x):
    return jnp.sum(x, axis=0, keepdims=True)


class _Side(NamedTuple):
    ins: list
    out_shapes: list
    n_sems: int
    start: Callable
    wait: Callable
    aliases: tuple = ()


def _call(body, side, *, name, grid, in_specs, out_specs, out_shape, scratch, params, args):
    vmem = [pltpu.VMEM(s, d) for s, d in scratch]
    if side is None:
        outs = pl.pallas_call(body, name=name, grid=grid, in_specs=in_specs, out_specs=out_specs,
                              out_shape=out_shape, scratch_shapes=vmem, compiler_params=params)(*args)
        return list(outs), []
    n_in, n_out, n_sc = len(in_specs), len(out_specs), len(scratch)
    ns_in, ns_out = len(side.ins), len(side.out_shapes)

    def carrier(*refs):
        pos = [0]
        def take(n):
            pos[0] += n
            return refs[pos[0] - n:pos[0]]
        ins, s_ins, outs, s_outs, scr = take(n_in), take(ns_in), take(n_out), take(ns_out), take(n_sc)
        send_sems, recv_sems = take(2)
        first = last = None
        for axis, extent in enumerate(grid):
            at_start, at_end = pl.program_id(axis) == 0, pl.program_id(axis) == extent - 1
            first = at_start if first is None else first & at_start
            last = at_end if last is None else last & at_end

        @pl.when(first)
        def _():
            side.start(s_ins, s_outs, send_sems, recv_sems, 0)
        body(*ins, *outs, *scr)

        @pl.when(last)
        def _():
            side.wait(s_ins, s_outs, send_sems, recv_sems, 0)

    outs = pl.pallas_call(
        carrier, name=name, grid=grid, in_specs=list(in_specs) + [ANY] * ns_in,
        out_specs=list(out_specs) + [ANY] * ns_out, out_shape=list(out_shape) + list(side.out_shapes),
        scratch_shapes=vmem + [pltpu.SemaphoreType.DMA((side.n_sems,)), pltpu.SemaphoreType.DMA((side.n_sems,))],
        input_output_aliases={n_in + a: n_out + b for a, b in side.aliases},
        compiler_params=params)(*args, *_in_hbm(side.ins))
    return list(outs[:n_out]), list(outs[n_out:])


def _comm_only(name, side):
    n_in = len(side.ins)

    def body(*refs):
        ins, outs = refs[:n_in], refs[n_in:n_in + len(side.out_shapes)]
        send_sems, recv_sems = refs[n_in + len(side.out_shapes):]
        side.start(ins, outs, send_sems, recv_sems, 0)
        side.wait(ins, outs, send_sems, recv_sems, 0)

    return pl.pallas_call(
        body, name=name, out_shape=list(side.out_shapes), in_specs=[ANY] * n_in,
        out_specs=[ANY] * len(side.out_shapes),
        scratch_shapes=[pltpu.SemaphoreType.DMA((side.n_sems,)), pltpu.SemaphoreType.DMA((side.n_sems,))],
        input_output_aliases=dict(side.aliases),
    )(*_in_hbm(side.ins))


def _rms_stats(x):
    return lax.rsqrt(jnp.mean(x * x, axis=-1, keepdims=True) + EPS)


def _rms_bwd(h, g, dout):
    r = _rms_stats(h)
    n = h * r
    dn = dout * g
    dh = r * (dn - n * jnp.mean(dn * n, axis=-1, keepdims=True))
    return dh, _rsum(dout * n)


def _identity(t):
    return t


def _chip_major(nb, place=_identity):
    return lambda i, j: (place(j) // nb, 0, place(j) % nb)


def _rows_mm(name, S, TM, N, TN, *, row_ins, vec_ins=(), weights, tile_ins=(), tile_outs, row_outs=(),
             prologue=None, epilogue, place=_identity, side=None):
    nI, nJ = S // TM, N // TN
    n_row, n_vec, n_w, n_tile = len(row_ins), len(vec_ins), len(weights), len(tile_ins)
    n_to, n_ro = len(tile_outs), len(row_outs)

    in_specs, blocks, scratch, ks = [], [], [], []
    for a in row_ins:
        in_specs.append(pl.BlockSpec((TM, a.shape[1]), lambda i, j: (i, 0)))
        blocks.append(((TM, a.shape[1]), a.dtype))
    for a in vec_ins:
        in_specs.append(pl.BlockSpec(a.shape, lambda i, j: (0, 0)))
        blocks.append((a.shape, a.dtype))
    for w, mode in weights:
        if mode == "nn2":
            k = w.shape[0]
            in_specs.append(pl.BlockSpec((k, TN), lambda i, j: (0, j)))
        elif mode == "nn3":
            k = w.shape[1]
            in_specs.append(pl.BlockSpec((None, k, TN), _chip_major(w.shape[2] // TN, place)))
        else:
            k = w.shape[1]
            in_specs.append(pl.BlockSpec((TN, k), lambda i, j: (j, 0)))
        ks.append(k)
        blocks.append(((k, TN), BF16))
        if prologue is not None:
            scratch.append(((TM, k), BF16))
    for a in tile_ins:
        in_specs.append(pl.BlockSpec((TM, TN), lambda i, j: (i, j)))
        blocks.append(((TM, TN), a.dtype))

    out_shape, out_specs = [], []
    for dt in tile_outs:
        out_shape.append(jax.ShapeDtypeStruct((S, N), dt))
        out_specs.append(pl.BlockSpec((TM, TN), lambda i, j: (i, j)))
        blocks.append(((TM, TN), dt))
    for width, dt in row_outs:
        out_shape.append(jax.ShapeDtypeStruct((S, width), dt))
        out_specs.append(pl.BlockSpec((TM, width), lambda i, j: (i, 0)))
        blocks.append(((TM, width), dt))

    modes = [m for _, m in weights]

    def body(*refs):
        pos = 0
        def take(n):
            nonlocal pos
            out = refs[pos:pos + n]
            pos += n
            return out
        row_r, vec_r, w_r, tile_r = take(n_row), take(n_vec), take(n_w), take(n_tile)
        to_r, ro_r, a_sc = take(n_to), take(n_ro), take(len(scratch))

        if prologue is None:
            a_sc = row_r[:n_w]
        else:
            @pl.when(pl.program_id(1) == 0)
            def _():
                def chunk(ci, carry):
                    rows = pl.ds(pl.multiple_of(ci * ROW_CHUNK, ROW_CHUNK), ROW_CHUNK)
                    for sc, a in zip(a_sc, prologue(rows, row_r, vec_r, ro_r)):
                        sc[rows, :] = a
                    return carry
                lax.fori_loop(0, TM // ROW_CHUNK, chunk, 0)

        accs = []
        for w_ref, sc, mode in zip(w_r, a_sc, modes):
            if mode == "nt2":
                accs.append(lax.dot_general(sc[...], w_ref[...], (((1,), (1,)), ((), ())),
                                            preferred_element_type=F32))
            else:
                accs.append(jnp.dot(sc[...], w_ref[...], preferred_element_type=F32))
        outs = epilogue(accs, tile_r)
        for r, o in zip(to_r, outs):
            r[...] = o.astype(r.dtype)

    outs, side_outs = _call(
        body, side, name=name, grid=(nI, nJ), in_specs=in_specs, out_specs=out_specs, out_shape=out_shape,
        scratch=scratch, params=_params(("arbitrary", "arbitrary"), blocks, scratch, temps=[((TM, TN), F32)] * 3),
        args=[*row_ins, *vec_ins, *[w for w, _ in weights], *tile_ins])
    return outs if side is None else (outs, side_outs)


def _kloop_mm(name, S, TM, a, w3, TK, *, row_ins, vec_ins, row_outs, acc_outs, epilogue, place=_identity,
              side=None):
    _, N, Ks = w3.shape
    nb = Ks // TK
    nK = N_CHIPS * nb
    n_row, n_vec, n_ro, n_ao = len(row_ins), len(vec_ins), len(row_outs), len(acc_outs)

    in_specs = [pl.BlockSpec((TM, TK), lambda i, k: (i, k)),
                pl.BlockSpec((None, N, TK), _chip_major(nb, place))]
    blocks = [((TM, TK), BF16), ((N, TK), BF16)]
    for r in row_ins:
        in_specs.append(pl.BlockSpec((TM, r.shape[1]), lambda i, k: (i, 0)))
        blocks.append(((TM, r.shape[1]), r.dtype))
    for v in vec_ins:
        in_specs.append(pl.BlockSpec(v.shape, lambda i, k: (0, 0)))
        blocks.append((v.shape, v.dtype))
    out_shape, out_specs = [], []
    for width, dt in row_outs:
        out_shape.append(jax.ShapeDtypeStruct((S, width), dt))
        out_specs.append(pl.BlockSpec((TM, width), lambda i, k: (i, 0)))
        blocks.append(((TM, width), dt))
    for rows, width in acc_outs:
        out_shape.append(jax.ShapeDtypeStruct((rows, width), F32))
        out_specs.append(pl.BlockSpec((rows, width), lambda i, k: (0, 0)))
        blocks.append(((rows, width), F32))
    scratch = [((TM, N), F32)]

    def body(*refs):
        a_ref, w_ref = refs[0], refs[1]
        row_r = refs[2:2 + n_row]
        vec_r = refs[2 + n_row:2 + n_row + n_vec]
        pos = 2 + n_row + n_vec
        ro_r = refs[pos:pos + n_ro]
        ao_r = refs[pos + n_ro:pos + n_ro + n_ao]
        acc_sc = refs[pos + n_ro + n_ao]
        i, k = pl.program_id(0), pl.program_id(1)
        @pl.when(k == 0)
        def _():
            acc_sc[...] = jnp.zeros_like(acc_sc)
        acc_sc[...] += lax.dot_general(a_ref[...], w_ref[...], (((1,), (1,)), ((), ())),
                                       preferred_element_type=F32)

        @pl.when(k == nK - 1)
        def _():
            @pl.when(i == 0)
            def _():
                for r in ao_r:
                    r[...] = jnp.zeros_like(r)

            def chunk(ci, carry):
                rows = pl.ds(pl.multiple_of(ci * ROW_CHUNK, ROW_CHUNK), ROW_CHUNK)
                epilogue(acc_sc[rows, :], rows, row_r, vec_r, ro_r, ao_r)
                return carry
            lax.fori_loop(0, TM // ROW_CHUNK, chunk, 0)

    outs, side_outs = _call(
        body, side, name=name, grid=(S // TM, nK), in_specs=in_specs, out_specs=out_specs, out_shape=out_shape,
        scratch=scratch, params=_params(("arbitrary", "arbitrary"), blocks, scratch, temps=[((TM, N), F32)]),
        args=[a, w3, *row_ins, *vec_ins])
    return outs if side is None else (outs, side_outs)


def _tn_mm(name, a, b, TMw, TNw, TK, cols_per_chip=None, place=_identity):
    S, M = a.shape
    N = b.shape[1]
    nK = S // TK
    if cols_per_chip is None:
        out_shape = jax.ShapeDtypeStruct((M, N), F32)
        out_spec = pl.BlockSpec((TMw, TNw), lambda i, j, k: (i, j))
    else:
        nb = cols_per_chip // TNw
        out_shape = jax.ShapeDtypeStruct((N_CHIPS, M, cols_per_chip), F32)
        out_spec = pl.BlockSpec((None, TMw, TNw), lambda i, j, k: (place(j) // nb, i, place(j) % nb))

    def body(a_ref, b_ref, o_ref):
        @pl.when(pl.program_id(2) == 0)
        def _():
            o_ref[...] = jnp.zeros_like(o_ref)
        o_ref[...] += lax.dot_general(a_ref[...], b_ref[...], (((0,), (0,)), ((), ())),
                                      preferred_element_type=F32)

    blocks = [((TK, TMw), BF16), ((TK, TNw), BF16), ((TMw, TNw), F32)]
    return pl.pallas_call(
        body, name=name, grid=(M // TMw, N // TNw, nK),
        in_specs=[pl.BlockSpec((TK, TMw), lambda i, j, k: (k, i)),
                  pl.BlockSpec((TK, TNw), lambda i, j, k: (k, j))],
        out_specs=out_spec, out_shape=out_shape,
        compiler_params=_params(("arbitrary", "arbitrary", "arbitrary"), blocks,
                                temps=[((TMw, TNw), F32), ((TK, TMw), BF16)]),
    )(a, b)


def _prev_rows(TM, H, col):
    return lambda i: (jnp.maximum(i * (TM // H) - 1, 0), col)


def _next_rows(S, TM, H, col):
    return lambda i: (jnp.minimum((i + 1) * (TM // H), S // H - 1), col)


def _taps_causal(ext_ref, w_ref, K, H, TM, cs):
    acc = None
    for k in range(K):
        term = ext_ref[pl.ds(H - (K - 1) + k, TM), cs] * w_ref[pl.ds(k, 1), cs]
        acc = term if acc is None else acc + term
    return acc


def _taps_anticausal(ext_ref, w_ref, K, TM, cs):
    acc = None
    for k in range(K):
        term = ext_ref[pl.ds(K - 1 - k, TM), cs] * w_ref[pl.ds(k, 1), cs]
        acc = term if acc is None else acc + term
    return acc


def _tap_grads(ext_ref, g, K, H, TM, cs):
    return [_rsum(ext_ref[pl.ds(H - (K - 1) + k, TM), cs] * g) for k in range(K)]


def _shift_copies(ext_ref, shifted, cs):
    n = shifted.shape[1]
    for r in range(1, SUBLANES):
        shifted[r - 1] = ext_ref[pl.ds(r, n), cs]


def _rows_at(ext_ref, shifted, start, n, cs):
    q, r = divmod(start, SUBLANES)
    if r == 0:
        return ext_ref[pl.ds(start, n), cs]
    return shifted[r - 1, pl.ds(SUBLANES * q, n), :]


def _mixer_fwd(z, conv_a_w, conv_a_b, ln_g, ln_b, conv_b_w, S, TM, A, side=None):
    H = 32
    KA, KB = conv_a_w.shape[0], conv_b_w.shape[0]
    n_chunks = A // LANES
    RB = _pick(TM, (64, 32))

    def body(zc_ref, zh_ref, wa_ref, ba_ref, g_ref, b_ref, wb_ref, a1_ref, cat_ref, ext_a, ext_b, shifted):
        i = pl.program_id(0)
        live = (i > 0).astype(F32)
        zc = zc_ref[...].astype(F32)
        zh = zh_ref[...].astype(F32) * live
        ext_a[pl.ds(0, H), :] = zh[:, 0:A] * _sigmoid(zh[:, A:2 * A])
        ext_a[pl.ds(H, TM), :] = zc[:, 0:A] * _sigmoid(zc[:, A:2 * A])
        ext_b[pl.ds(0, H), :] = zh[:, 3 * A:4 * A] * zh[:, 4 * A:5 * A]
        ext_b[pl.ds(H, TM), :] = zc[:, 3 * A:4 * A] * zc[:, 4 * A:5 * A]

        def chunk(c, carry):
            cs = pl.ds(pl.multiple_of(c * LANES, LANES), LANES)
            _shift_copies(ext_a, shifted, cs)
            for r0 in range(0, TM, RB):
                acc = None
                for k in range(KA):
                    term = _rows_at(ext_a, shifted, H - (KA - 1) + k + r0, RB, cs) * wa_ref[pl.ds(k, 1), cs]
                    acc = term if acc is None else acc + term
                a1_ref[pl.ds(r0, RB), cs] = acc + ba_ref[:, cs]
            return carry
        lax.fori_loop(0, n_chunks, chunk, 0)

        a1 = a1_ref[...]
        mu = jnp.mean(a1, axis=-1, keepdims=True)
        d = a1 - mu
        var = jnp.mean(d * d, axis=-1, keepdims=True)
        a2 = d * lax.rsqrt(var + EPS) * g_ref[...] + b_ref[...]
        cat_ref[:, 0:A] = (a2 * _sigmoid(a2)).astype(BF16)
        cbc = _taps_causal(ext_b, wb_ref, KB, H, TM, slice(None))
        cat_ref[:, A:2 * A] = (zc[:, 2 * A:3 * A] * cbc).astype(BF16)

    blocks = [((TM, 5 * A), BF16), ((H, 5 * A), BF16), ((KA, A), F32), ((KB, A), F32),
              ((TM, A), F32), ((TM, 2 * A), BF16)]
    scratch = [((H + TM, A), F32), ((H + TM, A), F32), ((SUBLANES - 1, H + TM - SUBLANES, LANES), F32)]
    vec = lambda r: pl.BlockSpec((r, A), lambda i: (0, 0))
    outs, side_outs = _call(
        body, side, name="mixer_fwd", grid=(S // TM,),
        in_specs=[pl.BlockSpec((TM, 5 * A), lambda i: (i, 0)),
                  pl.BlockSpec((H, 5 * A), _prev_rows(TM, H, 0)),
                  vec(KA), vec(1), vec(1), vec(1), vec(KB)],
        out_specs=[pl.BlockSpec((TM, A), lambda i: (i, 0)), pl.BlockSpec((TM, 2 * A), lambda i: (i, 0))],
        out_shape=[jax.ShapeDtypeStruct((S, A), F32), jax.ShapeDtypeStruct((S, 2 * A), BF16)],
        scratch=scratch,
        params=_params(("arbitrary",), blocks, scratch, temps=[((TM, 5 * A), F32)] * 2 + [((TM, A), F32)] * 10),
        args=[z, z, conv_a_w, conv_a_b, ln_g, ln_b, conv_b_w])
    return outs if side is None else (outs, side_outs)


def _pair_tile(nF):
    return lambda t: (t % 2) * nF + t // 2


FFN_ROWS = 32


def _bcast_taps(w_ref, K, lanes):
    return [jnp.broadcast_to(w_ref[pl.ds(k, 1), lanes], (FFN_ROWS, LANES)) for k in range(K)]


def _ffn_act(u0, conv_w, S, TM, F, TC, side=None):
    H = 16
    K = conv_w.shape[0]
    nF = F // TC

    def body(uc_ref, uh_ref, wg_ref, wu_ref, o_ref, conv_ref, ext):
        live = (pl.program_id(0) > 0).astype(F32)
        ext[pl.ds(0, H), :] = uh_ref[...].astype(F32) * live
        ext[pl.ds(H, TM), :] = uc_ref[...].astype(F32)

        def lane_chunk(c, carry):
            lo = pl.ds(pl.multiple_of(c * LANES, LANES), LANES)
            lg, lu = lo, pl.ds(pl.multiple_of(TC + c * LANES, LANES), LANES)
            wg, wu = _bcast_taps(wg_ref, K, lo), _bcast_taps(wu_ref, K, lo)
            for r0 in range(0, TM, FFN_ROWS):
                g = u = None
                for k in range(K):
                    rows = pl.ds(H - (K - 1) + k + r0, FFN_ROWS)
                    tg, tu = ext[rows, lg] * wg[k], ext[rows, lu] * wu[k]
                    g, u = (tg, tu) if g is None else (g + tg, u + tu)
                o_ref[pl.ds(r0, FFN_ROWS), lo] = (g * _sigmoid(g) * u).astype(BF16)
                conv_ref[pl.ds(r0, FFN_ROWS), lg] = g.astype(BF16)
                conv_ref[pl.ds(r0, FFN_ROWS), lu] = u.astype(BF16)
            return carry
        lax.fori_loop(0, TC // LANES, lane_chunk, 0)

    blocks = [((TM, 2 * TC), BF16), ((H, 2 * TC), BF16), ((K, TC), F32), ((K, TC), F32), ((TM, TC), BF16),
              ((TM, 2 * TC), BF16)]
    scratch = [((H + TM, 2 * TC), F32)]
    outs, side_outs = _call(
        body, side, name="ffn_act", grid=(S // TM, nF),
        in_specs=[pl.BlockSpec((TM, 2 * TC), lambda i, j: (i, j)),
                  pl.BlockSpec((H, 2 * TC), lambda i, j: (jnp.maximum(i * (TM // H) - 1, 0), j)),
                  pl.BlockSpec((K, TC), lambda i, j: (0, j)),
                  pl.BlockSpec((K, TC), lambda i, j: (0, j + nF))],
        out_specs=[pl.BlockSpec((TM, TC), lambda i, j: (i, j)), pl.BlockSpec((TM, 2 * TC), lambda i, j: (i, j))],
        out_shape=[jax.ShapeDtypeStruct((S, F), BF16), jax.ShapeDtypeStruct((S, 2 * F), BF16)],
        scratch=scratch,
        params=_params(("arbitrary", "arbitrary"), blocks, scratch, temps=[((TM, 2 * TC), F32)]),
        args=[u0, u0, conv_w, conv_w])
    return outs if side is None else (outs, side_outs)


def _ple_loss(h2, p, target, w_gate, w_proj, b_gate, g_final, S, TM):
    D, P = h2.shape[1], p.shape[1]

    def body(h_ref, p_ref, t_ref, wg_ref, wp_ref, b_ref, g_ref,
             loss_ref, dg_ref, db_ref, dh_ref, dpre_ref, dpp_ref, hb_ref, pb_ref, pre_sc, pp_sc):
        @pl.when(pl.program_id(0) == 0)
        def _():
            loss_ref[...] = jnp.zeros_like(loss_ref)
            dg_ref[...] = jnp.zeros_like(dg_ref)
            db_ref[...] = jnp.zeros_like(db_ref)
        hb_ref[...] = h_ref[...].astype(BF16)
        pb_ref[...] = p_ref[...].astype(BF16)
        pre_sc[...] = jnp.dot(hb_ref[...], wg_ref[...], preferred_element_type=F32)
        pp_sc[...] = jnp.dot(pb_ref[...], wp_ref[...], preferred_element_type=F32)

        def chunk(ci, carry):
            rows = pl.ds(pl.multiple_of(ci * ROW_CHUNK, ROW_CHUNK), ROW_CHUNK)
            g = g_ref[...]
            gate = _sigmoid(pre_sc[rows, :] + b_ref[...])
            pp = pp_sc[rows, :]
            h = h_ref[rows, :] + pp * gate
            r = _rms_stats(h)
            n = h * r
            diff = n * g - t_ref[rows, :]
            loss_ref[...] += 0.5 * jnp.sum(jnp.mean(diff * diff, axis=-1, keepdims=True), axis=0, keepdims=True)
            dy = diff * (1.0 / D)
            dn = dy * g
            dh = r * (dn - n * jnp.mean(dn * n, axis=-1, keepdims=True))
            dh_ref[rows, :] = dh
            dg_ref[...] += _rsum(dy * n)
            dpre = dh * pp * gate * (1.0 - gate)
            dpre_ref[rows, :] = dpre.astype(BF16)
            dpp_ref[rows, :] = (dh * gate).astype(BF16)
            db_ref[...] += _rsum(dpre)
            return carry
        lax.fori_loop(0, TM // ROW_CHUNK, chunk, 0)

    row = pl.BlockSpec((TM, D), lambda i: (i, 0))
    prow = pl.BlockSpec((TM, P), lambda i: (i, 0))
    vec = pl.BlockSpec((1, D), lambda i: (0, 0))
    whole = lambda a: pl.BlockSpec(a.shape, lambda i: (0, 0))
    blocks = ([((TM, D), F32)] * 3 + [((TM, P), F32), ((D, D), BF16), ((P, D), BF16)]
              + [((TM, D), BF16)] * 3 + [((TM, P), BF16)])
    scratch = [((TM, D), F32)] * 2
    return pl.pallas_call(
        body, name="ple_loss", grid=(S // TM,),
        in_specs=[row, prow, row, whole(w_gate), whole(w_proj), vec, vec],
        out_specs=[pl.BlockSpec((1, 1), lambda i: (0, 0)), vec, vec, row, row, row, row, prow],
        out_shape=[jax.ShapeDtypeStruct((1, 1), F32), jax.ShapeDtypeStruct((1, D), F32),
                   jax.ShapeDtypeStruct((1, D), F32), jax.ShapeDtypeStruct((S, D), F32),
                   jax.ShapeDtypeStruct((S, D), BF16), jax.ShapeDtypeStruct((S, D), BF16),
                   jax.ShapeDtypeStruct((S, D), BF16), jax.ShapeDtypeStruct((S, P), BF16)],
        scratch_shapes=[pltpu.VMEM(s, d) for s, d in scratch],
        compiler_params=_params(("arbitrary",), blocks, scratch, temps=[((TM, D), F32)] * 2),
    )(h2, p, target, w_gate, w_proj, b_gate, g_final)


def _ffn_bwd(u0, conv_u0, dact, conv_w, S, TM, F, TC, side=None):
    H = FFN_ROWS
    K = conv_w.shape[0]
    nF, nI = F // TC, S // TM

    def body(xc_ref, cc_ref, cn_ref, dc_ref, dn_ref, wg_ref, wu_ref, o_ref, dwg_ref, dwu_ref, ext_d):
        i = pl.program_id(1)
        @pl.when(i == 0)
        def _():
            dwg_ref[...] = jnp.zeros_like(dwg_ref)
            dwu_ref[...] = jnp.zeros_like(dwu_ref)
        last = (i < nI - 1).astype(F32)

        def lane_chunk(c, carry):
            lo = pl.ds(pl.multiple_of(c * LANES, LANES), LANES)
            lg, lu = lo, pl.ds(pl.multiple_of(TC + c * LANES, LANES), LANES)
            wg, wu = _bcast_taps(wg_ref, K, lo), _bcast_taps(wu_ref, K, lo)
            for r0 in range(0, TM + H, FFN_ROWS):
                if r0 < TM:
                    rows = pl.ds(r0, FFN_ROWS)
                    g, u, da = cc_ref[rows, lg], cc_ref[rows, lu], dc_ref[rows, lo].astype(F32)
                else:
                    g, u, da = cn_ref[:, lg], cn_ref[:, lu], dn_ref[:, lo].astype(F32) * last
                g, u = g.astype(F32), u.astype(F32)
                s = _sigmoid(g)
                ext_d[pl.ds(r0, FFN_ROWS), lg] = da * u * s * (1.0 + g * (1.0 - s))
                ext_d[pl.ds(r0, FFN_ROWS), lu] = da * g * s
            sums_g, sums_u = [None] * K, [None] * K
            for r0 in range(0, TM, FFN_ROWS):
                xg = xc_ref[pl.ds(r0, FFN_ROWS), lg].astype(F32)
                xu = xc_ref[pl.ds(r0, FFN_ROWS), lu].astype(F32)
                g = u = None
                for k in range(K):
                    rows = pl.ds(K - 1 - k + r0, FFN_ROWS)
                    dg, du = ext_d[rows, lg], ext_d[rows, lu]
                    tg, tu = dg * wg[k], du * wu[k]
                    g, u = (tg, tu) if g is None else (g + tg, u + tu)
                    pg, pu = xg * dg, xu * du
                    sums_g[k] = pg if sums_g[k] is None else sums_g[k] + pg
                    sums_u[k] = pu if sums_u[k] is None else sums_u[k] + pu
                o_ref[pl.ds(r0, FFN_ROWS), lg] = g.astype(BF16)
                o_ref[pl.ds(r0, FFN_ROWS), lu] = u.astype(BF16)
            for k in range(K):
                dwg_ref[pl.ds(k, 1), lo] += _rsum(sums_g[k])
                dwu_ref[pl.ds(k, 1), lo] += _rsum(sums_u[k])
            return carry
        lax.fori_loop(0, TC // LANES, lane_chunk, 0)

    blocks = [((TM, 2 * TC), BF16), ((TM, 2 * TC), BF16), ((H, 2 * TC), BF16), ((TM, TC), BF16), ((H, TC), BF16),
              ((K, TC), F32), ((K, TC), F32), ((TM, 2 * TC), BF16), ((K, TC), F32), ((K, TC), F32)]
    scratch = [((TM + H, 2 * TC), F32)]
    nxt = lambda j, i: (jnp.minimum((i + 1) * (TM // H), S // H - 1), j)
    tile = pl.BlockSpec((TM, 2 * TC), lambda j, i: (i, j))
    taps_out = pl.BlockSpec((K, TC), lambda j, i: (0, j))
    outs, side_outs = _call(
        body, side, name="ffn_bwd", grid=(nF, nI),
        in_specs=[tile, tile, pl.BlockSpec((H, 2 * TC), nxt),
                  pl.BlockSpec((TM, TC), lambda j, i: (i, j)), pl.BlockSpec((H, TC), nxt),
                  pl.BlockSpec((K, TC), lambda j, i: (0, j)), pl.BlockSpec((K, TC), lambda j, i: (0, j + nF))],
        out_specs=[tile, taps_out, taps_out],
        out_shape=[jax.ShapeDtypeStruct((S, 2 * F), BF16), jax.ShapeDtypeStruct((K, F), F32),
                   jax.ShapeDtypeStruct((K, F), F32)],
        scratch=scratch,
        params=_params(("arbitrary", "arbitrary"), blocks, scratch),
        args=[u0, conv_u0, conv_u0, dact, dact, conv_w, conv_w])
    return outs if side is None else (outs, side_outs)


def _mixer_bwd_ln(dcat, a1, ln_g, ln_b, S, TM, A):
    def body(dc_ref, a1_ref, g_ref, b_ref, da1_ref, acc_ref):
        @pl.when(pl.program_id(0) == 0)
        def _():
            acc_ref[...] = jnp.zeros_like(acc_ref)
        a1 = a1_ref[...]
        g = g_ref[...]
        mu = jnp.mean(a1, axis=-1, keepdims=True)
        d = a1 - mu
        rstd = lax.rsqrt(jnp.mean(d * d, axis=-1, keepdims=True) + EPS)
        nh = d * rstd
        a2 = nh * g + b_ref[...]
        s = _sigmoid(a2)
        da2 = dc_ref[...].astype(F32) * s * (1.0 + a2 * (1.0 - s))
        dnh = da2 * g
        da1 = rstd * (dnh - jnp.mean(dnh, axis=-1, keepdims=True)
                      - nh * jnp.mean(dnh * nh, axis=-1, keepdims=True))
        da1_ref[...] = da1
        acc_ref[pl.ds(0, 1), :] += _rsum(da2 * nh)
        acc_ref[pl.ds(1, 1), :] += _rsum(da2)
        acc_ref[pl.ds(2, 1), :] += _rsum(da1)

    blocks = [((TM, A), BF16), ((TM, A), F32), ((TM, A), F32), ((4, A), F32)]
    return pl.pallas_call(
        body, name="mixer_bwd_ln", grid=(S // TM,),
        in_specs=[pl.BlockSpec((TM, A), lambda i: (i, 0)), pl.BlockSpec((TM, A), lambda i: (i, 0)),
                  pl.BlockSpec((1, A), lambda i: (0, 0)), pl.BlockSpec((1, A), lambda i: (0, 0))],
        out_specs=[pl.BlockSpec((TM, A), lambda i: (i, 0)), pl.BlockSpec((4, A), lambda i: (0, 0))],
        out_shape=[jax.ShapeDtypeStruct((S, A), F32), jax.ShapeDtypeStruct((4, A), F32)],
        compiler_params=_params(("arbitrary",), blocks, temps=[((TM, A), F32)] * 12),
    )(dcat, a1, ln_g, ln_b)


def _mixer_bwd_conv(z, dcat, da1, conv_a_w, conv_b_w, S, TM, A, side=None):
    H = 32
    KA, KB = conv_a_w.shape[0], conv_b_w.shape[0]
    nI = S // TM
    n_chunks = A // LANES
    RB = _pick(TM, (64, 32))

    def body(zc_ref, zp_ref, zn_ref, dbc_ref, dbn_ref, d1c_ref, d1n_ref, wa_ref, wb_ref,
             dz_ref, dwa_ref, dwb_ref, ext_a0, ext_d1, ext_cb, ext_dc, da0_sc, shifted_d, shifted_a):
        i = pl.program_id(0)
        @pl.when(i == 0)
        def _():
            dwa_ref[...] = jnp.zeros_like(dwa_ref)
            dwb_ref[...] = jnp.zeros_like(dwb_ref)
        first = (i > 0).astype(F32)
        last = (i < nI - 1).astype(F32)
        zc = zc_ref[...].astype(F32)
        zp = zp_ref[...].astype(F32) * first
        a_val, a_gate = zc[:, 0:A], zc[:, A:2 * A]
        b_gate, c_gate, b_h = zc[:, 2 * A:3 * A], zc[:, 3 * A:4 * A], zc[:, 4 * A:5 * A]
        sig = _sigmoid(a_gate)
        ext_a0[pl.ds(0, H), :] = zp[:, 0:A] * _sigmoid(zp[:, A:2 * A])
        ext_a0[pl.ds(H, TM), :] = a_val * sig
        ext_d1[pl.ds(0, TM), :] = d1c_ref[...]
        ext_d1[pl.ds(TM, H), :] = d1n_ref[...] * last
        ext_cb[pl.ds(0, H), :] = zp[:, 3 * A:4 * A] * zp[:, 4 * A:5 * A]
        ext_cb[pl.ds(H, TM), :] = c_gate * b_h
        dbx = dbc_ref[...].astype(F32)
        dcbc = dbx * b_gate
        ext_dc[pl.ds(0, TM), :] = dcbc
        ext_dc[pl.ds(TM, H), :] = dbn_ref[...].astype(F32) * zn_ref[...].astype(F32) * last

        def chunk(c, carry):
            cs = pl.ds(pl.multiple_of(c * LANES, LANES), LANES)
            _shift_copies(ext_d1, shifted_d, cs)
            _shift_copies(ext_a0, shifted_a, cs)
            for r0 in range(0, TM, RB):
                acc = None
                for k in range(KA):
                    term = _rows_at(ext_d1, shifted_d, KA - 1 - k + r0, RB, cs) * wa_ref[pl.ds(k, 1), cs]
                    acc = term if acc is None else acc + term
                da0_sc[pl.ds(r0, RB), cs] = acc
            for k in range(KA):
                acc = None
                for r0 in range(0, TM, RB):
                    term = (_rows_at(ext_a0, shifted_a, H - (KA - 1) + k + r0, RB, cs)
                            * ext_d1[pl.ds(r0, RB), cs])
                    acc = term if acc is None else acc + term
                dwa_ref[pl.ds(k, 1), cs] += _rsum(acc)
            return carry
        lax.fori_loop(0, n_chunks, chunk, 0)

        da0 = da0_sc[...]
        dz_ref[:, 0:A] = (da0 * sig).astype(BF16)
        dz_ref[:, A:2 * A] = (da0 * a_val * sig * (1.0 - sig)).astype(BF16)
        cbc = _taps_causal(ext_cb, wb_ref, KB, H, TM, slice(None))
        dz_ref[:, 2 * A:3 * A] = (dbx * cbc).astype(BF16)
        dcb = _taps_anticausal(ext_dc, wb_ref, KB, TM, slice(None))
        dz_ref[:, 3 * A:4 * A] = (dcb * b_h).astype(BF16)
        dz_ref[:, 4 * A:5 * A] = (dcb * c_gate).astype(BF16)
        grads = _tap_grads(ext_cb, dcbc, KB, H, TM, slice(None))
        for k in range(KB):
            dwb_ref[pl.ds(k, 1), :] += grads[k]

    blocks = [((TM, 5 * A), BF16), ((H, 5 * A), BF16), ((H, A), BF16), ((TM, A), BF16), ((H, A), BF16),
              ((TM, A), F32), ((H, A), F32), ((KA, A), F32), ((KB, A), F32),
              ((TM, 5 * A), BF16), ((KA, A), F32), ((KB, A), F32)]
    scratch = ([((H + TM, A), F32)] * 4 + [((TM, A), F32)]
               + [((SUBLANES - 1, H + TM - SUBLANES, LANES), F32)] * 2)
    vec = lambda r: pl.BlockSpec((r, A), lambda i: (0, 0))
    outs, side_outs = _call(
        body, side, name="mixer_bwd_conv", grid=(nI,),
        in_specs=[pl.BlockSpec((TM, 5 * A), lambda i: (i, 0)),
                  pl.BlockSpec((H, 5 * A), _prev_rows(TM, H, 0)),
                  pl.BlockSpec((H, A), _next_rows(S, TM, H, 2)),
                  pl.BlockSpec((TM, A), lambda i: (i, 1)),
                  pl.BlockSpec((H, A), _next_rows(S, TM, H, 1)),
                  pl.BlockSpec((TM, A), lambda i: (i, 0)),
                  pl.BlockSpec((H, A), _next_rows(S, TM, H, 0)),
                  vec(KA), vec(KB)],
        out_specs=[pl.BlockSpec((TM, 5 * A), lambda i: (i, 0)), vec(KA), vec(KB)],
        out_shape=[jax.ShapeDtypeStruct((S, 5 * A), BF16), jax.ShapeDtypeStruct((KA, A), F32),
                   jax.ShapeDtypeStruct((KB, A), F32)],
        scratch=scratch,
        params=_params(("arbitrary",), blocks, scratch, temps=[((TM, 5 * A), F32)] * 2 + [((TM, A), F32)] * 14),
        args=[z, z, z, dcat, dcat, da1, da1, conv_a_w, conv_b_w])
    return outs if side is None else (outs, side_outs)


def _row_tile(R):
    return _pick(R, (256, 128, 64, 32, 16, 8))


def _scalars(*vals):
    return jnp.stack([jnp.asarray(v, jnp.int32) for v in vals])


def _cast_into_gathered(name, w, chip):
    R, C = w.shape
    TR = _row_tile(R)

    def body(s_ref, w_ref, o_ref):
        o_ref[...] = w_ref[...].astype(BF16)

    grid_spec = pltpu.PrefetchScalarGridSpec(
        num_scalar_prefetch=1, grid=(R // TR,),
        in_specs=[pl.BlockSpec((TR, C), lambda r, s: (r, 0))],
        out_specs=pl.BlockSpec((None, TR, C), lambda r, s: (s[0], r, 0)))
    return pl.pallas_call(body, name=name, grid_spec=grid_spec,
                          out_shape=jax.ShapeDtypeStruct((N_CHIPS, R, C), BF16),
                          compiler_params=_params(("arbitrary",), [((TR, C), F32), ((TR, C), BF16)]),
                          )(_scalars(chip), *_in_hbm([w]))


def _add_pair(name, dw, recv, c, chip):
    _, _, Rh, C = dw.shape
    TR = _row_tile(Rh)

    def body(s_ref, a_ref, b_ref, o_ref, ob_ref):
        s = a_ref[...] + b_ref[...]
        ob_ref[...] = s.astype(BF16)

        @pl.when(pl.program_id(1) == s_ref[1])
        def _():
            o_ref[...] = s

    grid_spec = pltpu.PrefetchScalarGridSpec(
        num_scalar_prefetch=1, grid=(Rh // TR, N_CHIPS),
        in_specs=[pl.BlockSpec((None, None, TR, C), lambda r, k, s: (k, s[0], r, 0)),
                  pl.BlockSpec((None, TR, C), lambda r, k, s: (k, r, 0))],
        out_specs=[pl.BlockSpec((TR, C), lambda r, k, s: (r, 0)),
                   pl.BlockSpec((None, TR, C), lambda r, k, s: (k, r, 0))])
    return pl.pallas_call(body, name=name, grid_spec=grid_spec,
                          out_shape=[jax.ShapeDtypeStruct((Rh, C), F32),
                                     jax.ShapeDtypeStruct((N_CHIPS, Rh, C), BF16)],
                          compiler_params=_params(("arbitrary", "arbitrary"), [((TR, C), F32)] * 4),
                          )(_scalars(c, chip), *_in_hbm([dw, recv]))


def _add_chips(name, own, recv, c):
    Rh, C = own.shape
    TR = _row_tile(Rh)

    def body(s_ref, p_ref, r_ref, o_ref):
        o_ref[...] = ((p_ref[...] + r_ref[0].astype(F32)) + r_ref[1].astype(F32)) + r_ref[2].astype(F32)

    grid_spec = pltpu.PrefetchScalarGridSpec(
        num_scalar_prefetch=1, grid=(Rh // TR,),
        in_specs=[pl.BlockSpec((TR, C), lambda r, s: (r, 0)),
                  pl.BlockSpec((N_CHIPS - 1, TR, C), lambda r, s: (0, r, 0))],
        out_specs=pl.BlockSpec((None, TR, C), lambda r, s: (s[0], r, 0)))
    return pl.pallas_call(body, name=name, grid_spec=grid_spec,
                          out_shape=jax.ShapeDtypeStruct((2, Rh, C), F32),
                          compiler_params=_params(("arbitrary",), [((N_CHIPS + 1, TR, C), F32)]),
                          )(_scalars(c), *_in_hbm([own, recv]))


def _sum_devices(name, parts):
    _, R, C = parts.shape

    def body(p_ref, o_ref):
        acc = p_ref[0]
        for d in range(1, N_DEV):
            acc = acc + p_ref[d]
        o_ref[...] = acc

    return pl.pallas_call(body, name=name, out_shape=jax.ShapeDtypeStruct((R, C), F32),
                          in_specs=[pl.BlockSpec(memory_space=pltpu.VMEM)],
                          out_specs=pl.BlockSpec(memory_space=pltpu.VMEM))(parts)


def _adamw(name, w, g, m, v, copy_grad=False):
    R, C = w.shape
    TR = _pick(R, (128, 64, 32, 16, 8))
    c1 = 1.0 - ADAM_B1 ** ADAM_STEP
    c2 = 1.0 - ADAM_B2 ** ADAM_STEP
    n_out = 4 if copy_grad else 3

    def body(w_ref, g_ref, m_ref, v_ref, d_ref, nm_ref, nv_ref, *g_out):
        g_ = g_ref[...]
        nm = ADAM_B1 * m_ref[...] + (1.0 - ADAM_B1) * g_
        nv = ADAM_B2 * v_ref[...] + (1.0 - ADAM_B2) * (g_ * g_)
        d_ref[...] = -ADAM_LR * ((nm / c1) / (jnp.sqrt(nv / c2) + ADAM_EPS) + ADAM_WD * w_ref[...])
        nm_ref[...] = nm
        nv_ref[...] = nv
        for ref in g_out:
            ref[...] = g_

    spec = pl.BlockSpec((TR, C), lambda r: (r, 0))
    shp = jax.ShapeDtypeStruct((R, C), F32)
    return pl.pallas_call(body, name=name, grid=(R // TR,), in_specs=[spec] * 4, out_specs=[spec] * n_out,
                          out_shape=[shp] * n_out,
                          compiler_params=_params(("arbitrary",), [((TR, C), F32)] * (4 + n_out)),
                          )(*_in_hbm([w, g, m, v]))


def _place():
    x, y, c = lax.axis_index("x"), lax.axis_index("y"), lax.axis_index("c")
    others = [(1 - x, y), (x, 1 - y), (1 - x, 1 - y)]
    return x, y, c, others


def _allgather_small(name, block):
    R, C = block.shape

    def body(x_ref, out_ref, send_sems, recv_sems, local_sem):
        x, y, c, chips = _place()
        me, sibling = (x, y, c), (x, y, 1 - c)

        def rows(px, py, pc):
            return out_ref.at[4 * px + 2 * py + pc]

        def copy(k, blk, to, src=None):
            return pltpu.make_async_remote_copy(
                src_ref=rows(*blk) if src is None else src, dst_ref=rows(*blk),
                send_sem=send_sems.at[k], recv_sem=recv_sems.at[k], device_id=to, device_id_type=MESH)

        mine = pltpu.make_async_copy(x_ref, rows(*me), local_sem)
        mine.start()
        first = [copy(0, me, sibling, src=x_ref)]
        first += [copy(1 + j, me, (*chip, c), src=x_ref) for j, chip in enumerate(chips)]
        for cp in first:
            cp.start()
        passed = [copy(4 + j, (*chip, c), sibling) for j, chip in enumerate(chips)]
        for j, chip in enumerate(chips):
            copy(1 + j, (*chip, c), me).wait_recv()
            passed[j].start()
        copy(0, sibling, me).wait_recv()
        for j, chip in enumerate(chips):
            copy(4 + j, (*chip, 1 - c), me).wait_recv()
        for cp in first + passed:
            cp.wait_send()
        mine.wait()

    return pl.pallas_call(
        body, name=name, out_shape=jax.ShapeDtypeStruct((N_DEV, R, C), F32),
        in_specs=[pl.BlockSpec(memory_space=pltpu.VMEM)], out_specs=pl.BlockSpec(memory_space=pltpu.VMEM),
        scratch_shapes=[pltpu.SemaphoreType.DMA((7,)), pltpu.SemaphoreType.DMA((7,)), pltpu.SemaphoreType.DMA],
    )(block)


def _gather_side(bufs, across, within):
    def rows(ref, chip, half, piece):
        _, r0, n = piece
        return ref.at[2 * chip[0] + chip[1], pl.ds(half * (ref.shape[1] // 2) + r0, n)]

    def copies(ins, outs, send_sems, recv_sems, base):
        x, y, c, chips = _place()
        sibling = (x, y, 1 - c)
        pairs = []

        def add(k, src, dst, to, arrival):
            mk = lambda s, d, dev: pltpu.make_async_remote_copy(
                src_ref=s, dst_ref=d, send_sem=send_sems.at[base + k], recv_sem=recv_sems.at[base + k],
                device_id=dev, device_id_type=MESH)
            pairs.append((mk(src, dst, to), mk(arrival, arrival, (x, y, c))))

        for p, piece in enumerate(across):
            ref = outs[piece[0]]
            for j, chip in enumerate(chips):
                mine = rows(ref, (x, y), c, piece)
                add(3 * p + j, mine, mine, (*chip, c), rows(ref, chip, c, piece))
        for q, piece in enumerate(within):
            ref = outs[piece[0]]
            for j, chip in enumerate(chips):
                held = rows(ref, chip, c, piece)
                add(3 * (len(across) + q) + j, held, held, sibling, rows(ref, chip, 1 - c, piece))
        return pairs

    def start(*refs):
        for send, _ in copies(*refs):
            send.start()

    def wait(*refs):
        pairs = copies(*refs)
        for _, arrival in pairs:
            arrival.wait_recv()
        for send, _ in pairs:
            send.wait_send()

    return _Side(list(bufs), [jax.ShapeDtypeStruct(b.shape, b.dtype) for b in bufs],
                 3 * (len(across) + len(within)), start, wait, aliases=tuple((i, i) for i in range(len(bufs))))


def _chip_exchange(parts):
    n = len(parts)

    def copies(ins, outs, send_sems, recv_sems, base):
        x, y, c, chips = _place()
        return [pltpu.make_async_remote_copy(
            src_ref=ins[a].at[2 * chip[0] + chip[1]], dst_ref=outs[a].at[j],
            send_sem=send_sems.at[base + 3 * a + j], recv_sem=recv_sems.at[base + 3 * a + j],
            device_id=(*chip, c), device_id_type=MESH) for a in range(n) for j, chip in enumerate(chips)]

    return _Side(list(parts), [jax.ShapeDtypeStruct((N_CHIPS - 1,) + p.shape[1:], p.dtype) for p in parts],
                 3 * n, *_start_wait(copies))


def _pair_exchange(grads):
    def copies(ins, outs, send_sems, recv_sems, base):
        x, y, c, _ = _place()
        return [pltpu.make_async_remote_copy(
            src_ref=ins[a].at[:, 1 - c], dst_ref=outs[a], send_sem=send_sems.at[base + a],
            recv_sem=recv_sems.at[base + a], device_id=(x, y, 1 - c), device_id_type=MESH)
            for a in range(len(grads))]

    return _Side(list(grads), [jax.ShapeDtypeStruct((N_CHIPS,) + g.shape[2:], F32) for g in grads],
                 len(grads), *_start_wait(copies))


def _start_wait(copies):
    def start(*refs):
        for cp in copies(*refs):
            cp.start()

    def wait(*refs):
        cps = copies(*refs)
        for cp in cps:
            cp.wait_recv()
        for cp in cps:
            cp.wait_send()
    return start, wait


def _both(first, second):
    n_in, n_out = len(first.ins), len(first.out_shapes)

    def run(which):
        def go(ins, outs, send_sems, recv_sems, base):
            getattr(first, which)(ins[:n_in], outs[:n_out], send_sems, recv_sems, base)
            getattr(second, which)(ins[n_in:], outs[n_out:], send_sems, recv_sems, base + first.n_sems)
        return go

    aliases = first.aliases + tuple((a + n_in, b + n_out) for a, b in second.aliases)
    return _Side(first.ins + second.ins, first.out_shapes + second.out_shapes,
                 first.n_sems + second.n_sems, run("start"), run("wait"), aliases)


def _share_halves(halves):
    def copies(ins, outs, send_sems, recv_sems, base):
        x, y, c, _ = _place()
        pairs = []
        for a in range(len(halves)):
            mk = lambda s, d, dev, a=a: pltpu.make_async_remote_copy(
                src_ref=s, dst_ref=d, send_sem=send_sems.at[base + a], recv_sem=recv_sems.at[base + a],
                device_id=dev, device_id_type=MESH)
            theirs = outs[a].at[1 - c]
            pairs.append((mk(outs[a].at[c], outs[a].at[c], (x, y, 1 - c)), mk(theirs, theirs, (x, y, c))))
        return pairs

    def start(*refs):
        for send, _ in copies(*refs):
            send.start()

    def wait(*refs):
        pairs = copies(*refs)
        for _, arrival in pairs:
            arrival.wait_recv()
        for send, _ in pairs:
            send.wait_send()

    return _Side(list(halves), [jax.ShapeDtypeStruct(h.shape, F32) for h in halves], len(halves), start, wait,
                 aliases=tuple((i, i) for i in range(len(halves))))


def _pack(arrays):
    pieces = []
    for a in arrays:
        flat = a.reshape(-1).astype(F32)
        pieces.append(jnp.pad(flat, (0, (-flat.size) % PACK_ALIGN)))
    return jnp.concatenate(pieces).reshape(-1, LANES)


def _unpack(buf, shapes):
    lead = buf.shape[:-2]
    flat = buf.reshape(lead + (-1,))
    out, off = [], 0
    for shp in shapes:
        size = 1
        for s in shp:
            size *= s
        out.append(flat[..., off:off + size].reshape(lead + tuple(shp)))
        off += size + (-size) % PACK_ALIGN
    return out


def _gather_channels(buf, shapes):
    per_chip = _unpack(buf[0::2], shapes)
    return [jnp.transpose(a, (1, 0, 2)).reshape(a.shape[1], -1) for a in per_chip]


def _mm_tile(n, rows, limit_bytes=6 * 1024 * 1024):
    for t in (1408, 1280, 1024, 640, 512, 384, 256, 128):
        if n % t == 0 and rows * t * 2 <= limit_bytes:
            return t
    raise ValueError(f"no column tile for {n} x {rows}")


def kernel(x, p, norm_mix_g, w_in, conv_a_w, conv_a_b, ln_a_g, ln_a_b, conv_b_w, w_out, norm_ffn_g, w_up, conv_ffn_w, w_down, w_ple_gate, b_ple_gate, w_ple_proj, norm_final_g, loss_target, m_norm_mix_g, m_w_in, m_conv_a_w, m_conv_a_b, m_ln_a_g, m_ln_a_b, m_conv_b_w, m_w_out, m_norm_ffn_g, m_w_up, m_conv_ffn_w, m_w_down, m_w_ple_gate, m_b_ple_gate, m_w_ple_proj, m_norm_final_g, v_norm_mix_g, v_w_in, v_conv_a_w, v_conv_a_b, v_ln_a_g, v_ln_a_b, v_conv_b_w, v_w_out, v_norm_ffn_g, v_w_up, v_conv_ffn_w, v_w_down, v_w_ple_gate, v_b_ple_gate, v_w_ple_proj, v_norm_final_g):
    S, D = x.shape[1], x.shape[2]
    P = p.shape[3]
    A = conv_a_b.shape[1]
    F = w_down.shape[1] * N_CHIPS
    KA, KB, KF = conv_a_w.shape[1], conv_b_w.shape[1], conv_ffn_w.shape[1]
    xi, yi, ci = lax.axis_index("x"), lax.axis_index("y"), lax.axis_index("c")
    chip = 2 * xi + yi

    TM = _pick(S, (512, 256, 128))
    TL = _pick(S, (1024, 512, 256, 128))
    TE = _pick(S, (256, 128))
    TC = _pick(2 * F // N_CHIPS, (1408, 1024, 512, 256, 128))
    ffn_place = _pair_tile(F // TC)

    x2, p2, t2 = x.reshape(S, D), p.reshape(S, P), loss_target.reshape(S, D)
    gfin = norm_final_g.reshape(1, D)

    big = dict(w_in=w_in[0], w_out=w_out[0], w_up=w_up[0], w_down=w_down[0],
               w_ple_gate=w_ple_gate[0], w_ple_proj=w_ple_proj[0])
    names = list(big)
    buf = {n: _cast_into_gathered("cast_" + n, big[n], chip) for n in names}
    half = {n: big[n].shape[0] // 2 for n in names}
    up_a = half["w_up"] // 2
    (w_in3,) = _comm_only("gather_w_in_across", _gather_side([buf["w_in"]], [(0, 0, half["w_in"])], []))
    (w_in3,) = _comm_only("gather_w_in_within", _gather_side([w_in3], [], [(0, 0, half["w_in"])]))

    tap_shapes = [(KA, A // N_CHIPS), (KB, A // N_CHIPS), (KF, 2 * F // N_CHIPS)]
    taps = _allgather_small("allgather_taps", _pack([conv_a_w[0], conv_b_w[0], conv_ffn_w[0]]))
    conv_a_f, conv_b_f, conv_ffn_f = _gather_channels(taps, tap_shapes)

    def rms_prologue(rows, row_r, vec_r, ro_r):
        h = row_r[0][rows, :]
        hn = (h * _rms_stats(h) * vec_r[0][...]).astype(BF16)
        ro_r[0][rows, :] = hn
        return [hn]

    def cast_prologue(rows, row_r, vec_r, ro_r):
        hb = row_r[0][rows, :].astype(BF16)
        ro_r[0][rows, :] = hb
        return [hb]

    plain = lambda accs, tile_r: [accs[0]]
    residual = lambda accs, tile_r: [tile_r[0][...] + accs[0]]

    (z, hn1), (w_out_t, w_up_t) = _rows_mm(
        "in_proj", S, TL, 5 * A, _mm_tile(5 * A // N_CHIPS, D), row_ins=[x2], vec_ins=[norm_mix_g],
        weights=[(w_in3, "nn3")], tile_outs=[BF16], row_outs=[(D, BF16)], prologue=rms_prologue, epilogue=plain,
        side=_gather_side([buf["w_out"], buf["w_up"]], [(0, 0, half["w_out"]), (1, 0, up_a)], []))
    (a1, cat), (w_out3, w_up_t) = _mixer_fwd(
        z, conv_a_f, conv_a_b, ln_a_g, ln_a_b, conv_b_f, S, TE, A,
        side=_gather_side([w_out_t, w_up_t], [(1, up_a, half["w_up"] - up_a)],
                          [(0, 0, half["w_out"]), (1, 0, up_a)]))
    w_out_f = w_out3.reshape(2 * A, D)
    (h1,), (w_up3, w_proj_t) = _rows_mm(
        "out_proj", S, TL, D, _mm_tile(D, 2 * A), row_ins=[cat], weights=[(w_out_f, "nn2")],
        tile_ins=[x2], tile_outs=[F32], epilogue=residual,
        side=_gather_side([w_up_t, buf["w_ple_proj"]], [(1, 0, half["w_ple_proj"])],
                          [(0, up_a, half["w_up"] - up_a)]))
    (u0, hn2), (w_down_t, w_gate_t, w_proj3) = _rows_mm(
        "up_proj", S, TL, 2 * F, TC, row_ins=[h1], vec_ins=[norm_ffn_g],
        weights=[(w_up3, "nn3")], tile_outs=[BF16], row_outs=[(D, BF16)],
        prologue=rms_prologue, epilogue=plain, place=ffn_place,
        side=_gather_side([buf["w_down"], buf["w_ple_gate"], w_proj_t],
                          [(0, 0, half["w_down"]), (1, 0, half["w_ple_gate"])], [(2, 0, half["w_ple_proj"])]))
    (act, conv_u0), (w_down3, w_gate3) = _ffn_act(
        u0, conv_ffn_f, S, TM, F, TC,
        side=_gather_side([w_down_t, w_gate_t], [], [(0, 0, half["w_down"]), (1, 0, half["w_ple_gate"])]))
    w_down_f = w_down3.reshape(F, D)
    w_gate_f = w_gate3.reshape(D, D)
    w_proj_f = jnp.transpose(w_proj3, (1, 0, 2)).reshape(P, D)
    (h2,) = _rows_mm("down_proj", S, TL, D, _mm_tile(D, F), row_ins=[act], weights=[(w_down_f, "nn2")],
                     tile_ins=[h1], tile_outs=[F32], epilogue=residual)

    loss_part, g_norm_final, g_b_gate, dh3, dpre, dpp, h2b, pb = _ple_loss(
        h2, p2, t2, w_gate_f, w_proj_f, b_ple_gate, gfin, S, TE)

    TK = _pick(S, (2048, 1024, 512, 256, 128))
    wt = lambda n: _pick(n, (1408, 1280, 1024, 512, 256, 128))
    chip_sums, from_chips = {}, {}

    def to_sibling(parts):
        ns = list(parts)
        halves = [parts[n].reshape(N_CHIPS, 2, big[n].shape[0] // 2, big[n].shape[1]) for n in ns]
        return ns, halves, _pair_exchange(halves)

    def to_chips(ns, halves, from_sibling):
        sums = [_add_pair("pair_sum_" + n, h, r, ci, chip) for n, h, r in zip(ns, halves, from_sibling)]
        for n, (s, _) in zip(ns, sums):
            chip_sums[n] = s
        return ns, _chip_exchange([b for _, b in sums])

    def landed(ns, side_outs):
        for n, r in zip(ns, side_outs):
            from_chips[n] = r

    ns, halves, side = to_sibling(dict(
        w_ple_gate=_tn_mm("dw_ple_gate", h2b, dpre, wt(D), wt(D), TK),
        w_ple_proj=_tn_mm("dw_ple_proj", pb, dpp, wt(P), wt(D // N_CHIPS), TK, cols_per_chip=D // N_CHIPS)))
    (dh2,), got = _rows_mm("ple_bwd", S, TL, D, _mm_tile(D, D), row_ins=[dpre], weights=[(w_gate_f, "nt2")],
                           tile_ins=[dh3], tile_outs=[F32], epilogue=residual, side=side)
    ple_ns, ple_chips = to_chips(ns, halves, got)
    (dact, dh2b), got = _rows_mm("down_bwd", S, TL, F, _mm_tile(F, D), row_ins=[dh2], weights=[(w_down_f, "nt2")],
                                 tile_outs=[BF16], row_outs=[(D, BF16)], prologue=cast_prologue, epilogue=plain,
                                 side=ple_chips)
    landed(ple_ns, got)
    ns, halves, side = to_sibling(dict(w_down=_tn_mm("dw_down", act, dh2b, wt(F), wt(D), TK)))
    (du0, g_conv_gate, g_conv_up), got = _ffn_bwd(u0, conv_u0, dact, conv_ffn_f, S, TM, F, TC, side=side)
    down_ns, down_chips = to_chips(ns, halves, got)
    g_conv_ffn = jnp.concatenate([g_conv_gate, g_conv_up], axis=1)
    ns, halves, side = to_sibling(dict(
        w_up=_tn_mm("dw_up", hn2, du0, wt(D), TC, TK, cols_per_chip=2 * F // N_CHIPS, place=ffn_place)))

    def up_bwd_epilogue(acc, rows, row_r, vec_r, ro_r, ao_r):
        dh, dg = _rms_bwd(row_r[0][rows, :], vec_r[0][...], acc)
        dh1_ = row_r[1][rows, :] + dh
        ro_r[0][rows, :] = dh1_
        ro_r[1][rows, :] = dh1_.astype(BF16)
        ao_r[0][...] += dg

    (dh1, dh1b, g_norm_ffn), got = _kloop_mm(
        "up_bwd", S, TM, du0, w_up3, TC, row_ins=[h1, dh2], vec_ins=[norm_ffn_g],
        row_outs=[(D, F32), (D, BF16)], acc_outs=[(1, D)], epilogue=up_bwd_epilogue, place=ffn_place,
        side=_both(down_chips, side))
    landed(down_ns, got[:len(down_ns)])
    up_ns, up_chips = to_chips(ns, halves, got[len(down_ns):])
    ns, halves, side = to_sibling(dict(w_out=_tn_mm("dw_out", cat, dh1b, wt(2 * A), wt(D), TK)))
    (dcat,), got = _rows_mm("out_bwd", S, TL, 2 * A, _mm_tile(2 * A, D), row_ins=[dh1b],
                            weights=[(w_out_f, "nt2")], tile_outs=[BF16], epilogue=plain, side=side)
    out_ns, out_chips = to_chips(ns, halves, got)
    da1, ln_sums = _mixer_bwd_ln(dcat, a1, ln_a_g, ln_a_b, S, TE, A)
    (dz, g_conv_a, g_conv_b), got = _mixer_bwd_conv(z, dcat, da1, conv_a_f, conv_b_f, S, TE, A,
                                                    side=_both(up_chips, out_chips))
    landed(up_ns + out_ns, got)
    ns, halves, side = to_sibling(dict(
        w_in=_tn_mm("dw_in", hn1, dz, wt(D), wt(5 * A // N_CHIPS), TK, cols_per_chip=5 * A // N_CHIPS)))
    ns, side = to_chips(ns, halves, _comm_only("grads_exchange_pairs_in", side))

    def in_bwd_epilogue(acc, rows, row_r, vec_r, ro_r, ao_r):
        dh, dg = _rms_bwd(row_r[0][rows, :], vec_r[0][...], acc)
        ro_r[0][rows, :] = row_r[1][rows, :] + dh
        ao_r[0][...] += dg

    early = [n for n in names if n != "w_in"]
    early_halves = [_add_chips("chip_sum_" + n, chip_sums[n], from_chips[n], ci) for n in early]
    (dx, g_norm_mix), got = _kloop_mm(
        "in_bwd", S, TM, dz, w_in3, _mm_tile(5 * A // N_CHIPS, D), row_ins=[x2, dh1],
        vec_ins=[norm_mix_g], row_outs=[(D, F32)], acc_outs=[(1, D)], epilogue=in_bwd_epilogue,
        side=_both(side, _share_halves(early_halves)))
    landed(ns, got[:1])
    shared = dict(zip(early, got[1:]))
    (shared["w_in"],) = _comm_only("grads_share_w_in", _share_halves(
        [_add_chips("chip_sum_w_in", chip_sums["w_in"], from_chips["w_in"], ci)]))

    reduced = [shared[n] for n in names]
    moments = dict(w_in=(m_w_in, v_w_in), w_out=(m_w_out, v_w_out), w_up=(m_w_up, v_w_up),
                   w_down=(m_w_down, v_w_down), w_ple_gate=(m_w_ple_gate, v_w_ple_gate),
                   w_ple_proj=(m_w_ple_proj, v_w_ple_proj))
    grads, deltas, new_m, new_v = {}, {}, {}, {}
    for n, g in zip(names, reduced):
        d_, m_, v_, g = _adamw("adamw_" + n, big[n], g.reshape(big[n].shape), moments[n][0][0], moments[n][1][0],
                               copy_grad=True)
        grads[n], deltas[n], new_m[n], new_v[n] = g[None], d_[None], m_[None], v_[None]

    small = ["norm_mix_g", "conv_a_w", "conv_a_b", "ln_a_g", "ln_a_b", "conv_b_w", "norm_ffn_g",
             "conv_ffn_w", "b_ple_gate", "norm_final_g"]
    small_part = [g_norm_mix, g_conv_a, ln_sums[2:3], ln_sums[0:1], ln_sums[1:2], g_conv_b, g_norm_ffn,
                  g_conv_ffn, g_b_gate, g_norm_final]
    full_shapes = [a.shape for a in small_part]
    summed = _sum_devices("small_grads_sum", _allgather_small("allgather_small_grads", _pack(small_part)))
    small_g = dict(zip(small, _unpack(summed, full_shapes)))
    for n, width in (("conv_a_w", A), ("conv_b_w", A), ("conv_ffn_w", 2 * F)):
        small_g[n] = lax.dynamic_slice_in_dim(small_g[n], chip * (width // N_CHIPS), width // N_CHIPS, axis=1)
    small_w = dict(norm_mix_g=(norm_mix_g, m_norm_mix_g, v_norm_mix_g), conv_a_w=(conv_a_w, m_conv_a_w, v_conv_a_w),
                   conv_a_b=(conv_a_b, m_conv_a_b, v_conv_a_b), ln_a_g=(ln_a_g, m_ln_a_g, v_ln_a_g),
                   ln_a_b=(ln_a_b, m_ln_a_b, v_ln_a_b), conv_b_w=(conv_b_w, m_conv_b_w, v_conv_b_w),
                   norm_ffn_g=(norm_ffn_g, m_norm_ffn_g, v_norm_ffn_g),
                   conv_ffn_w=(conv_ffn_w, m_conv_ffn_w, v_conv_ffn_w),
                   b_ple_gate=(b_ple_gate, m_b_ple_gate, v_b_ple_gate),
                   norm_final_g=(norm_final_g, m_norm_final_g, v_norm_final_g))
    out_shapes = [small_w[n][0].shape for n in small]
    packed_g = _pack([small_g[n] for n in small])
    packed = [_pack([small_w[n][k] for n in small]) for k in range(3)]
    d_s, m_s, v_s = _adamw("adamw_small", packed[0], packed_g, packed[1], packed[2])
    for n, g, d_, m_, v_ in zip(small, _unpack(packed_g, out_shapes), _unpack(d_s, out_shapes),
                                _unpack(m_s, out_shapes), _unpack(v_s, out_shapes)):
        grads[n], deltas[n], new_m[n], new_v[n] = g, d_, m_, v_

    order = ["norm_mix_g", "w_in", "conv_a_w", "conv_a_b", "ln_a_g", "ln_a_b", "conv_b_w", "w_out", "norm_ffn_g",
             "w_up", "conv_ffn_w", "w_down", "w_ple_gate", "b_ple_gate", "w_ple_proj", "norm_final_g"]
    loss = lax.psum(loss_part[0, 0], ("x", "y", "c"))
    return (loss, dx.reshape(x.shape), *[grads[n] for n in order], *[deltas[n] for n in order],
            *[new_m[n] for n in order], *[new_v[n] for n in order])
```

```python
from typing import Callable, NamedTuple

import jax
import jax.numpy as jnp
from jax import lax
from jax.experimental import pallas as pl
from jax.experimental.pallas import tpu as pltpu

F32 = jnp.float32
BF16 = jnp.bfloat16
MESH = pl.DeviceIdType.MESH
ANY = pl.BlockSpec(memory_space=pl.ANY)

EPS = 1e-6
ADAM_LR = 0.001
ADAM_B1 = 0.9
ADAM_B2 = 0.999
ADAM_EPS = 1e-08
ADAM_WD = 0.01
ADAM_STEP = 10

N_CHIPS = 4
N_DEV = 8
LANES = 128
SUBLANES = 8
PACK_ALIGN = LANES * SUBLANES
ROW_CHUNK = 32
VMEM_CAP = 60 * 1024 * 1024
VMEM_SLACK = 6 * 1024 * 1024


def _pick(n, cands):
    for c in cands:
        if n % c == 0:
            return c
    raise ValueError(f"no tile of {cands} divides {n}")


def _nbytes(shape, dtype):
    n = 1
    for s in shape:
        if s is not None:
            n *= s
    return n * jnp.dtype(dtype).itemsize


def _params(sem, blocks, scratch=(), temps=()):
    est = (2 * sum(_nbytes(s, d) for s, d in blocks) + sum(_nbytes(s, d) for s, d in scratch)
           + sum(_nbytes(s, d) for s, d in temps))
    return pltpu.CompilerParams(dimension_semantics=sem,
                                vmem_limit_bytes=min(est + VMEM_SLACK, VMEM_CAP))


def _in_hbm(arrays):
    return [pltpu.with_memory_space_constraint(a, pltpu.HBM) for a in arrays]


def _sigmoid(x):
    return 1.0 / (1.0 + jnp.exp(-x))


def _rsum(x):
    return jnp.sum(x, axis=0, keepdims=True)


class _Side(NamedTuple):
    ins: list
    out_shapes: list
    n_sems: int
    start: Callable
    wait: Callable
    aliases: tuple = ()


def _call(body, side, *, name, grid, in_specs, out_specs, out_shape, scratch, params, args):
    vmem = [pltpu.VMEM(s, d) for s, d in scratch]
    if side is None:
        outs = pl.pallas_call(body, name=name, grid=grid, in_specs=in_specs, out_specs=out_specs,
                              out_shape=out_shape, scratch_shapes=vmem, compiler_params=params)(*args)
        return list(outs), []
    n_in, n_out, n_sc = len(in_specs), len(out_specs), len(scratch)
    ns_in, ns_out = len(side.ins), len(side.out_shapes)

    def carrier(*refs):
        pos = [0]
        def take(n):
            pos[0] += n
            return refs[pos[0] - n:pos[0]]
        ins, s_ins, outs, s_outs, scr = take(n_in), take(ns_in), take(n_out), take(ns_out), take(n_sc)
        send_sems, recv_sems = take(2)
        first = last = None
        for axis, extent in enumerate(grid):
            at_start, at_end = pl.program_id(axis) == 0, pl.program_id(axis) == extent - 1
            first = at_start if first is None else first & at_start
            last = at_end if last is None else last & at_end

        @pl.when(first)
        def _():
            side.start(s_ins, s_outs, send_sems, recv_sems, 0)
        body(*ins, *outs, *scr)

        @pl.when(last)
        def _():
            side.wait(s_ins, s_outs, send_sems, recv_sems, 0)

    outs = pl.pallas_call(
        carrier, name=name, grid=grid, in_specs=list(in_specs) + [ANY] * ns_in,
        out_specs=list(out_specs) + [ANY] * ns_out, out_shape=list(out_shape) + list(side.out_shapes),
        scratch_shapes=vmem + [pltpu.SemaphoreType.DMA((side.n_sems,)), pltpu.SemaphoreType.DMA((side.n_sems,))],
        input_output_aliases={n_in + a: n_out + b for a, b in side.aliases},
        compiler_params=params)(*args, *_in_hbm(side.ins))
    return list(outs[:n_out]), list(outs[n_out:])


def _comm_only(name, side):
    n_in = len(side.ins)

    def body(*refs):
        ins, outs = refs[:n_in], refs[n_in:n_in + len(side.out_shapes)]
        send_sems, recv_sems = refs[n_in + len(side.out_shapes):]
        side.start(ins, outs, send_sems, recv_sems, 0)
        side.wait(ins, outs, send_sems, recv_sems, 0)

    return pl.pallas_call(
        body, name=name, out_shape=list(side.out_shapes), in_specs=[ANY] * n_in,
        out_specs=[ANY] * len(side.out_shapes),
        scratch_shapes=[pltpu.SemaphoreType.DMA((side.n_sems,)), pltpu.SemaphoreType.DMA((side.n_sems,))],
        input_output_aliases=dict(side.aliases),
    )(*_in_hbm(side.ins))


def _rms_stats(x):
    return lax.rsqrt(jnp.mean(x * x, axis=-1, keepdims=True) + EPS)


def _rms_bwd(h, g, dout):
    r = _rms_stats(h)
    n = h * r
    dn = dout * g
    dh = r * (dn - n * jnp.mean(dn * n, axis=-1, keepdims=True))
    return dh, _rsum(dout * n)


def _identity(t):
    return t


def _chip_major(nb, place=_identity):
    return lambda i, j: (place(j) // nb, 0, place(j) % nb)


def _rows_mm(name, S, TM, N, TN, *, row_ins, vec_ins=(), weights, tile_ins=(), tile_outs, row_outs=(),
             prologue=None, epilogue, place=_identity, side=None):
    nI, nJ = S // TM, N // TN
    n_row, n_vec, n_w, n_tile = len(row_ins), len(vec_ins), len(weights), len(tile_ins)
    n_to, n_ro = len(tile_outs), len(row_outs)

    in_specs, blocks, scratch, ks = [], [], [], []
    for a in row_ins:
        in_specs.append(pl.BlockSpec((TM, a.shape[1]), lambda i, j: (i, 0)))
        blocks.append(((TM, a.shape[1]), a.dtype))
    for a in vec_ins:
        in_specs.append(pl.BlockSpec(a.shape, lambda i, j: (0, 0)))
        blocks.append((a.shape, a.dtype))
    for w, mode in weights:
        if mode == "nn2":
            k = w.shape[0]
            in_specs.append(pl.BlockSpec((k, TN), lambda i, j: (0, j)))
        elif mode == "nn3":
            k = w.shape[1]
            in_specs.append(pl.BlockSpec((None, k, TN), _chip_major(w.shape[2] // TN, place)))
        else:
            k = w.shape[1]
            in_specs.append(pl.BlockSpec((TN, k), lambda i, j: (j, 0)))
        ks.append(k)
        blocks.append(((k, TN), BF16))
        if prologue is not None:
            scratch.append(((TM, k), BF16))
    for a in tile_ins:
        in_specs.append(pl.BlockSpec((TM, TN), lambda i, j: (i, j)))
        blocks.append(((TM, TN), a.dtype))

    out_shape, out_specs = [], []
    for dt in tile_outs:
        out_shape.append(jax.ShapeDtypeStruct((S, N), dt))
        out_specs.append(pl.BlockSpec((TM, TN), lambda i, j: (i, j)))
        blocks.append(((TM, TN), dt))
    for width, dt in row_outs:
        out_shape.append(jax.ShapeDtypeStruct((S, width), dt))
        out_specs.append(pl.BlockSpec((TM, width), lambda i, j: (i, 0)))
        blocks.append(((TM, width), dt))

    modes = [m for _, m in weights]

    def body(*refs):
        pos = 0
        def take(n):
            nonlocal pos
            out = refs[pos:pos + n]
            pos += n
            return out
        row_r, vec_r, w_r, tile_r = take(n_row), take(n_vec), take(n_w), take(n_tile)
        to_r, ro_r, a_sc = take(n_to), take(n_ro), take(len(scratch))

        if prologue is None:
            a_sc = row_r[:n_w]
        else:
            @pl.when(pl.program_id(1) == 0)
            def _():
                def chunk(ci, carry):
                    rows = pl.ds(pl.multiple_of(ci * ROW_CHUNK, ROW_CHUNK), ROW_CHUNK)
                    for sc, a in zip(a_sc, prologue(rows, row_r, vec_r, ro_r)):
                        sc[rows, :] = a
                    return carry
                lax.fori_loop(0, TM // ROW_CHUNK, chunk, 0)

        accs = []
        for w_ref, sc, mode in zip(w_r, a_sc, modes):
            if mode == "nt2":
                accs.append(lax.dot_general(sc[...], w_ref[...], (((1,), (1,)), ((), ())),
                                            preferred_element_type=F32))
            else:
                accs.append(jnp.dot(sc[...], w_ref[...], preferred_element_type=F32))
        outs = epilogue(accs, tile_r)
        for r, o in zip(to_r, outs):
            r[...] = o.astype(r.dtype)

    outs, side_outs = _call(
        body, side, name=name, grid=(nI, nJ), in_specs=in_specs, out_specs=out_specs, out_shape=out_shape,
        scratch=scratch, params=_params(("arbitrary", "arbitrary"), blocks, scratch, temps=[((TM, TN), F32)] * 3),
        args=[*row_ins, *vec_ins, *[w for w, _ in weights], *tile_ins])
    return outs if side is None else (outs, side_outs)


def _kloop_mm(name, S, TM, a, w3, TK, *, row_ins, vec_ins, row_outs, acc_outs, epilogue, place=_identity,
              side=None):
    _, N, Ks = w3.shape
    nb = Ks // TK
    nK = N_CHIPS * nb
    n_row, n_vec, n_ro, n_ao = len(row_ins), len(vec_ins), len(row_outs), len(acc_outs)

    in_specs = [pl.BlockSpec((TM, TK), lambda i, k: (i, k)),
                pl.BlockSpec((None, N, TK), _chip_major(nb, place))]
    blocks = [((TM, TK), BF16), ((N, TK), BF16)]
    for r in row_ins:
        in_specs.append(pl.BlockSpec((TM, r.shape[1]), lambda i, k: (i, 0)))
        blocks.append(((TM, r.shape[1]), r.dtype))
    for v in vec_ins:
        in_specs.append(pl.BlockSpec(v.shape, lambda i, k: (0, 0)))
        blocks.append((v.shape, v.dtype))
    out_shape, out_specs = [], []
    for width, dt in row_outs:
        out_shape.append(jax.ShapeDtypeStruct((S, width), dt))
        out_specs.append(pl.BlockSpec((TM, width), lambda i, k: (i, 0)))
        blocks.append(((TM, width), dt))
    for rows, width in acc_outs:
        out_shape.append(jax.ShapeDtypeStruct((rows, width), F32))
        out_specs.append(pl.BlockSpec((rows, width), lambda i, k: (0, 0)))
        blocks.append(((rows, width), F32))
    scratch = [((TM, N), F32)]

    def body(*refs):
        a_ref, w_ref = refs[0], refs[1]
        row_r = refs[2:2 + n_row]
        vec_r = refs[2 + n_row:2 + n_row + n_vec]
        pos = 2 + n_row + n_vec
        ro_r = refs[pos:pos + n_ro]
        ao_r = refs[pos + n_ro:pos + n_ro + n_ao]
        acc_sc = refs[pos + n_ro + n_ao]
        i, k = pl.program_id(0), pl.program_id(1)
        @pl.when(k == 0)
        def _():
            acc_sc[...] = jnp.zeros_like(acc_sc)
        acc_sc[...] += lax.dot_general(a_ref[...], w_ref[...], (((1,), (1,)), ((), ())),
                                       preferred_element_type=F32)

        @pl.when(k == nK - 1)
        def _():
            @pl.when(i == 0)
            def _():
                for r in ao_r:
                    r[...] = jnp.zeros_like(r)

            def chunk(ci, carry):
                rows = pl.ds(pl.multiple_of(ci * ROW_CHUNK, ROW_CHUNK), ROW_CHUNK)
                epilogue(acc_sc[rows, :], rows, row_r, vec_r, ro_r, ao_r)
                return carry
            lax.fori_loop(0, TM // ROW_CHUNK, chunk, 0)

    outs, side_outs = _call(
        body, side, name=name, grid=(S // TM, nK), in_specs=in_specs, out_specs=out_specs, out_shape=out_shape,
        scratch=scratch, params=_params(("arbitrary", "arbitrary"), blocks, scratch, temps=[((TM, N), F32)]),
        args=[a, w3, *row_ins, *vec_ins])
    return outs if side is None else (outs, side_outs)


def _tn_mm(name, a, b, TMw, TNw, TK, cols_per_chip=None, place=_identity):
    S, M = a.shape
    N = b.shape[1]
    nK = S // TK
    if cols_per_chip is None:
        out_shape = jax.ShapeDtypeStruct((M, N), F32)
        out_spec = pl.BlockSpec((TMw, TNw), lambda i, j, k: (i, j))
    else:
        nb = cols_per_chip // TNw
        out_shape = jax.ShapeDtypeStruct((N_CHIPS, M, cols_per_chip), F32)
        out_spec = pl.BlockSpec((None, TMw, TNw), lambda i, j, k: (place(j) // nb, i, place(j) % nb))

    def body(a_ref, b_ref, o_ref):
        @pl.when(pl.program_id(2) == 0)
        def _():
            o_ref[...] = jnp.zeros_like(o_ref)
        o_ref[...] += lax.dot_general(a_ref[...], b_ref[...], (((0,), (0,)), ((), ())),
                                      preferred_element_type=F32)

    blocks = [((TK, TMw), BF16), ((TK, TNw), BF16), ((TMw, TNw), F32)]
    return pl.pallas_call(
        body, name=name, grid=(M // TMw, N // TNw, nK),
        in_specs=[pl.BlockSpec((TK, TMw), lambda i, j, k: (k, i)),
                  pl.BlockSpec((TK, TNw), lambda i, j, k: (k, j))],
        out_specs=out_spec, out_shape=out_shape,
        compiler_params=_params(("arbitrary", "arbitrary", "arbitrary"), blocks,
                                temps=[((TMw, TNw), F32), ((TK, TMw), BF16)]),
    )(*_in_hbm([a, b]))


def _prev_rows(TM, H, col):
    return lambda i: (jnp.maximum(i * (TM // H) - 1, 0), col)


def _next_rows(S, TM, H, col):
    return lambda i: (jnp.minimum((i + 1) * (TM // H), S // H - 1), col)


def _taps_causal(ext_ref, w_ref, K, H, TM, cs):
    acc = None
    for k in range(K):
        term = ext_ref[pl.ds(H - (K - 1) + k, TM), cs] * w_ref[pl.ds(k, 1), cs]
        acc = term if acc is None else acc + term
    return acc


def _taps_anticausal(ext_ref, w_ref, K, TM, cs):
    acc = None
    for k in range(K):
        term = ext_ref[pl.ds(K - 1 - k, TM), cs] * w_ref[pl.ds(k, 1), cs]
        acc = term if acc is None else acc + term
    return acc


def _tap_grads(ext_ref, g, K, H, TM, cs):
    return [_rsum(ext_ref[pl.ds(H - (K - 1) + k, TM), cs] * g) for k in range(K)]


def _shift_copies(ext_ref, shifted, cs):
    n = shifted.shape[1]
    for r in range(1, SUBLANES):
        shifted[r - 1] = ext_ref[pl.ds(r, n), cs]


def _rows_at(ext_ref, shifted, start, n, cs):
    q, r = divmod(start, SUBLANES)
    if r == 0:
        return ext_ref[pl.ds(start, n), cs]
    return shifted[r - 1, pl.ds(SUBLANES * q, n), :]


def _mixer_fwd(z, conv_a_w, conv_a_b, ln_g, ln_b, conv_b_w, S, TM, A, side=None):
    H = 32
    KA, KB = conv_a_w.shape[0], conv_b_w.shape[0]
    n_chunks = A // LANES
    RB = _pick(TM, (64, 32))

    def body(zc_ref, zh_ref, wa_ref, ba_ref, g_ref, b_ref, wb_ref, a1_ref, cat_ref, ext_a, ext_b, shifted):
        i = pl.program_id(0)
        live = (i > 0).astype(F32)
        zc = zc_ref[...].astype(F32)
        zh = zh_ref[...].astype(F32) * live
        ext_a[pl.ds(0, H), :] = zh[:, 0:A] * _sigmoid(zh[:, A:2 * A])
        ext_a[pl.ds(H, TM), :] = zc[:, 0:A] * _sigmoid(zc[:, A:2 * A])
        ext_b[pl.ds(0, H), :] = zh[:, 3 * A:4 * A] * zh[:, 4 * A:5 * A]
        ext_b[pl.ds(H, TM), :] = zc[:, 3 * A:4 * A] * zc[:, 4 * A:5 * A]

        def chunk(c, carry):
            cs = pl.ds(pl.multiple_of(c * LANES, LANES), LANES)
            _shift_copies(ext_a, shifted, cs)
            for r0 in range(0, TM, RB):
                acc = None
                for k in range(KA):
                    term = _rows_at(ext_a, shifted, H - (KA - 1) + k + r0, RB, cs) * wa_ref[pl.ds(k, 1), cs]
                    acc = term if acc is None else acc + term
                a1_ref[pl.ds(r0, RB), cs] = acc + ba_ref[:, cs]
            return carry
        lax.fori_loop(0, n_chunks, chunk, 0)

        a1 = a1_ref[...]
        mu = jnp.mean(a1, axis=-1, keepdims=True)
        d = a1 - mu
        var = jnp.mean(d * d, axis=-1, keepdims=True)
        a2 = d * lax.rsqrt(var + EPS) * g_ref[...] + b_ref[...]
        cat_ref[:, 0:A] = (a2 * _sigmoid(a2)).astype(BF16)
        cbc = _taps_causal(ext_b, wb_ref, KB, H, TM, slice(None))
        cat_ref[:, A:2 * A] = (zc[:, 2 * A:3 * A] * cbc).astype(BF16)

    blocks = [((TM, 5 * A), BF16), ((H, 5 * A), BF16), ((KA, A), F32), ((KB, A), F32),
              ((TM, A), F32), ((TM, 2 * A), BF16)]
    scratch = [((H + TM, A), F32), ((H + TM, A), F32), ((SUBLANES - 1, H + TM - SUBLANES, LANES), F32)]
    vec = lambda r: pl.BlockSpec((r, A), lambda i: (0, 0))
    outs, side_outs = _call(
        body, side, name="mixer_fwd", grid=(S // TM,),
        in_specs=[pl.BlockSpec((TM, 5 * A), lambda i: (i, 0)),
                  pl.BlockSpec((H, 5 * A), _prev_rows(TM, H, 0)),
                  vec(KA), vec(1), vec(1), vec(1), vec(KB)],
        out_specs=[pl.BlockSpec((TM, A), lambda i: (i, 0)), pl.BlockSpec((TM, 2 * A), lambda i: (i, 0))],
        out_shape=[jax.ShapeDtypeStruct((S, A), F32), jax.ShapeDtypeStruct((S, 2 * A), BF16)],
        scratch=scratch,
        params=_params(("arbitrary",), blocks, scratch, temps=[((TM, 5 * A), F32)] * 2 + [((TM, A), F32)] * 10),
        args=[z, z, conv_a_w, conv_a_b, ln_g, ln_b, conv_b_w])
    return outs if side is None else (outs, side_outs)


def _pair_tile(nF):
    return lambda t: (t % 2) * nF + t // 2


FFN_ROWS = 32


def _bcast_taps(w_ref, K, lanes):
    return [jnp.broadcast_to(w_ref[pl.ds(k, 1), lanes], (FFN_ROWS, LANES)) for k in range(K)]


def _ffn_act(u0, conv_w, S, TM, F, TC, side=None):
    H = 16
    K = conv_w.shape[0]
    nF = F // TC

    def body(uc_ref, uh_ref, wg_ref, wu_ref, o_ref, conv_ref, ext):
        live = (pl.program_id(0) > 0).astype(F32)
        ext[pl.ds(0, H), :] = uh_ref[...].astype(F32) * live
        ext[pl.ds(H, TM), :] = uc_ref[...].astype(F32)

        def lane_chunk(c, carry):
            lo = pl.ds(pl.multiple_of(c * LANES, LANES), LANES)
            lg, lu = lo, pl.ds(pl.multiple_of(TC + c * LANES, LANES), LANES)
            wg, wu = _bcast_taps(wg_ref, K, lo), _bcast_taps(wu_ref, K, lo)
            for r0 in range(0, TM, FFN_ROWS):
                g = u = None
                for k in range(K):
                    rows = pl.ds(H - (K - 1) + k + r0, FFN_ROWS)
                    tg, tu = ext[rows, lg] * wg[k], ext[rows, lu] * wu[k]
                    g, u = (tg, tu) if g is None else (g + tg, u + tu)
                o_ref[pl.ds(r0, FFN_ROWS), lo] = (g * _sigmoid(g) * u).astype(BF16)
                conv_ref[pl.ds(r0, FFN_ROWS), lg] = g.astype(BF16)
                conv_ref[pl.ds(r0, FFN_ROWS), lu] = u.astype(BF16)
            return carry
        lax.fori_loop(0, TC // LANES, lane_chunk, 0)

    blocks = [((TM, 2 * TC), BF16), ((H, 2 * TC), BF16), ((K, TC), F32), ((K, TC), F32), ((TM, TC), BF16),
              ((TM, 2 * TC), BF16)]
    scratch = [((H + TM, 2 * TC), F32)]
    outs, side_outs = _call(
        body, side, name="ffn_act", grid=(S // TM, nF),
        in_specs=[pl.BlockSpec((TM, 2 * TC), lambda i, j: (i, j)),
                  pl.BlockSpec((H, 2 * TC), lambda i, j: (jnp.maximum(i * (TM // H) - 1, 0), j)),
                  pl.BlockSpec((K, TC), lambda i, j: (0, j)),
                  pl.BlockSpec((K, TC), lambda i, j: (0, j + nF))],
        out_specs=[pl.BlockSpec((TM, TC), lambda i, j: (i, j)), pl.BlockSpec((TM, 2 * TC), lambda i, j: (i, j))],
        out_shape=[jax.ShapeDtypeStruct((S, F), BF16), jax.ShapeDtypeStruct((S, 2 * F), BF16)],
        scratch=scratch,
        params=_params(("arbitrary", "arbitrary"), blocks, scratch, temps=[((TM, 2 * TC), F32)]),
        args=[u0, u0, conv_w, conv_w])
    return outs if side is None else (outs, side_outs)


def _ple_loss(h2, p, target, w_gate, w_proj, b_gate, g_final, S, TM):
    D, P = h2.shape[1], p.shape[1]

    def body(h_ref, p_ref, t_ref, wg_ref, wp_ref, b_ref, g_ref,
             loss_ref, dg_ref, db_ref, dh_ref, dpre_ref, dpp_ref, hb_ref, pb_ref, pre_sc, pp_sc):
        @pl.when(pl.program_id(0) == 0)
        def _():
            loss_ref[...] = jnp.zeros_like(loss_ref)
            dg_ref[...] = jnp.zeros_like(dg_ref)
            db_ref[...] = jnp.zeros_like(db_ref)
        hb_ref[...] = h_ref[...].astype(BF16)
        pb_ref[...] = p_ref[...].astype(BF16)
        pre_sc[...] = jnp.dot(hb_ref[...], wg_ref[...], preferred_element_type=F32)
        pp_sc[...] = jnp.dot(pb_ref[...], wp_ref[...], preferred_element_type=F32)

        def chunk(ci, carry):
            rows = pl.ds(pl.multiple_of(ci * ROW_CHUNK, ROW_CHUNK), ROW_CHUNK)
            g = g_ref[...]
            gate = _sigmoid(pre_sc[rows, :] + b_ref[...])
            pp = pp_sc[rows, :]
            h = h_ref[rows, :] + pp * gate
            r = _rms_stats(h)
            n = h * r
            diff = n * g - t_ref[rows, :]
            loss_ref[...] += 0.5 * jnp.sum(jnp.mean(diff * diff, axis=-1, keepdims=True), axis=0, keepdims=True)
            dy = diff * (1.0 / D)
            dn = dy * g
            dh = r * (dn - n * jnp.mean(dn * n, axis=-1, keepdims=True))
            dh_ref[rows, :] = dh
            dg_ref[...] += _rsum(dy * n)
            dpre = dh * pp * gate * (1.0 - gate)
            dpre_ref[rows, :] = dpre.astype(BF16)
            dpp_ref[rows, :] = (dh * gate).astype(BF16)
            db_ref[...] += _rsum(dpre)
            return carry
        lax.fori_loop(0, TM // ROW_CHUNK, chunk, 0)

    row = pl.BlockSpec((TM, D), lambda i: (i, 0))
    prow = pl.BlockSpec((TM, P), lambda i: (i, 0))
    vec = pl.BlockSpec((1, D), lambda i: (0, 0))
    whole = lambda a: pl.BlockSpec(a.shape, lambda i: (0, 0))
    blocks = ([((TM, D), F32)] * 3 + [((TM, P), F32), ((D, D), BF16), ((P, D), BF16)]
              + [((TM, D), BF16)] * 3 + [((TM, P), BF16)])
    scratch = [((TM, D), F32)] * 2
    return pl.pallas_call(
        body, name="ple_loss", grid=(S // TM,),
        in_specs=[row, prow, row, whole(w_gate), whole(w_proj), vec, vec],
        out_specs=[pl.BlockSpec((1, 1), lambda i: (0, 0)), vec, vec, row, row, row, row, prow],
        out_shape=[jax.ShapeDtypeStruct((1, 1), F32), jax.ShapeDtypeStruct((1, D), F32),
                   jax.ShapeDtypeStruct((1, D), F32), jax.ShapeDtypeStruct((S, D), F32),
                   jax.ShapeDtypeStruct((S, D), BF16), jax.ShapeDtypeStruct((S, D), BF16),
                   jax.ShapeDtypeStruct((S, D), BF16), jax.ShapeDtypeStruct((S, P), BF16)],
        scratch_shapes=[pltpu.VMEM(s, d) for s, d in scratch],
        compiler_params=_params(("arbitrary",), blocks, scratch, temps=[((TM, D), F32)] * 2),
    )(h2, p, target, w_gate, w_proj, b_gate, g_final)


def _ffn_bwd(u0, conv_u0, dact, conv_w, S, TM, F, TC, side=None):
    H = FFN_ROWS
    K = conv_w.shape[0]
    nF, nI = F // TC, S // TM

    def body(xc_ref, cc_ref, cn_ref, dc_ref, dn_ref, wg_ref, wu_ref, o_ref, dwg_ref, dwu_ref, ext_d):
        i = pl.program_id(1)
        @pl.when(i == 0)
        def _():
            dwg_ref[...] = jnp.zeros_like(dwg_ref)
            dwu_ref[...] = jnp.zeros_like(dwu_ref)
        last = (i < nI - 1).astype(F32)

        def lane_chunk(c, carry):
            lo = pl.ds(pl.multiple_of(c * LANES, LANES), LANES)
            lg, lu = lo, pl.ds(pl.multiple_of(TC + c * LANES, LANES), LANES)
            wg, wu = _bcast_taps(wg_ref, K, lo), _bcast_taps(wu_ref, K, lo)
            for r0 in range(0, TM + H, FFN_ROWS):
                if r0 < TM:
                    rows = pl.ds(r0, FFN_ROWS)
                    g, u, da = cc_ref[rows, lg], cc_ref[rows, lu], dc_ref[rows, lo].astype(F32)
                else:
                    g, u, da = cn_ref[:, lg], cn_ref[:, lu], dn_ref[:, lo].astype(F32) * last
                g, u = g.astype(F32), u.astype(F32)
                s = _sigmoid(g)
                ext_d[pl.ds(r0, FFN_ROWS), lg] = da * u * s * (1.0 + g * (1.0 - s))
                ext_d[pl.ds(r0, FFN_ROWS), lu] = da * g * s
            sums_g, sums_u = [None] * K, [None] * K
            for r0 in range(0, TM, FFN_ROWS):
                xg = xc_ref[pl.ds(r0, FFN_ROWS), lg].astype(F32)
                xu = xc_ref[pl.ds(r0, FFN_ROWS), lu].astype(F32)
                g = u = None
                for k in range(K):
                    rows = pl.ds(K - 1 - k + r0, FFN_ROWS)
                    dg, du = ext_d[rows, lg], ext_d[rows, lu]
                    tg, tu = dg * wg[k], du * wu[k]
                    g, u = (tg, tu) if g is None else (g + tg, u + tu)
                    pg, pu = xg * dg, xu * du
                    sums_g[k] = pg if sums_g[k] is None else sums_g[k] + pg
                    sums_u[k] = pu if sums_u[k] is None else sums_u[k] + pu
                o_ref[pl.ds(r0, FFN_ROWS), lg] = g.astype(BF16)
                o_ref[pl.ds(r0, FFN_ROWS), lu] = u.astype(BF16)
            for k in range(K):
                dwg_ref[pl.ds(k, 1), lo] += _rsum(sums_g[k])
                dwu_ref[pl.ds(k, 1), lo] += _rsum(sums_u[k])
            return carry
        lax.fori_loop(0, TC // LANES, lane_chunk, 0)

    blocks = [((TM, 2 * TC), BF16), ((TM, 2 * TC), BF16), ((H, 2 * TC), BF16), ((TM, TC), BF16), ((H, TC), BF16),
              ((K, TC), F32), ((K, TC), F32), ((TM, 2 * TC), BF16), ((K, TC), F32), ((K, TC), F32)]
    scratch = [((TM + H, 2 * TC), F32)]
    nxt = lambda j, i: (jnp.minimum((i + 1) * (TM // H), S // H - 1), j)
    tile = pl.BlockSpec((TM, 2 * TC), lambda j, i: (i, j))
    taps_out = pl.BlockSpec((K, TC), lambda j, i: (0, j))
    outs, side_outs = _call(
        body, side, name="ffn_bwd", grid=(nF, nI),
        in_specs=[tile, tile, pl.BlockSpec((H, 2 * TC), nxt),
                  pl.BlockSpec((TM, TC), lambda j, i: (i, j)), pl.BlockSpec((H, TC), nxt),
                  pl.BlockSpec((K, TC), lambda j, i: (0, j)), pl.BlockSpec((K, TC), lambda j, i: (0, j + nF))],
        out_specs=[tile, taps_out, taps_out],
        out_shape=[jax.ShapeDtypeStruct((S, 2 * F), BF16), jax.ShapeDtypeStruct((K, F), F32),
                   jax.ShapeDtypeStruct((K, F), F32)],
        scratch=scratch,
        params=_params(("arbitrary", "arbitrary"), blocks, scratch),
        args=[u0, conv_u0, conv_u0, dact, dact, conv_w, conv_w])
    return outs if side is None else (outs, side_outs)


def _mixer_bwd_ln(dcat, a1, ln_g, ln_b, S, TM, A):
    def body(dc_ref, a1_ref, g_ref, b_ref, da1_ref, acc_ref):
        @pl.when(pl.program_id(0) == 0)
        def _():
            acc_ref[...] = jnp.zeros_like(acc_ref)
        a1 = a1_ref[...]
        g = g_ref[...]
        mu = jnp.mean(a1, axis=-1, keepdims=True)
        d = a1 - mu
        rstd = lax.rsqrt(jnp.mean(d * d, axis=-1, keepdims=True) + EPS)
        nh = d * rstd
        a2 = nh * g + b_ref[...]
        s = _sigmoid(a2)
        da2 = dc_ref[...].astype(F32) * s * (1.0 + a2 * (1.0 - s))
        dnh = da2 * g
        da1 = rstd * (dnh - jnp.mean(dnh, axis=-1, keepdims=True)
                      - nh * jnp.mean(dnh * nh, axis=-1, keepdims=True))
        da1_ref[...] = da1
        acc_ref[pl.ds(0, 1), :] += _rsum(da2 * nh)
        acc_ref[pl.ds(1, 1), :] += _rsum(da2)
        acc_ref[pl.ds(2, 1), :] += _rsum(da1)

    blocks = [((TM, A), BF16), ((TM, A), F32), ((TM, A), F32), ((4, A), F32)]
    return pl.pallas_call(
        body, name="mixer_bwd_ln", grid=(S // TM,),
        in_specs=[pl.BlockSpec((TM, A), lambda i: (i, 0)), pl.BlockSpec((TM, A), lambda i: (i, 0)),
                  pl.BlockSpec((1, A), lambda i: (0, 0)), pl.BlockSpec((1, A), lambda i: (0, 0))],
        out_specs=[pl.BlockSpec((TM, A), lambda i: (i, 0)), pl.BlockSpec((4, A), lambda i: (0, 0))],
        out_shape=[jax.ShapeDtypeStruct((S, A), F32), jax.ShapeDtypeStruct((4, A), F32)],
        compiler_params=_params(("arbitrary",), blocks, temps=[((TM, A), F32)] * 12),
    )(dcat, a1, ln_g, ln_b)


def _mixer_bwd_conv(z, dcat, da1, conv_a_w, conv_b_w, S, TM, A, side=None):
    H = 32
    KA, KB = conv_a_w.shape[0], conv_b_w.shape[0]
    nI = S // TM
    n_chunks = A // LANES
    RB = _pick(TM, (64, 32))

    def body(zc_ref, zp_ref, zn_ref, dbc_ref, dbn_ref, d1c_ref, d1n_ref, wa_ref, wb_ref,
             dz_ref, dwa_ref, dwb_ref, ext_a0, ext_d1, ext_cb, ext_dc, da0_sc, shifted_d, shifted_a):
        i = pl.program_id(0)
        @pl.when(i == 0)
        def _():
            dwa_ref[...] = jnp.zeros_like(dwa_ref)
            dwb_ref[...] = jnp.zeros_like(dwb_ref)
        first = (i > 0).astype(F32)
        last = (i < nI - 1).astype(F32)
        zc = zc_ref[...].astype(F32)
        zp = zp_ref[...].astype(F32) * first
        a_val, a_gate = zc[:, 0:A], zc[:, A:2 * A]
        b_gate, c_gate, b_h = zc[:, 2 * A:3 * A], zc[:, 3 * A:4 * A], zc[:, 4 * A:5 * A]
        sig = _sigmoid(a_gate)
        ext_a0[pl.ds(0, H), :] = zp[:, 0:A] * _sigmoid(zp[:, A:2 * A])
        ext_a0[pl.ds(H, TM), :] = a_val * sig
        ext_d1[pl.ds(0, TM), :] = d1c_ref[...]
        ext_d1[pl.ds(TM, H), :] = d1n_ref[...] * last
        ext_cb[pl.ds(0, H), :] = zp[:, 3 * A:4 * A] * zp[:, 4 * A:5 * A]
        ext_cb[pl.ds(H, TM), :] = c_gate * b_h
        dbx = dbc_ref[...].astype(F32)
        dcbc = dbx * b_gate
        ext_dc[pl.ds(0, TM), :] = dcbc
        ext_dc[pl.ds(TM, H), :] = dbn_ref[...].astype(F32) * zn_ref[...].astype(F32) * last

        def chunk(c, carry):
            cs = pl.ds(pl.multiple_of(c * LANES, LANES), LANES)
            _shift_copies(ext_d1, shifted_d, cs)
            _shift_copies(ext_a0, shifted_a, cs)
            for r0 in range(0, TM, RB):
                acc = None
                for k in range(KA):
                    term = _rows_at(ext_d1, shifted_d, KA - 1 - k + r0, RB, cs) * wa_ref[pl.ds(k, 1), cs]
                    acc = term if acc is None else acc + term
                da0_sc[pl.ds(r0, RB), cs] = acc
            for k in range(KA):
                acc = None
                for r0 in range(0, TM, RB):
                    term = (_rows_at(ext_a0, shifted_a, H - (KA - 1) + k + r0, RB, cs)
                            * ext_d1[pl.ds(r0, RB), cs])
                    acc = term if acc is None else acc + term
                dwa_ref[pl.ds(k, 1), cs] += _rsum(acc)
            return carry
        lax.fori_loop(0, n_chunks, chunk, 0)

        da0 = da0_sc[...]
        dz_ref[:, 0:A] = (da0 * sig).astype(BF16)
        dz_ref[:, A:2 * A] = (da0 * a_val * sig * (1.0 - sig)).astype(BF16)
        cbc = _taps_causal(ext_cb, wb_ref, KB, H, TM, slice(None))
        dz_ref[:, 2 * A:3 * A] = (dbx * cbc).astype(BF16)
        dcb = _taps_anticausal(ext_dc, wb_ref, KB, TM, slice(None))
        dz_ref[:, 3 * A:4 * A] = (dcb * b_h).astype(BF16)
        dz_ref[:, 4 * A:5 * A] = (dcb * c_gate).astype(BF16)
        grads = _tap_grads(ext_cb, dcbc, KB, H, TM, slice(None))
        for k in range(KB):
            dwb_ref[pl.ds(k, 1), :] += grads[k]

    blocks = [((TM, 5 * A), BF16), ((H, 5 * A), BF16), ((H, A), BF16), ((TM, A), BF16), ((H, A), BF16),
              ((TM, A), F32), ((H, A), F32), ((KA, A), F32), ((KB, A), F32),
              ((TM, 5 * A), BF16), ((KA, A), F32), ((KB, A), F32)]
    scratch = ([((H + TM, A), F32)] * 4 + [((TM, A), F32)]
               + [((SUBLANES - 1, H + TM - SUBLANES, LANES), F32)] * 2)
    vec = lambda r: pl.BlockSpec((r, A), lambda i: (0, 0))
    outs, side_outs = _call(
        body, side, name="mixer_bwd_conv", grid=(nI,),
        in_specs=[pl.BlockSpec((TM, 5 * A), lambda i: (i, 0)),
                  pl.BlockSpec((H, 5 * A), _prev_rows(TM, H, 0)),
                  pl.BlockSpec((H, A), _next_rows(S, TM, H, 2)),
                  pl.BlockSpec((TM, A), lambda i: (i, 1)),
                  pl.BlockSpec((H, A), _next_rows(S, TM, H, 1)),
                  pl.BlockSpec((TM, A), lambda i: (i, 0)),
                  pl.BlockSpec((H, A), _next_rows(S, TM, H, 0)),
                  vec(KA), vec(KB)],
        out_specs=[pl.BlockSpec((TM, 5 * A), lambda i: (i, 0)), vec(KA), vec(KB)],
        out_shape=[jax.ShapeDtypeStruct((S, 5 * A), BF16), jax.ShapeDtypeStruct((KA, A), F32),
                   jax.ShapeDtypeStruct((KB, A), F32)],
        scratch=scratch,
        params=_params(("arbitrary",), blocks, scratch, temps=[((TM, 5 * A), F32)] * 2 + [((TM, A), F32)] * 14),
        args=[z, z, z, dcat, dcat, da1, da1, conv_a_w, conv_b_w])
    return outs if side is None else (outs, side_outs)


def _row_tile(R):
    return _pick(R, (256, 128, 64, 32, 16, 8))


def _scalars(*vals):
    return jnp.stack([jnp.asarray(v, jnp.int32) for v in vals])


def _cast_into_gathered(name, w, chip):
    R, C = w.shape
    TR = _row_tile(R)

    def body(s_ref, w_ref, o_ref):
        o_ref[...] = w_ref[...].astype(BF16)

    grid_spec = pltpu.PrefetchScalarGridSpec(
        num_scalar_prefetch=1, grid=(R // TR,),
        in_specs=[pl.BlockSpec((TR, C), lambda r, s: (r, 0))],
        out_specs=pl.BlockSpec((None, TR, C), lambda r, s: (s[0], r, 0)))
    return pl.pallas_call(body, name=name, grid_spec=grid_spec,
                          out_shape=jax.ShapeDtypeStruct((N_CHIPS, R, C), BF16),
                          compiler_params=_params(("arbitrary",), [((TR, C), F32), ((TR, C), BF16)]),
                          )(_scalars(chip), *_in_hbm([w]))


def _add_pair(name, dw, recv, c, chip):
    _, _, Rh, C = dw.shape
    TR = _row_tile(Rh)

    def body(s_ref, a_ref, b_ref, o_ref, ob_ref):
        s = a_ref[...] + b_ref[...]
        ob_ref[...] = s.astype(BF16)

        @pl.when(pl.program_id(1) == s_ref[1])
        def _():
            o_ref[...] = s

    grid_spec = pltpu.PrefetchScalarGridSpec(
        num_scalar_prefetch=1, grid=(Rh // TR, N_CHIPS),
        in_specs=[pl.BlockSpec((None, None, TR, C), lambda r, k, s: (k, s[0], r, 0)),
                  pl.BlockSpec((None, TR, C), lambda r, k, s: (k, r, 0))],
        out_specs=[pl.BlockSpec((TR, C), lambda r, k, s: (r, 0)),
                   pl.BlockSpec((None, TR, C), lambda r, k, s: (k, r, 0))])
    return pl.pallas_call(body, name=name, grid_spec=grid_spec,
                          out_shape=[jax.ShapeDtypeStruct((Rh, C), F32),
                                     jax.ShapeDtypeStruct((N_CHIPS, Rh, C), BF16)],
                          compiler_params=_params(("arbitrary", "arbitrary"), [((TR, C), F32)] * 4),
                          )(_scalars(c, chip), *_in_hbm([dw, recv]))


def _add_chips(name, own, recv, c):
    Rh, C = own.shape
    TR = _row_tile(Rh)

    def body(s_ref, p_ref, r_ref, o_ref):
        o_ref[...] = ((p_ref[...] + r_ref[0].astype(F32)) + r_ref[1].astype(F32)) + r_ref[2].astype(F32)

    grid_spec = pltpu.PrefetchScalarGridSpec(
        num_scalar_prefetch=1, grid=(Rh // TR,),
        in_specs=[pl.BlockSpec((TR, C), lambda r, s: (r, 0)),
                  pl.BlockSpec((N_CHIPS - 1, TR, C), lambda r, s: (0, r, 0))],
        out_specs=pl.BlockSpec((None, TR, C), lambda r, s: (s[0], r, 0)))
    return pl.pallas_call(body, name=name, grid_spec=grid_spec,
                          out_shape=jax.ShapeDtypeStruct((2, Rh, C), F32),
                          compiler_params=_params(("arbitrary",), [((N_CHIPS + 1, TR, C), F32)]),
                          )(_scalars(c), *_in_hbm([own, recv]))


def _sum_devices(name, parts):
    _, R, C = parts.shape

    def body(p_ref, o_ref):
        acc = p_ref[0]
        for d in range(1, N_DEV):
            acc = acc + p_ref[d]
        o_ref[...] = acc

    return pl.pallas_call(body, name=name, out_shape=jax.ShapeDtypeStruct((R, C), F32),
                          in_specs=[pl.BlockSpec(memory_space=pltpu.VMEM)],
                          out_specs=pl.BlockSpec(memory_space=pltpu.VMEM))(parts)


def _adamw(name, w, g, m, v, copy_grad=False):
    R, C = w.shape
    TR = _pick(R, (128, 64, 32, 16, 8))
    c1 = 1.0 - ADAM_B1 ** ADAM_STEP
    c2 = 1.0 - ADAM_B2 ** ADAM_STEP
    n_out = 4 if copy_grad else 3

    def body(w_ref, g_ref, m_ref, v_ref, d_ref, nm_ref, nv_ref, *g_out):
        g_ = g_ref[...]
        nm = ADAM_B1 * m_ref[...] + (1.0 - ADAM_B1) * g_
        nv = ADAM_B2 * v_ref[...] + (1.0 - ADAM_B2) * (g_ * g_)
        d_ref[...] = -ADAM_LR * ((nm / c1) / (jnp.sqrt(nv / c2) + ADAM_EPS) + ADAM_WD * w_ref[...])
        nm_ref[...] = nm
        nv_ref[...] = nv
        for ref in g_out:
            ref[...] = g_

    spec = pl.BlockSpec((TR, C), lambda r: (r, 0))
    shp = jax.ShapeDtypeStruct((R, C), F32)
    return pl.pallas_call(body, name=name, grid=(R // TR,), in_specs=[spec] * 4, out_specs=[spec] * n_out,
                          out_shape=[shp] * n_out,
                          compiler_params=_params(("arbitrary",), [((TR, C), F32)] * (4 + n_out)),
                          )(*_in_hbm([w, g, m, v]))


def _place():
    x, y, c = lax.axis_index("x"), lax.axis_index("y"), lax.axis_index("c")
    others = [(1 - x, y), (x, 1 - y), (1 - x, 1 - y)]
    return x, y, c, others


def _allgather_small(name, block):
    R, C = block.shape

    def body(x_ref, out_ref, send_sems, recv_sems, local_sem):
        x, y, c, chips = _place()
        me, sibling = (x, y, c), (x, y, 1 - c)

        def rows(px, py, pc):
            return out_ref.at[4 * px + 2 * py + pc]

        def copy(k, blk, to, src=None):
            return pltpu.make_async_remote_copy(
                src_ref=rows(*blk) if src is None else src, dst_ref=rows(*blk),
                send_sem=send_sems.at[k], recv_sem=recv_sems.at[k], device_id=to, device_id_type=MESH)

        mine = pltpu.make_async_copy(x_ref, rows(*me), local_sem)
        mine.start()
        first = [copy(0, me, sibling, src=x_ref)]
        first += [copy(1 + j, me, (*chip, c), src=x_ref) for j, chip in enumerate(chips)]
        for cp in first:
            cp.start()
        passed = [copy(4 + j, (*chip, c), sibling) for j, chip in enumerate(chips)]
        for j, chip in enumerate(chips):
            copy(1 + j, (*chip, c), me).wait_recv()
            passed[j].start()
        copy(0, sibling, me).wait_recv()
        for j, chip in enumerate(chips):
            copy(4 + j, (*chip, 1 - c), me).wait_recv()
        for cp in first + passed:
            cp.wait_send()
        mine.wait()

    return pl.pallas_call(
        body, name=name, out_shape=jax.ShapeDtypeStruct((N_DEV, R, C), F32),
        in_specs=[pl.BlockSpec(memory_space=pltpu.VMEM)], out_specs=pl.BlockSpec(memory_space=pltpu.VMEM),
        scratch_shapes=[pltpu.SemaphoreType.DMA((7,)), pltpu.SemaphoreType.DMA((7,)), pltpu.SemaphoreType.DMA],
    )(block)


def _gather_side(bufs, across, within):
    def rows(ref, chip, half, piece):
        _, r0, n = piece
        return ref.at[2 * chip[0] + chip[1], pl.ds(half * (ref.shape[1] // 2) + r0, n)]

    def copies(ins, outs, send_sems, recv_sems, base):
        x, y, c, chips = _place()
        sibling = (x, y, 1 - c)
        pairs = []

        def add(k, src, dst, to, arrival):
            mk = lambda s, d, dev: pltpu.make_async_remote_copy(
                src_ref=s, dst_ref=d, send_sem=send_sems.at[base + k], recv_sem=recv_sems.at[base + k],
                device_id=dev, device_id_type=MESH)
            pairs.append((mk(src, dst, to), mk(arrival, arrival, (x, y, c))))

        for p, piece in enumerate(across):
            ref = outs[piece[0]]
            for j, chip in enumerate(chips):
                mine = rows(ref, (x, y), c, piece)
                add(3 * p + j, mine, mine, (*chip, c), rows(ref, chip, c, piece))
        for q, piece in enumerate(within):
            ref = outs[piece[0]]
            for j, chip in enumerate(chips):
                held = rows(ref, chip, c, piece)
                add(3 * (len(across) + q) + j, held, held, sibling, rows(ref, chip, 1 - c, piece))
        return pairs

    def start(*refs):
        for send, _ in copies(*refs):
            send.start()

    def wait(*refs):
        pairs = copies(*refs)
        for _, arrival in pairs:
            arrival.wait_recv()
        for send, _ in pairs:
            send.wait_send()

    return _Side(list(bufs), [jax.ShapeDtypeStruct(b.shape, b.dtype) for b in bufs],
                 3 * (len(across) + len(within)), start, wait, aliases=tuple((i, i) for i in range(len(bufs))))


def _chip_exchange(parts):
    n = len(parts)

    def copies(ins, outs, send_sems, recv_sems, base):
        x, y, c, chips = _place()
        return [pltpu.make_async_remote_copy(
            src_ref=ins[a].at[2 * chip[0] + chip[1]], dst_ref=outs[a].at[j],
            send_sem=send_sems.at[base + 3 * a + j], recv_sem=recv_sems.at[base + 3 * a + j],
            device_id=(*chip, c), device_id_type=MESH) for a in range(n) for j, chip in enumerate(chips)]

    return _Side(list(parts), [jax.ShapeDtypeStruct((N_CHIPS - 1,) + p.shape[1:], p.dtype) for p in parts],
                 3 * n, *_start_wait(copies))


def _pair_exchange(grads):
    def copies(ins, outs, send_sems, recv_sems, base):
        x, y, c, _ = _place()
        return [pltpu.make_async_remote_copy(
            src_ref=ins[a].at[:, 1 - c], dst_ref=outs[a], send_sem=send_sems.at[base + a],
            recv_sem=recv_sems.at[base + a], device_id=(x, y, 1 - c), device_id_type=MESH)
            for a in range(len(grads))]

    return _Side(list(grads), [jax.ShapeDtypeStruct((N_CHIPS,) + g.shape[2:], F32) for g in grads],
                 len(grads), *_start_wait(copies))


def _start_wait(copies):
    def start(*refs):
        for cp in copies(*refs):
            cp.start()

    def wait(*refs):
        cps = copies(*refs)
        for cp in cps:
            cp.wait_recv()
        for cp in cps:
            cp.wait_send()
    return start, wait


def _both(first, second):
    n_in, n_out = len(first.ins), len(first.out_shapes)

    def run(which):
        def go(ins, outs, send_sems, recv_sems, base):
            getattr(first, which)(ins[:n_in], outs[:n_out], send_sems, recv_sems, base)
            getattr(second, which)(ins[n_in:], outs[n_out:], send_sems, recv_sems, base + first.n_sems)
        return go

    aliases = first.aliases + tuple((a + n_in, b + n_out) for a, b in second.aliases)
    return _Side(first.ins + second.ins, first.out_shapes + second.out_shapes,
                 first.n_sems + second.n_sems, run("start"), run("wait"), aliases)


def _share_halves(halves):
    def copies(ins, outs, send_sems, recv_sems, base):
        x, y, c, _ = _place()
        pairs = []
        for a in range(len(halves)):
            mk = lambda s, d, dev, a=a: pltpu.make_async_remote_copy(
                src_ref=s, dst_ref=d, send_sem=send_sems.at[base + a], recv_sem=recv_sems.at[base + a],
                device_id=dev, device_id_type=MESH)
            theirs = outs[a].at[1 - c]
            pairs.append((mk(outs[a].at[c], outs[a].at[c], (x, y, 1 - c)), mk(theirs, theirs, (x, y, c))))
        return pairs

    def start(*refs):
        for send, _ in copies(*refs):
            send.start()

    def wait(*refs):
        pairs = copies(*refs)
        for _, arrival in pairs:
            arrival.wait_recv()
        for send, _ in pairs:
            send.wait_send()

    return _Side(list(halves), [jax.ShapeDtypeStruct(h.shape, F32) for h in halves], len(halves), start, wait,
                 aliases=tuple((i, i) for i in range(len(halves))))


def _pack(arrays):
    pieces = []
    for a in arrays:
        flat = a.reshape(-1).astype(F32)
        pieces.append(jnp.pad(flat, (0, (-flat.size) % PACK_ALIGN)))
    return jnp.concatenate(pieces).reshape(-1, LANES)


def _unpack(buf, shapes):
    lead = buf.shape[:-2]
    flat = buf.reshape(lead + (-1,))
    out, off = [], 0
    for shp in shapes:
        size = 1
        for s in shp:
            size *= s
        out.append(flat[..., off:off + size].reshape(lead + tuple(shp)))
        off += size + (-size) % PACK_ALIGN
    return out


def _gather_channels(buf, shapes):
    per_chip = _unpack(buf[0::2], shapes)
    return [jnp.transpose(a, (1, 0, 2)).reshape(a.shape[1], -1) for a in per_chip]


def _mm_tile(n, rows, limit_bytes=6 * 1024 * 1024):
    for t in (1408, 1280, 1024, 640, 512, 384, 256, 128):
        if n % t == 0 and rows * t * 2 <= limit_bytes:
            return t
    raise ValueError(f"no column tile for {n} x {rows}")


def kernel(x, p, norm_mix_g, w_in, conv_a_w, conv_a_b, ln_a_g, ln_a_b, conv_b_w, w_out, norm_ffn_g, w_up, conv_ffn_w, w_down, w_ple_gate, b_ple_gate, w_ple_proj, norm_final_g, loss_target, m_norm_mix_g, m_w_in, m_conv_a_w, m_conv_a_b, m_ln_a_g, m_ln_a_b, m_conv_b_w, m_w_out, m_norm_ffn_g, m_w_up, m_conv_ffn_w, m_w_down, m_w_ple_gate, m_b_ple_gate, m_w_ple_proj, m_norm_final_g, v_norm_mix_g, v_w_in, v_conv_a_w, v_conv_a_b, v_ln_a_g, v_ln_a_b, v_conv_b_w, v_w_out, v_norm_ffn_g, v_w_up, v_conv_ffn_w, v_w_down, v_w_ple_gate, v_b_ple_gate, v_w_ple_proj, v_norm_final_g):
    S, D = x.shape[1], x.shape[2]
    P = p.shape[3]
    A = conv_a_b.shape[1]
    F = w_down.shape[1] * N_CHIPS
    KA, KB, KF = conv_a_w.shape[1], conv_b_w.shape[1], conv_ffn_w.shape[1]
    xi, yi, ci = lax.axis_index("x"), lax.axis_index("y"), lax.axis_index("c")
    chip = 2 * xi + yi

    TM = _pick(S, (512, 256, 128))
    TL = _pick(S, (1024, 512, 256, 128))
    TE = _pick(S, (256, 128))
    TC = _pick(2 * F // N_CHIPS, (1408, 1024, 512, 256, 128))
    ffn_place = _pair_tile(F // TC)

    x2, p2, t2 = x.reshape(S, D), p.reshape(S, P), loss_target.reshape(S, D)
    gfin = norm_final_g.reshape(1, D)

    big = dict(w_in=w_in[0], w_out=w_out[0], w_up=w_up[0], w_down=w_down[0],
               w_ple_gate=w_ple_gate[0], w_ple_proj=w_ple_proj[0])
    names = list(big)
    buf = {n: _cast_into_gathered("cast_" + n, big[n], chip) for n in names}
    half = {n: big[n].shape[0] // 2 for n in names}
    up_a = half["w_up"] // 2
    (w_in3,) = _comm_only("gather_w_in_across", _gather_side([buf["w_in"]], [(0, 0, half["w_in"])], []))
    (w_in3,) = _comm_only("gather_w_in_within", _gather_side([w_in3], [], [(0, 0, half["w_in"])]))

    tap_shapes = [(KA, A // N_CHIPS), (KB, A // N_CHIPS), (KF, 2 * F // N_CHIPS)]
    taps = _allgather_small("allgather_taps", _pack([conv_a_w[0], conv_b_w[0], conv_ffn_w[0]]))
    conv_a_f, conv_b_f, conv_ffn_f = _gather_channels(taps, tap_shapes)

    def rms_prologue(rows, row_r, vec_r, ro_r):
        h = row_r[0][rows, :]
        hn = (h * _rms_stats(h) * vec_r[0][...]).astype(BF16)
        ro_r[0][rows, :] = hn
        return [hn]

    def cast_prologue(rows, row_r, vec_r, ro_r):
        hb = row_r[0][rows, :].astype(BF16)
        ro_r[0][rows, :] = hb
        return [hb]

    plain = lambda accs, tile_r: [accs[0]]
    residual = lambda accs, tile_r: [tile_r[0][...] + accs[0]]

    (z, hn1), (w_out_t, w_up_t) = _rows_mm(
        "in_proj", S, TL, 5 * A, _mm_tile(5 * A // N_CHIPS, D), row_ins=[x2], vec_ins=[norm_mix_g],
        weights=[(w_in3, "nn3")], tile_outs=[BF16], row_outs=[(D, BF16)], prologue=rms_prologue, epilogue=plain,
        side=_gather_side([buf["w_out"], buf["w_up"]], [(0, 0, half["w_out"]), (1, 0, up_a)], []))
    (a1, cat), (w_out3, w_up_t) = _mixer_fwd(
        z, conv_a_f, conv_a_b, ln_a_g, ln_a_b, conv_b_f, S, TE, A,
        side=_gather_side([w_out_t, w_up_t], [(1, up_a, half["w_up"] - up_a)],
                          [(0, 0, half["w_out"]), (1, 0, up_a)]))
    w_out_f = w_out3.reshape(2 * A, D)
    (h1,), (w_up3, w_proj_t) = _rows_mm(
        "out_proj", S, TL, D, _mm_tile(D, 2 * A), row_ins=[cat], weights=[(w_out_f, "nn2")],
        tile_ins=[x2], tile_outs=[F32], epilogue=residual,
        side=_gather_side([w_up_t, buf["w_ple_proj"]], [(1, 0, half["w_ple_proj"])],
                          [(0, up_a, half["w_up"] - up_a)]))
    (u0, hn2), (w_down_t, w_gate_t, w_proj3) = _rows_mm(
        "up_proj", S, TL, 2 * F, TC, row_ins=[h1], vec_ins=[norm_ffn_g],
        weights=[(w_up3, "nn3")], tile_outs=[BF16], row_outs=[(D, BF16)],
        prologue=rms_prologue, epilogue=plain, place=ffn_place,
        side=_gather_side([buf["w_down"], buf["w_ple_gate"], w_proj_t],
                          [(0, 0, half["w_down"]), (1, 0, half["w_ple_gate"])], [(2, 0, half["w_ple_proj"])]))
    (act, conv_u0), (w_down3, w_gate3) = _ffn_act(
        u0, conv_ffn_f, S, TM, F, TC,
        side=_gather_side([w_down_t, w_gate_t], [], [(0, 0, half["w_down"]), (1, 0, half["w_ple_gate"])]))
    w_down_f = w_down3.reshape(F, D)
    w_gate_f = w_gate3.reshape(D, D)
    w_proj_f = jnp.transpose(w_proj3, (1, 0, 2)).reshape(P, D)
    (h2,) = _rows_mm("down_proj", S, TL, D, _mm_tile(D, F), row_ins=[act], weights=[(w_down_f, "nn2")],
                     tile_ins=[h1], tile_outs=[F32], epilogue=residual)

    loss_part, g_norm_final, g_b_gate, dh3, dpre, dpp, h2b, pb = _ple_loss(
        h2, p2, t2, w_gate_f, w_proj_f, b_ple_gate, gfin, S, TE)

    TK = _pick(S, (2048, 1024, 512, 256, 128))
    wt = lambda n: _pick(n, (1408, 1280, 1024, 512, 256, 128))
    chip_sums, from_chips = {}, {}

    def to_sibling(parts):
        ns = list(parts)
        halves = [parts[n].reshape(N_CHIPS, 2, big[n].shape[0] // 2, big[n].shape[1]) for n in ns]
        return ns, halves, _pair_exchange(halves)

    def to_chips(ns, halves, from_sibling):
        sums = [_add_pair("pair_sum_" + n, h, r, ci, chip) for n, h, r in zip(ns, halves, from_sibling)]
        for n, (s, _) in zip(ns, sums):
            chip_sums[n] = s
        return ns, _chip_exchange([b for _, b in sums])

    def landed(ns, side_outs):
        for n, r in zip(ns, side_outs):
            from_chips[n] = r

    ns, halves, side = to_sibling(dict(
        w_ple_gate=_tn_mm("dw_ple_gate", h2b, dpre, wt(D), wt(D), TK),
        w_ple_proj=_tn_mm("dw_ple_proj", pb, dpp, wt(P), wt(D // N_CHIPS), TK, cols_per_chip=D // N_CHIPS)))
    (dh2,), got = _rows_mm("ple_bwd", S, TL, D, _mm_tile(D, D), row_ins=[dpre], weights=[(w_gate_f, "nt2")],
                           tile_ins=[dh3], tile_outs=[F32], epilogue=residual, side=side)
    ple_ns, ple_chips = to_chips(ns, halves, got)
    (dact, dh2b), got = _rows_mm("down_bwd", S, TL, F, _mm_tile(F, D), row_ins=[dh2], weights=[(w_down_f, "nt2")],
                                 tile_outs=[BF16], row_outs=[(D, BF16)], prologue=cast_prologue, epilogue=plain,
                                 side=ple_chips)
    landed(ple_ns, got)
    ns, halves, side = to_sibling(dict(w_down=_tn_mm("dw_down", act, dh2b, wt(F), wt(D), TK)))
    (du0, g_conv_gate, g_conv_up), got = _ffn_bwd(u0, conv_u0, dact, conv_ffn_f, S, TM, F, TC, side=side)
    down_ns, down_chips = to_chips(ns, halves, got)
    g_conv_ffn = jnp.concatenate([g_conv_gate, g_conv_up], axis=1)
    ns, halves, side = to_sibling(dict(
        w_up=_tn_mm("dw_up", hn2, du0, wt(D), TC, TK, cols_per_chip=2 * F // N_CHIPS, place=ffn_place)))

    def up_bwd_epilogue(acc, rows, row_r, vec_r, ro_r, ao_r):
        dh, dg = _rms_bwd(row_r[0][rows, :], vec_r[0][...], acc)
        dh1_ = row_r[1][rows, :] + dh
        ro_r[0][rows, :] = dh1_
        ro_r[1][rows, :] = dh1_.astype(BF16)
        ao_r[0][...] += dg

    (dh1, dh1b, g_norm_ffn), got = _kloop_mm(
        "up_bwd", S, TM, du0, w_up3, TC, row_ins=[h1, dh2], vec_ins=[norm_ffn_g],
        row_outs=[(D, F32), (D, BF16)], acc_outs=[(1, D)], epilogue=up_bwd_epilogue, place=ffn_place,
        side=_both(down_chips, side))
    landed(down_ns, got[:len(down_ns)])
    up_ns, up_chips = to_chips(ns, halves, got[len(down_ns):])
    ns, halves, side = to_sibling(dict(w_out=_tn_mm("dw_out", cat, dh1b, wt(2 * A), wt(D), TK)))
    (dcat,), got = _rows_mm("out_bwd", S, TL, 2 * A, _mm_tile(2 * A, D), row_ins=[dh1b],
                            weights=[(w_out_f, "nt2")], tile_outs=[BF16], epilogue=plain, side=side)
    out_ns, out_chips = to_chips(ns, halves, got)
    da1, ln_sums = _mixer_bwd_ln(dcat, a1, ln_a_g, ln_a_b, S, TE, A)
    (dz, g_conv_a, g_conv_b), got = _mixer_bwd_conv(z, dcat, da1, conv_a_f, conv_b_f, S, TE, A,
                                                    side=_both(up_chips, out_chips))
    landed(up_ns + out_ns, got)
    ns, halves, side = to_sibling(dict(
        w_in=_tn_mm("dw_in", hn1, dz, wt(D), wt(5 * A // N_CHIPS), TK, cols_per_chip=5 * A // N_CHIPS)))
    ns, side = to_chips(ns, halves, _comm_only("grads_exchange_pairs_in", side))

    def in_bwd_epilogue(acc, rows, row_r, vec_r, ro_r, ao_r):
        dh, dg = _rms_bwd(row_r[0][rows, :], vec_r[0][...], acc)
        ro_r[0][rows, :] = row_r[1][rows, :] + dh
        ao_r[0][...] += dg

    early = [n for n in names if n != "w_in"]
    early_halves = [_add_chips("chip_sum_" + n, chip_sums[n], from_chips[n], ci) for n in early]
    (dx, g_norm_mix), got = _kloop_mm(
        "in_bwd", S, TM, dz, w_in3, _mm_tile(5 * A // N_CHIPS, D), row_ins=[x2, dh1],
        vec_ins=[norm_mix_g], row_outs=[(D, F32)], acc_outs=[(1, D)], epilogue=in_bwd_epilogue,
        side=_both(side, _share_halves(early_halves)))
    landed(ns, got[:1])
    shared = dict(zip(early, got[1:]))
    (shared["w_in"],) = _comm_only("grads_share_w_in", _share_halves(
        [_add_chips("chip_sum_w_in", chip_sums["w_in"], from_chips["w_in"], ci)]))

    reduced = [shared[n] for n in names]
    moments = dict(w_in=(m_w_in, v_w_in), w_out=(m_w_out, v_w_out), w_up=(m_w_up, v_w_up),
                   w_down=(m_w_down, v_w_down), w_ple_gate=(m_w_ple_gate, v_w_ple_gate),
                   w_ple_proj=(m_w_ple_proj, v_w_ple_proj))
    grads, deltas, new_m, new_v = {}, {}, {}, {}
    for n, g in zip(names, reduced):
        d_, m_, v_, g = _adamw("adamw_" + n, big[n], g.reshape(big[n].shape), moments[n][0][0], moments[n][1][0],
                               copy_grad=True)
        grads[n], deltas[n], new_m[n], new_v[n] = g[None], d_[None], m_[None], v_[None]

    small = ["norm_mix_g", "conv_a_w", "conv_a_b", "ln_a_g", "ln_a_b", "conv_b_w", "norm_ffn_g",
             "conv_ffn_w", "b_ple_gate", "norm_final_g"]
    small_part = [g_norm_mix, g_conv_a, ln_sums[2:3], ln_sums[0:1], ln_sums[1:2], g_conv_b, g_norm_ffn,
                  g_conv_ffn, g_b_gate, g_norm_final]
    full_shapes = [a.shape for a in small_part]
    summed = _sum_devices("small_grads_sum", _allgather_small("allgather_small_grads", _pack(small_part)))
    small_g = dict(zip(small, _unpack(summed, full_shapes)))
    for n, width in (("conv_a_w", A), ("conv_b_w", A), ("conv_ffn_w", 2 * F)):
        small_g[n] = lax.dynamic_slice_in_dim(small_g[n], chip * (width // N_CHIPS), width // N_CHIPS, axis=1)
    small_w = dict(norm_mix_g=(norm_mix_g, m_norm_mix_g, v_norm_mix_g), conv_a_w=(conv_a_w, m_conv_a_w, v_conv_a_w),
                   conv_a_b=(conv_a_b, m_conv_a_b, v_conv_a_b), ln_a_g=(ln_a_g, m_ln_a_g, v_ln_a_g),
                   ln_a_b=(ln_a_b, m_ln_a_b, v_ln_a_b), conv_b_w=(conv_b_w, m_conv_b_w, v_conv_b_w),
                   norm_ffn_g=(norm_ffn_g, m_norm_ffn_g, v_norm_ffn_g),
                   conv_ffn_w=(conv_ffn_w, m_conv_ffn_w, v_conv_ffn_w),
                   b_ple_gate=(b_ple_gate, m_b_ple_gate, v_b_ple_gate),
                   norm_final_g=(norm_final_g, m_norm_final_g, v_norm_final_g))
    out_shapes = [small_w[n][0].shape for n in small]
    packed_g = _pack([small_g[n] for n in small])
    packed = [_pack([small_w[n][k] for n in small]) for k in range(3)]
    d_s, m_s, v_s = _adamw("adamw_small", packed[0], packed_g, packed[1], packed[2])
    for n, g, d_, m_, v_ in zip(small, _unpack(packed_g, out_shapes), _unpack(d_s, out_shapes),
                                _unpack(m_s, out_shapes), _unpack(v_s, out_shapes)):
        grads[n], deltas[n], new_m[n], new_v[n] = g, d_, m_, v_

    order = ["norm_mix_g", "w_in", "conv_a_w", "conv_a_b", "ln_a_g", "ln_a_b", "conv_b_w", "w_out", "norm_ffn_g",
             "w_up", "conv_ffn_w", "w_down", "w_ple_gate", "b_ple_gate", "w_ple_proj", "norm_final_g"]
    loss = lax.psum(loss_part[0, 0], ("x", "y", "c"))
    return (loss, dx.reshape(x.shape), *[grads[n] for n in order], *[deltas[n] for n in order],
            *[new_m[n] for n in order], *[new_v[n] for n in order])
```

```python
from typing import Callable, NamedTuple

import jax
import jax.numpy as jnp
from jax import lax
from jax.experimental import pallas as pl
from jax.experimental.pallas import tpu as pltpu

F32 = jnp.float32
BF16 = jnp.bfloat16
MESH = pl.DeviceIdType.MESH
ANY = pl.BlockSpec(memory_space=pl.ANY)

EPS = 1e-6
ADAM_LR = 0.001
ADAM_B1 = 0.9
ADAM_B2 = 0.999
ADAM_EPS = 1e-08
ADAM_WD = 0.01
ADAM_STEP = 10

N_CHIPS = 4
N_DEV = 8
LANES = 128
SUBLANES = 8
PACK_ALIGN = LANES * SUBLANES
ROW_CHUNK = 32
VMEM_CAP = 60 * 1024 * 1024
VMEM_SLACK = 6 * 1024 * 1024


def _pick(n, cands):
    for c in cands:
        if n % c == 0:
            return c
    raise ValueError(f"no tile of {cands} divides {n}")


def _nbytes(shape, dtype):
    n = 1
    for s in shape:
        if s is not None:
            n *= s
    return n * jnp.dtype(dtype).itemsize


def _params(sem, blocks, scratch=(), temps=()):
    est = (2 * sum(_nbytes(s, d) for s, d in blocks) + sum(_nbytes(s, d) for s, d in scratch)
           + sum(_nbytes(s, d) for s, d in temps))
    return pltpu.CompilerParams(dimension_semantics=sem,
                                vmem_limit_bytes=min(est + VMEM_SLACK, VMEM_CAP))


def _in_hbm(arrays):
    return [pltpu.with_memory_space_constraint(a, pltpu.HBM) for a in arrays]


def _sigmoid(x):
    return 1.0 / (1.0 + jnp.exp(-x))


def _rsum(x):
    return jnp.sum(x, axis=0, keepdims=True)


class _Side(NamedTuple):
    ins: list
    out_shapes: list
    n_sems: int
    start: Callable
    wait: Callable
    aliases: tuple = ()


def _call(body, side, *, name, grid, in_specs, out_specs, out_shape, scratch, params, args):
    vmem = [pltpu.VMEM(s, d) for s, d in scratch]
    if side is None:
        outs = pl.pallas_call(body, name=name, grid=grid, in_specs=in_specs, out_specs=out_specs,
                              out_shape=out_shape, scratch_shapes=vmem, compiler_params=params)(*args)
        return list(outs), []
    n_in, n_out, n_sc = len(in_specs), len(out_specs), len(scratch)
    ns_in, ns_out = len(side.ins), len(side.out_shapes)

    def carrier(*refs):
        pos = [0]
        def take(n):
            pos[0] += n
            return refs[pos[0] - n:pos[0]]
        ins, s_ins, outs, s_outs, scr = take(n_in), take(ns_in), take(n_out), take(ns_out), take(n_sc)
        send_sems, recv_sems = take(2)
        first = last = None
        for axis, extent in enumerate(grid):
            at_start, at_end = pl.program_id(axis) == 0, pl.program_id(axis) == extent - 1
            first = at_start if first is None else first & at_start
            last = at_end if last is None else last & at_end

        @pl.when(first)
        def _():
            side.start(s_ins, s_outs, send_sems, recv_sems, 0)
        body(*ins, *outs, *scr)

        @pl.when(last)
        def _():
            side.wait(s_ins, s_outs, send_sems, recv_sems, 0)

    outs = pl.pallas_call(
        carrier, name=name, grid=grid, in_specs=list(in_specs) + [ANY] * ns_in,
        out_specs=list(out_specs) + [ANY] * ns_out, out_shape=list(out_shape) + list(side.out_shapes),
        scratch_shapes=vmem + [pltpu.SemaphoreType.DMA((side.n_sems,)), pltpu.SemaphoreType.DMA((side.n_sems,))],
        input_output_aliases={n_in + a: n_out + b for a, b in side.aliases},
        compiler_params=params)(*args, *_in_hbm(side.ins))
    return list(outs[:n_out]), list(outs[n_out:])


def _comm_only(name, side):
    n_in = len(side.ins)

    def body(*refs):
        ins, outs = refs[:n_in], refs[n_in:n_in + len(side.out_shapes)]
        send_sems, recv_sems = refs[n_in + len(side.out_shapes):]
        side.start(ins, outs, send_sems, recv_sems, 0)
        side.wait(ins, outs, send_sems, recv_sems, 0)

    return pl.pallas_call(
        body, name=name, out_shape=list(side.out_shapes), in_specs=[ANY] * n_in,
        out_specs=[ANY] * len(side.out_shapes),
        scratch_shapes=[pltpu.SemaphoreType.DMA((side.n_sems,)), pltpu.SemaphoreType.DMA((side.n_sems,))],
        input_output_aliases=dict(side.aliases),
    )(*_in_hbm(side.ins))


def _rms_stats(x):
    return lax.rsqrt(jnp.mean(x * x, axis=-1, keepdims=True) + EPS)


def _rms_bwd(h, g, dout):
    r = _rms_stats(h)
    n = h * r
    dn = dout * g
    dh = r * (dn - n * jnp.mean(dn * n, axis=-1, keepdims=True))
    return dh, _rsum(dout * n)


def _identity(t):
    return t


def _chip_major(nb, place=_identity):
    return lambda i, j: (place(j) // nb, 0, place(j) % nb)


def _rows_mm(name, S, TM, N, TN, *, row_ins, vec_ins=(), weights, tile_ins=(), tile_outs, row_outs=(),
             prologue=None, epilogue, place=_identity, side=None):
    nI, nJ = S // TM, N // TN
    n_row, n_vec, n_w, n_tile = len(row_ins), len(vec_ins), len(weights), len(tile_ins)
    n_to, n_ro = len(tile_outs), len(row_outs)

    in_specs, blocks, scratch, ks = [], [], [], []
    for a in row_ins:
        in_specs.append(pl.BlockSpec((TM, a.shape[1]), lambda i, j: (i, 0)))
        blocks.append(((TM, a.shape[1]), a.dtype))
    for a in vec_ins:
        in_specs.append(pl.BlockSpec(a.shape, lambda i, j: (0, 0)))
        blocks.append((a.shape, a.dtype))
    for w, mode in weights:
        if mode == "nn2":
            k = w.shape[0]
            in_specs.append(pl.BlockSpec((k, TN), lambda i, j: (0, j)))
        elif mode == "nn3":
            k = w.shape[1]
            in_specs.append(pl.BlockSpec((None, k, TN), _chip_major(w.shape[2] // TN, place)))
        else:
            k = w.shape[1]
            in_specs.append(pl.BlockSpec((TN, k), lambda i, j: (j, 0)))
        ks.append(k)
        blocks.append(((k, TN), BF16))
        if prologue is not None:
            scratch.append(((TM, k), BF16))
    for a in tile_ins:
        in_specs.append(pl.BlockSpec((TM, TN), lambda i, j: (i, j)))
        blocks.append(((TM, TN), a.dtype))

    out_shape, out_specs = [], []
    for dt in tile_outs:
        out_shape.append(jax.ShapeDtypeStruct((S, N), dt))
        out_specs.append(pl.BlockSpec((TM, TN), lambda i, j: (i, j)))
        blocks.append(((TM, TN), dt))
    for width, dt in row_outs:
        out_shape.append(jax.ShapeDtypeStruct((S, width), dt))
        out_specs.append(pl.BlockSpec((TM, width), lambda i, j: (i, 0)))
        blocks.append(((TM, width), dt))

    modes = [m for _, m in weights]

    def body(*refs):
        pos = 0
        def take(n):
            nonlocal pos
            out = refs[pos:pos + n]
            pos += n
            return out
        row_r, vec_r, w_r, tile_r = take(n_row), take(n_vec), take(n_w), take(n_tile)
        to_r, ro_r, a_sc = take(n_to), take(n_ro), take(len(scratch))

        if prologue is None:
            a_sc = row_r[:n_w]
        else:
            @pl.when(pl.program_id(1) == 0)
            def _():
                def chunk(ci, carry):
                    rows = pl.ds(pl.multiple_of(ci * ROW_CHUNK, ROW_CHUNK), ROW_CHUNK)
                    for sc, a in zip(a_sc, prologue(rows, row_r, vec_r, ro_r)):
                        sc[rows, :] = a
                    return carry
                lax.fori_loop(0, TM // ROW_CHUNK, chunk, 0)

        accs = []
        for w_ref, sc, mode in zip(w_r, a_sc, modes):
            if mode == "nt2":
                accs.append(lax.dot_general(sc[...], w_ref[...], (((1,), (1,)), ((), ())),
                                            preferred_element_type=F32))
            else:
                accs.append(jnp.dot(sc[...], w_ref[...], preferred_element_type=F32))
        outs = epilogue(accs, tile_r)
        for r, o in zip(to_r, outs):
            r[...] = o.astype(r.dtype)

    outs, side_outs = _call(
        body, side, name=name, grid=(nI, nJ), in_specs=in_specs, out_specs=out_specs, out_shape=out_shape,
        scratch=scratch, params=_params(("arbitrary", "arbitrary"), blocks, scratch, temps=[((TM, TN), F32)] * 3),
        args=[*row_ins, *vec_ins, *[w for w, _ in weights], *tile_ins])
    return outs if side is None else (outs, side_outs)


def _kloop_mm(name, S, TM, a, w3, TK, *, row_ins, vec_ins, row_outs, acc_outs, epilogue, place=_identity,
              side=None):
    _, N, Ks = w3.shape
    nb = Ks // TK
    nK = N_CHIPS * nb
    n_row, n_vec, n_ro, n_ao = len(row_ins), len(vec_ins), len(row_outs), len(acc_outs)

    in_specs = [pl.BlockSpec((TM, TK), lambda i, k: (i, k)),
                pl.BlockSpec((None, N, TK), _chip_major(nb, place))]
    blocks = [((TM, TK), BF16), ((N, TK), BF16)]
    for r in row_ins:
        in_specs.append(pl.BlockSpec((TM, r.shape[1]), lambda i, k: (i, 0)))
        blocks.append(((TM, r.shape[1]), r.dtype))
    for v in vec_ins:
        in_specs.append(pl.BlockSpec(v.shape, lambda i, k: (0, 0)))
        blocks.append((v.shape, v.dtype))
    out_shape, out_specs = [], []
    for width, dt in row_outs:
        out_shape.append(jax.ShapeDtypeStruct((S, width), dt))
        out_specs.append(pl.BlockSpec((TM, width), lambda i, k: (i, 0)))
        blocks.append(((TM, width), dt))
    for rows, width in acc_outs:
        out_shape.append(jax.ShapeDtypeStruct((rows, width), F32))
        out_specs.append(pl.BlockSpec((rows, width), lambda i, k: (0, 0)))
        blocks.append(((rows, width), F32))
    scratch = [((TM, N), F32)]

    def body(*refs):
        a_ref, w_ref = refs[0], refs[1]
        row_r = refs[2:2 + n_row]
        vec_r = refs[2 + n_row:2 + n_row + n_vec]
        pos = 2 + n_row + n_vec
        ro_r = refs[pos:pos + n_ro]
        ao_r = refs[pos + n_ro:pos + n_ro + n_ao]
        acc_sc = refs[pos + n_ro + n_ao]
        i, k = pl.program_id(0), pl.program_id(1)
        @pl.when(k == 0)
        def _():
            acc_sc[...] = jnp.zeros_like(acc_sc)
        acc_sc[...] += lax.dot_general(a_ref[...], w_ref[...], (((1,), (1,)), ((), ())),
                                       preferred_element_type=F32)

        @pl.when(k == nK - 1)
        def _():
            @pl.when(i == 0)
            def _():
                for r in ao_r:
                    r[...] = jnp.zeros_like(r)

            def chunk(ci, carry):
                rows = pl.ds(pl.multiple_of(ci * ROW_CHUNK, ROW_CHUNK), ROW_CHUNK)
                epilogue(acc_sc[rows, :], rows, row_r, vec_r, ro_r, ao_r)
                return carry
            lax.fori_loop(0, TM // ROW_CHUNK, chunk, 0)

    outs, side_outs = _call(
        body, side, name=name, grid=(S // TM, nK), in_specs=in_specs, out_specs=out_specs, out_shape=out_shape,
        scratch=scratch, params=_params(("arbitrary", "arbitrary"), blocks, scratch, temps=[((TM, N), F32)]),
        args=[a, w3, *row_ins, *vec_ins])
    return outs if side is None else (outs, side_outs)


def _tn_mm(name, a, b, TMw, TNw, TK, cols_per_chip=None, place=_identity):
    S, M = a.shape
    N = b.shape[1]
    nK = S // TK
    if cols_per_chip is None:
        out_shape = jax.ShapeDtypeStruct((M, N), F32)
        out_spec = pl.BlockSpec((TMw, TNw), lambda i, j, k: (i, j))
    else:
        nb = cols_per_chip // TNw
        out_shape = jax.ShapeDtypeStruct((N_CHIPS, M, cols_per_chip), F32)
        out_spec = pl.BlockSpec((None, TMw, TNw), lambda i, j, k: (place(j) // nb, i, place(j) % nb))

    def body(a_ref, b_ref, o_ref):
        @pl.when(pl.program_id(2) == 0)
        def _():
            o_ref[...] = jnp.zeros_like(o_ref)
        o_ref[...] += lax.dot_general(a_ref[...], b_ref[...], (((0,), (0,)), ((), ())),
                                      preferred_element_type=F32)

    blocks = [((TK, TMw), BF16), ((TK, TNw), BF16), ((TMw, TNw), F32)]
    return pl.pallas_call(
        body, name=name, grid=(M // TMw, N // TNw, nK),
        in_specs=[pl.BlockSpec((TK, TMw), lambda i, j, k: (k, i)),
                  pl.BlockSpec((TK, TNw), lambda i, j, k: (k, j))],
        out_specs=out_spec, out_shape=out_shape,
        compiler_params=_params(("arbitrary", "arbitrary", "arbitrary"), blocks,
                                temps=[((TMw, TNw), F32), ((TK, TMw), BF16)]),
    )(a, b)


def _prev_rows(TM, H, col):
    return lambda i: (jnp.maximum(i * (TM // H) - 1, 0), col)


def _next_rows(S, TM, H, col):
    return lambda i: (jnp.minimum((i + 1) * (TM // H), S // H - 1), col)


def _taps_causal(ext_ref, w_ref, K, H, TM, cs):
    acc = None
    for k in range(K):
        term = ext_ref[pl.ds(H - (K - 1) + k, TM), cs] * w_ref[pl.ds(k, 1), cs]
        acc = term if acc is None else acc + term
    return acc


def _taps_anticausal(ext_ref, w_ref, K, TM, cs):
    acc = None
    for k in range(K):
        term = ext_ref[pl.ds(K - 1 - k, TM), cs] * w_ref[pl.ds(k, 1), cs]
        acc = term if acc is None else acc + term
    return acc


def _tap_grads(ext_ref, g, K, H, TM, cs):
    return [_rsum(ext_ref[pl.ds(H - (K - 1) + k, TM), cs] * g) for k in range(K)]


def _shift_copies(ext_ref, shifted, cs):
    n = shifted.shape[1]
    for r in range(1, SUBLANES):
        shifted[r - 1] = ext_ref[pl.ds(r, n), cs]


def _rows_at(ext_ref, shifted, start, n, cs):
    q, r = divmod(start, SUBLANES)
    if r == 0:
        return ext_ref[pl.ds(start, n), cs]
    return shifted[r - 1, pl.ds(SUBLANES * q, n), :]


def _mixer_fwd(z, conv_a_w, conv_a_b, ln_g, ln_b, conv_b_w, S, TM, A, side=None):
    H = 32
    KA, KB = conv_a_w.shape[0], conv_b_w.shape[0]
    n_chunks = A // LANES
    RB = _pick(TM, (64, 32))

    def body(zc_ref, zh_ref, wa_ref, ba_ref, g_ref, b_ref, wb_ref, a1_ref, cat_ref, ext_a, ext_b, shifted):
        i = pl.program_id(0)
        live = (i > 0).astype(F32)
        zc = zc_ref[...].astype(F32)
        zh = zh_ref[...].astype(F32) * live
        ext_a[pl.ds(0, H), :] = zh[:, 0:A] * _sigmoid(zh[:, A:2 * A])
        ext_a[pl.ds(H, TM), :] = zc[:, 0:A] * _sigmoid(zc[:, A:2 * A])
        ext_b[pl.ds(0, H), :] = zh[:, 3 * A:4 * A] * zh[:, 4 * A:5 * A]
        ext_b[pl.ds(H, TM), :] = zc[:, 3 * A:4 * A] * zc[:, 4 * A:5 * A]

        def chunk(c, carry):
            cs = pl.ds(pl.multiple_of(c * LANES, LANES), LANES)
            _shift_copies(ext_a, shifted, cs)
            for r0 in range(0, TM, RB):
                acc = None
                for k in range(KA):
                    term = _rows_at(ext_a, shifted, H - (KA - 1) + k + r0, RB, cs) * wa_ref[pl.ds(k, 1), cs]
                    acc = term if acc is None else acc + term
                a1_ref[pl.ds(r0, RB), cs] = acc + ba_ref[:, cs]
            return carry
        lax.fori_loop(0, n_chunks, chunk, 0)

        a1 = a1_ref[...]
        mu = jnp.mean(a1, axis=-1, keepdims=True)
        d = a1 - mu
        var = jnp.mean(d * d, axis=-1, keepdims=True)
        a2 = d * lax.rsqrt(var + EPS) * g_ref[...] + b_ref[...]
        cat_ref[:, 0:A] = (a2 * _sigmoid(a2)).astype(BF16)
        cbc = _taps_causal(ext_b, wb_ref, KB, H, TM, slice(None))
        cat_ref[:, A:2 * A] = (zc[:, 2 * A:3 * A] * cbc).astype(BF16)

    blocks = [((TM, 5 * A), BF16), ((H, 5 * A), BF16), ((KA, A), F32), ((KB, A), F32),
              ((TM, A), F32), ((TM, 2 * A), BF16)]
    scratch = [((H + TM, A), F32), ((H + TM, A), F32), ((SUBLANES - 1, H + TM - SUBLANES, LANES), F32)]
    vec = lambda r: pl.BlockSpec((r, A), lambda i: (0, 0))
    outs, side_outs = _call(
        body, side, name="mixer_fwd", grid=(S // TM,),
        in_specs=[pl.BlockSpec((TM, 5 * A), lambda i: (i, 0)),
                  pl.BlockSpec((H, 5 * A), _prev_rows(TM, H, 0)),
                  vec(KA), vec(1), vec(1), vec(1), vec(KB)],
        out_specs=[pl.BlockSpec((TM, A), lambda i: (i, 0)), pl.BlockSpec((TM, 2 * A), lambda i: (i, 0))],
        out_shape=[jax.ShapeDtypeStruct((S, A), F32), jax.ShapeDtypeStruct((S, 2 * A), BF16)],
        scratch=scratch,
        params=_params(("arbitrary",), blocks, scratch, temps=[((TM, 5 * A), F32)] * 2 + [((TM, A), F32)] * 10),
        args=[z, z, conv_a_w, conv_a_b, ln_g, ln_b, conv_b_w])
    return outs if side is None else (outs, side_outs)


def _pair_tile(nF):
    return lambda t: (t % 2) * nF + t // 2


FFN_ROWS = 32


def _bcast_taps(w_ref, K, lanes):
    return [jnp.broadcast_to(w_ref[pl.ds(k, 1), lanes], (FFN_ROWS, LANES)) for k in range(K)]


def _ffn_act(u0, conv_w, S, TM, F, TC, side=None):
    H = 16
    K = conv_w.shape[0]
    nF = F // TC

    def body(uc_ref, uh_ref, wg_ref, wu_ref, o_ref, conv_ref, ext):
        live = (pl.program_id(0) > 0).astype(F32)
        ext[pl.ds(0, H), :] = uh_ref[...].astype(F32) * live
        ext[pl.ds(H, TM), :] = uc_ref[...].astype(F32)

        def lane_chunk(c, carry):
            lo = pl.ds(pl.multiple_of(c * LANES, LANES), LANES)
            lg, lu = lo, pl.ds(pl.multiple_of(TC + c * LANES, LANES), LANES)
            wg, wu = _bcast_taps(wg_ref, K, lo), _bcast_taps(wu_ref, K, lo)
            for r0 in range(0, TM, FFN_ROWS):
                g = u = None
                for k in range(K):
                    rows = pl.ds(H - (K - 1) + k + r0, FFN_ROWS)
                    tg, tu = ext[rows, lg] * wg[k], ext[rows, lu] * wu[k]
                    g, u = (tg, tu) if g is None else (g + tg, u + tu)
                o_ref[pl.ds(r0, FFN_ROWS), lo] = (g * _sigmoid(g) * u).astype(BF16)
                conv_ref[pl.ds(r0, FFN_ROWS), lg] = g.astype(BF16)
                conv_ref[pl.ds(r0, FFN_ROWS), lu] = u.astype(BF16)
            return carry
        lax.fori_loop(0, TC // LANES, lane_chunk, 0)

    blocks = [((TM, 2 * TC), BF16), ((H, 2 * TC), BF16), ((K, TC), F32), ((K, TC), F32), ((TM, TC), BF16),
              ((TM, 2 * TC), BF16)]
    scratch = [((H + TM, 2 * TC), F32)]
    outs, side_outs = _call(
        body, side, name="ffn_act", grid=(S // TM, nF),
        in_specs=[pl.BlockSpec((TM, 2 * TC), lambda i, j: (i, j)),
                  pl.BlockSpec((H, 2 * TC), lambda i, j: (jnp.maximum(i * (TM // H) - 1, 0), j)),
                  pl.BlockSpec((K, TC), lambda i, j: (0, j)),
                  pl.BlockSpec((K, TC), lambda i, j: (0, j + nF))],
        out_specs=[pl.BlockSpec((TM, TC), lambda i, j: (i, j)), pl.BlockSpec((TM, 2 * TC), lambda i, j: (i, j))],
        out_shape=[jax.ShapeDtypeStruct((S, F), BF16), jax.ShapeDtypeStruct((S, 2 * F), BF16)],
        scratch=scratch,
        params=_params(("arbitrary", "arbitrary"), blocks, scratch, temps=[((TM, 2 * TC), F32)]),
        args=[u0, u0, conv_w, conv_w])
    return outs if side is None else (outs, side_outs)


def _ple_loss(h2, p, target, w_gate, w_proj, b_gate, g_final, S, TM):
    D, P = h2.shape[1], p.shape[1]

    def body(h_ref, p_ref, t_ref, wg_ref, wp_ref, b_ref, g_ref,
             loss_ref, dg_ref, db_ref, dh_ref, dpre_ref, dpp_ref, hb_ref, pb_ref, pre_sc, pp_sc):
        @pl.when(pl.program_id(0) == 0)
        def _():
            loss_ref[...] = jnp.zeros_like(loss_ref)
            dg_ref[...] = jnp.zeros_like(dg_ref)
            db_ref[...] = jnp.zeros_like(db_ref)
        hb_ref[...] = h_ref[...].astype(BF16)
        pb_ref[...] = p_ref[...].astype(BF16)
        pre_sc[...] = jnp.dot(hb_ref[...], wg_ref[...], preferred_element_type=F32)
        pp_sc[...] = jnp.dot(pb_ref[...], wp_ref[...], preferred_element_type=F32)

        def chunk(ci, carry):
            rows = pl.ds(pl.multiple_of(ci * ROW_CHUNK, ROW_CHUNK), ROW_CHUNK)
            g = g_ref[...]
            gate = _sigmoid(pre_sc[rows, :] + b_ref[...])
            pp = pp_sc[rows, :]
            h = h_ref[rows, :] + pp * gate
            r = _rms_stats(h)
            n = h * r
            diff = n * g - t_ref[rows, :]
            loss_ref[...] += 0.5 * jnp.sum(jnp.mean(diff * diff, axis=-1, keepdims=True), axis=0, keepdims=True)
            dy = diff * (1.0 / D)
            dn = dy * g
            dh = r * (dn - n * jnp.mean(dn * n, axis=-1, keepdims=True))
            dh_ref[rows, :] = dh
            dg_ref[...] += _rsum(dy * n)
            dpre = dh * pp * gate * (1.0 - gate)
            dpre_ref[rows, :] = dpre.astype(BF16)
            dpp_ref[rows, :] = (dh * gate).astype(BF16)
            db_ref[...] += _rsum(dpre)
            return carry
        lax.fori_loop(0, TM // ROW_CHUNK, chunk, 0)

    row = pl.BlockSpec((TM, D), lambda i: (i, 0))
    prow = pl.BlockSpec((TM, P), lambda i: (i, 0))
    vec = pl.BlockSpec((1, D), lambda i: (0, 0))
    whole = lambda a: pl.BlockSpec(a.shape, lambda i: (0, 0))
    blocks = ([((TM, D), F32)] * 3 + [((TM, P), F32), ((D, D), BF16), ((P, D), BF16)]
              + [((TM, D), BF16)] * 3 + [((TM, P), BF16)])
    scratch = [((TM, D), F32)] * 2
    return pl.pallas_call(
        body, name="ple_loss", grid=(S // TM,),
        in_specs=[row, prow, row, whole(w_gate), whole(w_proj), vec, vec],
        out_specs=[pl.BlockSpec((1, 1), lambda i: (0, 0)), vec, vec, row, row, row, row, prow],
        out_shape=[jax.ShapeDtypeStruct((1, 1), F32), jax.ShapeDtypeStruct((1, D), F32),
                   jax.ShapeDtypeStruct((1, D), F32), jax.ShapeDtypeStruct((S, D), F32),
                   jax.ShapeDtypeStruct((S, D), BF16), jax.ShapeDtypeStruct((S, D), BF16),
                   jax.ShapeDtypeStruct((S, D), BF16), jax.ShapeDtypeStruct((S, P), BF16)],
        scratch_shapes=[pltpu.VMEM(s, d) for s, d in scratch],
        compiler_params=_params(("arbitrary",), blocks, scratch, temps=[((TM, D), F32)] * 2),
    )(h2, p, target, w_gate, w_proj, b_gate, g_final)


def _ffn_bwd(u0, conv_u0, dact, conv_w, S, TM, F, TC, side=None):
    H = FFN_ROWS
    K = conv_w.shape[0]
    nF, nI = F // TC, S // TM

    def body(xc_ref, cc_ref, cn_ref, dc_ref, dn_ref, wg_ref, wu_ref, o_ref, dwg_ref, dwu_ref, ext_d):
        i = pl.program_id(1)
        @pl.when(i == 0)
        def _():
            dwg_ref[...] = jnp.zeros_like(dwg_ref)
            dwu_ref[...] = jnp.zeros_like(dwu_ref)
        last = (i < nI - 1).astype(F32)

        def lane_chunk(c, carry):
            lo = pl.ds(pl.multiple_of(c * LANES, LANES), LANES)
            lg, lu = lo, pl.ds(pl.multiple_of(TC + c * LANES, LANES), LANES)
            wg, wu = _bcast_taps(wg_ref, K, lo), _bcast_taps(wu_ref, K, lo)
            for r0 in range(0, TM + H, FFN_ROWS):
                if r0 < TM:
                    rows = pl.ds(r0, FFN_ROWS)
                    g, u, da = cc_ref[rows, lg], cc_ref[rows, lu], dc_ref[rows, lo].astype(F32)
                else:
                    g, u, da = cn_ref[:, lg], cn_ref[:, lu], dn_ref[:, lo].astype(F32) * last
                g, u = g.astype(F32), u.astype(F32)
                s = _sigmoid(g)
                ext_d[pl.ds(r0, FFN_ROWS), lg] = da * u * s * (1.0 + g * (1.0 - s))
                ext_d[pl.ds(r0, FFN_ROWS), lu] = da * g * s
            sums_g, sums_u = [None] * K, [None] * K
            for r0 in range(0, TM, FFN_ROWS):
                xg = xc_ref[pl.ds(r0, FFN_ROWS), lg].astype(F32)
                xu = xc_ref[pl.ds(r0, FFN_ROWS), lu].astype(F32)
                g = u = None
                for k in range(K):
                    rows = pl.ds(K - 1 - k + r0, FFN_ROWS)
                    dg, du = ext_d[rows, lg], ext_d[rows, lu]
                    tg, tu = dg * wg[k], du * wu[k]
                    g, u = (tg, tu) if g is None else (g + tg, u + tu)
                    pg, pu = xg * dg, xu * du
                    sums_g[k] = pg if sums_g[k] is None else sums_g[k] + pg
                    sums_u[k] = pu if sums_u[k] is None else sums_u[k] + pu
                o_ref[pl.ds(r0, FFN_ROWS), lg] = g.astype(BF16)
                o_ref[pl.ds(r0, FFN_ROWS), lu] = u.astype(BF16)
            for k in range(K):
                dwg_ref[pl.ds(k, 1), lo] += _rsum(sums_g[k])
                dwu_ref[pl.ds(k, 1), lo] += _rsum(sums_u[k])
            return carry
        lax.fori_loop(0, TC // LANES, lane_chunk, 0)

    blocks = [((TM, 2 * TC), BF16), ((TM, 2 * TC), BF16), ((H, 2 * TC), BF16), ((TM, TC), BF16), ((H, TC), BF16),
              ((K, TC), F32), ((K, TC), F32), ((TM, 2 * TC), BF16), ((K, TC), F32), ((K, TC), F32)]
    scratch = [((TM + H, 2 * TC), F32)]
    nxt = lambda j, i: (jnp.minimum((i + 1) * (TM // H), S // H - 1), j)
    tile = pl.BlockSpec((TM, 2 * TC), lambda j, i: (i, j))
    taps_out = pl.BlockSpec((K, TC), lambda j, i: (0, j))
    outs, side_outs = _call(
        body, side, name="ffn_bwd", grid=(nF, nI),
        in_specs=[tile, tile, pl.BlockSpec((H, 2 * TC), nxt),
                  pl.BlockSpec((TM, TC), lambda j, i: (i, j)), pl.BlockSpec((H, TC), nxt),
                  pl.BlockSpec((K, TC), lambda j, i: (0, j)), pl.BlockSpec((K, TC), lambda j, i: (0, j + nF))],
        out_specs=[tile, taps_out, taps_out],
        out_shape=[jax.ShapeDtypeStruct((S, 2 * F), BF16), jax.ShapeDtypeStruct((K, F), F32),
                   jax.ShapeDtypeStruct((K, F), F32)],
        scratch=scratch,
        params=_params(("arbitrary", "arbitrary"), blocks, scratch),
        args=[u0, conv_u0, conv_u0, dact, dact, conv_w, conv_w])
    return outs if side is None else (outs, side_outs)


def _mixer_bwd_ln(dcat, a1, ln_g, ln_b, S, TM, A):
    def body(dc_ref, a1_ref, g_ref, b_ref, da1_ref, acc_ref):
        @pl.when(pl.program_id(0) == 0)
        def _():
            acc_ref[...] = jnp.zeros_like(acc_ref)
        a1 = a1_ref[...]
        g = g_ref[...]
        mu = jnp.mean(a1, axis=-1, keepdims=True)
        d = a1 - mu
        rstd = lax.rsqrt(jnp.mean(d * d, axis=-1, keepdims=True) + EPS)
        nh = d * rstd
        a2 = nh * g + b_ref[...]
        s = _sigmoid(a2)
        da2 = dc_ref[...].astype(F32) * s * (1.0 + a2 * (1.0 - s))
        dnh = da2 * g
        da1 = rstd * (dnh - jnp.mean(dnh, axis=-1, keepdims=True)
                      - nh * jnp.mean(dnh * nh, axis=-1, keepdims=True))
        da1_ref[...] = da1
        acc_ref[pl.ds(0, 1), :] += _rsum(da2 * nh)
        acc_ref[pl.ds(1, 1), :] += _rsum(da2)
        acc_ref[pl.ds(2, 1), :] += _rsum(da1)

    blocks = [((TM, A), BF16), ((TM, A), F32), ((TM, A), F32), ((4, A), F32)]
    return pl.pallas_call(
        body, name="mixer_bwd_ln", grid=(S // TM,),
        in_specs=[pl.BlockSpec((TM, A), lambda i: (i, 0)), pl.BlockSpec((TM, A), lambda i: (i, 0)),
                  pl.BlockSpec((1, A), lambda i: (0, 0)), pl.BlockSpec((1, A), lambda i: (0, 0))],
        out_specs=[pl.BlockSpec((TM, A), lambda i: (i, 0)), pl.BlockSpec((4, A), lambda i: (0, 0))],
        out_shape=[jax.ShapeDtypeStruct((S, A), F32), jax.ShapeDtypeStruct((4, A), F32)],
        compiler_params=_params(("arbitrary",), blocks, temps=[((TM, A), F32)] * 12),
    )(dcat, a1, ln_g, ln_b)


def _mixer_bwd_conv(z, dcat, da1, conv_a_w, conv_b_w, S, TM, A, side=None):
    H = 32
    KA, KB = conv_a_w.shape[0], conv_b_w.shape[0]
    nI = S // TM
    n_chunks = A // LANES
    RB = _pick(TM, (64, 32))

    def body(zc_ref, zp_ref, zn_ref, dbc_ref, dbn_ref, d1c_ref, d1n_ref, wa_ref, wb_ref,
             dz_ref, dwa_ref, dwb_ref, ext_a0, ext_d1, ext_cb, ext_dc, da0_sc, shifted_d, shifted_a):
        i = pl.program_id(0)
        @pl.when(i == 0)
        def _():
            dwa_ref[...] = jnp.zeros_like(dwa_ref)
            dwb_ref[...] = jnp.zeros_like(dwb_ref)
        first = (i > 0).astype(F32)
        last = (i < nI - 1).astype(F32)
        zc = zc_ref[...].astype(F32)
        zp = zp_ref[...].astype(F32) * first
        a_val, a_gate = zc[:, 0:A], zc[:, A:2 * A]
        b_gate, c_gate, b_h = zc[:, 2 * A:3 * A], zc[:, 3 * A:4 * A], zc[:, 4 * A:5 * A]
        sig = _sigmoid(a_gate)
        ext_a0[pl.ds(0, H), :] = zp[:, 0:A] * _sigmoid(zp[:, A:2 * A])
        ext_a0[pl.ds(H, TM), :] = a_val * sig
        ext_d1[pl.ds(0, TM), :] = d1c_ref[...]
        ext_d1[pl.ds(TM, H), :] = d1n_ref[...] * last
        ext_cb[pl.ds(0, H), :] = zp[:, 3 * A:4 * A] * zp[:, 4 * A:5 * A]
        ext_cb[pl.ds(H, TM), :] = c_gate * b_h
        dbx = dbc_ref[...].astype(F32)
        dcbc = dbx * b_gate
        ext_dc[pl.ds(0, TM), :] = dcbc
        ext_dc[pl.ds(TM, H), :] = dbn_ref[...].astype(F32) * zn_ref[...].astype(F32) * last

        def chunk(c, carry):
            cs = pl.ds(pl.multiple_of(c * LANES, LANES), LANES)
            _shift_copies(ext_d1, shifted_d, cs)
            _shift_copies(ext_a0, shifted_a, cs)
            for r0 in range(0, TM, RB):
                acc = None
                for k in range(KA):
                    term = _rows_at(ext_d1, shifted_d, KA - 1 - k + r0, RB, cs) * wa_ref[pl.ds(k, 1), cs]
                    acc = term if acc is None else acc + term
                da0_sc[pl.ds(r0, RB), cs] = acc
            for k in range(KA):
                acc = None
                for r0 in range(0, TM, RB):
                    term = (_rows_at(ext_a0, shifted_a, H - (KA - 1) + k + r0, RB, cs)
                            * ext_d1[pl.ds(r0, RB), cs])
                    acc = term if acc is None else acc + term
                dwa_ref[pl.ds(k, 1), cs] += _rsum(acc)
            return carry
        lax.fori_loop(0, n_chunks, chunk, 0)

        da0 = da0_sc[...]
        dz_ref[:, 0:A] = (da0 * sig).astype(BF16)
        dz_ref[:, A:2 * A] = (da0 * a_val * sig * (1.0 - sig)).astype(BF16)
        cbc = _taps_causal(ext_cb, wb_ref, KB, H, TM, slice(None))
        dz_ref[:, 2 * A:3 * A] = (dbx * cbc).astype(BF16)
        dcb = _taps_anticausal(ext_dc, wb_ref, KB, TM, slice(None))
        dz_ref[:, 3 * A:4 * A] = (dcb * b_h).astype(BF16)
        dz_ref[:, 4 * A:5 * A] = (dcb * c_gate).astype(BF16)
        grads = _tap_grads(ext_cb, dcbc, KB, H, TM, slice(None))
        for k in range(KB):
            dwb_ref[pl.ds(k, 1), :] += grads[k]

    blocks = [((TM, 5 * A), BF16), ((H, 5 * A), BF16), ((H, A), BF16), ((TM, A), BF16), ((H, A), BF16),
              ((TM, A), F32), ((H, A), F32), ((KA, A), F32), ((KB, A), F32),
              ((TM, 5 * A), BF16), ((KA, A), F32), ((KB, A), F32)]
    scratch = ([((H + TM, A), F32)] * 4 + [((TM, A), F32)]
               + [((SUBLANES - 1, H + TM - SUBLANES, LANES), F32)] * 2)
    vec = lambda r: pl.BlockSpec((r, A), lambda i: (0, 0))
    outs, side_outs = _call(
        body, side, name="mixer_bwd_conv", grid=(nI,),
        in_specs=[pl.BlockSpec((TM, 5 * A), lambda i: (i, 0)),
                  pl.BlockSpec((H, 5 * A), _prev_rows(TM, H, 0)),
                  pl.BlockSpec((H, A), _next_rows(S, TM, H, 2)),
                  pl.BlockSpec((TM, A), lambda i: (i, 1)),
                  pl.BlockSpec((H, A), _next_rows(S, TM, H, 1)),
                  pl.BlockSpec((TM, A), lambda i: (i, 0)),
                  pl.BlockSpec((H, A), _next_rows(S, TM, H, 0)),
                  vec(KA), vec(KB)],
        out_specs=[pl.BlockSpec((TM, 5 * A), lambda i: (i, 0)), vec(KA), vec(KB)],
        out_shape=[jax.ShapeDtypeStruct((S, 5 * A), BF16), jax.ShapeDtypeStruct((KA, A), F32),
                   jax.ShapeDtypeStruct((KB, A), F32)],
        scratch=scratch,
        params=_params(("arbitrary",), blocks, scratch, temps=[((TM, 5 * A), F32)] * 2 + [((TM, A), F32)] * 14),
        args=[z, z, z, dcat, dcat, da1, da1, conv_a_w, conv_b_w])
    return outs if side is None else (outs, side_outs)


def _row_tile(R):
    return _pick(R, (256, 128, 64, 32, 16, 8))


def _scalars(*vals):
    return jnp.stack([jnp.asarray(v, jnp.int32) for v in vals])


def _cast_into_gathered(name, w, chip):
    R, C = w.shape
    TR = _row_tile(R)

    def body(s_ref, w_ref, o_ref):
        o_ref[...] = w_ref[...].astype(BF16)

    grid_spec = pltpu.PrefetchScalarGridSpec(
        num_scalar_prefetch=1, grid=(R // TR,),
        in_specs=[pl.BlockSpec((TR, C), lambda r, s: (r, 0))],
        out_specs=pl.BlockSpec((None, TR, C), lambda r, s: (s[0], r, 0)))
    return pl.pallas_call(body, name=name, grid_spec=grid_spec,
                          out_shape=jax.ShapeDtypeStruct((N_CHIPS, R, C), BF16),
                          compiler_params=_params(("arbitrary",), [((TR, C), F32), ((TR, C), BF16)]),
                          )(_scalars(chip), w)


def _add_pair(name, dw, recv, c, chip):
    _, _, Rh, C = dw.shape
    TR = _row_tile(Rh)

    def body(s_ref, a_ref, b_ref, o_ref, ob_ref):
        s = a_ref[...] + b_ref[...]
        ob_ref[...] = s.astype(BF16)

        @pl.when(pl.program_id(1) == s_ref[1])
        def _():
            o_ref[...] = s

    grid_spec = pltpu.PrefetchScalarGridSpec(
        num_scalar_prefetch=1, grid=(Rh // TR, N_CHIPS),
        in_specs=[pl.BlockSpec((None, None, TR, C), lambda r, k, s: (k, s[0], r, 0)),
                  pl.BlockSpec((None, TR, C), lambda r, k, s: (k, r, 0))],
        out_specs=[pl.BlockSpec((TR, C), lambda r, k, s: (r, 0)),
                   pl.BlockSpec((None, TR, C), lambda r, k, s: (k, r, 0))])
    return pl.pallas_call(body, name=name, grid_spec=grid_spec,
                          out_shape=[jax.ShapeDtypeStruct((Rh, C), F32),
                                     jax.ShapeDtypeStruct((N_CHIPS, Rh, C), BF16)],
                          compiler_params=_params(("arbitrary", "arbitrary"), [((TR, C), F32)] * 4),
                          )(_scalars(c, chip), *_in_hbm([dw, recv]))


def _add_chips(name, own, recv, c):
    Rh, C = own.shape
    TR = _row_tile(Rh)

    def body(s_ref, p_ref, r_ref, o_ref):
        o_ref[...] = ((p_ref[...] + r_ref[0].astype(F32)) + r_ref[1].astype(F32)) + r_ref[2].astype(F32)

    grid_spec = pltpu.PrefetchScalarGridSpec(
        num_scalar_prefetch=1, grid=(Rh // TR,),
        in_specs=[pl.BlockSpec((TR, C), lambda r, s: (r, 0)),
                  pl.BlockSpec((N_CHIPS - 1, TR, C), lambda r, s: (0, r, 0))],
        out_specs=pl.BlockSpec((None, TR, C), lambda r, s: (s[0], r, 0)))
    return pl.pallas_call(body, name=name, grid_spec=grid_spec,
                          out_shape=jax.ShapeDtypeStruct((2, Rh, C), F32),
                          compiler_params=_params(("arbitrary",), [((N_CHIPS + 1, TR, C), F32)]),
                          )(_scalars(c), *_in_hbm([own, recv]))


def _sum_devices(name, parts):
    _, R, C = parts.shape

    def body(p_ref, o_ref):
        acc = p_ref[0]
        for d in range(1, N_DEV):
            acc = acc + p_ref[d]
        o_ref[...] = acc

    return pl.pallas_call(body, name=name, out_shape=jax.ShapeDtypeStruct((R, C), F32),
                          in_specs=[pl.BlockSpec(memory_space=pltpu.VMEM)],
                          out_specs=pl.BlockSpec(memory_space=pltpu.VMEM))(parts)


def _adamw(name, w, g, m, v, copy_grad=False):
    R, C = w.shape
    TR = _pick(R, (128, 64, 32, 16, 8))
    c1 = 1.0 - ADAM_B1 ** ADAM_STEP
    c2 = 1.0 - ADAM_B2 ** ADAM_STEP
    n_out = 4 if copy_grad else 3

    def body(w_ref, g_ref, m_ref, v_ref, d_ref, nm_ref, nv_ref, *g_out):
        g_ = g_ref[...]
        nm = ADAM_B1 * m_ref[...] + (1.0 - ADAM_B1) * g_
        nv = ADAM_B2 * v_ref[...] + (1.0 - ADAM_B2) * (g_ * g_)
        d_ref[...] = -ADAM_LR * ((nm / c1) / (jnp.sqrt(nv / c2) + ADAM_EPS) + ADAM_WD * w_ref[...])
        nm_ref[...] = nm
        nv_ref[...] = nv
        for ref in g_out:
            ref[...] = g_

    spec = pl.BlockSpec((TR, C), lambda r: (r, 0))
    shp = jax.ShapeDtypeStruct((R, C), F32)
    return pl.pallas_call(body, name=name, grid=(R // TR,), in_specs=[spec] * 4, out_specs=[spec] * n_out,
                          out_shape=[shp] * n_out,
                          compiler_params=_params(("arbitrary",), [((TR, C), F32)] * (4 + n_out)),
                          )(w, *_in_hbm([g]), m, v)


def _place():
    x, y, c = lax.axis_index("x"), lax.axis_index("y"), lax.axis_index("c")
    others = [(1 - x, y), (x, 1 - y), (1 - x, 1 - y)]
    return x, y, c, others


def _allgather_small(name, block):
    R, C = block.shape

    def body(x_ref, out_ref, send_sems, recv_sems, local_sem):
        x, y, c, chips = _place()
        me, sibling = (x, y, c), (x, y, 1 - c)

        def rows(px, py, pc):
            return out_ref.at[4 * px + 2 * py + pc]

        def copy(k, blk, to, src=None):
            return pltpu.make_async_remote_copy(
                src_ref=rows(*blk) if src is None else src, dst_ref=rows(*blk),
                send_sem=send_sems.at[k], recv_sem=recv_sems.at[k], device_id=to, device_id_type=MESH)

        mine = pltpu.make_async_copy(x_ref, rows(*me), local_sem)
        mine.start()
        first = [copy(0, me, sibling, src=x_ref)]
        first += [copy(1 + j, me, (*chip, c), src=x_ref) for j, chip in enumerate(chips)]
        for cp in first:
            cp.start()
        passed = [copy(4 + j, (*chip, c), sibling) for j, chip in enumerate(chips)]
        for j, chip in enumerate(chips):
            copy(1 + j, (*chip, c), me).wait_recv()
            passed[j].start()
        copy(0, sibling, me).wait_recv()
        for j, chip in enumerate(chips):
            copy(4 + j, (*chip, 1 - c), me).wait_recv()
        for cp in first + passed:
            cp.wait_send()
        mine.wait()

    return pl.pallas_call(
        body, name=name, out_shape=jax.ShapeDtypeStruct((N_DEV, R, C), F32),
        in_specs=[pl.BlockSpec(memory_space=pltpu.VMEM)], out_specs=pl.BlockSpec(memory_space=pltpu.VMEM),
        scratch_shapes=[pltpu.SemaphoreType.DMA((7,)), pltpu.SemaphoreType.DMA((7,)), pltpu.SemaphoreType.DMA],
    )(block)


def _gather_side(bufs, across, within):
    def rows(ref, chip, half, piece):
        _, r0, n = piece
        return ref.at[2 * chip[0] + chip[1], pl.ds(half * (ref.shape[1] // 2) + r0, n)]

    def copies(ins, outs, send_sems, recv_sems, base):
        x, y, c, chips = _place()
        sibling = (x, y, 1 - c)
        pairs = []

        def add(k, src, dst, to, arrival):
            mk = lambda s, d, dev: pltpu.make_async_remote_copy(
                src_ref=s, dst_ref=d, send_sem=send_sems.at[base + k], recv_sem=recv_sems.at[base + k],
                device_id=dev, device_id_type=MESH)
            pairs.append((mk(src, dst, to), mk(arrival, arrival, (x, y, c))))

        for p, piece in enumerate(across):
            ref = outs[piece[0]]
            for j, chip in enumerate(chips):
                mine = rows(ref, (x, y), c, piece)
                add(3 * p + j, mine, mine, (*chip, c), rows(ref, chip, c, piece))
        for q, piece in enumerate(within):
            ref = outs[piece[0]]
            for j, chip in enumerate(chips):
                held = rows(ref, chip, c, piece)
                add(3 * (len(across) + q) + j, held, held, sibling, rows(ref, chip, 1 - c, piece))
        return pairs

    def start(*refs):
        for send, _ in copies(*refs):
            send.start()

    def wait(*refs):
        pairs = copies(*refs)
        for _, arrival in pairs:
            arrival.wait_recv()
        for send, _ in pairs:
            send.wait_send()

    return _Side(list(bufs), [jax.ShapeDtypeStruct(b.shape, b.dtype) for b in bufs],
                 3 * (len(across) + len(within)), start, wait, aliases=tuple((i, i) for i in range(len(bufs))))


def _chip_exchange(parts):
    n = len(parts)

    def copies(ins, outs, send_sems, recv_sems, base):
        x, y, c, chips = _place()
        return [pltpu.make_async_remote_copy(
            src_ref=ins[a].at[2 * chip[0] + chip[1]], dst_ref=outs[a].at[j],
            send_sem=send_sems.at[base + 3 * a + j], recv_sem=recv_sems.at[base + 3 * a + j],
            device_id=(*chip, c), device_id_type=MESH) for a in range(n) for j, chip in enumerate(chips)]

    return _Side(list(parts), [jax.ShapeDtypeStruct((N_CHIPS - 1,) + p.shape[1:], p.dtype) for p in parts],
                 3 * n, *_start_wait(copies))


def _pair_exchange(grads):
    def copies(ins, outs, send_sems, recv_sems, base):
        x, y, c, _ = _place()
        return [pltpu.make_async_remote_copy(
            src_ref=ins[a].at[:, 1 - c], dst_ref=outs[a], send_sem=send_sems.at[base + a],
            recv_sem=recv_sems.at[base + a], device_id=(x, y, 1 - c), device_id_type=MESH)
            for a in range(len(grads))]

    return _Side(list(grads), [jax.ShapeDtypeStruct((N_CHIPS,) + g.shape[2:], F32) for g in grads],
                 len(grads), *_start_wait(copies))


def _start_wait(copies):
    def start(*refs):
        for cp in copies(*refs):
            cp.start()

    def wait(*refs):
        cps = copies(*refs)
        for cp in cps:
            cp.wait_recv()
        for cp in cps:
            cp.wait_send()
    return start, wait


def _both(first, second):
    n_in, n_out = len(first.ins), len(first.out_shapes)

    def run(which):
        def go(ins, outs, send_sems, recv_sems, base):
            getattr(first, which)(ins[:n_in], outs[:n_out], send_sems, recv_sems, base)
            getattr(second, which)(ins[n_in:], outs[n_out:], send_sems, recv_sems, base + first.n_sems)
        return go

    aliases = first.aliases + tuple((a + n_in, b + n_out) for a, b in second.aliases)
    return _Side(first.ins + second.ins, first.out_shapes + second.out_shapes,
                 first.n_sems + second.n_sems, run("start"), run("wait"), aliases)


def _share_halves(halves):
    def copies(ins, outs, send_sems, recv_sems, base):
        x, y, c, _ = _place()
        pairs = []
        for a in range(len(halves)):
            mk = lambda s, d, dev, a=a: pltpu.make_async_remote_copy(
                src_ref=s, dst_ref=d, send_sem=send_sems.at[base + a], recv_sem=recv_sems.at[base + a],
                device_id=dev, device_id_type=MESH)
            theirs = outs[a].at[1 - c]
            pairs.append((mk(outs[a].at[c], outs[a].at[c], (x, y, 1 - c)), mk(theirs, theirs, (x, y, c))))
        return pairs

    def start(*refs):
        for send, _ in copies(*refs):
            send.start()

    def wait(*refs):
        pairs = copies(*refs)
        for _, arrival in pairs:
            arrival.wait_recv()
        for send, _ in pairs:
            send.wait_send()

    return _Side(list(halves), [jax.ShapeDtypeStruct(h.shape, F32) for h in halves], len(halves), start, wait,
                 aliases=tuple((i, i) for i in range(len(halves))))


def _pack(arrays):
    pieces = []
    for a in arrays:
        flat = a.reshape(-1).astype(F32)
        pieces.append(jnp.pad(flat, (0, (-flat.size) % PACK_ALIGN)))
    return jnp.concatenate(pieces).reshape(-1, LANES)


def _unpack(buf, shapes):
    lead = buf.shape[:-2]
    flat = buf.reshape(lead + (-1,))
    out, off = [], 0
    for shp in shapes:
        size = 1
        for s in shp:
            size *= s
        out.append(flat[..., off:off + size].reshape(lead + tuple(shp)))
        off += size + (-size) % PACK_ALIGN
    return out


def _gather_channels(buf, shapes):
    per_chip = _unpack(buf[0::2], shapes)
    return [jnp.transpose(a, (1, 0, 2)).reshape(a.shape[1], -1) for a in per_chip]


def _mm_tile(n, rows, limit_bytes=6 * 1024 * 1024):
    for t in (1408, 1280, 1024, 640, 512, 384, 256, 128):
        if n % t == 0 and rows * t * 2 <= limit_bytes:
            return t
    raise ValueError(f"no column tile for {n} x {rows}")


def kernel(x, p, norm_mix_g, w_in, conv_a_w, conv_a_b, ln_a_g, ln_a_b, conv_b_w, w_out, norm_ffn_g, w_up, conv_ffn_w, w_down, w_ple_gate, b_ple_gate, w_ple_proj, norm_final_g, loss_target, m_norm_mix_g, m_w_in, m_conv_a_w, m_conv_a_b, m_ln_a_g, m_ln_a_b, m_conv_b_w, m_w_out, m_norm_ffn_g, m_w_up, m_conv_ffn_w, m_w_down, m_w_ple_gate, m_b_ple_gate, m_w_ple_proj, m_norm_final_g, v_norm_mix_g, v_w_in, v_conv_a_w, v_conv_a_b, v_ln_a_g, v_ln_a_b, v_conv_b_w, v_w_out, v_norm_ffn_g, v_w_up, v_conv_ffn_w, v_w_down, v_w_ple_gate, v_b_ple_gate, v_w_ple_proj, v_norm_final_g):
    S, D = x.shape[1], x.shape[2]
    P = p.shape[3]
    A = conv_a_b.shape[1]
    F = w_down.shape[1] * N_CHIPS
    KA, KB, KF = conv_a_w.shape[1], conv_b_w.shape[1], conv_ffn_w.shape[1]
    xi, yi, ci = lax.axis_index("x"), lax.axis_index("y"), lax.axis_index("c")
    chip = 2 * xi + yi

    TM = _pick(S, (512, 256, 128))
    TL = _pick(S, (1024, 512, 256, 128))
    TE = _pick(S, (256, 128))
    TC = _pick(2 * F // N_CHIPS, (1408, 1024, 512, 256, 128))
    ffn_place = _pair_tile(F // TC)

    x2, p2, t2 = x.reshape(S, D), p.reshape(S, P), loss_target.reshape(S, D)
    gfin = norm_final_g.reshape(1, D)

    big = dict(w_in=w_in[0], w_out=w_out[0], w_up=w_up[0], w_down=w_down[0],
               w_ple_gate=w_ple_gate[0], w_ple_proj=w_ple_proj[0])
    names = list(big)
    buf = {n: _cast_into_gathered("cast_" + n, big[n], chip) for n in names}
    half = {n: big[n].shape[0] // 2 for n in names}
    up_a = half["w_up"] // 2
    (w_in3,) = _comm_only("gather_w_in_across", _gather_side([buf["w_in"]], [(0, 0, half["w_in"])], []))
    (w_in3,) = _comm_only("gather_w_in_within", _gather_side([w_in3], [], [(0, 0, half["w_in"])]))

    tap_shapes = [(KA, A // N_CHIPS), (KB, A // N_CHIPS), (KF, 2 * F // N_CHIPS)]
    taps = _allgather_small("allgather_taps", _pack([conv_a_w[0], conv_b_w[0], conv_ffn_w[0]]))
    conv_a_f, conv_b_f, conv_ffn_f = _gather_channels(taps, tap_shapes)

    def rms_prologue(rows, row_r, vec_r, ro_r):
        h = row_r[0][rows, :]
        hn = (h * _rms_stats(h) * vec_r[0][...]).astype(BF16)
        ro_r[0][rows, :] = hn
        return [hn]

    def cast_prologue(rows, row_r, vec_r, ro_r):
        hb = row_r[0][rows, :].astype(BF16)
        ro_r[0][rows, :] = hb
        return [hb]

    plain = lambda accs, tile_r: [accs[0]]
    residual = lambda accs, tile_r: [tile_r[0][...] + accs[0]]

    (z, hn1), (w_out_t, w_up_t) = _rows_mm(
        "in_proj", S, TL, 5 * A, _mm_tile(5 * A // N_CHIPS, D), row_ins=[x2], vec_ins=[norm_mix_g],
        weights=[(w_in3, "nn3")], tile_outs=[BF16], row_outs=[(D, BF16)], prologue=rms_prologue, epilogue=plain,
        side=_gather_side([buf["w_out"], buf["w_up"]], [(0, 0, half["w_out"]), (1, 0, up_a)], []))
    (a1, cat), (w_out3, w_up_t) = _mixer_fwd(
        z, conv_a_f, conv_a_b, ln_a_g, ln_a_b, conv_b_f, S, TE, A,
        side=_gather_side([w_out_t, w_up_t], [(1, up_a, half["w_up"] - up_a)],
                          [(0, 0, half["w_out"]), (1, 0, up_a)]))
    w_out_f = w_out3.reshape(2 * A, D)
    (h1,), (w_up3, w_proj_t) = _rows_mm(
        "out_proj", S, TL, D, _mm_tile(D, 2 * A), row_ins=[cat], weights=[(w_out_f, "nn2")],
        tile_ins=[x2], tile_outs=[F32], epilogue=residual,
        side=_gather_side([w_up_t, buf["w_ple_proj"]], [(1, 0, half["w_ple_proj"])],
                          [(0, up_a, half["w_up"] - up_a)]))
    (u0, hn2), (w_down_t, w_gate_t, w_proj3) = _rows_mm(
        "up_proj", S, TL, 2 * F, TC, row_ins=[h1], vec_ins=[norm_ffn_g],
        weights=[(w_up3, "nn3")], tile_outs=[BF16], row_outs=[(D, BF16)],
        prologue=rms_prologue, epilogue=plain, place=ffn_place,
        side=_gather_side([buf["w_down"], buf["w_ple_gate"], w_proj_t],
                          [(0, 0, half["w_down"]), (1, 0, half["w_ple_gate"])], [(2, 0, half["w_ple_proj"])]))
    (act, conv_u0), (w_down3, w_gate3) = _ffn_act(
        u0, conv_ffn_f, S, TM, F, TC,
        side=_gather_side([w_down_t, w_gate_t], [], [(0, 0, half["w_down"]), (1, 0, half["w_ple_gate"])]))
    w_down_f = w_down3.reshape(F, D)
    w_gate_f = w_gate3.reshape(D, D)
    w_proj_f = jnp.transpose(w_proj3, (1, 0, 2)).reshape(P, D)
    (h2,) = _rows_mm("down_proj", S, TL, D, _mm_tile(D, F), row_ins=[act], weights=[(w_down_f, "nn2")],
                     tile_ins=[h1], tile_outs=[F32], epilogue=residual)

    loss_part, g_norm_final, g_b_gate, dh3, dpre, dpp, h2b, pb = _ple_loss(
        h2, p2, t2, w_gate_f, w_proj_f, b_ple_gate, gfin, S, TE)

    TK = _pick(S, (2048, 1024, 512, 256, 128))
    wt = lambda n: _pick(n, (1408, 1280, 1024, 512, 256, 128))
    chip_sums, from_chips = {}, {}

    def to_sibling(parts):
        ns = list(parts)
        halves = [parts[n].reshape(N_CHIPS, 2, big[n].shape[0] // 2, big[n].shape[1]) for n in ns]
        return ns, halves, _pair_exchange(halves)

    def to_chips(ns, halves, from_sibling):
        sums = [_add_pair("pair_sum_" + n, h, r, ci, chip) for n, h, r in zip(ns, halves, from_sibling)]
        for n, (s, _) in zip(ns, sums):
            chip_sums[n] = s
        return ns, _chip_exchange([b for _, b in sums])

    def landed(ns, side_outs):
        for n, r in zip(ns, side_outs):
            from_chips[n] = r

    ns, halves, side = to_sibling(dict(
        w_ple_gate=_tn_mm("dw_ple_gate", h2b, dpre, wt(D), wt(D), TK),
        w_ple_proj=_tn_mm("dw_ple_proj", pb, dpp, wt(P), wt(D // N_CHIPS), TK, cols_per_chip=D // N_CHIPS)))
    (dh2,), got = _rows_mm("ple_bwd", S, TL, D, _mm_tile(D, D), row_ins=[dpre], weights=[(w_gate_f, "nt2")],
                           tile_ins=[dh3], tile_outs=[F32], epilogue=residual, side=side)
    ple_ns, ple_chips = to_chips(ns, halves, got)
    (dact, dh2b), got = _rows_mm("down_bwd", S, TL, F, _mm_tile(F, D), row_ins=[dh2], weights=[(w_down_f, "nt2")],
                                 tile_outs=[BF16], row_outs=[(D, BF16)], prologue=cast_prologue, epilogue=plain,
                                 side=ple_chips)
    landed(ple_ns, got)
    ns, halves, side = to_sibling(dict(w_down=_tn_mm("dw_down", act, dh2b, wt(F), wt(D), TK)))
    (du0, g_conv_gate, g_conv_up), got = _ffn_bwd(u0, conv_u0, dact, conv_ffn_f, S, TM, F, TC, side=side)
    down_ns, down_chips = to_chips(ns, halves, got)
    g_conv_ffn = jnp.concatenate([g_conv_gate, g_conv_up], axis=1)
    ns, halves, side = to_sibling(dict(
        w_up=_tn_mm("dw_up", hn2, du0, wt(D), TC, TK, cols_per_chip=2 * F // N_CHIPS, place=ffn_place)))

    def up_bwd_epilogue(acc, rows, row_r, vec_r, ro_r, ao_r):
        dh, dg = _rms_bwd(row_r[0][rows, :], vec_r[0][...], acc)
        dh1_ = row_r[1][rows, :] + dh
        ro_r[0][rows, :] = dh1_
        ro_r[1][rows, :] = dh1_.astype(BF16)
        ao_r[0][...] += dg

    (dh1, dh1b, g_norm_ffn), got = _kloop_mm(
        "up_bwd", S, TM, du0, w_up3, TC, row_ins=[h1, dh2], vec_ins=[norm_ffn_g],
        row_outs=[(D, F32), (D, BF16)], acc_outs=[(1, D)], epilogue=up_bwd_epilogue, place=ffn_place,
        side=_both(down_chips, side))
    landed(down_ns, got[:len(down_ns)])
    up_ns, up_chips = to_chips(ns, halves, got[len(down_ns):])
    ns, halves, side = to_sibling(dict(w_out=_tn_mm("dw_out", cat, dh1b, wt(2 * A), wt(D), TK)))
    (dcat,), got = _rows_mm("out_bwd", S, TL, 2 * A, _mm_tile(2 * A, D), row_ins=[dh1b],
                            weights=[(w_out_f, "nt2")], tile_outs=[BF16], epilogue=plain, side=side)
    out_ns, out_chips = to_chips(ns, halves, got)
    da1, ln_sums = _mixer_bwd_ln(dcat, a1, ln_a_g, ln_a_b, S, TE, A)
    (dz, g_conv_a, g_conv_b), got = _mixer_bwd_conv(z, dcat, da1, conv_a_f, conv_b_f, S, TE, A,
                                                    side=_both(up_chips, out_chips))
    landed(up_ns + out_ns, got)
    ns, halves, side = to_sibling(dict(
        w_in=_tn_mm("dw_in", hn1, dz, wt(D), wt(5 * A // N_CHIPS), TK, cols_per_chip=5 * A // N_CHIPS)))
    ns, side = to_chips(ns, halves, _comm_only("grads_exchange_pairs_in", side))

    def in_bwd_epilogue(acc, rows, row_r, vec_r, ro_r, ao_r):
        dh, dg = _rms_bwd(row_r[0][rows, :], vec_r[0][...], acc)
        ro_r[0][rows, :] = row_r[1][rows, :] + dh
        ao_r[0][...] += dg

    early = [n for n in names if n != "w_in"]
    early_halves = [_add_chips("chip_sum_" + n, chip_sums[n], from_chips[n], ci) for n in early]
    (dx, g_norm_mix), got = _kloop_mm(
        "in_bwd", S, TM, dz, w_in3, _mm_tile(5 * A // N_CHIPS, D), row_ins=[x2, dh1],
        vec_ins=[norm_mix_g], row_outs=[(D, F32)], acc_outs=[(1, D)], epilogue=in_bwd_epilogue,
        side=_both(side, _share_halves(early_halves)))
    landed(ns, got[:1])
    shared = dict(zip(early, got[1:]))
    (shared["w_in"],) = _comm_only("grads_share_w_in", _share_halves(
        [_add_chips("chip_sum_w_in", chip_sums["w_in"], from_chips["w_in"], ci)]))

    reduced = [shared[n] for n in names]
    moments = dict(w_in=(m_w_in, v_w_in), w_out=(m_w_out, v_w_out), w_up=(m_w_up, v_w_up),
                   w_down=(m_w_down, v_w_down), w_ple_gate=(m_w_ple_gate, v_w_ple_gate),
                   w_ple_proj=(m_w_ple_proj, v_w_ple_proj))
    grads, deltas, new_m, new_v = {}, {}, {}, {}
    for n, g in zip(names, reduced):
        d_, m_, v_, g = _adamw("adamw_" + n, big[n], g.reshape(big[n].shape), moments[n][0][0], moments[n][1][0],
                               copy_grad=True)
        grads[n], deltas[n], new_m[n], new_v[n] = g[None], d_[None], m_[None], v_[None]

    small = ["norm_mix_g", "conv_a_w", "conv_a_b", "ln_a_g", "ln_a_b", "conv_b_w", "norm_ffn_g",
             "conv_ffn_w", "b_ple_gate", "norm_final_g"]
    small_part = [g_norm_mix, g_conv_a, ln_sums[2:3], ln_sums[0:1], ln_sums[1:2], g_conv_b, g_norm_ffn,
                  g_conv_ffn, g_b_gate, g_norm_final]
    full_shapes = [a.shape for a in small_part]
    summed = _sum_devices("small_grads_sum", _allgather_small("allgather_small_grads", _pack(small_part)))
    small_g = dict(zip(small, _unpack(summed, full_shapes)))
    for n, width in (("conv_a_w", A), ("conv_b_w", A), ("conv_ffn_w", 2 * F)):
        small_g[n] = lax.dynamic_slice_in_dim(small_g[n], chip * (width // N_CHIPS), width // N_CHIPS, axis=1)
    small_w = dict(norm_mix_g=(norm_mix_g, m_norm_mix_g, v_norm_mix_g), conv_a_w=(conv_a_w, m_conv_a_w, v_conv_a_w),
                   conv_a_b=(conv_a_b, m_conv_a_b, v_conv_a_b), ln_a_g=(ln_a_g, m_ln_a_g, v_ln_a_g),
                   ln_a_b=(ln_a_b, m_ln_a_b, v_ln_a_b), conv_b_w=(conv_b_w, m_conv_b_w, v_conv_b_w),
                   norm_ffn_g=(norm_ffn_g, m_norm_ffn_g, v_norm_ffn_g),
                   conv_ffn_w=(conv_ffn_w, m_conv_ffn_w, v_conv_ffn_w),
                   b_ple_gate=(b_ple_gate, m_b_ple_gate, v_b_ple_gate),
                   norm_final_g=(norm_final_g, m_norm_final_g, v_norm_final_g))
    out_shapes = [small_w[n][0].shape for n in small]
    packed_g = _pack([small_g[n] for n in small])
    packed = [_pack([small_w[n][k] for n in small]) for k in range(3)]
    d_s, m_s, v_s = _adamw("adamw_small", packed[0], packed_g, packed[1], packed[2])
    for n, g, d_, m_, v_ in zip(small, _unpack(packed_g, out_shapes), _unpack(d_s, out_shapes),
                                _unpack(m_s, out_shapes), _unpack(v_s, out_shapes)):
        grads[n], deltas[n], new_m[n], new_v[n] = g, d_, m_, v_

    order = ["norm_mix_g", "w_in", "conv_a_w", "conv_a_b", "ln_a_g", "ln_a_b", "conv_b_w", "w_out", "norm_ffn_g",
             "w_up", "conv_ffn_w", "w_down", "w_ple_gate", "b_ple_gate", "w_ple_proj", "norm_final_g"]
    loss = lax.psum(loss_part[0, 0], ("x", "y", "c"))
    return (loss, dx.reshape(x.shape), *[grads[n] for n in order], *[deltas[n] for n in order],
            *[new_m[n] for n in order], *[new_v[n] for n in order])
```

```python
from typing import Callable, NamedTuple

import jax
import jax.numpy as jnp
from jax import lax
from jax.experimental import pallas as pl
from jax.experimental.pallas import tpu as pltpu

F32 = jnp.float32
BF16 = jnp.bfloat16
MESH = pl.DeviceIdType.MESH
ANY = pl.BlockSpec(memory_space=pl.ANY)

EPS = 1e-6
ADAM_LR = 0.001
ADAM_B1 = 0.9
ADAM_B2 = 0.999
ADAM_EPS = 1e-08
ADAM_WD = 0.01
ADAM_STEP = 10

N_CHIPS = 4
N_DEV = 8
LANES = 128
SUBLANES = 8
PACK_ALIGN = LANES * SUBLANES
ROW_CHUNK = 32
VMEM_CAP = 60 * 1024 * 1024
VMEM_SLACK = 6 * 1024 * 1024


def _pick(n, cands):
    for c in cands:
        if n % c == 0:
            return c
    raise ValueError(f"no tile of {cands} divides {n}")


def _nbytes(shape, dtype):
    n = 1
    for s in shape:
        if s is not None:
            n *= s
    return n * jnp.dtype(dtype).itemsize


def _params(sem, blocks, scratch=(), temps=()):
    est = (2 * sum(_nbytes(s, d) for s, d in blocks) + sum(_nbytes(s, d) for s, d in scratch)
           + sum(_nbytes(s, d) for s, d in temps))
    return pltpu.CompilerParams(dimension_semantics=sem,
                                vmem_limit_bytes=min(est + VMEM_SLACK, VMEM_CAP))


def _in_hbm(arrays):
    return [pltpu.with_memory_space_constraint(a, pltpu.HBM) for a in arrays]


def _sigmoid(x):
    return 1.0 / (1.0 + jnp.exp(-x))


def _rsum(x):
    return jnp.sum(x, axis=0, keepdims=True)


class _Side(NamedTuple):
    ins: list
    out_shapes: list
    n_sems: int
    start: Callable
    wait: Callable
    aliases: tuple = ()


def _call(body, side, *, name, grid, in_specs, out_specs, out_shape, scratch, params, args):
    vmem = [pltpu.VMEM(s, d) for s, d in scratch]
    if side is None:
        outs = pl.pallas_call(body, name=name, grid=grid, in_specs=in_specs, out_specs=out_specs,
                              out_shape=out_shape, scratch_shapes=vmem, compiler_params=params)(*args)
        return list(outs), []
    n_in, n_out, n_sc = len(in_specs), len(out_specs), len(scratch)
    ns_in, ns_out = len(side.ins), len(side.out_shapes)

    def carrier(*refs):
        pos = [0]
        def take(n):
            pos[0] += n
            return refs[pos[0] - n:pos[0]]
        ins, s_ins, outs, s_outs, scr = take(n_in), take(ns_in), take(n_out), take(ns_out), take(n_sc)
        send_sems, recv_sems = take(2)
        first = last = None
        for axis, extent in enumerate(grid):
            at_start, at_end = pl.program_id(axis) == 0, pl.program_id(axis) == extent - 1
            first = at_start if first is None else first & at_start
            last = at_end if last is None else last & at_end

        @pl.when(first)
        def _():
            side.start(s_ins, s_outs, send_sems, recv_sems, 0)
        body(*ins, *outs, *scr)

        @pl.when(last)
        def _():
            side.wait(s_ins, s_outs, send_sems, recv_sems, 0)

    outs = pl.pallas_call(
        carrier, name=name, grid=grid, in_specs=list(in_specs) + [ANY] * ns_in,
        out_specs=list(out_specs) + [ANY] * ns_out, out_shape=list(out_shape) + list(side.out_shapes),
        scratch_shapes=vmem + [pltpu.SemaphoreType.DMA((side.n_sems,)), pltpu.SemaphoreType.DMA((side.n_sems,))],
        input_output_aliases={n_in + a: n_out + b for a, b in side.aliases},
        compiler_params=params)(*args, *_in_hbm(side.ins))
    return list(outs[:n_out]), list(outs[n_out:])


def _gather_pipelined(name, buf, n_pieces):
    rh = buf.shape[1] // 2
    n = rh // n_pieces
    n_sems = 6 * n_pieces

    def body(in_ref, out_ref, send_sems, recv_sems):
        x, y, c, chips = _place()
        sibling = (x, y, 1 - c)

        def rows(chip, half, p):
            return out_ref.at[2 * chip[0] + chip[1], pl.ds(half * rh + p * n, n)]

        def copy(k, src, dst, to):
            return pltpu.make_async_remote_copy(src_ref=src, dst_ref=dst, send_sem=send_sems.at[k],
                                                recv_sem=recv_sems.at[k], device_id=to, device_id_type=MESH)

        sends = []
        for p in range(n_pieces):
            for j, chip in enumerate(chips):
                mine = rows((x, y), c, p)
                sends.append(copy(6 * p + j, mine, mine, (*chip, c)))
                sends[-1].start()
        for p in range(n_pieces):
            for j, chip in enumerate(chips):
                landed = rows(chip, c, p)
                copy(6 * p + j, landed, landed, (x, y, c)).wait_recv()
                sends.append(copy(6 * p + 3 + j, landed, landed, sibling))
                sends[-1].start()
        for p in range(n_pieces):
            for j, chip in enumerate(chips):
                theirs = rows(chip, 1 - c, p)
                copy(6 * p + 3 + j, theirs, theirs, (x, y, c)).wait_recv()
        for cp in sends:
            cp.wait_send()

    (out,) = pl.pallas_call(
        body, name=name, out_shape=[jax.ShapeDtypeStruct(buf.shape, buf.dtype)], in_specs=[ANY], out_specs=[ANY],
        scratch_shapes=[pltpu.SemaphoreType.DMA((n_sems,)), pltpu.SemaphoreType.DMA((n_sems,))],
        input_output_aliases={0: 0},
    )(*_in_hbm([buf]))
    return out


def _comm_only(name, side):
    n_in = len(side.ins)

    def body(*refs):
        ins, outs = refs[:n_in], refs[n_in:n_in + len(side.out_shapes)]
        send_sems, recv_sems = refs[n_in + len(side.out_shapes):]
        side.start(ins, outs, send_sems, recv_sems, 0)
        side.wait(ins, outs, send_sems, recv_sems, 0)

    return pl.pallas_call(
        body, name=name, out_shape=list(side.out_shapes), in_specs=[ANY] * n_in,
        out_specs=[ANY] * len(side.out_shapes),
        scratch_shapes=[pltpu.SemaphoreType.DMA((side.n_sems,)), pltpu.SemaphoreType.DMA((side.n_sems,))],
        input_output_aliases=dict(side.aliases),
    )(*_in_hbm(side.ins))


def _rms_stats(x):
    return lax.rsqrt(jnp.mean(x * x, axis=-1, keepdims=True) + EPS)


def _rms_bwd(h, g, dout):
    r = _rms_stats(h)
    n = h * r
    dn = dout * g
    dh = r * (dn - n * jnp.mean(dn * n, axis=-1, keepdims=True))
    return dh, _rsum(dout * n)


def _identity(t):
    return t


def _chip_major(nb, place=_identity):
    return lambda i, j: (place(j) // nb, 0, place(j) % nb)


def _rows_mm(name, S, TM, N, TN, *, row_ins, vec_ins=(), weights, tile_ins=(), tile_outs, row_outs=(),
             prologue=None, epilogue, place=_identity, side=None):
    nI, nJ = S // TM, N // TN
    n_row, n_vec, n_w, n_tile = len(row_ins), len(vec_ins), len(weights), len(tile_ins)
    n_to, n_ro = len(tile_outs), len(row_outs)

    in_specs, blocks, scratch, ks = [], [], [], []
    for a in row_ins:
        in_specs.append(pl.BlockSpec((TM, a.shape[1]), lambda i, j: (i, 0)))
        blocks.append(((TM, a.shape[1]), a.dtype))
    for a in vec_ins:
        in_specs.append(pl.BlockSpec(a.shape, lambda i, j: (0, 0)))
        blocks.append((a.shape, a.dtype))
    for w, mode in weights:
        if mode == "nn2":
            k = w.shape[0]
            in_specs.append(pl.BlockSpec((k, TN), lambda i, j: (0, j)))
        elif mode == "nn3":
            k = w.shape[1]
            in_specs.append(pl.BlockSpec((None, k, TN), _chip_major(w.shape[2] // TN, place)))
        else:
            k = w.shape[1]
            in_specs.append(pl.BlockSpec((TN, k), lambda i, j: (j, 0)))
        ks.append(k)
        blocks.append(((k, TN), BF16))
        if prologue is not None:
            scratch.append(((TM, k), BF16))
    for a in tile_ins:
        in_specs.append(pl.BlockSpec((TM, TN), lambda i, j: (i, j)))
        blocks.append(((TM, TN), a.dtype))

    out_shape, out_specs = [], []
    for dt in tile_outs:
        out_shape.append(jax.ShapeDtypeStruct((S, N), dt))
        out_specs.append(pl.BlockSpec((TM, TN), lambda i, j: (i, j)))
        blocks.append(((TM, TN), dt))
    for width, dt in row_outs:
        out_shape.append(jax.ShapeDtypeStruct((S, width), dt))
        out_specs.append(pl.BlockSpec((TM, width), lambda i, j: (i, 0)))
        blocks.append(((TM, width), dt))

    modes = [m for _, m in weights]

    def body(*refs):
        pos = 0
        def take(n):
            nonlocal pos
            out = refs[pos:pos + n]
            pos += n
            return out
        row_r, vec_r, w_r, tile_r = take(n_row), take(n_vec), take(n_w), take(n_tile)
        to_r, ro_r, a_sc = take(n_to), take(n_ro), take(len(scratch))

        if prologue is None:
            a_sc = row_r[:n_w]
        else:
            @pl.when(pl.program_id(1) == 0)
            def _():
                def chunk(ci, carry):
                    rows = pl.ds(pl.multiple_of(ci * ROW_CHUNK, ROW_CHUNK), ROW_CHUNK)
                    for sc, a in zip(a_sc, prologue(rows, row_r, vec_r, ro_r)):
                        sc[rows, :] = a
                    return carry
                lax.fori_loop(0, TM // ROW_CHUNK, chunk, 0)

        accs = []
        for w_ref, sc, mode in zip(w_r, a_sc, modes):
            if mode == "nt2":
                accs.append(lax.dot_general(sc[...], w_ref[...], (((1,), (1,)), ((), ())),
                                            preferred_element_type=F32))
            else:
                accs.append(jnp.dot(sc[...], w_ref[...], preferred_element_type=F32))
        outs = epilogue(accs, tile_r)
        for r, o in zip(to_r, outs):
            r[...] = o.astype(r.dtype)

    outs, side_outs = _call(
        body, side, name=name, grid=(nI, nJ), in_specs=in_specs, out_specs=out_specs, out_shape=out_shape,
        scratch=scratch, params=_params(("arbitrary", "arbitrary"), blocks, scratch, temps=[((TM, TN), F32)] * 3),
        args=[*row_ins, *vec_ins, *[w for w, _ in weights], *tile_ins])
    return outs if side is None else (outs, side_outs)


def _kloop_mm(name, S, TM, a, w3, TK, *, row_ins, vec_ins, row_outs, acc_outs, epilogue, place=_identity,
              side=None):
    _, N, Ks = w3.shape
    nb = Ks // TK
    nK = N_CHIPS * nb
    n_row, n_vec, n_ro, n_ao = len(row_ins), len(vec_ins), len(row_outs), len(acc_outs)

    in_specs = [pl.BlockSpec((TM, TK), lambda i, k: (i, k)),
                pl.BlockSpec((None, N, TK), _chip_major(nb, place))]
    blocks = [((TM, TK), BF16), ((N, TK), BF16)]
    for r in row_ins:
        in_specs.append(pl.BlockSpec((TM, r.shape[1]), lambda i, k: (i, 0)))
        blocks.append(((TM, r.shape[1]), r.dtype))
    for v in vec_ins:
        in_specs.append(pl.BlockSpec(v.shape, lambda i, k: (0, 0)))
        blocks.append((v.shape, v.dtype))
    out_shape, out_specs = [], []
    for width, dt in row_outs:
        out_shape.append(jax.ShapeDtypeStruct((S, width), dt))
        out_specs.append(pl.BlockSpec((TM, width), lambda i, k: (i, 0)))
        blocks.append(((TM, width), dt))
    for rows, width in acc_outs:
        out_shape.append(jax.ShapeDtypeStruct((rows, width), F32))
        out_specs.append(pl.BlockSpec((rows, width), lambda i, k: (0, 0)))
        blocks.append(((rows, width), F32))
    scratch = [((TM, N), F32)]

    def body(*refs):
        a_ref, w_ref = refs[0], refs[1]
        row_r = refs[2:2 + n_row]
        vec_r = refs[2 + n_row:2 + n_row + n_vec]
        pos = 2 + n_row + n_vec
        ro_r = refs[pos:pos + n_ro]
        ao_r = refs[pos + n_ro:pos + n_ro + n_ao]
        acc_sc = refs[pos + n_ro + n_ao]
        i, k = pl.program_id(0), pl.program_id(1)
        @pl.when(k == 0)
        def _():
            acc_sc[...] = jnp.zeros_like(acc_sc)
        acc_sc[...] += lax.dot_general(a_ref[...], w_ref[...], (((1,), (1,)), ((), ())),
                                       preferred_element_type=F32)

        @pl.when(k == nK - 1)
        def _():
            @pl.when(i == 0)
            def _():
                for r in ao_r:
                    r[...] = jnp.zeros_like(r)

            def chunk(ci, carry):
                rows = pl.ds(pl.multiple_of(ci * ROW_CHUNK, ROW_CHUNK), ROW_CHUNK)
                epilogue(acc_sc[rows, :], rows, row_r, vec_r, ro_r, ao_r)
                return carry
            lax.fori_loop(0, TM // ROW_CHUNK, chunk, 0)

    outs, side_outs = _call(
        body, side, name=name, grid=(S // TM, nK), in_specs=in_specs, out_specs=out_specs, out_shape=out_shape,
        scratch=scratch, params=_params(("arbitrary", "arbitrary"), blocks, scratch, temps=[((TM, N), F32)]),
        args=[a, w3, *row_ins, *vec_ins])
    return outs if side is None else (outs, side_outs)


def _tn_mm(name, a, b, TMw, TNw, TK, cols_per_chip=None, place=_identity):
    S, M = a.shape
    N = b.shape[1]
    nK = S // TK
    if cols_per_chip is None:
        out_shape = jax.ShapeDtypeStruct((M, N), F32)
        out_spec = pl.BlockSpec((TMw, TNw), lambda i, j, k: (i, j))
    else:
        nb = cols_per_chip // TNw
        out_shape = jax.ShapeDtypeStruct((N_CHIPS, M, cols_per_chip), F32)
        out_spec = pl.BlockSpec((None, TMw, TNw), lambda i, j, k: (place(j) // nb, i, place(j) % nb))

    def body(a_ref, b_ref, o_ref):
        @pl.when(pl.program_id(2) == 0)
        def _():
            o_ref[...] = jnp.zeros_like(o_ref)
        o_ref[...] += lax.dot_general(a_ref[...], b_ref[...], (((0,), (0,)), ((), ())),
                                      preferred_element_type=F32)

    blocks = [((TK, TMw), BF16), ((TK, TNw), BF16), ((TMw, TNw), F32)]
    return pl.pallas_call(
        body, name=name, grid=(M // TMw, N // TNw, nK),
        in_specs=[pl.BlockSpec((TK, TMw), lambda i, j, k: (k, i)),
                  pl.BlockSpec((TK, TNw), lambda i, j, k: (k, j))],
        out_specs=out_spec, out_shape=out_shape,
        compiler_params=_params(("arbitrary", "arbitrary", "arbitrary"), blocks,
                                temps=[((TMw, TNw), F32), ((TK, TMw), BF16)]),
    )(a, b)


def _prev_rows(TM, H, col):
    return lambda i: (jnp.maximum(i * (TM // H) - 1, 0), col)


def _next_rows(S, TM, H, col):
    return lambda i: (jnp.minimum((i + 1) * (TM // H), S // H - 1), col)


def _taps_causal(ext_ref, w_ref, K, H, TM, cs):
    acc = None
    for k in range(K):
        term = ext_ref[pl.ds(H - (K - 1) + k, TM), cs] * w_ref[pl.ds(k, 1), cs]
        acc = term if acc is None else acc + term
    return acc


def _taps_anticausal(ext_ref, w_ref, K, TM, cs):
    acc = None
    for k in range(K):
        term = ext_ref[pl.ds(K - 1 - k, TM), cs] * w_ref[pl.ds(k, 1), cs]
        acc = term if acc is None else acc + term
    return acc


def _tap_grads(ext_ref, g, K, H, TM, cs):
    return [_rsum(ext_ref[pl.ds(H - (K - 1) + k, TM), cs] * g) for k in range(K)]


def _shift_copies(ext_ref, shifted, cs):
    n = shifted.shape[1]
    for r in range(1, SUBLANES):
        shifted[r - 1] = ext_ref[pl.ds(r, n), cs]


def _rows_at(ext_ref, shifted, start, n, cs):
    q, r = divmod(start, SUBLANES)
    if r == 0:
        return ext_ref[pl.ds(start, n), cs]
    return shifted[r - 1, pl.ds(SUBLANES * q, n), :]


def _mixer_fwd(z, conv_a_w, conv_a_b, ln_g, ln_b, conv_b_w, S, TM, A, side=None):
    H = 32
    KA, KB = conv_a_w.shape[0], conv_b_w.shape[0]
    n_chunks = A // LANES
    RB = _pick(TM, (64, 32))

    def body(zc_ref, zh_ref, wa_ref, ba_ref, g_ref, b_ref, wb_ref, a1_ref, cat_ref, ext_a, ext_b, shifted):
        i = pl.program_id(0)
        live = (i > 0).astype(F32)
        zc = zc_ref[...].astype(F32)
        zh = zh_ref[...].astype(F32) * live
        ext_a[pl.ds(0, H), :] = zh[:, 0:A] * _sigmoid(zh[:, A:2 * A])
        ext_a[pl.ds(H, TM), :] = zc[:, 0:A] * _sigmoid(zc[:, A:2 * A])
        ext_b[pl.ds(0, H), :] = zh[:, 3 * A:4 * A] * zh[:, 4 * A:5 * A]
        ext_b[pl.ds(H, TM), :] = zc[:, 3 * A:4 * A] * zc[:, 4 * A:5 * A]

        def chunk(c, carry):
            cs = pl.ds(pl.multiple_of(c * LANES, LANES), LANES)
            _shift_copies(ext_a, shifted, cs)
            for r0 in range(0, TM, RB):
                acc = None
                for k in range(KA):
                    term = _rows_at(ext_a, shifted, H - (KA - 1) + k + r0, RB, cs) * wa_ref[pl.ds(k, 1), cs]
                    acc = term if acc is None else acc + term
                a1_ref[pl.ds(r0, RB), cs] = acc + ba_ref[:, cs]
            return carry
        lax.fori_loop(0, n_chunks, chunk, 0)

        a1 = a1_ref[...]
        mu = jnp.mean(a1, axis=-1, keepdims=True)
        d = a1 - mu
        var = jnp.mean(d * d, axis=-1, keepdims=True)
        a2 = d * lax.rsqrt(var + EPS) * g_ref[...] + b_ref[...]
        cat_ref[:, 0:A] = (a2 * _sigmoid(a2)).astype(BF16)
        cbc = _taps_causal(ext_b, wb_ref, KB, H, TM, slice(None))
        cat_ref[:, A:2 * A] = (zc[:, 2 * A:3 * A] * cbc).astype(BF16)

    blocks = [((TM, 5 * A), BF16), ((H, 5 * A), BF16), ((KA, A), F32), ((KB, A), F32),
              ((TM, A), F32), ((TM, 2 * A), BF16)]
    scratch = [((H + TM, A), F32), ((H + TM, A), F32), ((SUBLANES - 1, H + TM - SUBLANES, LANES), F32)]
    vec = lambda r: pl.BlockSpec((r, A), lambda i: (0, 0))
    outs, side_outs = _call(
        body, side, name="mixer_fwd", grid=(S // TM,),
        in_specs=[pl.BlockSpec((TM, 5 * A), lambda i: (i, 0)),
                  pl.BlockSpec((H, 5 * A), _prev_rows(TM, H, 0)),
                  vec(KA), vec(1), vec(1), vec(1), vec(KB)],
        out_specs=[pl.BlockSpec((TM, A), lambda i: (i, 0)), pl.BlockSpec((TM, 2 * A), lambda i: (i, 0))],
        out_shape=[jax.ShapeDtypeStruct((S, A), F32), jax.ShapeDtypeStruct((S, 2 * A), BF16)],
        scratch=scratch,
        params=_params(("arbitrary",), blocks, scratch, temps=[((TM, 5 * A), F32)] * 2 + [((TM, A), F32)] * 10),
        args=[z, z, conv_a_w, conv_a_b, ln_g, ln_b, conv_b_w])
    return outs if side is None else (outs, side_outs)


def _pair_tile(nF):
    return lambda t: (t % 2) * nF + t // 2


FFN_ROWS = 32


def _bcast_taps(w_ref, K, lanes):
    return [jnp.broadcast_to(w_ref[pl.ds(k, 1), lanes], (FFN_ROWS, LANES)) for k in range(K)]


def _ffn_act(u0, conv_w, S, TM, F, TC, side=None):
    H = 16
    K = conv_w.shape[0]
    nF = F // TC

    def body(uc_ref, uh_ref, wg_ref, wu_ref, o_ref, conv_ref, ext):
        live = (pl.program_id(0) > 0).astype(F32)
        ext[pl.ds(0, H), :] = uh_ref[...].astype(F32) * live
        ext[pl.ds(H, TM), :] = uc_ref[...].astype(F32)

        def lane_chunk(c, carry):
            lo = pl.ds(pl.multiple_of(c * LANES, LANES), LANES)
            lg, lu = lo, pl.ds(pl.multiple_of(TC + c * LANES, LANES), LANES)
            wg, wu = _bcast_taps(wg_ref, K, lo), _bcast_taps(wu_ref, K, lo)
            for r0 in range(0, TM, FFN_ROWS):
                g = u = None
                for k in range(K):
                    rows = pl.ds(H - (K - 1) + k + r0, FFN_ROWS)
                    tg, tu = ext[rows, lg] * wg[k], ext[rows, lu] * wu[k]
                    g, u = (tg, tu) if g is None else (g + tg, u + tu)
                o_ref[pl.ds(r0, FFN_ROWS), lo] = (g * _sigmoid(g) * u).astype(BF16)
                conv_ref[pl.ds(r0, FFN_ROWS), lg] = g.astype(BF16)
                conv_ref[pl.ds(r0, FFN_ROWS), lu] = u.astype(BF16)
            return carry
        lax.fori_loop(0, TC // LANES, lane_chunk, 0)

    blocks = [((TM, 2 * TC), BF16), ((H, 2 * TC), BF16), ((K, TC), F32), ((K, TC), F32), ((TM, TC), BF16),
              ((TM, 2 * TC), BF16)]
    scratch = [((H + TM, 2 * TC), F32)]
    outs, side_outs = _call(
        body, side, name="ffn_act", grid=(S // TM, nF),
        in_specs=[pl.BlockSpec((TM, 2 * TC), lambda i, j: (i, j)),
                  pl.BlockSpec((H, 2 * TC), lambda i, j: (jnp.maximum(i * (TM // H) - 1, 0), j)),
                  pl.BlockSpec((K, TC), lambda i, j: (0, j)),
                  pl.BlockSpec((K, TC), lambda i, j: (0, j + nF))],
        out_specs=[pl.BlockSpec((TM, TC), lambda i, j: (i, j)), pl.BlockSpec((TM, 2 * TC), lambda i, j: (i, j))],
        out_shape=[jax.ShapeDtypeStruct((S, F), BF16), jax.ShapeDtypeStruct((S, 2 * F), BF16)],
        scratch=scratch,
        params=_params(("arbitrary", "arbitrary"), blocks, scratch, temps=[((TM, 2 * TC), F32)]),
        args=[u0, u0, conv_w, conv_w])
    return outs if side is None else (outs, side_outs)


def _ple_loss(h2, p, target, w_gate, w_proj, b_gate, g_final, S, TM):
    D, P = h2.shape[1], p.shape[1]

    def body(h_ref, p_ref, t_ref, wg_ref, wp_ref, b_ref, g_ref,
             loss_ref, dg_ref, db_ref, dh_ref, dpre_ref, dpp_ref, hb_ref, pb_ref, pre_sc, pp_sc):
        @pl.when(pl.program_id(0) == 0)
        def _():
            loss_ref[...] = jnp.zeros_like(loss_ref)
            dg_ref[...] = jnp.zeros_like(dg_ref)
            db_ref[...] = jnp.zeros_like(db_ref)
        hb_ref[...] = h_ref[...].astype(BF16)
        pb_ref[...] = p_ref[...].astype(BF16)
        pre_sc[...] = jnp.dot(hb_ref[...], wg_ref[...], preferred_element_type=F32)
        pp_sc[...] = jnp.dot(pb_ref[...], wp_ref[...], preferred_element_type=F32)

        def chunk(ci, carry):
            rows = pl.ds(pl.multiple_of(ci * ROW_CHUNK, ROW_CHUNK), ROW_CHUNK)
            g = g_ref[...]
            gate = _sigmoid(pre_sc[rows, :] + b_ref[...])
            pp = pp_sc[rows, :]
            h = h_ref[rows, :] + pp * gate
            r = _rms_stats(h)
            n = h * r
            diff = n * g - t_ref[rows, :]
            loss_ref[...] += 0.5 * jnp.sum(jnp.mean(diff * diff, axis=-1, keepdims=True), axis=0, keepdims=True)
            dy = diff * (1.0 / D)
            dn = dy * g
            dh = r * (dn - n * jnp.mean(dn * n, axis=-1, keepdims=True))
            dh_ref[rows, :] = dh
            dg_ref[...] += _rsum(dy * n)
            dpre = dh * pp * gate * (1.0 - gate)
            dpre_ref[rows, :] = dpre.astype(BF16)
            dpp_ref[rows, :] = (dh * gate).astype(BF16)
            db_ref[...] += _rsum(dpre)
            return carry
        lax.fori_loop(0, TM // ROW_CHUNK, chunk, 0)

    row = pl.BlockSpec((TM, D), lambda i: (i, 0))
    prow = pl.BlockSpec((TM, P), lambda i: (i, 0))
    vec = pl.BlockSpec((1, D), lambda i: (0, 0))
    whole = lambda a: pl.BlockSpec(a.shape, lambda i: (0, 0))
    blocks = ([((TM, D), F32)] * 3 + [((TM, P), F32), ((D, D), BF16), ((P, D), BF16)]
              + [((TM, D), BF16)] * 3 + [((TM, P), BF16)])
    scratch = [((TM, D), F32)] * 2
    return pl.pallas_call(
        body, name="ple_loss", grid=(S // TM,),
        in_specs=[row, prow, row, whole(w_gate), whole(w_proj), vec, vec],
        out_specs=[pl.BlockSpec((1, 1), lambda i: (0, 0)), vec, vec, row, row, row, row, prow],
        out_shape=[jax.ShapeDtypeStruct((1, 1), F32), jax.ShapeDtypeStruct((1, D), F32),
                   jax.ShapeDtypeStruct((1, D), F32), jax.ShapeDtypeStruct((S, D), F32),
                   jax.ShapeDtypeStruct((S, D), BF16), jax.ShapeDtypeStruct((S, D), BF16),
                   jax.ShapeDtypeStruct((S, D), BF16), jax.ShapeDtypeStruct((S, P), BF16)],
        scratch_shapes=[pltpu.VMEM(s, d) for s, d in scratch],
        compiler_params=_params(("arbitrary",), blocks, scratch, temps=[((TM, D), F32)] * 2),
    )(h2, p, target, w_gate, w_proj, b_gate, g_final)


def _ffn_bwd(u0, conv_u0, dact, conv_w, S, TM, F, TC, side=None):
    H = FFN_ROWS
    K = conv_w.shape[0]
    nF, nI = F // TC, S // TM

    def body(xc_ref, cc_ref, cn_ref, dc_ref, dn_ref, wg_ref, wu_ref, o_ref, dwg_ref, dwu_ref, ext_d):
        i = pl.program_id(1)
        @pl.when(i == 0)
        def _():
            dwg_ref[...] = jnp.zeros_like(dwg_ref)
            dwu_ref[...] = jnp.zeros_like(dwu_ref)
        last = (i < nI - 1).astype(F32)

        def lane_chunk(c, carry):
            lo = pl.ds(pl.multiple_of(c * LANES, LANES), LANES)
            lg, lu = lo, pl.ds(pl.multiple_of(TC + c * LANES, LANES), LANES)
            wg, wu = _bcast_taps(wg_ref, K, lo), _bcast_taps(wu_ref, K, lo)
            for r0 in range(0, TM + H, FFN_ROWS):
                if r0 < TM:
                    rows = pl.ds(r0, FFN_ROWS)
                    g, u, da = cc_ref[rows, lg], cc_ref[rows, lu], dc_ref[rows, lo].astype(F32)
                else:
                    g, u, da = cn_ref[:, lg], cn_ref[:, lu], dn_ref[:, lo].astype(F32) * last
                g, u = g.astype(F32), u.astype(F32)
                s = _sigmoid(g)
                ext_d[pl.ds(r0, FFN_ROWS), lg] = da * u * s * (1.0 + g * (1.0 - s))
                ext_d[pl.ds(r0, FFN_ROWS), lu] = da * g * s
            sums_g, sums_u = [None] * K, [None] * K
            for r0 in range(0, TM, FFN_ROWS):
                xg = xc_ref[pl.ds(r0, FFN_ROWS), lg].astype(F32)
                xu = xc_ref[pl.ds(r0, FFN_ROWS), lu].astype(F32)
                g = u = None
                for k in range(K):
                    rows = pl.ds(K - 1 - k + r0, FFN_ROWS)
                    dg, du = ext_d[rows, lg], ext_d[rows, lu]
                    tg, tu = dg * wg[k], du * wu[k]
                    g, u = (tg, tu) if g is None else (g + tg, u + tu)
                    pg, pu = xg * dg, xu * du
                    sums_g[k] = pg if sums_g[k] is None else sums_g[k] + pg
                    sums_u[k] = pu if sums_u[k] is None else sums_u[k] + pu
                o_ref[pl.ds(r0, FFN_ROWS), lg] = g.astype(BF16)
                o_ref[pl.ds(r0, FFN_ROWS), lu] = u.astype(BF16)
            for k in range(K):
                dwg_ref[pl.ds(k, 1), lo] += _rsum(sums_g[k])
                dwu_ref[pl.ds(k, 1), lo] += _rsum(sums_u[k])
            return carry
        lax.fori_loop(0, TC // LANES, lane_chunk, 0)

    blocks = [((TM, 2 * TC), BF16), ((TM, 2 * TC), BF16), ((H, 2 * TC), BF16), ((TM, TC), BF16), ((H, TC), BF16),
              ((K, TC), F32), ((K, TC), F32), ((TM, 2 * TC), BF16), ((K, TC), F32), ((K, TC), F32)]
    scratch = [((TM + H, 2 * TC), F32)]
    nxt = lambda j, i: (jnp.minimum((i + 1) * (TM // H), S // H - 1), j)
    tile = pl.BlockSpec((TM, 2 * TC), lambda j, i: (i, j))
    taps_out = pl.BlockSpec((K, TC), lambda j, i: (0, j))
    outs, side_outs = _call(
        body, side, name="ffn_bwd", grid=(nF, nI),
        in_specs=[tile, tile, pl.BlockSpec((H, 2 * TC), nxt),
                  pl.BlockSpec((TM, TC), lambda j, i: (i, j)), pl.BlockSpec((H, TC), nxt),
                  pl.BlockSpec((K, TC), lambda j, i: (0, j)), pl.BlockSpec((K, TC), lambda j, i: (0, j + nF))],
        out_specs=[tile, taps_out, taps_out],
        out_shape=[jax.ShapeDtypeStruct((S, 2 * F), BF16), jax.ShapeDtypeStruct((K, F), F32),
                   jax.ShapeDtypeStruct((K, F), F32)],
        scratch=scratch,
        params=_params(("arbitrary", "arbitrary"), blocks, scratch),
        args=[u0, conv_u0, conv_u0, dact, dact, conv_w, conv_w])
    return outs if side is None else (outs, side_outs)


def _mixer_bwd_ln(dcat, a1, ln_g, ln_b, S, TM, A):
    def body(dc_ref, a1_ref, g_ref, b_ref, da1_ref, acc_ref):
        @pl.when(pl.program_id(0) == 0)
        def _():
            acc_ref[...] = jnp.zeros_like(acc_ref)
        a1 = a1_ref[...]
        g = g_ref[...]
        mu = jnp.mean(a1, axis=-1, keepdims=True)
        d = a1 - mu
        rstd = lax.rsqrt(jnp.mean(d * d, axis=-1, keepdims=True) + EPS)
        nh = d * rstd
        a2 = nh * g + b_ref[...]
        s = _sigmoid(a2)
        da2 = dc_ref[...].astype(F32) * s * (1.0 + a2 * (1.0 - s))
        dnh = da2 * g
        da1 = rstd * (dnh - jnp.mean(dnh, axis=-1, keepdims=True)
                      - nh * jnp.mean(dnh * nh, axis=-1, keepdims=True))
        da1_ref[...] = da1
        acc_ref[pl.ds(0, 1), :] += _rsum(da2 * nh)
        acc_ref[pl.ds(1, 1), :] += _rsum(da2)
        acc_ref[pl.ds(2, 1), :] += _rsum(da1)

    blocks = [((TM, A), BF16), ((TM, A), F32), ((TM, A), F32), ((4, A), F32)]
    return pl.pallas_call(
        body, name="mixer_bwd_ln", grid=(S // TM,),
        in_specs=[pl.BlockSpec((TM, A), lambda i: (i, 0)), pl.BlockSpec((TM, A), lambda i: (i, 0)),
                  pl.BlockSpec((1, A), lambda i: (0, 0)), pl.BlockSpec((1, A), lambda i: (0, 0))],
        out_specs=[pl.BlockSpec((TM, A), lambda i: (i, 0)), pl.BlockSpec((4, A), lambda i: (0, 0))],
        out_shape=[jax.ShapeDtypeStruct((S, A), F32), jax.ShapeDtypeStruct((4, A), F32)],
        compiler_params=_params(("arbitrary",), blocks, temps=[((TM, A), F32)] * 12),
    )(dcat, a1, ln_g, ln_b)


def _mixer_bwd_conv(z, dcat, da1, conv_a_w, conv_b_w, S, TM, A, side=None):
    H = 32
    KA, KB = conv_a_w.shape[0], conv_b_w.shape[0]
    nI = S // TM
    n_chunks = A // LANES
    RB = _pick(TM, (64, 32))

    def body(zc_ref, zp_ref, zn_ref, dbc_ref, dbn_ref, d1c_ref, d1n_ref, wa_ref, wb_ref,
             dz_ref, dwa_ref, dwb_ref, ext_a0, ext_d1, ext_cb, ext_dc, da0_sc, shifted_d, shifted_a):
        i = pl.program_id(0)
        @pl.when(i == 0)
        def _():
            dwa_ref[...] = jnp.zeros_like(dwa_ref)
            dwb_ref[...] = jnp.zeros_like(dwb_ref)
        first = (i > 0).astype(F32)
        last = (i < nI - 1).astype(F32)
        zc = zc_ref[...].astype(F32)
        zp = zp_ref[...].astype(F32) * first
        a_val, a_gate = zc[:, 0:A], zc[:, A:2 * A]
        b_gate, c_gate, b_h = zc[:, 2 * A:3 * A], zc[:, 3 * A:4 * A], zc[:, 4 * A:5 * A]
        sig = _sigmoid(a_gate)
        ext_a0[pl.ds(0, H), :] = zp[:, 0:A] * _sigmoid(zp[:, A:2 * A])
        ext_a0[pl.ds(H, TM), :] = a_val * sig
        ext_d1[pl.ds(0, TM), :] = d1c_ref[...]
        ext_d1[pl.ds(TM, H), :] = d1n_ref[...] * last
        ext_cb[pl.ds(0, H), :] = zp[:, 3 * A:4 * A] * zp[:, 4 * A:5 * A]
        ext_cb[pl.ds(H, TM), :] = c_gate * b_h
        dbx = dbc_ref[...].astype(F32)
        dcbc = dbx * b_gate
        ext_dc[pl.ds(0, TM), :] = dcbc
        ext_dc[pl.ds(TM, H), :] = dbn_ref[...].astype(F32) * zn_ref[...].astype(F32) * last

        def chunk(c, carry):
            cs = pl.ds(pl.multiple_of(c * LANES, LANES), LANES)
            _shift_copies(ext_d1, shifted_d, cs)
            _shift_copies(ext_a0, shifted_a, cs)
            for r0 in range(0, TM, RB):
                acc = None
                for k in range(KA):
                    term = _rows_at(ext_d1, shifted_d, KA - 1 - k + r0, RB, cs) * wa_ref[pl.ds(k, 1), cs]
                    acc = term if acc is None else acc + term
                da0_sc[pl.ds(r0, RB), cs] = acc
            for k in range(KA):
                acc = None
                for r0 in range(0, TM, RB):
                    term = (_rows_at(ext_a0, shifted_a, H - (KA - 1) + k + r0, RB, cs)
                            * ext_d1[pl.ds(r0, RB), cs])
                    acc = term if acc is None else acc + term
                dwa_ref[pl.ds(k, 1), cs] += _rsum(acc)
            return carry
        lax.fori_loop(0, n_chunks, chunk, 0)

        da0 = da0_sc[...]
        dz_ref[:, 0:A] = (da0 * sig).astype(BF16)
        dz_ref[:, A:2 * A] = (da0 * a_val * sig * (1.0 - sig)).astype(BF16)
        cbc = _taps_causal(ext_cb, wb_ref, KB, H, TM, slice(None))
        dz_ref[:, 2 * A:3 * A] = (dbx * cbc).astype(BF16)
        dcb = _taps_anticausal(ext_dc, wb_ref, KB, TM, slice(None))
        dz_ref[:, 3 * A:4 * A] = (dcb * b_h).astype(BF16)
        dz_ref[:, 4 * A:5 * A] = (dcb * c_gate).astype(BF16)
        grads = _tap_grads(ext_cb, dcbc, KB, H, TM, slice(None))
        for k in range(KB):
            dwb_ref[pl.ds(k, 1), :] += grads[k]

    blocks = [((TM, 5 * A), BF16), ((H, 5 * A), BF16), ((H, A), BF16), ((TM, A), BF16), ((H, A), BF16),
              ((TM, A), F32), ((H, A), F32), ((KA, A), F32), ((KB, A), F32),
              ((TM, 5 * A), BF16), ((KA, A), F32), ((KB, A), F32)]
    scratch = ([((H + TM, A), F32)] * 4 + [((TM, A), F32)]
               + [((SUBLANES - 1, H + TM - SUBLANES, LANES), F32)] * 2)
    vec = lambda r: pl.BlockSpec((r, A), lambda i: (0, 0))
    outs, side_outs = _call(
        body, side, name="mixer_bwd_conv", grid=(nI,),
        in_specs=[pl.BlockSpec((TM, 5 * A), lambda i: (i, 0)),
                  pl.BlockSpec((H, 5 * A), _prev_rows(TM, H, 0)),
                  pl.BlockSpec((H, A), _next_rows(S, TM, H, 2)),
                  pl.BlockSpec((TM, A), lambda i: (i, 1)),
                  pl.BlockSpec((H, A), _next_rows(S, TM, H, 1)),
                  pl.BlockSpec((TM, A), lambda i: (i, 0)),
                  pl.BlockSpec((H, A), _next_rows(S, TM, H, 0)),
                  vec(KA), vec(KB)],
        out_specs=[pl.BlockSpec((TM, 5 * A), lambda i: (i, 0)), vec(KA), vec(KB)],
        out_shape=[jax.ShapeDtypeStruct((S, 5 * A), BF16), jax.ShapeDtypeStruct((KA, A), F32),
                   jax.ShapeDtypeStruct((KB, A), F32)],
        scratch=scratch,
        params=_params(("arbitrary",), blocks, scratch, temps=[((TM, 5 * A), F32)] * 2 + [((TM, A), F32)] * 14),
        args=[z, z, z, dcat, dcat, da1, da1, conv_a_w, conv_b_w])
    return outs if side is None else (outs, side_outs)


def _row_tile(R):
    return _pick(R, (256, 128, 64, 32, 16, 8))


def _scalars(*vals):
    return jnp.stack([jnp.asarray(v, jnp.int32) for v in vals])


def _cast_into_gathered(name, w, chip):
    R, C = w.shape
    TR = _row_tile(R)

    def body(s_ref, w_ref, o_ref):
        o_ref[...] = w_ref[...].astype(BF16)

    grid_spec = pltpu.PrefetchScalarGridSpec(
        num_scalar_prefetch=1, grid=(R // TR,),
        in_specs=[pl.BlockSpec((TR, C), lambda r, s: (r, 0))],
        out_specs=pl.BlockSpec((None, TR, C), lambda r, s: (s[0], r, 0)))
    return pl.pallas_call(body, name=name, grid_spec=grid_spec,
                          out_shape=jax.ShapeDtypeStruct((N_CHIPS, R, C), BF16),
                          compiler_params=_params(("arbitrary",), [((TR, C), F32), ((TR, C), BF16)]),
                          )(_scalars(chip), w)


def _add_pair(name, dw, recv, c, chip):
    _, _, Rh, C = dw.shape
    TR = _row_tile(Rh)

    def body(s_ref, a_ref, b_ref, o_ref, ob_ref):
        s = a_ref[...] + b_ref[...]
        ob_ref[...] = s.astype(BF16)

        @pl.when(pl.program_id(1) == s_ref[1])
        def _():
            o_ref[...] = s

    grid_spec = pltpu.PrefetchScalarGridSpec(
        num_scalar_prefetch=1, grid=(Rh // TR, N_CHIPS),
        in_specs=[pl.BlockSpec((None, None, TR, C), lambda r, k, s: (k, s[0], r, 0)),
                  pl.BlockSpec((None, TR, C), lambda r, k, s: (k, r, 0))],
        out_specs=[pl.BlockSpec((TR, C), lambda r, k, s: (r, 0)),
                   pl.BlockSpec((None, TR, C), lambda r, k, s: (k, r, 0))])
    return pl.pallas_call(body, name=name, grid_spec=grid_spec,
                          out_shape=[jax.ShapeDtypeStruct((Rh, C), F32),
                                     jax.ShapeDtypeStruct((N_CHIPS, Rh, C), BF16)],
                          compiler_params=_params(("arbitrary", "arbitrary"), [((TR, C), F32)] * 4),
                          )(_scalars(c, chip), *_in_hbm([dw, recv]))


def _add_chips(name, own, recv, c):
    Rh, C = own.shape
    TR = _row_tile(Rh)

    def body(s_ref, p_ref, r_ref, o_ref):
        o_ref[...] = ((p_ref[...] + r_ref[0].astype(F32)) + r_ref[1].astype(F32)) + r_ref[2].astype(F32)

    grid_spec = pltpu.PrefetchScalarGridSpec(
        num_scalar_prefetch=1, grid=(Rh // TR,),
        in_specs=[pl.BlockSpec((TR, C), lambda r, s: (r, 0)),
                  pl.BlockSpec((N_CHIPS - 1, TR, C), lambda r, s: (0, r, 0))],
        out_specs=pl.BlockSpec((None, TR, C), lambda r, s: (s[0], r, 0)))
    return pl.pallas_call(body, name=name, grid_spec=grid_spec,
                          out_shape=jax.ShapeDtypeStruct((2, Rh, C), F32),
                          compiler_params=_params(("arbitrary",), [((N_CHIPS + 1, TR, C), F32)]),
                          )(_scalars(c), *_in_hbm([own, recv]))


def _sum_devices(name, parts):
    _, R, C = parts.shape

    def body(p_ref, o_ref):
        acc = p_ref[0]
        for d in range(1, N_DEV):
            acc = acc + p_ref[d]
        o_ref[...] = acc

    return pl.pallas_call(body, name=name, out_shape=jax.ShapeDtypeStruct((R, C), F32),
                          in_specs=[pl.BlockSpec(memory_space=pltpu.VMEM)],
                          out_specs=pl.BlockSpec(memory_space=pltpu.VMEM))(parts)


def _adamw(name, w, g, m, v, copy_grad=False):
    R, C = w.shape
    TR = _pick(R, (128, 64, 32, 16, 8))
    c1 = 1.0 - ADAM_B1 ** ADAM_STEP
    c2 = 1.0 - ADAM_B2 ** ADAM_STEP
    n_out = 4 if copy_grad else 3

    def body(w_ref, g_ref, m_ref, v_ref, d_ref, nm_ref, nv_ref, *g_out):
        g_ = g_ref[...]
        nm = ADAM_B1 * m_ref[...] + (1.0 - ADAM_B1) * g_
        nv = ADAM_B2 * v_ref[...] + (1.0 - ADAM_B2) * (g_ * g_)
        d_ref[...] = -ADAM_LR * ((nm / c1) / (jnp.sqrt(nv / c2) + ADAM_EPS) + ADAM_WD * w_ref[...])
        nm_ref[...] = nm
        nv_ref[...] = nv
        for ref in g_out:
            ref[...] = g_

    spec = pl.BlockSpec((TR, C), lambda r: (r, 0))
    shp = jax.ShapeDtypeStruct((R, C), F32)
    return pl.pallas_call(body, name=name, grid=(R // TR,), in_specs=[spec] * 4, out_specs=[spec] * n_out,
                          out_shape=[shp] * n_out,
                          compiler_params=_params(("arbitrary",), [((TR, C), F32)] * (4 + n_out)),
                          )(w, *_in_hbm([g]), m, v)


def _place():
    x, y, c = lax.axis_index("x"), lax.axis_index("y"), lax.axis_index("c")
    others = [(1 - x, y), (x, 1 - y), (1 - x, 1 - y)]
    return x, y, c, others


def _allgather_small(name, block):
    R, C = block.shape

    def body(x_ref, out_ref, send_sems, recv_sems, local_sem):
        x, y, c, chips = _place()
        me, sibling = (x, y, c), (x, y, 1 - c)

        def rows(px, py, pc):
            return out_ref.at[4 * px + 2 * py + pc]

        def copy(k, blk, to, src=None):
            return pltpu.make_async_remote_copy(
                src_ref=rows(*blk) if src is None else src, dst_ref=rows(*blk),
                send_sem=send_sems.at[k], recv_sem=recv_sems.at[k], device_id=to, device_id_type=MESH)

        mine = pltpu.make_async_copy(x_ref, rows(*me), local_sem)
        mine.start()
        first = [copy(0, me, sibling, src=x_ref)]
        first += [copy(1 + j, me, (*chip, c), src=x_ref) for j, chip in enumerate(chips)]
        for cp in first:
            cp.start()
        passed = [copy(4 + j, (*chip, c), sibling) for j, chip in enumerate(chips)]
        for j, chip in enumerate(chips):
            copy(1 + j, (*chip, c), me).wait_recv()
            passed[j].start()
        copy(0, sibling, me).wait_recv()
        for j, chip in enumerate(chips):
            copy(4 + j, (*chip, 1 - c), me).wait_recv()
        for cp in first + passed:
            cp.wait_send()
        mine.wait()

    return pl.pallas_call(
        body, name=name, out_shape=jax.ShapeDtypeStruct((N_DEV, R, C), F32),
        in_specs=[pl.BlockSpec(memory_space=pltpu.VMEM)], out_specs=pl.BlockSpec(memory_space=pltpu.VMEM),
        scratch_shapes=[pltpu.SemaphoreType.DMA((7,)), pltpu.SemaphoreType.DMA((7,)), pltpu.SemaphoreType.DMA],
    )(block)


def _gather_side(bufs, across, within):
    def rows(ref, chip, half, piece):
        _, r0, n = piece
        return ref.at[2 * chip[0] + chip[1], pl.ds(half * (ref.shape[1] // 2) + r0, n)]

    def copies(ins, outs, send_sems, recv_sems, base):
        x, y, c, chips = _place()
        sibling = (x, y, 1 - c)
        pairs = []

        def add(k, src, dst, to, arrival):
            mk = lambda s, d, dev: pltpu.make_async_remote_copy(
                src_ref=s, dst_ref=d, send_sem=send_sems.at[base + k], recv_sem=recv_sems.at[base + k],
                device_id=dev, device_id_type=MESH)
            pairs.append((mk(src, dst, to), mk(arrival, arrival, (x, y, c))))

        for p, piece in enumerate(across):
            ref = outs[piece[0]]
            for j, chip in enumerate(chips):
                mine = rows(ref, (x, y), c, piece)
                add(3 * p + j, mine, mine, (*chip, c), rows(ref, chip, c, piece))
        for q, piece in enumerate(within):
            ref = outs[piece[0]]
            for j, chip in enumerate(chips):
                held = rows(ref, chip, c, piece)
                add(3 * (len(across) + q) + j, held, held, sibling, rows(ref, chip, 1 - c, piece))
        return pairs

    def start(*refs):
        for send, _ in copies(*refs):
            send.start()

    def wait(*refs):
        pairs = copies(*refs)
        for _, arrival in pairs:
            arrival.wait_recv()
        for send, _ in pairs:
            send.wait_send()

    return _Side(list(bufs), [jax.ShapeDtypeStruct(b.shape, b.dtype) for b in bufs],
                 3 * (len(across) + len(within)), start, wait, aliases=tuple((i, i) for i in range(len(bufs))))


def _chip_exchange(parts):
    n = len(parts)

    def copies(ins, outs, send_sems, recv_sems, base):
        x, y, c, chips = _place()
        return [pltpu.make_async_remote_copy(
            src_ref=ins[a].at[2 * chip[0] + chip[1]], dst_ref=outs[a].at[j],
            send_sem=send_sems.at[base + 3 * a + j], recv_sem=recv_sems.at[base + 3 * a + j],
            device_id=(*chip, c), device_id_type=MESH) for a in range(n) for j, chip in enumerate(chips)]

    return _Side(list(parts), [jax.ShapeDtypeStruct((N_CHIPS - 1,) + p.shape[1:], p.dtype) for p in parts],
                 3 * n, *_start_wait(copies))


def _pair_exchange(grads):
    def copies(ins, outs, send_sems, recv_sems, base):
        x, y, c, _ = _place()
        return [pltpu.make_async_remote_copy(
            src_ref=ins[a].at[:, 1 - c], dst_ref=outs[a], send_sem=send_sems.at[base + a],
            recv_sem=recv_sems.at[base + a], device_id=(x, y, 1 - c), device_id_type=MESH)
            for a in range(len(grads))]

    return _Side(list(grads), [jax.ShapeDtypeStruct((N_CHIPS,) + g.shape[2:], F32) for g in grads],
                 len(grads), *_start_wait(copies))


def _start_wait(copies):
    def start(*refs):
        for cp in copies(*refs):
            cp.start()

    def wait(*refs):
        cps = copies(*refs)
        for cp in cps:
            cp.wait_recv()
        for cp in cps:
            cp.wait_send()
    return start, wait


def _both(first, second):
    n_in, n_out = len(first.ins), len(first.out_shapes)

    def run(which):
        def go(ins, outs, send_sems, recv_sems, base):
            getattr(first, which)(ins[:n_in], outs[:n_out], send_sems, recv_sems, base)
            getattr(second, which)(ins[n_in:], outs[n_out:], send_sems, recv_sems, base + first.n_sems)
        return go

    aliases = first.aliases + tuple((a + n_in, b + n_out) for a, b in second.aliases)
    return _Side(first.ins + second.ins, first.out_shapes + second.out_shapes,
                 first.n_sems + second.n_sems, run("start"), run("wait"), aliases)


def _share_halves(halves):
    def copies(ins, outs, send_sems, recv_sems, base):
        x, y, c, _ = _place()
        pairs = []
        for a in range(len(halves)):
            mk = lambda s, d, dev, a=a: pltpu.make_async_remote_copy(
                src_ref=s, dst_ref=d, send_sem=send_sems.at[base + a], recv_sem=recv_sems.at[base + a],
                device_id=dev, device_id_type=MESH)
            theirs = outs[a].at[1 - c]
            pairs.append((mk(outs[a].at[c], outs[a].at[c], (x, y, 1 - c)), mk(theirs, theirs, (x, y, c))))
        return pairs

    def start(*refs):
        for send, _ in copies(*refs):
            send.start()

    def wait(*refs):
        pairs = copies(*refs)
        for _, arrival in pairs:
            arrival.wait_recv()
        for send, _ in pairs:
            send.wait_send()

    return _Side(list(halves), [jax.ShapeDtypeStruct(h.shape, F32) for h in halves], len(halves), start, wait,
                 aliases=tuple((i, i) for i in range(len(halves))))


def _pack(arrays):
    pieces = []
    for a in arrays:
        flat = a.reshape(-1).astype(F32)
        pieces.append(jnp.pad(flat, (0, (-flat.size) % PACK_ALIGN)))
    return jnp.concatenate(pieces).reshape(-1, LANES)


def _unpack(buf, shapes):
    lead = buf.shape[:-2]
    flat = buf.reshape(lead + (-1,))
    out, off = [], 0
    for shp in shapes:
        size = 1
        for s in shp:
            size *= s
        out.append(flat[..., off:off + size].reshape(lead + tuple(shp)))
        off += size + (-size) % PACK_ALIGN
    return out


def _gather_channels(buf, shapes):
    per_chip = _unpack(buf[0::2], shapes)
    return [jnp.transpose(a, (1, 0, 2)).reshape(a.shape[1], -1) for a in per_chip]


def _mm_tile(n, rows, limit_bytes=6 * 1024 * 1024):
    for t in (1408, 1280, 1024, 640, 512, 384, 256, 128):
        if n % t == 0 and rows * t * 2 <= limit_bytes:
            return t
    raise ValueError(f"no column tile for {n} x {rows}")


def kernel(x, p, norm_mix_g, w_in, conv_a_w, conv_a_b, ln_a_g, ln_a_b, conv_b_w, w_out, norm_ffn_g, w_up, conv_ffn_w, w_down, w_ple_gate, b_ple_gate, w_ple_proj, norm_final_g, loss_target, m_norm_mix_g, m_w_in, m_conv_a_w, m_conv_a_b, m_ln_a_g, m_ln_a_b, m_conv_b_w, m_w_out, m_norm_ffn_g, m_w_up, m_conv_ffn_w, m_w_down, m_w_ple_gate, m_b_ple_gate, m_w_ple_proj, m_norm_final_g, v_norm_mix_g, v_w_in, v_conv_a_w, v_conv_a_b, v_ln_a_g, v_ln_a_b, v_conv_b_w, v_w_out, v_norm_ffn_g, v_w_up, v_conv_ffn_w, v_w_down, v_w_ple_gate, v_b_ple_gate, v_w_ple_proj, v_norm_final_g):
    S, D = x.shape[1], x.shape[2]
    P = p.shape[3]
    A = conv_a_b.shape[1]
    F = w_down.shape[1] * N_CHIPS
    KA, KB, KF = conv_a_w.shape[1], conv_b_w.shape[1], conv_ffn_w.shape[1]
    xi, yi, ci = lax.axis_index("x"), lax.axis_index("y"), lax.axis_index("c")
    chip = 2 * xi + yi

    TM = _pick(S, (512, 256, 128))
    TL = _pick(S, (1024, 512, 256, 128))
    TE = _pick(S, (256, 128))
    TC = _pick(2 * F // N_CHIPS, (1408, 1024, 512, 256, 128))
    ffn_place = _pair_tile(F // TC)

    x2, p2, t2 = x.reshape(S, D), p.reshape(S, P), loss_target.reshape(S, D)
    gfin = norm_final_g.reshape(1, D)

    big = dict(w_in=w_in[0], w_out=w_out[0], w_up=w_up[0], w_down=w_down[0],
               w_ple_gate=w_ple_gate[0], w_ple_proj=w_ple_proj[0])
    names = list(big)
    buf = {n: _cast_into_gathered("cast_" + n, big[n], chip) for n in names}
    half = {n: big[n].shape[0] // 2 for n in names}
    up_a = half["w_up"] // 2
    w_in3 = _gather_pipelined("gather_w_in", buf["w_in"], 4)

    tap_shapes = [(KA, A // N_CHIPS), (KB, A // N_CHIPS), (KF, 2 * F // N_CHIPS)]
    taps = _allgather_small("allgather_taps", _pack([conv_a_w[0], conv_b_w[0], conv_ffn_w[0]]))
    conv_a_f, conv_b_f, conv_ffn_f = _gather_channels(taps, tap_shapes)

    def rms_prologue(rows, row_r, vec_r, ro_r):
        h = row_r[0][rows, :]
        hn = (h * _rms_stats(h) * vec_r[0][...]).astype(BF16)
        ro_r[0][rows, :] = hn
        return [hn]

    def cast_prologue(rows, row_r, vec_r, ro_r):
        hb = row_r[0][rows, :].astype(BF16)
        ro_r[0][rows, :] = hb
        return [hb]

    plain = lambda accs, tile_r: [accs[0]]
    residual = lambda accs, tile_r: [tile_r[0][...] + accs[0]]

    (z, hn1), (w_out_t, w_up_t) = _rows_mm(
        "in_proj", S, TL, 5 * A, _mm_tile(5 * A // N_CHIPS, D), row_ins=[x2], vec_ins=[norm_mix_g],
        weights=[(w_in3, "nn3")], tile_outs=[BF16], row_outs=[(D, BF16)], prologue=rms_prologue, epilogue=plain,
        side=_gather_side([buf["w_out"], buf["w_up"]], [(0, 0, half["w_out"]), (1, 0, up_a)], []))
    (a1, cat), (w_out3, w_up_t) = _mixer_fwd(
        z, conv_a_f, conv_a_b, ln_a_g, ln_a_b, conv_b_f, S, TE, A,
        side=_gather_side([w_out_t, w_up_t], [(1, up_a, half["w_up"] - up_a)],
                          [(0, 0, half["w_out"]), (1, 0, up_a)]))
    w_out_f = w_out3.reshape(2 * A, D)
    (h1,), (w_up3, w_proj_t) = _rows_mm(
        "out_proj", S, TL, D, _mm_tile(D, 2 * A), row_ins=[cat], weights=[(w_out_f, "nn2")],
        tile_ins=[x2], tile_outs=[F32], epilogue=residual,
        side=_gather_side([w_up_t, buf["w_ple_proj"]], [(1, 0, half["w_ple_proj"])],
                          [(0, up_a, half["w_up"] - up_a)]))
    (u0, hn2), (w_down_t, w_gate_t, w_proj3) = _rows_mm(
        "up_proj", S, TL, 2 * F, TC, row_ins=[h1], vec_ins=[norm_ffn_g],
        weights=[(w_up3, "nn3")], tile_outs=[BF16], row_outs=[(D, BF16)],
        prologue=rms_prologue, epilogue=plain, place=ffn_place,
        side=_gather_side([buf["w_down"], buf["w_ple_gate"], w_proj_t],
                          [(0, 0, half["w_down"]), (1, 0, half["w_ple_gate"])], [(2, 0, half["w_ple_proj"])]))
    (act, conv_u0), (w_down3, w_gate3) = _ffn_act(
        u0, conv_ffn_f, S, TM, F, TC,
        side=_gather_side([w_down_t, w_gate_t], [], [(0, 0, half["w_down"]), (1, 0, half["w_ple_gate"])]))
    w_down_f = w_down3.reshape(F, D)
    w_gate_f = w_gate3.reshape(D, D)
    w_proj_f = jnp.transpose(w_proj3, (1, 0, 2)).reshape(P, D)
    (h2,) = _rows_mm("down_proj", S, TL, D, _mm_tile(D, F), row_ins=[act], weights=[(w_down_f, "nn2")],
                     tile_ins=[h1], tile_outs=[F32], epilogue=residual)

    loss_part, g_norm_final, g_b_gate, dh3, dpre, dpp, h2b, pb = _ple_loss(
        h2, p2, t2, w_gate_f, w_proj_f, b_ple_gate, gfin, S, TE)

    TK = _pick(S, (2048, 1024, 512, 256, 128))
    wt = lambda n: _pick(n, (1408, 1280, 1024, 512, 256, 128))
    chip_sums, from_chips = {}, {}

    def to_sibling(parts):
        ns = list(parts)
        halves = [parts[n].reshape(N_CHIPS, 2, big[n].shape[0] // 2, big[n].shape[1]) for n in ns]
        return ns, halves, _pair_exchange(halves)

    def to_chips(ns, halves, from_sibling):
        sums = [_add_pair("pair_sum_" + n, h, r, ci, chip) for n, h, r in zip(ns, halves, from_sibling)]
        for n, (s, _) in zip(ns, sums):
            chip_sums[n] = s
        return ns, _chip_exchange([b for _, b in sums])

    def landed(ns, side_outs):
        for n, r in zip(ns, side_outs):
            from_chips[n] = r

    ns, halves, side = to_sibling(dict(
        w_ple_gate=_tn_mm("dw_ple_gate", h2b, dpre, wt(D), wt(D), TK),
        w_ple_proj=_tn_mm("dw_ple_proj", pb, dpp, wt(P), wt(D // N_CHIPS), TK, cols_per_chip=D // N_CHIPS)))
    (dh2,), got = _rows_mm("ple_bwd", S, TL, D, _mm_tile(D, D), row_ins=[dpre], weights=[(w_gate_f, "nt2")],
                           tile_ins=[dh3], tile_outs=[F32], epilogue=residual, side=side)
    ple_ns, ple_chips = to_chips(ns, halves, got)
    (dact, dh2b), got = _rows_mm("down_bwd", S, TL, F, _mm_tile(F, D), row_ins=[dh2], weights=[(w_down_f, "nt2")],
                                 tile_outs=[BF16], row_outs=[(D, BF16)], prologue=cast_prologue, epilogue=plain,
                                 side=ple_chips)
    landed(ple_ns, got)
    ns, halves, side = to_sibling(dict(w_down=_tn_mm("dw_down", act, dh2b, wt(F), wt(D), TK)))
    (du0, g_conv_gate, g_conv_up), got = _ffn_bwd(u0, conv_u0, dact, conv_ffn_f, S, TM, F, TC, side=side)
    down_ns, down_chips = to_chips(ns, halves, got)
    g_conv_ffn = jnp.concatenate([g_conv_gate, g_conv_up], axis=1)
    ns, halves, side = to_sibling(dict(
        w_up=_tn_mm("dw_up", hn2, du0, wt(D), TC, TK, cols_per_chip=2 * F // N_CHIPS, place=ffn_place)))

    def up_bwd_epilogue(acc, rows, row_r, vec_r, ro_r, ao_r):
        dh, dg = _rms_bwd(row_r[0][rows, :], vec_r[0][...], acc)
        dh1_ = row_r[1][rows, :] + dh
        ro_r[0][rows, :] = dh1_
        ro_r[1][rows, :] = dh1_.astype(BF16)
        ao_r[0][...] += dg

    (dh1, dh1b, g_norm_ffn), got = _kloop_mm(
        "up_bwd", S, TM, du0, w_up3, TC, row_ins=[h1, dh2], vec_ins=[norm_ffn_g],
        row_outs=[(D, F32), (D, BF16)], acc_outs=[(1, D)], epilogue=up_bwd_epilogue, place=ffn_place,
        side=_both(down_chips, side))
    landed(down_ns, got[:len(down_ns)])
    up_ns, up_chips = to_chips(ns, halves, got[len(down_ns):])
    ns, halves, side = to_sibling(dict(w_out=_tn_mm("dw_out", cat, dh1b, wt(2 * A), wt(D), TK)))
    (dcat,), got = _rows_mm("out_bwd", S, TL, 2 * A, _mm_tile(2 * A, D), row_ins=[dh1b],
                            weights=[(w_out_f, "nt2")], tile_outs=[BF16], epilogue=plain, side=side)
    out_ns, out_chips = to_chips(ns, halves, got)
    da1, ln_sums = _mixer_bwd_ln(dcat, a1, ln_a_g, ln_a_b, S, TE, A)
    (dz, g_conv_a, g_conv_b), got = _mixer_bwd_conv(z, dcat, da1, conv_a_f, conv_b_f, S, TE, A,
                                                    side=_both(up_chips, out_chips))
    landed(up_ns + out_ns, got)
    ns, halves, side = to_sibling(dict(
        w_in=_tn_mm("dw_in", hn1, dz, wt(D), wt(5 * A // N_CHIPS), TK, cols_per_chip=5 * A // N_CHIPS)))
    ns, side = to_chips(ns, halves, _comm_only("grads_exchange_pairs_in", side))

    def in_bwd_epilogue(acc, rows, row_r, vec_r, ro_r, ao_r):
        dh, dg = _rms_bwd(row_r[0][rows, :], vec_r[0][...], acc)
        ro_r[0][rows, :] = row_r[1][rows, :] + dh
        ao_r[0][...] += dg

    early = [n for n in names if n != "w_in"]
    early_halves = [_add_chips("chip_sum_" + n, chip_sums[n], from_chips[n], ci) for n in early]
    (dx, g_norm_mix), got = _kloop_mm(
        "in_bwd", S, TM, dz, w_in3, _mm_tile(5 * A // N_CHIPS, D), row_ins=[x2, dh1],
        vec_ins=[norm_mix_g], row_outs=[(D, F32)], acc_outs=[(1, D)], epilogue=in_bwd_epilogue,
        side=_both(side, _share_halves(early_halves)))
    landed(ns, got[:1])
    shared = dict(zip(early, got[1:]))
    (shared["w_in"],) = _comm_only("grads_share_w_in", _share_halves(
        [_add_chips("chip_sum_w_in", chip_sums["w_in"], from_chips["w_in"], ci)]))

    reduced = [shared[n] for n in names]
    moments = dict(w_in=(m_w_in, v_w_in), w_out=(m_w_out, v_w_out), w_up=(m_w_up, v_w_up),
                   w_down=(m_w_down, v_w_down), w_ple_gate=(m_w_ple_gate, v_w_ple_gate),
                   w_ple_proj=(m_w_ple_proj, v_w_ple_proj))
    grads, deltas, new_m, new_v = {}, {}, {}, {}
    for n, g in zip(names, reduced):
        d_, m_, v_, g = _adamw("adamw_" + n, big[n], g.reshape(big[n].shape), moments[n][0][0], moments[n][1][0],
                               copy_grad=True)
        grads[n], deltas[n], new_m[n], new_v[n] = g[None], d_[None], m_[None], v_[None]

    small = ["norm_mix_g", "conv_a_w", "conv_a_b", "ln_a_g", "ln_a_b", "conv_b_w", "norm_ffn_g",
             "conv_ffn_w", "b_ple_gate", "norm_final_g"]
    small_part = [g_norm_mix, g_conv_a, ln_sums[2:3], ln_sums[0:1], ln_sums[1:2], g_conv_b, g_norm_ffn,
                  g_conv_ffn, g_b_gate, g_norm_final]
    full_shapes = [a.shape for a in small_part]
    summed = _sum_devices("small_grads_sum", _allgather_small("allgather_small_grads", _pack(small_part)))
    small_g = dict(zip(small, _unpack(summed, full_shapes)))
    for n, width in (("conv_a_w", A), ("conv_b_w", A), ("conv_ffn_w", 2 * F)):
        small_g[n] = lax.dynamic_slice_in_dim(small_g[n], chip * (width // N_CHIPS), width // N_CHIPS, axis=1)
    small_w = dict(norm_mix_g=(norm_mix_g, m_norm_mix_g, v_norm_mix_g), conv_a_w=(conv_a_w, m_conv_a_w, v_conv_a_w),
                   conv_a_b=(conv_a_b, m_conv_a_b, v_conv_a_b), ln_a_g=(ln_a_g, m_ln_a_g, v_ln_a_g),
                   ln_a_b=(ln_a_b, m_ln_a_b, v_ln_a_b), conv_b_w=(conv_b_w, m_conv_b_w, v_conv_b_w),
                   norm_ffn_g=(norm_ffn_g, m_norm_ffn_g, v_norm_ffn_g),
                   conv_ffn_w=(conv_ffn_w, m_conv_ffn_w, v_conv_ffn_w),
                   b_ple_gate=(b_ple_gate, m_b_ple_gate, v_b_ple_gate),
                   norm_final_g=(norm_final_g, m_norm_final_g, v_norm_final_g))
    out_shapes = [small_w[n][0].shape for n in small]
    packed_g = _pack([small_g[n] for n in small])
    packed = [_pack([small_w[n][k] for n in small]) for k in range(3)]
    d_s, m_s, v_s = _adamw("adamw_small", packed[0], packed_g, packed[1], packed[2])
    for n, g, d_, m_, v_ in zip(small, _unpack(packed_g, out_shapes), _unpack(d_s, out_shapes),
                                _unpack(m_s, out_shapes), _unpack(v_s, out_shapes)):
        grads[n], deltas[n], new_m[n], new_v[n] = g, d_, m_, v_

    order = ["norm_mix_g", "w_in", "conv_a_w", "conv_a_b", "ln_a_g", "ln_a_b", "conv_b_w", "w_out", "norm_ffn_g",
             "w_up", "conv_ffn_w", "w_down", "w_ple_gate", "b_ple_gate", "w_ple_proj", "norm_final_g"]
    loss = lax.psum(loss_part[0, 0], ("x", "y", "c"))
    return (loss, dx.reshape(x.shape), *[grads[n] for n in order], *[deltas[n] for n in order],
            *[new_m[n] for n in order], *[new_v[n] for n in order])
```
